```python
import math
import jax, jax.numpy as jnp
from jax import lax
import numpy as np

D_MODEL = 1024
BATCH = 4
SEQ = 4096
DEPTH = 4

GRID_W = 64
CTX_LEN = 256
EPS = 1e-6
N_MOD = 6

HEAD_DIM = 64
N_Q_HEADS = 8
N_KV_HEADS = 2
Q_PER_KV = N_Q_HEADS // N_KV_HEADS
WINDOW = 128
ATTN_BLOCK = 128
BAND = 3 * ATTN_BLOCK
ROPE_BASE = 10000.0

FNET_GROUPS = 4
FNET_GROUP_DIM = 128
FNET_WIDTH = FNET_GROUPS * FNET_GROUP_DIM

HYENA_WIDTH = 512
HYENA_ORDER = 2
SHORT_CONV = 3
FILTER_EMB = 33
FILTER_BANDS = (FILTER_EMB - 1) // 2
FILTER_HIDDEN = 64
DECAY_TARGET = 1e-2
FAST_DECAY_PCT = 0.3
SLOW_DECAY_PCT = 1.5

ATTN_WIDTH = N_Q_HEADS * HEAD_DIM
KV_WIDTH = N_KV_HEADS * HEAD_DIM
N_BRANCHES = 3
Q_OFF = 0
K_OFF = Q_OFF + ATTN_WIDTH
V_OFF = K_OFF + KV_WIDTH
F_OFF = V_OFF + KV_WIDTH
H_OFF = F_OFF + FNET_WIDTH
G_OFF = H_OFF + (HYENA_ORDER + 1) * HYENA_WIDTH
IN_WIDTH = G_OFF + N_BRANCHES * D_MODEL

N_EXPERTS = 16
N_GROUPS = 4
EXPERTS_PER_GROUP = N_EXPERTS // N_GROUPS
TOP_K = 2
EXPERT_FF = 1024
EXPERT_BLOCK = 128

kernel_name = 'hybrid_gated_attn_fnet_hyena_moe_dit'


def rms_norm(x, g):
    xf = x.astype(jnp.float32)
    y = xf * lax.rsqrt(jnp.mean(xf * xf, axis=-1, keepdims=True) + EPS)
    return (y * g.astype(jnp.float32)).astype(x.dtype)


def modulate(h, shift, scale):
    return h * (1 + scale) + shift


def axial_rope(x, rows, cols):
    n_freq = HEAD_DIM // 4
    freqs = ROPE_BASE ** (-jnp.arange(n_freq, dtype=jnp.float32) / n_freq)

    def rotate(xa, pos):
        ang = pos.astype(jnp.float32)[:, None] * freqs
        cos = jnp.cos(ang)[None, :, None, :]
        sin = jnp.sin(ang)[None, :, None, :]
        x1, x2 = xa[..., :n_freq], xa[..., n_freq:]
        return jnp.concatenate([x1 * cos - x2 * sin, x1 * sin + x2 * cos], axis=-1)

    half = HEAD_DIM // 2
    out = jnp.concatenate([rotate(x[..., :half], rows), rotate(x[..., half:], cols)], axis=-1)
    return out.astype(x.dtype)


def banded_attention(q, k, v, kc, vc, sink):
    bsz, n = q.shape[:2]
    nb = n // ATTN_BLOCK
    scale = HEAD_DIM ** -0.5
    qb = q.reshape(bsz, nb, ATTN_BLOCK, N_KV_HEADS, Q_PER_KV, HEAD_DIM)
    pad = ((0, 0), (ATTN_BLOCK, ATTN_BLOCK), (0, 0), (0, 0))

    def band(t):
        tb = jnp.pad(t, pad).reshape(bsz, nb + 2, ATTN_BLOCK, N_KV_HEADS, HEAD_DIM)
        return jnp.concatenate([tb[:, :-2], tb[:, 1:-1], tb[:, 2:]], axis=2)

    kb, vb = band(k), band(v)
    s_band = jnp.einsum('bnqhgd,bnkhd->bnhgqk', qb, kb).astype(jnp.float32) * scale
    qpos = jnp.arange(n, dtype=jnp.int32).reshape(nb, ATTN_BLOCK)
    kpos = (jnp.arange(nb, dtype=jnp.int32)[:, None] - 1) * ATTN_BLOCK + jnp.arange(BAND, dtype=jnp.int32)[None, :]
    kp = kpos[:, None, :]
    valid = (kp >= 0) & (kp < n) & (jnp.abs(kp - qpos[:, :, None]) <= WINDOW)
    s_band = jnp.where(valid[None, :, None, None], s_band, jnp.finfo(jnp.float32).min)
    s_ctx = jnp.einsum('bnqhgd,bchd->bnhgqc', qb, kc).astype(jnp.float32) * scale
    s_sink = jnp.broadcast_to(sink.astype(jnp.float32).reshape(1, 1, N_KV_HEADS, Q_PER_KV, 1, 1),
                              s_band.shape[:-1] + (1,))
    p = jax.nn.softmax(jnp.concatenate([s_band, s_ctx, s_sink], axis=-1), axis=-1)
    n_ctx = kc.shape[1]
    p_band = p[..., :BAND].astype(v.dtype)
    p_ctx = p[..., BAND:BAND + n_ctx].astype(v.dtype)
    o = jnp.einsum('bnhgqk,bnkhd->bnqhgd', p_band, vb) + jnp.einsum('bnhgqc,bchd->bnqhgd', p_ctx, vc)
    return o.reshape(bsz, n, ATTN_WIDTH)


def context_attention(q, kc, vc, sink):
    bsz, n = q.shape[:2]
    scale = HEAD_DIM ** -0.5
    qg = q.reshape(bsz, n, N_KV_HEADS, Q_PER_KV, HEAD_DIM)
    s = jnp.einsum('bqhgd,bkhd->bhgqk', qg, kc).astype(jnp.float32) * scale
    s_sink = jnp.broadcast_to(sink.astype(jnp.float32).reshape(1, N_KV_HEADS, Q_PER_KV, 1, 1), s.shape[:-1] + (1,))
    p = jax.nn.softmax(jnp.concatenate([s, s_sink], axis=-1), axis=-1)[..., :-1]
    o = jnp.einsum('bhgqk,bkhd->bqhgd', p.astype(vc.dtype), vc)
    return o.reshape(bsz, n, ATTN_WIDTH)


def fourier_mix(u):
    bsz, n, _ = u.shape
    ug = u.astype(jnp.float32).reshape(bsz, n, FNET_GROUPS, FNET_GROUP_DIM)
    y = jnp.fft.fft2(ug, axes=(1, 3), norm='ortho').real
    return y.reshape(bsz, n, FNET_WIDTH).astype(u.dtype)


def short_conv(u, w, b):
    up = jnp.pad(u, ((0, 0), (1, 1), (0, 0)))
    return up[:, :-2] * w[0] + up[:, 1:-1] * w[1] + up[:, 2:] * w[2] + b


def hyena_filters(n, w1, b1, freq, w2, b2, w3):
    pos = jnp.arange(n, dtype=jnp.float32)
    t = pos / max(n - 1, 1)
    omega = 2.0 * math.pi * pos / n
    bands = jnp.linspace(1e-4, FILTER_BANDS - 1, FILTER_BANDS, dtype=jnp.float32)
    feats = jnp.concatenate([t[:, None], jnp.cos(omega[:, None] * bands), -jnp.sin(omega[:, None] * bands)], axis=-1)
    h = jnp.sin(freq * (feats @ w1 + b1))
    h = jnp.sin(freq * (h @ w2 + b2))
    h = (h @ w3).reshape(n, HYENA_ORDER, 2, HYENA_WIDTH)
    deltas = jnp.abs(jnp.linspace(math.log(DECAY_TARGET) / SLOW_DECAY_PCT, math.log(DECAY_TARGET) / FAST_DECAY_PCT,
                                  HYENA_WIDTH, dtype=jnp.float32))
    h = h * jnp.exp(-t[:, None] * deltas)[:, None, None, :]
    fwd, bwd = h[:, :, 0], h[:, :, 1]
    k = jnp.concatenate([fwd, jnp.zeros_like(fwd[:1]), bwd[:0:-1]], axis=0)
    return k / jnp.sum(jnp.abs(k), axis=0, keepdims=True)


def fft_long_conv(u, k):
    n = u.shape[1]
    uf = jnp.fft.rfft(u, n=2 * n, axis=1)
    kf = jnp.fft.rfft(k, n=2 * n, axis=0)
    return jnp.fft.irfft(uf * kf[None], n=2 * n, axis=1)[:, :n]


def hyena_mix(u, conv_w, conv_b, filt, hyena_bias):
    n = u.shape[1]
    z = short_conv(u, conv_w, conv_b).astype(jnp.float32)
    v, x1, x2 = jnp.split(z, HYENA_ORDER + 1, axis=-1)
    k = hyena_filters(n, *filt)
    bias = hyena_bias.astype(jnp.float32)
    y = v
    for o, gate in enumerate((x1, x2)):
        y = gate * (fft_long_conv(y, k[:, o]) + y * bias[o])
    return y.astype(u.dtype)


def mix_tokens(p, kc, vc, pos, sink, conv_w, conv_b, filt, hyena_bias, w_ba, w_bf, w_bh, w_out):
    bsz, n, _ = p.shape
    q = p[..., Q_OFF:K_OFF].reshape(bsz, n, N_Q_HEADS, HEAD_DIM)
    if pos is None:
        ya = context_attention(q, kc, vc, sink)
    else:
        rows, cols = pos
        k = p[..., K_OFF:V_OFF].reshape(bsz, n, N_KV_HEADS, HEAD_DIM)
        v = p[..., V_OFF:F_OFF].reshape(bsz, n, N_KV_HEADS, HEAD_DIM)
        ya = banded_attention(axial_rope(q, rows, cols), axial_rope(k, rows, cols), v, kc, vc, sink)
    yf = fourier_mix(p[..., F_OFF:H_OFF])
    yh = hyena_mix(p[..., H_OFF:G_OFF], conv_w, conv_b, filt, hyena_bias)
    ga, gf, gh = jnp.split(jax.nn.sigmoid(p[..., G_OFF:]), N_BRANCHES, axis=-1)
    merged = ga * (ya @ w_ba) + gf * (yf @ w_bf) + gh * (yh @ w_bh)
    return merged @ w_out


def moe_ffn(h, w_router, b_router, w_gate, w_up, w_down):
    shape = h.shape
    tok = h.reshape(-1, D_MODEL)
    n_tok = tok.shape[0]
    aff = jax.nn.sigmoid((tok @ w_router).astype(jnp.float32))
    biased = (aff + b_router.astype(jnp.float32)).reshape(n_tok, N_GROUPS, EXPERTS_PER_GROUP)
    group_score = lax.top_k(biased, TOP_K)[0].sum(axis=-1)
    g_sel = jnp.argmax(group_score, axis=-1).astype(jnp.int32)
    in_group = jnp.take_along_axis(biased, g_sel[:, None, None], axis=1)[:, 0]
    e_idx = g_sel[:, None] * EXPERTS_PER_GROUP + lax.top_k(in_group, TOP_K)[1]
    w_sel = jnp.take_along_axis(aff, e_idx, axis=1)
    w_sel = w_sel / jnp.sum(w_sel, axis=-1, keepdims=True)
    n_asg = n_tok * TOP_K
    flat_e = e_idx.reshape(-1)
    flat_tok = jnp.repeat(jnp.arange(n_tok, dtype=jnp.int32), TOP_K)
    order = jnp.argsort(flat_e)
    se = flat_e[order]
    counts = jnp.bincount(flat_e, length=N_EXPERTS)
    padded = (counts + EXPERT_BLOCK - 1) // EXPERT_BLOCK * EXPERT_BLOCK
    pad_end = jnp.cumsum(padded)
    pad_start = pad_end - padded
    start = jnp.cumsum(counts) - counts
    dest = pad_start[se] + jnp.arange(n_asg, dtype=jnp.int32) - start[se]
    n_rows = -(-n_asg // EXPERT_BLOCK) * EXPERT_BLOCK + N_EXPERTS * EXPERT_BLOCK
    n_blk = n_rows // EXPERT_BLOCK
    row_tok = jnp.full((n_rows,), n_tok, jnp.int32).at[dest].set(flat_tok[order])
    row_w = jnp.zeros((n_rows,), jnp.float32).at[dest].set(w_sel.reshape(-1)[order])
    blk_expert = jnp.minimum(
        jnp.searchsorted(pad_end, jnp.arange(n_blk, dtype=pad_end.dtype) * EXPERT_BLOCK, side='right'),
        N_EXPERTS - 1)
    tok_pad = jnp.concatenate([tok, jnp.zeros((1, D_MODEL), tok.dtype)], axis=0)
    xs = tok_pad[row_tok].reshape(n_blk, EXPERT_BLOCK, D_MODEL)

    def expert_block(args):
        xb, e = args
        hb = jax.nn.silu(xb @ w_gate[e]) * (xb @ w_up[e])
        return hb @ w_down[e]

    ys = lax.map(expert_block, (xs, blk_expert)).reshape(n_rows, D_MODEL)
    out = jnp.zeros((n_tok + 1, D_MODEL), ys.dtype).at[row_tok].add(ys * row_w[:, None].astype(ys.dtype))
    return out[:n_tok].reshape(shape)


def setup_inputs(seed: int = 0) -> dict:
    key = jax.random.key(seed)
    ks = jax.random.split(key, 32)
    D = D_MODEL
    L = DEPTH

    def nrm(k, shape, scale):
        return scale * jax.random.normal(k, shape, jnp.float32)

    return {
        'x': nrm(ks[0], (BATCH, SEQ, D), 1.0),
        'c': nrm(ks[1], (BATCH, D), 1.0),
        'ctx': nrm(ks[2], (BATCH, CTX_LEN, D), 1.0),
        'c_ctx': nrm(ks[3], (D,), 1.0),
        'w_mod': nrm(ks[4], (L, D, N_MOD * D), 0.5 * D ** -0.5),
        'b_mod': nrm(ks[5], (L, N_MOD * D), 0.02),
        'norm_mix': 1.0 + nrm(ks[6], (L, D), 0.02),
        'norm_ffn': 1.0 + nrm(ks[7], (L, D), 0.02),
        'w_in': nrm(ks[8], (L, D, IN_WIDTH), D ** -0.5),
        'attn_sink': nrm(ks[9], (L, N_Q_HEADS), 0.5),
        'conv_w': nrm(ks[10], (L, SHORT_CONV, (HYENA_ORDER + 1) * HYENA_WIDTH), SHORT_CONV ** -0.5),
        'conv_b': nrm(ks[11], (L, (HYENA_ORDER + 1) * HYENA_WIDTH), 0.02),
        'filt_w1': nrm(ks[12], (L, FILTER_EMB, FILTER_HIDDEN), FILTER_EMB ** -0.5),
        'filt_b1': nrm(ks[13], (L, FILTER_HIDDEN), 0.02),
        'filt_freq': 1.0 + nrm(ks[14], (L, FILTER_HIDDEN), 0.02),
        'filt_w2': nrm(ks[15], (L, FILTER_HIDDEN, FILTER_HIDDEN), FILTER_HIDDEN ** -0.5),
        'filt_b2': nrm(ks[16], (L, FILTER_HIDDEN), 0.02),
        'filt_w3': nrm(ks[17], (L, FILTER_HIDDEN, HYENA_ORDER * 2 * HYENA_WIDTH), FILTER_HIDDEN ** -0.5),
        'hyena_bias': nrm(ks[18], (L, HYENA_ORDER, HYENA_WIDTH), 0.5),
        'w_branch_attn': nrm(ks[19], (L, ATTN_WIDTH, D), ATTN_WIDTH ** -0.5),
        'w_branch_fnet': nrm(ks[20], (L, FNET_WIDTH, D), FNET_WIDTH ** -0.5),
        'w_branch_hyena': nrm(ks[21], (L, HYENA_WIDTH, D), HYENA_WIDTH ** -0.5),
        'w_out': nrm(ks[22], (L, D, D), D ** -0.5),
        'w_router': nrm(ks[23], (D, N_EXPERTS), D ** -0.5),
        'b_router': nrm(ks[24], (N_EXPERTS,), 0.01),
        'w_exp_gate': nrm(ks[25], (L, N_EXPERTS, D, EXPERT_FF), D ** -0.5),
        'w_exp_up': nrm(ks[26], (L, N_EXPERTS, D, EXPERT_FF), D ** -0.5),
        'w_exp_down': nrm(ks[27], (L, N_EXPERTS, EXPERT_FF, D), EXPERT_FF ** -0.5),
        'norm_final': 1.0 + nrm(ks[28], (D,), 0.02),
    }


def reference(x, c, ctx, c_ctx, w_mod, b_mod, norm_mix, norm_ffn, w_in, attn_sink, conv_w, conv_b,
              filt_w1, filt_b1, filt_freq, filt_w2, filt_b2, filt_w3, hyena_bias,
              w_branch_attn, w_branch_fnet, w_branch_hyena, w_out,
              w_router, b_router, w_exp_gate, w_exp_up, w_exp_down, norm_final):
    bsz, n_lat, _ = x.shape
    n_ctx = ctx.shape[1]
    ROWS = n_lat // GRID_W
    rows = jnp.repeat(jnp.arange(ROWS, dtype=jnp.int32), GRID_W)
    cols = jnp.tile(jnp.arange(GRID_W, dtype=jnp.int32), ROWS)
    xl, xc = x, ctx
    silu_c = jax.nn.silu(c)
    silu_cc = jax.nn.silu(c_ctx)
    for l in range(DEPTH):
        last = l == DEPTH - 1
        mod_l = jnp.split(silu_c @ w_mod[l] + b_mod[l], N_MOD, axis=-1)
        mod_c = jnp.split(silu_cc @ w_mod[l] + b_mod[l], N_MOD, axis=-1)
        sh1, sc1, g1, sh2, sc2, g2 = [m[:, None, :] for m in mod_l]
        csh1, csc1, cg1, csh2, csc2, cg2 = mod_c
        filt = (filt_w1[l], filt_b1[l], filt_freq[l], filt_w2[l], filt_b2[l], filt_w3[l])
        mixer_weights = (attn_sink[l], conv_w[l], conv_b[l], filt, hyena_bias[l],
                         w_branch_attn[l], w_branch_fnet[l], w_branch_hyena[l], w_out[l])
        hl = modulate(rms_norm(xl, norm_mix[l]), sh1, sc1)
        hc = modulate(rms_norm(xc, norm_mix[l]), csh1, csc1)
        if last:
            kv_c = hc @ w_in[l][:, K_OFF:F_OFF]
        else:
            pc = hc @ w_in[l]
            kv_c = pc[..., K_OFF:F_OFF]
        kc = kv_c[..., :KV_WIDTH].reshape(bsz, n_ctx, N_KV_HEADS, HEAD_DIM)
        vc = kv_c[..., KV_WIDTH:].reshape(bsz, n_ctx, N_KV_HEADS, HEAD_DIM)
        pl = hl @ w_in[l]
        xl = xl + g1 * mix_tokens(pl, kc, vc, (rows, cols), *mixer_weights)
        if not last:
            xc = xc + cg1 * mix_tokens(pc, kc, vc, None, *mixer_weights)
        experts = (w_exp_gate[l], w_exp_up[l], w_exp_down[l])
        h2l = modulate(rms_norm(xl, norm_ffn[l]), sh2, sc2)
        xl = xl + g2 * moe_ffn(h2l, w_router, b_router, *experts)
        if not last:
            h2c = modulate(rms_norm(xc, norm_ffn[l]), csh2, csc2)
            xc = xc + cg2 * moe_ffn(h2c, w_router, b_router, *experts)
    return rms_norm(xl, norm_final)
```

```python
import functools
import math

import jax
import jax.numpy as jnp
from jax import lax
from jax.experimental import pallas as pl
from jax.experimental.pallas import tpu as pltpu

F32 = jnp.float32
BF16 = jnp.bfloat16

D_MODEL = 1024
BATCH = 4
SEQ = 4096
DEPTH = 4
GRID_W = 64
CTX_LEN = 256
EPS = 1e-6
N_MOD = 6

HEAD_DIM = 64
N_Q_HEADS = 8
N_KV_HEADS = 2
Q_PER_KV = N_Q_HEADS // N_KV_HEADS
ATTN_BLOCK = 128
ROPE_BASE = 10000.0

FNET_GROUPS = 4
FNET_GROUP_DIM = 128
FNET_WIDTH = FNET_GROUPS * FNET_GROUP_DIM

HYENA_WIDTH = 512
HYENA_ORDER = 2
FILTER_EMB = 33
FILTER_BANDS = (FILTER_EMB - 1) // 2
DECAY_TARGET = 1e-2
FAST_DECAY_PCT = 0.3
SLOW_DECAY_PCT = 1.5

ATTN_WIDTH = N_Q_HEADS * HEAD_DIM
KV_WIDTH = N_KV_HEADS * HEAD_DIM
Q_OFF = 0
K_OFF = Q_OFF + ATTN_WIDTH
V_OFF = K_OFF + KV_WIDTH
F_OFF = V_OFF + KV_WIDTH
H_OFF = F_OFF + FNET_WIDTH
G_OFF = H_OFF + (HYENA_ORDER + 1) * HYENA_WIDTH
IN_WIDTH = G_OFF + 3 * D_MODEL

N_EXPERTS = 16
N_GROUPS = 4
EXPERTS_PER_GROUP = N_EXPERTS // N_GROUPS
TOP_K = 2
EXPERT_FF = 1024

T_LAT = BATCH * SEQ
T_CTX = BATCH * CTX_LEN
T_ALL = T_LAT + T_CTX

PG_OFF = 0
PH_OFF = 3 * D_MODEL
PF_OFF = PH_OFF + (HYENA_ORDER + 1) * HYENA_WIDTH
PQ_OFF = PF_OFF + FNET_WIDTH
PK_OFF = PQ_OFF + ATTN_WIDTH
PV_OFF = PK_OFF + KV_WIDTH

HY_N = 2 * SEQ
HY_N2 = 64
HY_N1 = HY_N // HY_N2
FN_N = 64

TM = 512
EXPERT_BM = 256
VMEM_LIMIT = 52 * 1024 * 1024


def _cparams(sem, vmem=VMEM_LIMIT):
    return pltpu.CompilerParams(dimension_semantics=sem, vmem_limit_bytes=vmem)


def _dot(a, b):
    return jnp.dot(a, b, preferred_element_type=F32)


def _cis(expo, n):
    ang = (2.0 * math.pi / n) * jnp.mod(expo, n).astype(F32)
    return jnp.cos(ang), jnp.sin(ang)


def _real_form(gr, gi):
    return jnp.concatenate([jnp.concatenate([gr, -gi], axis=-1), jnp.concatenate([gi, gr], axis=-1)], axis=-2)


def _dft_mats():
    ar = lambda n: jnp.arange(n, dtype=jnp.int32)
    m = {}
    c, s = _cis(ar(HY_N1)[:, None] * ar(HY_N1 // 2)[None, :], HY_N1)
    m['hy_lead_f'] = _real_form(c, -s).astype(BF16)
    c, s = _cis(ar(HY_N1 // 2)[:, None] * ar(HY_N1)[None, :], HY_N1)
    m['hy_lead_i'] = _real_form(c, s).astype(BF16)
    c, s = _cis(ar(HY_N1)[:, None] * ar(HY_N1)[None, :], HY_N1)
    m['hy_lead_k'] = jnp.concatenate([c, -s], axis=0).astype(BF16)
    a = ar(HY_N1)[:, None, None]
    k2 = ar(HY_N2)[None, :, None]
    n2 = ar(HY_N2)[None, None, :]
    c, s = _cis(n2 * (a + HY_N1 * k2), HY_N)
    m['hy_slab_f'] = _real_form(c, -s).astype(BF16)
    ct = jnp.swapaxes(c, 1, 2) * (1.0 / HY_N)
    st = jnp.swapaxes(s, 1, 2) * (1.0 / HY_N)
    m['hy_slab_i'] = _real_form(ct, st).astype(BF16)
    nc = 2 * CTX_LEN
    c, s = _cis(ar(nc)[:, None] * ar(CTX_LEN)[None, :], nc)
    m['hc_f'] = _real_form(c, -s).astype(BF16)
    c, s = _cis(ar(CTX_LEN)[:, None] * ar(nc)[None, :], nc)
    m['hc_i'] = _real_form(c * (1.0 / nc), s * (1.0 / nc)).astype(BF16)
    c, s = _cis(ar(nc)[:, None] * ar(nc)[None, :], nc)
    m['hc_k'] = jnp.concatenate([c, -s], axis=0).astype(BF16)
    c, s = _cis(ar(FNET_GROUP_DIM)[:, None] * ar(FNET_GROUP_DIM)[None, :], FNET_GROUP_DIM)
    m['fn_chan'] = jnp.concatenate([c, -s], axis=1).astype(BF16)
    c, s = _cis(ar(FN_N)[:, None] * ar(FN_N)[None, :], FN_N)
    m['fn_lead'] = _real_form(c, -s).astype(BF16)
    a = ar(FN_N)[:, None, None]
    k1 = ar(FN_N)[None, :, None]
    n1 = ar(FN_N)[None, None, :]
    scale = 1.0 / math.sqrt(SEQ * FNET_GROUP_DIM)
    c, s = _cis(n1 * (a + FN_N * k1), SEQ)
    m['fn_slab'] = jnp.concatenate([c * scale, s * scale], axis=-1).astype(BF16)
    scale = 1.0 / math.sqrt(CTX_LEN * FNET_GROUP_DIM)
    c, s = _cis(ar(CTX_LEN)[:, None] * ar(CTX_LEN)[None, :], CTX_LEN)
    m['fc'] = jnp.concatenate([c * scale, s * scale], axis=-1).astype(BF16)
    return m


def _mod_kernel(c_ref, w_ref, b_ref, o_ref):
    c = c_ref[...]
    s = c * jax.nn.sigmoid(c)
    o_ref[0] = _dot(s.astype(BF16), w_ref[0].astype(BF16)) + b_ref[0]


def _modulation(c8, w_mod, b_mod):
    tn = 1536
    n = N_MOD * D_MODEL
    return pl.pallas_call(
        _mod_kernel,
        grid=(DEPTH, n // tn),
        in_specs=[pl.BlockSpec((8, D_MODEL), lambda l, j: (0, 0)),
                  pl.BlockSpec((1, D_MODEL, tn), lambda l, j: (l, 0, j)),
                  pl.BlockSpec((1, 1, tn), lambda l, j: (l, 0, j))],
        out_specs=pl.BlockSpec((1, 8, tn), lambda l, j: (l, 0, j)),
        out_shape=jax.ShapeDtypeStruct((DEPTH, 8, n), F32),
        compiler_params=_cparams(("parallel", "parallel")),
        name="adaln_modulation",
    )(c8, w_mod, b_mod.reshape(DEPTH, 1, n))


def _mod_row(tm):
    tiles_per_batch = SEQ // tm
    return lambda i: jnp.minimum(i // tiles_per_batch, BATCH)


def _rms_mod(x, g, sh, sc):
    y = x * lax.rsqrt(jnp.mean(x * x, axis=-1, keepdims=True) + EPS)
    return (y * g) * (1.0 + sc) + sh


def _norm_proj_kernel(x_ref, g_ref, sh_ref, sc_ref, w_ref, o_ref, h_scr):
    @pl.when(pl.program_id(1) == 0)
    def _():
        h_scr[...] = _rms_mod(x_ref[...], g_ref[...], sh_ref[0, 0], sc_ref[0, 0]).astype(BF16)

    o_ref[...] = _dot(h_scr[...], w_ref[...])


def _norm_proj(x, gain, mod4, w, n_tok):
    tm = TM
    n_out = w.shape[1]
    tn = n_out // 2
    row = _mod_row(tm)
    return pl.pallas_call(
        _norm_proj_kernel,
        grid=(n_tok // tm, n_out // tn),
        in_specs=[pl.BlockSpec((tm, D_MODEL), lambda i, j: (i, 0)),
                  pl.BlockSpec((1, D_MODEL), lambda i, j: (0, 0)),
                  pl.BlockSpec((1, 1, 1, D_MODEL), lambda i, j: (row(i), 0, 0, 0)),
                  pl.BlockSpec((1, 1, 1, D_MODEL), lambda i, j: (row(i), 1, 0, 0)),
                  pl.BlockSpec((D_MODEL, tn), lambda i, j: (0, j))],
        out_specs=pl.BlockSpec((tm, tn), lambda i, j: (i, j)),
        out_shape=jax.ShapeDtypeStruct((T_ALL, n_out), F32),
        scratch_shapes=[pltpu.VMEM((tm, D_MODEL), BF16)],
        compiler_params=_cparams(("parallel", "arbitrary")),
        name="norm_in_proj",
    )(x, gain, mod4, mod4, w)


def _softmax_pv(qh, k_parts, v_parts, masks, sink):
    nt = (((1,), (1,)), ((), ()))
    scores = []
    for kp, mk in zip(k_parts, masks):
        s = lax.dot_general(qh, kp, nt, preferred_element_type=F32)
        if mk is not None:
            s = jnp.where(mk, s, -1e30)
        scores.append(s)
    m = sink
    for s in scores:
        m = jnp.maximum(m, jnp.max(s, axis=-1, keepdims=True))
    es = [jnp.exp(s - m) for s in scores]
    den = jnp.exp(sink - m)
    for e in es:
        den = den + jnp.sum(e, axis=-1, keepdims=True)
    inv = 1.0 / den
    o = None
    for e, vp in zip(es, v_parts):
        t = _dot((e * inv).astype(BF16), vp)
        o = t if o is None else o + t
    return o


def _attn_kernel(sink_ref, q_ref, km_ref, k0_ref, kp_ref, vm_ref, v0_ref, vp_ref, kc_ref, vc_ref,
                 cos_ref, sin_ref, o_ref, *, nb):
    n = pl.program_id(1)
    blk = ATTN_BLOCK
    lane = lax.broadcasted_iota(jnp.int32, (blk, 128), 1)
    first = (lane % 32) < 16

    def rope(x, blk_idx):
        r0 = pl.multiple_of(blk_idx * blk, blk)
        c = cos_ref[pl.ds(r0, blk), :]
        s = sin_ref[pl.ds(r0, blk), :]
        sw = jnp.where(first, pltpu.roll(x, 112, 1), pltpu.roll(x, 16, 1))
        return x * c + sw * s

    nm = jnp.maximum(n - 1, 0)
    npl = jnp.minimum(n + 1, nb - 1)
    kband = jnp.concatenate([rope(km_ref[...], nm), rope(k0_ref[...], n), rope(kp_ref[...], npl)],
                            axis=0).astype(BF16)
    vband = jnp.concatenate([vm_ref[...], v0_ref[...], vp_ref[...]], axis=0).astype(BF16)
    kc = kc_ref[...].astype(BF16)
    vc = vc_ref[...].astype(BF16)

    row = lax.broadcasted_iota(jnp.int32, (blk, 3 * blk), 0)
    col = lax.broadcasted_iota(jnp.int32, (blk, 3 * blk), 1)
    cj = col % blk
    has_prev = (n > 0).astype(jnp.int32)
    has_next = (n < nb - 1).astype(jnp.int32)
    left = jnp.where(cj >= row, has_prev, 0)
    right = jnp.where(cj <= row, has_next, 0)
    valid = jnp.where(col < blk, left, jnp.where(col < 2 * blk, 1, right)) > 0

    scale = HEAD_DIM ** -0.5
    outs = []
    for pair in range(N_Q_HEADS // 2):
        q2 = (rope(q_ref[:, pair * 128:(pair + 1) * 128], n) * scale).astype(BF16)
        for sub in range(2):
            head = 2 * pair + sub
            kvh = head // Q_PER_KV
            sl = slice(kvh * HEAD_DIM, (kvh + 1) * HEAD_DIM)
            qh = q2[:, sub * HEAD_DIM:(sub + 1) * HEAD_DIM]
            outs.append(_softmax_pv(qh, [kband[:, sl], kc[:, sl]], [vband[:, sl], vc[:, sl]],
                                    [valid, None], sink_ref[head]))
    o_ref[...] = jnp.concatenate(outs, axis=-1).astype(o_ref.dtype)


def _ctx_attn_kernel(sink_ref, q_ref, kc_ref, vc_ref, o_ref):
    kc = kc_ref[...].astype(BF16)
    vc = vc_ref[...].astype(BF16)
    scale = HEAD_DIM ** -0.5
    outs = []
    for pair in range(N_Q_HEADS // 2):
        q2 = (q_ref[:, pair * 128:(pair + 1) * 128] * scale).astype(BF16)
        for sub in range(2):
            head = 2 * pair + sub
            kvh = head // Q_PER_KV
            sl = slice(kvh * HEAD_DIM, (kvh + 1) * HEAD_DIM)
            qh = q2[:, sub * HEAD_DIM:(sub + 1) * HEAD_DIM]
            outs.append(_softmax_pv(qh, [kc[:, sl]], [vc[:, sl]], [None], sink_ref[head]))
    o_ref[...] = jnp.concatenate(outs, axis=-1).astype(o_ref.dtype)


def _attention(p, sink, cos_t, sin_t, with_ctx):
    blk = ATTN_BLOCK
    nb = SEQ // blk
    qc, kcol, vcol = PQ_OFF // ATTN_WIDTH, PK_OFF // KV_WIDTH, PV_OFF // KV_WIDTH
    ctx_blk = T_LAT // CTX_LEN
    smem = pl.BlockSpec(memory_space=pltpu.SMEM)

    def kv_spec(col, d):
        return pl.BlockSpec((blk, KV_WIDTH),
                            lambda b, n: (b * nb + jnp.clip(n + d, 0, nb - 1), col))

    ya = pl.pallas_call(
        functools.partial(_attn_kernel, nb=nb),
        grid=(BATCH, nb),
        in_specs=[smem,
                  pl.BlockSpec((blk, ATTN_WIDTH), lambda b, n: (b * nb + n, qc)),
                  kv_spec(kcol, -1), kv_spec(kcol, 0), kv_spec(kcol, 1),
                  kv_spec(vcol, -1), kv_spec(vcol, 0), kv_spec(vcol, 1),
                  pl.BlockSpec((CTX_LEN, KV_WIDTH), lambda b, n: (ctx_blk + b, kcol)),
                  pl.BlockSpec((CTX_LEN, KV_WIDTH), lambda b, n: (ctx_blk + b, vcol)),
                  pl.BlockSpec((SEQ, KV_WIDTH), lambda b, n: (0, 0)),
                  pl.BlockSpec((SEQ, KV_WIDTH), lambda b, n: (0, 0))],
        out_specs=pl.BlockSpec((blk, ATTN_WIDTH), lambda b, n: (b * nb + n, 0)),
        out_shape=jax.ShapeDtypeStruct((T_ALL, ATTN_WIDTH), BF16),
        compiler_params=_cparams(("parallel", "parallel")),
        name="banded_attention",
    )(sink, p, p, p, p, p, p, p, p, p, cos_t, sin_t)
    if not with_ctx:
        return ya
    cb = CTX_LEN // blk
    lat_blk = T_LAT // blk

    def alias_kernel(sink_ref, q_ref, kc_ref, vc_ref, ya_in_ref, o_ref):
        del ya_in_ref
        _ctx_attn_kernel(sink_ref, q_ref, kc_ref, vc_ref, o_ref)

    return pl.pallas_call(
        alias_kernel,
        grid=(BATCH, cb),
        in_specs=[smem,
                  pl.BlockSpec((blk, ATTN_WIDTH), lambda b, n: (lat_blk + b * cb + n, qc)),
                  pl.BlockSpec((CTX_LEN, KV_WIDTH), lambda b, n: (ctx_blk + b, kcol)),
                  pl.BlockSpec((CTX_LEN, KV_WIDTH), lambda b, n: (ctx_blk + b, vcol)),
                  pl.BlockSpec(memory_space=pl.ANY)],
        out_specs=pl.BlockSpec((blk, ATTN_WIDTH), lambda b, n: (lat_blk + b * cb + n, 0)),
        out_shape=jax.ShapeDtypeStruct((T_ALL, ATTN_WIDTH), BF16),
        input_output_aliases={4: 0},
        compiler_params=_cparams(("parallel", "parallel")),
        name="context_attention",
    )(sink, p, p, p, ya)


def _rope_tables():
    n_freq = HEAD_DIM // 4
    freqs = ROPE_BASE ** (-jnp.arange(n_freq, dtype=F32) / n_freq)
    t = jnp.arange(SEQ, dtype=jnp.int32)
    rows = (t // GRID_W).astype(F32)[:, None] * freqs
    cols = (t % GRID_W).astype(F32)[:, None] * freqs
    cos_h = jnp.concatenate([jnp.cos(rows), jnp.cos(rows), jnp.cos(cols), jnp.cos(cols)], axis=-1)
    sin_h = jnp.concatenate([-jnp.sin(rows), jnp.sin(rows), -jnp.sin(cols), jnp.sin(cols)], axis=-1)
    return jnp.tile(cos_h, (1, 2)), jnp.tile(sin_h, (1, 2))


def _lead_kernel(*refs, n_in, cmul, epi):
    m_ref = refs[0]
    x_refs = refs[1:1 + n_in]
    pos = 1 + n_in
    xs = []
    for r in x_refs:
        v = r[...]
        xs.append(v.reshape(-1, v.shape[-1]))
    x = xs[0] if n_in == 1 else jnp.concatenate(xs, axis=0)
    if cmul:
        k = refs[pos][...]
        pos += 1
        k = k.reshape(-1, k.shape[-1])
        half = x.shape[0] // 2
        xr, xi, kr, ki = x[:half], x[half:], k[:half], k[half:]
        x = jnp.concatenate([xr * kr - xi * ki, xr * ki + xi * kr], axis=0)
    res = _dot(m_ref[...], x.astype(BF16))
    if epi:
        g_ref, y_ref, b_ref = refs[pos:pos + 3]
        pos += 3
        g = g_ref[...]
        y = y_ref[...]
        res = g.reshape(-1, g.shape[-1]) * (res + y.reshape(-1, y.shape[-1]) * b_ref[...])
    o_ref = refs[pos]
    o_ref[...] = res.reshape(o_ref.shape).astype(o_ref.dtype)


def _lead(mat, xs, x_specs, grid, out_shape, out_spec, *, kspec=None, epi=None, alias_to=None, name):
    ins = [mat] + list(xs)
    specs = [pl.BlockSpec(mat.shape, lambda *a: (0, 0))] + list(x_specs)
    if kspec is not None:
        ins.append(kspec[0])
        specs.append(kspec[1])
    if epi is not None:
        for arr, sp in epi:
            ins.append(arr)
            specs.append(sp)
    kern = functools.partial(_lead_kernel, n_in=len(xs), cmul=kspec is not None, epi=epi is not None)
    aliases = {}
    if alias_to is not None:
        aliases = {len(ins): 0}
        ins.append(alias_to)
        specs.append(pl.BlockSpec(memory_space=pl.ANY))
        inner = kern

        def kern(*refs):
            inner(*refs[:-2], refs[-1])

    return pl.pallas_call(
        kern, grid=grid, in_specs=specs, out_specs=out_spec, out_shape=out_shape,
        input_output_aliases=aliases,
        compiler_params=_cparams(("parallel",) * len(grid)), name=name,
    )(*ins)


def _slab_kernel(*refs, sb, mul, inv, real_out, transpose_out):
    mf_ref, x_ref = refs[0], refs[1]
    pos = 2
    if mul:
        k_ref = refs[pos]
        pos += 1
    if inv:
        mi_ref = refs[pos]
        pos += 1
    o_ref = refs[pos]
    r = x_ref.shape[3]
    for s in range(sb):
        x = jnp.concatenate([x_ref[0, 0, s], x_ref[0, 1, s]], axis=0).astype(BF16)
        y = _dot(mf_ref[s], x)
        if mul:
            yr, yi = y[:r], y[r:]
            kr, ki = k_ref[0, 0, s], k_ref[0, 1, s]
            y = jnp.concatenate([yr * kr - yi * ki, yr * ki + yi * kr], axis=0)
        if inv:
            y = _dot(mi_ref[s], y.astype(BF16))
        if real_out:
            if transpose_out:
                o_ref[0, :, s, :] = y
            else:
                o_ref[0, s] = y
        else:
            o_ref[0, 0, s] = y[:r]
            o_ref[0, 1, s] = y[r:]


def _slab(mf, x, *, kspec=None, korder=0, mi=None, real_out=False, name):
    g, _, ns, r, c = x.shape
    sb = 8
    ins = [mf, x]
    specs = [pl.BlockSpec((sb,) + mf.shape[1:], lambda b, j: (j, 0, 0)),
             pl.BlockSpec((1, 2, sb, r, c), lambda b, j: (b, 0, j, 0, 0))]
    if kspec is not None:
        ins.append(kspec)
        specs.append(pl.BlockSpec((1, 2, sb, r, c), lambda b, j: (korder, 0, j, 0, 0)))
    if mi is not None:
        ins.append(mi)
        specs.append(pl.BlockSpec((sb,) + mi.shape[1:], lambda b, j: (j, 0, 0)))
    if real_out:
        out_shape = jax.ShapeDtypeStruct((g, ns, mf.shape[1], c), F32)
        out_spec = pl.BlockSpec((1, sb, mf.shape[1], c), lambda b, j: (b, j, 0, 0))
    else:
        out_shape = jax.ShapeDtypeStruct((g, 2, ns, r, c), F32)
        out_spec = pl.BlockSpec((1, 2, sb, r, c), lambda b, j: (b, 0, j, 0, 0))
    kern = functools.partial(_slab_kernel, sb=sb, mul=kspec is not None, inv=mi is not None,
                             real_out=real_out, transpose_out=False)
    return pl.pallas_call(
        kern, grid=(g, ns // sb), in_specs=specs, out_specs=out_spec, out_shape=out_shape,
        compiler_params=_cparams(("parallel", "parallel")), name=name,
    )(*ins)


def _chan_dft_kernel(u_ref, m_ref, zr_ref, zi_ref):
    gd = FNET_GROUP_DIM
    m = m_ref[...]
    for g in range(FNET_GROUPS):
        z = _dot(u_ref[:, g * gd:(g + 1) * gd].astype(BF16), m)
        zr_ref[:, g * gd:(g + 1) * gd] = z[:, :gd]
        zi_ref[:, g * gd:(g + 1) * gd] = z[:, gd:]


def _fourier_mix(p, mats, n_tok, with_ctx):
    tm = TM
    fcol = PF_OFF // FNET_WIDTH
    zr, zi = pl.pallas_call(
        _chan_dft_kernel,
        grid=(n_tok // tm,),
        in_specs=[pl.BlockSpec((tm, FNET_WIDTH), lambda i: (i, fcol)),
                  pl.BlockSpec(mats['fn_chan'].shape, lambda i: (0, 0))],
        out_specs=[pl.BlockSpec((tm, FNET_WIDTH), lambda i: (i, 0))] * 2,
        out_shape=[jax.ShapeDtypeStruct((T_ALL, FNET_WIDTH), F32)] * 2,
        compiler_params=_cparams(("parallel",)),
        name="fnet_channel_dft",
    )(p, mats['fn_chan'])
    lanes = FN_N * FNET_WIDTH
    tl = 4096
    zr2 = zr.reshape(T_ALL // FN_N, lanes)
    zi2 = zi.reshape(T_ALL // FN_N, lanes)
    xspec = pl.BlockSpec((FN_N, tl), lambda b, j: (b, j))
    a = _lead(mats['fn_lead'], [zr2, zi2], [xspec, xspec], (BATCH, lanes // tl),
              jax.ShapeDtypeStruct((BATCH, 2 * FN_N, lanes), F32),
              pl.BlockSpec((1, 2 * FN_N, tl), lambda b, j: (b, 0, j)), name="fnet_lead")
    a = a.reshape(BATCH, 2, FN_N, FN_N, FNET_WIDTH)
    y2 = _slab(mats['fn_slab'], a, real_out=True, name="fnet_slab")
    yf_lat = jnp.swapaxes(y2, 1, 2).reshape(T_LAT, FNET_WIDTH)
    if not with_ctx:
        return yf_lat
    ctx_blk = T_LAT // CTX_LEN
    cspec = pl.BlockSpec((CTX_LEN, FNET_WIDTH), lambda b: (ctx_blk + b, 0))
    yf_ctx = _lead(mats['fc'], [zr, zi], [cspec, cspec], (BATCH,),
                   jax.ShapeDtypeStruct((T_CTX, FNET_WIDTH), F32),
                   pl.BlockSpec((CTX_LEN, FNET_WIDTH), lambda b: (b, 0)), name="fnet_ctx")
    return jnp.concatenate([yf_lat, yf_ctx], axis=0)


def _short_conv_kernel(u_ref, w_ref, b_ref, o_ref, *, rows, chunk):
    w0 = w_ref[0:1, :]
    w1 = w_ref[1:2, :]
    w2 = w_ref[2:3, :]
    bias = b_ref[...]
    width = u_ref.shape[-1]
    ridx = lax.broadcasted_iota(jnp.int32, (chunk, width), 0)
    n_chunks = rows // chunk
    for ci in range(n_chunks):
        r0 = ci * chunk
        cur = u_ref[r0:r0 + chunk, :]
        if ci > 0:
            prev_row = u_ref[r0 - 8:r0, :][7:8, :]
        else:
            prev_row = jnp.zeros((1, width), F32)
        if ci < n_chunks - 1:
            next_row = u_ref[r0 + chunk:r0 + chunk + 8, :][0:1, :]
        else:
            next_row = jnp.zeros((1, width), F32)
        up = jnp.where(ridx == 0, prev_row, pltpu.roll(cur, 1, 0))
        dn = jnp.where(ridx == chunk - 1, next_row, pltpu.roll(cur, chunk - 1, 0))
        o_ref[0, r0:r0 + chunk, :] = up * w0 + cur * w1 + dn * w2 + bias


def _short_conv(p, conv_w, conv_b, with_ctx):
    cw = 256
    hw = (HYENA_ORDER + 1) * HYENA_WIDTH
    ncol = hw // cw
    per = HYENA_WIDTH // cw
    col0 = PH_OFF // cw
    out_shape = jax.ShapeDtypeStruct((HYENA_ORDER + 1, T_ALL, HYENA_WIDTH), F32)
    b2 = conv_b.reshape(1, hw)

    def call(rows, blk0, alias):
        kern = functools.partial(_short_conv_kernel, rows=rows, chunk=min(rows, 256))
        ins = [p, conv_w, b2]
        specs = [pl.BlockSpec((rows, cw), lambda b, j: (blk0 + b, col0 + j)),
                 pl.BlockSpec((3, cw), lambda b, j: (0, j)),
                 pl.BlockSpec((1, cw), lambda b, j: (0, j))]
        aliases = {}
        if alias is not None:
            ins.append(alias)
            specs.append(pl.BlockSpec(memory_space=pl.ANY))
            aliases = {3: 0}
            inner = kern

            def kern(u_ref, w_ref, b_ref, a_ref, o_ref):
                del a_ref
                inner(u_ref, w_ref, b_ref, o_ref)

        return pl.pallas_call(
            kern, grid=(BATCH, ncol), in_specs=specs,
            out_specs=pl.BlockSpec((1, rows, cw), lambda b, j: (j // per, blk0 + b, j % per)),
            out_shape=out_shape, input_output_aliases=aliases,
            compiler_params=_cparams(("parallel", "parallel")), name="hyena_short_conv",
        )(*ins)

    z3 = call(SEQ, 0, None)
    if with_ctx:
        z3 = call(CTX_LEN, T_LAT // CTX_LEN, z3)
    return z3


def _hyena_filters(n, w1, b1, freq, w2, b2, w3):
    pos = jnp.arange(n, dtype=F32)
    t = pos / max(n - 1, 1)
    omega = 2.0 * math.pi * pos / n
    bands = jnp.linspace(1e-4, FILTER_BANDS - 1, FILTER_BANDS, dtype=F32)
    feats = jnp.concatenate([t[:, None], jnp.cos(omega[:, None] * bands), -jnp.sin(omega[:, None] * bands)], axis=-1)
    h = jnp.sin(freq * (feats @ w1 + b1))
    h = jnp.sin(freq * (h @ w2 + b2))
    h = (h @ w3).reshape(n, HYENA_ORDER, 2, HYENA_WIDTH)
    deltas = jnp.abs(jnp.linspace(math.log(DECAY_TARGET) / SLOW_DECAY_PCT, math.log(DECAY_TARGET) / FAST_DECAY_PCT,
                                  HYENA_WIDTH, dtype=F32))
    h = h * jnp.exp(-t[:, None] * deltas)[:, None, None, :]
    fwd, bwd = h[:, :, 0], h[:, :, 1]
    k = jnp.concatenate([fwd, jnp.zeros_like(fwd[:1]), bwd[:0:-1]], axis=0)
    return k / jnp.sum(jnp.abs(k), axis=0, keepdims=True)


def _hyena_mix(z3, filt, hyena_bias, mats, with_ctx):
    c = HYENA_WIDTH
    lanes = HY_N2 * c
    tl = 2048
    half = HY_N1 // 2
    pairs = BATCH // 2
    k_lat = _hyena_filters(SEQ, *filt)
    k2d = jnp.transpose(k_lat, (1, 0, 2)).reshape(HYENA_ORDER, HY_N1, lanes)
    ka = _lead(mats['hy_lead_k'], [k2d], [pl.BlockSpec((1, HY_N1, tl), lambda o, j: (o, 0, j))],
               (HYENA_ORDER, lanes // tl), jax.ShapeDtypeStruct((HYENA_ORDER, 2 * HY_N1, lanes), F32),
               pl.BlockSpec((1, 2 * HY_N1, tl), lambda o, j: (o, 0, j)), name="hyena_filter_lead")
    kspec = _slab(mats['hy_slab_f'], ka.reshape(HYENA_ORDER, 2, HY_N1, HY_N2, c), name="hyena_filter_slab")
    if with_ctx:
        nc = 2 * CTX_LEN
        k_ctx = jnp.transpose(_hyena_filters(CTX_LEN, *filt), (1, 0, 2))
        kspec_c = _lead(mats['hc_k'], [k_ctx], [pl.BlockSpec((1, nc, c), lambda o: (o, 0, 0))],
                        (HYENA_ORDER,), jax.ShapeDtypeStruct((HYENA_ORDER, 2 * nc, c), F32),
                        pl.BlockSpec((1, 2 * nc, c), lambda o: (o, 0, 0)), name="hyena_ctx_filter")
    bias_l = jnp.tile(hyena_bias, (1, HY_N2)).reshape(HYENA_ORDER, 1, lanes)
    bias_c = hyena_bias.reshape(HYENA_ORDER, 1, c)

    z2d = z3.reshape(HYENA_ORDER + 1, T_ALL // HY_N2, lanes)
    y = None
    for o in range(HYENA_ORDER):
        if y is None:
            xin, xspec = z2d, pl.BlockSpec((1, 2 * half, tl), lambda b, j: (0, b, j))
        else:
            xin, xspec = y.reshape(T_ALL // HY_N2, lanes), pl.BlockSpec((2 * half, tl), lambda b, j: (b, j))
        a = _lead(mats['hy_lead_f'], [xin], [xspec], (pairs, lanes // tl),
                  jax.ShapeDtypeStruct((pairs, 2 * HY_N1, lanes), F32),
                  pl.BlockSpec((1, 2 * HY_N1, tl), lambda b, j: (b, 0, j)), name="hyena_lead_fwd")
        bq = _slab(mats['hy_slab_f'], a.reshape(pairs, 2, HY_N1, HY_N2, c), kspec=kspec, korder=o,
                   mi=mats['hy_slab_i'], name="hyena_slab")
        bq = bq.reshape(pairs, 2 * HY_N1, lanes)
        gate_spec = pl.BlockSpec((1, 2 * half, tl), lambda b, j, o=o: (o + 1, b, j))
        ynew = _lead(mats['hy_lead_i'], [bq], [pl.BlockSpec((1, 2 * HY_N1, tl), lambda b, j: (b, 0, j))],
                     (pairs, lanes // tl), jax.ShapeDtypeStruct((T_ALL // HY_N2, lanes), F32),
                     pl.BlockSpec((2 * half, tl), lambda b, j: (b, j)),
                     epi=[(z2d, gate_spec), (xin, xspec),
                          (bias_l, pl.BlockSpec((1, 1, tl), lambda b, j, o=o: (o, 0, j)))],
                     name="hyena_lead_inv")
        ynew = ynew.reshape(T_ALL, c)
        if with_ctx:
            nc = 2 * CTX_LEN
            blk0 = T_LAT // nc
            if y is None:
                cin, cspec_in = z3, pl.BlockSpec((1, nc, c), lambda b: (0, blk0 + b, 0))
            else:
                cin, cspec_in = y, pl.BlockSpec((nc, c), lambda b: (blk0 + b, 0))
            xc = _lead(mats['hc_f'], [cin], [cspec_in], (pairs,),
                       jax.ShapeDtypeStruct((pairs, 2 * nc, c), F32),
                       pl.BlockSpec((1, 2 * nc, c), lambda b: (b, 0, 0)), name="hyena_ctx_fwd")
            ynew = _lead(mats['hc_i'], [xc], [pl.BlockSpec((1, 2 * nc, c), lambda b: (b, 0, 0))],
                         (pairs,), jax.ShapeDtypeStruct((T_ALL, c), F32),
                         pl.BlockSpec((nc, c), lambda b: (blk0 + b, 0)),
                         kspec=(kspec_c, pl.BlockSpec((1, 2 * nc, c), lambda b, o=o: (o, 0, 0))),
                         epi=[(z3, pl.BlockSpec((1, nc, c), lambda b, o=o: (o + 1, blk0 + b, 0))),
                              (cin, cspec_in),
                              (bias_c, pl.BlockSpec((1, 1, c), lambda b, o=o: (o, 0, 0)))],
                         alias_to=ynew, name="hyena_ctx_inv")
        y = ynew
    return y


def _merge_kernel(x_ref, ya_ref, yf_ref, yh_ref, gt_ref, wa_ref, wf_ref, wh_ref, wo_ref, g1_ref,
                  gn_ref, sh_ref, sc_ref, wr_ref, xo_ref, h2_ref, lg_ref):
    d = D_MODEL
    merged = jax.nn.sigmoid(gt_ref[:, 0:d]) * _dot(ya_ref[...], wa_ref[...])
    merged += jax.nn.sigmoid(gt_ref[:, d:2 * d]) * _dot(yf_ref[...].astype(BF16), wf_ref[...])
    merged += jax.nn.sigmoid(gt_ref[:, 2 * d:3 * d]) * _dot(yh_ref[...].astype(BF16), wh_ref[...])
    xn = x_ref[...] + g1_ref[0, 0] * _dot(merged.astype(BF16), wo_ref[...])
    xo_ref[...] = xn
    h2 = _rms_mod(xn, gn_ref[...], sh_ref[0, 0], sc_ref[0, 0]).astype(BF16)
    h2_ref[...] = h2
    lg_ref[...] = _dot(h2, wr_ref[...])


def _merge(x, ya, yf, yh, p, wa, wf, wh, wo, mod4, gain2, wr, n_tok):
    tm = 256
    row = _mod_row(tm)
    full = lambda a: pl.BlockSpec(a.shape, lambda i: (0,) * a.ndim)
    modspec = lambda k: pl.BlockSpec((1, 1, 1, D_MODEL), lambda i: (row(i), k, 0, 0))
    tok = lambda w: pl.BlockSpec((tm, w), lambda i: (i, 0))
    return pl.pallas_call(
        _merge_kernel,
        grid=(n_tok // tm,),
        in_specs=[tok(D_MODEL), tok(ATTN_WIDTH), tok(FNET_WIDTH), tok(HYENA_WIDTH),
                  pl.BlockSpec((tm, 3 * D_MODEL), lambda i: (i, PG_OFF // (3 * D_MODEL))),
                  full(wa), full(wf), full(wh), full(wo), modspec(2), full(gain2), modspec(3), modspec(4),
                  full(wr)],
        out_specs=[tok(D_MODEL), tok(D_MODEL), tok(128)],
        out_shape=[jax.ShapeDtypeStruct((T_ALL, D_MODEL), F32),
                   jax.ShapeDtypeStruct((T_ALL, D_MODEL), BF16),
                   jax.ShapeDtypeStruct((T_ALL, 128), F32)],
        compiler_params=_cparams(("parallel",)),
        name="merge_out_norm",
    )(x, ya, yf, yh, p, wa, wf, wh, wo, mod4, gain2, mod4, mod4, wr)


def _expert_kernel(be_ref, na_ref, x_ref, wg_ref, wu_ref, wd_ref, o_ref):
    del be_ref
    i = pl.program_id(0)

    @pl.when(i < na_ref[0])
    def _():
        x = x_ref[...]
        g = _dot(x, wg_ref[0])
        u = _dot(x, wu_ref[0])
        h = (g * jax.nn.sigmoid(g)) * u
        o_ref[...] = _dot(h.astype(BF16), wd_ref[0])

    @pl.when(i >= na_ref[0])
    def _():
        o_ref[...] = jnp.zeros_like(o_ref)


def _experts(xs, blk_expert, n_active, wg, wu, wd):
    bm = EXPERT_BM
    n_blk = xs.shape[0] // bm
    wspec = lambda k, n: pl.BlockSpec((1, k, n), lambda i, be, na: (be[i], 0, 0))
    return pl.pallas_call(
        _expert_kernel,
        grid_spec=pltpu.PrefetchScalarGridSpec(
            num_scalar_prefetch=2,
            grid=(n_blk,),
            in_specs=[pl.BlockSpec((bm, D_MODEL), lambda i, be, na: (i, 0)),
                      wspec(D_MODEL, EXPERT_FF), wspec(D_MODEL, EXPERT_FF), wspec(EXPERT_FF, D_MODEL)],
            out_specs=pl.BlockSpec((bm, D_MODEL), lambda i, be, na: (i, 0))),
        out_shape=jax.ShapeDtypeStruct((xs.shape[0], D_MODEL), F32),
        compiler_params=_cparams(("arbitrary",)),
        name="moe_experts",
    )(blk_expert, n_active, xs, wg, wu, wd)


def _route(logits, b_router, n_tok):
    bm = EXPERT_BM
    aff = jax.nn.sigmoid(logits[:n_tok, :N_EXPERTS])
    biased = (aff + b_router.astype(F32)).reshape(n_tok, N_GROUPS, EXPERTS_PER_GROUP)
    group_score = lax.top_k(biased, TOP_K)[0].sum(axis=-1)
    g_sel = jnp.argmax(group_score, axis=-1).astype(jnp.int32)
    in_group = jnp.take_along_axis(biased, g_sel[:, None, None], axis=1)[:, 0]
    e_idx = g_sel[:, None] * EXPERTS_PER_GROUP + lax.top_k(in_group, TOP_K)[1]
    w_sel = jnp.take_along_axis(aff, e_idx, axis=1)
    w_sel = w_sel / jnp.sum(w_sel, axis=-1, keepdims=True)
    n_asg = n_tok * TOP_K
    flat_e = e_idx.reshape(-1)
    onehot = (flat_e[:, None] == jnp.arange(N_EXPERTS, dtype=jnp.int32)[None, :]).astype(jnp.int32)
    csum = jnp.cumsum(onehot, axis=0)
    rank = jnp.take_along_axis(csum, flat_e[:, None], axis=1)[:, 0] - 1
    counts = csum[-1]
    padded = (counts + bm - 1) // bm * bm
    pad_end = jnp.cumsum(padded)
    pad_start = pad_end - padded
    dest = pad_start[flat_e] + rank
    n_rows = -(-n_asg // bm) * bm + N_EXPERTS * bm
    n_blk = n_rows // bm
    flat_tok = jnp.repeat(jnp.arange(n_tok, dtype=jnp.int32), TOP_K)
    row_tok = jnp.full((n_rows,), T_ALL, jnp.int32).at[dest].set(flat_tok)
    blk_expert = jnp.minimum(
        jnp.searchsorted(pad_end, jnp.arange(n_blk, dtype=pad_end.dtype) * bm, side='right'),
        N_EXPERTS - 1).astype(jnp.int32)
    n_active = (pad_end[-1] // bm).astype(jnp.int32).reshape(1)
    return row_tok, dest.reshape(n_tok, TOP_K), w_sel, blk_expert, n_active


def _combine_kernel(x_ref, y0_ref, y1_ref, w_ref, g2_ref, gf_ref, o_ref, *, final):
    w = w_ref[...]
    moe = y0_ref[...] * w[:, 0:1] + y1_ref[...] * w[:, 1:2]
    xn = x_ref[...] + g2_ref[0, 0] * moe
    if final:
        y = xn * lax.rsqrt(jnp.mean(xn * xn, axis=-1, keepdims=True) + EPS)
        xn = y * gf_ref[...]
    o_ref[...] = xn


def _combine(x, y0, y1, w_sel, mod4, gain_final, n_tok, final):
    tm = TM
    row = _mod_row(tm)
    tok = lambda w: pl.BlockSpec((tm, w), lambda i: (i, 0))
    return pl.pallas_call(
        functools.partial(_combine_kernel, final=final),
        grid=(n_tok // tm,),
        in_specs=[tok(D_MODEL), tok(D_MODEL), tok(D_MODEL), tok(TOP_K),
                  pl.BlockSpec((1, 1, 1, D_MODEL), lambda i: (row(i), 5, 0, 0)),
                  pl.BlockSpec((1, D_MODEL), lambda i: (0, 0))],
        out_specs=tok(D_MODEL),
        out_shape=jax.ShapeDtypeStruct((n_tok if final else T_ALL, D_MODEL), F32),
        compiler_params=_cparams(("parallel",)),
        name="moe_combine",
    )(x, y0, y1, w_sel, mod4, gain_final)


def kernel(x, c, ctx, c_ctx, w_mod, b_mod, norm_mix, norm_ffn, w_in, attn_sink, conv_w, conv_b, filt_w1, filt_b1, filt_freq, filt_w2, filt_b2, filt_w3, hyena_bias, w_branch_attn, w_branch_fnet, w_branch_hyena, w_out, w_router, b_router, w_exp_gate, w_exp_up, w_exp_down, norm_final):
    mats = _dft_mats()
    cos_t, sin_t = _rope_tables()
    c8 = jnp.concatenate([c, c_ctx[None, :], jnp.zeros((8 - BATCH - 1, D_MODEL), F32)], axis=0)
    mod_all = _modulation(c8, w_mod, b_mod)
    xa = jnp.concatenate([x.reshape(T_LAT, D_MODEL), ctx.reshape(T_CTX, D_MODEL)], axis=0)
    wr = jnp.pad(w_router, ((0, 0), (0, 128 - N_EXPERTS))).astype(BF16)
    gain_final = norm_final.reshape(1, D_MODEL)
    out = None
    for l in range(DEPTH):
        last = l == DEPTH - 1
        with_ctx = not last
        n_tok = T_LAT if last else T_ALL
        mod4 = mod_all[l].reshape(8, N_MOD, 1, D_MODEL)
        wl = w_in[l]
        w_perm = jnp.concatenate([wl[:, G_OFF:], wl[:, H_OFF:G_OFF], wl[:, F_OFF:H_OFF], wl[:, Q_OFF:F_OFF]],
                                 axis=1).astype(BF16)
        p = _norm_proj(xa, norm_mix[l].reshape(1, D_MODEL), mod4, w_perm, T_ALL)
        ya = _attention(p, attn_sink[l], cos_t, sin_t, with_ctx)
        yf = _fourier_mix(p, mats, n_tok, with_ctx)
        z3 = _short_conv(p, conv_w[l], conv_b[l], with_ctx)
        filt = (filt_w1[l], filt_b1[l], filt_freq[l], filt_w2[l], filt_b2[l], filt_w3[l])
        yh = _hyena_mix(z3, filt, hyena_bias[l], mats, with_ctx)
        xa, h2, logits = _merge(xa, ya, yf, yh, p, w_branch_attn[l].astype(BF16), w_branch_fnet[l].astype(BF16),
                                w_branch_hyena[l].astype(BF16), w_out[l].astype(BF16), mod4,
                                norm_ffn[l].reshape(1, D_MODEL), wr, n_tok)
        row_tok, dest, w_sel, blk_expert, n_active = _route(logits, b_router, n_tok)
        xs = jnp.take(h2, row_tok, axis=0, mode='fill', fill_value=0)
        ys = _experts(xs, blk_expert, n_active, w_exp_gate[l].astype(BF16), w_exp_up[l].astype(BF16),
                      w_exp_down[l].astype(BF16))
        y0 = jnp.take(ys, dest[:, 0], axis=0)
        y1 = jnp.take(ys, dest[:, 1], axis=0)
        res = _combine(xa, y0, y1, w_sel, mod4, gain_final, n_tok, last)
        if last:
            out = res
        else:
            xa = res
    return out.reshape(BATCH, SEQ, D_MODEL)
```

```python
import functools
import math

import jax
import jax.numpy as jnp
from jax import lax
from jax.experimental import pallas as pl
from jax.experimental.pallas import tpu as pltpu

F32 = jnp.float32
BF16 = jnp.bfloat16

D_MODEL = 1024
BATCH = 4
SEQ = 4096
DEPTH = 4
GRID_W = 64
CTX_LEN = 256
EPS = 1e-6
N_MOD = 6

HEAD_DIM = 64
N_Q_HEADS = 8
N_KV_HEADS = 2
Q_PER_KV = N_Q_HEADS // N_KV_HEADS
ATTN_BLOCK = 128
ROPE_BASE = 10000.0

FNET_GROUPS = 4
FNET_GROUP_DIM = 128
FNET_WIDTH = FNET_GROUPS * FNET_GROUP_DIM

HYENA_WIDTH = 512
HYENA_ORDER = 2
FILTER_EMB = 33
FILTER_BANDS = (FILTER_EMB - 1) // 2
DECAY_TARGET = 1e-2
FAST_DECAY_PCT = 0.3
SLOW_DECAY_PCT = 1.5

ATTN_WIDTH = N_Q_HEADS * HEAD_DIM
KV_WIDTH = N_KV_HEADS * HEAD_DIM
Q_OFF = 0
K_OFF = Q_OFF + ATTN_WIDTH
V_OFF = K_OFF + KV_WIDTH
F_OFF = V_OFF + KV_WIDTH
H_OFF = F_OFF + FNET_WIDTH
G_OFF = H_OFF + (HYENA_ORDER + 1) * HYENA_WIDTH
IN_WIDTH = G_OFF + 3 * D_MODEL

N_EXPERTS = 16
N_GROUPS = 4
EXPERTS_PER_GROUP = N_EXPERTS // N_GROUPS
TOP_K = 2
EXPERT_FF = 1024

T_LAT = BATCH * SEQ
T_CTX = BATCH * CTX_LEN
T_ALL = T_LAT + T_CTX

PG_OFF = 0
PH_OFF = 3 * D_MODEL
PF_OFF = PH_OFF + (HYENA_ORDER + 1) * HYENA_WIDTH
PQ_OFF = PF_OFF + FNET_WIDTH
PK_OFF = PQ_OFF + ATTN_WIDTH
PV_OFF = PK_OFF + KV_WIDTH

HY_N = 2 * SEQ
HY_N2 = 64
HY_N1 = HY_N // HY_N2
FN_N = 64

TM = 512
EXPERT_BM = 256
VMEM_LIMIT = 52 * 1024 * 1024


def _cparams(sem, vmem=VMEM_LIMIT):
    return pltpu.CompilerParams(dimension_semantics=sem, vmem_limit_bytes=vmem)


def _dot(a, b):
    return jnp.dot(a, b, preferred_element_type=F32)


def _cis(expo, n):
    ang = (2.0 * math.pi / n) * jnp.mod(expo, n).astype(F32)
    return jnp.cos(ang), jnp.sin(ang)


def _real_form(gr, gi):
    return jnp.concatenate([jnp.concatenate([gr, -gi], axis=-1), jnp.concatenate([gi, gr], axis=-1)], axis=-2)


def _dft_mats():
    ar = lambda n: jnp.arange(n, dtype=jnp.int32)
    m = {}
    c, s = _cis(ar(HY_N1)[:, None] * ar(HY_N1 // 2)[None, :], HY_N1)
    m['hy_lead_f'] = _real_form(c, -s).astype(BF16)
    c, s = _cis(ar(HY_N1 // 2)[:, None] * ar(HY_N1)[None, :], HY_N1)
    m['hy_lead_i'] = _real_form(c, s).astype(BF16)
    c, s = _cis(ar(HY_N1)[:, None] * ar(HY_N1)[None, :], HY_N1)
    m['hy_lead_k'] = jnp.concatenate([c, -s], axis=0).astype(BF16)
    a = ar(HY_N1)[:, None, None]
    k2 = ar(HY_N2)[None, :, None]
    n2 = ar(HY_N2)[None, None, :]
    c, s = _cis(n2 * (a + HY_N1 * k2), HY_N)
    m['hy_slab_f'] = _real_form(c, -s).astype(BF16)
    ct = jnp.swapaxes(c, 1, 2) * (1.0 / HY_N)
    st = jnp.swapaxes(s, 1, 2) * (1.0 / HY_N)
    m['hy_slab_i'] = _real_form(ct, st).astype(BF16)
    nc = 2 * CTX_LEN
    c, s = _cis(ar(nc)[:, None] * ar(CTX_LEN)[None, :], nc)
    m['hc_f'] = _real_form(c, -s).astype(BF16)
    c, s = _cis(ar(CTX_LEN)[:, None] * ar(nc)[None, :], nc)
    m['hc_i'] = _real_form(c * (1.0 / nc), s * (1.0 / nc)).astype(BF16)
    c, s = _cis(ar(nc)[:, None] * ar(nc)[None, :], nc)
    m['hc_k'] = jnp.concatenate([c, -s], axis=0).astype(BF16)
    c, s = _cis(ar(FNET_GROUP_DIM)[:, None] * ar(FNET_GROUP_DIM)[None, :], FNET_GROUP_DIM)
    m['fn_chan'] = jnp.concatenate([c, -s], axis=1).astype(BF16)
    c, s = _cis(ar(FN_N)[:, None] * ar(FN_N)[None, :], FN_N)
    m['fn_lead'] = _real_form(c, -s).astype(BF16)
    a = ar(FN_N)[:, None, None]
    k1 = ar(FN_N)[None, :, None]
    n1 = ar(FN_N)[None, None, :]
    scale = 1.0 / math.sqrt(SEQ * FNET_GROUP_DIM)
    c, s = _cis(n1 * (a + FN_N * k1), SEQ)
    m['fn_slab'] = jnp.concatenate([c * scale, s * scale], axis=-1).astype(BF16)
    scale = 1.0 / math.sqrt(CTX_LEN * FNET_GROUP_DIM)
    c, s = _cis(ar(CTX_LEN)[:, None] * ar(CTX_LEN)[None, :], CTX_LEN)
    m['fc'] = jnp.concatenate([c * scale, s * scale], axis=-1).astype(BF16)
    return m


def _mod_kernel(c_ref, w_ref, b_ref, o_ref):
    c = c_ref[...]
    s = c * jax.nn.sigmoid(c)
    o_ref[0] = _dot(s.astype(BF16), w_ref[0].astype(BF16)) + b_ref[0]


def _modulation(c8, w_mod, b_mod):
    tn = 1536
    n = N_MOD * D_MODEL
    return pl.pallas_call(
        _mod_kernel,
        grid=(DEPTH, n // tn),
        in_specs=[pl.BlockSpec((8, D_MODEL), lambda l, j: (0, 0)),
                  pl.BlockSpec((1, D_MODEL, tn), lambda l, j: (l, 0, j)),
                  pl.BlockSpec((1, 1, tn), lambda l, j: (l, 0, j))],
        out_specs=pl.BlockSpec((1, 8, tn), lambda l, j: (l, 0, j)),
        out_shape=jax.ShapeDtypeStruct((DEPTH, 8, n), F32),
        compiler_params=_cparams(("parallel", "parallel")),
        name="adaln_modulation",
    )(c8, w_mod, b_mod.reshape(DEPTH, 1, n))


def _mod_row(tm):
    tiles_per_batch = SEQ // tm
    return lambda i: jnp.minimum(i // tiles_per_batch, BATCH)


def _rms_mod(x, g, sh, sc):
    y = x * lax.rsqrt(jnp.mean(x * x, axis=-1, keepdims=True) + EPS)
    return (y * g) * (1.0 + sc) + sh


def _norm_proj_kernel(x_ref, g_ref, sh_ref, sc_ref, w_ref, o_ref, h_scr):
    @pl.when(pl.program_id(1) == 0)
    def _():
        h_scr[...] = _rms_mod(x_ref[...], g_ref[...], sh_ref[0, 0], sc_ref[0, 0]).astype(BF16)

    o_ref[...] = _dot(h_scr[...], w_ref[...])


def _norm_proj(x, gain, mod4, w, n_tok):
    tm = TM
    n_out = w.shape[1]
    tn = n_out // 2
    row = _mod_row(tm)
    return pl.pallas_call(
        _norm_proj_kernel,
        grid=(n_tok // tm, n_out // tn),
        in_specs=[pl.BlockSpec((tm, D_MODEL), lambda i, j: (i, 0)),
                  pl.BlockSpec((1, D_MODEL), lambda i, j: (0, 0)),
                  pl.BlockSpec((1, 1, 1, D_MODEL), lambda i, j: (row(i), 0, 0, 0)),
                  pl.BlockSpec((1, 1, 1, D_MODEL), lambda i, j: (row(i), 1, 0, 0)),
                  pl.BlockSpec((D_MODEL, tn), lambda i, j: (0, j))],
        out_specs=pl.BlockSpec((tm, tn), lambda i, j: (i, j)),
        out_shape=jax.ShapeDtypeStruct((T_ALL, n_out), F32),
        scratch_shapes=[pltpu.VMEM((tm, D_MODEL), BF16)],
        compiler_params=_cparams(("parallel", "arbitrary")),
        name="norm_in_proj",
    )(x, gain, mod4, mod4, w)


def _softmax_pv(qh, k_parts, v_parts, masks, sink):
    nt = (((1,), (1,)), ((), ()))
    scores = []
    for kp, mk in zip(k_parts, masks):
        s = lax.dot_general(qh, kp, nt, preferred_element_type=F32)
        if mk is not None:
            s = jnp.where(mk, s, -1e30)
        scores.append(s)
    m = sink
    for s in scores:
        m = jnp.maximum(m, jnp.max(s, axis=-1, keepdims=True))
    es = [jnp.exp(s - m) for s in scores]
    den = jnp.exp(sink - m)
    for e in es:
        den = den + jnp.sum(e, axis=-1, keepdims=True)
    inv = 1.0 / den
    o = None
    for e, vp in zip(es, v_parts):
        t = _dot((e * inv).astype(BF16), vp)
        o = t if o is None else o + t
    return o


def _attn_kernel(sink_ref, q_ref, km_ref, k0_ref, kp_ref, vm_ref, v0_ref, vp_ref, kc_ref, vc_ref,
                 cos_ref, sin_ref, o_ref, *, nb):
    n = pl.program_id(1)
    blk = ATTN_BLOCK
    lane = lax.broadcasted_iota(jnp.int32, (blk, 128), 1)
    first = (lane % 32) < 16

    def rope(x, blk_idx):
        r0 = pl.multiple_of(blk_idx * blk, blk)
        c = cos_ref[pl.ds(r0, blk), :]
        s = sin_ref[pl.ds(r0, blk), :]
        sw = jnp.where(first, pltpu.roll(x, 112, 1), pltpu.roll(x, 16, 1))
        return x * c + sw * s

    nm = jnp.maximum(n - 1, 0)
    npl = jnp.minimum(n + 1, nb - 1)
    kband = jnp.concatenate([rope(km_ref[...], nm), rope(k0_ref[...], n), rope(kp_ref[...], npl)],
                            axis=0).astype(BF16)
    vband = jnp.concatenate([vm_ref[...], v0_ref[...], vp_ref[...]], axis=0).astype(BF16)
    kc = kc_ref[...].astype(BF16)
    vc = vc_ref[...].astype(BF16)

    row = lax.broadcasted_iota(jnp.int32, (blk, 3 * blk), 0)
    col = lax.broadcasted_iota(jnp.int32, (blk, 3 * blk), 1)
    cj = col % blk
    has_prev = (n > 0).astype(jnp.int32)
    has_next = (n < nb - 1).astype(jnp.int32)
    left = jnp.where(cj >= row, has_prev, 0)
    right = jnp.where(cj <= row, has_next, 0)
    valid = jnp.where(col < blk, left, jnp.where(col < 2 * blk, 1, right)) > 0

    scale = HEAD_DIM ** -0.5
    outs = []
    for pair in range(N_Q_HEADS // 2):
        q2 = (rope(q_ref[:, pair * 128:(pair + 1) * 128], n) * scale).astype(BF16)
        for sub in range(2):
            head = 2 * pair + sub
            kvh = head // Q_PER_KV
            sl = slice(kvh * HEAD_DIM, (kvh + 1) * HEAD_DIM)
            qh = q2[:, sub * HEAD_DIM:(sub + 1) * HEAD_DIM]
            outs.append(_softmax_pv(qh, [kband[:, sl], kc[:, sl]], [vband[:, sl], vc[:, sl]],
                                    [valid, None], sink_ref[head]))
    o_ref[...] = jnp.concatenate(outs, axis=-1).astype(o_ref.dtype)


def _ctx_attn_kernel(sink_ref, q_ref, kc_ref, vc_ref, o_ref):
    kc = kc_ref[...].astype(BF16)
    vc = vc_ref[...].astype(BF16)
    scale = HEAD_DIM ** -0.5
    outs = []
    for pair in range(N_Q_HEADS // 2):
        q2 = (q_ref[:, pair * 128:(pair + 1) * 128] * scale).astype(BF16)
        for sub in range(2):
            head = 2 * pair + sub
            kvh = head // Q_PER_KV
            sl = slice(kvh * HEAD_DIM, (kvh + 1) * HEAD_DIM)
            qh = q2[:, sub * HEAD_DIM:(sub + 1) * HEAD_DIM]
            outs.append(_softmax_pv(qh, [kc[:, sl]], [vc[:, sl]], [None], sink_ref[head]))
    o_ref[...] = jnp.concatenate(outs, axis=-1).astype(o_ref.dtype)


def _attention(p, sink, cos_t, sin_t, with_ctx):
    blk = ATTN_BLOCK
    nb = SEQ // blk
    qc, kcol, vcol = PQ_OFF // ATTN_WIDTH, PK_OFF // KV_WIDTH, PV_OFF // KV_WIDTH
    ctx_blk = T_LAT // CTX_LEN
    smem = pl.BlockSpec(memory_space=pltpu.SMEM)

    def kv_spec(col, d):
        return pl.BlockSpec((blk, KV_WIDTH),
                            lambda b, n: (b * nb + jnp.clip(n + d, 0, nb - 1), col))

    ya = pl.pallas_call(
        functools.partial(_attn_kernel, nb=nb),
        grid=(BATCH, nb),
        in_specs=[smem,
                  pl.BlockSpec((blk, ATTN_WIDTH), lambda b, n: (b * nb + n, qc)),
                  kv_spec(kcol, -1), kv_spec(kcol, 0), kv_spec(kcol, 1),
                  kv_spec(vcol, -1), kv_spec(vcol, 0), kv_spec(vcol, 1),
                  pl.BlockSpec((CTX_LEN, KV_WIDTH), lambda b, n: (ctx_blk + b, kcol)),
                  pl.BlockSpec((CTX_LEN, KV_WIDTH), lambda b, n: (ctx_blk + b, vcol)),
                  pl.BlockSpec((SEQ, KV_WIDTH), lambda b, n: (0, 0)),
                  pl.BlockSpec((SEQ, KV_WIDTH), lambda b, n: (0, 0))],
        out_specs=pl.BlockSpec((blk, ATTN_WIDTH), lambda b, n: (b * nb + n, 0)),
        out_shape=jax.ShapeDtypeStruct((T_ALL, ATTN_WIDTH), BF16),
        compiler_params=_cparams(("parallel", "parallel")),
        name="banded_attention",
    )(sink, p, p, p, p, p, p, p, p, p, cos_t, sin_t)
    if not with_ctx:
        return ya
    cb = CTX_LEN // blk
    lat_blk = T_LAT // blk

    def alias_kernel(sink_ref, q_ref, kc_ref, vc_ref, ya_in_ref, o_ref):
        del ya_in_ref
        _ctx_attn_kernel(sink_ref, q_ref, kc_ref, vc_ref, o_ref)

    return pl.pallas_call(
        alias_kernel,
        grid=(BATCH, cb),
        in_specs=[smem,
                  pl.BlockSpec((blk, ATTN_WIDTH), lambda b, n: (lat_blk + b * cb + n, qc)),
                  pl.BlockSpec((CTX_LEN, KV_WIDTH), lambda b, n: (ctx_blk + b, kcol)),
                  pl.BlockSpec((CTX_LEN, KV_WIDTH), lambda b, n: (ctx_blk + b, vcol)),
                  pl.BlockSpec(memory_space=pl.ANY)],
        out_specs=pl.BlockSpec((blk, ATTN_WIDTH), lambda b, n: (lat_blk + b * cb + n, 0)),
        out_shape=jax.ShapeDtypeStruct((T_ALL, ATTN_WIDTH), BF16),
        input_output_aliases={4: 0},
        compiler_params=_cparams(("parallel", "parallel")),
        name="context_attention",
    )(sink, p, p, p, ya)


def _rope_tables():
    n_freq = HEAD_DIM // 4
    freqs = ROPE_BASE ** (-jnp.arange(n_freq, dtype=F32) / n_freq)
    t = jnp.arange(SEQ, dtype=jnp.int32)
    rows = (t // GRID_W).astype(F32)[:, None] * freqs
    cols = (t % GRID_W).astype(F32)[:, None] * freqs
    cos_h = jnp.concatenate([jnp.cos(rows), jnp.cos(rows), jnp.cos(cols), jnp.cos(cols)], axis=-1)
    sin_h = jnp.concatenate([-jnp.sin(rows), jnp.sin(rows), -jnp.sin(cols), jnp.sin(cols)], axis=-1)
    return jnp.tile(cos_h, (1, 2)), jnp.tile(sin_h, (1, 2))


def _lead_kernel(*refs, n_in, cmul, epi):
    m_ref = refs[0]
    x_refs = refs[1:1 + n_in]
    pos = 1 + n_in
    xs = []
    for r in x_refs:
        v = r[...]
        xs.append(v.reshape(-1, v.shape[-1]))
    x = xs[0] if n_in == 1 else jnp.concatenate(xs, axis=0)
    if cmul:
        k = refs[pos][...]
        pos += 1
        k = k.reshape(-1, k.shape[-1])
        half = x.shape[0] // 2
        xr, xi, kr, ki = x[:half], x[half:], k[:half], k[half:]
        x = jnp.concatenate([xr * kr - xi * ki, xr * ki + xi * kr], axis=0)
    res = _dot(m_ref[...], x.astype(BF16))
    if epi:
        g_ref, y_ref, b_ref = refs[pos:pos + 3]
        pos += 3
        g = g_ref[...]
        y = y_ref[...]
        res = g.reshape(-1, g.shape[-1]) * (res + y.reshape(-1, y.shape[-1]) * b_ref[...])
    o_ref = refs[pos]
    o_ref[...] = res.reshape(o_ref.shape).astype(o_ref.dtype)


def _lead(mat, xs, x_specs, grid, out_shape, out_spec, *, kspec=None, epi=None, alias_to=None, name):
    ins = [mat] + list(xs)
    specs = [pl.BlockSpec(mat.shape, lambda *a: (0, 0))] + list(x_specs)
    if kspec is not None:
        ins.append(kspec[0])
        specs.append(kspec[1])
    if epi is not None:
        for arr, sp in epi:
            ins.append(arr)
            specs.append(sp)
    kern = functools.partial(_lead_kernel, n_in=len(xs), cmul=kspec is not None, epi=epi is not None)
    aliases = {}
    if alias_to is not None:
        aliases = {len(ins): 0}
        ins.append(alias_to)
        specs.append(pl.BlockSpec(memory_space=pl.ANY))
        inner = kern

        def kern(*refs):
            inner(*refs[:-2], refs[-1])

    return pl.pallas_call(
        kern, grid=grid, in_specs=specs, out_specs=out_spec, out_shape=out_shape,
        input_output_aliases=aliases,
        compiler_params=_cparams(("parallel",) * len(grid)), name=name,
    )(*ins)


def _slab_kernel(*refs, sb, mul, inv, real_out, transpose_out):
    mf_ref, x_ref = refs[0], refs[1]
    pos = 2
    if mul:
        k_ref = refs[pos]
        pos += 1
    if inv:
        mi_ref = refs[pos]
        pos += 1
    o_ref = refs[pos]
    r = x_ref.shape[3]
    for s in range(sb):
        x = jnp.concatenate([x_ref[0, 0, s], x_ref[0, 1, s]], axis=0).astype(BF16)
        y = _dot(mf_ref[s], x)
        if mul:
            yr, yi = y[:r], y[r:]
            kr, ki = k_ref[0, 0, s], k_ref[0, 1, s]
            y = jnp.concatenate([yr * kr - yi * ki, yr * ki + yi * kr], axis=0)
        if inv:
            y = _dot(mi_ref[s], y.astype(BF16))
        if real_out:
            if transpose_out:
                o_ref[0, :, s, :] = y
            else:
                o_ref[0, s] = y
        else:
            o_ref[0, 0, s] = y[:r]
            o_ref[0, 1, s] = y[r:]


def _slab(mf, x, *, kspec=None, korder=0, mi=None, real_out=False, name):
    g, _, ns, r, c = x.shape
    sb = 8
    ins = [mf, x]
    specs = [pl.BlockSpec((sb,) + mf.shape[1:], lambda b, j: (j, 0, 0)),
             pl.BlockSpec((1, 2, sb, r, c), lambda b, j: (b, 0, j, 0, 0))]
    if kspec is not None:
        ins.append(kspec)
        specs.append(pl.BlockSpec((1, 2, sb, r, c), lambda b, j: (korder, 0, j, 0, 0)))
    if mi is not None:
        ins.append(mi)
        specs.append(pl.BlockSpec((sb,) + mi.shape[1:], lambda b, j: (j, 0, 0)))
    if real_out:
        out_shape = jax.ShapeDtypeStruct((g, ns, mf.shape[1], c), F32)
        out_spec = pl.BlockSpec((1, sb, mf.shape[1], c), lambda b, j: (b, j, 0, 0))
    else:
        out_shape = jax.ShapeDtypeStruct((g, 2, ns, r, c), F32)
        out_spec = pl.BlockSpec((1, 2, sb, r, c), lambda b, j: (b, 0, j, 0, 0))
    kern = functools.partial(_slab_kernel, sb=sb, mul=kspec is not None, inv=mi is not None,
                             real_out=real_out, transpose_out=False)
    return pl.pallas_call(
        kern, grid=(g, ns // sb), in_specs=specs, out_specs=out_spec, out_shape=out_shape,
        compiler_params=_cparams(("parallel", "parallel")), name=name,
    )(*ins)


def _chan_dft_kernel(u_ref, m_ref, zr_ref, zi_ref):
    gd = FNET_GROUP_DIM
    m = m_ref[...]
    for g in range(FNET_GROUPS):
        z = _dot(u_ref[:, g * gd:(g + 1) * gd].astype(BF16), m)
        zr_ref[:, g * gd:(g + 1) * gd] = z[:, :gd]
        zi_ref[:, g * gd:(g + 1) * gd] = z[:, gd:]


def _fourier_mix(p, mats, n_tok, with_ctx):
    tm = TM
    fcol = PF_OFF // FNET_WIDTH
    zr, zi = pl.pallas_call(
        _chan_dft_kernel,
        grid=(n_tok // tm,),
        in_specs=[pl.BlockSpec((tm, FNET_WIDTH), lambda i: (i, fcol)),
                  pl.BlockSpec(mats['fn_chan'].shape, lambda i: (0, 0))],
        out_specs=[pl.BlockSpec((tm, FNET_WIDTH), lambda i: (i, 0))] * 2,
        out_shape=[jax.ShapeDtypeStruct((T_ALL, FNET_WIDTH), F32)] * 2,
        compiler_params=_cparams(("parallel",)),
        name="fnet_channel_dft",
    )(p, mats['fn_chan'])
    lanes = FN_N * FNET_WIDTH
    tl = 4096
    zr2 = zr.reshape(T_ALL // FN_N, lanes)
    zi2 = zi.reshape(T_ALL // FN_N, lanes)
    xspec = pl.BlockSpec((FN_N, tl), lambda b, j: (b, j))
    a = _lead(mats['fn_lead'], [zr2, zi2], [xspec, xspec], (BATCH, lanes // tl),
              jax.ShapeDtypeStruct((BATCH, 2 * FN_N, lanes), F32),
              pl.BlockSpec((1, 2 * FN_N, tl), lambda b, j: (b, 0, j)), name="fnet_lead")
    a = a.reshape(BATCH, 2, FN_N, FN_N, FNET_WIDTH)
    y2 = _slab(mats['fn_slab'], a, real_out=True, name="fnet_slab")
    yf_lat = jnp.swapaxes(y2, 1, 2).reshape(T_LAT, FNET_WIDTH)
    if not with_ctx:
        return yf_lat
    ctx_blk = T_LAT // CTX_LEN
    cspec = pl.BlockSpec((CTX_LEN, FNET_WIDTH), lambda b: (ctx_blk + b, 0))
    yf_ctx = _lead(mats['fc'], [zr, zi], [cspec, cspec], (BATCH,),
                   jax.ShapeDtypeStruct((T_CTX, FNET_WIDTH), F32),
                   pl.BlockSpec((CTX_LEN, FNET_WIDTH), lambda b: (b, 0)), name="fnet_ctx")
    return jnp.concatenate([yf_lat, yf_ctx], axis=0)


def _short_conv_kernel(u_ref, w_ref, b_ref, o_ref, *, rows, chunk):
    w0 = w_ref[0:1, :]
    w1 = w_ref[1:2, :]
    w2 = w_ref[2:3, :]
    bias = b_ref[...]
    width = u_ref.shape[-1]
    ridx = lax.broadcasted_iota(jnp.int32, (chunk, width), 0)
    n_chunks = rows // chunk
    for ci in range(n_chunks):
        r0 = ci * chunk
        cur = u_ref[r0:r0 + chunk, :]
        if ci > 0:
            prev_row = u_ref[r0 - 8:r0, :][7:8, :]
        else:
            prev_row = jnp.zeros((1, width), F32)
        if ci < n_chunks - 1:
            next_row = u_ref[r0 + chunk:r0 + chunk + 8, :][0:1, :]
        else:
            next_row = jnp.zeros((1, width), F32)
        up = jnp.where(ridx == 0, prev_row, pltpu.roll(cur, 1, 0))
        dn = jnp.where(ridx == chunk - 1, next_row, pltpu.roll(cur, chunk - 1, 0))
        o_ref[0, r0:r0 + chunk, :] = up * w0 + cur * w1 + dn * w2 + bias


def _short_conv(p, conv_w, conv_b, with_ctx):
    cw = 256
    hw = (HYENA_ORDER + 1) * HYENA_WIDTH
    ncol = hw // cw
    per = HYENA_WIDTH // cw
    col0 = PH_OFF // cw
    out_shape = jax.ShapeDtypeStruct((HYENA_ORDER + 1, T_ALL, HYENA_WIDTH), F32)
    b2 = conv_b.reshape(1, hw)

    def call(rows, blk0, alias):
        kern = functools.partial(_short_conv_kernel, rows=rows, chunk=min(rows, 256))
        ins = [p, conv_w, b2]
        specs = [pl.BlockSpec((rows, cw), lambda b, j: (blk0 + b, col0 + j)),
                 pl.BlockSpec((3, cw), lambda b, j: (0, j)),
                 pl.BlockSpec((1, cw), lambda b, j: (0, j))]
        aliases = {}
        if alias is not None:
            ins.append(alias)
            specs.append(pl.BlockSpec(memory_space=pl.ANY))
            aliases = {3: 0}
            inner = kern

            def kern(u_ref, w_ref, b_ref, a_ref, o_ref):
                del a_ref
                inner(u_ref, w_ref, b_ref, o_ref)

        return pl.pallas_call(
            kern, grid=(BATCH, ncol), in_specs=specs,
            out_specs=pl.BlockSpec((1, rows, cw), lambda b, j: (j // per, blk0 + b, j % per)),
            out_shape=out_shape, input_output_aliases=aliases,
            compiler_params=_cparams(("parallel", "parallel")), name="hyena_short_conv",
        )(*ins)

    z3 = call(SEQ, 0, None)
    if with_ctx:
        z3 = call(CTX_LEN, T_LAT // CTX_LEN, z3)
    return z3


def _hyena_filters(n, w1, b1, freq, w2, b2, w3):
    pos = jnp.arange(n, dtype=F32)
    t = pos / max(n - 1, 1)
    omega = 2.0 * math.pi * pos / n
    bands = jnp.linspace(1e-4, FILTER_BANDS - 1, FILTER_BANDS, dtype=F32)
    feats = jnp.concatenate([t[:, None], jnp.cos(omega[:, None] * bands), -jnp.sin(omega[:, None] * bands)], axis=-1)
    h = jnp.sin(freq * (feats @ w1 + b1))
    h = jnp.sin(freq * (h @ w2 + b2))
    h = (h @ w3).reshape(n, HYENA_ORDER, 2, HYENA_WIDTH)
    deltas = jnp.abs(jnp.linspace(math.log(DECAY_TARGET) / SLOW_DECAY_PCT, math.log(DECAY_TARGET) / FAST_DECAY_PCT,
                                  HYENA_WIDTH, dtype=F32))
    h = h * jnp.exp(-t[:, None] * deltas)[:, None, None, :]
    fwd, bwd = h[:, :, 0], h[:, :, 1]
    k = jnp.concatenate([fwd, jnp.zeros_like(fwd[:1]), bwd[:0:-1]], axis=0)
    return k / jnp.sum(jnp.abs(k), axis=0, keepdims=True)


def _hyena_mix(z3, filt, hyena_bias, mats, with_ctx):
    c = HYENA_WIDTH
    lanes = HY_N2 * c
    tl = 2048
    half = HY_N1 // 2
    pairs = BATCH // 2
    k_lat = _hyena_filters(SEQ, *filt)
    k2d = jnp.transpose(k_lat, (1, 0, 2)).reshape(HYENA_ORDER, HY_N1, lanes)
    ka = _lead(mats['hy_lead_k'], [k2d], [pl.BlockSpec((1, HY_N1, tl), lambda o, j: (o, 0, j))],
               (HYENA_ORDER, lanes // tl), jax.ShapeDtypeStruct((HYENA_ORDER, 2 * HY_N1, lanes), F32),
               pl.BlockSpec((1, 2 * HY_N1, tl), lambda o, j: (o, 0, j)), name="hyena_filter_lead")
    kspec = _slab(mats['hy_slab_f'], ka.reshape(HYENA_ORDER, 2, HY_N1, HY_N2, c), name="hyena_filter_slab")
    if with_ctx:
        nc = 2 * CTX_LEN
        k_ctx = jnp.transpose(_hyena_filters(CTX_LEN, *filt), (1, 0, 2))
        kspec_c = _lead(mats['hc_k'], [k_ctx], [pl.BlockSpec((1, nc, c), lambda o: (o, 0, 0))],
                        (HYENA_ORDER,), jax.ShapeDtypeStruct((HYENA_ORDER, 2 * nc, c), F32),
                        pl.BlockSpec((1, 2 * nc, c), lambda o: (o, 0, 0)), name="hyena_ctx_filter")
    bias_l = jnp.tile(hyena_bias, (1, HY_N2)).reshape(HYENA_ORDER, 1, lanes)
    bias_c = hyena_bias.reshape(HYENA_ORDER, 1, c)

    z2d = z3.reshape(HYENA_ORDER + 1, T_ALL // HY_N2, lanes)
    y = None
    for o in range(HYENA_ORDER):
        if y is None:
            xin, xspec = z2d, pl.BlockSpec((1, 2 * half, tl), lambda b, j: (0, b, j))
        else:
            xin, xspec = y.reshape(T_ALL // HY_N2, lanes), pl.BlockSpec((2 * half, tl), lambda b, j: (b, j))
        a = _lead(mats['hy_lead_f'], [xin], [xspec], (pairs, lanes // tl),
                  jax.ShapeDtypeStruct((pairs, 2 * HY_N1, lanes), F32),
                  pl.BlockSpec((1, 2 * HY_N1, tl), lambda b, j: (b, 0, j)), name="hyena_lead_fwd")
        bq = _slab(mats['hy_slab_f'], a.reshape(pairs, 2, HY_N1, HY_N2, c), kspec=kspec, korder=o,
                   mi=mats['hy_slab_i'], name="hyena_slab")
        bq = bq.reshape(pairs, 2 * HY_N1, lanes)
        gate_spec = pl.BlockSpec((1, 2 * half, tl), lambda b, j, o=o: (o + 1, b, j))
        ynew = _lead(mats['hy_lead_i'], [bq], [pl.BlockSpec((1, 2 * HY_N1, tl), lambda b, j: (b, 0, j))],
                     (pairs, lanes // tl), jax.ShapeDtypeStruct((T_ALL // HY_N2, lanes), F32),
                     pl.BlockSpec((2 * half, tl), lambda b, j: (b, j)),
                     epi=[(z2d, gate_spec), (xin, xspec),
                          (bias_l, pl.BlockSpec((1, 1, tl), lambda b, j, o=o: (o, 0, j)))],
                     name="hyena_lead_inv")
        ynew = ynew.reshape(T_ALL, c)
        if with_ctx:
            nc = 2 * CTX_LEN
            blk0 = T_LAT // nc
            if y is None:
                cin, cspec_in = z3, pl.BlockSpec((1, nc, c), lambda b: (0, blk0 + b, 0))
            else:
                cin, cspec_in = y, pl.BlockSpec((nc, c), lambda b: (blk0 + b, 0))
            xc = _lead(mats['hc_f'], [cin], [cspec_in], (pairs,),
                       jax.ShapeDtypeStruct((pairs, 2 * nc, c), F32),
                       pl.BlockSpec((1, 2 * nc, c), lambda b: (b, 0, 0)), name="hyena_ctx_fwd")
            ynew = _lead(mats['hc_i'], [xc], [pl.BlockSpec((1, 2 * nc, c), lambda b: (b, 0, 0))],
                         (pairs,), jax.ShapeDtypeStruct((T_ALL, c), F32),
                         pl.BlockSpec((nc, c), lambda b: (blk0 + b, 0)),
                         kspec=(kspec_c, pl.BlockSpec((1, 2 * nc, c), lambda b, o=o: (o, 0, 0))),
                         epi=[(z3, pl.BlockSpec((1, nc, c), lambda b, o=o: (o + 1, blk0 + b, 0))),
                              (cin, cspec_in),
                              (bias_c, pl.BlockSpec((1, 1, c), lambda b, o=o: (o, 0, 0)))],
                         alias_to=ynew, name="hyena_ctx_inv")
        y = ynew
    return y


def _route_tile(lt, br, base, tri):
    tm = lt.shape[1]
    aff = jax.nn.sigmoid(lt)
    biased = aff + br
    b = [biased[e:e + 1, :] for e in range(N_EXPERTS)]
    a = [aff[e:e + 1, :] for e in range(N_EXPERTS)]
    epg = EXPERTS_PER_GROUP
    scores = []
    for g in range(N_GROUPS):
        x0, x1, x2, x3 = b[epg * g:epg * g + epg]
        s1, t1 = jnp.maximum(x0, x1), jnp.minimum(x0, x1)
        s2, t2 = jnp.maximum(x2, x3), jnp.minimum(x2, x3)
        scores.append(jnp.maximum(s1, s2) + jnp.maximum(jnp.minimum(s1, s2), jnp.maximum(t1, t2)))
    best = scores[0]
    gsel = jnp.zeros((1, tm), jnp.int32)
    for g in range(1, N_GROUPS):
        gsel = jnp.where(scores[g] > best, g, gsel)
        best = jnp.maximum(best, scores[g])

    def pick(rows, j):
        out = rows[j]
        for g in range(1, N_GROUPS):
            out = jnp.where(gsel == g, rows[epg * g + j], out)
        return out

    v = [pick(b, j) for j in range(epg)]
    av = [pick(a, j) for j in range(epg)]
    i1 = jnp.zeros((1, tm), jnp.int32)
    m1 = v[0]
    for j in range(1, epg):
        i1 = jnp.where(v[j] > m1, j, i1)
        m1 = jnp.maximum(m1, v[j])
    neg = jnp.float32(-3.0e38)
    i2 = jnp.zeros((1, tm), jnp.int32)
    m2 = jnp.full((1, tm), neg, F32)
    for j in range(epg):
        cand = jnp.where(i1 == j, neg, v[j])
        take = cand > m2
        i2 = jnp.where(take, j, i2)
        m2 = jnp.where(take, cand, m2)

    def sel(rows, idx):
        out = rows[0]
        for j in range(1, epg):
            out = jnp.where(idx == j, rows[j], out)
        return out

    a1, a2 = sel(av, i1), sel(av, i2)
    den = a1 + a2
    e1 = gsel * epg + i1
    e2 = gsel * epg + i2
    eio = lax.broadcasted_iota(jnp.int32, (N_EXPERTS, tm), 0)
    oh1 = jnp.where(eio == e1, 1.0, 0.0)
    oh2 = jnp.where(eio == e2, 1.0, 0.0)
    oh = oh1 + oh2
    tot = base + _dot(oh.astype(BF16), tri)
    r1 = jnp.sum(oh1 * tot, axis=0, keepdims=True)
    r2 = jnp.sum(oh2 * tot, axis=0, keepdims=True)
    new_base = base + jnp.sum(oh, axis=1, keepdims=True)
    return (e1, e2), (a1 / den, a2 / den), (r1.astype(jnp.int32), r2.astype(jnp.int32)), new_base


def _merge_kernel(x_ref, ya_ref, yf_ref, yh_ref, gt_ref, wa_ref, wf_ref, wh_ref, wo_ref, g1_ref,
                  gn_ref, sh_ref, sc_ref, wrt_ref, br_ref, xo_ref, h2_ref, e_ref, w_ref, r_ref, cnt_ref):
    d = D_MODEL
    merged = jax.nn.sigmoid(gt_ref[:, 0:d]) * _dot(ya_ref[...], wa_ref[...])
    merged += jax.nn.sigmoid(gt_ref[:, d:2 * d]) * _dot(yf_ref[...].astype(BF16), wf_ref[...])
    merged += jax.nn.sigmoid(gt_ref[:, 2 * d:3 * d]) * _dot(yh_ref[...].astype(BF16), wh_ref[...])
    xn = x_ref[...] + g1_ref[0, 0] * _dot(merged.astype(BF16), wo_ref[...])
    xo_ref[...] = xn
    h2 = _rms_mod(xn, gn_ref[...], sh_ref[0, 0], sc_ref[0, 0]).astype(BF16)
    h2_ref[...] = h2

    @pl.when(pl.program_id(0) == 0)
    def _():
        cnt_ref[...] = jnp.zeros_like(cnt_ref)

    tm = h2.shape[0]
    lt = lax.dot_general(wrt_ref[...], h2, (((1,), (1,)), ((), ())), preferred_element_type=F32)
    tri = jnp.where(lax.broadcasted_iota(jnp.int32, (tm, tm), 0) < lax.broadcasted_iota(jnp.int32, (tm, tm), 1),
                    1.0, 0.0).astype(BF16)
    es, ws, rs, new_base = _route_tile(lt, br_ref[...], cnt_ref[:, 0:1], tri)
    e_ref[0:1, :], e_ref[1:2, :] = es
    w_ref[0:1, :], w_ref[1:2, :] = ws
    r_ref[0:1, :], r_ref[1:2, :] = rs
    cnt_ref[...] = jnp.broadcast_to(new_base, cnt_ref.shape)


def _merge(x, ya, yf, yh, p, wa, wf, wh, wo, mod4, gain2, wrt, br, n_tok):
    tm = 256
    row = _mod_row(tm)
    full = lambda a: pl.BlockSpec(a.shape, lambda i: (0,) * a.ndim)
    modspec = lambda k: pl.BlockSpec((1, 1, 1, D_MODEL), lambda i: (row(i), k, 0, 0))
    tok = lambda w: pl.BlockSpec((tm, w), lambda i: (i, 0))
    lane = pl.BlockSpec((TOP_K, tm), lambda i: (0, i))
    return pl.pallas_call(
        _merge_kernel,
        grid=(n_tok // tm,),
        in_specs=[tok(D_MODEL), tok(ATTN_WIDTH), tok(FNET_WIDTH), tok(HYENA_WIDTH),
                  pl.BlockSpec((tm, 3 * D_MODEL), lambda i: (i, PG_OFF // (3 * D_MODEL))),
                  full(wa), full(wf), full(wh), full(wo), modspec(2), full(gain2), modspec(3), modspec(4),
                  full(wrt), full(br)],
        out_specs=[tok(D_MODEL), tok(D_MODEL), lane, lane, lane,
                   pl.BlockSpec((N_EXPERTS, 128), lambda i: (0, 0))],
        out_shape=[jax.ShapeDtypeStruct((n_tok, D_MODEL), F32),
                   jax.ShapeDtypeStruct((n_tok, D_MODEL), BF16),
                   jax.ShapeDtypeStruct((TOP_K, n_tok), jnp.int32),
                   jax.ShapeDtypeStruct((TOP_K, n_tok), F32),
                   jax.ShapeDtypeStruct((TOP_K, n_tok), jnp.int32),
                   jax.ShapeDtypeStruct((N_EXPERTS, 128), F32)],
        compiler_params=_cparams(("arbitrary",)),
        name="merge_out_norm_route",
    )(x, ya, yf, yh, p, wa, wf, wh, wo, mod4, gain2, mod4, mod4, wrt, br)


def _expert_kernel(be_ref, na_ref, x_ref, wg_ref, wu_ref, wd_ref, o_ref):
    del be_ref
    i = pl.program_id(0)

    @pl.when(i < na_ref[0])
    def _():
        x = x_ref[...]
        g = _dot(x, wg_ref[0])
        u = _dot(x, wu_ref[0])
        h = (g * jax.nn.sigmoid(g)) * u
        o_ref[...] = _dot(h.astype(BF16), wd_ref[0])

    @pl.when(i >= na_ref[0])
    def _():
        o_ref[...] = jnp.zeros_like(o_ref)


def _experts(xs, blk_expert, n_active, wg, wu, wd):
    bm = EXPERT_BM
    n_blk = xs.shape[0] // bm
    wspec = lambda k, n: pl.BlockSpec((1, k, n), lambda i, be, na: (be[i], 0, 0))
    return pl.pallas_call(
        _expert_kernel,
        grid_spec=pltpu.PrefetchScalarGridSpec(
            num_scalar_prefetch=2,
            grid=(n_blk,),
            in_specs=[pl.BlockSpec((bm, D_MODEL), lambda i, be, na: (i, 0)),
                      wspec(D_MODEL, EXPERT_FF), wspec(D_MODEL, EXPERT_FF), wspec(EXPERT_FF, D_MODEL)],
            out_specs=pl.BlockSpec((bm, D_MODEL), lambda i, be, na: (i, 0))),
        out_shape=jax.ShapeDtypeStruct((xs.shape[0], D_MODEL), F32),
        compiler_params=_cparams(("arbitrary",)),
        name="moe_experts",
    )(blk_expert, n_active, xs, wg, wu, wd)


def _dispatch(e_idx, rank, counts, n_tok):
    bm = EXPERT_BM
    counts = counts.astype(jnp.int32)
    padded = (counts + bm - 1) // bm * bm
    pad_end = jnp.cumsum(padded)
    pad_start = pad_end - padded
    experts = jnp.arange(N_EXPERTS, dtype=jnp.int32)
    start = jnp.sum(jnp.where(e_idx[..., None] == experts, pad_start, 0), axis=-1)
    dest = start + rank
    n_rows = -(-(n_tok * TOP_K) // bm) * bm + N_EXPERTS * bm
    n_blk = n_rows // bm
    tok = jnp.tile(jnp.arange(n_tok, dtype=jnp.int32), TOP_K)
    row_tok = jnp.full((n_rows,), T_ALL, jnp.int32).at[dest.reshape(-1)].set(tok)
    blk_start = jnp.arange(n_blk, dtype=jnp.int32) * bm
    blk_expert = jnp.minimum(jnp.sum((blk_start[:, None] >= pad_end[None, :]).astype(jnp.int32), axis=-1),
                             N_EXPERTS - 1)
    n_active = (pad_end[-1] // bm).astype(jnp.int32).reshape(1)
    return row_tok, dest, blk_expert, n_active


def _combine_kernel(x_ref, y0_ref, y1_ref, w_ref, g2_ref, gf_ref, o_ref, *, final):
    w = w_ref[...]
    moe = y0_ref[...] * w[:, 0:1] + y1_ref[...] * w[:, 1:2]
    xn = x_ref[...] + g2_ref[0, 0] * moe
    if final:
        y = xn * lax.rsqrt(jnp.mean(xn * xn, axis=-1, keepdims=True) + EPS)
        xn = y * gf_ref[...]
    o_ref[...] = xn


def _combine(x, y0, y1, w_sel, mod4, gain_final, n_tok, final):
    tm = TM
    row = _mod_row(tm)
    tok = lambda w: pl.BlockSpec((tm, w), lambda i: (i, 0))
    return pl.pallas_call(
        functools.partial(_combine_kernel, final=final),
        grid=(n_tok // tm,),
        in_specs=[tok(D_MODEL), tok(D_MODEL), tok(D_MODEL), tok(TOP_K),
                  pl.BlockSpec((1, 1, 1, D_MODEL), lambda i: (row(i), 5, 0, 0)),
                  pl.BlockSpec((1, D_MODEL), lambda i: (0, 0))],
        out_specs=tok(D_MODEL),
        out_shape=jax.ShapeDtypeStruct((n_tok if final else T_ALL, D_MODEL), F32),
        compiler_params=_cparams(("parallel",)),
        name="moe_combine",
    )(x, y0, y1, w_sel, mod4, gain_final)


def kernel(x, c, ctx, c_ctx, w_mod, b_mod, norm_mix, norm_ffn, w_in, attn_sink, conv_w, conv_b, filt_w1, filt_b1, filt_freq, filt_w2, filt_b2, filt_w3, hyena_bias, w_branch_attn, w_branch_fnet, w_branch_hyena, w_out, w_router, b_router, w_exp_gate, w_exp_up, w_exp_down, norm_final):
    mats = _dft_mats()
    cos_t, sin_t = _rope_tables()
    c8 = jnp.concatenate([c, c_ctx[None, :], jnp.zeros((8 - BATCH - 1, D_MODEL), F32)], axis=0)
    mod_all = _modulation(c8, w_mod, b_mod)
    xa = jnp.concatenate([x.reshape(T_LAT, D_MODEL), ctx.reshape(T_CTX, D_MODEL)], axis=0)
    wrt = w_router.T.astype(BF16)
    br = b_router.astype(F32).reshape(N_EXPERTS, 1)
    gain_final = norm_final.reshape(1, D_MODEL)
    out = None
    for l in range(DEPTH):
        last = l == DEPTH - 1
        with_ctx = not last
        n_tok = T_LAT if last else T_ALL
        mod4 = mod_all[l].reshape(8, N_MOD, 1, D_MODEL)
        wl = w_in[l]
        w_perm = jnp.concatenate([wl[:, G_OFF:], wl[:, H_OFF:G_OFF], wl[:, F_OFF:H_OFF], wl[:, Q_OFF:F_OFF]],
                                 axis=1).astype(BF16)
        p = _norm_proj(xa, norm_mix[l].reshape(1, D_MODEL), mod4, w_perm, T_ALL)
        ya = _attention(p, attn_sink[l], cos_t, sin_t, with_ctx)
        yf = _fourier_mix(p, mats, n_tok, with_ctx)
        z3 = _short_conv(p, conv_w[l], conv_b[l], with_ctx)
        filt = (filt_w1[l], filt_b1[l], filt_freq[l], filt_w2[l], filt_b2[l], filt_w3[l])
        yh = _hyena_mix(z3, filt, hyena_bias[l], mats, with_ctx)
        xa, h2, e_idx, w_sel, rank, cnt = _merge(
            xa, ya, yf, yh, p, w_branch_attn[l].astype(BF16), w_branch_fnet[l].astype(BF16),
            w_branch_hyena[l].astype(BF16), w_out[l].astype(BF16), mod4,
            norm_ffn[l].reshape(1, D_MODEL), wrt, br, n_tok)
        row_tok, dest, blk_expert, n_active = _dispatch(e_idx, rank, cnt[:, 0], n_tok)
        xs = jnp.take(h2, row_tok, axis=0, mode='fill', fill_value=0)
        ys = _experts(xs, blk_expert, n_active, w_exp_gate[l].astype(BF16), w_exp_up[l].astype(BF16),
                      w_exp_down[l].astype(BF16))
        y0 = jnp.take(ys, dest[0], axis=0)
        y1 = jnp.take(ys, dest[1], axis=0)
        res = _combine(xa, y0, y1, w_sel.T, mod4, gain_final, n_tok, last)
        if last:
            out = res
        else:
            xa = res
    return out.reshape(BATCH, SEQ, D_MODEL)
```

```python
import functools
import math

import jax
import jax.numpy as jnp
from jax import lax
from jax.experimental import pallas as pl
from jax.experimental.pallas import tpu as pltpu

F32 = jnp.float32
BF16 = jnp.bfloat16

D_MODEL = 1024
BATCH = 4
SEQ = 4096
DEPTH = 4
GRID_W = 64
CTX_LEN = 256
EPS = 1e-6
N_MOD = 6

HEAD_DIM = 64
N_Q_HEADS = 8
N_KV_HEADS = 2
Q_PER_KV = N_Q_HEADS // N_KV_HEADS
ATTN_BLOCK = 128
ROPE_BASE = 10000.0

FNET_GROUPS = 4
FNET_GROUP_DIM = 128
FNET_WIDTH = FNET_GROUPS * FNET_GROUP_DIM

HYENA_WIDTH = 512
HYENA_ORDER = 2
FILTER_EMB = 33
FILTER_BANDS = (FILTER_EMB - 1) // 2
DECAY_TARGET = 1e-2
FAST_DECAY_PCT = 0.3
SLOW_DECAY_PCT = 1.5

ATTN_WIDTH = N_Q_HEADS * HEAD_DIM
KV_WIDTH = N_KV_HEADS * HEAD_DIM
Q_OFF = 0
K_OFF = Q_OFF + ATTN_WIDTH
V_OFF = K_OFF + KV_WIDTH
F_OFF = V_OFF + KV_WIDTH
H_OFF = F_OFF + FNET_WIDTH
G_OFF = H_OFF + (HYENA_ORDER + 1) * HYENA_WIDTH
IN_WIDTH = G_OFF + 3 * D_MODEL

N_EXPERTS = 16
N_GROUPS = 4
EXPERTS_PER_GROUP = N_EXPERTS // N_GROUPS
TOP_K = 2
EXPERT_FF = 1024

T_LAT = BATCH * SEQ
T_CTX = BATCH * CTX_LEN
T_ALL = T_LAT + T_CTX

PG_OFF = 0
PH_OFF = 3 * D_MODEL
PF_OFF = PH_OFF + (HYENA_ORDER + 1) * HYENA_WIDTH
PQ_OFF = PF_OFF + FNET_WIDTH
PK_OFF = PQ_OFF + ATTN_WIDTH
PV_OFF = PK_OFF + KV_WIDTH

HY_N = 2 * SEQ
HY_N2 = 64
HY_N1 = HY_N // HY_N2
FN_N = 64

TM = 512
EXPERT_BM = 256
VMEM_LIMIT = 52 * 1024 * 1024


def _cparams(sem, vmem=VMEM_LIMIT):
    return pltpu.CompilerParams(dimension_semantics=sem, vmem_limit_bytes=vmem)


def _dot(a, b):
    return jnp.dot(a, b, preferred_element_type=F32)


def _cis(expo, n):
    ang = (2.0 * math.pi / n) * jnp.mod(expo, n).astype(F32)
    return jnp.cos(ang), jnp.sin(ang)


def _real_form(gr, gi):
    return jnp.concatenate([jnp.concatenate([gr, -gi], axis=-1), jnp.concatenate([gi, gr], axis=-1)], axis=-2)


def _dft_mats():
    ar = lambda n: jnp.arange(n, dtype=jnp.int32)
    m = {}
    c, s = _cis(ar(HY_N1)[:, None] * ar(HY_N1 // 2)[None, :], HY_N1)
    m['hy_lead_f'] = _real_form(c, -s).astype(BF16)
    c, s = _cis(ar(HY_N1 // 2)[:, None] * ar(HY_N1)[None, :], HY_N1)
    m['hy_lead_i'] = _real_form(c, s).astype(BF16)
    c, s = _cis(ar(HY_N1)[:, None] * ar(HY_N1)[None, :], HY_N1)
    m['hy_lead_k'] = jnp.concatenate([c, -s], axis=0).astype(BF16)
    a = ar(HY_N1)[:, None, None]
    k2 = ar(HY_N2)[None, :, None]
    n2 = ar(HY_N2)[None, None, :]
    c, s = _cis(n2 * (a + HY_N1 * k2), HY_N)
    m['hy_slab_f'] = _real_form(c, -s).astype(BF16)
    ct = jnp.swapaxes(c, 1, 2) * (1.0 / HY_N)
    st = jnp.swapaxes(s, 1, 2) * (1.0 / HY_N)
    m['hy_slab_i'] = _real_form(ct, st).astype(BF16)
    nc = 2 * CTX_LEN
    c, s = _cis(ar(nc)[:, None] * ar(CTX_LEN)[None, :], nc)
    m['hc_f'] = _real_form(c, -s).astype(BF16)
    c, s = _cis(ar(CTX_LEN)[:, None] * ar(nc)[None, :], nc)
    m['hc_i'] = _real_form(c * (1.0 / nc), s * (1.0 / nc)).astype(BF16)
    c, s = _cis(ar(nc)[:, None] * ar(nc)[None, :], nc)
    m['hc_k'] = jnp.concatenate([c, -s], axis=0).astype(BF16)
    c, s = _cis(ar(FNET_GROUP_DIM)[:, None] * ar(FNET_GROUP_DIM)[None, :], FNET_GROUP_DIM)
    m['fn_chan'] = jnp.concatenate([c, -s], axis=1).astype(BF16)
    c, s = _cis(ar(FN_N)[:, None] * ar(FN_N)[None, :], FN_N)
    m['fn_lead'] = _real_form(c, -s).astype(BF16)
    a = ar(FN_N)[:, None, None]
    k1 = ar(FN_N)[None, :, None]
    n1 = ar(FN_N)[None, None, :]
    scale = 1.0 / math.sqrt(SEQ * FNET_GROUP_DIM)
    c, s = _cis(n1 * (a + FN_N * k1), SEQ)
    m['fn_slab'] = jnp.concatenate([c * scale, s * scale], axis=-1).astype(BF16)
    scale = 1.0 / math.sqrt(CTX_LEN * FNET_GROUP_DIM)
    c, s = _cis(ar(CTX_LEN)[:, None] * ar(CTX_LEN)[None, :], CTX_LEN)
    m['fc'] = jnp.concatenate([c * scale, s * scale], axis=-1).astype(BF16)
    return m


def _mod_kernel(c_ref, w_ref, b_ref, o_ref):
    c = c_ref[...]
    s = c * jax.nn.sigmoid(c)
    o_ref[0] = _dot(s.astype(BF16), w_ref[0].astype(BF16)) + b_ref[0]


def _modulation(c8, w_mod, b_mod):
    tn = 1536
    n = N_MOD * D_MODEL
    return pl.pallas_call(
        _mod_kernel,
        grid=(DEPTH, n // tn),
        in_specs=[pl.BlockSpec((8, D_MODEL), lambda l, j: (0, 0)),
                  pl.BlockSpec((1, D_MODEL, tn), lambda l, j: (l, 0, j)),
                  pl.BlockSpec((1, 1, tn), lambda l, j: (l, 0, j))],
        out_specs=pl.BlockSpec((1, 8, tn), lambda l, j: (l, 0, j)),
        out_shape=jax.ShapeDtypeStruct((DEPTH, 8, n), F32),
        compiler_params=_cparams(("parallel", "parallel")),
        name="adaln_modulation",
    )(c8, w_mod, b_mod.reshape(DEPTH, 1, n))


def _mod_row(tm):
    tiles_per_batch = SEQ // tm
    return lambda i: jnp.minimum(i // tiles_per_batch, BATCH)


def _rms_mod(x, g, sh, sc):
    y = x * lax.rsqrt(jnp.mean(x * x, axis=-1, keepdims=True) + EPS)
    return (y * g) * (1.0 + sc) + sh


def _norm_proj_kernel(x_ref, g_ref, sh_ref, sc_ref, w_ref, o_ref, h_scr):
    @pl.when(pl.program_id(1) == 0)
    def _():
        h_scr[...] = _rms_mod(x_ref[...], g_ref[...], sh_ref[0, 0], sc_ref[0, 0]).astype(BF16)

    o_ref[...] = _dot(h_scr[...], w_ref[...])


def _norm_proj(x, gain, mod4, w, n_tok):
    tm = TM
    n_out = w.shape[1]
    tn = n_out // 2
    row = _mod_row(tm)
    return pl.pallas_call(
        _norm_proj_kernel,
        grid=(n_tok // tm, n_out // tn),
        in_specs=[pl.BlockSpec((tm, D_MODEL), lambda i, j: (i, 0)),
                  pl.BlockSpec((1, D_MODEL), lambda i, j: (0, 0)),
                  pl.BlockSpec((1, 1, 1, D_MODEL), lambda i, j: (row(i), 0, 0, 0)),
                  pl.BlockSpec((1, 1, 1, D_MODEL), lambda i, j: (row(i), 1, 0, 0)),
                  pl.BlockSpec((D_MODEL, tn), lambda i, j: (0, j))],
        out_specs=pl.BlockSpec((tm, tn), lambda i, j: (i, j)),
        out_shape=jax.ShapeDtypeStruct((T_ALL, n_out), F32),
        scratch_shapes=[pltpu.VMEM((tm, D_MODEL), BF16)],
        compiler_params=_cparams(("parallel", "arbitrary")),
        name="norm_in_proj",
    )(x, gain, mod4, mod4, w)


def _softmax_pv(qh, k_parts, v_parts, masks, sink):
    nt = (((1,), (1,)), ((), ()))
    scores = []
    for kp, mk in zip(k_parts, masks):
        s = lax.dot_general(qh, kp, nt, preferred_element_type=F32)
        if mk is not None:
            s = jnp.where(mk, s, -1e30)
        scores.append(s)
    m = sink
    for s in scores:
        m = jnp.maximum(m, jnp.max(s, axis=-1, keepdims=True))
    es = [jnp.exp(s - m) for s in scores]
    den = jnp.exp(sink - m)
    for e in es:
        den = den + jnp.sum(e, axis=-1, keepdims=True)
    inv = 1.0 / den
    o = None
    for e, vp in zip(es, v_parts):
        t = _dot((e * inv).astype(BF16), vp)
        o = t if o is None else o + t
    return o


def _attn_kernel(sink_ref, q_ref, km_ref, k0_ref, kp_ref, vm_ref, v0_ref, vp_ref, kc_ref, vc_ref,
                 cos_ref, sin_ref, o_ref, *, nb):
    n = pl.program_id(1)
    blk = ATTN_BLOCK
    lane = lax.broadcasted_iota(jnp.int32, (blk, 128), 1)
    first = (lane % 32) < 16

    def rope(x, blk_idx):
        r0 = pl.multiple_of(blk_idx * blk, blk)
        c = cos_ref[pl.ds(r0, blk), :]
        s = sin_ref[pl.ds(r0, blk), :]
        sw = jnp.where(first, pltpu.roll(x, 112, 1), pltpu.roll(x, 16, 1))
        return x * c + sw * s

    nm = jnp.maximum(n - 1, 0)
    npl = jnp.minimum(n + 1, nb - 1)
    kall = jnp.concatenate([rope(km_ref[...], nm), rope(k0_ref[...], n), rope(kp_ref[...], npl), kc_ref[...]],
                           axis=0)
    vall = jnp.concatenate([vm_ref[...], v0_ref[...], vp_ref[...], vc_ref[...]], axis=0)
    nk = kall.shape[0]
    kswap = pltpu.roll(kall, HEAD_DIM, 1)
    vswap = pltpu.roll(vall, HEAD_DIM, 1)
    lo = lax.broadcasted_iota(jnp.int32, (nk, 128), 1) < HEAD_DIM

    r = lax.broadcasted_iota(jnp.int32, (2 * blk, blk), 0) % blk
    cidx = lax.broadcasted_iota(jnp.int32, (2 * blk, blk), 1)
    ok_prev = jnp.where(cidx >= r, (n > 0).astype(jnp.int32), 0) > 0
    ok_next = jnp.where(cidx <= r, (n < nb - 1).astype(jnp.int32), 0) > 0
    top_rows = lax.broadcasted_iota(jnp.int32, (2 * blk, 1), 0) < blk
    neg = jnp.float32(-1e30)

    scale = HEAD_DIM ** -0.5
    q2 = [(rope(q_ref[:, p * 128:(p + 1) * 128], n) * scale).astype(BF16) for p in range(N_Q_HEADS // 2)]
    nt = (((1,), (1,)), ((), ()))
    for h in range(N_KV_HEADS):
        ka, kb = (kall, kswap) if h == 0 else (kswap, kall)
        va, vb = (vall, vswap) if h == 0 else (vswap, vall)
        kbd = jnp.concatenate([jnp.where(lo, ka, 0.0), jnp.where(lo, 0.0, kb)], axis=0).astype(BF16)
        vbd = jnp.concatenate([jnp.where(lo, va, 0.0), jnp.where(lo, 0.0, vb)], axis=0).astype(BF16)
        q4 = jnp.concatenate([q2[2 * h], q2[2 * h + 1]], axis=0)
        s = lax.dot_general(q4, kbd, nt, preferred_element_type=F32)
        probs = []
        for c in range(2):
            base = c * nk
            sink = jnp.where(top_rows, sink_ref[Q_PER_KV * h + c], sink_ref[Q_PER_KV * h + 2 + c])
            parts = [jnp.where(ok_prev, s[:, base:base + blk], neg),
                     s[:, base + blk:base + 2 * blk],
                     jnp.where(ok_next, s[:, base + 2 * blk:base + 3 * blk], neg),
                     s[:, base + 3 * blk:base + nk]]
            m = sink
            for part in parts:
                m = jnp.maximum(m, jnp.max(part, axis=-1, keepdims=True))
            es = [jnp.exp(part - m) for part in parts]
            den = jnp.exp(sink - m)
            for e in es:
                den = den + jnp.sum(e, axis=-1, keepdims=True)
            inv = 1.0 / den
            probs += [(e * inv).astype(BF16) for e in es]
        o = _dot(jnp.concatenate(probs, axis=1), vbd)
        w0 = h * Q_PER_KV * HEAD_DIM
        o_ref[:, w0:w0 + 128] = o[:blk].astype(o_ref.dtype)
        o_ref[:, w0 + 128:w0 + 256] = o[blk:].astype(o_ref.dtype)


def _ctx_attn_kernel(sink_ref, q_ref, kc_ref, vc_ref, o_ref):
    kc = kc_ref[...].astype(BF16)
    vc = vc_ref[...].astype(BF16)
    scale = HEAD_DIM ** -0.5
    outs = []
    for pair in range(N_Q_HEADS // 2):
        q2 = (q_ref[:, pair * 128:(pair + 1) * 128] * scale).astype(BF16)
        for sub in range(2):
            head = 2 * pair + sub
            kvh = head // Q_PER_KV
            sl = slice(kvh * HEAD_DIM, (kvh + 1) * HEAD_DIM)
            qh = q2[:, sub * HEAD_DIM:(sub + 1) * HEAD_DIM]
            outs.append(_softmax_pv(qh, [kc[:, sl]], [vc[:, sl]], [None], sink_ref[head]))
    o_ref[...] = jnp.concatenate(outs, axis=-1).astype(o_ref.dtype)


def _attention(p, sink, cos_t, sin_t, with_ctx):
    blk = ATTN_BLOCK
    nb = SEQ // blk
    qc, kcol, vcol = PQ_OFF // ATTN_WIDTH, PK_OFF // KV_WIDTH, PV_OFF // KV_WIDTH
    ctx_blk = T_LAT // CTX_LEN
    smem = pl.BlockSpec(memory_space=pltpu.SMEM)

    def kv_spec(col, d):
        return pl.BlockSpec((blk, KV_WIDTH),
                            lambda b, n: (b * nb + jnp.clip(n + d, 0, nb - 1), col))

    ya = pl.pallas_call(
        functools.partial(_attn_kernel, nb=nb),
        grid=(BATCH, nb),
        in_specs=[smem,
                  pl.BlockSpec((blk, ATTN_WIDTH), lambda b, n: (b * nb + n, qc)),
                  kv_spec(kcol, -1), kv_spec(kcol, 0), kv_spec(kcol, 1),
                  kv_spec(vcol, -1), kv_spec(vcol, 0), kv_spec(vcol, 1),
                  pl.BlockSpec((CTX_LEN, KV_WIDTH), lambda b, n: (ctx_blk + b, kcol)),
                  pl.BlockSpec((CTX_LEN, KV_WIDTH), lambda b, n: (ctx_blk + b, vcol)),
                  pl.BlockSpec((SEQ, KV_WIDTH), lambda b, n: (0, 0)),
                  pl.BlockSpec((SEQ, KV_WIDTH), lambda b, n: (0, 0))],
        out_specs=pl.BlockSpec((blk, ATTN_WIDTH), lambda b, n: (b * nb + n, 0)),
        out_shape=jax.ShapeDtypeStruct((T_ALL, ATTN_WIDTH), BF16),
        compiler_params=_cparams(("parallel", "parallel")),
        name="banded_attention",
    )(sink, p, p, p, p, p, p, p, p, p, cos_t, sin_t)
    if not with_ctx:
        return ya
    cb = CTX_LEN // blk
    lat_blk = T_LAT // blk

    def alias_kernel(sink_ref, q_ref, kc_ref, vc_ref, ya_in_ref, o_ref):
        del ya_in_ref
        _ctx_attn_kernel(sink_ref, q_ref, kc_ref, vc_ref, o_ref)

    return pl.pallas_call(
        alias_kernel,
        grid=(BATCH, cb),
        in_specs=[smem,
                  pl.BlockSpec((blk, ATTN_WIDTH), lambda b, n: (lat_blk + b * cb + n, qc)),
                  pl.BlockSpec((CTX_LEN, KV_WIDTH), lambda b, n: (ctx_blk + b, kcol)),
                  pl.BlockSpec((CTX_LEN, KV_WIDTH), lambda b, n: (ctx_blk + b, vcol)),
                  pl.BlockSpec(memory_space=pl.ANY)],
        out_specs=pl.BlockSpec((blk, ATTN_WIDTH), lambda b, n: (lat_blk + b * cb + n, 0)),
        out_shape=jax.ShapeDtypeStruct((T_ALL, ATTN_WIDTH), BF16),
        input_output_aliases={4: 0},
        compiler_params=_cparams(("parallel", "parallel")),
        name="context_attention",
    )(sink, p, p, p, ya)


def _rope_tables():
    n_freq = HEAD_DIM // 4
    freqs = ROPE_BASE ** (-jnp.arange(n_freq, dtype=F32) / n_freq)
    t = jnp.arange(SEQ, dtype=jnp.int32)
    rows = (t // GRID_W).astype(F32)[:, None] * freqs
    cols = (t % GRID_W).astype(F32)[:, None] * freqs
    cos_h = jnp.concatenate([jnp.cos(rows), jnp.cos(rows), jnp.cos(cols), jnp.cos(cols)], axis=-1)
    sin_h = jnp.concatenate([-jnp.sin(rows), jnp.sin(rows), -jnp.sin(cols), jnp.sin(cols)], axis=-1)
    return jnp.tile(cos_h, (1, 2)), jnp.tile(sin_h, (1, 2))


def _lead_kernel(*refs, n_in, cmul, epi):
    m_ref = refs[0]
    x_refs = refs[1:1 + n_in]
    pos = 1 + n_in
    xs = []
    for r in x_refs:
        v = r[...]
        xs.append(v.reshape(-1, v.shape[-1]))
    x = xs[0] if n_in == 1 else jnp.concatenate(xs, axis=0)
    if cmul:
        k = refs[pos][...]
        pos += 1
        k = k.reshape(-1, k.shape[-1])
        half = x.shape[0] // 2
        xr, xi, kr, ki = x[:half], x[half:], k[:half], k[half:]
        x = jnp.concatenate([xr * kr - xi * ki, xr * ki + xi * kr], axis=0)
    res = _dot(m_ref[...], x.astype(BF16))
    if epi:
        g_ref, y_ref, b_ref = refs[pos:pos + 3]
        pos += 3
        g = g_ref[...]
        y = y_ref[...]
        res = g.reshape(-1, g.shape[-1]) * (res + y.reshape(-1, y.shape[-1]) * b_ref[...])
    o_ref = refs[pos]
    o_ref[...] = res.reshape(o_ref.shape).astype(o_ref.dtype)


def _lead(mat, xs, x_specs, grid, out_shape, out_spec, *, kspec=None, epi=None, alias_to=None, name):
    ins = [mat] + list(xs)
    specs = [pl.BlockSpec(mat.shape, lambda *a: (0, 0))] + list(x_specs)
    if kspec is not None:
        ins.append(kspec[0])
        specs.append(kspec[1])
    if epi is not None:
        for arr, sp in epi:
            ins.append(arr)
            specs.append(sp)
    kern = functools.partial(_lead_kernel, n_in=len(xs), cmul=kspec is not None, epi=epi is not None)
    aliases = {}
    if alias_to is not None:
        aliases = {len(ins): 0}
        ins.append(alias_to)
        specs.append(pl.BlockSpec(memory_space=pl.ANY))
        inner = kern

        def kern(*refs):
            inner(*refs[:-2], refs[-1])

    return pl.pallas_call(
        kern, grid=grid, in_specs=specs, out_specs=out_spec, out_shape=out_shape,
        input_output_aliases=aliases,
        compiler_params=_cparams(("parallel",) * len(grid)), name=name,
    )(*ins)


def _slab_kernel(*refs, sb, mul, inv, real_out, transpose_out):
    mf_ref, x_ref = refs[0], refs[1]
    pos = 2
    if mul:
        k_ref = refs[pos]
        pos += 1
    if inv:
        mi_ref = refs[pos]
        pos += 1
    o_ref = refs[pos]
    r = x_ref.shape[3]
    for s in range(sb):
        x = jnp.concatenate([x_ref[0, 0, s], x_ref[0, 1, s]], axis=0).astype(BF16)
        y = _dot(mf_ref[s], x)
        if mul:
            yr, yi = y[:r], y[r:]
            kr, ki = k_ref[0, 0, s], k_ref[0, 1, s]
            y = jnp.concatenate([yr * kr - yi * ki, yr * ki + yi * kr], axis=0)
        if inv:
            y = _dot(mi_ref[s], y.astype(BF16))
        if real_out:
            if transpose_out:
                o_ref[0, :, s, :] = y
            else:
                o_ref[0, s] = y
        else:
            o_ref[0, 0, s] = y[:r]
            o_ref[0, 1, s] = y[r:]


def _slab(mf, x, *, kspec=None, korder=0, mi=None, real_out=False, name):
    g, _, ns, r, c = x.shape
    sb = 8
    ins = [mf, x]
    specs = [pl.BlockSpec((sb,) + mf.shape[1:], lambda b, j: (j, 0, 0)),
             pl.BlockSpec((1, 2, sb, r, c), lambda b, j: (b, 0, j, 0, 0))]
    if kspec is not None:
        ins.append(kspec)
        specs.append(pl.BlockSpec((1, 2, sb, r, c), lambda b, j: (korder, 0, j, 0, 0)))
    if mi is not None:
        ins.append(mi)
        specs.append(pl.BlockSpec((sb,) + mi.shape[1:], lambda b, j: (j, 0, 0)))
    if real_out:
        out_shape = jax.ShapeDtypeStruct((g, ns, mf.shape[1], c), F32)
        out_spec = pl.BlockSpec((1, sb, mf.shape[1], c), lambda b, j: (b, j, 0, 0))
    else:
        out_shape = jax.ShapeDtypeStruct((g, 2, ns, r, c), F32)
        out_spec = pl.BlockSpec((1, 2, sb, r, c), lambda b, j: (b, 0, j, 0, 0))
    kern = functools.partial(_slab_kernel, sb=sb, mul=kspec is not None, inv=mi is not None,
                             real_out=real_out, transpose_out=False)
    return pl.pallas_call(
        kern, grid=(g, ns // sb), in_specs=specs, out_specs=out_spec, out_shape=out_shape,
        compiler_params=_cparams(("parallel", "parallel")), name=name,
    )(*ins)


def _chan_dft_kernel(u_ref, m_ref, zr_ref, zi_ref):
    gd = FNET_GROUP_DIM
    m = m_ref[...]
    for g in range(FNET_GROUPS):
        z = _dot(u_ref[:, g * gd:(g + 1) * gd].astype(BF16), m)
        zr_ref[:, g * gd:(g + 1) * gd] = z[:, :gd]
        zi_ref[:, g * gd:(g + 1) * gd] = z[:, gd:]


def _fourier_mix(p, mats, n_tok, with_ctx):
    tm = TM
    fcol = PF_OFF // FNET_WIDTH
    zr, zi = pl.pallas_call(
        _chan_dft_kernel,
        grid=(n_tok // tm,),
        in_specs=[pl.BlockSpec((tm, FNET_WIDTH), lambda i: (i, fcol)),
                  pl.BlockSpec(mats['fn_chan'].shape, lambda i: (0, 0))],
        out_specs=[pl.BlockSpec((tm, FNET_WIDTH), lambda i: (i, 0))] * 2,
        out_shape=[jax.ShapeDtypeStruct((T_ALL, FNET_WIDTH), F32)] * 2,
        compiler_params=_cparams(("parallel",)),
        name="fnet_channel_dft",
    )(p, mats['fn_chan'])
    lanes = FN_N * FNET_WIDTH
    tl = 4096
    zr2 = zr.reshape(T_ALL // FN_N, lanes)
    zi2 = zi.reshape(T_ALL // FN_N, lanes)
    xspec = pl.BlockSpec((FN_N, tl), lambda b, j: (b, j))
    a = _lead(mats['fn_lead'], [zr2, zi2], [xspec, xspec], (BATCH, lanes // tl),
              jax.ShapeDtypeStruct((BATCH, 2 * FN_N, lanes), F32),
              pl.BlockSpec((1, 2 * FN_N, tl), lambda b, j: (b, 0, j)), name="fnet_lead")
    a = a.reshape(BATCH, 2, FN_N, FN_N, FNET_WIDTH)
    y2 = _slab(mats['fn_slab'], a, real_out=True, name="fnet_slab")
    yf_lat = jnp.swapaxes(y2, 1, 2).reshape(T_LAT, FNET_WIDTH)
    if not with_ctx:
        return yf_lat
    ctx_blk = T_LAT // CTX_LEN
    cspec = pl.BlockSpec((CTX_LEN, FNET_WIDTH), lambda b: (ctx_blk + b, 0))
    yf_ctx = _lead(mats['fc'], [zr, zi], [cspec, cspec], (BATCH,),
                   jax.ShapeDtypeStruct((T_CTX, FNET_WIDTH), F32),
                   pl.BlockSpec((CTX_LEN, FNET_WIDTH), lambda b: (b, 0)), name="fnet_ctx")
    return jnp.concatenate([yf_lat, yf_ctx], axis=0)


def _short_conv_kernel(u_ref, w_ref, b_ref, o_ref, *, rows, chunk):
    w0 = w_ref[0:1, :]
    w1 = w_ref[1:2, :]
    w2 = w_ref[2:3, :]
    bias = b_ref[...]
    width = u_ref.shape[-1]
    ridx = lax.broadcasted_iota(jnp.int32, (chunk, width), 0)
    n_chunks = rows // chunk
    for ci in range(n_chunks):
        r0 = ci * chunk
        cur = u_ref[r0:r0 + chunk, :]
        if ci > 0:
            prev_row = u_ref[r0 - 8:r0, :][7:8, :]
        else:
            prev_row = jnp.zeros((1, width), F32)
        if ci < n_chunks - 1:
            next_row = u_ref[r0 + chunk:r0 + chunk + 8, :][0:1, :]
        else:
            next_row = jnp.zeros((1, width), F32)
        up = jnp.where(ridx == 0, prev_row, pltpu.roll(cur, 1, 0))
        dn = jnp.where(ridx == chunk - 1, next_row, pltpu.roll(cur, chunk - 1, 0))
        o_ref[0, r0:r0 + chunk, :] = up * w0 + cur * w1 + dn * w2 + bias


def _short_conv(p, conv_w, conv_b, with_ctx):
    cw = 256
    hw = (HYENA_ORDER + 1) * HYENA_WIDTH
    ncol = hw // cw
    per = HYENA_WIDTH // cw
    col0 = PH_OFF // cw
    out_shape = jax.ShapeDtypeStruct((HYENA_ORDER + 1, T_ALL, HYENA_WIDTH), F32)
    b2 = conv_b.reshape(1, hw)

    def call(rows, blk0, alias):
        kern = functools.partial(_short_conv_kernel, rows=rows, chunk=min(rows, 256))
        ins = [p, conv_w, b2]
        specs = [pl.BlockSpec((rows, cw), lambda b, j: (blk0 + b, col0 + j)),
                 pl.BlockSpec((3, cw), lambda b, j: (0, j)),
                 pl.BlockSpec((1, cw), lambda b, j: (0, j))]
        aliases = {}
        if alias is not None:
            ins.append(alias)
            specs.append(pl.BlockSpec(memory_space=pl.ANY))
            aliases = {3: 0}
            inner = kern

            def kern(u_ref, w_ref, b_ref, a_ref, o_ref):
                del a_ref
                inner(u_ref, w_ref, b_ref, o_ref)

        return pl.pallas_call(
            kern, grid=(BATCH, ncol), in_specs=specs,
            out_specs=pl.BlockSpec((1, rows, cw), lambda b, j: (j // per, blk0 + b, j % per)),
            out_shape=out_shape, input_output_aliases=aliases,
            compiler_params=_cparams(("parallel", "parallel")), name="hyena_short_conv",
        )(*ins)

    z3 = call(SEQ, 0, None)
    if with_ctx:
        z3 = call(CTX_LEN, T_LAT // CTX_LEN, z3)
    return z3


def _hyena_filters(n, w1, b1, freq, w2, b2, w3):
    pos = jnp.arange(n, dtype=F32)
    t = pos / max(n - 1, 1)
    omega = 2.0 * math.pi * pos / n
    bands = jnp.linspace(1e-4, FILTER_BANDS - 1, FILTER_BANDS, dtype=F32)
    feats = jnp.concatenate([t[:, None], jnp.cos(omega[:, None] * bands), -jnp.sin(omega[:, None] * bands)], axis=-1)
    h = jnp.sin(freq * (feats @ w1 + b1))
    h = jnp.sin(freq * (h @ w2 + b2))
    h = (h @ w3).reshape(n, HYENA_ORDER, 2, HYENA_WIDTH)
    deltas = jnp.abs(jnp.linspace(math.log(DECAY_TARGET) / SLOW_DECAY_PCT, math.log(DECAY_TARGET) / FAST_DECAY_PCT,
                                  HYENA_WIDTH, dtype=F32))
    h = h * jnp.exp(-t[:, None] * deltas)[:, None, None, :]
    fwd, bwd = h[:, :, 0], h[:, :, 1]
    k = jnp.concatenate([fwd, jnp.zeros_like(fwd[:1]), bwd[:0:-1]], axis=0)
    return k / jnp.sum(jnp.abs(k), axis=0, keepdims=True)


def _hyena_mix(z3, filt, hyena_bias, mats, with_ctx):
    c = HYENA_WIDTH
    lanes = HY_N2 * c
    tl = 2048
    half = HY_N1 // 2
    pairs = BATCH // 2
    k_lat = _hyena_filters(SEQ, *filt)
    k2d = jnp.transpose(k_lat, (1, 0, 2)).reshape(HYENA_ORDER, HY_N1, lanes)
    ka = _lead(mats['hy_lead_k'], [k2d], [pl.BlockSpec((1, HY_N1, tl), lambda o, j: (o, 0, j))],
               (HYENA_ORDER, lanes // tl), jax.ShapeDtypeStruct((HYENA_ORDER, 2 * HY_N1, lanes), F32),
               pl.BlockSpec((1, 2 * HY_N1, tl), lambda o, j: (o, 0, j)), name="hyena_filter_lead")
    kspec = _slab(mats['hy_slab_f'], ka.reshape(HYENA_ORDER, 2, HY_N1, HY_N2, c), name="hyena_filter_slab")
    if with_ctx:
        nc = 2 * CTX_LEN
        k_ctx = jnp.transpose(_hyena_filters(CTX_LEN, *filt), (1, 0, 2))
        kspec_c = _lead(mats['hc_k'], [k_ctx], [pl.BlockSpec((1, nc, c), lambda o: (o, 0, 0))],
                        (HYENA_ORDER,), jax.ShapeDtypeStruct((HYENA_ORDER, 2 * nc, c), F32),
                        pl.BlockSpec((1, 2 * nc, c), lambda o: (o, 0, 0)), name="hyena_ctx_filter")
    bias_l = jnp.tile(hyena_bias, (1, HY_N2)).reshape(HYENA_ORDER, 1, lanes)
    bias_c = hyena_bias.reshape(HYENA_ORDER, 1, c)

    z2d = z3.reshape(HYENA_ORDER + 1, T_ALL // HY_N2, lanes)
    y = None
    for o in range(HYENA_ORDER):
        if y is None:
            xin, xspec = z2d, pl.BlockSpec((1, 2 * half, tl), lambda b, j: (0, b, j))
        else:
            xin, xspec = y.reshape(T_ALL // HY_N2, lanes), pl.BlockSpec((2 * half, tl), lambda b, j: (b, j))
        a = _lead(mats['hy_lead_f'], [xin], [xspec], (pairs, lanes // tl),
                  jax.ShapeDtypeStruct((pairs, 2 * HY_N1, lanes), F32),
                  pl.BlockSpec((1, 2 * HY_N1, tl), lambda b, j: (b, 0, j)), name="hyena_lead_fwd")
        bq = _slab(mats['hy_slab_f'], a.reshape(pairs, 2, HY_N1, HY_N2, c), kspec=kspec, korder=o,
                   mi=mats['hy_slab_i'], name="hyena_slab")
        bq = bq.reshape(pairs, 2 * HY_N1, lanes)
        gate_spec = pl.BlockSpec((1, 2 * half, tl), lambda b, j, o=o: (o + 1, b, j))
        ynew = _lead(mats['hy_lead_i'], [bq], [pl.BlockSpec((1, 2 * HY_N1, tl), lambda b, j: (b, 0, j))],
                     (pairs, lanes // tl), jax.ShapeDtypeStruct((T_ALL // HY_N2, lanes), F32),
                     pl.BlockSpec((2 * half, tl), lambda b, j: (b, j)),
                     epi=[(z2d, gate_spec), (xin, xspec),
                          (bias_l, pl.BlockSpec((1, 1, tl), lambda b, j, o=o: (o, 0, j)))],
                     name="hyena_lead_inv")
        ynew = ynew.reshape(T_ALL, c)
        if with_ctx:
            nc = 2 * CTX_LEN
            blk0 = T_LAT // nc
            if y is None:
                cin, cspec_in = z3, pl.BlockSpec((1, nc, c), lambda b: (0, blk0 + b, 0))
            else:
                cin, cspec_in = y, pl.BlockSpec((nc, c), lambda b: (blk0 + b, 0))
            xc = _lead(mats['hc_f'], [cin], [cspec_in], (pairs,),
                       jax.ShapeDtypeStruct((pairs, 2 * nc, c), F32),
                       pl.BlockSpec((1, 2 * nc, c), lambda b: (b, 0, 0)), name="hyena_ctx_fwd")
            ynew = _lead(mats['hc_i'], [xc], [pl.BlockSpec((1, 2 * nc, c), lambda b: (b, 0, 0))],
                         (pairs,), jax.ShapeDtypeStruct((T_ALL, c), F32),
                         pl.BlockSpec((nc, c), lambda b: (blk0 + b, 0)),
                         kspec=(kspec_c, pl.BlockSpec((1, 2 * nc, c), lambda b, o=o: (o, 0, 0))),
                         epi=[(z3, pl.BlockSpec((1, nc, c), lambda b, o=o: (o + 1, blk0 + b, 0))),
                              (cin, cspec_in),
                              (bias_c, pl.BlockSpec((1, 1, c), lambda b, o=o: (o, 0, 0)))],
                         alias_to=ynew, name="hyena_ctx_inv")
        y = ynew
    return y


def _route_tile(lt, br, base, tri):
    tm = lt.shape[1]
    aff = jax.nn.sigmoid(lt)
    biased = aff + br
    b = [biased[e:e + 1, :] for e in range(N_EXPERTS)]
    a = [aff[e:e + 1, :] for e in range(N_EXPERTS)]
    epg = EXPERTS_PER_GROUP
    scores = []
    for g in range(N_GROUPS):
        x0, x1, x2, x3 = b[epg * g:epg * g + epg]
        s1, t1 = jnp.maximum(x0, x1), jnp.minimum(x0, x1)
        s2, t2 = jnp.maximum(x2, x3), jnp.minimum(x2, x3)
        scores.append(jnp.maximum(s1, s2) + jnp.maximum(jnp.minimum(s1, s2), jnp.maximum(t1, t2)))
    best = scores[0]
    gsel = jnp.zeros((1, tm), jnp.int32)
    for g in range(1, N_GROUPS):
        gsel = jnp.where(scores[g] > best, g, gsel)
        best = jnp.maximum(best, scores[g])

    def pick(rows, j):
        out = rows[j]
        for g in range(1, N_GROUPS):
            out = jnp.where(gsel == g, rows[epg * g + j], out)
        return out

    v = [pick(b, j) for j in range(epg)]
    av = [pick(a, j) for j in range(epg)]
    i1 = jnp.zeros((1, tm), jnp.int32)
    m1 = v[0]
    for j in range(1, epg):
        i1 = jnp.where(v[j] > m1, j, i1)
        m1 = jnp.maximum(m1, v[j])
    neg = jnp.float32(-3.0e38)
    i2 = jnp.zeros((1, tm), jnp.int32)
    m2 = jnp.full((1, tm), neg, F32)
    for j in range(epg):
        cand = jnp.where(i1 == j, neg, v[j])
        take = cand > m2
        i2 = jnp.where(take, j, i2)
        m2 = jnp.where(take, cand, m2)

    def sel(rows, idx):
        out = rows[0]
        for j in range(1, epg):
            out = jnp.where(idx == j, rows[j], out)
        return out

    a1, a2 = sel(av, i1), sel(av, i2)
    den = a1 + a2
    e1 = gsel * epg + i1
    e2 = gsel * epg + i2
    eio = lax.broadcasted_iota(jnp.int32, (N_EXPERTS, tm), 0)
    oh1 = jnp.where(eio == e1, 1.0, 0.0)
    oh2 = jnp.where(eio == e2, 1.0, 0.0)
    oh = oh1 + oh2
    tot = base + _dot(oh.astype(BF16), tri)
    r1 = jnp.sum(oh1 * tot, axis=0, keepdims=True)
    r2 = jnp.sum(oh2 * tot, axis=0, keepdims=True)
    new_base = base + jnp.sum(oh, axis=1, keepdims=True)
    return (e1, e2), (a1 / den, a2 / den), (r1.astype(jnp.int32), r2.astype(jnp.int32)), new_base


def _merge_kernel(x_ref, ya_ref, yf_ref, yh_ref, gt_ref, wa_ref, wf_ref, wh_ref, wo_ref, g1_ref,
                  gn_ref, sh_ref, sc_ref, wrt_ref, br_ref, xo_ref, h2_ref, e_ref, w_ref, r_ref, cnt_ref):
    d = D_MODEL
    merged = jax.nn.sigmoid(gt_ref[:, 0:d]) * _dot(ya_ref[...], wa_ref[...])
    merged += jax.nn.sigmoid(gt_ref[:, d:2 * d]) * _dot(yf_ref[...].astype(BF16), wf_ref[...])
    merged += jax.nn.sigmoid(gt_ref[:, 2 * d:3 * d]) * _dot(yh_ref[...].astype(BF16), wh_ref[...])
    xn = x_ref[...] + g1_ref[0, 0] * _dot(merged.astype(BF16), wo_ref[...])
    xo_ref[...] = xn
    h2f = _rms_mod(xn, gn_ref[...], sh_ref[0, 0], sc_ref[0, 0])
    h2_ref[...] = h2f
    h2 = h2f.astype(BF16)

    @pl.when(pl.program_id(0) == 0)
    def _():
        cnt_ref[...] = jnp.zeros_like(cnt_ref)

    tm = h2.shape[0]
    lt = lax.dot_general(wrt_ref[...], h2, (((1,), (1,)), ((), ())), preferred_element_type=F32)
    tri = jnp.where(lax.broadcasted_iota(jnp.int32, (tm, tm), 0) < lax.broadcasted_iota(jnp.int32, (tm, tm), 1),
                    1.0, 0.0).astype(BF16)
    es, ws, rs, new_base = _route_tile(lt, br_ref[...], cnt_ref[:, 0:1], tri)
    e_ref[0:1, :], e_ref[1:2, :] = es
    w_ref[0:1, :], w_ref[1:2, :] = ws
    r_ref[0:1, :], r_ref[1:2, :] = rs
    cnt_ref[...] = jnp.broadcast_to(new_base, cnt_ref.shape)


def _merge(x, ya, yf, yh, p, wa, wf, wh, wo, mod4, gain2, wrt, br, n_tok):
    tm = 256
    row = _mod_row(tm)
    full = lambda a: pl.BlockSpec(a.shape, lambda i: (0,) * a.ndim)
    modspec = lambda k: pl.BlockSpec((1, 1, 1, D_MODEL), lambda i: (row(i), k, 0, 0))
    tok = lambda w: pl.BlockSpec((tm, w), lambda i: (i, 0))
    lane = pl.BlockSpec((TOP_K, tm), lambda i: (0, i))
    return pl.pallas_call(
        _merge_kernel,
        grid=(n_tok // tm,),
        in_specs=[tok(D_MODEL), tok(ATTN_WIDTH), tok(FNET_WIDTH), tok(HYENA_WIDTH),
                  pl.BlockSpec((tm, 3 * D_MODEL), lambda i: (i, PG_OFF // (3 * D_MODEL))),
                  full(wa), full(wf), full(wh), full(wo), modspec(2), full(gain2), modspec(3), modspec(4),
                  full(wrt), full(br)],
        out_specs=[tok(D_MODEL), tok(D_MODEL), lane, lane, lane,
                   pl.BlockSpec((N_EXPERTS, 128), lambda i: (0, 0))],
        out_shape=[jax.ShapeDtypeStruct((n_tok, D_MODEL), F32),
                   jax.ShapeDtypeStruct((n_tok, D_MODEL), F32),
                   jax.ShapeDtypeStruct((TOP_K, n_tok), jnp.int32),
                   jax.ShapeDtypeStruct((TOP_K, n_tok), F32),
                   jax.ShapeDtypeStruct((TOP_K, n_tok), jnp.int32),
                   jax.ShapeDtypeStruct((N_EXPERTS, 128), F32)],
        compiler_params=_cparams(("arbitrary",)),
        name="merge_out_norm_route",
    )(x, ya, yf, yh, p, wa, wf, wh, wo, mod4, gain2, mod4, mod4, wrt, br)


def _expert_kernel(be_ref, na_ref, x_ref, wg_ref, wu_ref, wd_ref, o_ref):
    del be_ref
    i = pl.program_id(0)

    @pl.when(i < na_ref[0])
    def _():
        x = x_ref[...].astype(BF16)
        g = _dot(x, wg_ref[0])
        u = _dot(x, wu_ref[0])
        h = (g * jax.nn.sigmoid(g)) * u
        o_ref[...] = _dot(h.astype(BF16), wd_ref[0])

    @pl.when(i >= na_ref[0])
    def _():
        o_ref[...] = jnp.zeros_like(o_ref)


def _experts(xs, blk_expert, n_active, wg, wu, wd):
    bm = EXPERT_BM
    n_blk = xs.shape[0] // bm
    wspec = lambda k, n: pl.BlockSpec((1, k, n), lambda i, be, na: (be[i], 0, 0))
    return pl.pallas_call(
        _expert_kernel,
        grid_spec=pltpu.PrefetchScalarGridSpec(
            num_scalar_prefetch=2,
            grid=(n_blk,),
            in_specs=[pl.BlockSpec((bm, D_MODEL), lambda i, be, na: (i, 0)),
                      wspec(D_MODEL, EXPERT_FF), wspec(D_MODEL, EXPERT_FF), wspec(EXPERT_FF, D_MODEL)],
            out_specs=pl.BlockSpec((bm, D_MODEL), lambda i, be, na: (i, 0))),
        out_shape=jax.ShapeDtypeStruct((xs.shape[0], D_MODEL), F32),
        compiler_params=_cparams(("arbitrary",)),
        name="moe_experts",
    )(blk_expert, n_active, xs, wg, wu, wd)


def _dispatch(e_idx, rank, counts, n_tok):
    bm = EXPERT_BM
    counts = counts.astype(jnp.int32)
    padded = (counts + bm - 1) // bm * bm
    pad_end = jnp.cumsum(padded)
    pad_start = pad_end - padded
    experts = jnp.arange(N_EXPERTS, dtype=jnp.int32)
    start = jnp.sum(jnp.where(e_idx[..., None] == experts, pad_start, 0), axis=-1)
    dest = start + rank
    n_rows = -(-(n_tok * TOP_K) // bm) * bm + N_EXPERTS * bm
    n_blk = n_rows // bm
    tok = jnp.tile(jnp.arange(n_tok, dtype=jnp.int32), TOP_K)
    row_tok = jnp.zeros((n_rows,), jnp.int32).at[dest.reshape(-1)].set(tok)
    blk_start = jnp.arange(n_blk, dtype=jnp.int32) * bm
    blk_expert = jnp.minimum(jnp.sum((blk_start[:, None] >= pad_end[None, :]).astype(jnp.int32), axis=-1),
                             N_EXPERTS - 1)
    n_active = (pad_end[-1] // bm).astype(jnp.int32).reshape(1)
    return row_tok, dest, blk_expert, n_active


def _combine_kernel(x_ref, y0_ref, y1_ref, w_ref, g2_ref, gf_ref, o_ref, *, final):
    w = w_ref[...]
    moe = y0_ref[...] * w[:, 0:1] + y1_ref[...] * w[:, 1:2]
    xn = x_ref[...] + g2_ref[0, 0] * moe
    if final:
        y = xn * lax.rsqrt(jnp.mean(xn * xn, axis=-1, keepdims=True) + EPS)
        xn = y * gf_ref[...]
    o_ref[...] = xn


def _combine(x, y0, y1, w_sel, mod4, gain_final, n_tok, final):
    tm = TM
    row = _mod_row(tm)
    tok = lambda w: pl.BlockSpec((tm, w), lambda i: (i, 0))
    return pl.pallas_call(
        functools.partial(_combine_kernel, final=final),
        grid=(n_tok // tm,),
        in_specs=[tok(D_MODEL), tok(D_MODEL), tok(D_MODEL), tok(TOP_K),
                  pl.BlockSpec((1, 1, 1, D_MODEL), lambda i: (row(i), 5, 0, 0)),
                  pl.BlockSpec((1, D_MODEL), lambda i: (0, 0))],
        out_specs=tok(D_MODEL),
        out_shape=jax.ShapeDtypeStruct((n_tok if final else T_ALL, D_MODEL), F32),
        compiler_params=_cparams(("parallel",)),
        name="moe_combine",
    )(x, y0, y1, w_sel, mod4, gain_final)


def kernel(x, c, ctx, c_ctx, w_mod, b_mod, norm_mix, norm_ffn, w_in, attn_sink, conv_w, conv_b, filt_w1, filt_b1, filt_freq, filt_w2, filt_b2, filt_w3, hyena_bias, w_branch_attn, w_branch_fnet, w_branch_hyena, w_out, w_router, b_router, w_exp_gate, w_exp_up, w_exp_down, norm_final):
    mats = _dft_mats()
    cos_t, sin_t = _rope_tables()
    c8 = jnp.concatenate([c, c_ctx[None, :], jnp.zeros((8 - BATCH - 1, D_MODEL), F32)], axis=0)
    mod_all = _modulation(c8, w_mod, b_mod)
    xa = jnp.concatenate([x.reshape(T_LAT, D_MODEL), ctx.reshape(T_CTX, D_MODEL)], axis=0)
    wrt = w_router.T.astype(BF16)
    br = b_router.astype(F32).reshape(N_EXPERTS, 1)
    gain_final = norm_final.reshape(1, D_MODEL)
    out = None
    for l in range(DEPTH):
        last = l == DEPTH - 1
        with_ctx = not last
        n_tok = T_LAT if last else T_ALL
        mod4 = mod_all[l].reshape(8, N_MOD, 1, D_MODEL)
        wl = w_in[l]
        w_perm = jnp.concatenate([wl[:, G_OFF:], wl[:, H_OFF:G_OFF], wl[:, F_OFF:H_OFF], wl[:, Q_OFF:F_OFF]],
                                 axis=1).astype(BF16)
        p = _norm_proj(xa, norm_mix[l].reshape(1, D_MODEL), mod4, w_perm, T_ALL)
        ya = _attention(p, attn_sink[l], cos_t, sin_t, with_ctx)
        yf = _fourier_mix(p, mats, n_tok, with_ctx)
        z3 = _short_conv(p, conv_w[l], conv_b[l], with_ctx)
        filt = (filt_w1[l], filt_b1[l], filt_freq[l], filt_w2[l], filt_b2[l], filt_w3[l])
        yh = _hyena_mix(z3, filt, hyena_bias[l], mats, with_ctx)
        xa, h2, e_idx, w_sel, rank, cnt = _merge(
            xa, ya, yf, yh, p, w_branch_attn[l].astype(BF16), w_branch_fnet[l].astype(BF16),
            w_branch_hyena[l].astype(BF16), w_out[l].astype(BF16), mod4,
            norm_ffn[l].reshape(1, D_MODEL), wrt, br, n_tok)
        row_tok, dest, blk_expert, n_active = _dispatch(e_idx, rank, cnt[:, 0], n_tok)
        xs = jnp.take(h2, row_tok, axis=0, mode='clip')
        ys = _experts(xs, blk_expert, n_active, w_exp_gate[l].astype(BF16), w_exp_up[l].astype(BF16),
                      w_exp_down[l].astype(BF16))
        y0 = jnp.take(ys, dest[0], axis=0)
        y1 = jnp.take(ys, dest[1], axis=0)
        res = _combine(xa, y0, y1, w_sel.T, mod4, gain_final, n_tok, last)
        if last:
            out = res
        else:
            xa = res
    return out.reshape(BATCH, SEQ, D_MODEL)
```

```python
import functools
import math

import jax
import jax.numpy as jnp
from jax import lax
from jax.experimental import pallas as pl
from jax.experimental.pallas import tpu as pltpu

F32 = jnp.float32
BF16 = jnp.bfloat16

D_MODEL = 1024
BATCH = 4
SEQ = 4096
DEPTH = 4
GRID_W = 64
CTX_LEN = 256
EPS = 1e-6
N_MOD = 6

HEAD_DIM = 64
N_Q_HEADS = 8
N_KV_HEADS = 2
Q_PER_KV = N_Q_HEADS // N_KV_HEADS
ATTN_BLOCK = 128
ROPE_BASE = 10000.0

FNET_GROUPS = 4
FNET_GROUP_DIM = 128
FNET_WIDTH = FNET_GROUPS * FNET_GROUP_DIM

HYENA_WIDTH = 512
HYENA_ORDER = 2
FILTER_EMB = 33
FILTER_BANDS = (FILTER_EMB - 1) // 2
FILTER_HIDDEN = 64
DECAY_TARGET = 1e-2
FAST_DECAY_PCT = 0.3
SLOW_DECAY_PCT = 1.5

ATTN_WIDTH = N_Q_HEADS * HEAD_DIM
KV_WIDTH = N_KV_HEADS * HEAD_DIM
Q_OFF = 0
K_OFF = Q_OFF + ATTN_WIDTH
V_OFF = K_OFF + KV_WIDTH
F_OFF = V_OFF + KV_WIDTH
H_OFF = F_OFF + FNET_WIDTH
G_OFF = H_OFF + (HYENA_ORDER + 1) * HYENA_WIDTH
IN_WIDTH = G_OFF + 3 * D_MODEL

N_EXPERTS = 16
N_GROUPS = 4
EXPERTS_PER_GROUP = N_EXPERTS // N_GROUPS
TOP_K = 2
EXPERT_FF = 1024

T_LAT = BATCH * SEQ
T_CTX = BATCH * CTX_LEN
T_ALL = T_LAT + T_CTX

PG_OFF = 0
PH_OFF = 3 * D_MODEL
PF_OFF = PH_OFF + (HYENA_ORDER + 1) * HYENA_WIDTH
PQ_OFF = PF_OFF + FNET_WIDTH
PK_OFF = PQ_OFF + ATTN_WIDTH
PV_OFF = PK_OFF + KV_WIDTH

HY_N = 2 * SEQ
HY_N2 = 64
HY_N1 = HY_N // HY_N2
FN_N = 64

PITCH = HY_N2 + 8
UNROLL = 4
TM = 512
EXPERT_BM = 256
VMEM_LIMIT = 52 * 1024 * 1024


def _cparams(sem, vmem=VMEM_LIMIT):
    return pltpu.CompilerParams(dimension_semantics=sem, vmem_limit_bytes=vmem)


def _dot(a, b):
    return jnp.dot(a, b, preferred_element_type=F32)


def _cis(expo, n):
    ang = (2.0 * math.pi / n) * jnp.mod(expo, n).astype(F32)
    return jnp.cos(ang), jnp.sin(ang)


def _real_form(gr, gi):
    return jnp.concatenate([jnp.concatenate([gr, -gi], axis=-1), jnp.concatenate([gi, gr], axis=-1)], axis=-2)


def _dft_mats():
    ar = lambda n: jnp.arange(n, dtype=jnp.int32)
    m = {}
    c, s = _cis(ar(HY_N1)[:, None] * ar(HY_N1 // 2)[None, :], HY_N1)
    m['hy_lead_f'] = _real_form(c, -s).astype(BF16)
    c, s = _cis(ar(HY_N1 // 2)[:, None] * ar(HY_N1)[None, :], HY_N1)
    m['hy_lead_i'] = _real_form(c, s).astype(BF16)
    c, s = _cis(ar(HY_N1)[:, None] * ar(HY_N1 // 2)[None, :], HY_N1)
    m['hy_lead_kh'] = jnp.concatenate([c, -s], axis=0).astype(BF16)
    a = ar(HY_N1)[:, None, None]
    k2 = ar(HY_N2)[None, :, None]
    n2 = ar(HY_N2)[None, None, :]
    c, s = _cis(n2 * (a + HY_N1 * k2), HY_N)
    m['hy_slab_f2'] = jnp.concatenate([c, -s], axis=-1).astype(BF16)
    ct = jnp.swapaxes(c, 1, 2) * (1.0 / HY_N)
    st = jnp.swapaxes(s, 1, 2) * (1.0 / HY_N)
    m['hy_slab_i2'] = jnp.concatenate([ct, st], axis=-1).astype(BF16)
    nc = 2 * CTX_LEN
    c, s = _cis(ar(nc)[:, None] * ar(CTX_LEN)[None, :], nc)
    m['hc_f'] = _real_form(c, -s).astype(BF16)
    c, s = _cis(ar(CTX_LEN)[:, None] * ar(nc)[None, :], nc)
    m['hc_i'] = _real_form(c * (1.0 / nc), s * (1.0 / nc)).astype(BF16)
    c, s = _cis(ar(nc)[:, None] * ar(CTX_LEN)[None, :], nc)
    m['hc_kh'] = jnp.concatenate([c, -s], axis=0).astype(BF16)
    c, s = _cis(ar(FNET_GROUP_DIM)[:, None] * ar(FNET_GROUP_DIM)[None, :], FNET_GROUP_DIM)
    m['fn_chan'] = jnp.concatenate([c, -s], axis=1).astype(BF16)
    c, s = _cis(ar(FN_N)[:, None] * ar(FN_N)[None, :], FN_N)
    m['fn_lead'] = _real_form(c, -s).astype(BF16)
    a = ar(FN_N)[:, None, None]
    k1 = ar(FN_N)[None, :, None]
    n1 = ar(FN_N)[None, None, :]
    scale = 1.0 / math.sqrt(SEQ * FNET_GROUP_DIM)
    c, s = _cis(n1 * (a + FN_N * k1), SEQ)
    m['fn_slab'] = jnp.concatenate([c * scale, s * scale], axis=-1).astype(BF16)
    scale = 1.0 / math.sqrt(CTX_LEN * FNET_GROUP_DIM)
    c, s = _cis(ar(CTX_LEN)[:, None] * ar(CTX_LEN)[None, :], CTX_LEN)
    m['fc'] = jnp.concatenate([c * scale, s * scale], axis=-1).astype(BF16)
    return m


def _mod_kernel(c_ref, w_ref, b_ref, o_ref):
    c = c_ref[...]
    s = c * jax.nn.sigmoid(c)
    o_ref[0] = _dot(s.astype(BF16), w_ref[0].astype(BF16)) + b_ref[0]


def _modulation(c8, w_mod, b_mod):
    tn = 1536
    n = N_MOD * D_MODEL
    return pl.pallas_call(
        _mod_kernel,
        grid=(DEPTH, n // tn),
        in_specs=[pl.BlockSpec((8, D_MODEL), lambda l, j: (0, 0)),
                  pl.BlockSpec((1, D_MODEL, tn), lambda l, j: (l, 0, j)),
                  pl.BlockSpec((1, 1, tn), lambda l, j: (l, 0, j))],
        out_specs=pl.BlockSpec((1, 8, tn), lambda l, j: (l, 0, j)),
        out_shape=jax.ShapeDtypeStruct((DEPTH, 8, n), F32),
        compiler_params=_cparams(("parallel", "parallel")),
        name="adaln_modulation",
    )(c8, w_mod, b_mod.reshape(DEPTH, 1, n))


def _mod_row(tm):
    tiles_per_batch = SEQ // tm
    return lambda i: jnp.minimum(i // tiles_per_batch, BATCH)


def _rms_mod(x, g, sh, sc):
    y = x * lax.rsqrt(jnp.mean(x * x, axis=-1, keepdims=True) + EPS)
    return (y * g) * (1.0 + sc) + sh


def _norm_proj_kernel(x_ref, g_ref, sh_ref, sc_ref, w_ref, o_ref, h_scr):
    @pl.when(pl.program_id(1) == 0)
    def _():
        h_scr[...] = _rms_mod(x_ref[...], g_ref[...], sh_ref[0, 0], sc_ref[0, 0]).astype(BF16)

    o_ref[...] = _dot(h_scr[...], w_ref[...])


def _norm_proj(x, gain, mod4, w, n_tok):
    tm = TM
    n_out = w.shape[1]
    tn = n_out // 2
    row = _mod_row(tm)
    return pl.pallas_call(
        _norm_proj_kernel,
        grid=(n_tok // tm, n_out // tn),
        in_specs=[pl.BlockSpec((tm, D_MODEL), lambda i, j: (i, 0)),
                  pl.BlockSpec((1, D_MODEL), lambda i, j: (0, 0)),
                  pl.BlockSpec((1, 1, 1, D_MODEL), lambda i, j: (row(i), 0, 0, 0)),
                  pl.BlockSpec((1, 1, 1, D_MODEL), lambda i, j: (row(i), 1, 0, 0)),
                  pl.BlockSpec((D_MODEL, tn), lambda i, j: (0, j))],
        out_specs=pl.BlockSpec((tm, tn), lambda i, j: (i, j)),
        out_shape=jax.ShapeDtypeStruct((T_ALL, n_out), F32),
        scratch_shapes=[pltpu.VMEM((tm, D_MODEL), BF16)],
        compiler_params=_cparams(("parallel", "arbitrary")),
        name="norm_in_proj",
    )(x, gain, mod4, mod4, w)


def _softmax_pv(qh, k_parts, v_parts, masks, sink):
    nt = (((1,), (1,)), ((), ()))
    scores = []
    for kp, mk in zip(k_parts, masks):
        s = lax.dot_general(qh, kp, nt, preferred_element_type=F32)
        if mk is not None:
            s = jnp.where(mk, s, -1e30)
        scores.append(s)
    m = sink
    for s in scores:
        m = jnp.maximum(m, jnp.max(s, axis=-1, keepdims=True))
    es = [jnp.exp(s - m) for s in scores]
    den = jnp.exp(sink - m)
    for e in es:
        den = den + jnp.sum(e, axis=-1, keepdims=True)
    inv = 1.0 / den
    o = None
    for e, vp in zip(es, v_parts):
        t = _dot((e * inv).astype(BF16), vp)
        o = t if o is None else o + t
    return o


def _attn_kernel(sink_ref, q_ref, km_ref, k0_ref, kp_ref, vm_ref, v0_ref, vp_ref, kc_ref, vc_ref,
                 cos_ref, sin_ref, o_ref, *, nb):
    n = pl.program_id(1)
    blk = ATTN_BLOCK
    lane = lax.broadcasted_iota(jnp.int32, (blk, 128), 1)
    first = (lane % 32) < 16

    def rope(x, blk_idx):
        r0 = pl.multiple_of(blk_idx * blk, blk)
        c = cos_ref[pl.ds(r0, blk), :]
        s = sin_ref[pl.ds(r0, blk), :]
        sw = jnp.where(first, pltpu.roll(x, 112, 1), pltpu.roll(x, 16, 1))
        return x * c + sw * s

    nm = jnp.maximum(n - 1, 0)
    npl = jnp.minimum(n + 1, nb - 1)
    kall = jnp.concatenate([rope(km_ref[...], nm), rope(k0_ref[...], n), rope(kp_ref[...], npl), kc_ref[...]],
                           axis=0)
    vall = jnp.concatenate([vm_ref[...], v0_ref[...], vp_ref[...], vc_ref[...]], axis=0)
    nk = kall.shape[0]
    kswap = pltpu.roll(kall, HEAD_DIM, 1)
    vswap = pltpu.roll(vall, HEAD_DIM, 1)
    lo = lax.broadcasted_iota(jnp.int32, (nk, 128), 1) < HEAD_DIM

    r = lax.broadcasted_iota(jnp.int32, (2 * blk, blk), 0) % blk
    cidx = lax.broadcasted_iota(jnp.int32, (2 * blk, blk), 1)
    ok_prev = jnp.where(cidx >= r, (n > 0).astype(jnp.int32), 0) > 0
    ok_next = jnp.where(cidx <= r, (n < nb - 1).astype(jnp.int32), 0) > 0
    top_rows = lax.broadcasted_iota(jnp.int32, (2 * blk, 1), 0) < blk
    neg = jnp.float32(-1e30)

    scale = HEAD_DIM ** -0.5
    q2 = [(rope(q_ref[:, p * 128:(p + 1) * 128], n) * scale).astype(BF16) for p in range(N_Q_HEADS // 2)]
    nt = (((1,), (1,)), ((), ()))
    for h in range(N_KV_HEADS):
        ka, kb = (kall, kswap) if h == 0 else (kswap, kall)
        va, vb = (vall, vswap) if h == 0 else (vswap, vall)
        kbd = jnp.concatenate([jnp.where(lo, ka, 0.0), jnp.where(lo, 0.0, kb)], axis=0).astype(BF16)
        vbd = jnp.concatenate([jnp.where(lo, va, 0.0), jnp.where(lo, 0.0, vb)], axis=0).astype(BF16)
        q4 = jnp.concatenate([q2[2 * h], q2[2 * h + 1]], axis=0)
        s = lax.dot_general(q4, kbd, nt, preferred_element_type=F32)
        probs = []
        for c in range(2):
            base = c * nk
            sink = jnp.where(top_rows, sink_ref[Q_PER_KV * h + c], sink_ref[Q_PER_KV * h + 2 + c])
            parts = [jnp.where(ok_prev, s[:, base:base + blk], neg),
                     s[:, base + blk:base + 2 * blk],
                     jnp.where(ok_next, s[:, base + 2 * blk:base + 3 * blk], neg),
                     s[:, base + 3 * blk:base + nk]]
            m = sink
            for part in parts:
                m = jnp.maximum(m, jnp.max(part, axis=-1, keepdims=True))
            es = [jnp.exp(part - m) for part in parts]
            den = jnp.exp(sink - m)
            for e in es:
                den = den + jnp.sum(e, axis=-1, keepdims=True)
            inv = 1.0 / den
            probs += [(e * inv).astype(BF16) for e in es]
        o = _dot(jnp.concatenate(probs, axis=1), vbd)
        w0 = h * Q_PER_KV * HEAD_DIM
        o_ref[:, w0:w0 + 128] = o[:blk].astype(o_ref.dtype)
        o_ref[:, w0 + 128:w0 + 256] = o[blk:].astype(o_ref.dtype)


def _ctx_attn_kernel(sink_ref, q_ref, kc_ref, vc_ref, o_ref):
    kc = kc_ref[...].astype(BF16)
    vc = vc_ref[...].astype(BF16)
    scale = HEAD_DIM ** -0.5
    outs = []
    for pair in range(N_Q_HEADS // 2):
        q2 = (q_ref[:, pair * 128:(pair + 1) * 128] * scale).astype(BF16)
        for sub in range(2):
            head = 2 * pair + sub
            kvh = head // Q_PER_KV
            sl = slice(kvh * HEAD_DIM, (kvh + 1) * HEAD_DIM)
            qh = q2[:, sub * HEAD_DIM:(sub + 1) * HEAD_DIM]
            outs.append(_softmax_pv(qh, [kc[:, sl]], [vc[:, sl]], [None], sink_ref[head]))
    o_ref[...] = jnp.concatenate(outs, axis=-1).astype(o_ref.dtype)


def _attention(p, sink, cos_t, sin_t, with_ctx):
    blk = ATTN_BLOCK
    nb = SEQ // blk
    qc, kcol, vcol = PQ_OFF // ATTN_WIDTH, PK_OFF // KV_WIDTH, PV_OFF // KV_WIDTH
    ctx_blk = T_LAT // CTX_LEN
    smem = pl.BlockSpec(memory_space=pltpu.SMEM)

    def kv_spec(col, d):
        return pl.BlockSpec((blk, KV_WIDTH),
                            lambda b, n: (b * nb + jnp.clip(n + d, 0, nb - 1), col))

    ya = pl.pallas_call(
        functools.partial(_attn_kernel, nb=nb),
        grid=(BATCH, nb),
        in_specs=[smem,
                  pl.BlockSpec((blk, ATTN_WIDTH), lambda b, n: (b * nb + n, qc)),
                  kv_spec(kcol, -1), kv_spec(kcol, 0), kv_spec(kcol, 1),
                  kv_spec(vcol, -1), kv_spec(vcol, 0), kv_spec(vcol, 1),
                  pl.BlockSpec((CTX_LEN, KV_WIDTH), lambda b, n: (ctx_blk + b, kcol)),
                  pl.BlockSpec((CTX_LEN, KV_WIDTH), lambda b, n: (ctx_blk + b, vcol)),
                  pl.BlockSpec((SEQ, KV_WIDTH), lambda b, n: (0, 0)),
                  pl.BlockSpec((SEQ, KV_WIDTH), lambda b, n: (0, 0))],
        out_specs=pl.BlockSpec((blk, ATTN_WIDTH), lambda b, n: (b * nb + n, 0)),
        out_shape=jax.ShapeDtypeStruct((T_ALL, ATTN_WIDTH), BF16),
        compiler_params=_cparams(("parallel", "parallel")),
        name="banded_attention",
    )(sink, p, p, p, p, p, p, p, p, p, cos_t, sin_t)
    if not with_ctx:
        return ya
    cb = CTX_LEN // blk
    lat_blk = T_LAT // blk

    def alias_kernel(sink_ref, q_ref, kc_ref, vc_ref, ya_in_ref, o_ref):
        del ya_in_ref
        _ctx_attn_kernel(sink_ref, q_ref, kc_ref, vc_ref, o_ref)

    return pl.pallas_call(
        alias_kernel,
        grid=(BATCH, cb),
        in_specs=[smem,
                  pl.BlockSpec((blk, ATTN_WIDTH), lambda b, n: (lat_blk + b * cb + n, qc)),
                  pl.BlockSpec((CTX_LEN, KV_WIDTH), lambda b, n: (ctx_blk + b, kcol)),
                  pl.BlockSpec((CTX_LEN, KV_WIDTH), lambda b, n: (ctx_blk + b, vcol)),
                  pl.BlockSpec(memory_space=pl.ANY)],
        out_specs=pl.BlockSpec((blk, ATTN_WIDTH), lambda b, n: (lat_blk + b * cb + n, 0)),
        out_shape=jax.ShapeDtypeStruct((T_ALL, ATTN_WIDTH), BF16),
        input_output_aliases={4: 0},
        compiler_params=_cparams(("parallel", "parallel")),
        name="context_attention",
    )(sink, p, p, p, ya)


def _rope_tables():
    n_freq = HEAD_DIM // 4
    freqs = ROPE_BASE ** (-jnp.arange(n_freq, dtype=F32) / n_freq)
    t = jnp.arange(SEQ, dtype=jnp.int32)
    rows = (t // GRID_W).astype(F32)[:, None] * freqs
    cols = (t % GRID_W).astype(F32)[:, None] * freqs
    cos_h = jnp.concatenate([jnp.cos(rows), jnp.cos(rows), jnp.cos(cols), jnp.cos(cols)], axis=-1)
    sin_h = jnp.concatenate([-jnp.sin(rows), jnp.sin(rows), -jnp.sin(cols), jnp.sin(cols)], axis=-1)
    return jnp.tile(cos_h, (1, 2)), jnp.tile(sin_h, (1, 2))


def _lead_kernel(*refs, n_in, cmul, epi):
    m_ref = refs[0]
    x_refs = refs[1:1 + n_in]
    pos = 1 + n_in
    xs = []
    for r in x_refs:
        v = r[...]
        xs.append(v.reshape(-1, v.shape[-1]))
    x = xs[0] if n_in == 1 else jnp.concatenate(xs, axis=0)
    if cmul:
        k = refs[pos][...]
        pos += 1
        k = k.reshape(-1, k.shape[-1])
        half = x.shape[0] // 2
        xr, xi, kr, ki = x[:half], x[half:], k[:half], k[half:]
        x = jnp.concatenate([xr * kr - xi * ki, xr * ki + xi * kr], axis=0)
    res = _dot(m_ref[...], x.astype(BF16))
    if epi:
        g_ref, y_ref, b_ref = refs[pos:pos + 3]
        pos += 3
        g = g_ref[...]
        y = y_ref[...]
        res = g.reshape(-1, g.shape[-1]) * (res + y.reshape(-1, y.shape[-1]) * b_ref[...])
    o_ref = refs[pos]
    o_ref[...] = res.reshape(o_ref.shape).astype(o_ref.dtype)


def _lead(mat, xs, x_specs, grid, out_shape, out_spec, *, kspec=None, epi=None, alias_to=None, name):
    ins = [mat] + list(xs)
    specs = [pl.BlockSpec(mat.shape, lambda *a: (0, 0))] + list(x_specs)
    if kspec is not None:
        ins.append(kspec[0])
        specs.append(kspec[1])
    if epi is not None:
        for arr, sp in epi:
            ins.append(arr)
            specs.append(sp)
    kern = functools.partial(_lead_kernel, n_in=len(xs), cmul=kspec is not None, epi=epi is not None)
    aliases = {}
    if alias_to is not None:
        aliases = {len(ins): 0}
        ins.append(alias_to)
        specs.append(pl.BlockSpec(memory_space=pl.ANY))
        inner = kern

        def kern(*refs):
            inner(*refs[:-2], refs[-1])

    return pl.pallas_call(
        kern, grid=grid, in_specs=specs, out_specs=out_spec, out_shape=out_shape,
        input_output_aliases=aliases,
        compiler_params=_cparams(("parallel",) * len(grid)), name=name,
    )(*ins)


def _cstack(xr, xi):
    return jnp.concatenate([jnp.concatenate([xr, xi], axis=1), jnp.concatenate([-xi, xr], axis=1)], axis=0)


def _fnet_kernel(u_ref, mc_ref, ml_ref, ms_ref, o_ref, zr, zi, are, aim):
    n = FN_N
    pitch = PITCH
    gd = FNET_GROUP_DIM
    rows = 4 * n
    mc = mc_ref[...]

    def chan(i, carry):
        r_in = pl.multiple_of(i * rows, rows)
        z = _dot(u_ref[pl.ds(r_in, rows), :].astype(BF16), mc)
        for q in range(rows // n):
            r_out = pl.multiple_of((i * (rows // n) + q) * pitch, 8)
            zr[pl.ds(r_out, n), :] = z[q * n:(q + 1) * n, :gd]
            zi[pl.ds(r_out, n), :] = z[q * n:(q + 1) * n, gd:]
        return carry

    lax.fori_loop(0, SEQ // rows, chan, 0)
    ml = ml_ref[...]

    def lead(i, carry):
        for u in range(UNROLL):
            n1 = i * UNROLL + u
            x = jnp.concatenate([zr[pl.ds(n1, n, stride=pitch), :], zi[pl.ds(n1, n, stride=pitch), :]],
                                axis=0).astype(BF16)
            r = _dot(ml, x)
            are[pl.ds(n1, n, stride=pitch), :] = r[:n]
            aim[pl.ds(n1, n, stride=pitch), :] = r[n:]
        return carry

    lax.fori_loop(0, n // UNROLL, lead, 0)

    def slab(i, carry):
        for u in range(UNROLL):
            k2 = i * UNROLL + u
            r0 = pl.multiple_of(k2 * pitch, 8)
            x = jnp.concatenate([are[pl.ds(r0, n), :], aim[pl.ds(r0, n), :]], axis=0).astype(BF16)
            o_ref[pl.ds(k2, n, stride=n), :] = _dot(ms_ref[k2], x)
        return carry

    lax.fori_loop(0, n // UNROLL, slab, 0)


def _fnet_ctx_kernel(u_ref, mc_ref, mf_ref, yf_in_ref, o_ref):
    del yf_in_ref
    gd = FNET_GROUP_DIM
    mc = mc_ref[...]
    mf = mf_ref[...]
    for g in range(FNET_GROUPS):
        z = _dot(u_ref[:, g * gd:(g + 1) * gd].astype(BF16), mc)
        x = jnp.concatenate([z[:, :gd], z[:, gd:]], axis=0).astype(BF16)
        o_ref[:, g * gd:(g + 1) * gd] = _dot(mf, x)


def _fourier_mix(p, mats, with_ctx):
    gd = FNET_GROUP_DIM
    col0 = PF_OFF // gd
    full = lambda a: pl.BlockSpec(a.shape, lambda *i: (0,) * a.ndim)
    scr = pltpu.VMEM((FN_N * PITCH, gd), F32)
    yf = pl.pallas_call(
        _fnet_kernel,
        grid=(BATCH, FNET_GROUPS),
        in_specs=[pl.BlockSpec((SEQ, gd), lambda b, g: (b, col0 + g)),
                  full(mats['fn_chan']), full(mats['fn_lead']), full(mats['fn_slab'])],
        out_specs=pl.BlockSpec((SEQ, gd), lambda b, g: (b, g)),
        out_shape=jax.ShapeDtypeStruct((T_ALL if with_ctx else T_LAT, FNET_WIDTH), F32),
        scratch_shapes=[scr, scr, scr, scr],
        compiler_params=_cparams(("parallel", "parallel")),
        name="fnet_latent",
    )(p, mats['fn_chan'], mats['fn_lead'], mats['fn_slab'])
    if not with_ctx:
        return yf
    blk0 = T_LAT // CTX_LEN
    return pl.pallas_call(
        _fnet_ctx_kernel,
        grid=(BATCH,),
        in_specs=[pl.BlockSpec((CTX_LEN, FNET_WIDTH), lambda b: (blk0 + b, PF_OFF // FNET_WIDTH)),
                  full(mats['fn_chan']), full(mats['fc']), pl.BlockSpec(memory_space=pl.ANY)],
        out_specs=pl.BlockSpec((CTX_LEN, FNET_WIDTH), lambda b: (blk0 + b, 0)),
        out_shape=jax.ShapeDtypeStruct((T_ALL, FNET_WIDTH), F32),
        input_output_aliases={3: 0},
        compiler_params=_cparams(("parallel",)),
        name="fnet_ctx",
    )(p, mats['fn_chan'], mats['fc'], yf)


def _short_conv_kernel(u_ref, w_ref, b_ref, o_ref, *, rows, chunk):
    w0 = w_ref[0:1, :]
    w1 = w_ref[1:2, :]
    w2 = w_ref[2:3, :]
    bias = b_ref[...]
    width = u_ref.shape[-1]
    ridx = lax.broadcasted_iota(jnp.int32, (chunk, width), 0)
    n_chunks = rows // chunk
    for ci in range(n_chunks):
        r0 = ci * chunk
        cur = u_ref[r0:r0 + chunk, :]
        if ci > 0:
            prev_row = u_ref[r0 - 8:r0, :][7:8, :]
        else:
            prev_row = jnp.zeros((1, width), F32)
        if ci < n_chunks - 1:
            next_row = u_ref[r0 + chunk:r0 + chunk + 8, :][0:1, :]
        else:
            next_row = jnp.zeros((1, width), F32)
        up = jnp.where(ridx == 0, prev_row, pltpu.roll(cur, 1, 0))
        dn = jnp.where(ridx == chunk - 1, next_row, pltpu.roll(cur, chunk - 1, 0))
        o_ref[0, r0:r0 + chunk, :] = up * w0 + cur * w1 + dn * w2 + bias


def _short_conv(p, conv_w, conv_b, with_ctx):
    cw = 256
    hw = (HYENA_ORDER + 1) * HYENA_WIDTH
    ncol = hw // cw
    per = HYENA_WIDTH // cw
    col0 = PH_OFF // cw
    out_shape = jax.ShapeDtypeStruct((HYENA_ORDER + 1, T_ALL if with_ctx else T_LAT, HYENA_WIDTH), F32)
    b2 = conv_b.reshape(1, hw)

    def call(rows, blk0, alias):
        kern = functools.partial(_short_conv_kernel, rows=rows, chunk=min(rows, 256))
        ins = [p, conv_w, b2]
        specs = [pl.BlockSpec((rows, cw), lambda b, j: (blk0 + b, col0 + j)),
                 pl.BlockSpec((3, cw), lambda b, j: (0, j)),
                 pl.BlockSpec((1, cw), lambda b, j: (0, j))]
        aliases = {}
        if alias is not None:
            ins.append(alias)
            specs.append(pl.BlockSpec(memory_space=pl.ANY))
            aliases = {3: 0}
            inner = kern

            def kern(u_ref, w_ref, b_ref, a_ref, o_ref):
                del a_ref
                inner(u_ref, w_ref, b_ref, o_ref)

        return pl.pallas_call(
            kern, grid=(BATCH, ncol), in_specs=specs,
            out_specs=pl.BlockSpec((1, rows, cw), lambda b, j: (j // per, blk0 + b, j % per)),
            out_shape=out_shape, input_output_aliases=aliases,
            compiler_params=_cparams(("parallel", "parallel")), name="hyena_short_conv",
        )(*ins)

    z3 = call(SEQ, 0, None)
    if with_ctx:
        z3 = call(CTX_LEN, T_LAT // CTX_LEN, z3)
    return z3


def _filter_mlp_kernel(ft_ref, w1_ref, b1_ref, fq_ref, w2_ref, b2_ref, o_ref):
    fq = fq_ref[...]
    h = jnp.sin(fq * (_dot(ft_ref[...].astype(BF16), w1_ref[...]) + b1_ref[...]))
    h = jnp.sin(fq * (_dot(h.astype(BF16), w2_ref[...]) + b2_ref[...]))
    o_ref[...] = h.astype(o_ref.dtype)


def _filter_kernel(h_ref, t_ref, w3f_ref, w3b_ref, dl_ref, m1_ref, *rest, n, dense):
    hb = h_ref[...]
    decay = jnp.exp(-t_ref[...] * dl_ref[...])
    tf = _dot(hb, w3f_ref[...]) * decay
    tb = _dot(hb, w3b_ref[...]) * decay
    tb = jnp.where(lax.broadcasted_iota(jnp.int32, tb.shape, 0) == 0, 0.0, tb)
    scale = 1.0 / (jnp.sum(jnp.abs(tf), axis=0, keepdims=True) + jnp.sum(jnp.abs(tb), axis=0, keepdims=True))
    cw = tf.shape[1]
    if dense:
        o_ref = rest[0]
        r = _dot(m1_ref[...], jnp.concatenate([tf, tb], axis=1).astype(BF16))
        nc = r.shape[0] // 2
        o_ref[0:nc, :] = (r[:nc, :cw] + r[:nc, cw:]) * scale
        o_ref[nc:, :] = (r[nc:, :cw] - r[nc:, cw:]) * scale
        return
    f2_ref, o_ref, tf_s, tb_s, are_f, aim_f, are_b, aim_b = rest
    tf_s[...] = tf
    tb_s[...] = tb
    half = HY_N1 // 2
    m1 = m1_ref[...]

    def stage1(i, carry):
        for u in range(UNROLL):
            n2 = i * UNROLL + u
            x = jnp.concatenate([tf_s[pl.ds(n2, half, stride=HY_N2), :], tb_s[pl.ds(n2, half, stride=HY_N2), :]],
                                axis=1).astype(BF16)
            r = _dot(m1, x)
            rows = pl.ds(n2, HY_N1, stride=PITCH)
            are_f[rows, :] = r[:HY_N1, :cw]
            are_b[rows, :] = r[:HY_N1, cw:]
            aim_f[rows, :] = r[HY_N1:, :cw]
            aim_b[rows, :] = r[HY_N1:, cw:]
        return carry

    lax.fori_loop(0, HY_N2 // UNROLL, stage1, 0)

    def stage2(i, carry):
        for u in range(UNROLL):
            k1 = i * UNROLL + u
            rows = pl.ds(pl.multiple_of(k1 * PITCH, 8), HY_N2)
            ar = jnp.concatenate([are_f[rows, :], are_b[rows, :]], axis=1)
            ai = jnp.concatenate([aim_f[rows, :], aim_b[rows, :]], axis=1)
            t = _dot(f2_ref[k1], _cstack(ar, ai).astype(BF16))
            o_ref[0, k1] = (t[:, 0:cw] + t[:, cw:2 * cw]) * scale
            o_ref[1, k1] = (t[:, 2 * cw:3 * cw] - t[:, 3 * cw:4 * cw]) * scale
        return carry

    lax.fori_loop(0, HY_N1 // UNROLL, stage2, 0)


def _filter_feats(n):
    pos = jnp.arange(n, dtype=F32)
    t = pos / max(n - 1, 1)
    omega = 2.0 * math.pi * pos / n
    bands = jnp.linspace(1e-4, FILTER_BANDS - 1, FILTER_BANDS, dtype=F32)
    feats = jnp.concatenate([t[:, None], jnp.cos(omega[:, None] * bands), -jnp.sin(omega[:, None] * bands)], axis=-1)
    return jnp.pad(feats, ((0, 0), (0, 128 - FILTER_EMB))), t[:, None]


def _filter_spectrum(n, filt, mats):
    w1, b1, freq, w2, b2, w3 = filt
    dense = n == CTX_LEN
    cw = 128
    nch = HYENA_WIDTH // cw
    feats, t = _filter_feats(n)
    w1p = jnp.pad(w1, ((0, 128 - FILTER_EMB), (0, 0))).astype(BF16)
    deltas = jnp.abs(jnp.linspace(math.log(DECAY_TARGET) / SLOW_DECAY_PCT, math.log(DECAY_TARGET) / FAST_DECAY_PCT,
                                  HYENA_WIDTH, dtype=F32)).reshape(1, HYENA_WIDTH)
    full = lambda a: pl.BlockSpec(a.shape, lambda *i: (0,) * a.ndim)
    row = lambda a: a.reshape(1, -1)
    w3b16 = w3.astype(BF16)
    tap_spec = lambda d: pl.BlockSpec((FILTER_HIDDEN, cw), lambda o, ch: (0, (o * 2 + d) * nch + ch))
    m1 = mats['hc_kh'] if dense else mats['hy_lead_kh']
    mlp_ins = [feats, w1p, row(b1), row(freq), w2.astype(BF16), row(b2)]
    hb = pl.pallas_call(
        _filter_mlp_kernel, grid=(1,), in_specs=[full(a) for a in mlp_ins],
        out_specs=pl.BlockSpec((n, FILTER_HIDDEN), lambda i: (0, 0)),
        out_shape=jax.ShapeDtypeStruct((n, FILTER_HIDDEN), BF16),
        compiler_params=_cparams(("arbitrary",)), name="hyena_filter_mlp",
    )(*mlp_ins)
    ins = [hb, t, w3b16, w3b16, deltas, m1]
    specs = [full(hb), full(t), tap_spec(0), tap_spec(1), pl.BlockSpec((1, cw), lambda o, ch: (0, ch)), full(m1)]
    if dense:
        nc = 2 * n
        out_shape = jax.ShapeDtypeStruct((HYENA_ORDER, 2 * nc, HYENA_WIDTH), F32)
        out_spec = pl.BlockSpec((None, 2 * nc, cw), lambda o, ch: (o, 0, ch))
        scratch = []
    else:
        ins.append(mats['hy_slab_f2'])
        specs.append(full(mats['hy_slab_f2']))
        out_shape = jax.ShapeDtypeStruct((HYENA_ORDER, 2, HY_N1, HY_N2, HYENA_WIDTH), F32)
        out_spec = pl.BlockSpec((None, 2, HY_N1, HY_N2, cw), lambda o, ch: (o, 0, 0, 0, ch))
        scratch = [pltpu.VMEM((n, cw), F32)] * 2 + [pltpu.VMEM((HY_N1 * PITCH, cw), F32)] * 4
    return pl.pallas_call(
        functools.partial(_filter_kernel, n=n, dense=dense),
        grid=(HYENA_ORDER, nch), in_specs=specs, out_specs=out_spec, out_shape=out_shape,
        scratch_shapes=scratch,
        compiler_params=_cparams(("parallel", "parallel")),
        name="hyena_filter_ctx" if dense else "hyena_filter",
    )(*ins)


def _hyena_conv_kernel(y_ref, g_ref, k_ref, m1_ref, f2f_ref, f2i_ref, m3_ref, b_ref, o_ref, are, aim):
    half = HY_N1 // 2
    m1 = m1_ref[...]

    def stage1(i, carry):
        for u in range(UNROLL):
            n2 = i * UNROLL + u
            x = jnp.concatenate([y_ref[pl.ds(n2, half, stride=HY_N2), :],
                                 y_ref[pl.ds(SEQ + n2, half, stride=HY_N2), :]], axis=0).astype(BF16)
            r = _dot(m1, x)
            are[pl.ds(n2, HY_N1, stride=PITCH), :] = r[:HY_N1]
            aim[pl.ds(n2, HY_N1, stride=PITCH), :] = r[HY_N1:]
        return carry

    lax.fori_loop(0, HY_N2 // UNROLL, stage1, 0)
    cw = o_ref.shape[-1]

    def stage2(i, carry):
        for u in range(UNROLL):
            k1 = i * UNROLL + u
            r0 = pl.multiple_of(k1 * PITCH, 8)
            y = _dot(f2f_ref[k1], _cstack(are[pl.ds(r0, HY_N2), :], aim[pl.ds(r0, HY_N2), :]).astype(BF16))
            yr, yi = y[:, :cw], y[:, cw:]
            kr, ki = k_ref[0, k1], k_ref[1, k1]
            w = _dot(f2i_ref[k1], _cstack(yr * kr - yi * ki, yr * ki + yi * kr).astype(BF16))
            are[pl.ds(r0, HY_N2), :] = w[:, :cw]
            aim[pl.ds(r0, HY_N2), :] = w[:, cw:]
        return carry

    lax.fori_loop(0, HY_N1 // UNROLL, stage2, 0)
    m3 = m3_ref[...]
    bias = b_ref[...]

    def stage3(i, carry):
        for u in range(UNROLL):
            n2 = i * UNROLL + u
            bn = jnp.concatenate([are[pl.ds(n2, HY_N1, stride=PITCH), :], aim[pl.ds(n2, HY_N1, stride=PITCH), :]],
                                 axis=0).astype(BF16)
            y = _dot(m3, bn)
            for b in range(2):
                rows = pl.ds(b * SEQ + n2, half, stride=HY_N2)
                o_ref[rows, :] = g_ref[rows, :] * (y[b * half:(b + 1) * half] + y_ref[rows, :] * bias)
        return carry

    lax.fori_loop(0, HY_N2 // UNROLL, stage3, 0)


def _hyena_mix(z3, filt, hyena_bias, mats, with_ctx):
    c = HYENA_WIDTH
    cw = 128
    nch = c // cw
    pairs = BATCH // 2
    full = lambda a: pl.BlockSpec(a.shape, lambda *i: (0,) * a.ndim)
    kspec = _filter_spectrum(SEQ, filt, mats)
    if with_ctx:
        kspec_c = _filter_spectrum(CTX_LEN, filt, mats)
    bias3 = hyena_bias.reshape(HYENA_ORDER, 1, c)
    scr = pltpu.VMEM((HY_N1 * PITCH, cw), F32)
    y = None
    for o in range(HYENA_ORDER):
        if y is None:
            xin, xspec = z3, pl.BlockSpec((None, 2 * SEQ, cw), lambda b, ch: (0, b, ch))
        else:
            xin, xspec = y, pl.BlockSpec((2 * SEQ, cw), lambda b, ch: (b, ch))
        ynew = pl.pallas_call(
            _hyena_conv_kernel,
            grid=(pairs, nch),
            in_specs=[xspec,
                      pl.BlockSpec((None, 2 * SEQ, cw), lambda b, ch, o=o: (o + 1, b, ch)),
                      pl.BlockSpec((None, 2, HY_N1, HY_N2, cw), lambda b, ch, o=o: (o, 0, 0, 0, ch)),
                      full(mats['hy_lead_f']), full(mats['hy_slab_f2']), full(mats['hy_slab_i2']),
                      full(mats['hy_lead_i']),
                      pl.BlockSpec((None, 1, cw), lambda b, ch, o=o: (o, 0, ch))],
            out_specs=pl.BlockSpec((2 * SEQ, cw), lambda b, ch: (b, ch)),
            out_shape=jax.ShapeDtypeStruct((T_ALL if with_ctx else T_LAT, c), F32),
            scratch_shapes=[scr, scr],
            compiler_params=_cparams(("parallel", "parallel"), 58 * 1024 * 1024),
            name="hyena_conv",
        )(xin, z3, kspec, mats['hy_lead_f'], mats['hy_slab_f2'], mats['hy_slab_i2'], mats['hy_lead_i'], bias3)
        if with_ctx:
            nc = 2 * CTX_LEN
            blk0 = T_LAT // nc
            if y is None:
                cin, cspec_in = z3, pl.BlockSpec((1, nc, c), lambda b: (0, blk0 + b, 0))
            else:
                cin, cspec_in = y, pl.BlockSpec((nc, c), lambda b: (blk0 + b, 0))
            xc = _lead(mats['hc_f'], [cin], [cspec_in], (pairs,),
                       jax.ShapeDtypeStruct((pairs, 2 * nc, c), F32),
                       pl.BlockSpec((1, 2 * nc, c), lambda b: (b, 0, 0)), name="hyena_ctx_fwd")
            ynew = _lead(mats['hc_i'], [xc], [pl.BlockSpec((1, 2 * nc, c), lambda b: (b, 0, 0))],
                         (pairs,), jax.ShapeDtypeStruct((T_ALL, c), F32),
                         pl.BlockSpec((nc, c), lambda b: (blk0 + b, 0)),
                         kspec=(kspec_c, pl.BlockSpec((1, 2 * nc, c), lambda b, o=o: (o, 0, 0))),
                         epi=[(z3, pl.BlockSpec((1, nc, c), lambda b, o=o: (o + 1, blk0 + b, 0))),
                              (cin, cspec_in),
                              (bias3, pl.BlockSpec((1, 1, c), lambda b, o=o: (o, 0, 0)))],
                         alias_to=ynew, name="hyena_ctx_inv")
        y = ynew
    return y


def _route_tile(lt, br, base, tri):
    tm = lt.shape[1]
    aff = jax.nn.sigmoid(lt)
    biased = aff + br
    b = [biased[e:e + 1, :] for e in range(N_EXPERTS)]
    a = [aff[e:e + 1, :] for e in range(N_EXPERTS)]
    epg = EXPERTS_PER_GROUP
    scores = []
    for g in range(N_GROUPS):
        x0, x1, x2, x3 = b[epg * g:epg * g + epg]
        s1, t1 = jnp.maximum(x0, x1), jnp.minimum(x0, x1)
        s2, t2 = jnp.maximum(x2, x3), jnp.minimum(x2, x3)
        scores.append(jnp.maximum(s1, s2) + jnp.maximum(jnp.minimum(s1, s2), jnp.maximum(t1, t2)))
    best = scores[0]
    gsel = jnp.zeros((1, tm), jnp.int32)
    for g in range(1, N_GROUPS):
        gsel = jnp.where(scores[g] > best, g, gsel)
        best = jnp.maximum(best, scores[g])

    def pick(rows, j):
        out = rows[j]
        for g in range(1, N_GROUPS):
            out = jnp.where(gsel == g, rows[epg * g + j], out)
        return out

    v = [pick(b, j) for j in range(epg)]
    av = [pick(a, j) for j in range(epg)]
    i1 = jnp.zeros((1, tm), jnp.int32)
    m1 = v[0]
    for j in range(1, epg):
        i1 = jnp.where(v[j] > m1, j, i1)
        m1 = jnp.maximum(m1, v[j])
    neg = jnp.float32(-3.0e38)
    i2 = jnp.zeros((1, tm), jnp.int32)
    m2 = jnp.full((1, tm), neg, F32)
    for j in range(epg):
        cand = jnp.where(i1 == j, neg, v[j])
        take = cand > m2
        i2 = jnp.where(take, j, i2)
        m2 = jnp.where(take, cand, m2)

    def sel(rows, idx):
        out = rows[0]
        for j in range(1, epg):
            out = jnp.where(idx == j, rows[j], out)
        return out

    a1, a2 = sel(av, i1), sel(av, i2)
    den = a1 + a2
    e1 = gsel * epg + i1
    e2 = gsel * epg + i2
    eio = lax.broadcasted_iota(jnp.int32, (N_EXPERTS, tm), 0)
    oh1 = jnp.where(eio == e1, 1.0, 0.0)
    oh2 = jnp.where(eio == e2, 1.0, 0.0)
    oh = oh1 + oh2
    tot = base + _dot(oh.astype(BF16), tri)
    r1 = jnp.sum(oh1 * tot, axis=0, keepdims=True)
    r2 = jnp.sum(oh2 * tot, axis=0, keepdims=True)
    new_base = base + jnp.sum(oh, axis=1, keepdims=True)
    return (e1, e2), (a1 / den, a2 / den), (r1.astype(jnp.int32), r2.astype(jnp.int32)), new_base


def _merge_kernel(x_ref, ya_ref, yf_ref, yh_ref, gt_ref, wa_ref, wf_ref, wh_ref, wo_ref, g1_ref,
                  gn_ref, sh_ref, sc_ref, wrt_ref, br_ref, xo_ref, h2_ref, e_ref, w_ref, r_ref, cnt_ref):
    d = D_MODEL
    merged = jax.nn.sigmoid(gt_ref[:, 0:d]) * _dot(ya_ref[...], wa_ref[...])
    merged += jax.nn.sigmoid(gt_ref[:, d:2 * d]) * _dot(yf_ref[...].astype(BF16), wf_ref[...])
    merged += jax.nn.sigmoid(gt_ref[:, 2 * d:3 * d]) * _dot(yh_ref[...].astype(BF16), wh_ref[...])
    xn = x_ref[...] + g1_ref[0, 0] * _dot(merged.astype(BF16), wo_ref[...])
    xo_ref[...] = xn
    h2f = _rms_mod(xn, gn_ref[...], sh_ref[0, 0], sc_ref[0, 0])
    h2_ref[...] = h2f
    h2 = h2f.astype(BF16)

    @pl.when(pl.program_id(0) == 0)
    def _():
        cnt_ref[...] = jnp.zeros_like(cnt_ref)

    tm = h2.shape[0]
    lt = lax.dot_general(wrt_ref[...], h2, (((1,), (1,)), ((), ())), preferred_element_type=F32)
    tri = jnp.where(lax.broadcasted_iota(jnp.int32, (tm, tm), 0) < lax.broadcasted_iota(jnp.int32, (tm, tm), 1),
                    1.0, 0.0).astype(BF16)
    es, ws, rs, new_base = _route_tile(lt, br_ref[...], cnt_ref[:, 0:1], tri)
    e_ref[0:1, :], e_ref[1:2, :] = es
    w_ref[0:1, :], w_ref[1:2, :] = ws
    r_ref[0:1, :], r_ref[1:2, :] = rs
    cnt_ref[...] = jnp.broadcast_to(new_base, cnt_ref.shape)


def _merge(x, ya, yf, yh, p, wa, wf, wh, wo, mod4, gain2, wrt, br, n_tok):
    tm = 256
    row = _mod_row(tm)
    full = lambda a: pl.BlockSpec(a.shape, lambda i: (0,) * a.ndim)
    modspec = lambda k: pl.BlockSpec((1, 1, 1, D_MODEL), lambda i: (row(i), k, 0, 0))
    tok = lambda w: pl.BlockSpec((tm, w), lambda i: (i, 0))
    lane = pl.BlockSpec((TOP_K, tm), lambda i: (0, i))
    return pl.pallas_call(
        _merge_kernel,
        grid=(n_tok // tm,),
        in_specs=[tok(D_MODEL), tok(ATTN_WIDTH), tok(FNET_WIDTH), tok(HYENA_WIDTH),
                  pl.BlockSpec((tm, 3 * D_MODEL), lambda i: (i, PG_OFF // (3 * D_MODEL))),
                  full(wa), full(wf), full(wh), full(wo), modspec(2), full(gain2), modspec(3), modspec(4),
                  full(wrt), full(br)],
        out_specs=[tok(D_MODEL), tok(D_MODEL), lane, lane, lane,
                   pl.BlockSpec((N_EXPERTS, 128), lambda i: (0, 0))],
        out_shape=[jax.ShapeDtypeStruct((n_tok, D_MODEL), F32),
                   jax.ShapeDtypeStruct((n_tok, D_MODEL), F32),
                   jax.ShapeDtypeStruct((TOP_K, n_tok), jnp.int32),
                   jax.ShapeDtypeStruct((TOP_K, n_tok), F32),
                   jax.ShapeDtypeStruct((TOP_K, n_tok), jnp.int32),
                   jax.ShapeDtypeStruct((N_EXPERTS, 128), F32)],
        compiler_params=_cparams(("arbitrary",)),
        name="merge_out_norm_route",
    )(x, ya, yf, yh, p, wa, wf, wh, wo, mod4, gain2, mod4, mod4, wrt, br)


def _expert_kernel(be_ref, na_ref, x_ref, wg_ref, wu_ref, wd_ref, o_ref, wg_s, wu_s, wd_s):
    i = pl.program_id(0)
    active = i < na_ref[0]
    prev = be_ref[jnp.maximum(i - 1, 0)]

    @pl.when(active & ((i == 0) | (be_ref[i] != prev)))
    def _():
        wg_s[...] = wg_ref[0].astype(BF16)
        wu_s[...] = wu_ref[0].astype(BF16)
        wd_s[...] = wd_ref[0].astype(BF16)

    @pl.when(active)
    def _():
        x = x_ref[...].astype(BF16)
        g = _dot(x, wg_s[...])
        u = _dot(x, wu_s[...])
        h = (g * jax.nn.sigmoid(g)) * u
        o_ref[...] = _dot(h.astype(BF16), wd_s[...])

    @pl.when(jnp.logical_not(active))
    def _():
        o_ref[...] = jnp.zeros_like(o_ref)


def _experts(xs, blk_expert, n_active, wg, wu, wd):
    bm = EXPERT_BM
    n_blk = xs.shape[0] // bm
    wspec = lambda k, n: pl.BlockSpec((1, k, n), lambda i, be, na: (be[i], 0, 0))
    return pl.pallas_call(
        _expert_kernel,
        grid_spec=pltpu.PrefetchScalarGridSpec(
            num_scalar_prefetch=2,
            grid=(n_blk,),
            in_specs=[pl.BlockSpec((bm, D_MODEL), lambda i, be, na: (i, 0)),
                      wspec(D_MODEL, EXPERT_FF), wspec(D_MODEL, EXPERT_FF), wspec(EXPERT_FF, D_MODEL)],
            out_specs=pl.BlockSpec((bm, D_MODEL), lambda i, be, na: (i, 0)),
            scratch_shapes=[pltpu.VMEM((D_MODEL, EXPERT_FF), BF16), pltpu.VMEM((D_MODEL, EXPERT_FF), BF16),
                            pltpu.VMEM((EXPERT_FF, D_MODEL), BF16)]),
        out_shape=jax.ShapeDtypeStruct((xs.shape[0], D_MODEL), F32),
        compiler_params=_cparams(("arbitrary",)),
        name="moe_experts",
    )(blk_expert, n_active, xs, wg, wu, wd)


def _dispatch(e_idx, rank, counts, n_tok):
    bm = EXPERT_BM
    counts = counts.astype(jnp.int32)
    padded = (counts + bm - 1) // bm * bm
    pad_end = jnp.cumsum(padded)
    pad_start = pad_end - padded
    experts = jnp.arange(N_EXPERTS, dtype=jnp.int32)
    start = jnp.sum(jnp.where(e_idx[..., None] == experts, pad_start, 0), axis=-1)
    dest = start + rank
    n_rows = -(-(n_tok * TOP_K) // bm) * bm + N_EXPERTS * bm
    n_blk = n_rows // bm
    tok = jnp.tile(jnp.arange(n_tok, dtype=jnp.int32), TOP_K)
    row_tok = jnp.zeros((n_rows,), jnp.int32).at[dest.reshape(-1)].set(tok)
    blk_start = jnp.arange(n_blk, dtype=jnp.int32) * bm
    blk_expert = jnp.minimum(jnp.sum((blk_start[:, None] >= pad_end[None, :]).astype(jnp.int32), axis=-1),
                             N_EXPERTS - 1)
    n_active = (pad_end[-1] // bm).astype(jnp.int32).reshape(1)
    return row_tok, dest, blk_expert, n_active


def _combine_kernel(x_ref, y0_ref, y1_ref, w_ref, g2_ref, gf_ref, o_ref, *, final):
    w = w_ref[...]
    moe = y0_ref[...] * w[:, 0:1] + y1_ref[...] * w[:, 1:2]
    xn = x_ref[...] + g2_ref[0, 0] * moe
    if final:
        y = xn * lax.rsqrt(jnp.mean(xn * xn, axis=-1, keepdims=True) + EPS)
        xn = y * gf_ref[...]
    o_ref[...] = xn


def _combine(x, y0, y1, w_sel, mod4, gain_final, n_tok, final):
    tm = TM
    row = _mod_row(tm)
    tok = lambda w: pl.BlockSpec((tm, w), lambda i: (i, 0))
    return pl.pallas_call(
        functools.partial(_combine_kernel, final=final),
        grid=(n_tok // tm,),
        in_specs=[tok(D_MODEL), tok(D_MODEL), tok(D_MODEL), tok(TOP_K),
                  pl.BlockSpec((1, 1, 1, D_MODEL), lambda i: (row(i), 5, 0, 0)),
                  pl.BlockSpec((1, D_MODEL), lambda i: (0, 0))],
        out_specs=tok(D_MODEL),
        out_shape=jax.ShapeDtypeStruct((n_tok if final else T_ALL, D_MODEL), F32),
        compiler_params=_cparams(("parallel",)),
        name="moe_combine",
    )(x, y0, y1, w_sel, mod4, gain_final)


def kernel(x, c, ctx, c_ctx, w_mod, b_mod, norm_mix, norm_ffn, w_in, attn_sink, conv_w, conv_b, filt_w1, filt_b1, filt_freq, filt_w2, filt_b2, filt_w3, hyena_bias, w_branch_attn, w_branch_fnet, w_branch_hyena, w_out, w_router, b_router, w_exp_gate, w_exp_up, w_exp_down, norm_final):
    mats = _dft_mats()
    cos_t, sin_t = _rope_tables()
    c8 = jnp.concatenate([c, c_ctx[None, :], jnp.zeros((8 - BATCH - 1, D_MODEL), F32)], axis=0)
    mod_all = _modulation(c8, w_mod, b_mod)
    xa = jnp.concatenate([x.reshape(T_LAT, D_MODEL), ctx.reshape(T_CTX, D_MODEL)], axis=0)
    wrt = w_router.T.astype(BF16)
    br = b_router.astype(F32).reshape(N_EXPERTS, 1)
    gain_final = norm_final.reshape(1, D_MODEL)
    out = None
    for l in range(DEPTH):
        last = l == DEPTH - 1
        with_ctx = not last
        n_tok = T_LAT if last else T_ALL
        mod4 = mod_all[l].reshape(8, N_MOD, 1, D_MODEL)
        wl = w_in[l]
        w_perm = jnp.concatenate([wl[:, G_OFF:], wl[:, H_OFF:G_OFF], wl[:, F_OFF:H_OFF], wl[:, Q_OFF:F_OFF]],
                                 axis=1).astype(BF16)
        p = _norm_proj(xa, norm_mix[l].reshape(1, D_MODEL), mod4, w_perm, T_ALL)
        ya = _attention(p, attn_sink[l], cos_t, sin_t, with_ctx)
        yf = _fourier_mix(p, mats, with_ctx)
        z3 = _short_conv(p, conv_w[l], conv_b[l], with_ctx)
        filt = (filt_w1[l], filt_b1[l], filt_freq[l], filt_w2[l], filt_b2[l], filt_w3[l])
        yh = _hyena_mix(z3, filt, hyena_bias[l], mats, with_ctx)
        xa, h2, e_idx, w_sel, rank, cnt = _merge(
            xa, ya, yf, yh, p, w_branch_attn[l].astype(BF16), w_branch_fnet[l].astype(BF16),
            w_branch_hyena[l].astype(BF16), w_out[l].astype(BF16), mod4,
            norm_ffn[l].reshape(1, D_MODEL), wrt, br, n_tok)
        row_tok, dest, blk_expert, n_active = _dispatch(e_idx, rank, cnt[:, 0], n_tok)
        xs = jnp.take(h2, row_tok, axis=0, mode='clip')
        ys = _experts(xs, blk_expert, n_active, w_exp_gate[l], w_exp_up[l], w_exp_down[l])
        y0 = jnp.take(ys, dest[0], axis=0, mode='clip')
        y1 = jnp.take(ys, dest[1], axis=0, mode='clip')
        res = _combine(xa, y0, y1, w_sel.T, mod4, gain_final, n_tok, last)
        if last:
            out = res
        else:
            xa = res
    return out.reshape(BATCH, SEQ, D_MODEL)
```

```python
import functools
import math

import jax
import jax.numpy as jnp
from jax import lax
from jax.experimental import pallas as pl
from jax.experimental.pallas import tpu as pltpu

F32 = jnp.float32
BF16 = jnp.bfloat16

D_MODEL = 1024
BATCH = 4
SEQ = 4096
DEPTH = 4
GRID_W = 64
CTX_LEN = 256
EPS = 1e-6
N_MOD = 6

HEAD_DIM = 64
N_Q_HEADS = 8
N_KV_HEADS = 2
Q_PER_KV = N_Q_HEADS // N_KV_HEADS
ATTN_BLOCK = 128
ROPE_BASE = 10000.0

FNET_GROUPS = 4
FNET_GROUP_DIM = 128
FNET_WIDTH = FNET_GROUPS * FNET_GROUP_DIM

HYENA_WIDTH = 512
HYENA_ORDER = 2
FILTER_EMB = 33
FILTER_BANDS = (FILTER_EMB - 1) // 2
FILTER_HIDDEN = 64
DECAY_TARGET = 1e-2
FAST_DECAY_PCT = 0.3
SLOW_DECAY_PCT = 1.5

ATTN_WIDTH = N_Q_HEADS * HEAD_DIM
KV_WIDTH = N_KV_HEADS * HEAD_DIM
Q_OFF = 0
K_OFF = Q_OFF + ATTN_WIDTH
V_OFF = K_OFF + KV_WIDTH
F_OFF = V_OFF + KV_WIDTH
H_OFF = F_OFF + FNET_WIDTH
G_OFF = H_OFF + (HYENA_ORDER + 1) * HYENA_WIDTH
IN_WIDTH = G_OFF + 3 * D_MODEL

N_EXPERTS = 16
N_GROUPS = 4
EXPERTS_PER_GROUP = N_EXPERTS // N_GROUPS
TOP_K = 2
EXPERT_FF = 1024

T_LAT = BATCH * SEQ
T_CTX = BATCH * CTX_LEN
T_ALL = T_LAT + T_CTX

PG_OFF = 0
PH_OFF = 3 * D_MODEL
PF_OFF = PH_OFF + (HYENA_ORDER + 1) * HYENA_WIDTH
PQ_OFF = PF_OFF + FNET_WIDTH
PK_OFF = PQ_OFF + ATTN_WIDTH
PV_OFF = PK_OFF + KV_WIDTH

HY_N = 2 * SEQ
HY_N2 = 64
HY_N1 = HY_N // HY_N2
FN_N = 64

PITCH = HY_N2 + 8
UNROLL = 8
TM = 512
EXPERT_BM = 512
VMEM_LIMIT = 52 * 1024 * 1024


def _cparams(sem, vmem=VMEM_LIMIT):
    return pltpu.CompilerParams(dimension_semantics=sem, vmem_limit_bytes=vmem)


def _dot(a, b):
    return jnp.dot(a, b, preferred_element_type=F32)


def _cis(expo, n):
    ang = (2.0 * math.pi / n) * jnp.mod(expo, n).astype(F32)
    return jnp.cos(ang), jnp.sin(ang)


def _real_form(gr, gi):
    return jnp.concatenate([jnp.concatenate([gr, -gi], axis=-1), jnp.concatenate([gi, gr], axis=-1)], axis=-2)


def _dft_mats():
    ar = lambda n: jnp.arange(n, dtype=jnp.int32)
    m = {}
    c, s = _cis(ar(HY_N1)[:, None] * ar(HY_N1 // 2)[None, :], HY_N1)
    m['hy_lead_f'] = _real_form(c, -s).astype(BF16)
    c, s = _cis(ar(HY_N1 // 2)[:, None] * ar(HY_N1)[None, :], HY_N1)
    m['hy_lead_i'] = _real_form(c, s).astype(BF16)
    c, s = _cis(ar(HY_N1)[:, None] * ar(HY_N1 // 2)[None, :], HY_N1)
    m['hy_lead_kh'] = jnp.concatenate([c, -s], axis=0).astype(BF16)
    a = ar(HY_N1)[:, None, None]
    k2 = ar(HY_N2)[None, :, None]
    n2 = ar(HY_N2)[None, None, :]
    c, s = _cis(n2 * (a + HY_N1 * k2), HY_N)
    m['hy_slab_f2'] = jnp.concatenate([c, -s], axis=-1).astype(BF16)
    ct = jnp.swapaxes(c, 1, 2) * (1.0 / HY_N)
    st = jnp.swapaxes(s, 1, 2) * (1.0 / HY_N)
    m['hy_slab_i2'] = jnp.concatenate([ct, st], axis=-1).astype(BF16)
    nc = 2 * CTX_LEN
    c, s = _cis(ar(nc)[:, None] * ar(CTX_LEN)[None, :], nc)
    m['hc_f'] = _real_form(c, -s).astype(BF16)
    c, s = _cis(ar(CTX_LEN)[:, None] * ar(nc)[None, :], nc)
    m['hc_i'] = _real_form(c * (1.0 / nc), s * (1.0 / nc)).astype(BF16)
    c, s = _cis(ar(nc)[:, None] * ar(CTX_LEN)[None, :], nc)
    m['hc_kh'] = jnp.concatenate([c, -s], axis=0).astype(BF16)
    c, s = _cis(ar(FNET_GROUP_DIM)[:, None] * ar(FNET_GROUP_DIM)[None, :], FNET_GROUP_DIM)
    m['fn_chan'] = jnp.concatenate([c, -s], axis=1).astype(BF16)
    c, s = _cis(ar(FN_N)[:, None] * ar(FN_N)[None, :], FN_N)
    m['fn_lead'] = _real_form(c, -s).astype(BF16)
    a = ar(FN_N)[:, None, None]
    k1 = ar(FN_N)[None, :, None]
    n1 = ar(FN_N)[None, None, :]
    scale = 1.0 / math.sqrt(SEQ * FNET_GROUP_DIM)
    c, s = _cis(n1 * (a + FN_N * k1), SEQ)
    m['fn_slab'] = jnp.concatenate([c * scale, s * scale], axis=-1).astype(BF16)
    scale = 1.0 / math.sqrt(CTX_LEN * FNET_GROUP_DIM)
    c, s = _cis(ar(CTX_LEN)[:, None] * ar(CTX_LEN)[None, :], CTX_LEN)
    m['fc'] = jnp.concatenate([c * scale, s * scale], axis=-1).astype(BF16)
    return m


def _mod_kernel(c_ref, w_ref, b_ref, o_ref):
    c = c_ref[...]
    s = c * jax.nn.sigmoid(c)
    o_ref[0] = _dot(s.astype(BF16), w_ref[0].astype(BF16)) + b_ref[0]


def _modulation(c8, w_mod, b_mod):
    tn = 1536
    n = N_MOD * D_MODEL
    return pl.pallas_call(
        _mod_kernel,
        grid=(DEPTH, n // tn),
        in_specs=[pl.BlockSpec((8, D_MODEL), lambda l, j: (0, 0)),
                  pl.BlockSpec((1, D_MODEL, tn), lambda l, j: (l, 0, j)),
                  pl.BlockSpec((1, 1, tn), lambda l, j: (l, 0, j))],
        out_specs=pl.BlockSpec((1, 8, tn), lambda l, j: (l, 0, j)),
        out_shape=jax.ShapeDtypeStruct((DEPTH, 8, n), F32),
        compiler_params=_cparams(("parallel", "parallel")),
        name="adaln_modulation",
    )(c8, w_mod, b_mod.reshape(DEPTH, 1, n))


def _mod_row(tm):
    tiles_per_batch = SEQ // tm
    return lambda i: jnp.minimum(i // tiles_per_batch, BATCH)


def _rms_mod(x, g, sh, sc):
    y = x * lax.rsqrt(jnp.mean(x * x, axis=-1, keepdims=True) + EPS)
    return (y * g) * (1.0 + sc) + sh


def _norm_proj_kernel(x_ref, g_ref, sh_ref, sc_ref, w_ref, o_ref):
    h = _rms_mod(x_ref[...], g_ref[...], sh_ref[0, 0], sc_ref[0, 0]).astype(BF16)
    o_ref[...] = _dot(h, w_ref[...])


def _norm_proj(x, gain, mod4, w, n_tok):
    tm = TM
    n_out = w.shape[1]
    tn = n_out // 2
    row = _mod_row(tm)
    return pl.pallas_call(
        _norm_proj_kernel,
        grid=(n_out // tn, n_tok // tm),
        in_specs=[pl.BlockSpec((tm, D_MODEL), lambda j, i: (i, 0)),
                  pl.BlockSpec((1, D_MODEL), lambda j, i: (0, 0)),
                  pl.BlockSpec((1, 1, 1, D_MODEL), lambda j, i: (row(i), 0, 0, 0)),
                  pl.BlockSpec((1, 1, 1, D_MODEL), lambda j, i: (row(i), 1, 0, 0)),
                  pl.BlockSpec((D_MODEL, tn), lambda j, i: (0, j))],
        out_specs=pl.BlockSpec((tm, tn), lambda j, i: (i, j)),
        out_shape=jax.ShapeDtypeStruct((T_ALL, n_out), F32),
        compiler_params=_cparams(("parallel", "parallel")),
        name="norm_in_proj",
    )(x, gain, mod4, mod4, w)


def _softmax_pv(qh, k_parts, v_parts, masks, sink):
    nt = (((1,), (1,)), ((), ()))
    scores = []
    for kp, mk in zip(k_parts, masks):
        s = lax.dot_general(qh, kp, nt, preferred_element_type=F32)
        if mk is not None:
            s = jnp.where(mk, s, -1e30)
        scores.append(s)
    m = sink
    for s in scores:
        m = jnp.maximum(m, jnp.max(s, axis=-1, keepdims=True))
    es = [jnp.exp(s - m) for s in scores]
    den = jnp.exp(sink - m)
    for e in es:
        den = den + jnp.sum(e, axis=-1, keepdims=True)
    inv = 1.0 / den
    o = None
    for e, vp in zip(es, v_parts):
        t = _dot((e * inv).astype(BF16), vp)
        o = t if o is None else o + t
    return o


def _attn_kernel(sink_ref, q_ref, km_ref, k0_ref, kp_ref, vm_ref, v0_ref, vp_ref, kc_ref, vc_ref,
                 cos_ref, sin_ref, o_ref, *, nb):
    n = pl.program_id(1)
    blk = ATTN_BLOCK
    lane = lax.broadcasted_iota(jnp.int32, (blk, 128), 1)
    first = (lane % 32) < 16

    def rope(x, blk_idx):
        r0 = pl.multiple_of(blk_idx * blk, blk)
        c = cos_ref[pl.ds(r0, blk), :]
        s = sin_ref[pl.ds(r0, blk), :]
        sw = jnp.where(first, pltpu.roll(x, 112, 1), pltpu.roll(x, 16, 1))
        return x * c + sw * s

    nm = jnp.maximum(n - 1, 0)
    npl = jnp.minimum(n + 1, nb - 1)
    kall = jnp.concatenate([rope(km_ref[...], nm), rope(k0_ref[...], n), rope(kp_ref[...], npl), kc_ref[...]],
                           axis=0)
    vall = jnp.concatenate([vm_ref[...], v0_ref[...], vp_ref[...], vc_ref[...]], axis=0)
    nk = kall.shape[0]
    kswap = pltpu.roll(kall, HEAD_DIM, 1)
    vswap = pltpu.roll(vall, HEAD_DIM, 1)
    lo = lax.broadcasted_iota(jnp.int32, (nk, 128), 1) < HEAD_DIM

    r = lax.broadcasted_iota(jnp.int32, (2 * blk, blk), 0) % blk
    cidx = lax.broadcasted_iota(jnp.int32, (2 * blk, blk), 1)
    ok_prev = jnp.where(cidx >= r, (n > 0).astype(jnp.int32), 0) > 0
    ok_next = jnp.where(cidx <= r, (n < nb - 1).astype(jnp.int32), 0) > 0
    top_rows = lax.broadcasted_iota(jnp.int32, (2 * blk, 1), 0) < blk
    neg = jnp.float32(-1e30)

    scale = HEAD_DIM ** -0.5
    q2 = [(rope(q_ref[:, p * 128:(p + 1) * 128], n) * scale).astype(BF16) for p in range(N_Q_HEADS // 2)]
    nt = (((1,), (1,)), ((), ()))
    for h in range(N_KV_HEADS):
        ka, kb = (kall, kswap) if h == 0 else (kswap, kall)
        va, vb = (vall, vswap) if h == 0 else (vswap, vall)
        kbd = jnp.concatenate([jnp.where(lo, ka, 0.0), jnp.where(lo, 0.0, kb)], axis=0).astype(BF16)
        vbd = jnp.concatenate([jnp.where(lo, va, 0.0), jnp.where(lo, 0.0, vb)], axis=0).astype(BF16)
        q4 = jnp.concatenate([q2[2 * h], q2[2 * h + 1]], axis=0)
        s = lax.dot_general(q4, kbd, nt, preferred_element_type=F32)
        probs = []
        for c in range(2):
            base = c * nk
            sink = jnp.where(top_rows, sink_ref[Q_PER_KV * h + c], sink_ref[Q_PER_KV * h + 2 + c])
            parts = [jnp.where(ok_prev, s[:, base:base + blk], neg),
                     s[:, base + blk:base + 2 * blk],
                     jnp.where(ok_next, s[:, base + 2 * blk:base + 3 * blk], neg),
                     s[:, base + 3 * blk:base + nk]]
            m = sink
            for part in parts:
                m = jnp.maximum(m, jnp.max(part, axis=-1, keepdims=True))
            es = [jnp.exp(part - m) for part in parts]
            den = jnp.exp(sink - m)
            for e in es:
                den = den + jnp.sum(e, axis=-1, keepdims=True)
            inv = 1.0 / den
            probs += [(e * inv).astype(BF16) for e in es]
        o = _dot(jnp.concatenate(probs, axis=1), vbd)
        w0 = h * Q_PER_KV * HEAD_DIM
        o_ref[:, w0:w0 + 128] = o[:blk].astype(o_ref.dtype)
        o_ref[:, w0 + 128:w0 + 256] = o[blk:].astype(o_ref.dtype)


def _ctx_attn_kernel(sink_ref, q_ref, kc_ref, vc_ref, o_ref):
    kc = kc_ref[...].astype(BF16)
    vc = vc_ref[...].astype(BF16)
    scale = HEAD_DIM ** -0.5
    outs = []
    for pair in range(N_Q_HEADS // 2):
        q2 = (q_ref[:, pair * 128:(pair + 1) * 128] * scale).astype(BF16)
        for sub in range(2):
            head = 2 * pair + sub
            kvh = head // Q_PER_KV
            sl = slice(kvh * HEAD_DIM, (kvh + 1) * HEAD_DIM)
            qh = q2[:, sub * HEAD_DIM:(sub + 1) * HEAD_DIM]
            outs.append(_softmax_pv(qh, [kc[:, sl]], [vc[:, sl]], [None], sink_ref[head]))
    o_ref[...] = jnp.concatenate(outs, axis=-1).astype(o_ref.dtype)


def _attention(p, sink, cos_t, sin_t, with_ctx):
    blk = ATTN_BLOCK
    nb = SEQ // blk
    qc, kcol, vcol = PQ_OFF // ATTN_WIDTH, PK_OFF // KV_WIDTH, PV_OFF // KV_WIDTH
    ctx_blk = T_LAT // CTX_LEN
    smem = pl.BlockSpec(memory_space=pltpu.SMEM)

    def kv_spec(col, d):
        return pl.BlockSpec((blk, KV_WIDTH),
                            lambda b, n: (b * nb + jnp.clip(n + d, 0, nb - 1), col))

    ya = pl.pallas_call(
        functools.partial(_attn_kernel, nb=nb),
        grid=(BATCH, nb),
        in_specs=[smem,
                  pl.BlockSpec((blk, ATTN_WIDTH), lambda b, n: (b * nb + n, qc)),
                  kv_spec(kcol, -1), kv_spec(kcol, 0), kv_spec(kcol, 1),
                  kv_spec(vcol, -1), kv_spec(vcol, 0), kv_spec(vcol, 1),
                  pl.BlockSpec((CTX_LEN, KV_WIDTH), lambda b, n: (ctx_blk + b, kcol)),
                  pl.BlockSpec((CTX_LEN, KV_WIDTH), lambda b, n: (ctx_blk + b, vcol)),
                  pl.BlockSpec((SEQ, KV_WIDTH), lambda b, n: (0, 0)),
                  pl.BlockSpec((SEQ, KV_WIDTH), lambda b, n: (0, 0))],
        out_specs=pl.BlockSpec((blk, ATTN_WIDTH), lambda b, n: (b * nb + n, 0)),
        out_shape=jax.ShapeDtypeStruct((T_ALL, ATTN_WIDTH), BF16),
        compiler_params=_cparams(("parallel", "parallel")),
        name="banded_attention",
    )(sink, p, p, p, p, p, p, p, p, p, cos_t, sin_t)
    if not with_ctx:
        return ya
    cb = CTX_LEN // blk
    lat_blk = T_LAT // blk

    def alias_kernel(sink_ref, q_ref, kc_ref, vc_ref, ya_in_ref, o_ref):
        del ya_in_ref
        _ctx_attn_kernel(sink_ref, q_ref, kc_ref, vc_ref, o_ref)

    return pl.pallas_call(
        alias_kernel,
        grid=(BATCH, cb),
        in_specs=[smem,
                  pl.BlockSpec((blk, ATTN_WIDTH), lambda b, n: (lat_blk + b * cb + n, qc)),
                  pl.BlockSpec((CTX_LEN, KV_WIDTH), lambda b, n: (ctx_blk + b, kcol)),
                  pl.BlockSpec((CTX_LEN, KV_WIDTH), lambda b, n: (ctx_blk + b, vcol)),
                  pl.BlockSpec(memory_space=pl.ANY)],
        out_specs=pl.BlockSpec((blk, ATTN_WIDTH), lambda b, n: (lat_blk + b * cb + n, 0)),
        out_shape=jax.ShapeDtypeStruct((T_ALL, ATTN_WIDTH), BF16),
        input_output_aliases={4: 0},
        compiler_params=_cparams(("parallel", "parallel")),
        name="context_attention",
    )(sink, p, p, p, ya)


def _rope_tables():
    n_freq = HEAD_DIM // 4
    freqs = ROPE_BASE ** (-jnp.arange(n_freq, dtype=F32) / n_freq)
    t = jnp.arange(SEQ, dtype=jnp.int32)
    rows = (t // GRID_W).astype(F32)[:, None] * freqs
    cols = (t % GRID_W).astype(F32)[:, None] * freqs
    cos_h = jnp.concatenate([jnp.cos(rows), jnp.cos(rows), jnp.cos(cols), jnp.cos(cols)], axis=-1)
    sin_h = jnp.concatenate([-jnp.sin(rows), jnp.sin(rows), -jnp.sin(cols), jnp.sin(cols)], axis=-1)
    return jnp.tile(cos_h, (1, 2)), jnp.tile(sin_h, (1, 2))


def _lead_kernel(*refs, n_in, cmul, epi):
    m_ref = refs[0]
    x_refs = refs[1:1 + n_in]
    pos = 1 + n_in
    xs = []
    for r in x_refs:
        v = r[...]
        xs.append(v.reshape(-1, v.shape[-1]))
    x = xs[0] if n_in == 1 else jnp.concatenate(xs, axis=0)
    if cmul:
        k = refs[pos][...]
        pos += 1
        k = k.reshape(-1, k.shape[-1])
        half = x.shape[0] // 2
        xr, xi, kr, ki = x[:half], x[half:], k[:half], k[half:]
        x = jnp.concatenate([xr * kr - xi * ki, xr * ki + xi * kr], axis=0)
    res = _dot(m_ref[...], x.astype(BF16))
    if epi:
        g_ref, y_ref, b_ref = refs[pos:pos + 3]
        pos += 3
        g = g_ref[...]
        y = y_ref[...]
        res = g.reshape(-1, g.shape[-1]) * (res + y.reshape(-1, y.shape[-1]) * b_ref[...])
    o_ref = refs[pos]
    o_ref[...] = res.reshape(o_ref.shape).astype(o_ref.dtype)


def _lead(mat, xs, x_specs, grid, out_shape, out_spec, *, kspec=None, epi=None, alias_to=None, name):
    ins = [mat] + list(xs)
    specs = [pl.BlockSpec(mat.shape, lambda *a: (0, 0))] + list(x_specs)
    if kspec is not None:
        ins.append(kspec[0])
        specs.append(kspec[1])
    if epi is not None:
        for arr, sp in epi:
            ins.append(arr)
            specs.append(sp)
    kern = functools.partial(_lead_kernel, n_in=len(xs), cmul=kspec is not None, epi=epi is not None)
    aliases = {}
    if alias_to is not None:
        aliases = {len(ins): 0}
        ins.append(alias_to)
        specs.append(pl.BlockSpec(memory_space=pl.ANY))
        inner = kern

        def kern(*refs):
            inner(*refs[:-2], refs[-1])

    return pl.pallas_call(
        kern, grid=grid, in_specs=specs, out_specs=out_spec, out_shape=out_shape,
        input_output_aliases=aliases,
        compiler_params=_cparams(("parallel",) * len(grid)), name=name,
    )(*ins)


def _cstack(xr, xi):
    return jnp.concatenate([jnp.concatenate([xr, xi], axis=1), jnp.concatenate([-xi, xr], axis=1)], axis=0)


def _fnet_kernel(u_ref, mc_ref, ml_ref, ms_ref, o_ref, zr, zi, are, aim):
    n = FN_N
    pitch = PITCH
    gd = FNET_GROUP_DIM
    rows = 4 * n
    mc = mc_ref[...]

    def chan(i, carry):
        r_in = pl.multiple_of(i * rows, rows)
        z = _dot(u_ref[pl.ds(r_in, rows), :].astype(BF16), mc)
        for q in range(rows // n):
            r_out = pl.multiple_of((i * (rows // n) + q) * pitch, 8)
            zr[pl.ds(r_out, n), :] = z[q * n:(q + 1) * n, :gd]
            zi[pl.ds(r_out, n), :] = z[q * n:(q + 1) * n, gd:]
        return carry

    lax.fori_loop(0, SEQ // rows, chan, 0)
    ml = ml_ref[...]

    def lead(i, carry):
        for u in range(UNROLL):
            n1 = i * UNROLL + u
            x = jnp.concatenate([zr[pl.ds(n1, n, stride=pitch), :], zi[pl.ds(n1, n, stride=pitch), :]],
                                axis=0).astype(BF16)
            r = _dot(ml, x)
            are[pl.ds(n1, n, stride=pitch), :] = r[:n]
            aim[pl.ds(n1, n, stride=pitch), :] = r[n:]
        return carry

    lax.fori_loop(0, n // UNROLL, lead, 0)

    def slab(i, carry):
        for u in range(UNROLL):
            k2 = i * UNROLL + u
            r0 = pl.multiple_of(k2 * pitch, 8)
            x = jnp.concatenate([are[pl.ds(r0, n), :], aim[pl.ds(r0, n), :]], axis=0).astype(BF16)
            o_ref[pl.ds(k2, n, stride=n), :] = _dot(ms_ref[k2], x)
        return carry

    lax.fori_loop(0, n // UNROLL, slab, 0)


def _fnet_ctx_kernel(u_ref, mc_ref, mf_ref, yf_in_ref, o_ref):
    del yf_in_ref
    gd = FNET_GROUP_DIM
    mc = mc_ref[...]
    mf = mf_ref[...]
    for g in range(FNET_GROUPS):
        z = _dot(u_ref[:, g * gd:(g + 1) * gd].astype(BF16), mc)
        x = jnp.concatenate([z[:, :gd], z[:, gd:]], axis=0).astype(BF16)
        o_ref[:, g * gd:(g + 1) * gd] = _dot(mf, x)


def _fourier_mix(p, mats, with_ctx):
    gd = FNET_GROUP_DIM
    col0 = PF_OFF // gd
    full = lambda a: pl.BlockSpec(a.shape, lambda *i: (0,) * a.ndim)
    scr = pltpu.VMEM((FN_N * PITCH, gd), F32)
    yf = pl.pallas_call(
        _fnet_kernel,
        grid=(BATCH, FNET_GROUPS),
        in_specs=[pl.BlockSpec((SEQ, gd), lambda b, g: (b, col0 + g)),
                  full(mats['fn_chan']), full(mats['fn_lead']), full(mats['fn_slab'])],
        out_specs=pl.BlockSpec((SEQ, gd), lambda b, g: (b, g)),
        out_shape=jax.ShapeDtypeStruct((T_ALL if with_ctx else T_LAT, FNET_WIDTH), F32),
        scratch_shapes=[scr, scr, scr, scr],
        compiler_params=_cparams(("parallel", "parallel")),
        name="fnet_latent",
    )(p, mats['fn_chan'], mats['fn_lead'], mats['fn_slab'])
    if not with_ctx:
        return yf
    blk0 = T_LAT // CTX_LEN
    return pl.pallas_call(
        _fnet_ctx_kernel,
        grid=(BATCH,),
        in_specs=[pl.BlockSpec((CTX_LEN, FNET_WIDTH), lambda b: (blk0 + b, PF_OFF // FNET_WIDTH)),
                  full(mats['fn_chan']), full(mats['fc']), pl.BlockSpec(memory_space=pl.ANY)],
        out_specs=pl.BlockSpec((CTX_LEN, FNET_WIDTH), lambda b: (blk0 + b, 0)),
        out_shape=jax.ShapeDtypeStruct((T_ALL, FNET_WIDTH), F32),
        input_output_aliases={3: 0},
        compiler_params=_cparams(("parallel",)),
        name="fnet_ctx",
    )(p, mats['fn_chan'], mats['fc'], yf)


def _short_conv_kernel(u_ref, w_ref, b_ref, o_ref, *, rows, chunk):
    w0 = w_ref[0:1, :]
    w1 = w_ref[1:2, :]
    w2 = w_ref[2:3, :]
    bias = b_ref[...]
    width = u_ref.shape[-1]
    ridx = lax.broadcasted_iota(jnp.int32, (chunk, width), 0)
    n_chunks = rows // chunk
    for ci in range(n_chunks):
        r0 = ci * chunk
        cur = u_ref[r0:r0 + chunk, :]
        if ci > 0:
            prev_row = u_ref[r0 - 8:r0, :][7:8, :]
        else:
            prev_row = jnp.zeros((1, width), F32)
        if ci < n_chunks - 1:
            next_row = u_ref[r0 + chunk:r0 + chunk + 8, :][0:1, :]
        else:
            next_row = jnp.zeros((1, width), F32)
        up = jnp.where(ridx == 0, prev_row, pltpu.roll(cur, 1, 0))
        dn = jnp.where(ridx == chunk - 1, next_row, pltpu.roll(cur, chunk - 1, 0))
        o_ref[0, r0:r0 + chunk, :] = up * w0 + cur * w1 + dn * w2 + bias


def _short_conv(p, conv_w, conv_b, with_ctx):
    cw = 256
    hw = (HYENA_ORDER + 1) * HYENA_WIDTH
    ncol = hw // cw
    per = HYENA_WIDTH // cw
    col0 = PH_OFF // cw
    out_shape = jax.ShapeDtypeStruct((HYENA_ORDER + 1, T_ALL if with_ctx else T_LAT, HYENA_WIDTH), F32)
    b2 = conv_b.reshape(1, hw)

    def call(rows, blk0, alias):
        kern = functools.partial(_short_conv_kernel, rows=rows, chunk=min(rows, 256))
        ins = [p, conv_w, b2]
        specs = [pl.BlockSpec((rows, cw), lambda b, j: (blk0 + b, col0 + j)),
                 pl.BlockSpec((3, cw), lambda b, j: (0, j)),
                 pl.BlockSpec((1, cw), lambda b, j: (0, j))]
        aliases = {}
        if alias is not None:
            ins.append(alias)
            specs.append(pl.BlockSpec(memory_space=pl.ANY))
            aliases = {3: 0}
            inner = kern

            def kern(u_ref, w_ref, b_ref, a_ref, o_ref):
                del a_ref
                inner(u_ref, w_ref, b_ref, o_ref)

        return pl.pallas_call(
            kern, grid=(BATCH, ncol), in_specs=specs,
            out_specs=pl.BlockSpec((1, rows, cw), lambda b, j: (j // per, blk0 + b, j % per)),
            out_shape=out_shape, input_output_aliases=aliases,
            compiler_params=_cparams(("parallel", "parallel")), name="hyena_short_conv",
        )(*ins)

    z3 = call(SEQ, 0, None)
    if with_ctx:
        z3 = call(CTX_LEN, T_LAT // CTX_LEN, z3)
    return z3


def _filter_mlp_kernel(ft_ref, w1_ref, b1_ref, fq_ref, w2_ref, b2_ref, o_ref):
    fq = fq_ref[...]
    h = jnp.sin(fq * (_dot(ft_ref[...].astype(BF16), w1_ref[...]) + b1_ref[...]))
    h = jnp.sin(fq * (_dot(h.astype(BF16), w2_ref[...]) + b2_ref[...]))
    o_ref[...] = h.astype(o_ref.dtype)


def _filter_kernel(h_ref, t_ref, w3f_ref, w3b_ref, dl_ref, m1_ref, *rest, n, dense):
    hb = h_ref[...]
    decay = jnp.exp(-t_ref[...] * dl_ref[...])
    tf = _dot(hb, w3f_ref[...]) * decay
    tb = _dot(hb, w3b_ref[...]) * decay
    tb = jnp.where(lax.broadcasted_iota(jnp.int32, tb.shape, 0) == 0, 0.0, tb)
    scale = 1.0 / (jnp.sum(jnp.abs(tf), axis=0, keepdims=True) + jnp.sum(jnp.abs(tb), axis=0, keepdims=True))
    cw = tf.shape[1]
    if dense:
        o_ref = rest[0]
        r = _dot(m1_ref[...], jnp.concatenate([tf, tb], axis=1).astype(BF16))
        nc = r.shape[0] // 2
        o_ref[0:nc, :] = (r[:nc, :cw] + r[:nc, cw:]) * scale
        o_ref[nc:, :] = (r[nc:, :cw] - r[nc:, cw:]) * scale
        return
    f2_ref, o_ref, tf_s, tb_s, are_f, aim_f, are_b, aim_b = rest
    tf_s[...] = tf
    tb_s[...] = tb
    half = HY_N1 // 2
    m1 = m1_ref[...]

    def stage1(i, carry):
        for u in range(UNROLL):
            n2 = i * UNROLL + u
            x = jnp.concatenate([tf_s[pl.ds(n2, half, stride=HY_N2), :], tb_s[pl.ds(n2, half, stride=HY_N2), :]],
                                axis=1).astype(BF16)
            r = _dot(m1, x)
            rows = pl.ds(n2, HY_N1, stride=PITCH)
            are_f[rows, :] = r[:HY_N1, :cw]
            are_b[rows, :] = r[:HY_N1, cw:]
            aim_f[rows, :] = r[HY_N1:, :cw]
            aim_b[rows, :] = r[HY_N1:, cw:]
        return carry

    lax.fori_loop(0, HY_N2 // UNROLL, stage1, 0)

    def stage2(i, carry):
        for u in range(UNROLL):
            k1 = i * UNROLL + u
            rows = pl.ds(pl.multiple_of(k1 * PITCH, 8), HY_N2)
            ar = jnp.concatenate([are_f[rows, :], are_b[rows, :]], axis=1)
            ai = jnp.concatenate([aim_f[rows, :], aim_b[rows, :]], axis=1)
            t = _dot(f2_ref[k1], _cstack(ar, ai).astype(BF16))
            o_ref[0, k1] = (t[:, 0:cw] + t[:, cw:2 * cw]) * scale
            o_ref[1, k1] = (t[:, 2 * cw:3 * cw] - t[:, 3 * cw:4 * cw]) * scale
        return carry

    lax.fori_loop(0, HY_N1 // UNROLL, stage2, 0)


def _filter_feats(n):
    pos = jnp.arange(n, dtype=F32)
    t = pos / max(n - 1, 1)
    omega = 2.0 * math.pi * pos / n
    bands = jnp.linspace(1e-4, FILTER_BANDS - 1, FILTER_BANDS, dtype=F32)
    feats = jnp.concatenate([t[:, None], jnp.cos(omega[:, None] * bands), -jnp.sin(omega[:, None] * bands)], axis=-1)
    return jnp.pad(feats, ((0, 0), (0, 128 - FILTER_EMB))), t[:, None]


def _filter_spectrum(n, filt, mats):
    w1, b1, freq, w2, b2, w3 = filt
    dense = n == CTX_LEN
    cw = 128
    nch = HYENA_WIDTH // cw
    feats, t = _filter_feats(n)
    w1p = jnp.pad(w1, ((0, 128 - FILTER_EMB), (0, 0))).astype(BF16)
    deltas = jnp.abs(jnp.linspace(math.log(DECAY_TARGET) / SLOW_DECAY_PCT, math.log(DECAY_TARGET) / FAST_DECAY_PCT,
                                  HYENA_WIDTH, dtype=F32)).reshape(1, HYENA_WIDTH)
    full = lambda a: pl.BlockSpec(a.shape, lambda *i: (0,) * a.ndim)
    row = lambda a: a.reshape(1, -1)
    w3b16 = w3.astype(BF16)
    tap_spec = lambda d: pl.BlockSpec((FILTER_HIDDEN, cw), lambda o, ch: (0, (o * 2 + d) * nch + ch))
    m1 = mats['hc_kh'] if dense else mats['hy_lead_kh']
    mlp_ins = [feats, w1p, row(b1), row(freq), w2.astype(BF16), row(b2)]
    hb = pl.pallas_call(
        _filter_mlp_kernel, grid=(1,), in_specs=[full(a) for a in mlp_ins],
        out_specs=pl.BlockSpec((n, FILTER_HIDDEN), lambda i: (0, 0)),
        out_shape=jax.ShapeDtypeStruct((n, FILTER_HIDDEN), BF16),
        compiler_params=_cparams(("arbitrary",)), name="hyena_filter_mlp",
    )(*mlp_ins)
    ins = [hb, t, w3b16, w3b16, deltas, m1]
    specs = [full(hb), full(t), tap_spec(0), tap_spec(1), pl.BlockSpec((1, cw), lambda o, ch: (0, ch)), full(m1)]
    if dense:
        nc = 2 * n
        out_shape = jax.ShapeDtypeStruct((HYENA_ORDER, 2 * nc, HYENA_WIDTH), F32)
        out_spec = pl.BlockSpec((None, 2 * nc, cw), lambda o, ch: (o, 0, ch))
        scratch = []
    else:
        ins.append(mats['hy_slab_f2'])
        specs.append(full(mats['hy_slab_f2']))
        out_shape = jax.ShapeDtypeStruct((HYENA_ORDER, 2, HY_N1, HY_N2, HYENA_WIDTH), F32)
        out_spec = pl.BlockSpec((None, 2, HY_N1, HY_N2, cw), lambda o, ch: (o, 0, 0, 0, ch))
        scratch = [pltpu.VMEM((n, cw), F32)] * 2 + [pltpu.VMEM((HY_N1 * PITCH, cw), F32)] * 4
    return pl.pallas_call(
        functools.partial(_filter_kernel, n=n, dense=dense),
        grid=(HYENA_ORDER, nch), in_specs=specs, out_specs=out_spec, out_shape=out_shape,
        scratch_shapes=scratch,
        compiler_params=_cparams(("parallel", "parallel")),
        name="hyena_filter_ctx" if dense else "hyena_filter",
    )(*ins)


def _hyena_conv_kernel(y_ref, g_ref, k_ref, m1_ref, f2f_ref, f2i_ref, m3_ref, b_ref, o_ref, are, aim):
    half = HY_N1 // 2
    m1 = m1_ref[...]

    def stage1(i, carry):
        for u in range(UNROLL):
            n2 = i * UNROLL + u
            x = jnp.concatenate([y_ref[pl.ds(n2, half, stride=HY_N2), :],
                                 y_ref[pl.ds(SEQ + n2, half, stride=HY_N2), :]], axis=0).astype(BF16)
            r = _dot(m1, x)
            are[pl.ds(n2, HY_N1, stride=PITCH), :] = r[:HY_N1]
            aim[pl.ds(n2, HY_N1, stride=PITCH), :] = r[HY_N1:]
        return carry

    lax.fori_loop(0, HY_N2 // UNROLL, stage1, 0)
    cw = o_ref.shape[-1]

    def stage2(i, carry):
        for u in range(UNROLL):
            k1 = i * UNROLL + u
            r0 = pl.multiple_of(k1 * PITCH, 8)
            y = _dot(f2f_ref[k1], _cstack(are[pl.ds(r0, HY_N2), :], aim[pl.ds(r0, HY_N2), :]).astype(BF16))
            yr, yi = y[:, :cw], y[:, cw:]
            kr, ki = k_ref[0, k1], k_ref[1, k1]
            w = _dot(f2i_ref[k1], _cstack(yr * kr - yi * ki, yr * ki + yi * kr).astype(BF16))
            are[pl.ds(r0, HY_N2), :] = w[:, :cw]
            aim[pl.ds(r0, HY_N2), :] = w[:, cw:]
        return carry

    lax.fori_loop(0, HY_N1 // UNROLL, stage2, 0)
    m3 = m3_ref[...]
    bias = b_ref[...]

    def stage3(i, carry):
        for u in range(UNROLL):
            n2 = i * UNROLL + u
            bn = jnp.concatenate([are[pl.ds(n2, HY_N1, stride=PITCH), :], aim[pl.ds(n2, HY_N1, stride=PITCH), :]],
                                 axis=0).astype(BF16)
            y = _dot(m3, bn)
            for b in range(2):
                rows = pl.ds(b * SEQ + n2, half, stride=HY_N2)
                o_ref[rows, :] = g_ref[rows, :] * (y[b * half:(b + 1) * half] + y_ref[rows, :] * bias)
        return carry

    lax.fori_loop(0, HY_N2 // UNROLL, stage3, 0)


def _hyena_mix(z3, filt, hyena_bias, mats, with_ctx):
    c = HYENA_WIDTH
    cw = 128
    nch = c // cw
    pairs = BATCH // 2
    full = lambda a: pl.BlockSpec(a.shape, lambda *i: (0,) * a.ndim)
    kspec = _filter_spectrum(SEQ, filt, mats)
    if with_ctx:
        kspec_c = _filter_spectrum(CTX_LEN, filt, mats)
    bias3 = hyena_bias.reshape(HYENA_ORDER, 1, c)
    scr = pltpu.VMEM((HY_N1 * PITCH, cw), F32)
    y = None
    for o in range(HYENA_ORDER):
        if y is None:
            xin, xspec = z3, pl.BlockSpec((None, 2 * SEQ, cw), lambda b, ch: (0, b, ch))
        else:
            xin, xspec = y, pl.BlockSpec((2 * SEQ, cw), lambda b, ch: (b, ch))
        ynew = pl.pallas_call(
            _hyena_conv_kernel,
            grid=(pairs, nch),
            in_specs=[xspec,
                      pl.BlockSpec((None, 2 * SEQ, cw), lambda b, ch, o=o: (o + 1, b, ch)),
                      pl.BlockSpec((None, 2, HY_N1, HY_N2, cw), lambda b, ch, o=o: (o, 0, 0, 0, ch)),
                      full(mats['hy_lead_f']), full(mats['hy_slab_f2']), full(mats['hy_slab_i2']),
                      full(mats['hy_lead_i']),
                      pl.BlockSpec((None, 1, cw), lambda b, ch, o=o: (o, 0, ch))],
            out_specs=pl.BlockSpec((2 * SEQ, cw), lambda b, ch: (b, ch)),
            out_shape=jax.ShapeDtypeStruct((T_ALL if with_ctx else T_LAT, c), F32),
            scratch_shapes=[scr, scr],
            compiler_params=_cparams(("parallel", "parallel"), 58 * 1024 * 1024),
            name="hyena_conv",
        )(xin, z3, kspec, mats['hy_lead_f'], mats['hy_slab_f2'], mats['hy_slab_i2'], mats['hy_lead_i'], bias3)
        if with_ctx:
            nc = 2 * CTX_LEN
            blk0 = T_LAT // nc
            if y is None:
                cin, cspec_in = z3, pl.BlockSpec((1, nc, c), lambda b: (0, blk0 + b, 0))
            else:
                cin, cspec_in = y, pl.BlockSpec((nc, c), lambda b: (blk0 + b, 0))
            xc = _lead(mats['hc_f'], [cin], [cspec_in], (pairs,),
                       jax.ShapeDtypeStruct((pairs, 2 * nc, c), F32),
                       pl.BlockSpec((1, 2 * nc, c), lambda b: (b, 0, 0)), name="hyena_ctx_fwd")
            ynew = _lead(mats['hc_i'], [xc], [pl.BlockSpec((1, 2 * nc, c), lambda b: (b, 0, 0))],
                         (pairs,), jax.ShapeDtypeStruct((T_ALL, c), F32),
                         pl.BlockSpec((nc, c), lambda b: (blk0 + b, 0)),
                         kspec=(kspec_c, pl.BlockSpec((1, 2 * nc, c), lambda b, o=o: (o, 0, 0))),
                         epi=[(z3, pl.BlockSpec((1, nc, c), lambda b, o=o: (o + 1, blk0 + b, 0))),
                              (cin, cspec_in),
                              (bias3, pl.BlockSpec((1, 1, c), lambda b, o=o: (o, 0, 0)))],
                         alias_to=ynew, name="hyena_ctx_inv")
        y = ynew
    return y


def _route_tile(lt, br, base, tri):
    tm = lt.shape[1]
    aff = jax.nn.sigmoid(lt)
    biased = aff + br
    b = [biased[e:e + 1, :] for e in range(N_EXPERTS)]
    a = [aff[e:e + 1, :] for e in range(N_EXPERTS)]
    epg = EXPERTS_PER_GROUP
    scores = []
    for g in range(N_GROUPS):
        x0, x1, x2, x3 = b[epg * g:epg * g + epg]
        s1, t1 = jnp.maximum(x0, x1), jnp.minimum(x0, x1)
        s2, t2 = jnp.maximum(x2, x3), jnp.minimum(x2, x3)
        scores.append(jnp.maximum(s1, s2) + jnp.maximum(jnp.minimum(s1, s2), jnp.maximum(t1, t2)))
    best = scores[0]
    gsel = jnp.zeros((1, tm), jnp.int32)
    for g in range(1, N_GROUPS):
        gsel = jnp.where(scores[g] > best, g, gsel)
        best = jnp.maximum(best, scores[g])

    def pick(rows, j):
        out = rows[j]
        for g in range(1, N_GROUPS):
            out = jnp.where(gsel == g, rows[epg * g + j], out)
        return out

    v = [pick(b, j) for j in range(epg)]
    av = [pick(a, j) for j in range(epg)]
    i1 = jnp.zeros((1, tm), jnp.int32)
    m1 = v[0]
    for j in range(1, epg):
        i1 = jnp.where(v[j] > m1, j, i1)
        m1 = jnp.maximum(m1, v[j])
    neg = jnp.float32(-3.0e38)
    i2 = jnp.zeros((1, tm), jnp.int32)
    m2 = jnp.full((1, tm), neg, F32)
    for j in range(epg):
        cand = jnp.where(i1 == j, neg, v[j])
        take = cand > m2
        i2 = jnp.where(take, j, i2)
        m2 = jnp.where(take, cand, m2)

    def sel(rows, idx):
        out = rows[0]
        for j in range(1, epg):
            out = jnp.where(idx == j, rows[j], out)
        return out

    a1, a2 = sel(av, i1), sel(av, i2)
    den = a1 + a2
    e1 = gsel * epg + i1
    e2 = gsel * epg + i2
    eio = lax.broadcasted_iota(jnp.int32, (N_EXPERTS, tm), 0)
    oh1 = jnp.where(eio == e1, 1.0, 0.0)
    oh2 = jnp.where(eio == e2, 1.0, 0.0)
    oh = oh1 + oh2
    tot = base + _dot(oh.astype(BF16), tri)
    r1 = jnp.sum(oh1 * tot, axis=0, keepdims=True)
    r2 = jnp.sum(oh2 * tot, axis=0, keepdims=True)
    new_base = base + jnp.sum(oh, axis=1, keepdims=True)
    return (e1, e2), (a1 / den, a2 / den), (r1.astype(jnp.int32), r2.astype(jnp.int32)), new_base


def _merge_kernel(x_ref, ya_ref, yf_ref, yh_ref, gt_ref, wa_ref, wf_ref, wh_ref, wo_ref, g1_ref,
                  gn_ref, sh_ref, sc_ref, wrt_ref, br_ref, xo_ref, h2_ref, e_ref, w_ref, r_ref, cnt_ref):
    d = D_MODEL
    merged = jax.nn.sigmoid(gt_ref[:, 0:d]) * _dot(ya_ref[...], wa_ref[...])
    merged += jax.nn.sigmoid(gt_ref[:, d:2 * d]) * _dot(yf_ref[...].astype(BF16), wf_ref[...])
    merged += jax.nn.sigmoid(gt_ref[:, 2 * d:3 * d]) * _dot(yh_ref[...].astype(BF16), wh_ref[...])
    xn = x_ref[...] + g1_ref[0, 0] * _dot(merged.astype(BF16), wo_ref[...])
    xo_ref[...] = xn
    h2f = _rms_mod(xn, gn_ref[...], sh_ref[0, 0], sc_ref[0, 0])
    h2_ref[...] = h2f
    h2 = h2f.astype(BF16)

    @pl.when(pl.program_id(0) == 0)
    def _():
        cnt_ref[...] = jnp.zeros_like(cnt_ref)

    tm = h2.shape[0]
    lt = lax.dot_general(wrt_ref[...], h2, (((1,), (1,)), ((), ())), preferred_element_type=F32)
    tri = jnp.where(lax.broadcasted_iota(jnp.int32, (tm, tm), 0) < lax.broadcasted_iota(jnp.int32, (tm, tm), 1),
                    1.0, 0.0).astype(BF16)
    es, ws, rs, new_base = _route_tile(lt, br_ref[...], cnt_ref[:, 0:1], tri)
    e_ref[0:1, :], e_ref[1:2, :] = es
    w_ref[0:1, :], w_ref[1:2, :] = ws
    r_ref[0:1, :], r_ref[1:2, :] = rs
    cnt_ref[...] = jnp.broadcast_to(new_base, cnt_ref.shape)


def _merge(x, ya, yf, yh, p, wa, wf, wh, wo, mod4, gain2, wrt, br, n_tok):
    tm = TM
    row = _mod_row(tm)
    full = lambda a: pl.BlockSpec(a.shape, lambda i: (0,) * a.ndim)
    modspec = lambda k: pl.BlockSpec((1, 1, 1, D_MODEL), lambda i: (row(i), k, 0, 0))
    tok = lambda w: pl.BlockSpec((tm, w), lambda i: (i, 0))
    lane = pl.BlockSpec((TOP_K, tm), lambda i: (0, i))
    return pl.pallas_call(
        _merge_kernel,
        grid=(n_tok // tm,),
        in_specs=[tok(D_MODEL), tok(ATTN_WIDTH), tok(FNET_WIDTH), tok(HYENA_WIDTH),
                  pl.BlockSpec((tm, 3 * D_MODEL), lambda i: (i, PG_OFF // (3 * D_MODEL))),
                  full(wa), full(wf), full(wh), full(wo), modspec(2), full(gain2), modspec(3), modspec(4),
                  full(wrt), full(br)],
        out_specs=[tok(D_MODEL), tok(D_MODEL), lane, lane, lane,
                   pl.BlockSpec((N_EXPERTS, 128), lambda i: (0, 0))],
        out_shape=[jax.ShapeDtypeStruct((n_tok, D_MODEL), F32),
                   jax.ShapeDtypeStruct((n_tok, D_MODEL), F32),
                   jax.ShapeDtypeStruct((TOP_K, n_tok), jnp.int32),
                   jax.ShapeDtypeStruct((TOP_K, n_tok), F32),
                   jax.ShapeDtypeStruct((TOP_K, n_tok), jnp.int32),
                   jax.ShapeDtypeStruct((N_EXPERTS, 128), F32)],
        compiler_params=_cparams(("arbitrary",)),
        name="merge_out_norm_route",
    )(x, ya, yf, yh, p, wa, wf, wh, wo, mod4, gain2, mod4, mod4, wrt, br)


def _expert_kernel(be_ref, na_ref, x_ref, wg_ref, wu_ref, wd_ref, o_ref, wg_s, wu_s, wd_s):
    i = pl.program_id(0)
    active = i < na_ref[0]
    prev = be_ref[jnp.maximum(i - 1, 0)]

    @pl.when(active & ((i == 0) | (be_ref[i] != prev)))
    def _():
        wg_s[...] = wg_ref[0].astype(BF16)
        wu_s[...] = wu_ref[0].astype(BF16)
        wd_s[...] = wd_ref[0].astype(BF16)

    @pl.when(active)
    def _():
        x = x_ref[...].astype(BF16)
        g = _dot(x, wg_s[...])
        u = _dot(x, wu_s[...])
        h = (g * jax.nn.sigmoid(g)) * u
        o_ref[...] = _dot(h.astype(BF16), wd_s[...])

    @pl.when(jnp.logical_not(active))
    def _():
        o_ref[...] = jnp.zeros_like(o_ref)


def _experts(xs, blk_expert, n_active, wg, wu, wd):
    bm = EXPERT_BM
    n_blk = xs.shape[0] // bm
    wspec = lambda k, n: pl.BlockSpec((1, k, n), lambda i, be, na: (be[i], 0, 0))
    return pl.pallas_call(
        _expert_kernel,
        grid_spec=pltpu.PrefetchScalarGridSpec(
            num_scalar_prefetch=2,
            grid=(n_blk,),
            in_specs=[pl.BlockSpec((bm, D_MODEL), lambda i, be, na: (i, 0)),
                      wspec(D_MODEL, EXPERT_FF), wspec(D_MODEL, EXPERT_FF), wspec(EXPERT_FF, D_MODEL)],
            out_specs=pl.BlockSpec((bm, D_MODEL), lambda i, be, na: (i, 0)),
            scratch_shapes=[pltpu.VMEM((D_MODEL, EXPERT_FF), BF16), pltpu.VMEM((D_MODEL, EXPERT_FF), BF16),
                            pltpu.VMEM((EXPERT_FF, D_MODEL), BF16)]),
        out_shape=jax.ShapeDtypeStruct((xs.shape[0], D_MODEL), F32),
        compiler_params=_cparams(("arbitrary",)),
        name="moe_experts",
    )(blk_expert, n_active, xs, wg, wu, wd)


def _dispatch(e_idx, rank, counts, n_tok):
    bm = EXPERT_BM
    counts = counts.astype(jnp.int32)
    padded = (counts + bm - 1) // bm * bm
    pad_end = jnp.cumsum(padded)
    pad_start = pad_end - padded
    experts = jnp.arange(N_EXPERTS, dtype=jnp.int32)
    start = jnp.sum(jnp.where(e_idx[..., None] == experts, pad_start, 0), axis=-1)
    dest = start + rank
    n_rows = -(-(n_tok * TOP_K) // bm) * bm + N_EXPERTS * bm
    n_blk = n_rows // bm
    tok = jnp.tile(jnp.arange(n_tok, dtype=jnp.int32), TOP_K)
    row_tok = jnp.zeros((n_rows,), jnp.int32).at[dest.reshape(-1)].set(tok)
    blk_start = jnp.arange(n_blk, dtype=jnp.int32) * bm
    blk_expert = jnp.minimum(jnp.sum((blk_start[:, None] >= pad_end[None, :]).astype(jnp.int32), axis=-1),
                             N_EXPERTS - 1)
    n_active = (pad_end[-1] // bm).astype(jnp.int32).reshape(1)
    return row_tok, dest, blk_expert, n_active


def _combine_kernel(x_ref, y0_ref, y1_ref, w_ref, g2_ref, gf_ref, o_ref, *, final):
    w = w_ref[...]
    moe = y0_ref[...] * w[:, 0:1] + y1_ref[...] * w[:, 1:2]
    xn = x_ref[...] + g2_ref[0, 0] * moe
    if final:
        y = xn * lax.rsqrt(jnp.mean(xn * xn, axis=-1, keepdims=True) + EPS)
        xn = y * gf_ref[...]
    o_ref[...] = xn


def _combine(x, y0, y1, w_sel, mod4, gain_final, n_tok, final):
    tm = TM
    row = _mod_row(tm)
    tok = lambda w: pl.BlockSpec((tm, w), lambda i: (i, 0))
    return pl.pallas_call(
        functools.partial(_combine_kernel, final=final),
        grid=(n_tok // tm,),
        in_specs=[tok(D_MODEL), tok(D_MODEL), tok(D_MODEL), tok(TOP_K),
                  pl.BlockSpec((1, 1, 1, D_MODEL), lambda i: (row(i), 5, 0, 0)),
                  pl.BlockSpec((1, D_MODEL), lambda i: (0, 0))],
        out_specs=tok(D_MODEL),
        out_shape=jax.ShapeDtypeStruct((n_tok if final else T_ALL, D_MODEL), F32),
        compiler_params=_cparams(("parallel",)),
        name="moe_combine",
    )(x, y0, y1, w_sel, mod4, gain_final)


def kernel(x, c, ctx, c_ctx, w_mod, b_mod, norm_mix, norm_ffn, w_in, attn_sink, conv_w, conv_b, filt_w1, filt_b1, filt_freq, filt_w2, filt_b2, filt_w3, hyena_bias, w_branch_attn, w_branch_fnet, w_branch_hyena, w_out, w_router, b_router, w_exp_gate, w_exp_up, w_exp_down, norm_final):
    mats = _dft_mats()
    cos_t, sin_t = _rope_tables()
    c8 = jnp.concatenate([c, c_ctx[None, :], jnp.zeros((8 - BATCH - 1, D_MODEL), F32)], axis=0)
    mod_all = _modulation(c8, w_mod, b_mod)
    xa = jnp.concatenate([x.reshape(T_LAT, D_MODEL), ctx.reshape(T_CTX, D_MODEL)], axis=0)
    wrt = w_router.T.astype(BF16)
    br = b_router.astype(F32).reshape(N_EXPERTS, 1)
    gain_final = norm_final.reshape(1, D_MODEL)
    out = None
    for l in range(DEPTH):
        last = l == DEPTH - 1
        with_ctx = not last
        n_tok = T_LAT if last else T_ALL
        mod4 = mod_all[l].reshape(8, N_MOD, 1, D_MODEL)
        wl = w_in[l]
        w_perm = jnp.concatenate([wl[:, G_OFF:], wl[:, H_OFF:G_OFF], wl[:, F_OFF:H_OFF], wl[:, Q_OFF:F_OFF]],
                                 axis=1).astype(BF16)
        p = _norm_proj(xa, norm_mix[l].reshape(1, D_MODEL), mod4, w_perm, T_ALL)
        ya = _attention(p, attn_sink[l], cos_t, sin_t, with_ctx)
        yf = _fourier_mix(p, mats, with_ctx)
        z3 = _short_conv(p, conv_w[l], conv_b[l], with_ctx)
        filt = (filt_w1[l], filt_b1[l], filt_freq[l], filt_w2[l], filt_b2[l], filt_w3[l])
        yh = _hyena_mix(z3, filt, hyena_bias[l], mats, with_ctx)
        xa, h2, e_idx, w_sel, rank, cnt = _merge(
            xa, ya, yf, yh, p, w_branch_attn[l].astype(BF16), w_branch_fnet[l].astype(BF16),
            w_branch_hyena[l].astype(BF16), w_out[l].astype(BF16), mod4,
            norm_ffn[l].reshape(1, D_MODEL), wrt, br, n_tok)
        row_tok, dest, blk_expert, n_active = _dispatch(e_idx, rank, cnt[:, 0], n_tok)
        xs = jnp.take(h2, row_tok, axis=0, mode='clip')
        ys = _experts(xs, blk_expert, n_active, w_exp_gate[l], w_exp_up[l], w_exp_down[l])
        y0 = jnp.take(ys, dest[0], axis=0, mode='clip')
        y1 = jnp.take(ys, dest[1], axis=0, mode='clip')
        res = _combine(xa, y0, y1, w_sel.T, mod4, gain_final, n_tok, last)
        if last:
            out = res
        else:
            xa = res
    return out.reshape(BATCH, SEQ, D_MODEL)
```

```python
import functools
import math

import jax
import jax.numpy as jnp
from jax import lax
from jax.experimental import pallas as pl
from jax.experimental.pallas import tpu as pltpu

F32 = jnp.float32
BF16 = jnp.bfloat16

D_MODEL = 1024
BATCH = 4
SEQ = 4096
DEPTH = 4
GRID_W = 64
CTX_LEN = 256
EPS = 1e-6
N_MOD = 6

HEAD_DIM = 64
N_Q_HEADS = 8
N_KV_HEADS = 2
Q_PER_KV = N_Q_HEADS // N_KV_HEADS
ATTN_BLOCK = 128
ROPE_BASE = 10000.0

FNET_GROUPS = 4
FNET_GROUP_DIM = 128
FNET_WIDTH = FNET_GROUPS * FNET_GROUP_DIM

HYENA_WIDTH = 512
HYENA_ORDER = 2
FILTER_EMB = 33
FILTER_BANDS = (FILTER_EMB - 1) // 2
FILTER_HIDDEN = 64
DECAY_TARGET = 1e-2
FAST_DECAY_PCT = 0.3
SLOW_DECAY_PCT = 1.5

ATTN_WIDTH = N_Q_HEADS * HEAD_DIM
KV_WIDTH = N_KV_HEADS * HEAD_DIM
Q_OFF = 0
K_OFF = Q_OFF + ATTN_WIDTH
V_OFF = K_OFF + KV_WIDTH
F_OFF = V_OFF + KV_WIDTH
H_OFF = F_OFF + FNET_WIDTH
G_OFF = H_OFF + (HYENA_ORDER + 1) * HYENA_WIDTH
IN_WIDTH = G_OFF + 3 * D_MODEL

N_EXPERTS = 16
N_GROUPS = 4
EXPERTS_PER_GROUP = N_EXPERTS // N_GROUPS
TOP_K = 2
EXPERT_FF = 1024

T_LAT = BATCH * SEQ
T_CTX = BATCH * CTX_LEN
T_ALL = T_LAT + T_CTX

PG_OFF = 0
PH_OFF = 3 * D_MODEL
PF_OFF = PH_OFF + (HYENA_ORDER + 1) * HYENA_WIDTH
PQ_OFF = PF_OFF + FNET_WIDTH
PK_OFF = PQ_OFF + ATTN_WIDTH
PV_OFF = PK_OFF + KV_WIDTH

HY_N = 2 * SEQ
HY_N2 = 64
HY_N1 = HY_N // HY_N2
FN_N = 64

PITCH = HY_N2 + 8
UNROLL = 8
TM = 512
EXPERT_BM = 512
VMEM_LIMIT = 52 * 1024 * 1024


def _cparams(sem, vmem=VMEM_LIMIT):
    return pltpu.CompilerParams(dimension_semantics=sem, vmem_limit_bytes=vmem)


def _dot(a, b):
    return jnp.dot(a, b, preferred_element_type=F32)


def _cis(expo, n):
    ang = (2.0 * math.pi / n) * jnp.mod(expo, n).astype(F32)
    return jnp.cos(ang), jnp.sin(ang)


def _real_form(gr, gi):
    return jnp.concatenate([jnp.concatenate([gr, -gi], axis=-1), jnp.concatenate([gi, gr], axis=-1)], axis=-2)


def _dft_mats():
    ar = lambda n: jnp.arange(n, dtype=jnp.int32)
    m = {}
    c, s = _cis(ar(HY_N1)[:, None] * ar(HY_N1 // 2)[None, :], HY_N1)
    m['hy_lead_f'] = _real_form(c, -s).astype(BF16)
    c, s = _cis(ar(HY_N1 // 2)[:, None] * ar(HY_N1)[None, :], HY_N1)
    m['hy_lead_i'] = _real_form(c, s).astype(BF16)
    c, s = _cis(ar(HY_N1)[:, None] * ar(HY_N1 // 2)[None, :], HY_N1)
    m['hy_lead_kh'] = jnp.concatenate([c, -s], axis=0).astype(BF16)
    a = ar(HY_N1)[:, None, None]
    k2 = ar(HY_N2)[None, :, None]
    n2 = ar(HY_N2)[None, None, :]
    c, s = _cis(n2 * (a + HY_N1 * k2), HY_N)
    m['hy_slab_f2'] = jnp.concatenate([c, -s], axis=-1).astype(BF16)
    ct = jnp.swapaxes(c, 1, 2) * (1.0 / HY_N)
    st = jnp.swapaxes(s, 1, 2) * (1.0 / HY_N)
    m['hy_slab_i2'] = jnp.concatenate([ct, st], axis=-1).astype(BF16)
    nc = 2 * CTX_LEN
    c, s = _cis(ar(nc)[:, None] * ar(CTX_LEN)[None, :], nc)
    m['hc_f'] = _real_form(c, -s).astype(BF16)
    c, s = _cis(ar(CTX_LEN)[:, None] * ar(nc)[None, :], nc)
    m['hc_i'] = _real_form(c * (1.0 / nc), s * (1.0 / nc)).astype(BF16)
    c, s = _cis(ar(nc)[:, None] * ar(CTX_LEN)[None, :], nc)
    m['hc_kh'] = jnp.concatenate([c, -s], axis=0).astype(BF16)
    c, s = _cis(ar(FNET_GROUP_DIM)[:, None] * ar(FNET_GROUP_DIM)[None, :], FNET_GROUP_DIM)
    m['fn_chan'] = jnp.concatenate([c, -s], axis=1).astype(BF16)
    c, s = _cis(ar(FN_N)[:, None] * ar(FN_N)[None, :], FN_N)
    m['fn_lead'] = _real_form(c, -s).astype(BF16)
    a = ar(FN_N)[:, None, None]
    k1 = ar(FN_N)[None, :, None]
    n1 = ar(FN_N)[None, None, :]
    scale = 1.0 / math.sqrt(SEQ * FNET_GROUP_DIM)
    c, s = _cis(n1 * (a + FN_N * k1), SEQ)
    m['fn_slab'] = jnp.concatenate([c * scale, s * scale], axis=-1).astype(BF16)
    scale = 1.0 / math.sqrt(CTX_LEN * FNET_GROUP_DIM)
    c, s = _cis(ar(CTX_LEN)[:, None] * ar(CTX_LEN)[None, :], CTX_LEN)
    m['fc'] = jnp.concatenate([c * scale, s * scale], axis=-1).astype(BF16)
    return m


def _mod_kernel(c_ref, w_ref, b_ref, o_ref):
    c = c_ref[...]
    s = c * jax.nn.sigmoid(c)
    o_ref[0] = _dot(s.astype(BF16), w_ref[0].astype(BF16)) + b_ref[0]


def _modulation(c8, w_mod, b_mod):
    tn = 1536
    n = N_MOD * D_MODEL
    return pl.pallas_call(
        _mod_kernel,
        grid=(DEPTH, n // tn),
        in_specs=[pl.BlockSpec((8, D_MODEL), lambda l, j: (0, 0)),
                  pl.BlockSpec((1, D_MODEL, tn), lambda l, j: (l, 0, j)),
                  pl.BlockSpec((1, 1, tn), lambda l, j: (l, 0, j))],
        out_specs=pl.BlockSpec((1, 8, tn), lambda l, j: (l, 0, j)),
        out_shape=jax.ShapeDtypeStruct((DEPTH, 8, n), F32),
        compiler_params=_cparams(("parallel", "parallel")),
        name="adaln_modulation",
    )(c8, w_mod, b_mod.reshape(DEPTH, 1, n))


def _mod_row(tm):
    tiles_per_batch = SEQ // tm
    return lambda i: jnp.minimum(i // tiles_per_batch, BATCH)


def _rms_mod(x, g, sh, sc):
    y = x * lax.rsqrt(jnp.mean(x * x, axis=-1, keepdims=True) + EPS)
    return (y * g) * (1.0 + sc) + sh


def _norm_proj_kernel(x_ref, g_ref, sh_ref, sc_ref, w_ref, o_ref):
    h = _rms_mod(x_ref[...], g_ref[...], sh_ref[0, 0], sc_ref[0, 0]).astype(BF16)
    o_ref[...] = _dot(h, w_ref[...])


def _norm_proj(x, gain, mod4, w, n_tok):
    tm = TM
    n_out = w.shape[1]
    tn = n_out // 2
    row = _mod_row(tm)
    return pl.pallas_call(
        _norm_proj_kernel,
        grid=(n_out // tn, n_tok // tm),
        in_specs=[pl.BlockSpec((tm, D_MODEL), lambda j, i: (i, 0)),
                  pl.BlockSpec((1, D_MODEL), lambda j, i: (0, 0)),
                  pl.BlockSpec((1, 1, 1, D_MODEL), lambda j, i: (row(i), 0, 0, 0)),
                  pl.BlockSpec((1, 1, 1, D_MODEL), lambda j, i: (row(i), 1, 0, 0)),
                  pl.BlockSpec((D_MODEL, tn), lambda j, i: (0, j))],
        out_specs=pl.BlockSpec((tm, tn), lambda j, i: (i, j)),
        out_shape=jax.ShapeDtypeStruct((T_ALL, n_out), F32),
        compiler_params=_cparams(("parallel", "parallel")),
        name="norm_in_proj",
    )(x, gain, mod4, mod4, w)


def _softmax_pv(qh, k_parts, v_parts, masks, sink):
    nt = (((1,), (1,)), ((), ()))
    scores = []
    for kp, mk in zip(k_parts, masks):
        s = lax.dot_general(qh, kp, nt, preferred_element_type=F32)
        if mk is not None:
            s = jnp.where(mk, s, -1e30)
        scores.append(s)
    m = sink
    for s in scores:
        m = jnp.maximum(m, jnp.max(s, axis=-1, keepdims=True))
    es = [jnp.exp(s - m) for s in scores]
    den = jnp.exp(sink - m)
    for e in es:
        den = den + jnp.sum(e, axis=-1, keepdims=True)
    inv = 1.0 / den
    o = None
    for e, vp in zip(es, v_parts):
        t = _dot((e * inv).astype(BF16), vp)
        o = t if o is None else o + t
    return o


def _attn_kernel(sink_ref, q_ref, km_ref, k0_ref, kp_ref, vm_ref, v0_ref, vp_ref, kc_ref, vc_ref,
                 cos_ref, sin_ref, o_ref, *, nb):
    n = pl.program_id(1)
    blk = ATTN_BLOCK
    lane = lax.broadcasted_iota(jnp.int32, (blk, 128), 1)
    first = (lane % 32) < 16

    def rope(x, blk_idx):
        r0 = pl.multiple_of(blk_idx * blk, blk)
        c = cos_ref[pl.ds(r0, blk), :]
        s = sin_ref[pl.ds(r0, blk), :]
        sw = jnp.where(first, pltpu.roll(x, 112, 1), pltpu.roll(x, 16, 1))
        return x * c + sw * s

    nm = jnp.maximum(n - 1, 0)
    npl = jnp.minimum(n + 1, nb - 1)
    kall = jnp.concatenate([rope(km_ref[...], nm), rope(k0_ref[...], n), rope(kp_ref[...], npl), kc_ref[...]],
                           axis=0)
    vall = jnp.concatenate([vm_ref[...], v0_ref[...], vp_ref[...], vc_ref[...]], axis=0)
    nk = kall.shape[0]
    kswap = pltpu.roll(kall, HEAD_DIM, 1)
    vswap = pltpu.roll(vall, HEAD_DIM, 1)
    lo = lax.broadcasted_iota(jnp.int32, (nk, 128), 1) < HEAD_DIM

    r = lax.broadcasted_iota(jnp.int32, (2 * blk, blk), 0) % blk
    cidx = lax.broadcasted_iota(jnp.int32, (2 * blk, blk), 1)
    ok_prev = jnp.where(cidx >= r, (n > 0).astype(jnp.int32), 0) > 0
    ok_next = jnp.where(cidx <= r, (n < nb - 1).astype(jnp.int32), 0) > 0
    top_rows = lax.broadcasted_iota(jnp.int32, (2 * blk, 1), 0) < blk
    neg = jnp.float32(-1e30)

    scale = HEAD_DIM ** -0.5
    q2 = [(rope(q_ref[:, p * 128:(p + 1) * 128], n) * scale).astype(BF16) for p in range(N_Q_HEADS // 2)]
    nt = (((1,), (1,)), ((), ()))
    for h in range(N_KV_HEADS):
        ka, kb = (kall, kswap) if h == 0 else (kswap, kall)
        va, vb = (vall, vswap) if h == 0 else (vswap, vall)
        kbd = jnp.concatenate([jnp.where(lo, ka, 0.0), jnp.where(lo, 0.0, kb)], axis=0).astype(BF16)
        vbd = jnp.concatenate([jnp.where(lo, va, 0.0), jnp.where(lo, 0.0, vb)], axis=0).astype(BF16)
        q4 = jnp.concatenate([q2[2 * h], q2[2 * h + 1]], axis=0)
        s = lax.dot_general(q4, kbd, nt, preferred_element_type=F32)
        probs = []
        for c in range(2):
            base = c * nk
            sink = jnp.where(top_rows, sink_ref[Q_PER_KV * h + c], sink_ref[Q_PER_KV * h + 2 + c])
            parts = [jnp.where(ok_prev, s[:, base:base + blk], neg),
                     s[:, base + blk:base + 2 * blk],
                     jnp.where(ok_next, s[:, base + 2 * blk:base + 3 * blk], neg),
                     s[:, base + 3 * blk:base + nk]]
            m = sink
            for part in parts:
                m = jnp.maximum(m, jnp.max(part, axis=-1, keepdims=True))
            es = [jnp.exp(part - m) for part in parts]
            den = jnp.exp(sink - m)
            for e in es:
                den = den + jnp.sum(e, axis=-1, keepdims=True)
            inv = 1.0 / den
            probs += [(e * inv).astype(BF16) for e in es]
        o = _dot(jnp.concatenate(probs, axis=1), vbd)
        w0 = h * Q_PER_KV * HEAD_DIM
        o_ref[:, w0:w0 + 128] = o[:blk].astype(o_ref.dtype)
        o_ref[:, w0 + 128:w0 + 256] = o[blk:].astype(o_ref.dtype)


def _ctx_attn_kernel(sink_ref, q_ref, kc_ref, vc_ref, o_ref):
    kc = kc_ref[...].astype(BF16)
    vc = vc_ref[...].astype(BF16)
    scale = HEAD_DIM ** -0.5
    outs = []
    for pair in range(N_Q_HEADS // 2):
        q2 = (q_ref[:, pair * 128:(pair + 1) * 128] * scale).astype(BF16)
        for sub in range(2):
            head = 2 * pair + sub
            kvh = head // Q_PER_KV
            sl = slice(kvh * HEAD_DIM, (kvh + 1) * HEAD_DIM)
            qh = q2[:, sub * HEAD_DIM:(sub + 1) * HEAD_DIM]
            outs.append(_softmax_pv(qh, [kc[:, sl]], [vc[:, sl]], [None], sink_ref[head]))
    o_ref[...] = jnp.concatenate(outs, axis=-1).astype(o_ref.dtype)


def _attention(p, sink, cos_t, sin_t, with_ctx):
    blk = ATTN_BLOCK
    nb = SEQ // blk
    qc, kcol, vcol = PQ_OFF // ATTN_WIDTH, PK_OFF // KV_WIDTH, PV_OFF // KV_WIDTH
    ctx_blk = T_LAT // CTX_LEN
    smem = pl.BlockSpec(memory_space=pltpu.SMEM)

    def kv_spec(col, d):
        return pl.BlockSpec((blk, KV_WIDTH),
                            lambda b, n: (b * nb + jnp.clip(n + d, 0, nb - 1), col))

    ya = pl.pallas_call(
        functools.partial(_attn_kernel, nb=nb),
        grid=(BATCH, nb),
        in_specs=[smem,
                  pl.BlockSpec((blk, ATTN_WIDTH), lambda b, n: (b * nb + n, qc)),
                  kv_spec(kcol, -1), kv_spec(kcol, 0), kv_spec(kcol, 1),
                  kv_spec(vcol, -1), kv_spec(vcol, 0), kv_spec(vcol, 1),
                  pl.BlockSpec((CTX_LEN, KV_WIDTH), lambda b, n: (ctx_blk + b, kcol)),
                  pl.BlockSpec((CTX_LEN, KV_WIDTH), lambda b, n: (ctx_blk + b, vcol)),
                  pl.BlockSpec((SEQ, KV_WIDTH), lambda b, n: (0, 0)),
                  pl.BlockSpec((SEQ, KV_WIDTH), lambda b, n: (0, 0))],
        out_specs=pl.BlockSpec((blk, ATTN_WIDTH), lambda b, n: (b * nb + n, 0)),
        out_shape=jax.ShapeDtypeStruct((T_ALL, ATTN_WIDTH), BF16),
        compiler_params=_cparams(("parallel", "parallel")),
        name="banded_attention",
    )(sink, p, p, p, p, p, p, p, p, p, cos_t, sin_t)
    if not with_ctx:
        return ya
    cb = CTX_LEN // blk
    lat_blk = T_LAT // blk

    def alias_kernel(sink_ref, q_ref, kc_ref, vc_ref, ya_in_ref, o_ref):
        del ya_in_ref
        _ctx_attn_kernel(sink_ref, q_ref, kc_ref, vc_ref, o_ref)

    return pl.pallas_call(
        alias_kernel,
        grid=(BATCH, cb),
        in_specs=[smem,
                  pl.BlockSpec((blk, ATTN_WIDTH), lambda b, n: (lat_blk + b * cb + n, qc)),
                  pl.BlockSpec((CTX_LEN, KV_WIDTH), lambda b, n: (ctx_blk + b, kcol)),
                  pl.BlockSpec((CTX_LEN, KV_WIDTH), lambda b, n: (ctx_blk + b, vcol)),
                  pl.BlockSpec(memory_space=pl.ANY)],
        out_specs=pl.BlockSpec((blk, ATTN_WIDTH), lambda b, n: (lat_blk + b * cb + n, 0)),
        out_shape=jax.ShapeDtypeStruct((T_ALL, ATTN_WIDTH), BF16),
        input_output_aliases={4: 0},
        compiler_params=_cparams(("parallel", "parallel")),
        name="context_attention",
    )(sink, p, p, p, ya)


def _rope_tables():
    n_freq = HEAD_DIM // 4
    freqs = ROPE_BASE ** (-jnp.arange(n_freq, dtype=F32) / n_freq)
    t = jnp.arange(SEQ, dtype=jnp.int32)
    rows = (t // GRID_W).astype(F32)[:, None] * freqs
    cols = (t % GRID_W).astype(F32)[:, None] * freqs
    cos_h = jnp.concatenate([jnp.cos(rows), jnp.cos(rows), jnp.cos(cols), jnp.cos(cols)], axis=-1)
    sin_h = jnp.concatenate([-jnp.sin(rows), jnp.sin(rows), -jnp.sin(cols), jnp.sin(cols)], axis=-1)
    return jnp.tile(cos_h, (1, 2)), jnp.tile(sin_h, (1, 2))


def _lead_kernel(*refs, n_in, cmul, epi):
    m_ref = refs[0]
    x_refs = refs[1:1 + n_in]
    pos = 1 + n_in
    xs = []
    for r in x_refs:
        v = r[...]
        xs.append(v.reshape(-1, v.shape[-1]))
    x = xs[0] if n_in == 1 else jnp.concatenate(xs, axis=0)
    if cmul:
        k = refs[pos][...]
        pos += 1
        k = k.reshape(-1, k.shape[-1])
        half = x.shape[0] // 2
        xr, xi, kr, ki = x[:half], x[half:], k[:half], k[half:]
        x = jnp.concatenate([xr * kr - xi * ki, xr * ki + xi * kr], axis=0)
    res = _dot(m_ref[...], x.astype(BF16))
    if epi:
        g_ref, y_ref, b_ref = refs[pos:pos + 3]
        pos += 3
        g = g_ref[...]
        y = y_ref[...]
        res = g.reshape(-1, g.shape[-1]) * (res + y.reshape(-1, y.shape[-1]) * b_ref[...])
    o_ref = refs[pos]
    o_ref[...] = res.reshape(o_ref.shape).astype(o_ref.dtype)


def _lead(mat, xs, x_specs, grid, out_shape, out_spec, *, kspec=None, epi=None, alias_to=None, name):
    ins = [mat] + list(xs)
    specs = [pl.BlockSpec(mat.shape, lambda *a: (0, 0))] + list(x_specs)
    if kspec is not None:
        ins.append(kspec[0])
        specs.append(kspec[1])
    if epi is not None:
        for arr, sp in epi:
            ins.append(arr)
            specs.append(sp)
    kern = functools.partial(_lead_kernel, n_in=len(xs), cmul=kspec is not None, epi=epi is not None)
    aliases = {}
    if alias_to is not None:
        aliases = {len(ins): 0}
        ins.append(alias_to)
        specs.append(pl.BlockSpec(memory_space=pl.ANY))
        inner = kern

        def kern(*refs):
            inner(*refs[:-2], refs[-1])

    return pl.pallas_call(
        kern, grid=grid, in_specs=specs, out_specs=out_spec, out_shape=out_shape,
        input_output_aliases=aliases,
        compiler_params=_cparams(("parallel",) * len(grid)), name=name,
    )(*ins)


def _cstack(xr, xi):
    return jnp.concatenate([jnp.concatenate([xr, xi], axis=1), jnp.concatenate([-xi, xr], axis=1)], axis=0)


def _fnet_kernel(u_ref, mc_ref, ml_ref, ms_ref, o_ref, zr, zi, are, aim):
    n = FN_N
    pitch = PITCH
    gd = FNET_GROUP_DIM
    rows = 4 * n
    mc = mc_ref[...]

    def chan(i, carry):
        r_in = pl.multiple_of(i * rows, rows)
        z = _dot(u_ref[pl.ds(r_in, rows), :].astype(BF16), mc)
        for q in range(rows // n):
            r_out = pl.multiple_of((i * (rows // n) + q) * pitch, 8)
            zr[pl.ds(r_out, n), :] = z[q * n:(q + 1) * n, :gd]
            zi[pl.ds(r_out, n), :] = z[q * n:(q + 1) * n, gd:]
        return carry

    lax.fori_loop(0, SEQ // rows, chan, 0)
    ml = ml_ref[...]

    def lead(i, carry):
        for u in range(UNROLL):
            n1 = i * UNROLL + u
            x = jnp.concatenate([zr[pl.ds(n1, n, stride=pitch), :], zi[pl.ds(n1, n, stride=pitch), :]],
                                axis=0).astype(BF16)
            r = _dot(ml, x)
            are[pl.ds(n1, n, stride=pitch), :] = r[:n]
            aim[pl.ds(n1, n, stride=pitch), :] = r[n:]
        return carry

    lax.fori_loop(0, n // UNROLL, lead, 0)

    def slab(i, carry):
        for u in range(UNROLL):
            k2 = i * UNROLL + u
            r0 = pl.multiple_of(k2 * pitch, 8)
            x = jnp.concatenate([are[pl.ds(r0, n), :], aim[pl.ds(r0, n), :]], axis=0).astype(BF16)
            o_ref[pl.ds(k2, n, stride=n), :] = _dot(ms_ref[k2], x)
        return carry

    lax.fori_loop(0, n // UNROLL, slab, 0)


def _fnet_ctx_kernel(u_ref, mc_ref, mf_ref, yf_in_ref, o_ref):
    del yf_in_ref
    gd = FNET_GROUP_DIM
    mc = mc_ref[...]
    mf = mf_ref[...]
    for g in range(FNET_GROUPS):
        z = _dot(u_ref[:, g * gd:(g + 1) * gd].astype(BF16), mc)
        x = jnp.concatenate([z[:, :gd], z[:, gd:]], axis=0).astype(BF16)
        o_ref[:, g * gd:(g + 1) * gd] = _dot(mf, x)


def _fourier_mix(p, mats, with_ctx):
    gd = FNET_GROUP_DIM
    col0 = PF_OFF // gd
    full = lambda a: pl.BlockSpec(a.shape, lambda *i: (0,) * a.ndim)
    scr = pltpu.VMEM((FN_N * PITCH, gd), F32)
    yf = pl.pallas_call(
        _fnet_kernel,
        grid=(BATCH, FNET_GROUPS),
        in_specs=[pl.BlockSpec((SEQ, gd), lambda b, g: (b, col0 + g)),
                  full(mats['fn_chan']), full(mats['fn_lead']), full(mats['fn_slab'])],
        out_specs=pl.BlockSpec((SEQ, gd), lambda b, g: (b, g)),
        out_shape=jax.ShapeDtypeStruct((T_ALL if with_ctx else T_LAT, FNET_WIDTH), F32),
        scratch_shapes=[scr, scr, scr, scr],
        compiler_params=_cparams(("parallel", "parallel")),
        name="fnet_latent",
    )(p, mats['fn_chan'], mats['fn_lead'], mats['fn_slab'])
    if not with_ctx:
        return yf
    blk0 = T_LAT // CTX_LEN
    return pl.pallas_call(
        _fnet_ctx_kernel,
        grid=(BATCH,),
        in_specs=[pl.BlockSpec((CTX_LEN, FNET_WIDTH), lambda b: (blk0 + b, PF_OFF // FNET_WIDTH)),
                  full(mats['fn_chan']), full(mats['fc']), pl.BlockSpec(memory_space=pl.ANY)],
        out_specs=pl.BlockSpec((CTX_LEN, FNET_WIDTH), lambda b: (blk0 + b, 0)),
        out_shape=jax.ShapeDtypeStruct((T_ALL, FNET_WIDTH), F32),
        input_output_aliases={3: 0},
        compiler_params=_cparams(("parallel",)),
        name="fnet_ctx",
    )(p, mats['fn_chan'], mats['fc'], yf)


def _short_conv_kernel(u_ref, w_ref, b_ref, o_ref, *, rows, chunk):
    w0 = w_ref[0:1, :]
    w1 = w_ref[1:2, :]
    w2 = w_ref[2:3, :]
    bias = b_ref[...]
    width = u_ref.shape[-1]
    ridx = lax.broadcasted_iota(jnp.int32, (chunk, width), 0)
    n_chunks = rows // chunk
    for ci in range(n_chunks):
        r0 = ci * chunk
        cur = u_ref[r0:r0 + chunk, :]
        if ci > 0:
            prev_row = u_ref[r0 - 8:r0, :][7:8, :]
        else:
            prev_row = jnp.zeros((1, width), F32)
        if ci < n_chunks - 1:
            next_row = u_ref[r0 + chunk:r0 + chunk + 8, :][0:1, :]
        else:
            next_row = jnp.zeros((1, width), F32)
        up = jnp.where(ridx == 0, prev_row, pltpu.roll(cur, 1, 0))
        dn = jnp.where(ridx == chunk - 1, next_row, pltpu.roll(cur, chunk - 1, 0))
        o_ref[0, r0:r0 + chunk, :] = up * w0 + cur * w1 + dn * w2 + bias


def _short_conv(p, conv_w, conv_b, with_ctx):
    cw = 256
    hw = (HYENA_ORDER + 1) * HYENA_WIDTH
    ncol = hw // cw
    per = HYENA_WIDTH // cw
    col0 = PH_OFF // cw
    out_shape = jax.ShapeDtypeStruct((HYENA_ORDER + 1, T_ALL if with_ctx else T_LAT, HYENA_WIDTH), F32)
    b2 = conv_b.reshape(1, hw)

    def call(rows, blk0, alias):
        kern = functools.partial(_short_conv_kernel, rows=rows, chunk=min(rows, 256))
        ins = [p, conv_w, b2]
        specs = [pl.BlockSpec((rows, cw), lambda b, j: (blk0 + b, col0 + j)),
                 pl.BlockSpec((3, cw), lambda b, j: (0, j)),
                 pl.BlockSpec((1, cw), lambda b, j: (0, j))]
        aliases = {}
        if alias is not None:
            ins.append(alias)
            specs.append(pl.BlockSpec(memory_space=pl.ANY))
            aliases = {3: 0}
            inner = kern

            def kern(u_ref, w_ref, b_ref, a_ref, o_ref):
                del a_ref
                inner(u_ref, w_ref, b_ref, o_ref)

        return pl.pallas_call(
            kern, grid=(BATCH, ncol), in_specs=specs,
            out_specs=pl.BlockSpec((1, rows, cw), lambda b, j: (j // per, blk0 + b, j % per)),
            out_shape=out_shape, input_output_aliases=aliases,
            compiler_params=_cparams(("parallel", "parallel")), name="hyena_short_conv",
        )(*ins)

    z3 = call(SEQ, 0, None)
    if with_ctx:
        z3 = call(CTX_LEN, T_LAT // CTX_LEN, z3)
    return z3


def _filter_mlp_kernel(ft_ref, w1_ref, b1_ref, fq_ref, w2_ref, b2_ref, o_ref):
    fq = fq_ref[...]
    h = jnp.sin(fq * (_dot(ft_ref[...].astype(BF16), w1_ref[...]) + b1_ref[...]))
    h = jnp.sin(fq * (_dot(h.astype(BF16), w2_ref[...]) + b2_ref[...]))
    o_ref[...] = h.astype(o_ref.dtype)


def _filter_kernel(h_ref, t_ref, w3f_ref, w3b_ref, dl_ref, m1_ref, *rest, n, dense):
    hb = h_ref[...]
    decay = jnp.exp(-t_ref[...] * dl_ref[...])
    tf = _dot(hb, w3f_ref[...]) * decay
    tb = _dot(hb, w3b_ref[...]) * decay
    tb = jnp.where(lax.broadcasted_iota(jnp.int32, tb.shape, 0) == 0, 0.0, tb)
    scale = 1.0 / (jnp.sum(jnp.abs(tf), axis=0, keepdims=True) + jnp.sum(jnp.abs(tb), axis=0, keepdims=True))
    cw = tf.shape[1]
    if dense:
        o_ref = rest[0]
        r = _dot(m1_ref[...], jnp.concatenate([tf, tb], axis=1).astype(BF16))
        nc = r.shape[0] // 2
        o_ref[0:nc, :] = (r[:nc, :cw] + r[:nc, cw:]) * scale
        o_ref[nc:, :] = (r[nc:, :cw] - r[nc:, cw:]) * scale
        return
    f2_ref, o_ref, tf_s, tb_s, are_f, aim_f, are_b, aim_b = rest
    tf_s[...] = tf
    tb_s[...] = tb
    half = HY_N1 // 2
    m1 = m1_ref[...]

    def stage1(i, carry):
        for u in range(UNROLL):
            n2 = i * UNROLL + u
            x = jnp.concatenate([tf_s[pl.ds(n2, half, stride=HY_N2), :], tb_s[pl.ds(n2, half, stride=HY_N2), :]],
                                axis=1).astype(BF16)
            r = _dot(m1, x)
            rows = pl.ds(n2, HY_N1, stride=PITCH)
            are_f[rows, :] = r[:HY_N1, :cw]
            are_b[rows, :] = r[:HY_N1, cw:]
            aim_f[rows, :] = r[HY_N1:, :cw]
            aim_b[rows, :] = r[HY_N1:, cw:]
        return carry

    lax.fori_loop(0, HY_N2 // UNROLL, stage1, 0)

    def stage2(i, carry):
        for u in range(UNROLL):
            k1 = i * UNROLL + u
            rows = pl.ds(pl.multiple_of(k1 * PITCH, 8), HY_N2)
            ar = jnp.concatenate([are_f[rows, :], are_b[rows, :]], axis=1)
            ai = jnp.concatenate([aim_f[rows, :], aim_b[rows, :]], axis=1)
            t = _dot(f2_ref[k1], _cstack(ar, ai).astype(BF16))
            o_ref[0, k1] = (t[:, 0:cw] + t[:, cw:2 * cw]) * scale
            o_ref[1, k1] = (t[:, 2 * cw:3 * cw] - t[:, 3 * cw:4 * cw]) * scale
        return carry

    lax.fori_loop(0, HY_N1 // UNROLL, stage2, 0)


def _filter_feats(n):
    pos = jnp.arange(n, dtype=F32)
    t = pos / max(n - 1, 1)
    omega = 2.0 * math.pi * pos / n
    bands = jnp.linspace(1e-4, FILTER_BANDS - 1, FILTER_BANDS, dtype=F32)
    feats = jnp.concatenate([t[:, None], jnp.cos(omega[:, None] * bands), -jnp.sin(omega[:, None] * bands)], axis=-1)
    return jnp.pad(feats, ((0, 0), (0, 128 - FILTER_EMB))), t[:, None]


def _filter_spectrum(n, filt, mats):
    w1, b1, freq, w2, b2, w3 = filt
    dense = n == CTX_LEN
    cw = 128
    nch = HYENA_WIDTH // cw
    feats, t = _filter_feats(n)
    w1p = jnp.pad(w1, ((0, 128 - FILTER_EMB), (0, 0))).astype(BF16)
    deltas = jnp.abs(jnp.linspace(math.log(DECAY_TARGET) / SLOW_DECAY_PCT, math.log(DECAY_TARGET) / FAST_DECAY_PCT,
                                  HYENA_WIDTH, dtype=F32)).reshape(1, HYENA_WIDTH)
    full = lambda a: pl.BlockSpec(a.shape, lambda *i: (0,) * a.ndim)
    row = lambda a: a.reshape(1, -1)
    w3b16 = w3.astype(BF16)
    tap_spec = lambda d: pl.BlockSpec((FILTER_HIDDEN, cw), lambda o, ch: (0, (o * 2 + d) * nch + ch))
    m1 = mats['hc_kh'] if dense else mats['hy_lead_kh']
    mlp_ins = [feats, w1p, row(b1), row(freq), w2.astype(BF16), row(b2)]
    hb = pl.pallas_call(
        _filter_mlp_kernel, grid=(1,), in_specs=[full(a) for a in mlp_ins],
        out_specs=pl.BlockSpec((n, FILTER_HIDDEN), lambda i: (0, 0)),
        out_shape=jax.ShapeDtypeStruct((n, FILTER_HIDDEN), BF16),
        compiler_params=_cparams(("arbitrary",)), name="hyena_filter_mlp",
    )(*mlp_ins)
    ins = [hb, t, w3b16, w3b16, deltas, m1]
    specs = [full(hb), full(t), tap_spec(0), tap_spec(1), pl.BlockSpec((1, cw), lambda o, ch: (0, ch)), full(m1)]
    if dense:
        nc = 2 * n
        out_shape = jax.ShapeDtypeStruct((HYENA_ORDER, 2 * nc, HYENA_WIDTH), F32)
        out_spec = pl.BlockSpec((None, 2 * nc, cw), lambda o, ch: (o, 0, ch))
        scratch = []
    else:
        ins.append(mats['hy_slab_f2'])
        specs.append(full(mats['hy_slab_f2']))
        out_shape = jax.ShapeDtypeStruct((HYENA_ORDER, 2, HY_N1, HY_N2, HYENA_WIDTH), F32)
        out_spec = pl.BlockSpec((None, 2, HY_N1, HY_N2, cw), lambda o, ch: (o, 0, 0, 0, ch))
        scratch = [pltpu.VMEM((n, cw), F32)] * 2 + [pltpu.VMEM((HY_N1 * PITCH, cw), F32)] * 4
    return pl.pallas_call(
        functools.partial(_filter_kernel, n=n, dense=dense),
        grid=(HYENA_ORDER, nch), in_specs=specs, out_specs=out_spec, out_shape=out_shape,
        scratch_shapes=scratch,
        compiler_params=_cparams(("parallel", "parallel")),
        name="hyena_filter_ctx" if dense else "hyena_filter",
    )(*ins)


def _hyena_conv_kernel(y_ref, g_ref, k_ref, m1_ref, f2f_ref, f2i_ref, m3_ref, b_ref, o_ref, are, aim):
    half = HY_N1 // 2
    m1 = m1_ref[...]

    def stage1(i, carry):
        for u in range(UNROLL):
            n2 = i * UNROLL + u
            x = jnp.concatenate([y_ref[pl.ds(n2, half, stride=HY_N2), :],
                                 y_ref[pl.ds(SEQ + n2, half, stride=HY_N2), :]], axis=0).astype(BF16)
            r = _dot(m1, x)
            are[pl.ds(n2, HY_N1, stride=PITCH), :] = r[:HY_N1]
            aim[pl.ds(n2, HY_N1, stride=PITCH), :] = r[HY_N1:]
        return carry

    lax.fori_loop(0, HY_N2 // UNROLL, stage1, 0)
    cw = o_ref.shape[-1]

    def stage2(i, carry):
        for u in range(UNROLL):
            k1 = i * UNROLL + u
            r0 = pl.multiple_of(k1 * PITCH, 8)
            y = _dot(f2f_ref[k1], _cstack(are[pl.ds(r0, HY_N2), :], aim[pl.ds(r0, HY_N2), :]).astype(BF16))
            yr, yi = y[:, :cw], y[:, cw:]
            kr, ki = k_ref[0, k1], k_ref[1, k1]
            w = _dot(f2i_ref[k1], _cstack(yr * kr - yi * ki, yr * ki + yi * kr).astype(BF16))
            are[pl.ds(r0, HY_N2), :] = w[:, :cw]
            aim[pl.ds(r0, HY_N2), :] = w[:, cw:]
        return carry

    lax.fori_loop(0, HY_N1 // UNROLL, stage2, 0)
    m3 = m3_ref[...]
    bias = b_ref[...]

    def stage3(i, carry):
        for u in range(UNROLL):
            n2 = i * UNROLL + u
            bn = jnp.concatenate([are[pl.ds(n2, HY_N1, stride=PITCH), :], aim[pl.ds(n2, HY_N1, stride=PITCH), :]],
                                 axis=0).astype(BF16)
            y = _dot(m3, bn)
            for b in range(2):
                rows = pl.ds(b * SEQ + n2, half, stride=HY_N2)
                o_ref[rows, :] = g_ref[rows, :] * (y[b * half:(b + 1) * half] + y_ref[rows, :] * bias)
        return carry

    lax.fori_loop(0, HY_N2 // UNROLL, stage3, 0)


def _hyena_mix(z3, filt, hyena_bias, mats, with_ctx):
    c = HYENA_WIDTH
    cw = 128
    nch = c // cw
    pairs = BATCH // 2
    full = lambda a: pl.BlockSpec(a.shape, lambda *i: (0,) * a.ndim)
    kspec = _filter_spectrum(SEQ, filt, mats)
    if with_ctx:
        kspec_c = _filter_spectrum(CTX_LEN, filt, mats)
    bias3 = hyena_bias.reshape(HYENA_ORDER, 1, c)
    scr = pltpu.VMEM((HY_N1 * PITCH, cw), F32)
    y = None
    for o in range(HYENA_ORDER):
        if y is None:
            xin, xspec = z3, pl.BlockSpec((None, 2 * SEQ, cw), lambda b, ch: (0, b, ch))
        else:
            xin, xspec = y, pl.BlockSpec((2 * SEQ, cw), lambda b, ch: (b, ch))
        ynew = pl.pallas_call(
            _hyena_conv_kernel,
            grid=(pairs, nch),
            in_specs=[xspec,
                      pl.BlockSpec((None, 2 * SEQ, cw), lambda b, ch, o=o: (o + 1, b, ch)),
                      pl.BlockSpec((None, 2, HY_N1, HY_N2, cw), lambda b, ch, o=o: (o, 0, 0, 0, ch)),
                      full(mats['hy_lead_f']), full(mats['hy_slab_f2']), full(mats['hy_slab_i2']),
                      full(mats['hy_lead_i']),
                      pl.BlockSpec((None, 1, cw), lambda b, ch, o=o: (o, 0, ch))],
            out_specs=pl.BlockSpec((2 * SEQ, cw), lambda b, ch: (b, ch)),
            out_shape=jax.ShapeDtypeStruct((T_ALL if with_ctx else T_LAT, c), F32),
            scratch_shapes=[scr, scr],
            compiler_params=_cparams(("parallel", "parallel"), 58 * 1024 * 1024),
            name="hyena_conv",
        )(xin, z3, kspec, mats['hy_lead_f'], mats['hy_slab_f2'], mats['hy_slab_i2'], mats['hy_lead_i'], bias3)
        if with_ctx:
            nc = 2 * CTX_LEN
            blk0 = T_LAT // nc
            if y is None:
                cin, cspec_in = z3, pl.BlockSpec((1, nc, c), lambda b: (0, blk0 + b, 0))
            else:
                cin, cspec_in = y, pl.BlockSpec((nc, c), lambda b: (blk0 + b, 0))
            xc = _lead(mats['hc_f'], [cin], [cspec_in], (pairs,),
                       jax.ShapeDtypeStruct((pairs, 2 * nc, c), F32),
                       pl.BlockSpec((1, 2 * nc, c), lambda b: (b, 0, 0)), name="hyena_ctx_fwd")
            ynew = _lead(mats['hc_i'], [xc], [pl.BlockSpec((1, 2 * nc, c), lambda b: (b, 0, 0))],
                         (pairs,), jax.ShapeDtypeStruct((T_ALL, c), F32),
                         pl.BlockSpec((nc, c), lambda b: (blk0 + b, 0)),
                         kspec=(kspec_c, pl.BlockSpec((1, 2 * nc, c), lambda b, o=o: (o, 0, 0))),
                         epi=[(z3, pl.BlockSpec((1, nc, c), lambda b, o=o: (o + 1, blk0 + b, 0))),
                              (cin, cspec_in),
                              (bias3, pl.BlockSpec((1, 1, c), lambda b, o=o: (o, 0, 0)))],
                         alias_to=ynew, name="hyena_ctx_inv")
        y = ynew
    return y


def _route_tile(lt, br, base, tri):
    tm = lt.shape[1]
    aff = jax.nn.sigmoid(lt)
    biased = aff + br
    b = [biased[e:e + 1, :] for e in range(N_EXPERTS)]
    a = [aff[e:e + 1, :] for e in range(N_EXPERTS)]
    epg = EXPERTS_PER_GROUP
    scores = []
    for g in range(N_GROUPS):
        x0, x1, x2, x3 = b[epg * g:epg * g + epg]
        s1, t1 = jnp.maximum(x0, x1), jnp.minimum(x0, x1)
        s2, t2 = jnp.maximum(x2, x3), jnp.minimum(x2, x3)
        scores.append(jnp.maximum(s1, s2) + jnp.maximum(jnp.minimum(s1, s2), jnp.maximum(t1, t2)))
    best = scores[0]
    gsel = jnp.zeros((1, tm), jnp.int32)
    for g in range(1, N_GROUPS):
        gsel = jnp.where(scores[g] > best, g, gsel)
        best = jnp.maximum(best, scores[g])

    def pick(rows, j):
        out = rows[j]
        for g in range(1, N_GROUPS):
            out = jnp.where(gsel == g, rows[epg * g + j], out)
        return out

    v = [pick(b, j) for j in range(epg)]
    av = [pick(a, j) for j in range(epg)]
    i1 = jnp.zeros((1, tm), jnp.int32)
    m1 = v[0]
    for j in range(1, epg):
        i1 = jnp.where(v[j] > m1, j, i1)
        m1 = jnp.maximum(m1, v[j])
    neg = jnp.float32(-3.0e38)
    i2 = jnp.zeros((1, tm), jnp.int32)
    m2 = jnp.full((1, tm), neg, F32)
    for j in range(epg):
        cand = jnp.where(i1 == j, neg, v[j])
        take = cand > m2
        i2 = jnp.where(take, j, i2)
        m2 = jnp.where(take, cand, m2)

    def sel(rows, idx):
        out = rows[0]
        for j in range(1, epg):
            out = jnp.where(idx == j, rows[j], out)
        return out

    a1, a2 = sel(av, i1), sel(av, i2)
    den = a1 + a2
    e1 = gsel * epg + i1
    e2 = gsel * epg + i2
    eio = lax.broadcasted_iota(jnp.int32, (N_EXPERTS, tm), 0)
    oh1 = jnp.where(eio == e1, 1.0, 0.0)
    oh2 = jnp.where(eio == e2, 1.0, 0.0)
    oh = oh1 + oh2
    tot = base + _dot(oh.astype(BF16), tri)
    r1 = jnp.sum(oh1 * tot, axis=0, keepdims=True)
    r2 = jnp.sum(oh2 * tot, axis=0, keepdims=True)
    new_base = base + jnp.sum(oh, axis=1, keepdims=True)
    return (e1, e2), (a1 / den, a2 / den), (r1.astype(jnp.int32), r2.astype(jnp.int32)), new_base


def _merge_kernel(x_ref, ya_ref, yf_ref, yh_ref, gt_ref, wa_ref, wf_ref, wh_ref, wo_ref, g1_ref,
                  gn_ref, sh_ref, sc_ref, wrt_ref, br_ref, xo_ref, h2_ref, e_ref, w_ref, r_ref, cnt_ref):
    d = D_MODEL
    merged = jax.nn.sigmoid(gt_ref[:, 0:d]) * _dot(ya_ref[...], wa_ref[...])
    merged += jax.nn.sigmoid(gt_ref[:, d:2 * d]) * _dot(yf_ref[...].astype(BF16), wf_ref[...])
    merged += jax.nn.sigmoid(gt_ref[:, 2 * d:3 * d]) * _dot(yh_ref[...].astype(BF16), wh_ref[...])
    xn = x_ref[...] + g1_ref[0, 0] * _dot(merged.astype(BF16), wo_ref[...])
    xo_ref[...] = xn
    h2f = _rms_mod(xn, gn_ref[...], sh_ref[0, 0], sc_ref[0, 0])
    h2_ref[...] = h2f
    h2 = h2f.astype(BF16)

    @pl.when(pl.program_id(0) == 0)
    def _():
        cnt_ref[...] = jnp.zeros_like(cnt_ref)

    tm = h2.shape[0]
    lt = lax.dot_general(wrt_ref[...], h2, (((1,), (1,)), ((), ())), preferred_element_type=F32)
    tri = jnp.where(lax.broadcasted_iota(jnp.int32, (tm, tm), 0) < lax.broadcasted_iota(jnp.int32, (tm, tm), 1),
                    1.0, 0.0).astype(BF16)
    es, ws, rs, new_base = _route_tile(lt, br_ref[...], cnt_ref[:, 0:1], tri)
    e_ref[0:1, :], e_ref[1:2, :] = es
    w_ref[0:1, :], w_ref[1:2, :] = ws
    r_ref[0:1, :], r_ref[1:2, :] = rs
    cnt_ref[...] = jnp.broadcast_to(new_base, cnt_ref.shape)


def _merge(x, ya, yf, yh, p, wa, wf, wh, wo, mod4, gain2, wrt, br, n_tok):
    tm = TM
    row = _mod_row(tm)
    full = lambda a: pl.BlockSpec(a.shape, lambda i: (0,) * a.ndim)
    modspec = lambda k: pl.BlockSpec((1, 1, 1, D_MODEL), lambda i: (row(i), k, 0, 0))
    tok = lambda w: pl.BlockSpec((tm, w), lambda i: (i, 0))
    lane = pl.BlockSpec((TOP_K, tm), lambda i: (0, i))
    return pl.pallas_call(
        _merge_kernel,
        grid=(n_tok // tm,),
        in_specs=[tok(D_MODEL), tok(ATTN_WIDTH), tok(FNET_WIDTH), tok(HYENA_WIDTH),
                  pl.BlockSpec((tm, 3 * D_MODEL), lambda i: (i, PG_OFF // (3 * D_MODEL))),
                  full(wa), full(wf), full(wh), full(wo), modspec(2), full(gain2), modspec(3), modspec(4),
                  full(wrt), full(br)],
        out_specs=[tok(D_MODEL), tok(D_MODEL), lane, lane, lane,
                   pl.BlockSpec((N_EXPERTS, 128), lambda i: (0, 0))],
        out_shape=[jax.ShapeDtypeStruct((n_tok, D_MODEL), F32),
                   jax.ShapeDtypeStruct((n_tok, D_MODEL), F32),
                   jax.ShapeDtypeStruct((TOP_K, n_tok), jnp.int32),
                   jax.ShapeDtypeStruct((TOP_K, n_tok), F32),
                   jax.ShapeDtypeStruct((TOP_K, n_tok), jnp.int32),
                   jax.ShapeDtypeStruct((N_EXPERTS, 128), F32)],
        compiler_params=_cparams(("arbitrary",)),
        name="merge_out_norm_route",
    )(x, ya, yf, yh, p, wa, wf, wh, wo, mod4, gain2, mod4, mod4, wrt, br)


def _row_copy(src_ref, row, dst_ref, r, sem):
    return pltpu.make_async_copy(src_ref.at[pl.ds(row, 1), :], dst_ref.at[pl.ds(r, 1), :], sem)


def _expert_kernel(be_ref, na_ref, rt_cur_ref, rt_next_ref, h2_ref, wg_ref, wu_ref, wd_ref, o_ref,
                   xbuf, sem, wg_s, wu_s, wd_s):
    i = pl.program_id(0)
    n_act = na_ref[0]
    bm = xbuf.shape[1]
    active = i < n_act

    def gather(rt_ref, slot):
        def body(j, carry):
            r0 = pl.multiple_of(j * 8, 8)
            tile = xbuf.at[slot, pl.ds(r0, 8)]
            for u in range(8):
                _row_copy(h2_ref, rt_ref[0, 0, r0 + u], tile, u, sem.at[slot]).start()
            return carry

        lax.fori_loop(0, bm // 8, body, 0)

    @pl.when((i == 0) & active)
    def _():
        gather(rt_cur_ref, 0)

    @pl.when(i + 1 < n_act)
    def _():
        gather(rt_next_ref, (i + 1) % 2)

    prev = be_ref[jnp.maximum(i - 1, 0)]

    @pl.when(active & ((i == 0) | (be_ref[i] != prev)))
    def _():
        wg_s[...] = wg_ref[0].astype(BF16)
        wu_s[...] = wu_ref[0].astype(BF16)
        wd_s[...] = wd_ref[0].astype(BF16)

    @pl.when(active)
    def _():
        slot = i % 2
        pltpu.make_async_copy(h2_ref.at[pl.ds(0, bm), :], xbuf.at[slot], sem.at[slot]).wait()
        x = xbuf[slot].astype(BF16)
        g = _dot(x, wg_s[...])
        u = _dot(x, wu_s[...])
        h = (g * jax.nn.sigmoid(g)) * u
        o_ref[...] = _dot(h.astype(BF16), wd_s[...])

    @pl.when(jnp.logical_not(active))
    def _():
        o_ref[...] = jnp.zeros_like(o_ref)


def _experts(h2, row_tok, blk_expert, n_active, wg, wu, wd):
    bm = EXPERT_BM
    n_blk = row_tok.shape[0] // bm
    rt = row_tok.reshape(n_blk, 1, bm)
    wspec = lambda k, n: pl.BlockSpec((1, k, n), lambda i, be, na: (be[i], 0, 0))
    return pl.pallas_call(
        _expert_kernel,
        grid_spec=pltpu.PrefetchScalarGridSpec(
            num_scalar_prefetch=2,
            grid=(n_blk,),
            in_specs=[pl.BlockSpec((1, 1, bm), lambda i, be, na: (i, 0, 0), memory_space=pltpu.SMEM),
                      pl.BlockSpec((1, 1, bm), lambda i, be, na: (jnp.minimum(i + 1, n_blk - 1), 0, 0),
                                   memory_space=pltpu.SMEM),
                      pl.BlockSpec(memory_space=pl.ANY),
                      wspec(D_MODEL, EXPERT_FF), wspec(D_MODEL, EXPERT_FF), wspec(EXPERT_FF, D_MODEL)],
            out_specs=pl.BlockSpec((bm, D_MODEL), lambda i, be, na: (i, 0)),
            scratch_shapes=[pltpu.VMEM((2, bm, D_MODEL), F32), pltpu.SemaphoreType.DMA((2,)),
                            pltpu.VMEM((D_MODEL, EXPERT_FF), BF16), pltpu.VMEM((D_MODEL, EXPERT_FF), BF16),
                            pltpu.VMEM((EXPERT_FF, D_MODEL), BF16)]),
        out_shape=jax.ShapeDtypeStruct((row_tok.shape[0], D_MODEL), F32),
        compiler_params=_cparams(("arbitrary",)),
        name="moe_experts",
    )(blk_expert, n_active, rt, rt, h2, wg, wu, wd)


def _dispatch(e_idx, rank, counts, n_tok):
    bm = EXPERT_BM
    counts = counts.astype(jnp.int32)
    padded = (counts + bm - 1) // bm * bm
    pad_end = jnp.cumsum(padded)
    pad_start = pad_end - padded
    experts = jnp.arange(N_EXPERTS, dtype=jnp.int32)
    start = jnp.sum(jnp.where(e_idx[..., None] == experts, pad_start, 0), axis=-1)
    dest = start + rank
    n_rows = -(-(n_tok * TOP_K) // bm) * bm + N_EXPERTS * bm
    n_blk = n_rows // bm
    tok = jnp.tile(jnp.arange(n_tok, dtype=jnp.int32), TOP_K)
    row_tok = jnp.zeros((n_rows,), jnp.int32).at[dest.reshape(-1)].set(tok)
    blk_start = jnp.arange(n_blk, dtype=jnp.int32) * bm
    blk_expert = jnp.minimum(jnp.sum((blk_start[:, None] >= pad_end[None, :]).astype(jnp.int32), axis=-1),
                             N_EXPERTS - 1)
    n_active = (pad_end[-1] // bm).astype(jnp.int32).reshape(1)
    return row_tok, dest, blk_expert, n_active


def _combine_kernel(d_cur_ref, d_next_ref, x_ref, w_ref, g2_ref, gf_ref, ys_ref, o_ref, ybuf, sem,
                    *, final, n_tiles):
    i = pl.program_id(0)
    tm = x_ref.shape[0]

    def gather(d_ref, slot):
        def body(j, carry):
            r0 = pl.multiple_of(j * 8, 8)
            for k in range(TOP_K):
                tile = ybuf.at[slot, k, pl.ds(r0, 8)]
                for u in range(8):
                    _row_copy(ys_ref, d_ref[0, k, r0 + u], tile, u, sem.at[slot]).start()
            return carry

        lax.fori_loop(0, tm // 8, body, 0)

    @pl.when(i == 0)
    def _():
        gather(d_cur_ref, 0)

    @pl.when(i + 1 < n_tiles)
    def _():
        gather(d_next_ref, (i + 1) % 2)

    slot = i % 2
    for k in range(TOP_K):
        pltpu.make_async_copy(ys_ref.at[pl.ds(0, tm), :], ybuf.at[slot, k], sem.at[slot]).wait()
    w = w_ref[...]
    moe = ybuf[slot, 0] * w[:, 0:1] + ybuf[slot, 1] * w[:, 1:2]
    xn = x_ref[...] + g2_ref[0, 0] * moe
    if final:
        y = xn * lax.rsqrt(jnp.mean(xn * xn, axis=-1, keepdims=True) + EPS)
        xn = y * gf_ref[...]
    o_ref[...] = xn


def _combine(x, ys, dest, w_sel, mod4, gain_final, n_tok, final):
    tm = 256
    n_tiles = n_tok // tm
    row = _mod_row(tm)
    tok = lambda w: pl.BlockSpec((tm, w), lambda i: (i, 0))
    d3 = jnp.transpose(dest.reshape(TOP_K, n_tiles, tm), (1, 0, 2))
    return pl.pallas_call(
        functools.partial(_combine_kernel, final=final, n_tiles=n_tiles),
        grid=(n_tiles,),
        in_specs=[pl.BlockSpec((1, TOP_K, tm), lambda i: (i, 0, 0), memory_space=pltpu.SMEM),
                  pl.BlockSpec((1, TOP_K, tm), lambda i: (jnp.minimum(i + 1, n_tiles - 1), 0, 0),
                               memory_space=pltpu.SMEM),
                  tok(D_MODEL), tok(TOP_K),
                  pl.BlockSpec((1, 1, 1, D_MODEL), lambda i: (row(i), 5, 0, 0)),
                  pl.BlockSpec((1, D_MODEL), lambda i: (0, 0)),
                  pl.BlockSpec(memory_space=pl.ANY)],
        out_specs=tok(D_MODEL),
        out_shape=jax.ShapeDtypeStruct((n_tok if final else T_ALL, D_MODEL), F32),
        scratch_shapes=[pltpu.VMEM((2, TOP_K, tm, D_MODEL), F32), pltpu.SemaphoreType.DMA((2,))],
        compiler_params=_cparams(("arbitrary",)),
        name="moe_combine",
    )(d3, d3, x, w_sel, mod4, gain_final, ys)


def kernel(x, c, ctx, c_ctx, w_mod, b_mod, norm_mix, norm_ffn, w_in, attn_sink, conv_w, conv_b, filt_w1, filt_b1, filt_freq, filt_w2, filt_b2, filt_w3, hyena_bias, w_branch_attn, w_branch_fnet, w_branch_hyena, w_out, w_router, b_router, w_exp_gate, w_exp_up, w_exp_down, norm_final):
    mats = _dft_mats()
    cos_t, sin_t = _rope_tables()
    c8 = jnp.concatenate([c, c_ctx[None, :], jnp.zeros((8 - BATCH - 1, D_MODEL), F32)], axis=0)
    mod_all = _modulation(c8, w_mod, b_mod)
    xa = jnp.concatenate([x.reshape(T_LAT, D_MODEL), ctx.reshape(T_CTX, D_MODEL)], axis=0)
    wrt = w_router.T.astype(BF16)
    br = b_router.astype(F32).reshape(N_EXPERTS, 1)
    gain_final = norm_final.reshape(1, D_MODEL)
    out = None
    for l in range(DEPTH):
        last = l == DEPTH - 1
        with_ctx = not last
        n_tok = T_LAT if last else T_ALL
        mod4 = mod_all[l].reshape(8, N_MOD, 1, D_MODEL)
        wl = w_in[l]
        w_perm = jnp.concatenate([wl[:, G_OFF:], wl[:, H_OFF:G_OFF], wl[:, F_OFF:H_OFF], wl[:, Q_OFF:F_OFF]],
                                 axis=1).astype(BF16)
        p = _norm_proj(xa, norm_mix[l].reshape(1, D_MODEL), mod4, w_perm, T_ALL)
        ya = _attention(p, attn_sink[l], cos_t, sin_t, with_ctx)
        yf = _fourier_mix(p, mats, with_ctx)
        z3 = _short_conv(p, conv_w[l], conv_b[l], with_ctx)
        filt = (filt_w1[l], filt_b1[l], filt_freq[l], filt_w2[l], filt_b2[l], filt_w3[l])
        yh = _hyena_mix(z3, filt, hyena_bias[l], mats, with_ctx)
        xa, h2, e_idx, w_sel, rank, cnt = _merge(
            xa, ya, yf, yh, p, w_branch_attn[l].astype(BF16), w_branch_fnet[l].astype(BF16),
            w_branch_hyena[l].astype(BF16), w_out[l].astype(BF16), mod4,
            norm_ffn[l].reshape(1, D_MODEL), wrt, br, n_tok)
        row_tok, dest, blk_expert, n_active = _dispatch(e_idx, rank, cnt[:, 0], n_tok)
        ys = _experts(h2, row_tok, blk_expert, n_active, w_exp_gate[l], w_exp_up[l], w_exp_down[l])
        res = _combine(xa, ys, dest, w_sel.T, mod4, gain_final, n_tok, last)
        if last:
            out = res
        else:
            xa = res
    return out.reshape(BATCH, SEQ, D_MODEL)
```

```python
import functools
import math

import jax
import jax.numpy as jnp
from jax import lax
from jax.experimental import pallas as pl
from jax.experimental.pallas import tpu as pltpu

F32 = jnp.float32
BF16 = jnp.bfloat16

D_MODEL = 1024
BATCH = 4
SEQ = 4096
DEPTH = 4
GRID_W = 64
CTX_LEN = 256
EPS = 1e-6
N_MOD = 6

HEAD_DIM = 64
N_Q_HEADS = 8
N_KV_HEADS = 2
Q_PER_KV = N_Q_HEADS // N_KV_HEADS
ATTN_BLOCK = 128
ROPE_BASE = 10000.0

FNET_GROUPS = 4
FNET_GROUP_DIM = 128
FNET_WIDTH = FNET_GROUPS * FNET_GROUP_DIM

HYENA_WIDTH = 512
HYENA_ORDER = 2
FILTER_EMB = 33
FILTER_BANDS = (FILTER_EMB - 1) // 2
FILTER_HIDDEN = 64
DECAY_TARGET = 1e-2
FAST_DECAY_PCT = 0.3
SLOW_DECAY_PCT = 1.5

ATTN_WIDTH = N_Q_HEADS * HEAD_DIM
KV_WIDTH = N_KV_HEADS * HEAD_DIM
Q_OFF = 0
K_OFF = Q_OFF + ATTN_WIDTH
V_OFF = K_OFF + KV_WIDTH
F_OFF = V_OFF + KV_WIDTH
H_OFF = F_OFF + FNET_WIDTH
G_OFF = H_OFF + (HYENA_ORDER + 1) * HYENA_WIDTH
IN_WIDTH = G_OFF + 3 * D_MODEL

N_EXPERTS = 16
N_GROUPS = 4
EXPERTS_PER_GROUP = N_EXPERTS // N_GROUPS
TOP_K = 2
EXPERT_FF = 1024

T_LAT = BATCH * SEQ
T_CTX = BATCH * CTX_LEN
T_ALL = T_LAT + T_CTX

PG_OFF = G_OFF
PH_OFF = H_OFF
PF_OFF = F_OFF
PQ_OFF = Q_OFF
PK_OFF = K_OFF
PV_OFF = V_OFF

HY_N = 2 * SEQ
HY_N2 = 64
HY_N1 = HY_N // HY_N2
FN_N = 64

PITCH = HY_N2 + 8
UNROLL = 8
TM = 512
EXPERT_BM = 512
VMEM_LIMIT = 52 * 1024 * 1024


def _cparams(sem, vmem=VMEM_LIMIT):
    return pltpu.CompilerParams(dimension_semantics=sem, vmem_limit_bytes=vmem)


def _dot(a, b):
    return jnp.dot(a, b, preferred_element_type=F32)


def _cis(expo, n):
    ang = (2.0 * math.pi / n) * jnp.mod(expo, n).astype(F32)
    return jnp.cos(ang), jnp.sin(ang)


def _real_form(gr, gi):
    return jnp.concatenate([jnp.concatenate([gr, -gi], axis=-1), jnp.concatenate([gi, gr], axis=-1)], axis=-2)


def _dft_mats():
    ar = lambda n: jnp.arange(n, dtype=jnp.int32)
    m = {}
    c, s = _cis(ar(HY_N1)[:, None] * ar(HY_N1 // 2)[None, :], HY_N1)
    m['hy_lead_f'] = _real_form(c, -s).astype(BF16)
    c, s = _cis(ar(HY_N1 // 2)[:, None] * ar(HY_N1)[None, :], HY_N1)
    m['hy_lead_i'] = _real_form(c, s).astype(BF16)
    c, s = _cis(ar(HY_N1)[:, None] * ar(HY_N1 // 2)[None, :], HY_N1)
    m['hy_lead_kh'] = jnp.concatenate([c, -s], axis=0).astype(BF16)
    a = ar(HY_N1)[:, None, None]
    k2 = ar(HY_N2)[None, :, None]
    n2 = ar(HY_N2)[None, None, :]
    c, s = _cis(n2 * (a + HY_N1 * k2), HY_N)
    m['hy_slab_f2'] = jnp.concatenate([c, -s], axis=-1).astype(BF16)
    ct = jnp.swapaxes(c, 1, 2) * (1.0 / HY_N)
    st = jnp.swapaxes(s, 1, 2) * (1.0 / HY_N)
    m['hy_slab_i2'] = jnp.concatenate([ct, st], axis=-1).astype(BF16)
    nc = 2 * CTX_LEN
    c, s = _cis(ar(nc)[:, None] * ar(CTX_LEN)[None, :], nc)
    m['hc_f'] = _real_form(c, -s).astype(BF16)
    c, s = _cis(ar(CTX_LEN)[:, None] * ar(nc)[None, :], nc)
    m['hc_i'] = _real_form(c * (1.0 / nc), s * (1.0 / nc)).astype(BF16)
    c, s = _cis(ar(nc)[:, None] * ar(CTX_LEN)[None, :], nc)
    m['hc_kh'] = jnp.concatenate([c, -s], axis=0).astype(BF16)
    c, s = _cis(ar(FNET_GROUP_DIM)[:, None] * ar(FNET_GROUP_DIM)[None, :], FNET_GROUP_DIM)
    m['fn_chan'] = jnp.concatenate([c, -s], axis=1).astype(BF16)
    c, s = _cis(ar(FN_N)[:, None] * ar(FN_N)[None, :], FN_N)
    m['fn_lead'] = _real_form(c, -s).astype(BF16)
    a = ar(FN_N)[:, None, None]
    k1 = ar(FN_N)[None, :, None]
    n1 = ar(FN_N)[None, None, :]
    scale = 1.0 / math.sqrt(SEQ * FNET_GROUP_DIM)
    c, s = _cis(n1 * (a + FN_N * k1), SEQ)
    m['fn_slab'] = jnp.concatenate([c * scale, s * scale], axis=-1).astype(BF16)
    scale = 1.0 / math.sqrt(CTX_LEN * FNET_GROUP_DIM)
    c, s = _cis(ar(CTX_LEN)[:, None] * ar(CTX_LEN)[None, :], CTX_LEN)
    m['fc'] = jnp.concatenate([c * scale, s * scale], axis=-1).astype(BF16)
    return m


def _mod_kernel(c_ref, w_ref, b_ref, o_ref):
    c = c_ref[...]
    s = c * jax.nn.sigmoid(c)
    o_ref[0] = _dot(s.astype(BF16), w_ref[0].astype(BF16)) + b_ref[0]


def _modulation(c8, w_mod, b_mod):
    tn = 1536
    n = N_MOD * D_MODEL
    return pl.pallas_call(
        _mod_kernel,
        grid=(DEPTH, n // tn),
        in_specs=[pl.BlockSpec((8, D_MODEL), lambda l, j: (0, 0)),
                  pl.BlockSpec((1, D_MODEL, tn), lambda l, j: (l, 0, j)),
                  pl.BlockSpec((1, 1, tn), lambda l, j: (l, 0, j))],
        out_specs=pl.BlockSpec((1, 8, tn), lambda l, j: (l, 0, j)),
        out_shape=jax.ShapeDtypeStruct((DEPTH, 8, n), F32),
        compiler_params=_cparams(("parallel", "parallel")),
        name="adaln_modulation",
    )(c8, w_mod, b_mod.reshape(DEPTH, 1, n))


def _mod_row(tm):
    tiles_per_batch = SEQ // tm
    return lambda i: jnp.minimum(i // tiles_per_batch, BATCH)


def _rms_mod(x, g, sh, sc):
    y = x * lax.rsqrt(jnp.mean(x * x, axis=-1, keepdims=True) + EPS)
    return (y * g) * (1.0 + sc) + sh


def _norm_proj_kernel(x_ref, g_ref, sh_ref, sc_ref, w_ref, o_ref, w_s):
    @pl.when(pl.program_id(1) == 0)
    def _():
        w_s[...] = w_ref[...].astype(BF16)

    h = _rms_mod(x_ref[...], g_ref[...], sh_ref[0, 0], sc_ref[0, 0]).astype(BF16)
    o_ref[...] = _dot(h, w_s[...])


def _norm_proj(x, gain, mod4, w, n_tok):
    tm = TM
    n_out = w.shape[1]
    tn = n_out // 2
    row = _mod_row(tm)
    return pl.pallas_call(
        _norm_proj_kernel,
        grid=(n_out // tn, n_tok // tm),
        in_specs=[pl.BlockSpec((tm, D_MODEL), lambda j, i: (i, 0)),
                  pl.BlockSpec((1, D_MODEL), lambda j, i: (0, 0)),
                  pl.BlockSpec((1, 1, 1, D_MODEL), lambda j, i: (row(i), 0, 0, 0)),
                  pl.BlockSpec((1, 1, 1, D_MODEL), lambda j, i: (row(i), 1, 0, 0)),
                  pl.BlockSpec((D_MODEL, tn), lambda j, i: (0, j))],
        out_specs=pl.BlockSpec((tm, tn), lambda j, i: (i, j)),
        out_shape=jax.ShapeDtypeStruct((T_ALL, n_out), F32),
        scratch_shapes=[pltpu.VMEM((D_MODEL, tn), BF16)],
        compiler_params=_cparams(("arbitrary", "arbitrary")),
        name="norm_in_proj",
    )(x, gain, mod4, mod4, w)


def _softmax_pv(qh, k_parts, v_parts, masks, sink):
    nt = (((1,), (1,)), ((), ()))
    scores = []
    for kp, mk in zip(k_parts, masks):
        s = lax.dot_general(qh, kp, nt, preferred_element_type=F32)
        if mk is not None:
            s = jnp.where(mk, s, -1e30)
        scores.append(s)
    m = sink
    for s in scores:
        m = jnp.maximum(m, jnp.max(s, axis=-1, keepdims=True))
    es = [jnp.exp(s - m) for s in scores]
    den = jnp.exp(sink - m)
    for e in es:
        den = den + jnp.sum(e, axis=-1, keepdims=True)
    inv = 1.0 / den
    o = None
    for e, vp in zip(es, v_parts):
        t = _dot((e * inv).astype(BF16), vp)
        o = t if o is None else o + t
    return o


def _attn_kernel(sink_ref, q_ref, km_ref, k0_ref, kp_ref, vm_ref, v0_ref, vp_ref, kc_ref, vc_ref,
                 cos_ref, sin_ref, o_ref, *, nb):
    n = pl.program_id(1)
    blk = ATTN_BLOCK
    lane = lax.broadcasted_iota(jnp.int32, (blk, 128), 1)
    first = (lane % 32) < 16

    def rope(x, blk_idx):
        r0 = pl.multiple_of(blk_idx * blk, blk)
        c = cos_ref[pl.ds(r0, blk), :]
        s = sin_ref[pl.ds(r0, blk), :]
        sw = jnp.where(first, pltpu.roll(x, 112, 1), pltpu.roll(x, 16, 1))
        return x * c + sw * s

    nm = jnp.maximum(n - 1, 0)
    npl = jnp.minimum(n + 1, nb - 1)
    kall = jnp.concatenate([rope(km_ref[...], nm), rope(k0_ref[...], n), rope(kp_ref[...], npl), kc_ref[...]],
                           axis=0)
    vall = jnp.concatenate([vm_ref[...], v0_ref[...], vp_ref[...], vc_ref[...]], axis=0)
    nk = kall.shape[0]
    kswap = pltpu.roll(kall, HEAD_DIM, 1)
    vswap = pltpu.roll(vall, HEAD_DIM, 1)
    lo = lax.broadcasted_iota(jnp.int32, (nk, 128), 1) < HEAD_DIM

    r = lax.broadcasted_iota(jnp.int32, (2 * blk, blk), 0) % blk
    cidx = lax.broadcasted_iota(jnp.int32, (2 * blk, blk), 1)
    ok_prev = jnp.where(cidx >= r, (n > 0).astype(jnp.int32), 0) > 0
    ok_next = jnp.where(cidx <= r, (n < nb - 1).astype(jnp.int32), 0) > 0
    top_rows = lax.broadcasted_iota(jnp.int32, (2 * blk, 1), 0) < blk
    neg = jnp.float32(-1e30)

    scale = HEAD_DIM ** -0.5
    q2 = [(rope(q_ref[:, p * 128:(p + 1) * 128], n) * scale).astype(BF16) for p in range(N_Q_HEADS // 2)]
    nt = (((1,), (1,)), ((), ()))
    for h in range(N_KV_HEADS):
        ka, kb = (kall, kswap) if h == 0 else (kswap, kall)
        va, vb = (vall, vswap) if h == 0 else (vswap, vall)
        kbd = jnp.concatenate([jnp.where(lo, ka, 0.0), jnp.where(lo, 0.0, kb)], axis=0).astype(BF16)
        vbd = jnp.concatenate([jnp.where(lo, va, 0.0), jnp.where(lo, 0.0, vb)], axis=0).astype(BF16)
        q4 = jnp.concatenate([q2[2 * h], q2[2 * h + 1]], axis=0)
        s = lax.dot_general(q4, kbd, nt, preferred_element_type=F32)
        probs = []
        for c in range(2):
            base = c * nk
            sink = jnp.where(top_rows, sink_ref[Q_PER_KV * h + c], sink_ref[Q_PER_KV * h + 2 + c])
            parts = [jnp.where(ok_prev, s[:, base:base + blk], neg),
                     s[:, base + blk:base + 2 * blk],
                     jnp.where(ok_next, s[:, base + 2 * blk:base + 3 * blk], neg),
                     s[:, base + 3 * blk:base + nk]]
            m = sink
            for part in parts:
                m = jnp.maximum(m, jnp.max(part, axis=-1, keepdims=True))
            es = [jnp.exp(part - m) for part in parts]
            den = jnp.exp(sink - m)
            for e in es:
                den = den + jnp.sum(e, axis=-1, keepdims=True)
            inv = 1.0 / den
            probs += [(e * inv).astype(BF16) for e in es]
        o = _dot(jnp.concatenate(probs, axis=1), vbd)
        w0 = h * Q_PER_KV * HEAD_DIM
        o_ref[:, w0:w0 + 128] = o[:blk].astype(o_ref.dtype)
        o_ref[:, w0 + 128:w0 + 256] = o[blk:].astype(o_ref.dtype)


def _ctx_attn_kernel(sink_ref, q_ref, kc_ref, vc_ref, o_ref):
    kc = kc_ref[...].astype(BF16)
    vc = vc_ref[...].astype(BF16)
    scale = HEAD_DIM ** -0.5
    outs = []
    for pair in range(N_Q_HEADS // 2):
        q2 = (q_ref[:, pair * 128:(pair + 1) * 128] * scale).astype(BF16)
        for sub in range(2):
            head = 2 * pair + sub
            kvh = head // Q_PER_KV
            sl = slice(kvh * HEAD_DIM, (kvh + 1) * HEAD_DIM)
            qh = q2[:, sub * HEAD_DIM:(sub + 1) * HEAD_DIM]
            outs.append(_softmax_pv(qh, [kc[:, sl]], [vc[:, sl]], [None], sink_ref[head]))
    o_ref[...] = jnp.concatenate(outs, axis=-1).astype(o_ref.dtype)


def _attention(p, sink, cos_t, sin_t, with_ctx):
    blk = ATTN_BLOCK
    nb = SEQ // blk
    qc, kcol, vcol = PQ_OFF // ATTN_WIDTH, PK_OFF // KV_WIDTH, PV_OFF // KV_WIDTH
    ctx_blk = T_LAT // CTX_LEN
    smem = pl.BlockSpec(memory_space=pltpu.SMEM)

    def kv_spec(col, d):
        return pl.BlockSpec((blk, KV_WIDTH),
                            lambda b, n: (b * nb + jnp.clip(n + d, 0, nb - 1), col))

    ya = pl.pallas_call(
        functools.partial(_attn_kernel, nb=nb),
        grid=(BATCH, nb),
        in_specs=[smem,
                  pl.BlockSpec((blk, ATTN_WIDTH), lambda b, n: (b * nb + n, qc)),
                  kv_spec(kcol, -1), kv_spec(kcol, 0), kv_spec(kcol, 1),
                  kv_spec(vcol, -1), kv_spec(vcol, 0), kv_spec(vcol, 1),
                  pl.BlockSpec((CTX_LEN, KV_WIDTH), lambda b, n: (ctx_blk + b, kcol)),
                  pl.BlockSpec((CTX_LEN, KV_WIDTH), lambda b, n: (ctx_blk + b, vcol)),
                  pl.BlockSpec((SEQ, KV_WIDTH), lambda b, n: (0, 0)),
                  pl.BlockSpec((SEQ, KV_WIDTH), lambda b, n: (0, 0))],
        out_specs=pl.BlockSpec((blk, ATTN_WIDTH), lambda b, n: (b * nb + n, 0)),
        out_shape=jax.ShapeDtypeStruct((T_ALL, ATTN_WIDTH), BF16),
        compiler_params=_cparams(("parallel", "parallel")),
        name="banded_attention",
    )(sink, p, p, p, p, p, p, p, p, p, cos_t, sin_t)
    if not with_ctx:
        return ya
    cb = CTX_LEN // blk
    lat_blk = T_LAT // blk

    def alias_kernel(sink_ref, q_ref, kc_ref, vc_ref, ya_in_ref, o_ref):
        del ya_in_ref
        _ctx_attn_kernel(sink_ref, q_ref, kc_ref, vc_ref, o_ref)

    return pl.pallas_call(
        alias_kernel,
        grid=(BATCH, cb),
        in_specs=[smem,
                  pl.BlockSpec((blk, ATTN_WIDTH), lambda b, n: (lat_blk + b * cb + n, qc)),
                  pl.BlockSpec((CTX_LEN, KV_WIDTH), lambda b, n: (ctx_blk + b, kcol)),
                  pl.BlockSpec((CTX_LEN, KV_WIDTH), lambda b, n: (ctx_blk + b, vcol)),
                  pl.BlockSpec(memory_space=pl.ANY)],
        out_specs=pl.BlockSpec((blk, ATTN_WIDTH), lambda b, n: (lat_blk + b * cb + n, 0)),
        out_shape=jax.ShapeDtypeStruct((T_ALL, ATTN_WIDTH), BF16),
        input_output_aliases={4: 0},
        compiler_params=_cparams(("parallel", "parallel")),
        name="context_attention",
    )(sink, p, p, p, ya)


def _rope_tables():
    n_freq = HEAD_DIM // 4
    freqs = ROPE_BASE ** (-jnp.arange(n_freq, dtype=F32) / n_freq)
    t = jnp.arange(SEQ, dtype=jnp.int32)
    rows = (t // GRID_W).astype(F32)[:, None] * freqs
    cols = (t % GRID_W).astype(F32)[:, None] * freqs
    cos_h = jnp.concatenate([jnp.cos(rows), jnp.cos(rows), jnp.cos(cols), jnp.cos(cols)], axis=-1)
    sin_h = jnp.concatenate([-jnp.sin(rows), jnp.sin(rows), -jnp.sin(cols), jnp.sin(cols)], axis=-1)
    return jnp.tile(cos_h, (1, 2)), jnp.tile(sin_h, (1, 2))


def _lead_kernel(*refs, n_in, cmul, epi):
    m_ref = refs[0]
    x_refs = refs[1:1 + n_in]
    pos = 1 + n_in
    xs = []
    for r in x_refs:
        v = r[...]
        xs.append(v.reshape(-1, v.shape[-1]))
    x = xs[0] if n_in == 1 else jnp.concatenate(xs, axis=0)
    if cmul:
        k = refs[pos][...]
        pos += 1
        k = k.reshape(-1, k.shape[-1])
        half = x.shape[0] // 2
        xr, xi, kr, ki = x[:half], x[half:], k[:half], k[half:]
        x = jnp.concatenate([xr * kr - xi * ki, xr * ki + xi * kr], axis=0)
    res = _dot(m_ref[...], x.astype(BF16))
    if epi:
        g_ref, y_ref, b_ref = refs[pos:pos + 3]
        pos += 3
        g = g_ref[...]
        y = y_ref[...]
        res = g.reshape(-1, g.shape[-1]) * (res + y.reshape(-1, y.shape[-1]) * b_ref[...])
    o_ref = refs[pos]
    o_ref[...] = res.reshape(o_ref.shape).astype(o_ref.dtype)


def _lead(mat, xs, x_specs, grid, out_shape, out_spec, *, kspec=None, epi=None, alias_to=None, name):
    ins = [mat] + list(xs)
    specs = [pl.BlockSpec(mat.shape, lambda *a: (0, 0))] + list(x_specs)
    if kspec is not None:
        ins.append(kspec[0])
        specs.append(kspec[1])
    if epi is not None:
        for arr, sp in epi:
            ins.append(arr)
            specs.append(sp)
    kern = functools.partial(_lead_kernel, n_in=len(xs), cmul=kspec is not None, epi=epi is not None)
    aliases = {}
    if alias_to is not None:
        aliases = {len(ins): 0}
        ins.append(alias_to)
        specs.append(pl.BlockSpec(memory_space=pl.ANY))
        inner = kern

        def kern(*refs):
            inner(*refs[:-2], refs[-1])

    return pl.pallas_call(
        kern, grid=grid, in_specs=specs, out_specs=out_spec, out_shape=out_shape,
        input_output_aliases=aliases,
        compiler_params=_cparams(("parallel",) * len(grid)), name=name,
    )(*ins)


def _cstack(xr, xi):
    return jnp.concatenate([jnp.concatenate([xr, xi], axis=1), jnp.concatenate([-xi, xr], axis=1)], axis=0)


def _fnet_kernel(u_ref, mc_ref, ml_ref, ms_ref, o_ref, zr, zi, are, aim):
    n = FN_N
    pitch = PITCH
    gd = FNET_GROUP_DIM
    rows = 4 * n
    mc = mc_ref[...]

    def chan(i, carry):
        r_in = pl.multiple_of(i * rows, rows)
        z = _dot(u_ref[pl.ds(r_in, rows), :].astype(BF16), mc)
        for q in range(rows // n):
            r_out = pl.multiple_of((i * (rows // n) + q) * pitch, 8)
            zr[pl.ds(r_out, n), :] = z[q * n:(q + 1) * n, :gd]
            zi[pl.ds(r_out, n), :] = z[q * n:(q + 1) * n, gd:]
        return carry

    lax.fori_loop(0, SEQ // rows, chan, 0)
    ml = ml_ref[...]

    def lead(i, carry):
        for u in range(UNROLL):
            n1 = i * UNROLL + u
            x = jnp.concatenate([zr[pl.ds(n1, n, stride=pitch), :], zi[pl.ds(n1, n, stride=pitch), :]],
                                axis=0).astype(BF16)
            r = _dot(ml, x)
            are[pl.ds(n1, n, stride=pitch), :] = r[:n]
            aim[pl.ds(n1, n, stride=pitch), :] = r[n:]
        return carry

    lax.fori_loop(0, n // UNROLL, lead, 0)

    def slab(i, carry):
        for u in range(UNROLL):
            k2 = i * UNROLL + u
            r0 = pl.multiple_of(k2 * pitch, 8)
            x = jnp.concatenate([are[pl.ds(r0, n), :], aim[pl.ds(r0, n), :]], axis=0).astype(BF16)
            o_ref[pl.ds(k2, n, stride=n), :] = _dot(ms_ref[k2], x)
        return carry

    lax.fori_loop(0, n // UNROLL, slab, 0)


def _fnet_ctx_kernel(u_ref, mc_ref, mf_ref, yf_in_ref, o_ref):
    del yf_in_ref
    gd = FNET_GROUP_DIM
    mc = mc_ref[...]
    mf = mf_ref[...]
    for g in range(FNET_GROUPS):
        z = _dot(u_ref[:, g * gd:(g + 1) * gd].astype(BF16), mc)
        x = jnp.concatenate([z[:, :gd], z[:, gd:]], axis=0).astype(BF16)
        o_ref[:, g * gd:(g + 1) * gd] = _dot(mf, x)


def _fourier_mix(p, mats, with_ctx):
    gd = FNET_GROUP_DIM
    col0 = PF_OFF // gd
    full = lambda a: pl.BlockSpec(a.shape, lambda *i: (0,) * a.ndim)
    scr = pltpu.VMEM((FN_N * PITCH, gd), F32)
    yf = pl.pallas_call(
        _fnet_kernel,
        grid=(BATCH, FNET_GROUPS),
        in_specs=[pl.BlockSpec((SEQ, gd), lambda b, g: (b, col0 + g)),
                  full(mats['fn_chan']), full(mats['fn_lead']), full(mats['fn_slab'])],
        out_specs=pl.BlockSpec((SEQ, gd), lambda b, g: (b, g)),
        out_shape=jax.ShapeDtypeStruct((T_ALL if with_ctx else T_LAT, FNET_WIDTH), F32),
        scratch_shapes=[scr, scr, scr, scr],
        compiler_params=_cparams(("parallel", "parallel")),
        name="fnet_latent",
    )(p, mats['fn_chan'], mats['fn_lead'], mats['fn_slab'])
    if not with_ctx:
        return yf
    blk0 = T_LAT // CTX_LEN
    return pl.pallas_call(
        _fnet_ctx_kernel,
        grid=(BATCH,),
        in_specs=[pl.BlockSpec((pl.Element(CTX_LEN), pl.Element(FNET_WIDTH)),
                               lambda b: ((blk0 + b) * CTX_LEN, PF_OFF)),
                  full(mats['fn_chan']), full(mats['fc']), pl.BlockSpec(memory_space=pl.ANY)],
        out_specs=pl.BlockSpec((CTX_LEN, FNET_WIDTH), lambda b: (blk0 + b, 0)),
        out_shape=jax.ShapeDtypeStruct((T_ALL, FNET_WIDTH), F32),
        input_output_aliases={3: 0},
        compiler_params=_cparams(("parallel",)),
        name="fnet_ctx",
    )(p, mats['fn_chan'], mats['fc'], yf)


def _short_conv_kernel(u_ref, w_ref, b_ref, o_ref, *, rows, chunk):
    w0 = w_ref[0:1, :]
    w1 = w_ref[1:2, :]
    w2 = w_ref[2:3, :]
    bias = b_ref[...]
    width = u_ref.shape[-1]
    ridx = lax.broadcasted_iota(jnp.int32, (chunk, width), 0)
    n_chunks = rows // chunk
    for ci in range(n_chunks):
        r0 = ci * chunk
        cur = u_ref[r0:r0 + chunk, :]
        if ci > 0:
            prev_row = u_ref[r0 - 8:r0, :][7:8, :]
        else:
            prev_row = jnp.zeros((1, width), F32)
        if ci < n_chunks - 1:
            next_row = u_ref[r0 + chunk:r0 + chunk + 8, :][0:1, :]
        else:
            next_row = jnp.zeros((1, width), F32)
        up = jnp.where(ridx == 0, prev_row, pltpu.roll(cur, 1, 0))
        dn = jnp.where(ridx == chunk - 1, next_row, pltpu.roll(cur, chunk - 1, 0))
        o_ref[0, r0:r0 + chunk, :] = up * w0 + cur * w1 + dn * w2 + bias


def _short_conv(p, conv_w, conv_b, with_ctx):
    cw = 256
    hw = (HYENA_ORDER + 1) * HYENA_WIDTH
    ncol = hw // cw
    per = HYENA_WIDTH // cw
    col0 = PH_OFF // cw
    out_shape = jax.ShapeDtypeStruct((HYENA_ORDER + 1, T_ALL if with_ctx else T_LAT, HYENA_WIDTH), F32)
    b2 = conv_b.reshape(1, hw)

    def call(rows, blk0, alias):
        kern = functools.partial(_short_conv_kernel, rows=rows, chunk=min(rows, 256))
        ins = [p, conv_w, b2]
        specs = [pl.BlockSpec((rows, cw), lambda b, j: (blk0 + b, col0 + j)),
                 pl.BlockSpec((3, cw), lambda b, j: (0, j)),
                 pl.BlockSpec((1, cw), lambda b, j: (0, j))]
        aliases = {}
        if alias is not None:
            ins.append(alias)
            specs.append(pl.BlockSpec(memory_space=pl.ANY))
            aliases = {3: 0}
            inner = kern

            def kern(u_ref, w_ref, b_ref, a_ref, o_ref):
                del a_ref
                inner(u_ref, w_ref, b_ref, o_ref)

        return pl.pallas_call(
            kern, grid=(BATCH, ncol), in_specs=specs,
            out_specs=pl.BlockSpec((1, rows, cw), lambda b, j: (j // per, blk0 + b, j % per)),
            out_shape=out_shape, input_output_aliases=aliases,
            compiler_params=_cparams(("parallel", "parallel")), name="hyena_short_conv",
        )(*ins)

    z3 = call(SEQ, 0, None)
    if with_ctx:
        z3 = call(CTX_LEN, T_LAT // CTX_LEN, z3)
    return z3


def _filter_mlp_kernel(ft_ref, w1_ref, b1_ref, fq_ref, w2_ref, b2_ref, o_ref):
    fq = fq_ref[...]
    h = jnp.sin(fq * (_dot(ft_ref[...].astype(BF16), w1_ref[...]) + b1_ref[...]))
    h = jnp.sin(fq * (_dot(h.astype(BF16), w2_ref[...]) + b2_ref[...]))
    o_ref[...] = h.astype(o_ref.dtype)


def _filter_kernel(h_ref, t_ref, w3f_ref, w3b_ref, dl_ref, m1_ref, *rest, n, dense):
    hb = h_ref[...]
    decay = jnp.exp(-t_ref[...] * dl_ref[...])
    tf = _dot(hb, w3f_ref[...]) * decay
    tb = _dot(hb, w3b_ref[...]) * decay
    tb = jnp.where(lax.broadcasted_iota(jnp.int32, tb.shape, 0) == 0, 0.0, tb)
    scale = 1.0 / (jnp.sum(jnp.abs(tf), axis=0, keepdims=True) + jnp.sum(jnp.abs(tb), axis=0, keepdims=True))
    cw = tf.shape[1]
    if dense:
        o_ref = rest[0]
        r = _dot(m1_ref[...], jnp.concatenate([tf, tb], axis=1).astype(BF16))
        nc = r.shape[0] // 2
        o_ref[0:nc, :] = (r[:nc, :cw] + r[:nc, cw:]) * scale
        o_ref[nc:, :] = (r[nc:, :cw] - r[nc:, cw:]) * scale
        return
    f2_ref, o_ref, tf_s, tb_s, are_f, aim_f, are_b, aim_b = rest
    tf_s[...] = tf
    tb_s[...] = tb
    half = HY_N1 // 2
    m1 = m1_ref[...]

    def stage1(i, carry):
        for u in range(UNROLL):
            n2 = i * UNROLL + u
            x = jnp.concatenate([tf_s[pl.ds(n2, half, stride=HY_N2), :], tb_s[pl.ds(n2, half, stride=HY_N2), :]],
                                axis=1).astype(BF16)
            r = _dot(m1, x)
            rows = pl.ds(n2, HY_N1, stride=PITCH)
            are_f[rows, :] = r[:HY_N1, :cw]
            are_b[rows, :] = r[:HY_N1, cw:]
            aim_f[rows, :] = r[HY_N1:, :cw]
            aim_b[rows, :] = r[HY_N1:, cw:]
        return carry

    lax.fori_loop(0, HY_N2 // UNROLL, stage1, 0)

    def stage2(i, carry):
        for u in range(UNROLL):
            k1 = i * UNROLL + u
            rows = pl.ds(pl.multiple_of(k1 * PITCH, 8), HY_N2)
            ar = jnp.concatenate([are_f[rows, :], are_b[rows, :]], axis=1)
            ai = jnp.concatenate([aim_f[rows, :], aim_b[rows, :]], axis=1)
            t = _dot(f2_ref[k1], _cstack(ar, ai).astype(BF16))
            o_ref[0, k1] = (t[:, 0:cw] + t[:, cw:2 * cw]) * scale
            o_ref[1, k1] = (t[:, 2 * cw:3 * cw] - t[:, 3 * cw:4 * cw]) * scale
        return carry

    lax.fori_loop(0, HY_N1 // UNROLL, stage2, 0)


def _filter_feats(n):
    pos = jnp.arange(n, dtype=F32)
    t = pos / max(n - 1, 1)
    omega = 2.0 * math.pi * pos / n
    bands = jnp.linspace(1e-4, FILTER_BANDS - 1, FILTER_BANDS, dtype=F32)
    feats = jnp.concatenate([t[:, None], jnp.cos(omega[:, None] * bands), -jnp.sin(omega[:, None] * bands)], axis=-1)
    return jnp.pad(feats, ((0, 0), (0, 128 - FILTER_EMB))), t[:, None]


def _filter_spectrum(n, filt, mats):
    w1, b1, freq, w2, b2, w3 = filt
    dense = n == CTX_LEN
    cw = 128
    nch = HYENA_WIDTH // cw
    feats, t = _filter_feats(n)
    w1p = jnp.pad(w1, ((0, 128 - FILTER_EMB), (0, 0))).astype(BF16)
    deltas = jnp.abs(jnp.linspace(math.log(DECAY_TARGET) / SLOW_DECAY_PCT, math.log(DECAY_TARGET) / FAST_DECAY_PCT,
                                  HYENA_WIDTH, dtype=F32)).reshape(1, HYENA_WIDTH)
    full = lambda a: pl.BlockSpec(a.shape, lambda *i: (0,) * a.ndim)
    row = lambda a: a.reshape(1, -1)
    w3b16 = w3.astype(BF16)
    tap_spec = lambda d: pl.BlockSpec((FILTER_HIDDEN, cw), lambda o, ch: (0, (o * 2 + d) * nch + ch))
    m1 = mats['hc_kh'] if dense else mats['hy_lead_kh']
    mlp_ins = [feats, w1p, row(b1), row(freq), w2.astype(BF16), row(b2)]
    hb = pl.pallas_call(
        _filter_mlp_kernel, grid=(1,), in_specs=[full(a) for a in mlp_ins],
        out_specs=pl.BlockSpec((n, FILTER_HIDDEN), lambda i: (0, 0)),
        out_shape=jax.ShapeDtypeStruct((n, FILTER_HIDDEN), BF16),
        compiler_params=_cparams(("arbitrary",)), name="hyena_filter_mlp",
    )(*mlp_ins)
    ins = [hb, t, w3b16, w3b16, deltas, m1]
    specs = [full(hb), full(t), tap_spec(0), tap_spec(1), pl.BlockSpec((1, cw), lambda o, ch: (0, ch)), full(m1)]
    if dense:
        nc = 2 * n
        out_shape = jax.ShapeDtypeStruct((HYENA_ORDER, 2 * nc, HYENA_WIDTH), F32)
        out_spec = pl.BlockSpec((None, 2 * nc, cw), lambda o, ch: (o, 0, ch))
        scratch = []
    else:
        ins.append(mats['hy_slab_f2'])
        specs.append(full(mats['hy_slab_f2']))
        out_shape = jax.ShapeDtypeStruct((HYENA_ORDER, 2, HY_N1, HY_N2, HYENA_WIDTH), F32)
        out_spec = pl.BlockSpec((None, 2, HY_N1, HY_N2, cw), lambda o, ch: (o, 0, 0, 0, ch))
        scratch = [pltpu.VMEM((n, cw), F32)] * 2 + [pltpu.VMEM((HY_N1 * PITCH, cw), F32)] * 4
    return pl.pallas_call(
        functools.partial(_filter_kernel, n=n, dense=dense),
        grid=(HYENA_ORDER, nch), in_specs=specs, out_specs=out_spec, out_shape=out_shape,
        scratch_shapes=scratch,
        compiler_params=_cparams(("parallel", "parallel")),
        name="hyena_filter_ctx" if dense else "hyena_filter",
    )(*ins)


def _hyena_conv_kernel(y_ref, g_ref, k_ref, m1_ref, f2f_ref, f2i_ref, m3_ref, b_ref, o_ref, are, aim):
    half = HY_N1 // 2
    m1 = m1_ref[...]

    def stage1(i, carry):
        for u in range(UNROLL):
            n2 = i * UNROLL + u
            x = jnp.concatenate([y_ref[pl.ds(n2, half, stride=HY_N2), :],
                                 y_ref[pl.ds(SEQ + n2, half, stride=HY_N2), :]], axis=0).astype(BF16)
            r = _dot(m1, x)
            are[pl.ds(n2, HY_N1, stride=PITCH), :] = r[:HY_N1]
            aim[pl.ds(n2, HY_N1, stride=PITCH), :] = r[HY_N1:]
        return carry

    lax.fori_loop(0, HY_N2 // UNROLL, stage1, 0)
    cw = o_ref.shape[-1]

    def stage2(i, carry):
        for u in range(UNROLL):
            k1 = i * UNROLL + u
            r0 = pl.multiple_of(k1 * PITCH, 8)
            y = _dot(f2f_ref[k1], _cstack(are[pl.ds(r0, HY_N2), :], aim[pl.ds(r0, HY_N2), :]).astype(BF16))
            yr, yi = y[:, :cw], y[:, cw:]
            kr, ki = k_ref[0, k1], k_ref[1, k1]
            w = _dot(f2i_ref[k1], _cstack(yr * kr - yi * ki, yr * ki + yi * kr).astype(BF16))
            are[pl.ds(r0, HY_N2), :] = w[:, :cw]
            aim[pl.ds(r0, HY_N2), :] = w[:, cw:]
        return carry

    lax.fori_loop(0, HY_N1 // UNROLL, stage2, 0)
    m3 = m3_ref[...]
    bias = b_ref[...]

    def stage3(i, carry):
        for u in range(UNROLL):
            n2 = i * UNROLL + u
            bn = jnp.concatenate([are[pl.ds(n2, HY_N1, stride=PITCH), :], aim[pl.ds(n2, HY_N1, stride=PITCH), :]],
                                 axis=0).astype(BF16)
            y = _dot(m3, bn)
            for b in range(2):
                rows = pl.ds(b * SEQ + n2, half, stride=HY_N2)
                o_ref[rows, :] = g_ref[rows, :] * (y[b * half:(b + 1) * half] + y_ref[rows, :] * bias)
        return carry

    lax.fori_loop(0, HY_N2 // UNROLL, stage3, 0)


def _hyena_mix(z3, filt, hyena_bias, mats, with_ctx):
    c = HYENA_WIDTH
    cw = 128
    nch = c // cw
    pairs = BATCH // 2
    full = lambda a: pl.BlockSpec(a.shape, lambda *i: (0,) * a.ndim)
    kspec = _filter_spectrum(SEQ, filt, mats)
    if with_ctx:
        kspec_c = _filter_spectrum(CTX_LEN, filt, mats)
    bias3 = hyena_bias.reshape(HYENA_ORDER, 1, c)
    scr = pltpu.VMEM((HY_N1 * PITCH, cw), F32)
    y = None
    for o in range(HYENA_ORDER):
        if y is None:
            xin, xspec = z3, pl.BlockSpec((None, 2 * SEQ, cw), lambda b, ch: (0, b, ch))
        else:
            xin, xspec = y, pl.BlockSpec((2 * SEQ, cw), lambda b, ch: (b, ch))
        ynew = pl.pallas_call(
            _hyena_conv_kernel,
            grid=(pairs, nch),
            in_specs=[xspec,
                      pl.BlockSpec((None, 2 * SEQ, cw), lambda b, ch, o=o: (o + 1, b, ch)),
                      pl.BlockSpec((None, 2, HY_N1, HY_N2, cw), lambda b, ch, o=o: (o, 0, 0, 0, ch)),
                      full(mats['hy_lead_f']), full(mats['hy_slab_f2']), full(mats['hy_slab_i2']),
                      full(mats['hy_lead_i']),
                      pl.BlockSpec((None, 1, cw), lambda b, ch, o=o: (o, 0, ch))],
            out_specs=pl.BlockSpec((2 * SEQ, cw), lambda b, ch: (b, ch)),
            out_shape=jax.ShapeDtypeStruct((T_ALL if with_ctx else T_LAT, c), F32),
            scratch_shapes=[scr, scr],
            compiler_params=_cparams(("parallel", "parallel"), 58 * 1024 * 1024),
            name="hyena_conv",
        )(xin, z3, kspec, mats['hy_lead_f'], mats['hy_slab_f2'], mats['hy_slab_i2'], mats['hy_lead_i'], bias3)
        if with_ctx:
            nc = 2 * CTX_LEN
            blk0 = T_LAT // nc
            if y is None:
                cin, cspec_in = z3, pl.BlockSpec((1, nc, c), lambda b: (0, blk0 + b, 0))
            else:
                cin, cspec_in = y, pl.BlockSpec((nc, c), lambda b: (blk0 + b, 0))
            xc = _lead(mats['hc_f'], [cin], [cspec_in], (pairs,),
                       jax.ShapeDtypeStruct((pairs, 2 * nc, c), F32),
                       pl.BlockSpec((1, 2 * nc, c), lambda b: (b, 0, 0)), name="hyena_ctx_fwd")
            ynew = _lead(mats['hc_i'], [xc], [pl.BlockSpec((1, 2 * nc, c), lambda b: (b, 0, 0))],
                         (pairs,), jax.ShapeDtypeStruct((T_ALL, c), F32),
                         pl.BlockSpec((nc, c), lambda b: (blk0 + b, 0)),
                         kspec=(kspec_c, pl.BlockSpec((1, 2 * nc, c), lambda b, o=o: (o, 0, 0))),
                         epi=[(z3, pl.BlockSpec((1, nc, c), lambda b, o=o: (o + 1, blk0 + b, 0))),
                              (cin, cspec_in),
                              (bias3, pl.BlockSpec((1, 1, c), lambda b, o=o: (o, 0, 0)))],
                         alias_to=ynew, name="hyena_ctx_inv")
        y = ynew
    return y


def _route_tile(lt, br, base, tri):
    tm = lt.shape[1]
    aff = jax.nn.sigmoid(lt)
    biased = aff + br
    b = [biased[e:e + 1, :] for e in range(N_EXPERTS)]
    a = [aff[e:e + 1, :] for e in range(N_EXPERTS)]
    epg = EXPERTS_PER_GROUP
    scores = []
    for g in range(N_GROUPS):
        x0, x1, x2, x3 = b[epg * g:epg * g + epg]
        s1, t1 = jnp.maximum(x0, x1), jnp.minimum(x0, x1)
        s2, t2 = jnp.maximum(x2, x3), jnp.minimum(x2, x3)
        scores.append(jnp.maximum(s1, s2) + jnp.maximum(jnp.minimum(s1, s2), jnp.maximum(t1, t2)))
    best = scores[0]
    gsel = jnp.zeros((1, tm), jnp.int32)
    for g in range(1, N_GROUPS):
        gsel = jnp.where(scores[g] > best, g, gsel)
        best = jnp.maximum(best, scores[g])

    def pick(rows, j):
        out = rows[j]
        for g in range(1, N_GROUPS):
            out = jnp.where(gsel == g, rows[epg * g + j], out)
        return out

    v = [pick(b, j) for j in range(epg)]
    av = [pick(a, j) for j in range(epg)]
    i1 = jnp.zeros((1, tm), jnp.int32)
    m1 = v[0]
    for j in range(1, epg):
        i1 = jnp.where(v[j] > m1, j, i1)
        m1 = jnp.maximum(m1, v[j])
    neg = jnp.float32(-3.0e38)
    i2 = jnp.zeros((1, tm), jnp.int32)
    m2 = jnp.full((1, tm), neg, F32)
    for j in range(epg):
        cand = jnp.where(i1 == j, neg, v[j])
        take = cand > m2
        i2 = jnp.where(take, j, i2)
        m2 = jnp.where(take, cand, m2)

    def sel(rows, idx):
        out = rows[0]
        for j in range(1, epg):
            out = jnp.where(idx == j, rows[j], out)
        return out

    a1, a2 = sel(av, i1), sel(av, i2)
    den = a1 + a2
    e1 = gsel * epg + i1
    e2 = gsel * epg + i2
    eio = lax.broadcasted_iota(jnp.int32, (N_EXPERTS, tm), 0)
    oh1 = jnp.where(eio == e1, 1.0, 0.0)
    oh2 = jnp.where(eio == e2, 1.0, 0.0)
    oh = oh1 + oh2
    tot = base + _dot(oh.astype(BF16), tri)
    r1 = jnp.sum(oh1 * tot, axis=0, keepdims=True)
    r2 = jnp.sum(oh2 * tot, axis=0, keepdims=True)
    new_base = base + jnp.sum(oh, axis=1, keepdims=True)
    return (e1, e2), (a1 / den, a2 / den), (r1.astype(jnp.int32), r2.astype(jnp.int32)), new_base


def _merge_kernel(x_ref, ya_ref, yf_ref, yh_ref, gt_ref, wa_ref, wf_ref, wh_ref, wo_ref, g1_ref,
                  gn_ref, sh_ref, sc_ref, wrt_ref, br_ref, xo_ref, h2_ref, e_ref, w_ref, r_ref, cnt_ref):
    d = D_MODEL
    merged = jax.nn.sigmoid(gt_ref[:, 0:d]) * _dot(ya_ref[...], wa_ref[...])
    merged += jax.nn.sigmoid(gt_ref[:, d:2 * d]) * _dot(yf_ref[...].astype(BF16), wf_ref[...])
    merged += jax.nn.sigmoid(gt_ref[:, 2 * d:3 * d]) * _dot(yh_ref[...].astype(BF16), wh_ref[...])
    xn = x_ref[...] + g1_ref[0, 0] * _dot(merged.astype(BF16), wo_ref[...])
    xo_ref[...] = xn
    h2f = _rms_mod(xn, gn_ref[...], sh_ref[0, 0], sc_ref[0, 0])
    h2_ref[...] = h2f
    h2 = h2f.astype(BF16)

    @pl.when(pl.program_id(0) == 0)
    def _():
        cnt_ref[...] = jnp.zeros_like(cnt_ref)

    tm = h2.shape[0]
    lt = lax.dot_general(wrt_ref[...], h2, (((1,), (1,)), ((), ())), preferred_element_type=F32)
    tri = jnp.where(lax.broadcasted_iota(jnp.int32, (tm, tm), 0) < lax.broadcasted_iota(jnp.int32, (tm, tm), 1),
                    1.0, 0.0).astype(BF16)
    es, ws, rs, new_base = _route_tile(lt, br_ref[...], cnt_ref[:, 0:1], tri)
    e_ref[0:1, :], e_ref[1:2, :] = es
    w_ref[0:1, :], w_ref[1:2, :] = ws
    r_ref[0:1, :], r_ref[1:2, :] = rs
    cnt_ref[...] = jnp.broadcast_to(new_base, cnt_ref.shape)


def _merge(x, ya, yf, yh, p, wa, wf, wh, wo, mod4, gain2, wrt, br, n_tok):
    tm = TM
    row = _mod_row(tm)
    full = lambda a: pl.BlockSpec(a.shape, lambda i: (0,) * a.ndim)
    modspec = lambda k: pl.BlockSpec((1, 1, 1, D_MODEL), lambda i: (row(i), k, 0, 0))
    tok = lambda w: pl.BlockSpec((tm, w), lambda i: (i, 0))
    lane = pl.BlockSpec((TOP_K, tm), lambda i: (0, i))
    return pl.pallas_call(
        _merge_kernel,
        grid=(n_tok // tm,),
        in_specs=[tok(D_MODEL), tok(ATTN_WIDTH), tok(FNET_WIDTH), tok(HYENA_WIDTH),
                  pl.BlockSpec((pl.Element(tm), pl.Element(3 * D_MODEL)), lambda i: (i * tm, PG_OFF)),
                  full(wa), full(wf), full(wh), full(wo), modspec(2), full(gain2), modspec(3), modspec(4),
                  full(wrt), full(br)],
        out_specs=[tok(D_MODEL), tok(D_MODEL), lane, lane, lane,
                   pl.BlockSpec((N_EXPERTS, 128), lambda i: (0, 0))],
        out_shape=[jax.ShapeDtypeStruct((n_tok, D_MODEL), F32),
                   jax.ShapeDtypeStruct((n_tok, D_MODEL), F32),
                   jax.ShapeDtypeStruct((TOP_K, n_tok), jnp.int32),
                   jax.ShapeDtypeStruct((TOP_K, n_tok), F32),
                   jax.ShapeDtypeStruct((TOP_K, n_tok), jnp.int32),
                   jax.ShapeDtypeStruct((N_EXPERTS, 128), F32)],
        compiler_params=_cparams(("arbitrary",)),
        name="merge_out_norm_route",
    )(x, ya, yf, yh, p, wa, wf, wh, wo, mod4, gain2, mod4, mod4, wrt, br)


def _row_copy(src_ref, row, dst_ref, r, sem):
    return pltpu.make_async_copy(src_ref.at[pl.ds(row, 1), :], dst_ref.at[pl.ds(r, 1), :], sem)


def _expert_kernel(be_ref, na_ref, rt_cur_ref, rt_next_ref, h2_ref, wg_ref, wu_ref, wd_ref, o_ref,
                   xbuf, sem, wg_s, wu_s, wd_s):
    i = pl.program_id(0)
    n_act = na_ref[0]
    bm = xbuf.shape[1]
    active = i < n_act

    def gather(rt_ref, slot):
        def body(j, carry):
            r0 = pl.multiple_of(j * 8, 8)
            tile = xbuf.at[slot, pl.ds(r0, 8)]
            for u in range(8):
                _row_copy(h2_ref, rt_ref[0, 0, r0 + u], tile, u, sem.at[slot]).start()
            return carry

        lax.fori_loop(0, bm // 8, body, 0)

    @pl.when((i == 0) & active)
    def _():
        gather(rt_cur_ref, 0)

    @pl.when(i + 1 < n_act)
    def _():
        gather(rt_next_ref, (i + 1) % 2)

    prev = be_ref[jnp.maximum(i - 1, 0)]

    @pl.when(active & ((i == 0) | (be_ref[i] != prev)))
    def _():
        wg_s[...] = wg_ref[0].astype(BF16)
        wu_s[...] = wu_ref[0].astype(BF16)
        wd_s[...] = wd_ref[0].astype(BF16)

    @pl.when(active)
    def _():
        slot = i % 2
        pltpu.make_async_copy(h2_ref.at[pl.ds(0, bm), :], xbuf.at[slot], sem.at[slot]).wait()
        x = xbuf[slot].astype(BF16)
        g = _dot(x, wg_s[...])
        u = _dot(x, wu_s[...])
        h = (g * jax.nn.sigmoid(g)) * u
        o_ref[...] = _dot(h.astype(BF16), wd_s[...])

    @pl.when(jnp.logical_not(active))
    def _():
        o_ref[...] = jnp.zeros_like(o_ref)


def _experts(h2, row_tok, blk_expert, n_active, wg, wu, wd):
    bm = EXPERT_BM
    n_blk = row_tok.shape[0] // bm
    rt = row_tok.reshape(n_blk, 1, bm)
    wspec = lambda k, n: pl.BlockSpec((1, k, n), lambda i, be, na: (be[i], 0, 0))
    return pl.pallas_call(
        _expert_kernel,
        grid_spec=pltpu.PrefetchScalarGridSpec(
            num_scalar_prefetch=2,
            grid=(n_blk,),
            in_specs=[pl.BlockSpec((1, 1, bm), lambda i, be, na: (i, 0, 0), memory_space=pltpu.SMEM),
                      pl.BlockSpec((1, 1, bm), lambda i, be, na: (jnp.minimum(i + 1, n_blk - 1), 0, 0),
                                   memory_space=pltpu.SMEM),
                      pl.BlockSpec(memory_space=pl.ANY),
                      wspec(D_MODEL, EXPERT_FF), wspec(D_MODEL, EXPERT_FF), wspec(EXPERT_FF, D_MODEL)],
            out_specs=pl.BlockSpec((bm, D_MODEL), lambda i, be, na: (i, 0)),
            scratch_shapes=[pltpu.VMEM((2, bm, D_MODEL), F32), pltpu.SemaphoreType.DMA((2,)),
                            pltpu.VMEM((D_MODEL, EXPERT_FF), BF16), pltpu.VMEM((D_MODEL, EXPERT_FF), BF16),
                            pltpu.VMEM((EXPERT_FF, D_MODEL), BF16)]),
        out_shape=jax.ShapeDtypeStruct((row_tok.shape[0], D_MODEL), F32),
        compiler_params=_cparams(("arbitrary",)),
        name="moe_experts",
    )(blk_expert, n_active, rt, rt, h2, wg, wu, wd)


def _dispatch(e_idx, rank, counts, n_tok):
    bm = EXPERT_BM
    counts = counts.astype(jnp.int32)
    padded = (counts + bm - 1) // bm * bm
    pad_end = jnp.cumsum(padded)
    pad_start = pad_end - padded
    experts = jnp.arange(N_EXPERTS, dtype=jnp.int32)
    start = jnp.sum(jnp.where(e_idx[..., None] == experts, pad_start, 0), axis=-1)
    dest = start + rank
    n_rows = -(-(n_tok * TOP_K) // bm) * bm + N_EXPERTS * bm
    n_blk = n_rows // bm
    tok = jnp.tile(jnp.arange(n_tok, dtype=jnp.int32), TOP_K)
    row_tok = jnp.zeros((n_rows,), jnp.int32).at[dest.reshape(-1)].set(tok)
    blk_start = jnp.arange(n_blk, dtype=jnp.int32) * bm
    blk_expert = jnp.minimum(jnp.sum((blk_start[:, None] >= pad_end[None, :]).astype(jnp.int32), axis=-1),
                             N_EXPERTS - 1)
    n_active = (pad_end[-1] // bm).astype(jnp.int32).reshape(1)
    return row_tok, dest, blk_expert, n_active


def _combine_kernel(d_cur_ref, d_next_ref, x_ref, w_ref, g2_ref, gf_ref, ys_ref, o_ref, ybuf, sem,
                    *, final, n_tiles):
    i = pl.program_id(0)
    tm = x_ref.shape[0]

    def gather(d_ref, slot):
        def body(j, carry):
            r0 = pl.multiple_of(j * 8, 8)
            for k in range(TOP_K):
                tile = ybuf.at[slot, k, pl.ds(r0, 8)]
                for u in range(8):
                    _row_copy(ys_ref, d_ref[0, k, r0 + u], tile, u, sem.at[slot]).start()
            return carry

        lax.fori_loop(0, tm // 8, body, 0)

    @pl.when(i == 0)
    def _():
        gather(d_cur_ref, 0)

    @pl.when(i + 1 < n_tiles)
    def _():
        gather(d_next_ref, (i + 1) % 2)

    slot = i % 2
    for k in range(TOP_K):
        pltpu.make_async_copy(ys_ref.at[pl.ds(0, tm), :], ybuf.at[slot, k], sem.at[slot]).wait()
    w = w_ref[...]
    moe = ybuf[slot, 0] * w[:, 0:1] + ybuf[slot, 1] * w[:, 1:2]
    xn = x_ref[...] + g2_ref[0, 0] * moe
    if final:
        y = xn * lax.rsqrt(jnp.mean(xn * xn, axis=-1, keepdims=True) + EPS)
        xn = y * gf_ref[...]
    o_ref[...] = xn


def _combine(x, ys, dest, w_sel, mod4, gain_final, n_tok, final):
    tm = 256
    n_tiles = n_tok // tm
    row = _mod_row(tm)
    tok = lambda w: pl.BlockSpec((tm, w), lambda i: (i, 0))
    d3 = jnp.transpose(dest.reshape(TOP_K, n_tiles, tm), (1, 0, 2))
    return pl.pallas_call(
        functools.partial(_combine_kernel, final=final, n_tiles=n_tiles),
        grid=(n_tiles,),
        in_specs=[pl.BlockSpec((1, TOP_K, tm), lambda i: (i, 0, 0), memory_space=pltpu.SMEM),
                  pl.BlockSpec((1, TOP_K, tm), lambda i: (jnp.minimum(i + 1, n_tiles - 1), 0, 0),
                               memory_space=pltpu.SMEM),
                  tok(D_MODEL), tok(TOP_K),
                  pl.BlockSpec((1, 1, 1, D_MODEL), lambda i: (row(i), 5, 0, 0)),
                  pl.BlockSpec((1, D_MODEL), lambda i: (0, 0)),
                  pl.BlockSpec(memory_space=pl.ANY)],
        out_specs=tok(D_MODEL),
        out_shape=jax.ShapeDtypeStruct((n_tok if final else T_ALL, D_MODEL), F32),
        scratch_shapes=[pltpu.VMEM((2, TOP_K, tm, D_MODEL), F32), pltpu.SemaphoreType.DMA((2,))],
        compiler_params=_cparams(("arbitrary",)),
        name="moe_combine",
    )(d3, d3, x, w_sel, mod4, gain_final, ys)


def _moe_kernel(be_ref, na_ref, ip_ref, ic_ref, sc_ref, sn_ref, h2_ref, wg_ref, wu_ref, wd_ref, y_ref,
                xbuf, obuf, gsem, ssem, wg_s, wu_s, wd_s):
    i = pl.program_id(0)
    n_act = na_ref[0]
    bm = xbuf.shape[1]
    active = i < n_act
    slot = i % 2
    other = (i + 1) % 2
    n_chunk = 4
    cw = EXPERT_FF // n_chunk
    rows_per = bm // n_chunk

    def gather(src_ref, dst_slot, lo, hi):
        for r in range(lo, hi):
            _row_copy(h2_ref, src_ref[0, 0, r], xbuf.at[dst_slot], r, gsem.at[dst_slot]).start()

    def scatter(info_ref, src_slot, lo, hi):
        for r in range(lo, hi):
            pltpu.make_async_copy(obuf.at[src_slot, pl.ds(r, 1), :], y_ref.at[pl.ds(info_ref[0, 0, r], 1), :],
                                  ssem.at[src_slot]).start()

    def wait_gather(s):
        pltpu.make_async_copy(h2_ref.at[pl.ds(0, bm), :], xbuf.at[s], gsem.at[s]).wait()

    def wait_scatter(s):
        pltpu.make_async_copy(obuf.at[s], y_ref.at[pl.ds(0, bm), :], ssem.at[s]).wait()

    @pl.when((i == 0) & active)
    def _():
        obuf[...] = jnp.zeros_like(obuf)
        gather(sc_ref, 0, 0, bm)

    prev = be_ref[jnp.maximum(i - 1, 0)]

    @pl.when(active & ((i == 0) | (be_ref[i] != prev)))
    def _():
        wg_s[...] = wg_ref[0].astype(BF16)
        wu_s[...] = wu_ref[0].astype(BF16)
        wd_s[...] = wd_ref[0].astype(BF16)

    @pl.when(active)
    def _():
        wait_gather(slot)
        x = xbuf[slot].astype(BF16)
        hs = []
        for c in range(n_chunk):
            g = _dot(x, wg_s[:, c * cw:(c + 1) * cw])
            u = _dot(x, wu_s[:, c * cw:(c + 1) * cw])
            hs.append(((g * jax.nn.sigmoid(g)) * u).astype(BF16))
            gather(sn_ref, other, c * rows_per, (c + 1) * rows_per)
        h = jnp.concatenate(hs, axis=1)
        for c in range(n_chunk):
            obuf[slot, :, c * cw:(c + 1) * cw] = _dot(h, wd_s[:, c * cw:(c + 1) * cw])
            scatter(ip_ref, other, c * rows_per, (c + 1) * rows_per)
        wait_scatter(other)

    @pl.when(i == n_act - 1)
    def _():
        wait_gather(other)
        scatter(ic_ref, slot, 0, bm)
        wait_scatter(slot)


def _moe(h2, info, blk_expert, n_active, wg, wu, wd, n_tok):
    bm = EXPERT_BM
    n_rows = info.shape[0]
    n_blk = n_rows // bm
    info3 = info.reshape(n_blk, 1, bm)
    src3 = jnp.where(info3 >= TOP_K * n_tok, 0, info3 % n_tok)
    wspec = lambda k, n: pl.BlockSpec((1, k, n), lambda i, be, na: (be[i], 0, 0))
    ispec = lambda f: pl.BlockSpec((1, 1, bm), lambda i, be, na: (f(i), 0, 0), memory_space=pltpu.SMEM)
    return pl.pallas_call(
        _moe_kernel,
        grid_spec=pltpu.PrefetchScalarGridSpec(
            num_scalar_prefetch=2,
            grid=(n_blk,),
            in_specs=[ispec(lambda i: jnp.maximum(i - 1, 0)), ispec(lambda i: i),
                      ispec(lambda i: i), ispec(lambda i: jnp.minimum(i + 1, n_blk - 1)),
                      pl.BlockSpec(memory_space=pl.ANY),
                      wspec(D_MODEL, EXPERT_FF), wspec(D_MODEL, EXPERT_FF), wspec(EXPERT_FF, D_MODEL)],
            out_specs=pl.BlockSpec(memory_space=pl.ANY),
            scratch_shapes=[pltpu.VMEM((2, bm, D_MODEL), F32), pltpu.VMEM((2, bm, D_MODEL), F32),
                            pltpu.SemaphoreType.DMA((2,)), pltpu.SemaphoreType.DMA((2,)),
                            pltpu.VMEM((D_MODEL, EXPERT_FF), BF16), pltpu.VMEM((D_MODEL, EXPERT_FF), BF16),
                            pltpu.VMEM((EXPERT_FF, D_MODEL), BF16)]),
        out_shape=jax.ShapeDtypeStruct((TOP_K * n_tok + n_rows, D_MODEL), F32),
        compiler_params=_cparams(("arbitrary",)),
        name="moe_experts",
    )(blk_expert, n_active, info3, info3, src3, src3, h2, wg, wu, wd)


def _dispatch_info(e_idx, rank, counts, n_tok):
    bm = EXPERT_BM
    counts = counts.astype(jnp.int32)
    padded = (counts + bm - 1) // bm * bm
    pad_end = jnp.cumsum(padded)
    pad_start = pad_end - padded
    experts = jnp.arange(N_EXPERTS, dtype=jnp.int32)
    start = jnp.sum(jnp.where(e_idx[..., None] == experts, pad_start, 0), axis=-1)
    dest = start + rank
    n_rows = -(-(n_tok * TOP_K) // bm) * bm + N_EXPERTS * bm
    n_blk = n_rows // bm
    spill = TOP_K * n_tok + jnp.arange(n_rows, dtype=jnp.int32)
    info = spill.at[dest.reshape(-1)].set(jnp.arange(TOP_K * n_tok, dtype=jnp.int32))
    blk_start = jnp.arange(n_blk, dtype=jnp.int32) * bm
    blk_expert = jnp.minimum(jnp.sum((blk_start[:, None] >= pad_end[None, :]).astype(jnp.int32), axis=-1),
                             N_EXPERTS - 1)
    n_active = (pad_end[-1] // bm).astype(jnp.int32).reshape(1)
    return info, blk_expert, n_active


def _residual_kernel(x_ref, y0_ref, y1_ref, w_ref, g2_ref, gf_ref, o_ref, *, final):
    w = w_ref[...]
    moe = y0_ref[...] * w[:, 0:1] + y1_ref[...] * w[:, 1:2]
    xn = x_ref[...] + g2_ref[0, 0] * moe
    if final:
        y = xn * lax.rsqrt(jnp.mean(xn * xn, axis=-1, keepdims=True) + EPS)
        xn = y * gf_ref[...]
    o_ref[...] = xn


def _residual(x, y, w_sel, mod4, gain_final, n_tok, final):
    tm = TM
    n_tiles = n_tok // tm
    row = _mod_row(tm)
    tok = lambda w: pl.BlockSpec((tm, w), lambda i: (i, 0))
    return pl.pallas_call(
        functools.partial(_residual_kernel, final=final),
        grid=(n_tiles,),
        in_specs=[tok(D_MODEL), tok(D_MODEL), pl.BlockSpec((tm, D_MODEL), lambda i: (n_tiles + i, 0)), tok(TOP_K),
                  pl.BlockSpec((1, 1, 1, D_MODEL), lambda i: (row(i), 5, 0, 0)),
                  pl.BlockSpec((1, D_MODEL), lambda i: (0, 0))],
        out_specs=tok(D_MODEL),
        out_shape=jax.ShapeDtypeStruct((n_tok if final else T_ALL, D_MODEL), F32),
        compiler_params=_cparams(("parallel",)),
        name="moe_residual",
    )(x, y, y, w_sel, mod4, gain_final)


def kernel(x, c, ctx, c_ctx, w_mod, b_mod, norm_mix, norm_ffn, w_in, attn_sink, conv_w, conv_b, filt_w1, filt_b1, filt_freq, filt_w2, filt_b2, filt_w3, hyena_bias, w_branch_attn, w_branch_fnet, w_branch_hyena, w_out, w_router, b_router, w_exp_gate, w_exp_up, w_exp_down, norm_final):
    mats = _dft_mats()
    cos_t, sin_t = _rope_tables()
    c8 = jnp.concatenate([c, c_ctx[None, :], jnp.zeros((8 - BATCH - 1, D_MODEL), F32)], axis=0)
    mod_all = _modulation(c8, w_mod, b_mod)
    xa = jnp.concatenate([x.reshape(T_LAT, D_MODEL), ctx.reshape(T_CTX, D_MODEL)], axis=0)
    wrt = w_router.T.astype(BF16)
    br = b_router.astype(F32).reshape(N_EXPERTS, 1)
    gain_final = norm_final.reshape(1, D_MODEL)
    out = None
    for l in range(DEPTH):
        last = l == DEPTH - 1
        with_ctx = not last
        n_tok = T_LAT if last else T_ALL
        mod4 = mod_all[l].reshape(8, N_MOD, 1, D_MODEL)
        p = _norm_proj(xa, norm_mix[l].reshape(1, D_MODEL), mod4, w_in[l], T_ALL)
        ya = _attention(p, attn_sink[l], cos_t, sin_t, with_ctx)
        yf = _fourier_mix(p, mats, with_ctx)
        z3 = _short_conv(p, conv_w[l], conv_b[l], with_ctx)
        filt = (filt_w1[l], filt_b1[l], filt_freq[l], filt_w2[l], filt_b2[l], filt_w3[l])
        yh = _hyena_mix(z3, filt, hyena_bias[l], mats, with_ctx)
        xa, h2, e_idx, w_sel, rank, cnt = _merge(
            xa, ya, yf, yh, p, w_branch_attn[l].astype(BF16), w_branch_fnet[l].astype(BF16),
            w_branch_hyena[l].astype(BF16), w_out[l].astype(BF16), mod4,
            norm_ffn[l].reshape(1, D_MODEL), wrt, br, n_tok)
        info, blk_expert, n_active = _dispatch_info(e_idx, rank, cnt[:, 0], n_tok)
        y = _moe(h2, info, blk_expert, n_active, w_exp_gate[l], w_exp_up[l], w_exp_down[l], n_tok)
        res = _residual(xa, y, w_sel.T, mod4, gain_final, n_tok, last)
        if last:
            out = res
        else:
            xa = res
    return out.reshape(BATCH, SEQ, D_MODEL)
```

```python
import functools
import math

import jax
import jax.numpy as jnp
from jax import lax
from jax.experimental import pallas as pl
from jax.experimental.pallas import tpu as pltpu

F32 = jnp.float32
BF16 = jnp.bfloat16

D_MODEL = 1024
BATCH = 4
SEQ = 4096
DEPTH = 4
GRID_W = 64
CTX_LEN = 256
EPS = 1e-6
N_MOD = 6

HEAD_DIM = 64
N_Q_HEADS = 8
N_KV_HEADS = 2
Q_PER_KV = N_Q_HEADS // N_KV_HEADS
ATTN_BLOCK = 128
ROPE_BASE = 10000.0

FNET_GROUPS = 4
FNET_GROUP_DIM = 128
FNET_WIDTH = FNET_GROUPS * FNET_GROUP_DIM

HYENA_WIDTH = 512
HYENA_ORDER = 2
FILTER_EMB = 33
FILTER_BANDS = (FILTER_EMB - 1) // 2
FILTER_HIDDEN = 64
DECAY_TARGET = 1e-2
FAST_DECAY_PCT = 0.3
SLOW_DECAY_PCT = 1.5

ATTN_WIDTH = N_Q_HEADS * HEAD_DIM
KV_WIDTH = N_KV_HEADS * HEAD_DIM
Q_OFF = 0
K_OFF = Q_OFF + ATTN_WIDTH
V_OFF = K_OFF + KV_WIDTH
F_OFF = V_OFF + KV_WIDTH
H_OFF = F_OFF + FNET_WIDTH
G_OFF = H_OFF + (HYENA_ORDER + 1) * HYENA_WIDTH
IN_WIDTH = G_OFF + 3 * D_MODEL

N_EXPERTS = 16
N_GROUPS = 4
EXPERTS_PER_GROUP = N_EXPERTS // N_GROUPS
TOP_K = 2
EXPERT_FF = 1024

T_LAT = BATCH * SEQ
T_CTX = BATCH * CTX_LEN
T_ALL = T_LAT + T_CTX

PG_OFF = G_OFF
PH_OFF = H_OFF
PF_OFF = F_OFF
PQ_OFF = Q_OFF
PK_OFF = K_OFF
PV_OFF = V_OFF

HY_N = 2 * SEQ
HY_N2 = 64
HY_N1 = HY_N // HY_N2
FN_N = 64

PITCH = HY_N2 + 8
UNROLL = 8
TM = 512
EXPERT_BM = 512
VMEM_LIMIT = 52 * 1024 * 1024


def _cparams(sem, vmem=VMEM_LIMIT):
    return pltpu.CompilerParams(dimension_semantics=sem, vmem_limit_bytes=vmem)


def _dot(a, b):
    return jnp.dot(a, b, preferred_element_type=F32)


def _cis(expo, n):
    ang = (2.0 * math.pi / n) * jnp.mod(expo, n).astype(F32)
    return jnp.cos(ang), jnp.sin(ang)


def _real_form(gr, gi):
    return jnp.concatenate([jnp.concatenate([gr, -gi], axis=-1), jnp.concatenate([gi, gr], axis=-1)], axis=-2)


def _dft_mats():
    ar = lambda n: jnp.arange(n, dtype=jnp.int32)
    m = {}
    c, s = _cis(ar(HY_N1)[:, None] * ar(HY_N1 // 2)[None, :], HY_N1)
    m['hy_lead_f'] = _real_form(c, -s).astype(BF16)
    c, s = _cis(ar(HY_N1 // 2)[:, None] * ar(HY_N1)[None, :], HY_N1)
    m['hy_lead_i'] = _real_form(c, s).astype(BF16)
    c, s = _cis(ar(HY_N1)[:, None] * ar(HY_N1 // 2)[None, :], HY_N1)
    m['hy_lead_kh'] = jnp.concatenate([c, -s], axis=0).astype(BF16)
    a = ar(HY_N1)[:, None, None]
    k2 = ar(HY_N2)[None, :, None]
    n2 = ar(HY_N2)[None, None, :]
    c, s = _cis(n2 * (a + HY_N1 * k2), HY_N)
    m['hy_slab_f2'] = jnp.concatenate([c, -s], axis=-1).astype(BF16)
    ct = jnp.swapaxes(c, 1, 2) * (1.0 / HY_N)
    st = jnp.swapaxes(s, 1, 2) * (1.0 / HY_N)
    m['hy_slab_i2'] = jnp.concatenate([ct, st], axis=-1).astype(BF16)
    nc = 2 * CTX_LEN
    c, s = _cis(ar(nc)[:, None] * ar(CTX_LEN)[None, :], nc)
    m['hc_f'] = _real_form(c, -s).astype(BF16)
    c, s = _cis(ar(CTX_LEN)[:, None] * ar(nc)[None, :], nc)
    m['hc_i'] = _real_form(c * (1.0 / nc), s * (1.0 / nc)).astype(BF16)
    c, s = _cis(ar(nc)[:, None] * ar(CTX_LEN)[None, :], nc)
    m['hc_kh'] = jnp.concatenate([c, -s], axis=0).astype(BF16)
    c, s = _cis(ar(FNET_GROUP_DIM)[:, None] * ar(FNET_GROUP_DIM)[None, :], FNET_GROUP_DIM)
    m['fn_chan'] = jnp.concatenate([c, -s], axis=1).astype(BF16)
    c, s = _cis(ar(FN_N)[:, None] * ar(FN_N)[None, :], FN_N)
    m['fn_lead'] = _real_form(c, -s).astype(BF16)
    a = ar(FN_N)[:, None, None]
    k1 = ar(FN_N)[None, :, None]
    n1 = ar(FN_N)[None, None, :]
    scale = 1.0 / math.sqrt(SEQ * FNET_GROUP_DIM)
    c, s = _cis(n1 * (a + FN_N * k1), SEQ)
    m['fn_slab'] = jnp.concatenate([c * scale, s * scale], axis=-1).astype(BF16)
    scale = 1.0 / math.sqrt(CTX_LEN * FNET_GROUP_DIM)
    c, s = _cis(ar(CTX_LEN)[:, None] * ar(CTX_LEN)[None, :], CTX_LEN)
    m['fc'] = jnp.concatenate([c * scale, s * scale], axis=-1).astype(BF16)
    return m


def _mod_kernel(c_ref, w_ref, b_ref, o_ref):
    c = c_ref[...]
    s = c * jax.nn.sigmoid(c)
    o_ref[0] = _dot(s.astype(BF16), w_ref[0].astype(BF16)) + b_ref[0]


def _modulation(c8, w_mod, b_mod):
    tn = 1536
    n = N_MOD * D_MODEL
    return pl.pallas_call(
        _mod_kernel,
        grid=(DEPTH, n // tn),
        in_specs=[pl.BlockSpec((8, D_MODEL), lambda l, j: (0, 0)),
                  pl.BlockSpec((1, D_MODEL, tn), lambda l, j: (l, 0, j)),
                  pl.BlockSpec((1, 1, tn), lambda l, j: (l, 0, j))],
        out_specs=pl.BlockSpec((1, 8, tn), lambda l, j: (l, 0, j)),
        out_shape=jax.ShapeDtypeStruct((DEPTH, 8, n), F32),
        compiler_params=_cparams(("parallel", "parallel")),
        name="adaln_modulation",
    )(c8, w_mod, b_mod.reshape(DEPTH, 1, n))


def _mod_row(tm):
    tiles_per_batch = SEQ // tm
    return lambda i: jnp.minimum(i // tiles_per_batch, BATCH)


def _rms_mod(x, g, sh, sc):
    y = x * lax.rsqrt(jnp.mean(x * x, axis=-1, keepdims=True) + EPS)
    return (y * g) * (1.0 + sc) + sh


def _norm_proj_kernel(x_ref, g_ref, sh_ref, sc_ref, w_ref, o_ref, w_s):
    @pl.when(pl.program_id(1) == 0)
    def _():
        w_s[...] = w_ref[...].astype(BF16)

    h = _rms_mod(x_ref[...], g_ref[...], sh_ref[0, 0], sc_ref[0, 0]).astype(BF16)
    o_ref[...] = _dot(h, w_s[...])


def _norm_proj(x, gain, mod4, w, layer, n_tok):
    tm = TM
    n_out = w.shape[2]
    tn = n_out // 2
    row = _mod_row(tm)
    return pl.pallas_call(
        _norm_proj_kernel,
        grid=(n_out // tn, n_tok // tm),
        in_specs=[pl.BlockSpec((tm, D_MODEL), lambda j, i: (i, 0)),
                  pl.BlockSpec((1, D_MODEL), lambda j, i: (0, 0)),
                  pl.BlockSpec((1, 1, 1, D_MODEL), lambda j, i: (row(i), 0, 0, 0)),
                  pl.BlockSpec((1, 1, 1, D_MODEL), lambda j, i: (row(i), 1, 0, 0)),
                  pl.BlockSpec((None, D_MODEL, tn), lambda j, i: (layer, 0, j))],
        out_specs=pl.BlockSpec((tm, tn), lambda j, i: (i, j)),
        out_shape=jax.ShapeDtypeStruct((T_ALL, n_out), F32),
        scratch_shapes=[pltpu.VMEM((D_MODEL, tn), BF16)],
        compiler_params=_cparams(("arbitrary", "arbitrary")),
        name="norm_in_proj",
    )(x, gain, mod4, mod4, w)


def _softmax_pv(qh, k_parts, v_parts, masks, sink):
    nt = (((1,), (1,)), ((), ()))
    scores = []
    for kp, mk in zip(k_parts, masks):
        s = lax.dot_general(qh, kp, nt, preferred_element_type=F32)
        if mk is not None:
            s = jnp.where(mk, s, -1e30)
        scores.append(s)
    m = sink
    for s in scores:
        m = jnp.maximum(m, jnp.max(s, axis=-1, keepdims=True))
    es = [jnp.exp(s - m) for s in scores]
    den = jnp.exp(sink - m)
    for e in es:
        den = den + jnp.sum(e, axis=-1, keepdims=True)
    inv = 1.0 / den
    o = None
    for e, vp in zip(es, v_parts):
        t = _dot((e * inv).astype(BF16), vp)
        o = t if o is None else o + t
    return o


def _attn_kernel(sink_ref, q_ref, km_ref, k0_ref, kp_ref, vm_ref, v0_ref, vp_ref, kc_ref, vc_ref,
                 cos_ref, sin_ref, o_ref, *, nb):
    n = pl.program_id(1)
    blk = ATTN_BLOCK
    lane = lax.broadcasted_iota(jnp.int32, (blk, 128), 1)
    first = (lane % 32) < 16

    def rope(x, blk_idx):
        r0 = pl.multiple_of(blk_idx * blk, blk)
        c = cos_ref[pl.ds(r0, blk), :]
        s = sin_ref[pl.ds(r0, blk), :]
        sw = jnp.where(first, pltpu.roll(x, 112, 1), pltpu.roll(x, 16, 1))
        return x * c + sw * s

    nm = jnp.maximum(n - 1, 0)
    npl = jnp.minimum(n + 1, nb - 1)
    kall = jnp.concatenate([rope(km_ref[...], nm), rope(k0_ref[...], n), rope(kp_ref[...], npl), kc_ref[...]],
                           axis=0)
    vall = jnp.concatenate([vm_ref[...], v0_ref[...], vp_ref[...], vc_ref[...]], axis=0)
    nk = kall.shape[0]
    kswap = pltpu.roll(kall, HEAD_DIM, 1)
    vswap = pltpu.roll(vall, HEAD_DIM, 1)
    lo = lax.broadcasted_iota(jnp.int32, (nk, 128), 1) < HEAD_DIM

    r = lax.broadcasted_iota(jnp.int32, (2 * blk, blk), 0) % blk
    cidx = lax.broadcasted_iota(jnp.int32, (2 * blk, blk), 1)
    ok_prev = jnp.where(cidx >= r, (n > 0).astype(jnp.int32), 0) > 0
    ok_next = jnp.where(cidx <= r, (n < nb - 1).astype(jnp.int32), 0) > 0
    top_rows = lax.broadcasted_iota(jnp.int32, (2 * blk, 1), 0) < blk
    neg = jnp.float32(-1e30)

    scale = HEAD_DIM ** -0.5
    q2 = [(rope(q_ref[:, p * 128:(p + 1) * 128], n) * scale).astype(BF16) for p in range(N_Q_HEADS // 2)]
    nt = (((1,), (1,)), ((), ()))
    for h in range(N_KV_HEADS):
        ka, kb = (kall, kswap) if h == 0 else (kswap, kall)
        va, vb = (vall, vswap) if h == 0 else (vswap, vall)
        kbd = jnp.concatenate([jnp.where(lo, ka, 0.0), jnp.where(lo, 0.0, kb)], axis=0).astype(BF16)
        vbd = jnp.concatenate([jnp.where(lo, va, 0.0), jnp.where(lo, 0.0, vb)], axis=0).astype(BF16)
        q4 = jnp.concatenate([q2[2 * h], q2[2 * h + 1]], axis=0)
        s = lax.dot_general(q4, kbd, nt, preferred_element_type=F32)
        probs = []
        for c in range(2):
            base = c * nk
            sink = jnp.where(top_rows, sink_ref[Q_PER_KV * h + c], sink_ref[Q_PER_KV * h + 2 + c])
            parts = [jnp.where(ok_prev, s[:, base:base + blk], neg),
                     s[:, base + blk:base + 2 * blk],
                     jnp.where(ok_next, s[:, base + 2 * blk:base + 3 * blk], neg),
                     s[:, base + 3 * blk:base + nk]]
            m = sink
            for part in parts:
                m = jnp.maximum(m, jnp.max(part, axis=-1, keepdims=True))
            es = [jnp.exp(part - m) for part in parts]
            den = jnp.exp(sink - m)
            for e in es:
                den = den + jnp.sum(e, axis=-1, keepdims=True)
            inv = 1.0 / den
            probs += [(e * inv).astype(BF16) for e in es]
        o = _dot(jnp.concatenate(probs, axis=1), vbd)
        w0 = h * Q_PER_KV * HEAD_DIM
        o_ref[:, w0:w0 + 128] = o[:blk].astype(o_ref.dtype)
        o_ref[:, w0 + 128:w0 + 256] = o[blk:].astype(o_ref.dtype)


def _ctx_attn_kernel(sink_ref, q_ref, kc_ref, vc_ref, o_ref):
    kc = kc_ref[...].astype(BF16)
    vc = vc_ref[...].astype(BF16)
    scale = HEAD_DIM ** -0.5
    outs = []
    for pair in range(N_Q_HEADS // 2):
        q2 = (q_ref[:, pair * 128:(pair + 1) * 128] * scale).astype(BF16)
        for sub in range(2):
            head = 2 * pair + sub
            kvh = head // Q_PER_KV
            sl = slice(kvh * HEAD_DIM, (kvh + 1) * HEAD_DIM)
            qh = q2[:, sub * HEAD_DIM:(sub + 1) * HEAD_DIM]
            outs.append(_softmax_pv(qh, [kc[:, sl]], [vc[:, sl]], [None], sink_ref[head]))
    o_ref[...] = jnp.concatenate(outs, axis=-1).astype(o_ref.dtype)


def _attention(p, sink, cos_t, sin_t, with_ctx):
    blk = ATTN_BLOCK
    nb = SEQ // blk
    qc, kcol, vcol = PQ_OFF // ATTN_WIDTH, PK_OFF // KV_WIDTH, PV_OFF // KV_WIDTH
    ctx_blk = T_LAT // CTX_LEN
    smem = pl.BlockSpec(memory_space=pltpu.SMEM)

    def kv_spec(col, d):
        return pl.BlockSpec((blk, KV_WIDTH),
                            lambda b, n: (b * nb + jnp.clip(n + d, 0, nb - 1), col))

    ya = pl.pallas_call(
        functools.partial(_attn_kernel, nb=nb),
        grid=(BATCH, nb),
        in_specs=[smem,
                  pl.BlockSpec((blk, ATTN_WIDTH), lambda b, n: (b * nb + n, qc)),
                  kv_spec(kcol, -1), kv_spec(kcol, 0), kv_spec(kcol, 1),
                  kv_spec(vcol, -1), kv_spec(vcol, 0), kv_spec(vcol, 1),
                  pl.BlockSpec((CTX_LEN, KV_WIDTH), lambda b, n: (ctx_blk + b, kcol)),
                  pl.BlockSpec((CTX_LEN, KV_WIDTH), lambda b, n: (ctx_blk + b, vcol)),
                  pl.BlockSpec((SEQ, KV_WIDTH), lambda b, n: (0, 0)),
                  pl.BlockSpec((SEQ, KV_WIDTH), lambda b, n: (0, 0))],
        out_specs=pl.BlockSpec((blk, ATTN_WIDTH), lambda b, n: (b * nb + n, 0)),
        out_shape=jax.ShapeDtypeStruct((T_ALL, ATTN_WIDTH), BF16),
        compiler_params=_cparams(("parallel", "parallel")),
        name="banded_attention",
    )(sink, p, p, p, p, p, p, p, p, p, cos_t, sin_t)
    if not with_ctx:
        return ya
    cb = CTX_LEN // blk
    lat_blk = T_LAT // blk

    def alias_kernel(sink_ref, q_ref, kc_ref, vc_ref, ya_in_ref, o_ref):
        del ya_in_ref
        _ctx_attn_kernel(sink_ref, q_ref, kc_ref, vc_ref, o_ref)

    return pl.pallas_call(
        alias_kernel,
        grid=(BATCH, cb),
        in_specs=[smem,
                  pl.BlockSpec((blk, ATTN_WIDTH), lambda b, n: (lat_blk + b * cb + n, qc)),
                  pl.BlockSpec((CTX_LEN, KV_WIDTH), lambda b, n: (ctx_blk + b, kcol)),
                  pl.BlockSpec((CTX_LEN, KV_WIDTH), lambda b, n: (ctx_blk + b, vcol)),
                  pl.BlockSpec(memory_space=pl.ANY)],
        out_specs=pl.BlockSpec((blk, ATTN_WIDTH), lambda b, n: (lat_blk + b * cb + n, 0)),
        out_shape=jax.ShapeDtypeStruct((T_ALL, ATTN_WIDTH), BF16),
        input_output_aliases={4: 0},
        compiler_params=_cparams(("parallel", "parallel")),
        name="context_attention",
    )(sink, p, p, p, ya)


def _rope_tables():
    n_freq = HEAD_DIM // 4
    freqs = ROPE_BASE ** (-jnp.arange(n_freq, dtype=F32) / n_freq)
    t = jnp.arange(SEQ, dtype=jnp.int32)
    rows = (t // GRID_W).astype(F32)[:, None] * freqs
    cols = (t % GRID_W).astype(F32)[:, None] * freqs
    cos_h = jnp.concatenate([jnp.cos(rows), jnp.cos(rows), jnp.cos(cols), jnp.cos(cols)], axis=-1)
    sin_h = jnp.concatenate([-jnp.sin(rows), jnp.sin(rows), -jnp.sin(cols), jnp.sin(cols)], axis=-1)
    return jnp.tile(cos_h, (1, 2)), jnp.tile(sin_h, (1, 2))


def _lead_kernel(*refs, n_in, cmul, epi):
    m_ref = refs[0]
    x_refs = refs[1:1 + n_in]
    pos = 1 + n_in
    xs = []
    for r in x_refs:
        v = r[...]
        xs.append(v.reshape(-1, v.shape[-1]))
    x = xs[0] if n_in == 1 else jnp.concatenate(xs, axis=0)
    if cmul:
        k = refs[pos][...]
        pos += 1
        k = k.reshape(-1, k.shape[-1])
        half = x.shape[0] // 2
        xr, xi, kr, ki = x[:half], x[half:], k[:half], k[half:]
        x = jnp.concatenate([xr * kr - xi * ki, xr * ki + xi * kr], axis=0)
    res = _dot(m_ref[...], x.astype(BF16))
    if epi:
        g_ref, y_ref, b_ref = refs[pos:pos + 3]
        pos += 3
        g = g_ref[...]
        y = y_ref[...]
        res = g.reshape(-1, g.shape[-1]) * (res + y.reshape(-1, y.shape[-1]) * b_ref[...])
    o_ref = refs[pos]
    o_ref[...] = res.reshape(o_ref.shape).astype(o_ref.dtype)


def _lead(mat, xs, x_specs, grid, out_shape, out_spec, *, kspec=None, epi=None, alias_to=None, name):
    ins = [mat] + list(xs)
    specs = [pl.BlockSpec(mat.shape, lambda *a: (0, 0))] + list(x_specs)
    if kspec is not None:
        ins.append(kspec[0])
        specs.append(kspec[1])
    if epi is not None:
        for arr, sp in epi:
            ins.append(arr)
            specs.append(sp)
    kern = functools.partial(_lead_kernel, n_in=len(xs), cmul=kspec is not None, epi=epi is not None)
    aliases = {}
    if alias_to is not None:
        aliases = {len(ins): 0}
        ins.append(alias_to)
        specs.append(pl.BlockSpec(memory_space=pl.ANY))
        inner = kern

        def kern(*refs):
            inner(*refs[:-2], refs[-1])

    return pl.pallas_call(
        kern, grid=grid, in_specs=specs, out_specs=out_spec, out_shape=out_shape,
        input_output_aliases=aliases,
        compiler_params=_cparams(("parallel",) * len(grid)), name=name,
    )(*ins)


def _cstack(xr, xi):
    return jnp.concatenate([jnp.concatenate([xr, xi], axis=1), jnp.concatenate([-xi, xr], axis=1)], axis=0)


def _fnet_kernel(u_ref, mc_ref, ml_ref, ms_ref, o_ref, zr, zi, are, aim):
    n = FN_N
    pitch = PITCH
    gd = FNET_GROUP_DIM
    rows = 4 * n
    mc = mc_ref[...]

    def chan(i, carry):
        r_in = pl.multiple_of(i * rows, rows)
        z = _dot(u_ref[pl.ds(r_in, rows), :].astype(BF16), mc)
        for q in range(rows // n):
            r_out = pl.multiple_of((i * (rows // n) + q) * pitch, 8)
            zr[pl.ds(r_out, n), :] = z[q * n:(q + 1) * n, :gd]
            zi[pl.ds(r_out, n), :] = z[q * n:(q + 1) * n, gd:]
        return carry

    lax.fori_loop(0, SEQ // rows, chan, 0)
    ml = ml_ref[...]

    def lead(i, carry):
        for u in range(UNROLL):
            n1 = i * UNROLL + u
            x = jnp.concatenate([zr[pl.ds(n1, n, stride=pitch), :], zi[pl.ds(n1, n, stride=pitch), :]],
                                axis=0).astype(BF16)
            r = _dot(ml, x)
            are[pl.ds(n1, n, stride=pitch), :] = r[:n]
            aim[pl.ds(n1, n, stride=pitch), :] = r[n:]
        return carry

    lax.fori_loop(0, n // UNROLL, lead, 0)

    def slab(i, carry):
        for u in range(UNROLL):
            k2 = i * UNROLL + u
            r0 = pl.multiple_of(k2 * pitch, 8)
            x = jnp.concatenate([are[pl.ds(r0, n), :], aim[pl.ds(r0, n), :]], axis=0).astype(BF16)
            o_ref[pl.ds(k2, n, stride=n), :] = _dot(ms_ref[k2], x)
        return carry

    lax.fori_loop(0, n // UNROLL, slab, 0)


def _fnet_ctx_kernel(u_ref, mc_ref, mf_ref, yf_in_ref, o_ref):
    del yf_in_ref
    gd = FNET_GROUP_DIM
    mc = mc_ref[...]
    mf = mf_ref[...]
    for g in range(FNET_GROUPS):
        z = _dot(u_ref[:, g * gd:(g + 1) * gd].astype(BF16), mc)
        x = jnp.concatenate([z[:, :gd], z[:, gd:]], axis=0).astype(BF16)
        o_ref[:, g * gd:(g + 1) * gd] = _dot(mf, x)


def _fourier_mix(p, mats, with_ctx):
    gd = FNET_GROUP_DIM
    col0 = PF_OFF // gd
    full = lambda a: pl.BlockSpec(a.shape, lambda *i: (0,) * a.ndim)
    scr = pltpu.VMEM((FN_N * PITCH, gd), F32)
    yf = pl.pallas_call(
        _fnet_kernel,
        grid=(BATCH, FNET_GROUPS),
        in_specs=[pl.BlockSpec((SEQ, gd), lambda b, g: (b, col0 + g)),
                  full(mats['fn_chan']), full(mats['fn_lead']), full(mats['fn_slab'])],
        out_specs=pl.BlockSpec((SEQ, gd), lambda b, g: (b, g)),
        out_shape=jax.ShapeDtypeStruct((T_ALL if with_ctx else T_LAT, FNET_WIDTH), F32),
        scratch_shapes=[scr, scr, scr, scr],
        compiler_params=_cparams(("parallel", "parallel")),
        name="fnet_latent",
    )(p, mats['fn_chan'], mats['fn_lead'], mats['fn_slab'])
    if not with_ctx:
        return yf
    blk0 = T_LAT // CTX_LEN
    return pl.pallas_call(
        _fnet_ctx_kernel,
        grid=(BATCH,),
        in_specs=[pl.BlockSpec((pl.Element(CTX_LEN), pl.Element(FNET_WIDTH)),
                               lambda b: ((blk0 + b) * CTX_LEN, PF_OFF)),
                  full(mats['fn_chan']), full(mats['fc']), pl.BlockSpec(memory_space=pl.ANY)],
        out_specs=pl.BlockSpec((CTX_LEN, FNET_WIDTH), lambda b: (blk0 + b, 0)),
        out_shape=jax.ShapeDtypeStruct((T_ALL, FNET_WIDTH), F32),
        input_output_aliases={3: 0},
        compiler_params=_cparams(("parallel",)),
        name="fnet_ctx",
    )(p, mats['fn_chan'], mats['fc'], yf)


def _short_conv_kernel(u_ref, w_ref, b_ref, o_ref, *, rows, chunk):
    w0 = w_ref[0:1, :]
    w1 = w_ref[1:2, :]
    w2 = w_ref[2:3, :]
    bias = b_ref[...]
    width = u_ref.shape[-1]
    ridx = lax.broadcasted_iota(jnp.int32, (chunk, width), 0)
    n_chunks = rows // chunk
    for ci in range(n_chunks):
        r0 = ci * chunk
        cur = u_ref[r0:r0 + chunk, :]
        if ci > 0:
            prev_row = u_ref[r0 - 8:r0, :][7:8, :]
        else:
            prev_row = jnp.zeros((1, width), F32)
        if ci < n_chunks - 1:
            next_row = u_ref[r0 + chunk:r0 + chunk + 8, :][0:1, :]
        else:
            next_row = jnp.zeros((1, width), F32)
        up = jnp.where(ridx == 0, prev_row, pltpu.roll(cur, 1, 0))
        dn = jnp.where(ridx == chunk - 1, next_row, pltpu.roll(cur, chunk - 1, 0))
        o_ref[0, r0:r0 + chunk, :] = up * w0 + cur * w1 + dn * w2 + bias


def _short_conv(p, conv_w, conv_b, with_ctx):
    cw = 256
    hw = (HYENA_ORDER + 1) * HYENA_WIDTH
    ncol = hw // cw
    per = HYENA_WIDTH // cw
    col0 = PH_OFF // cw
    out_shape = jax.ShapeDtypeStruct((HYENA_ORDER + 1, T_ALL if with_ctx else T_LAT, HYENA_WIDTH), F32)
    b2 = conv_b.reshape(1, hw)

    def call(rows, blk0, alias):
        kern = functools.partial(_short_conv_kernel, rows=rows, chunk=min(rows, 256))
        ins = [p, conv_w, b2]
        specs = [pl.BlockSpec((rows, cw), lambda b, j: (blk0 + b, col0 + j)),
                 pl.BlockSpec((3, cw), lambda b, j: (0, j)),
                 pl.BlockSpec((1, cw), lambda b, j: (0, j))]
        aliases = {}
        if alias is not None:
            ins.append(alias)
            specs.append(pl.BlockSpec(memory_space=pl.ANY))
            aliases = {3: 0}
            inner = kern

            def kern(u_ref, w_ref, b_ref, a_ref, o_ref):
                del a_ref
                inner(u_ref, w_ref, b_ref, o_ref)

        return pl.pallas_call(
            kern, grid=(BATCH, ncol), in_specs=specs,
            out_specs=pl.BlockSpec((1, rows, cw), lambda b, j: (j // per, blk0 + b, j % per)),
            out_shape=out_shape, input_output_aliases=aliases,
            compiler_params=_cparams(("parallel", "parallel")), name="hyena_short_conv",
        )(*ins)

    z3 = call(SEQ, 0, None)
    if with_ctx:
        z3 = call(CTX_LEN, T_LAT // CTX_LEN, z3)
    return z3


def _filter_mlp_kernel(ft_ref, w1_ref, b1_ref, fq_ref, w2_ref, b2_ref, o_ref):
    fq = fq_ref[...]
    h = jnp.sin(fq * (_dot(ft_ref[...].astype(BF16), w1_ref[...]) + b1_ref[...]))
    h = jnp.sin(fq * (_dot(h.astype(BF16), w2_ref[...]) + b2_ref[...]))
    o_ref[...] = h.astype(o_ref.dtype)


def _filter_kernel(h_ref, t_ref, w3f_ref, w3b_ref, dl_ref, m1_ref, *rest, n, dense):
    hb = h_ref[...]
    decay = jnp.exp(-t_ref[...] * dl_ref[...])
    tf = _dot(hb, w3f_ref[...]) * decay
    tb = _dot(hb, w3b_ref[...]) * decay
    tb = jnp.where(lax.broadcasted_iota(jnp.int32, tb.shape, 0) == 0, 0.0, tb)
    scale = 1.0 / (jnp.sum(jnp.abs(tf), axis=0, keepdims=True) + jnp.sum(jnp.abs(tb), axis=0, keepdims=True))
    cw = tf.shape[1]
    if dense:
        o_ref = rest[0]
        r = _dot(m1_ref[...], jnp.concatenate([tf, tb], axis=1).astype(BF16))
        nc = r.shape[0] // 2
        o_ref[0:nc, :] = (r[:nc, :cw] + r[:nc, cw:]) * scale
        o_ref[nc:, :] = (r[nc:, :cw] - r[nc:, cw:]) * scale
        return
    f2_ref, o_ref, tf_s, tb_s, are_f, aim_f, are_b, aim_b = rest
    tf_s[...] = tf
    tb_s[...] = tb
    half = HY_N1 // 2
    m1 = m1_ref[...]

    def stage1(i, carry):
        for u in range(UNROLL):
            n2 = i * UNROLL + u
            x = jnp.concatenate([tf_s[pl.ds(n2, half, stride=HY_N2), :], tb_s[pl.ds(n2, half, stride=HY_N2), :]],
                                axis=1).astype(BF16)
            r = _dot(m1, x)
            rows = pl.ds(n2, HY_N1, stride=PITCH)
            are_f[rows, :] = r[:HY_N1, :cw]
            are_b[rows, :] = r[:HY_N1, cw:]
            aim_f[rows, :] = r[HY_N1:, :cw]
            aim_b[rows, :] = r[HY_N1:, cw:]
        return carry

    lax.fori_loop(0, HY_N2 // UNROLL, stage1, 0)

    def stage2(i, carry):
        for u in range(UNROLL):
            k1 = i * UNROLL + u
            rows = pl.ds(pl.multiple_of(k1 * PITCH, 8), HY_N2)
            ar = jnp.concatenate([are_f[rows, :], are_b[rows, :]], axis=1)
            ai = jnp.concatenate([aim_f[rows, :], aim_b[rows, :]], axis=1)
            t = _dot(f2_ref[k1], _cstack(ar, ai).astype(BF16))
            o_ref[0, k1] = (t[:, 0:cw] + t[:, cw:2 * cw]) * scale
            o_ref[1, k1] = (t[:, 2 * cw:3 * cw] - t[:, 3 * cw:4 * cw]) * scale
        return carry

    lax.fori_loop(0, HY_N1 // UNROLL, stage2, 0)


def _filter_feats(n):
    pos = jnp.arange(n, dtype=F32)
    t = pos / max(n - 1, 1)
    omega = 2.0 * math.pi * pos / n
    bands = jnp.linspace(1e-4, FILTER_BANDS - 1, FILTER_BANDS, dtype=F32)
    feats = jnp.concatenate([t[:, None], jnp.cos(omega[:, None] * bands), -jnp.sin(omega[:, None] * bands)], axis=-1)
    return jnp.pad(feats, ((0, 0), (0, 128 - FILTER_EMB))), t[:, None]


def _filter_spectrum(n, filt, mats):
    w1, b1, freq, w2, b2, w3 = filt
    dense = n == CTX_LEN
    cw = 128
    nch = HYENA_WIDTH // cw
    feats, t = _filter_feats(n)
    w1p = jnp.pad(w1, ((0, 128 - FILTER_EMB), (0, 0))).astype(BF16)
    deltas = jnp.abs(jnp.linspace(math.log(DECAY_TARGET) / SLOW_DECAY_PCT, math.log(DECAY_TARGET) / FAST_DECAY_PCT,
                                  HYENA_WIDTH, dtype=F32)).reshape(1, HYENA_WIDTH)
    full = lambda a: pl.BlockSpec(a.shape, lambda *i: (0,) * a.ndim)
    row = lambda a: a.reshape(1, -1)
    w3b16 = w3.astype(BF16)
    tap_spec = lambda d: pl.BlockSpec((FILTER_HIDDEN, cw), lambda o, ch: (0, (o * 2 + d) * nch + ch))
    m1 = mats['hc_kh'] if dense else mats['hy_lead_kh']
    mlp_ins = [feats, w1p, row(b1), row(freq), w2.astype(BF16), row(b2)]
    hb = pl.pallas_call(
        _filter_mlp_kernel, grid=(1,), in_specs=[full(a) for a in mlp_ins],
        out_specs=pl.BlockSpec((n, FILTER_HIDDEN), lambda i: (0, 0)),
        out_shape=jax.ShapeDtypeStruct((n, FILTER_HIDDEN), BF16),
        compiler_params=_cparams(("arbitrary",)), name="hyena_filter_mlp",
    )(*mlp_ins)
    ins = [hb, t, w3b16, w3b16, deltas, m1]
    specs = [full(hb), full(t), tap_spec(0), tap_spec(1), pl.BlockSpec((1, cw), lambda o, ch: (0, ch)), full(m1)]
    if dense:
        nc = 2 * n
        out_shape = jax.ShapeDtypeStruct((HYENA_ORDER, 2 * nc, HYENA_WIDTH), F32)
        out_spec = pl.BlockSpec((None, 2 * nc, cw), lambda o, ch: (o, 0, ch))
        scratch = []
    else:
        ins.append(mats['hy_slab_f2'])
        specs.append(full(mats['hy_slab_f2']))
        out_shape = jax.ShapeDtypeStruct((HYENA_ORDER, 2, HY_N1, HY_N2, HYENA_WIDTH), F32)
        out_spec = pl.BlockSpec((None, 2, HY_N1, HY_N2, cw), lambda o, ch: (o, 0, 0, 0, ch))
        scratch = [pltpu.VMEM((n, cw), F32)] * 2 + [pltpu.VMEM((HY_N1 * PITCH, cw), F32)] * 4
    return pl.pallas_call(
        functools.partial(_filter_kernel, n=n, dense=dense),
        grid=(HYENA_ORDER, nch), in_specs=specs, out_specs=out_spec, out_shape=out_shape,
        scratch_shapes=scratch,
        compiler_params=_cparams(("parallel", "parallel")),
        name="hyena_filter_ctx" if dense else "hyena_filter",
    )(*ins)


def _hyena_conv_kernel(y_ref, g_ref, k_ref, m1_ref, f2f_ref, f2i_ref, m3_ref, b_ref, o_ref, are, aim):
    half = HY_N1 // 2
    m1 = m1_ref[...]

    def stage1(i, carry):
        for u in range(UNROLL):
            n2 = i * UNROLL + u
            x = jnp.concatenate([y_ref[pl.ds(n2, half, stride=HY_N2), :],
                                 y_ref[pl.ds(SEQ + n2, half, stride=HY_N2), :]], axis=0).astype(BF16)
            r = _dot(m1, x)
            are[pl.ds(n2, HY_N1, stride=PITCH), :] = r[:HY_N1]
            aim[pl.ds(n2, HY_N1, stride=PITCH), :] = r[HY_N1:]
        return carry

    lax.fori_loop(0, HY_N2 // UNROLL, stage1, 0)
    cw = o_ref.shape[-1]

    def stage2(i, carry):
        for u in range(UNROLL):
            k1 = i * UNROLL + u
            r0 = pl.multiple_of(k1 * PITCH, 8)
            y = _dot(f2f_ref[k1], _cstack(are[pl.ds(r0, HY_N2), :], aim[pl.ds(r0, HY_N2), :]).astype(BF16))
            yr, yi = y[:, :cw], y[:, cw:]
            kr, ki = k_ref[0, k1], k_ref[1, k1]
            w = _dot(f2i_ref[k1], _cstack(yr * kr - yi * ki, yr * ki + yi * kr).astype(BF16))
            are[pl.ds(r0, HY_N2), :] = w[:, :cw]
            aim[pl.ds(r0, HY_N2), :] = w[:, cw:]
        return carry

    lax.fori_loop(0, HY_N1 // UNROLL, stage2, 0)
    m3 = m3_ref[...]
    bias = b_ref[...]

    def stage3(i, carry):
        for u in range(UNROLL):
            n2 = i * UNROLL + u
            bn = jnp.concatenate([are[pl.ds(n2, HY_N1, stride=PITCH), :], aim[pl.ds(n2, HY_N1, stride=PITCH), :]],
                                 axis=0).astype(BF16)
            y = _dot(m3, bn)
            for b in range(2):
                rows = pl.ds(b * SEQ + n2, half, stride=HY_N2)
                o_ref[rows, :] = g_ref[rows, :] * (y[b * half:(b + 1) * half] + y_ref[rows, :] * bias)
        return carry

    lax.fori_loop(0, HY_N2 // UNROLL, stage3, 0)


def _hyena_mix(z3, filt, hyena_bias, mats, with_ctx):
    c = HYENA_WIDTH
    cw = 128
    nch = c // cw
    pairs = BATCH // 2
    full = lambda a: pl.BlockSpec(a.shape, lambda *i: (0,) * a.ndim)
    kspec = _filter_spectrum(SEQ, filt, mats)
    if with_ctx:
        kspec_c = _filter_spectrum(CTX_LEN, filt, mats)
    bias3 = hyena_bias.reshape(HYENA_ORDER, 1, c)
    scr = pltpu.VMEM((HY_N1 * PITCH, cw), F32)
    y = None
    for o in range(HYENA_ORDER):
        if y is None:
            xin, xspec = z3, pl.BlockSpec((None, 2 * SEQ, cw), lambda b, ch: (0, b, ch))
        else:
            xin, xspec = y, pl.BlockSpec((2 * SEQ, cw), lambda b, ch: (b, ch))
        ynew = pl.pallas_call(
            _hyena_conv_kernel,
            grid=(pairs, nch),
            in_specs=[xspec,
                      pl.BlockSpec((None, 2 * SEQ, cw), lambda b, ch, o=o: (o + 1, b, ch)),
                      pl.BlockSpec((None, 2, HY_N1, HY_N2, cw), lambda b, ch, o=o: (o, 0, 0, 0, ch)),
                      full(mats['hy_lead_f']), full(mats['hy_slab_f2']), full(mats['hy_slab_i2']),
                      full(mats['hy_lead_i']),
                      pl.BlockSpec((None, 1, cw), lambda b, ch, o=o: (o, 0, ch))],
            out_specs=pl.BlockSpec((2 * SEQ, cw), lambda b, ch: (b, ch)),
            out_shape=jax.ShapeDtypeStruct((T_ALL if with_ctx else T_LAT, c), F32),
            scratch_shapes=[scr, scr],
            compiler_params=_cparams(("parallel", "parallel"), 58 * 1024 * 1024),
            name="hyena_conv",
        )(xin, z3, kspec, mats['hy_lead_f'], mats['hy_slab_f2'], mats['hy_slab_i2'], mats['hy_lead_i'], bias3)
        if with_ctx:
            nc = 2 * CTX_LEN
            blk0 = T_LAT // nc
            if y is None:
                cin, cspec_in = z3, pl.BlockSpec((1, nc, c), lambda b: (0, blk0 + b, 0))
            else:
                cin, cspec_in = y, pl.BlockSpec((nc, c), lambda b: (blk0 + b, 0))
            xc = _lead(mats['hc_f'], [cin], [cspec_in], (pairs,),
                       jax.ShapeDtypeStruct((pairs, 2 * nc, c), F32),
                       pl.BlockSpec((1, 2 * nc, c), lambda b: (b, 0, 0)), name="hyena_ctx_fwd")
            ynew = _lead(mats['hc_i'], [xc], [pl.BlockSpec((1, 2 * nc, c), lambda b: (b, 0, 0))],
                         (pairs,), jax.ShapeDtypeStruct((T_ALL, c), F32),
                         pl.BlockSpec((nc, c), lambda b: (blk0 + b, 0)),
                         kspec=(kspec_c, pl.BlockSpec((1, 2 * nc, c), lambda b, o=o: (o, 0, 0))),
                         epi=[(z3, pl.BlockSpec((1, nc, c), lambda b, o=o: (o + 1, blk0 + b, 0))),
                              (cin, cspec_in),
                              (bias3, pl.BlockSpec((1, 1, c), lambda b, o=o: (o, 0, 0)))],
                         alias_to=ynew, name="hyena_ctx_inv")
        y = ynew
    return y


def _route_tile(lt, br, base, tri):
    tm = lt.shape[1]
    aff = jax.nn.sigmoid(lt)
    biased = aff + br
    b = [biased[e:e + 1, :] for e in range(N_EXPERTS)]
    a = [aff[e:e + 1, :] for e in range(N_EXPERTS)]
    epg = EXPERTS_PER_GROUP
    scores = []
    for g in range(N_GROUPS):
        x0, x1, x2, x3 = b[epg * g:epg * g + epg]
        s1, t1 = jnp.maximum(x0, x1), jnp.minimum(x0, x1)
        s2, t2 = jnp.maximum(x2, x3), jnp.minimum(x2, x3)
        scores.append(jnp.maximum(s1, s2) + jnp.maximum(jnp.minimum(s1, s2), jnp.maximum(t1, t2)))
    best = scores[0]
    gsel = jnp.zeros((1, tm), jnp.int32)
    for g in range(1, N_GROUPS):
        gsel = jnp.where(scores[g] > best, g, gsel)
        best = jnp.maximum(best, scores[g])

    def pick(rows, j):
        out = rows[j]
        for g in range(1, N_GROUPS):
            out = jnp.where(gsel == g, rows[epg * g + j], out)
        return out

    v = [pick(b, j) for j in range(epg)]
    av = [pick(a, j) for j in range(epg)]
    i1 = jnp.zeros((1, tm), jnp.int32)
    m1 = v[0]
    for j in range(1, epg):
        i1 = jnp.where(v[j] > m1, j, i1)
        m1 = jnp.maximum(m1, v[j])
    neg = jnp.float32(-3.0e38)
    i2 = jnp.zeros((1, tm), jnp.int32)
    m2 = jnp.full((1, tm), neg, F32)
    for j in range(epg):
        cand = jnp.where(i1 == j, neg, v[j])
        take = cand > m2
        i2 = jnp.where(take, j, i2)
        m2 = jnp.where(take, cand, m2)

    def sel(rows, idx):
        out = rows[0]
        for j in range(1, epg):
            out = jnp.where(idx == j, rows[j], out)
        return out

    a1, a2 = sel(av, i1), sel(av, i2)
    den = a1 + a2
    e1 = gsel * epg + i1
    e2 = gsel * epg + i2
    eio = lax.broadcasted_iota(jnp.int32, (N_EXPERTS, tm), 0)
    oh1 = jnp.where(eio == e1, 1.0, 0.0)
    oh2 = jnp.where(eio == e2, 1.0, 0.0)
    oh = oh1 + oh2
    tot = base + _dot(oh.astype(BF16), tri)
    r1 = jnp.sum(oh1 * tot, axis=0, keepdims=True)
    r2 = jnp.sum(oh2 * tot, axis=0, keepdims=True)
    new_base = base + jnp.sum(oh, axis=1, keepdims=True)
    return (e1, e2), (a1 / den, a2 / den), (r1.astype(jnp.int32), r2.astype(jnp.int32)), new_base


def _merge_kernel(x_ref, ya_ref, yf_ref, yh_ref, gt_ref, wa_ref, wf_ref, wh_ref, wo_ref, g1_ref,
                  gn_ref, sh_ref, sc_ref, wrt_ref, br_ref, xo_ref, h2_ref, e_ref, w_ref, r_ref, cnt_ref):
    d = D_MODEL
    merged = jax.nn.sigmoid(gt_ref[:, 0:d]) * _dot(ya_ref[...], wa_ref[...])
    merged += jax.nn.sigmoid(gt_ref[:, d:2 * d]) * _dot(yf_ref[...].astype(BF16), wf_ref[...])
    merged += jax.nn.sigmoid(gt_ref[:, 2 * d:3 * d]) * _dot(yh_ref[...].astype(BF16), wh_ref[...])
    xn = x_ref[...] + g1_ref[0, 0] * _dot(merged.astype(BF16), wo_ref[...])
    xo_ref[...] = xn
    h2f = _rms_mod(xn, gn_ref[...], sh_ref[0, 0], sc_ref[0, 0])
    _rows_to_tiles(h2_ref, h2f, 0, h2f.shape[0])
    h2 = h2f.astype(BF16)

    @pl.when(pl.program_id(0) == 0)
    def _():
        cnt_ref[...] = jnp.zeros_like(cnt_ref)

    tm = h2.shape[0]
    lt = lax.dot_general(wrt_ref[...], h2, (((1,), (1,)), ((), ())), preferred_element_type=F32)
    tri = jnp.where(lax.broadcasted_iota(jnp.int32, (tm, tm), 0) < lax.broadcasted_iota(jnp.int32, (tm, tm), 1),
                    1.0, 0.0).astype(BF16)
    es, ws, rs, new_base = _route_tile(lt, br_ref[...], cnt_ref[:, 0:1], tri)
    e_ref[0:1, :], e_ref[1:2, :] = es
    w_ref[0:1, :], w_ref[1:2, :] = ws
    r_ref[0:1, :], r_ref[1:2, :] = rs
    cnt_ref[...] = jnp.broadcast_to(new_base, cnt_ref.shape)


def _merge(x, ya, yf, yh, p, wa, wf, wh, wo, mod4, gain2, wrt, br, n_tok):
    tm = TM
    row = _mod_row(tm)
    full = lambda a: pl.BlockSpec(a.shape, lambda i: (0,) * a.ndim)
    modspec = lambda k: pl.BlockSpec((1, 1, 1, D_MODEL), lambda i: (row(i), k, 0, 0))
    tok = lambda w: pl.BlockSpec((tm, w), lambda i: (i, 0))
    lane = pl.BlockSpec((TOP_K, tm), lambda i: (0, i))
    return pl.pallas_call(
        _merge_kernel,
        grid=(n_tok // tm,),
        in_specs=[tok(D_MODEL), tok(ATTN_WIDTH), tok(FNET_WIDTH), tok(HYENA_WIDTH),
                  pl.BlockSpec((pl.Element(tm), pl.Element(3 * D_MODEL)), lambda i: (i * tm, PG_OFF)),
                  full(wa), full(wf), full(wh), full(wo), modspec(2), full(gain2), modspec(3), modspec(4),
                  full(wrt), full(br)],
        out_specs=[tok(D_MODEL), pl.BlockSpec((tm * ROW_TILE, 128), lambda i: (i, 0)), lane, lane, lane,
                   pl.BlockSpec((N_EXPERTS, 128), lambda i: (0, 0))],
        out_shape=[jax.ShapeDtypeStruct((n_tok, D_MODEL), F32),
                   jax.ShapeDtypeStruct((n_tok * ROW_TILE, 128), F32),
                   jax.ShapeDtypeStruct((TOP_K, n_tok), jnp.int32),
                   jax.ShapeDtypeStruct((TOP_K, n_tok), F32),
                   jax.ShapeDtypeStruct((TOP_K, n_tok), jnp.int32),
                   jax.ShapeDtypeStruct((N_EXPERTS, 128), F32)],
        compiler_params=_cparams(("arbitrary",)),
        name="merge_out_norm_route",
    )(x, ya, yf, yh, p, wa, wf, wh, wo, mod4, gain2, mod4, mod4, wrt, br)


def _row_copy(src_ref, row, dst_ref, r, sem):
    return pltpu.make_async_copy(src_ref.at[pl.ds(row, 1), :], dst_ref.at[pl.ds(r, 1), :], sem)


ROW_TILE = D_MODEL // 128


def _rows_from_tiles(ref, n):
    return jnp.concatenate([ref[pl.ds(s, n, stride=ROW_TILE), :] for s in range(ROW_TILE)], axis=1)


def _rows_to_tiles(ref, val, col0, n):
    for j in range(val.shape[1] // 128):
        ref[pl.ds(col0 // 128 + j, n, stride=ROW_TILE), :] = val[:, j * 128:(j + 1) * 128]


def _expert_kernel(be_ref, na_ref, rt_cur_ref, rt_next_ref, h2_ref, wg_ref, wu_ref, wd_ref, o_ref,
                   xbuf, sem, wg_s, wu_s, wd_s):
    i = pl.program_id(0)
    n_act = na_ref[0]
    bm = xbuf.shape[1]
    active = i < n_act

    def gather(rt_ref, slot):
        def body(j, carry):
            r0 = pl.multiple_of(j * 8, 8)
            tile = xbuf.at[slot, pl.ds(r0, 8)]
            for u in range(8):
                _row_copy(h2_ref, rt_ref[0, 0, r0 + u], tile, u, sem.at[slot]).start()
            return carry

        lax.fori_loop(0, bm // 8, body, 0)

    @pl.when((i == 0) & active)
    def _():
        gather(rt_cur_ref, 0)

    @pl.when(i + 1 < n_act)
    def _():
        gather(rt_next_ref, (i + 1) % 2)

    prev = be_ref[jnp.maximum(i - 1, 0)]

    @pl.when(active & ((i == 0) | (be_ref[i] != prev)))
    def _():
        wg_s[...] = wg_ref[0].astype(BF16)
        wu_s[...] = wu_ref[0].astype(BF16)
        wd_s[...] = wd_ref[0].astype(BF16)

    @pl.when(active)
    def _():
        slot = i % 2
        pltpu.make_async_copy(h2_ref.at[pl.ds(0, bm), :], xbuf.at[slot], sem.at[slot]).wait()
        x = xbuf[slot].astype(BF16)
        g = _dot(x, wg_s[...])
        u = _dot(x, wu_s[...])
        h = (g * jax.nn.sigmoid(g)) * u
        o_ref[...] = _dot(h.astype(BF16), wd_s[...])

    @pl.when(jnp.logical_not(active))
    def _():
        o_ref[...] = jnp.zeros_like(o_ref)


def _experts(h2, row_tok, blk_expert, n_active, wg, wu, wd):
    bm = EXPERT_BM
    n_blk = row_tok.shape[0] // bm
    rt = row_tok.reshape(n_blk, 1, bm)
    wspec = lambda k, n: pl.BlockSpec((1, k, n), lambda i, be, na: (be[i], 0, 0))
    return pl.pallas_call(
        _expert_kernel,
        grid_spec=pltpu.PrefetchScalarGridSpec(
            num_scalar_prefetch=2,
            grid=(n_blk,),
            in_specs=[pl.BlockSpec((1, 1, bm), lambda i, be, na: (i, 0, 0), memory_space=pltpu.SMEM),
                      pl.BlockSpec((1, 1, bm), lambda i, be, na: (jnp.minimum(i + 1, n_blk - 1), 0, 0),
                                   memory_space=pltpu.SMEM),
                      pl.BlockSpec(memory_space=pl.ANY),
                      wspec(D_MODEL, EXPERT_FF), wspec(D_MODEL, EXPERT_FF), wspec(EXPERT_FF, D_MODEL)],
            out_specs=pl.BlockSpec((bm, D_MODEL), lambda i, be, na: (i, 0)),
            scratch_shapes=[pltpu.VMEM((2, bm, D_MODEL), F32), pltpu.SemaphoreType.DMA((2,)),
                            pltpu.VMEM((D_MODEL, EXPERT_FF), BF16), pltpu.VMEM((D_MODEL, EXPERT_FF), BF16),
                            pltpu.VMEM((EXPERT_FF, D_MODEL), BF16)]),
        out_shape=jax.ShapeDtypeStruct((row_tok.shape[0], D_MODEL), F32),
        compiler_params=_cparams(("arbitrary",)),
        name="moe_experts",
    )(blk_expert, n_active, rt, rt, h2, wg, wu, wd)


def _dispatch(e_idx, rank, counts, n_tok):
    bm = EXPERT_BM
    counts = counts.astype(jnp.int32)
    padded = (counts + bm - 1) // bm * bm
    pad_end = jnp.cumsum(padded)
    pad_start = pad_end - padded
    experts = jnp.arange(N_EXPERTS, dtype=jnp.int32)
    start = jnp.sum(jnp.where(e_idx[..., None] == experts, pad_start, 0), axis=-1)
    dest = start + rank
    n_rows = -(-(n_tok * TOP_K) // bm) * bm + N_EXPERTS * bm
    n_blk = n_rows // bm
    tok = jnp.tile(jnp.arange(n_tok, dtype=jnp.int32), TOP_K)
    row_tok = jnp.zeros((n_rows,), jnp.int32).at[dest.reshape(-1)].set(tok)
    blk_start = jnp.arange(n_blk, dtype=jnp.int32) * bm
    blk_expert = jnp.minimum(jnp.sum((blk_start[:, None] >= pad_end[None, :]).astype(jnp.int32), axis=-1),
                             N_EXPERTS - 1)
    n_active = (pad_end[-1] // bm).astype(jnp.int32).reshape(1)
    return row_tok, dest, blk_expert, n_active


def _combine_kernel(d_cur_ref, d_next_ref, x_ref, w_ref, g2_ref, gf_ref, ys_ref, o_ref, ybuf, sem,
                    *, final, n_tiles):
    i = pl.program_id(0)
    tm = x_ref.shape[0]

    def gather(d_ref, slot):
        def body(j, carry):
            r0 = pl.multiple_of(j * 8, 8)
            for k in range(TOP_K):
                tile = ybuf.at[slot, k, pl.ds(r0, 8)]
                for u in range(8):
                    _row_copy(ys_ref, d_ref[0, k, r0 + u], tile, u, sem.at[slot]).start()
            return carry

        lax.fori_loop(0, tm // 8, body, 0)

    @pl.when(i == 0)
    def _():
        gather(d_cur_ref, 0)

    @pl.when(i + 1 < n_tiles)
    def _():
        gather(d_next_ref, (i + 1) % 2)

    slot = i % 2
    for k in range(TOP_K):
        pltpu.make_async_copy(ys_ref.at[pl.ds(0, tm), :], ybuf.at[slot, k], sem.at[slot]).wait()
    w = w_ref[...]
    moe = ybuf[slot, 0] * w[:, 0:1] + ybuf[slot, 1] * w[:, 1:2]
    xn = x_ref[...] + g2_ref[0, 0] * moe
    if final:
        y = xn * lax.rsqrt(jnp.mean(xn * xn, axis=-1, keepdims=True) + EPS)
        xn = y * gf_ref[...]
    o_ref[...] = xn


def _combine(x, ys, dest, w_sel, mod4, gain_final, n_tok, final):
    tm = 256
    n_tiles = n_tok // tm
    row = _mod_row(tm)
    tok = lambda w: pl.BlockSpec((tm, w), lambda i: (i, 0))
    d3 = jnp.transpose(dest.reshape(TOP_K, n_tiles, tm), (1, 0, 2))
    return pl.pallas_call(
        functools.partial(_combine_kernel, final=final, n_tiles=n_tiles),
        grid=(n_tiles,),
        in_specs=[pl.BlockSpec((1, TOP_K, tm), lambda i: (i, 0, 0), memory_space=pltpu.SMEM),
                  pl.BlockSpec((1, TOP_K, tm), lambda i: (jnp.minimum(i + 1, n_tiles - 1), 0, 0),
                               memory_space=pltpu.SMEM),
                  tok(D_MODEL), tok(TOP_K),
                  pl.BlockSpec((1, 1, 1, D_MODEL), lambda i: (row(i), 5, 0, 0)),
                  pl.BlockSpec((1, D_MODEL), lambda i: (0, 0)),
                  pl.BlockSpec(memory_space=pl.ANY)],
        out_specs=tok(D_MODEL),
        out_shape=jax.ShapeDtypeStruct((n_tok if final else T_ALL, D_MODEL), F32),
        scratch_shapes=[pltpu.VMEM((2, TOP_K, tm, D_MODEL), F32), pltpu.SemaphoreType.DMA((2,))],
        compiler_params=_cparams(("arbitrary",)),
        name="moe_combine",
    )(d3, d3, x, w_sel, mod4, gain_final, ys)


def _moe_kernel(be_ref, na_ref, ip_ref, ic_ref, sc_ref, sn_ref, h2_ref, wg_ref, wu_ref, wd_ref, y_ref,
                xbuf, obuf, gsem, ssem, wg_s, wu_s, wd_s):
    i = pl.program_id(0)
    n_act = na_ref[0]
    sub = ROW_TILE
    bm = xbuf.shape[1] // sub
    active = i < n_act
    slot = i % 2
    other = (i + 1) % 2
    n_chunk = 4
    cw = EXPERT_FF // n_chunk
    rows_per = bm // n_chunk

    def tile_rows(ref, row):
        return ref.at[pl.ds(pl.multiple_of(row * sub, sub), sub), :]

    def gather(src_ref, dst_slot, lo, hi):
        for r in range(lo, hi):
            pltpu.make_async_copy(tile_rows(h2_ref, src_ref[0, 0, r]), xbuf.at[dst_slot, pl.ds(r * sub, sub), :],
                                  gsem.at[dst_slot]).start()

    def scatter(info_ref, src_slot, lo, hi):
        for r in range(lo, hi):
            pltpu.make_async_copy(obuf.at[src_slot, pl.ds(r * sub, sub), :], tile_rows(y_ref, info_ref[0, 0, r]),
                                  ssem.at[src_slot]).start()

    def wait_gather(s):
        pltpu.make_async_copy(h2_ref.at[pl.ds(0, bm * sub), :], xbuf.at[s], gsem.at[s]).wait()

    def wait_scatter(s):
        pltpu.make_async_copy(obuf.at[s], y_ref.at[pl.ds(0, bm * sub), :], ssem.at[s]).wait()

    @pl.when((i == 0) & active)
    def _():
        obuf[...] = jnp.zeros_like(obuf)
        gather(sc_ref, 0, 0, bm)

    prev = be_ref[jnp.maximum(i - 1, 0)]

    @pl.when(active & ((i == 0) | (be_ref[i] != prev)))
    def _():
        wg_s[...] = wg_ref[...].astype(BF16)
        wu_s[...] = wu_ref[...].astype(BF16)
        wd_s[...] = wd_ref[...].astype(BF16)

    @pl.when(active)
    def _():
        wait_gather(slot)
        x = _rows_from_tiles(xbuf.at[slot], bm).astype(BF16)
        hs = []
        for c in range(n_chunk):
            g = _dot(x, wg_s[:, c * cw:(c + 1) * cw])
            u = _dot(x, wu_s[:, c * cw:(c + 1) * cw])
            hs.append(((g * jax.nn.sigmoid(g)) * u).astype(BF16))
            gather(sn_ref, other, c * rows_per, (c + 1) * rows_per)
        h = jnp.concatenate(hs, axis=1)
        for c in range(n_chunk):
            _rows_to_tiles(obuf.at[slot], _dot(h, wd_s[:, c * cw:(c + 1) * cw]), c * cw, bm)
            scatter(ip_ref, other, c * rows_per, (c + 1) * rows_per)
        wait_scatter(other)

    @pl.when(i == n_act - 1)
    def _():
        wait_gather(other)
        scatter(ic_ref, slot, 0, bm)
        wait_scatter(slot)


def _moe(h2, info, blk_expert, n_active, wg, wu, wd, layer, n_tok):
    bm = EXPERT_BM
    n_rows = info.shape[0]
    n_blk = n_rows // bm
    info3 = info.reshape(n_blk, 1, bm)
    src3 = jnp.where(info3 >= TOP_K * n_tok, 0, info3 % n_tok)
    wspec = lambda k, n: pl.BlockSpec((None, None, k, n), lambda i, be, na: (layer, be[i], 0, 0))
    ispec = lambda f: pl.BlockSpec((1, 1, bm), lambda i, be, na: (f(i), 0, 0), memory_space=pltpu.SMEM)
    return pl.pallas_call(
        _moe_kernel,
        grid_spec=pltpu.PrefetchScalarGridSpec(
            num_scalar_prefetch=2,
            grid=(n_blk,),
            in_specs=[ispec(lambda i: jnp.maximum(i - 1, 0)), ispec(lambda i: i),
                      ispec(lambda i: i), ispec(lambda i: jnp.minimum(i + 1, n_blk - 1)),
                      pl.BlockSpec(memory_space=pl.ANY),
                      wspec(D_MODEL, EXPERT_FF), wspec(D_MODEL, EXPERT_FF), wspec(EXPERT_FF, D_MODEL)],
            out_specs=pl.BlockSpec(memory_space=pl.ANY),
            scratch_shapes=[pltpu.VMEM((2, bm * ROW_TILE, 128), F32), pltpu.VMEM((2, bm * ROW_TILE, 128), F32),
                            pltpu.SemaphoreType.DMA((2,)), pltpu.SemaphoreType.DMA((2,)),
                            pltpu.VMEM((D_MODEL, EXPERT_FF), BF16), pltpu.VMEM((D_MODEL, EXPERT_FF), BF16),
                            pltpu.VMEM((EXPERT_FF, D_MODEL), BF16)]),
        out_shape=jax.ShapeDtypeStruct(((TOP_K * n_tok + n_rows) * ROW_TILE, 128), F32),
        compiler_params=_cparams(("arbitrary",)),
        name="moe_experts",
    )(blk_expert, n_active, info3, info3, src3, src3, h2, wg, wu, wd)


def _dispatch_info(e_idx, rank, counts, n_tok):
    bm = EXPERT_BM
    counts = counts.astype(jnp.int32)
    padded = (counts + bm - 1) // bm * bm
    pad_end = jnp.cumsum(padded)
    pad_start = pad_end - padded
    experts = jnp.arange(N_EXPERTS, dtype=jnp.int32)
    start = jnp.sum(jnp.where(e_idx[..., None] == experts, pad_start, 0), axis=-1)
    dest = start + rank
    n_rows = -(-(n_tok * TOP_K) // bm) * bm + N_EXPERTS * bm
    n_blk = n_rows // bm
    spill = TOP_K * n_tok + jnp.arange(n_rows, dtype=jnp.int32)
    info = spill.at[dest.reshape(-1)].set(jnp.arange(TOP_K * n_tok, dtype=jnp.int32))
    blk_start = jnp.arange(n_blk, dtype=jnp.int32) * bm
    blk_expert = jnp.minimum(jnp.sum((blk_start[:, None] >= pad_end[None, :]).astype(jnp.int32), axis=-1),
                             N_EXPERTS - 1)
    n_active = (pad_end[-1] // bm).astype(jnp.int32).reshape(1)
    return info, blk_expert, n_active


def _residual_kernel(x_ref, y0_ref, y1_ref, w_ref, g2_ref, gf_ref, o_ref, *, final):
    w = w_ref[...]
    tm = x_ref.shape[0]
    moe = _rows_from_tiles(y0_ref, tm) * w[:, 0:1] + _rows_from_tiles(y1_ref, tm) * w[:, 1:2]
    xn = x_ref[...] + g2_ref[0, 0] * moe
    if final:
        y = xn * lax.rsqrt(jnp.mean(xn * xn, axis=-1, keepdims=True) + EPS)
        xn = y * gf_ref[...]
    o_ref[...] = xn


def _residual(x, y, w_sel, mod4, gain_final, n_tok, final):
    tm = TM
    n_tiles = n_tok // tm
    row = _mod_row(tm)
    tok = lambda w: pl.BlockSpec((tm, w), lambda i: (i, 0))
    return pl.pallas_call(
        functools.partial(_residual_kernel, final=final),
        grid=(n_tiles,),
        in_specs=[tok(D_MODEL), pl.BlockSpec((tm * ROW_TILE, 128), lambda i: (i, 0)),
                  pl.BlockSpec((tm * ROW_TILE, 128), lambda i: (n_tiles + i, 0)), tok(TOP_K),
                  pl.BlockSpec((1, 1, 1, D_MODEL), lambda i: (row(i), 5, 0, 0)),
                  pl.BlockSpec((1, D_MODEL), lambda i: (0, 0))],
        out_specs=tok(D_MODEL),
        out_shape=jax.ShapeDtypeStruct((n_tok if final else T_ALL, D_MODEL), F32),
        compiler_params=_cparams(("parallel",)),
        name="moe_residual",
    )(x, y, y, w_sel, mod4, gain_final)


def kernel(x, c, ctx, c_ctx, w_mod, b_mod, norm_mix, norm_ffn, w_in, attn_sink, conv_w, conv_b, filt_w1, filt_b1, filt_freq, filt_w2, filt_b2, filt_w3, hyena_bias, w_branch_attn, w_branch_fnet, w_branch_hyena, w_out, w_router, b_router, w_exp_gate, w_exp_up, w_exp_down, norm_final):
    mats = _dft_mats()
    cos_t, sin_t = _rope_tables()
    c8 = jnp.concatenate([c, c_ctx[None, :], jnp.zeros((8 - BATCH - 1, D_MODEL), F32)], axis=0)
    mod_all = _modulation(c8, w_mod, b_mod)
    xa = jnp.concatenate([x.reshape(T_LAT, D_MODEL), ctx.reshape(T_CTX, D_MODEL)], axis=0)
    wrt = w_router.T.astype(BF16)
    br = b_router.astype(F32).reshape(N_EXPERTS, 1)
    gain_final = norm_final.reshape(1, D_MODEL)
    out = None
    for l in range(DEPTH):
        last = l == DEPTH - 1
        with_ctx = not last
        n_tok = T_LAT if last else T_ALL
        mod4 = mod_all[l].reshape(8, N_MOD, 1, D_MODEL)
        p = _norm_proj(xa, norm_mix[l].reshape(1, D_MODEL), mod4, w_in, l, T_ALL)
        ya = _attention(p, attn_sink[l], cos_t, sin_t, with_ctx)
        yf = _fourier_mix(p, mats, with_ctx)
        z3 = _short_conv(p, conv_w[l], conv_b[l], with_ctx)
        filt = (filt_w1[l], filt_b1[l], filt_freq[l], filt_w2[l], filt_b2[l], filt_w3[l])
        yh = _hyena_mix(z3, filt, hyena_bias[l], mats, with_ctx)
        xa, h2, e_idx, w_sel, rank, cnt = _merge(
            xa, ya, yf, yh, p, w_branch_attn[l].astype(BF16), w_branch_fnet[l].astype(BF16),
            w_branch_hyena[l].astype(BF16), w_out[l].astype(BF16), mod4,
            norm_ffn[l].reshape(1, D_MODEL), wrt, br, n_tok)
        info, blk_expert, n_active = _dispatch_info(e_idx, rank, cnt[:, 0], n_tok)
        y = _moe(h2, info, blk_expert, n_active, w_exp_gate, w_exp_up, w_exp_down, l, n_tok)
        res = _residual(xa, y, w_sel.T, mod4, gain_final, n_tok, last)
        if last:
            out = res
        else:
            xa = res
    return out.reshape(BATCH, SEQ, D_MODEL)
```

```python
import functools
import math

import jax
import jax.numpy as jnp
from jax import lax
from jax.experimental import pallas as pl
from jax.experimental.pallas import tpu as pltpu

F32 = jnp.float32
BF16 = jnp.bfloat16

D_MODEL = 1024
BATCH = 4
SEQ = 4096
DEPTH = 4
GRID_W = 64
CTX_LEN = 256
EPS = 1e-6
N_MOD = 6

HEAD_DIM = 64
N_Q_HEADS = 8
N_KV_HEADS = 2
Q_PER_KV = N_Q_HEADS // N_KV_HEADS
ATTN_BLOCK = 128
ROPE_BASE = 10000.0

FNET_GROUPS = 4
FNET_GROUP_DIM = 128
FNET_WIDTH = FNET_GROUPS * FNET_GROUP_DIM

HYENA_WIDTH = 512
HYENA_ORDER = 2
FILTER_EMB = 33
FILTER_BANDS = (FILTER_EMB - 1) // 2
FILTER_HIDDEN = 64
DECAY_TARGET = 1e-2
FAST_DECAY_PCT = 0.3
SLOW_DECAY_PCT = 1.5

ATTN_WIDTH = N_Q_HEADS * HEAD_DIM
KV_WIDTH = N_KV_HEADS * HEAD_DIM
Q_OFF = 0
K_OFF = Q_OFF + ATTN_WIDTH
V_OFF = K_OFF + KV_WIDTH
F_OFF = V_OFF + KV_WIDTH
H_OFF = F_OFF + FNET_WIDTH
G_OFF = H_OFF + (HYENA_ORDER + 1) * HYENA_WIDTH
IN_WIDTH = G_OFF + 3 * D_MODEL

N_EXPERTS = 16
N_GROUPS = 4
EXPERTS_PER_GROUP = N_EXPERTS // N_GROUPS
TOP_K = 2
EXPERT_FF = 1024

T_LAT = BATCH * SEQ
T_CTX = BATCH * CTX_LEN
T_ALL = T_LAT + T_CTX

PG_OFF = G_OFF
PH_OFF = H_OFF
PF_OFF = F_OFF
PQ_OFF = Q_OFF
PK_OFF = K_OFF
PV_OFF = V_OFF

HY_N = 2 * SEQ
HY_N2 = 64
HY_N1 = HY_N // HY_N2
FN_N = 64

PITCH = HY_N2 + 8
SEQ_PAD = SEQ // HY_N2 * PITCH
T_PAD = BATCH * SEQ_PAD + T_CTX
UNROLL = 8
TM = 512
EXPERT_BM = 512
VMEM_LIMIT = 52 * 1024 * 1024


def _cparams(sem, vmem=VMEM_LIMIT):
    return pltpu.CompilerParams(dimension_semantics=sem, vmem_limit_bytes=vmem)


def _dot(a, b):
    return jnp.dot(a, b, preferred_element_type=F32)


def _cis(expo, n):
    ang = (2.0 * math.pi / n) * jnp.mod(expo, n).astype(F32)
    return jnp.cos(ang), jnp.sin(ang)


def _real_form(gr, gi):
    return jnp.concatenate([jnp.concatenate([gr, -gi], axis=-1), jnp.concatenate([gi, gr], axis=-1)], axis=-2)


def _dft_mats():
    ar = lambda n: jnp.arange(n, dtype=jnp.int32)
    m = {}
    c, s = _cis(ar(HY_N1)[:, None] * ar(HY_N1 // 2)[None, :], HY_N1)
    m['hy_lead_f'] = _real_form(c, -s).astype(BF16)
    c, s = _cis(ar(HY_N1 // 2)[:, None] * ar(HY_N1)[None, :], HY_N1)
    m['hy_lead_i'] = _real_form(c, s).astype(BF16)
    c, s = _cis(ar(HY_N1)[:, None] * ar(HY_N1 // 2)[None, :], HY_N1)
    m['hy_lead_kh'] = jnp.concatenate([c, -s], axis=0).astype(BF16)
    a = ar(HY_N1)[:, None, None]
    k2 = ar(HY_N2)[None, :, None]
    n2 = ar(HY_N2)[None, None, :]
    c, s = _cis(n2 * (a + HY_N1 * k2), HY_N)
    m['hy_slab_f2'] = jnp.concatenate([c, -s], axis=-1).astype(BF16)
    ct = jnp.swapaxes(c, 1, 2) * (1.0 / HY_N)
    st = jnp.swapaxes(s, 1, 2) * (1.0 / HY_N)
    m['hy_slab_i2'] = jnp.concatenate([ct, st], axis=-1).astype(BF16)
    nc = 2 * CTX_LEN
    c, s = _cis(ar(nc)[:, None] * ar(CTX_LEN)[None, :], nc)
    m['hc_f'] = _real_form(c, -s).astype(BF16)
    c, s = _cis(ar(CTX_LEN)[:, None] * ar(nc)[None, :], nc)
    m['hc_i'] = _real_form(c * (1.0 / nc), s * (1.0 / nc)).astype(BF16)
    c, s = _cis(ar(nc)[:, None] * ar(CTX_LEN)[None, :], nc)
    m['hc_kh'] = jnp.concatenate([c, -s], axis=0).astype(BF16)
    c, s = _cis(ar(FNET_GROUP_DIM)[:, None] * ar(FNET_GROUP_DIM)[None, :], FNET_GROUP_DIM)
    m['fn_chan'] = jnp.concatenate([c, -s], axis=1).astype(BF16)
    c, s = _cis(ar(FN_N)[:, None] * ar(FN_N)[None, :], FN_N)
    m['fn_lead'] = _real_form(c, -s).astype(BF16)
    a = ar(FN_N)[:, None, None]
    k1 = ar(FN_N)[None, :, None]
    n1 = ar(FN_N)[None, None, :]
    scale = 1.0 / math.sqrt(SEQ * FNET_GROUP_DIM)
    c, s = _cis(n1 * (a + FN_N * k1), SEQ)
    m['fn_slab'] = jnp.concatenate([c * scale, s * scale], axis=-1).astype(BF16)
    scale = 1.0 / math.sqrt(CTX_LEN * FNET_GROUP_DIM)
    c, s = _cis(ar(CTX_LEN)[:, None] * ar(CTX_LEN)[None, :], CTX_LEN)
    m['fc'] = jnp.concatenate([c * scale, s * scale], axis=-1).astype(BF16)
    return m


def _mod_kernel(c_ref, w_ref, b_ref, o_ref):
    c = c_ref[...]
    s = c * jax.nn.sigmoid(c)
    o_ref[0] = _dot(s.astype(BF16), w_ref[0].astype(BF16)) + b_ref[0]


def _modulation(c8, w_mod, b_mod):
    tn = 1536
    n = N_MOD * D_MODEL
    return pl.pallas_call(
        _mod_kernel,
        grid=(DEPTH, n // tn),
        in_specs=[pl.BlockSpec((8, D_MODEL), lambda l, j: (0, 0)),
                  pl.BlockSpec((1, D_MODEL, tn), lambda l, j: (l, 0, j)),
                  pl.BlockSpec((1, 1, tn), lambda l, j: (l, 0, j))],
        out_specs=pl.BlockSpec((1, 8, tn), lambda l, j: (l, 0, j)),
        out_shape=jax.ShapeDtypeStruct((DEPTH, 8, n), F32),
        compiler_params=_cparams(("parallel", "parallel")),
        name="adaln_modulation",
    )(c8, w_mod, b_mod.reshape(DEPTH, 1, n))


def _mod_row(tm):
    tiles_per_batch = SEQ // tm
    return lambda i: jnp.minimum(i // tiles_per_batch, BATCH)


def _rms_mod(x, g, sh, sc):
    y = x * lax.rsqrt(jnp.mean(x * x, axis=-1, keepdims=True) + EPS)
    return (y * g) * (1.0 + sc) + sh


def _norm_proj_kernel(x_ref, g_ref, sh_ref, sc_ref, w_ref, o_ref, w_s):
    @pl.when(pl.program_id(1) == 0)
    def _():
        w_s[...] = w_ref[...].astype(BF16)

    h = _rms_mod(x_ref[...], g_ref[...], sh_ref[0, 0], sc_ref[0, 0]).astype(BF16)
    o_ref[...] = _dot(h, w_s[...])


def _norm_proj(x, gain, mod4, w, layer, n_tok):
    tm = TM
    n_out = w.shape[2]
    tn = n_out // 2
    row = _mod_row(tm)
    return pl.pallas_call(
        _norm_proj_kernel,
        grid=(n_out // tn, n_tok // tm),
        in_specs=[pl.BlockSpec((tm, D_MODEL), lambda j, i: (i, 0)),
                  pl.BlockSpec((1, D_MODEL), lambda j, i: (0, 0)),
                  pl.BlockSpec((1, 1, 1, D_MODEL), lambda j, i: (row(i), 0, 0, 0)),
                  pl.BlockSpec((1, 1, 1, D_MODEL), lambda j, i: (row(i), 1, 0, 0)),
                  pl.BlockSpec((None, D_MODEL, tn), lambda j, i: (layer, 0, j))],
        out_specs=pl.BlockSpec((tm, tn), lambda j, i: (i, j)),
        out_shape=jax.ShapeDtypeStruct((T_ALL, n_out), F32),
        scratch_shapes=[pltpu.VMEM((D_MODEL, tn), BF16)],
        compiler_params=_cparams(("arbitrary", "arbitrary")),
        name="norm_in_proj",
    )(x, gain, mod4, mod4, w)


def _softmax_pv(qh, k_parts, v_parts, masks, sink):
    nt = (((1,), (1,)), ((), ()))
    scores = []
    for kp, mk in zip(k_parts, masks):
        s = lax.dot_general(qh, kp, nt, preferred_element_type=F32)
        if mk is not None:
            s = jnp.where(mk, s, -1e30)
        scores.append(s)
    m = sink
    for s in scores:
        m = jnp.maximum(m, jnp.max(s, axis=-1, keepdims=True))
    es = [jnp.exp(s - m) for s in scores]
    den = jnp.exp(sink - m)
    for e in es:
        den = den + jnp.sum(e, axis=-1, keepdims=True)
    inv = 1.0 / den
    o = None
    for e, vp in zip(es, v_parts):
        t = _dot((e * inv).astype(BF16), vp)
        o = t if o is None else o + t
    return o


def _attn_kernel(sink_ref, q_ref, km_ref, k0_ref, kp_ref, vm_ref, v0_ref, vp_ref, kc_ref, vc_ref,
                 cos_ref, sin_ref, o_ref, *, nb):
    n = pl.program_id(1)
    blk = ATTN_BLOCK
    lane = lax.broadcasted_iota(jnp.int32, (blk, 128), 1)
    first = (lane % 32) < 16

    def rope(x, blk_idx):
        r0 = pl.multiple_of(blk_idx * blk, blk)
        c = cos_ref[pl.ds(r0, blk), :]
        s = sin_ref[pl.ds(r0, blk), :]
        sw = jnp.where(first, pltpu.roll(x, 112, 1), pltpu.roll(x, 16, 1))
        return x * c + sw * s

    nm = jnp.maximum(n - 1, 0)
    npl = jnp.minimum(n + 1, nb - 1)
    kall = jnp.concatenate([rope(km_ref[...], nm), rope(k0_ref[...], n), rope(kp_ref[...], npl), kc_ref[...]],
                           axis=0)
    vall = jnp.concatenate([vm_ref[...], v0_ref[...], vp_ref[...], vc_ref[...]], axis=0)
    nk = kall.shape[0]
    kswap = pltpu.roll(kall, HEAD_DIM, 1)
    vswap = pltpu.roll(vall, HEAD_DIM, 1)
    lo = lax.broadcasted_iota(jnp.int32, (nk, 128), 1) < HEAD_DIM

    r = lax.broadcasted_iota(jnp.int32, (2 * blk, blk), 0) % blk
    cidx = lax.broadcasted_iota(jnp.int32, (2 * blk, blk), 1)
    ok_prev = jnp.where(cidx >= r, (n > 0).astype(jnp.int32), 0) > 0
    ok_next = jnp.where(cidx <= r, (n < nb - 1).astype(jnp.int32), 0) > 0
    top_rows = lax.broadcasted_iota(jnp.int32, (2 * blk, 1), 0) < blk
    neg = jnp.float32(-1e30)

    scale = HEAD_DIM ** -0.5
    q2 = [(rope(q_ref[:, p * 128:(p + 1) * 128], n) * scale).astype(BF16) for p in range(N_Q_HEADS // 2)]
    nt = (((1,), (1,)), ((), ()))
    for h in range(N_KV_HEADS):
        ka, kb = (kall, kswap) if h == 0 else (kswap, kall)
        va, vb = (vall, vswap) if h == 0 else (vswap, vall)
        kbd = jnp.concatenate([jnp.where(lo, ka, 0.0), jnp.where(lo, 0.0, kb)], axis=0).astype(BF16)
        vbd = jnp.concatenate([jnp.where(lo, va, 0.0), jnp.where(lo, 0.0, vb)], axis=0).astype(BF16)
        q4 = jnp.concatenate([q2[2 * h], q2[2 * h + 1]], axis=0)
        s = lax.dot_general(q4, kbd, nt, preferred_element_type=F32)
        probs = []
        for c in range(2):
            base = c * nk
            sink = jnp.where(top_rows, sink_ref[Q_PER_KV * h + c], sink_ref[Q_PER_KV * h + 2 + c])
            parts = [jnp.where(ok_prev, s[:, base:base + blk], neg),
                     s[:, base + blk:base + 2 * blk],
                     jnp.where(ok_next, s[:, base + 2 * blk:base + 3 * blk], neg),
                     s[:, base + 3 * blk:base + nk]]
            m = sink
            for part in parts:
                m = jnp.maximum(m, jnp.max(part, axis=-1, keepdims=True))
            es = [jnp.exp(part - m) for part in parts]
            den = jnp.exp(sink - m)
            for e in es:
                den = den + jnp.sum(e, axis=-1, keepdims=True)
            inv = 1.0 / den
            probs += [(e * inv).astype(BF16) for e in es]
        o = _dot(jnp.concatenate(probs, axis=1), vbd)
        w0 = h * Q_PER_KV * HEAD_DIM
        o_ref[:, w0:w0 + 128] = o[:blk].astype(o_ref.dtype)
        o_ref[:, w0 + 128:w0 + 256] = o[blk:].astype(o_ref.dtype)


def _ctx_attn_kernel(sink_ref, q_ref, kc_ref, vc_ref, o_ref):
    kc = kc_ref[...].astype(BF16)
    vc = vc_ref[...].astype(BF16)
    scale = HEAD_DIM ** -0.5
    outs = []
    for pair in range(N_Q_HEADS // 2):
        q2 = (q_ref[:, pair * 128:(pair + 1) * 128] * scale).astype(BF16)
        for sub in range(2):
            head = 2 * pair + sub
            kvh = head // Q_PER_KV
            sl = slice(kvh * HEAD_DIM, (kvh + 1) * HEAD_DIM)
            qh = q2[:, sub * HEAD_DIM:(sub + 1) * HEAD_DIM]
            outs.append(_softmax_pv(qh, [kc[:, sl]], [vc[:, sl]], [None], sink_ref[head]))
    o_ref[...] = jnp.concatenate(outs, axis=-1).astype(o_ref.dtype)


def _attention(p, sink, cos_t, sin_t, with_ctx):
    blk = ATTN_BLOCK
    nb = SEQ // blk
    qc, kcol, vcol = PQ_OFF // ATTN_WIDTH, PK_OFF // KV_WIDTH, PV_OFF // KV_WIDTH
    ctx_blk = T_LAT // CTX_LEN
    smem = pl.BlockSpec(memory_space=pltpu.SMEM)

    def kv_spec(col, d):
        return pl.BlockSpec((blk, KV_WIDTH),
                            lambda b, n: (b * nb + jnp.clip(n + d, 0, nb - 1), col))

    ya = pl.pallas_call(
        functools.partial(_attn_kernel, nb=nb),
        grid=(BATCH, nb),
        in_specs=[smem,
                  pl.BlockSpec((blk, ATTN_WIDTH), lambda b, n: (b * nb + n, qc)),
                  kv_spec(kcol, -1), kv_spec(kcol, 0), kv_spec(kcol, 1),
                  kv_spec(vcol, -1), kv_spec(vcol, 0), kv_spec(vcol, 1),
                  pl.BlockSpec((CTX_LEN, KV_WIDTH), lambda b, n: (ctx_blk + b, kcol)),
                  pl.BlockSpec((CTX_LEN, KV_WIDTH), lambda b, n: (ctx_blk + b, vcol)),
                  pl.BlockSpec((SEQ, KV_WIDTH), lambda b, n: (0, 0)),
                  pl.BlockSpec((SEQ, KV_WIDTH), lambda b, n: (0, 0))],
        out_specs=pl.BlockSpec((blk, ATTN_WIDTH), lambda b, n: (b * nb + n, 0)),
        out_shape=jax.ShapeDtypeStruct((T_ALL, ATTN_WIDTH), BF16),
        compiler_params=_cparams(("parallel", "parallel")),
        name="banded_attention",
    )(sink, p, p, p, p, p, p, p, p, p, cos_t, sin_t)
    if not with_ctx:
        return ya
    cb = CTX_LEN // blk
    lat_blk = T_LAT // blk

    def alias_kernel(sink_ref, q_ref, kc_ref, vc_ref, ya_in_ref, o_ref):
        del ya_in_ref
        _ctx_attn_kernel(sink_ref, q_ref, kc_ref, vc_ref, o_ref)

    return pl.pallas_call(
        alias_kernel,
        grid=(BATCH, cb),
        in_specs=[smem,
                  pl.BlockSpec((blk, ATTN_WIDTH), lambda b, n: (lat_blk + b * cb + n, qc)),
                  pl.BlockSpec((CTX_LEN, KV_WIDTH), lambda b, n: (ctx_blk + b, kcol)),
                  pl.BlockSpec((CTX_LEN, KV_WIDTH), lambda b, n: (ctx_blk + b, vcol)),
                  pl.BlockSpec(memory_space=pl.ANY)],
        out_specs=pl.BlockSpec((blk, ATTN_WIDTH), lambda b, n: (lat_blk + b * cb + n, 0)),
        out_shape=jax.ShapeDtypeStruct((T_ALL, ATTN_WIDTH), BF16),
        input_output_aliases={4: 0},
        compiler_params=_cparams(("parallel", "parallel")),
        name="context_attention",
    )(sink, p, p, p, ya)


def _rope_tables():
    n_freq = HEAD_DIM // 4
    freqs = ROPE_BASE ** (-jnp.arange(n_freq, dtype=F32) / n_freq)
    t = jnp.arange(SEQ, dtype=jnp.int32)
    rows = (t // GRID_W).astype(F32)[:, None] * freqs
    cols = (t % GRID_W).astype(F32)[:, None] * freqs
    cos_h = jnp.concatenate([jnp.cos(rows), jnp.cos(rows), jnp.cos(cols), jnp.cos(cols)], axis=-1)
    sin_h = jnp.concatenate([-jnp.sin(rows), jnp.sin(rows), -jnp.sin(cols), jnp.sin(cols)], axis=-1)
    return jnp.tile(cos_h, (1, 2)), jnp.tile(sin_h, (1, 2))


def _lead_kernel(*refs, n_in, cmul, epi):
    m_ref = refs[0]
    x_refs = refs[1:1 + n_in]
    pos = 1 + n_in
    xs = []
    for r in x_refs:
        v = r[...]
        xs.append(v.reshape(-1, v.shape[-1]))
    x = xs[0] if n_in == 1 else jnp.concatenate(xs, axis=0)
    if cmul:
        k = refs[pos][...]
        pos += 1
        k = k.reshape(-1, k.shape[-1])
        half = x.shape[0] // 2
        xr, xi, kr, ki = x[:half], x[half:], k[:half], k[half:]
        x = jnp.concatenate([xr * kr - xi * ki, xr * ki + xi * kr], axis=0)
    res = _dot(m_ref[...], x.astype(BF16))
    if epi:
        g_ref, y_ref, b_ref = refs[pos:pos + 3]
        pos += 3
        g = g_ref[...]
        y = y_ref[...]
        res = g.reshape(-1, g.shape[-1]) * (res + y.reshape(-1, y.shape[-1]) * b_ref[...])
    o_ref = refs[pos]
    o_ref[...] = res.reshape(o_ref.shape).astype(o_ref.dtype)


def _lead(mat, xs, x_specs, grid, out_shape, out_spec, *, kspec=None, epi=None, alias_to=None, name):
    ins = [mat] + list(xs)
    specs = [pl.BlockSpec(mat.shape, lambda *a: (0, 0))] + list(x_specs)
    if kspec is not None:
        ins.append(kspec[0])
        specs.append(kspec[1])
    if epi is not None:
        for arr, sp in epi:
            ins.append(arr)
            specs.append(sp)
    kern = functools.partial(_lead_kernel, n_in=len(xs), cmul=kspec is not None, epi=epi is not None)
    aliases = {}
    if alias_to is not None:
        aliases = {len(ins): 0}
        ins.append(alias_to)
        specs.append(pl.BlockSpec(memory_space=pl.ANY))
        inner = kern

        def kern(*refs):
            inner(*refs[:-2], refs[-1])

    return pl.pallas_call(
        kern, grid=grid, in_specs=specs, out_specs=out_spec, out_shape=out_shape,
        input_output_aliases=aliases,
        compiler_params=_cparams(("parallel",) * len(grid)), name=name,
    )(*ins)


def _cstack(xr, xi):
    return jnp.concatenate([jnp.concatenate([xr, xi], axis=1), jnp.concatenate([-xi, xr], axis=1)], axis=0)


def _fnet_kernel(u_ref, mc_ref, ml_ref, ms_ref, o_ref, zr, zi, are, aim):
    n = FN_N
    pitch = PITCH
    gd = FNET_GROUP_DIM
    rows = 4 * n
    mc = mc_ref[...]

    def chan(i, carry):
        r_in = pl.multiple_of(i * rows, rows)
        z = _dot(u_ref[pl.ds(r_in, rows), :].astype(BF16), mc)
        for q in range(rows // n):
            r_out = pl.multiple_of((i * (rows // n) + q) * pitch, 8)
            zr[pl.ds(r_out, n), :] = z[q * n:(q + 1) * n, :gd]
            zi[pl.ds(r_out, n), :] = z[q * n:(q + 1) * n, gd:]
        return carry

    lax.fori_loop(0, SEQ // rows, chan, 0)
    ml = ml_ref[...]

    def lead(i, carry):
        for u in range(UNROLL):
            n1 = i * UNROLL + u
            x = jnp.concatenate([zr[pl.ds(n1, n, stride=pitch), :], zi[pl.ds(n1, n, stride=pitch), :]],
                                axis=0).astype(BF16)
            r = _dot(ml, x)
            are[pl.ds(n1, n, stride=pitch), :] = r[:n]
            aim[pl.ds(n1, n, stride=pitch), :] = r[n:]
        return carry

    lax.fori_loop(0, n // UNROLL, lead, 0)

    def slab(i, carry):
        for u in range(UNROLL):
            k2 = i * UNROLL + u
            r0 = pl.multiple_of(k2 * pitch, 8)
            x = jnp.concatenate([are[pl.ds(r0, n), :], aim[pl.ds(r0, n), :]], axis=0).astype(BF16)
            o_ref[pl.ds(k2, n, stride=n), :] = _dot(ms_ref[k2], x)
        return carry

    lax.fori_loop(0, n // UNROLL, slab, 0)


def _fnet_ctx_kernel(u_ref, mc_ref, mf_ref, yf_in_ref, o_ref):
    del yf_in_ref
    gd = FNET_GROUP_DIM
    mc = mc_ref[...]
    mf = mf_ref[...]
    for g in range(FNET_GROUPS):
        z = _dot(u_ref[:, g * gd:(g + 1) * gd].astype(BF16), mc)
        x = jnp.concatenate([z[:, :gd], z[:, gd:]], axis=0).astype(BF16)
        o_ref[:, g * gd:(g + 1) * gd] = _dot(mf, x)


def _fourier_mix(p, mats, with_ctx):
    gd = FNET_GROUP_DIM
    col0 = PF_OFF // gd
    full = lambda a: pl.BlockSpec(a.shape, lambda *i: (0,) * a.ndim)
    scr = pltpu.VMEM((FN_N * PITCH, gd), F32)
    yf = pl.pallas_call(
        _fnet_kernel,
        grid=(BATCH, FNET_GROUPS),
        in_specs=[pl.BlockSpec((SEQ, gd), lambda b, g: (b, col0 + g)),
                  full(mats['fn_chan']), full(mats['fn_lead']), full(mats['fn_slab'])],
        out_specs=pl.BlockSpec((SEQ, gd), lambda b, g: (b, g)),
        out_shape=jax.ShapeDtypeStruct((T_ALL if with_ctx else T_LAT, FNET_WIDTH), F32),
        scratch_shapes=[scr, scr, scr, scr],
        compiler_params=_cparams(("parallel", "parallel")),
        name="fnet_latent",
    )(p, mats['fn_chan'], mats['fn_lead'], mats['fn_slab'])
    if not with_ctx:
        return yf
    blk0 = T_LAT // CTX_LEN
    return pl.pallas_call(
        _fnet_ctx_kernel,
        grid=(BATCH,),
        in_specs=[pl.BlockSpec((pl.Element(CTX_LEN), pl.Element(FNET_WIDTH)),
                               lambda b: ((blk0 + b) * CTX_LEN, PF_OFF)),
                  full(mats['fn_chan']), full(mats['fc']), pl.BlockSpec(memory_space=pl.ANY)],
        out_specs=pl.BlockSpec((CTX_LEN, FNET_WIDTH), lambda b: (blk0 + b, 0)),
        out_shape=jax.ShapeDtypeStruct((T_ALL, FNET_WIDTH), F32),
        input_output_aliases={3: 0},
        compiler_params=_cparams(("parallel",)),
        name="fnet_ctx",
    )(p, mats['fn_chan'], mats['fc'], yf)


def _short_conv_kernel(u_ref, w_ref, b_ref, o_ref, *, rows, chunk, padded):
    w0 = w_ref[0:1, :]
    w1 = w_ref[1:2, :]
    w2 = w_ref[2:3, :]
    bias = b_ref[...]
    width = u_ref.shape[-1]
    ridx = lax.broadcasted_iota(jnp.int32, (chunk, width), 0)
    n_chunks = rows // chunk
    for ci in range(n_chunks):
        r0 = ci * chunk
        cur = u_ref[r0:r0 + chunk, :]
        if ci > 0:
            prev_row = u_ref[r0 - 8:r0, :][7:8, :]
        else:
            prev_row = jnp.zeros((1, width), F32)
        if ci < n_chunks - 1:
            next_row = u_ref[r0 + chunk:r0 + chunk + 8, :][0:1, :]
        else:
            next_row = jnp.zeros((1, width), F32)
        up = jnp.where(ridx == 0, prev_row, pltpu.roll(cur, 1, 0))
        dn = jnp.where(ridx == chunk - 1, next_row, pltpu.roll(cur, chunk - 1, 0))
        res = up * w0 + cur * w1 + dn * w2 + bias
        if not padded:
            o_ref[0, r0:r0 + chunk, :] = res
            continue
        for q in range(chunk // HY_N2):
            p0 = (r0 // HY_N2 + q) * PITCH
            o_ref[0, p0:p0 + HY_N2, :] = res[q * HY_N2:(q + 1) * HY_N2]
            o_ref[0, p0 + HY_N2:p0 + PITCH, :] = jnp.zeros((PITCH - HY_N2, width), F32)


def _short_conv(p, conv_w, conv_b, with_ctx):
    cw = 256
    hw = (HYENA_ORDER + 1) * HYENA_WIDTH
    ncol = hw // cw
    per = HYENA_WIDTH // cw
    col0 = PH_OFF // cw
    out_shape = jax.ShapeDtypeStruct((HYENA_ORDER + 1, T_PAD if with_ctx else BATCH * SEQ_PAD, HYENA_WIDTH), F32)
    b2 = conv_b.reshape(1, hw)

    def call(rows, blk0, alias):
        padded = rows == SEQ
        out_rows = SEQ_PAD if padded else rows
        out_blk0 = 0 if padded else BATCH * SEQ_PAD // rows
        kern = functools.partial(_short_conv_kernel, rows=rows, chunk=min(rows, 256), padded=padded)
        ins = [p, conv_w, b2]
        specs = [pl.BlockSpec((rows, cw), lambda b, j: (blk0 + b, col0 + j)),
                 pl.BlockSpec((3, cw), lambda b, j: (0, j)),
                 pl.BlockSpec((1, cw), lambda b, j: (0, j))]
        aliases = {}
        if alias is not None:
            ins.append(alias)
            specs.append(pl.BlockSpec(memory_space=pl.ANY))
            aliases = {3: 0}
            inner = kern

            def kern(u_ref, w_ref, b_ref, a_ref, o_ref):
                del a_ref
                inner(u_ref, w_ref, b_ref, o_ref)

        return pl.pallas_call(
            kern, grid=(BATCH, ncol), in_specs=specs,
            out_specs=pl.BlockSpec((1, out_rows, cw), lambda b, j: (j // per, out_blk0 + b, j % per)),
            out_shape=out_shape, input_output_aliases=aliases,
            compiler_params=_cparams(("parallel", "parallel")), name="hyena_short_conv",
        )(*ins)

    z3 = call(SEQ, 0, None)
    if with_ctx:
        z3 = call(CTX_LEN, T_LAT // CTX_LEN, z3)
    return z3


def _filter_mlp_kernel(ft_ref, w1_ref, b1_ref, fq_ref, w2_ref, b2_ref, o_ref):
    fq = fq_ref[...]
    h = jnp.sin(fq * (_dot(ft_ref[...].astype(BF16), w1_ref[...]) + b1_ref[...]))
    h = jnp.sin(fq * (_dot(h.astype(BF16), w2_ref[...]) + b2_ref[...]))
    o_ref[...] = h.astype(o_ref.dtype)


def _filter_kernel(h_ref, t_ref, w3f_ref, w3b_ref, dl_ref, m1_ref, *rest, n, dense):
    hb = h_ref[...]
    decay = jnp.exp(-t_ref[...] * dl_ref[...])
    tf = _dot(hb, w3f_ref[...]) * decay
    tb = _dot(hb, w3b_ref[...]) * decay
    tb = jnp.where(lax.broadcasted_iota(jnp.int32, tb.shape, 0) == 0, 0.0, tb)
    scale = 1.0 / (jnp.sum(jnp.abs(tf), axis=0, keepdims=True) + jnp.sum(jnp.abs(tb), axis=0, keepdims=True))
    cw = tf.shape[1]
    if dense:
        o_ref = rest[0]
        r = _dot(m1_ref[...], jnp.concatenate([tf, tb], axis=1).astype(BF16))
        nc = r.shape[0] // 2
        o_ref[0:nc, :] = (r[:nc, :cw] + r[:nc, cw:]) * scale
        o_ref[nc:, :] = (r[nc:, :cw] - r[nc:, cw:]) * scale
        return
    f2_ref, o_ref, tf_s, tb_s, are_f, aim_f, are_b, aim_b = rest
    tf_s[...] = tf
    tb_s[...] = tb
    half = HY_N1 // 2
    m1 = m1_ref[...]

    def stage1(i, carry):
        for u in range(UNROLL):
            n2 = i * UNROLL + u
            x = jnp.concatenate([tf_s[pl.ds(n2, half, stride=HY_N2), :], tb_s[pl.ds(n2, half, stride=HY_N2), :]],
                                axis=1).astype(BF16)
            r = _dot(m1, x)
            rows = pl.ds(n2, HY_N1, stride=PITCH)
            are_f[rows, :] = r[:HY_N1, :cw]
            are_b[rows, :] = r[:HY_N1, cw:]
            aim_f[rows, :] = r[HY_N1:, :cw]
            aim_b[rows, :] = r[HY_N1:, cw:]
        return carry

    lax.fori_loop(0, HY_N2 // UNROLL, stage1, 0)

    def stage2(i, carry):
        for u in range(UNROLL):
            k1 = i * UNROLL + u
            rows = pl.ds(pl.multiple_of(k1 * PITCH, 8), HY_N2)
            ar = jnp.concatenate([are_f[rows, :], are_b[rows, :]], axis=1)
            ai = jnp.concatenate([aim_f[rows, :], aim_b[rows, :]], axis=1)
            t = _dot(f2_ref[k1], _cstack(ar, ai).astype(BF16))
            o_ref[0, k1] = (t[:, 0:cw] + t[:, cw:2 * cw]) * scale
            o_ref[1, k1] = (t[:, 2 * cw:3 * cw] - t[:, 3 * cw:4 * cw]) * scale
        return carry

    lax.fori_loop(0, HY_N1 // UNROLL, stage2, 0)


def _filter_feats(n):
    pos = jnp.arange(n, dtype=F32)
    t = pos / max(n - 1, 1)
    omega = 2.0 * math.pi * pos / n
    bands = jnp.linspace(1e-4, FILTER_BANDS - 1, FILTER_BANDS, dtype=F32)
    feats = jnp.concatenate([t[:, None], jnp.cos(omega[:, None] * bands), -jnp.sin(omega[:, None] * bands)], axis=-1)
    return jnp.pad(feats, ((0, 0), (0, 128 - FILTER_EMB))), t[:, None]


def _filter_spectrum(n, filt, mats):
    w1, b1, freq, w2, b2, w3 = filt
    dense = n == CTX_LEN
    cw = 128
    nch = HYENA_WIDTH // cw
    feats, t = _filter_feats(n)
    w1p = jnp.pad(w1, ((0, 128 - FILTER_EMB), (0, 0))).astype(BF16)
    deltas = jnp.abs(jnp.linspace(math.log(DECAY_TARGET) / SLOW_DECAY_PCT, math.log(DECAY_TARGET) / FAST_DECAY_PCT,
                                  HYENA_WIDTH, dtype=F32)).reshape(1, HYENA_WIDTH)
    full = lambda a: pl.BlockSpec(a.shape, lambda *i: (0,) * a.ndim)
    row = lambda a: a.reshape(1, -1)
    w3b16 = w3.astype(BF16)
    tap_spec = lambda d: pl.BlockSpec((FILTER_HIDDEN, cw), lambda o, ch: (0, (o * 2 + d) * nch + ch))
    m1 = mats['hc_kh'] if dense else mats['hy_lead_kh']
    mlp_ins = [feats, w1p, row(b1), row(freq), w2.astype(BF16), row(b2)]
    hb = pl.pallas_call(
        _filter_mlp_kernel, grid=(1,), in_specs=[full(a) for a in mlp_ins],
        out_specs=pl.BlockSpec((n, FILTER_HIDDEN), lambda i: (0, 0)),
        out_shape=jax.ShapeDtypeStruct((n, FILTER_HIDDEN), BF16),
        compiler_params=_cparams(("arbitrary",)), name="hyena_filter_mlp",
    )(*mlp_ins)
    ins = [hb, t, w3b16, w3b16, deltas, m1]
    specs = [full(hb), full(t), tap_spec(0), tap_spec(1), pl.BlockSpec((1, cw), lambda o, ch: (0, ch)), full(m1)]
    if dense:
        nc = 2 * n
        out_shape = jax.ShapeDtypeStruct((HYENA_ORDER, 2 * nc, HYENA_WIDTH), F32)
        out_spec = pl.BlockSpec((None, 2 * nc, cw), lambda o, ch: (o, 0, ch))
        scratch = []
    else:
        ins.append(mats['hy_slab_f2'])
        specs.append(full(mats['hy_slab_f2']))
        out_shape = jax.ShapeDtypeStruct((HYENA_ORDER, 2, HY_N1, HY_N2, HYENA_WIDTH), F32)
        out_spec = pl.BlockSpec((None, 2, HY_N1, HY_N2, cw), lambda o, ch: (o, 0, 0, 0, ch))
        scratch = [pltpu.VMEM((n, cw), F32)] * 2 + [pltpu.VMEM((HY_N1 * PITCH, cw), F32)] * 4
    return pl.pallas_call(
        functools.partial(_filter_kernel, n=n, dense=dense),
        grid=(HYENA_ORDER, nch), in_specs=specs, out_specs=out_spec, out_shape=out_shape,
        scratch_shapes=scratch,
        compiler_params=_cparams(("parallel", "parallel")),
        name="hyena_filter_ctx" if dense else "hyena_filter",
    )(*ins)


def _hyena_conv_kernel(y_ref, g_ref, k_ref, m1_ref, f2f_ref, f2i_ref, m3_ref, b_ref, o_ref, are, aim, *, out_pitch):
    half = HY_N1 // 2
    out_seq = half * out_pitch
    m1 = m1_ref[...]

    def stage1(i, carry):
        for u in range(UNROLL):
            n2 = i * UNROLL + u
            x = jnp.concatenate([y_ref[pl.ds(n2, half, stride=PITCH), :],
                                 y_ref[pl.ds(SEQ_PAD + n2, half, stride=PITCH), :]], axis=0).astype(BF16)
            r = _dot(m1, x)
            are[pl.ds(n2, HY_N1, stride=PITCH), :] = r[:HY_N1]
            aim[pl.ds(n2, HY_N1, stride=PITCH), :] = r[HY_N1:]
        return carry

    lax.fori_loop(0, HY_N2 // UNROLL, stage1, 0)
    cw = o_ref.shape[-1]

    def stage2(i, carry):
        for u in range(UNROLL):
            k1 = i * UNROLL + u
            r0 = pl.multiple_of(k1 * PITCH, 8)
            y = _dot(f2f_ref[k1], _cstack(are[pl.ds(r0, HY_N2), :], aim[pl.ds(r0, HY_N2), :]).astype(BF16))
            yr, yi = y[:, :cw], y[:, cw:]
            kr, ki = k_ref[0, k1], k_ref[1, k1]
            w = _dot(f2i_ref[k1], _cstack(yr * kr - yi * ki, yr * ki + yi * kr).astype(BF16))
            are[pl.ds(r0, HY_N2), :] = w[:, :cw]
            aim[pl.ds(r0, HY_N2), :] = w[:, cw:]
        return carry

    lax.fori_loop(0, HY_N1 // UNROLL, stage2, 0)
    m3 = m3_ref[...]
    bias = b_ref[...]

    def stage3(i, carry):
        for u in range(UNROLL):
            n2 = i * UNROLL + u
            bn = jnp.concatenate([are[pl.ds(n2, HY_N1, stride=PITCH), :], aim[pl.ds(n2, HY_N1, stride=PITCH), :]],
                                 axis=0).astype(BF16)
            y = _dot(m3, bn)
            for b in range(2):
                rows = pl.ds(b * SEQ_PAD + n2, half, stride=PITCH)
                o_ref[pl.ds(b * out_seq + n2, half, stride=out_pitch), :] = (
                    g_ref[rows, :] * (y[b * half:(b + 1) * half] + y_ref[rows, :] * bias))
        return carry

    lax.fori_loop(0, HY_N2 // UNROLL, stage3, 0)


def _hyena_mix(z3, filt, hyena_bias, mats, with_ctx):
    c = HYENA_WIDTH
    cw = 128
    nch = c // cw
    pairs = BATCH // 2
    full = lambda a: pl.BlockSpec(a.shape, lambda *i: (0,) * a.ndim)
    kspec = _filter_spectrum(SEQ, filt, mats)
    if with_ctx:
        kspec_c = _filter_spectrum(CTX_LEN, filt, mats)
    bias3 = hyena_bias.reshape(HYENA_ORDER, 1, c)
    scr = pltpu.VMEM((HY_N1 * PITCH, cw), F32)
    y = None
    for o in range(HYENA_ORDER):
        final = o == HYENA_ORDER - 1
        out_pitch = HY_N2 if final else PITCH
        out_seq = SEQ if final else SEQ_PAD
        if final:
            out_rows = T_ALL if with_ctx else T_LAT
        else:
            out_rows = T_PAD if with_ctx else BATCH * SEQ_PAD
        if y is None:
            xin, xspec = z3, pl.BlockSpec((None, 2 * SEQ_PAD, cw), lambda b, ch: (0, b, ch))
        else:
            xin, xspec = y, pl.BlockSpec((2 * SEQ_PAD, cw), lambda b, ch: (b, ch))
        ynew = pl.pallas_call(
            functools.partial(_hyena_conv_kernel, out_pitch=out_pitch),
            grid=(pairs, nch),
            in_specs=[xspec,
                      pl.BlockSpec((None, 2 * SEQ_PAD, cw), lambda b, ch, o=o: (o + 1, b, ch)),
                      pl.BlockSpec((None, 2, HY_N1, HY_N2, cw), lambda b, ch, o=o: (o, 0, 0, 0, ch)),
                      full(mats['hy_lead_f']), full(mats['hy_slab_f2']), full(mats['hy_slab_i2']),
                      full(mats['hy_lead_i']),
                      pl.BlockSpec((None, 1, cw), lambda b, ch, o=o: (o, 0, ch))],
            out_specs=pl.BlockSpec((2 * out_seq, cw), lambda b, ch: (b, ch)),
            out_shape=jax.ShapeDtypeStruct((out_rows, c), F32),
            scratch_shapes=[scr, scr],
            compiler_params=_cparams(("parallel", "parallel"), 58 * 1024 * 1024),
            name="hyena_conv",
        )(xin, z3, kspec, mats['hy_lead_f'], mats['hy_slab_f2'], mats['hy_slab_i2'], mats['hy_lead_i'], bias3)
        if with_ctx:
            nc = 2 * CTX_LEN
            blk0 = BATCH * SEQ_PAD // nc
            oblk0 = BATCH * out_seq // nc
            if y is None:
                cin, cspec_in = z3, pl.BlockSpec((1, nc, c), lambda b: (0, blk0 + b, 0))
            else:
                cin, cspec_in = y, pl.BlockSpec((nc, c), lambda b: (blk0 + b, 0))
            xc = _lead(mats['hc_f'], [cin], [cspec_in], (pairs,),
                       jax.ShapeDtypeStruct((pairs, 2 * nc, c), F32),
                       pl.BlockSpec((1, 2 * nc, c), lambda b: (b, 0, 0)), name="hyena_ctx_fwd")
            ynew = _lead(mats['hc_i'], [xc], [pl.BlockSpec((1, 2 * nc, c), lambda b: (b, 0, 0))],
                         (pairs,), jax.ShapeDtypeStruct((out_rows, c), F32),
                         pl.BlockSpec((nc, c), lambda b, oblk0=oblk0: (oblk0 + b, 0)),
                         kspec=(kspec_c, pl.BlockSpec((1, 2 * nc, c), lambda b, o=o: (o, 0, 0))),
                         epi=[(z3, pl.BlockSpec((1, nc, c), lambda b, o=o: (o + 1, blk0 + b, 0))),
                              (cin, cspec_in),
                              (bias3, pl.BlockSpec((1, 1, c), lambda b, o=o: (o, 0, 0)))],
                         alias_to=ynew, name="hyena_ctx_inv")
        y = ynew
    return y


def _route_tile(lt, br, base, tri):
    tm = lt.shape[1]
    aff = jax.nn.sigmoid(lt)
    biased = aff + br
    b = [biased[e:e + 1, :] for e in range(N_EXPERTS)]
    a = [aff[e:e + 1, :] for e in range(N_EXPERTS)]
    epg = EXPERTS_PER_GROUP
    scores = []
    for g in range(N_GROUPS):
        x0, x1, x2, x3 = b[epg * g:epg * g + epg]
        s1, t1 = jnp.maximum(x0, x1), jnp.minimum(x0, x1)
        s2, t2 = jnp.maximum(x2, x3), jnp.minimum(x2, x3)
        scores.append(jnp.maximum(s1, s2) + jnp.maximum(jnp.minimum(s1, s2), jnp.maximum(t1, t2)))
    best = scores[0]
    gsel = jnp.zeros((1, tm), jnp.int32)
    for g in range(1, N_GROUPS):
        gsel = jnp.where(scores[g] > best, g, gsel)
        best = jnp.maximum(best, scores[g])

    def pick(rows, j):
        out = rows[j]
        for g in range(1, N_GROUPS):
            out = jnp.where(gsel == g, rows[epg * g + j], out)
        return out

    v = [pick(b, j) for j in range(epg)]
    av = [pick(a, j) for j in range(epg)]
    i1 = jnp.zeros((1, tm), jnp.int32)
    m1 = v[0]
    for j in range(1, epg):
        i1 = jnp.where(v[j] > m1, j, i1)
        m1 = jnp.maximum(m1, v[j])
    neg = jnp.float32(-3.0e38)
    i2 = jnp.zeros((1, tm), jnp.int32)
    m2 = jnp.full((1, tm), neg, F32)
    for j in range(epg):
        cand = jnp.where(i1 == j, neg, v[j])
        take = cand > m2
        i2 = jnp.where(take, j, i2)
        m2 = jnp.where(take, cand, m2)

    def sel(rows, idx):
        out = rows[0]
        for j in range(1, epg):
            out = jnp.where(idx == j, rows[j], out)
        return out

    a1, a2 = sel(av, i1), sel(av, i2)
    den = a1 + a2
    e1 = gsel * epg + i1
    e2 = gsel * epg + i2
    eio = lax.broadcasted_iota(jnp.int32, (N_EXPERTS, tm), 0)
    oh1 = jnp.where(eio == e1, 1.0, 0.0)
    oh2 = jnp.where(eio == e2, 1.0, 0.0)
    oh = oh1 + oh2
    tot = base + _dot(oh.astype(BF16), tri)
    r1 = jnp.sum(oh1 * tot, axis=0, keepdims=True)
    r2 = jnp.sum(oh2 * tot, axis=0, keepdims=True)
    new_base = base + jnp.sum(oh, axis=1, keepdims=True)
    return (e1, e2), (a1 / den, a2 / den), (r1.astype(jnp.int32), r2.astype(jnp.int32)), new_base


def _merge_kernel(x_ref, ya_ref, yf_ref, yh_ref, gt_ref, wa_ref, wf_ref, wh_ref, wo_ref, g1_ref,
                  gn_ref, sh_ref, sc_ref, wrt_ref, br_ref, xo_ref, h2_ref, e_ref, w_ref, r_ref, cnt_ref):
    d = D_MODEL
    merged = jax.nn.sigmoid(gt_ref[:, 0:d]) * _dot(ya_ref[...], wa_ref[...])
    merged += jax.nn.sigmoid(gt_ref[:, d:2 * d]) * _dot(yf_ref[...].astype(BF16), wf_ref[...])
    merged += jax.nn.sigmoid(gt_ref[:, 2 * d:3 * d]) * _dot(yh_ref[...].astype(BF16), wh_ref[...])
    xn = x_ref[...] + g1_ref[0, 0] * _dot(merged.astype(BF16), wo_ref[...])
    xo_ref[...] = xn
    h2f = _rms_mod(xn, gn_ref[...], sh_ref[0, 0], sc_ref[0, 0])
    _rows_to_tiles(h2_ref, h2f, 0, h2f.shape[0])
    h2 = h2f.astype(BF16)

    @pl.when(pl.program_id(0) == 0)
    def _():
        cnt_ref[...] = jnp.zeros_like(cnt_ref)

    tm = h2.shape[0]
    lt = lax.dot_general(wrt_ref[...], h2, (((1,), (1,)), ((), ())), preferred_element_type=F32)
    tri = jnp.where(lax.broadcasted_iota(jnp.int32, (tm, tm), 0) < lax.broadcasted_iota(jnp.int32, (tm, tm), 1),
                    1.0, 0.0).astype(BF16)
    es, ws, rs, new_base = _route_tile(lt, br_ref[...], cnt_ref[:, 0:1], tri)
    e_ref[0:1, :], e_ref[1:2, :] = es
    w_ref[0:1, :], w_ref[1:2, :] = ws
    r_ref[0:1, :], r_ref[1:2, :] = rs
    cnt_ref[...] = jnp.broadcast_to(new_base, cnt_ref.shape)


def _merge(x, ya, yf, yh, p, wa, wf, wh, wo, mod4, gain2, wrt, br, n_tok):
    tm = TM
    row = _mod_row(tm)
    full = lambda a: pl.BlockSpec(a.shape, lambda i: (0,) * a.ndim)
    modspec = lambda k: pl.BlockSpec((1, 1, 1, D_MODEL), lambda i: (row(i), k, 0, 0))
    tok = lambda w: pl.BlockSpec((tm, w), lambda i: (i, 0))
    lane = pl.BlockSpec((TOP_K, tm), lambda i: (0, i))
    return pl.pallas_call(
        _merge_kernel,
        grid=(n_tok // tm,),
        in_specs=[tok(D_MODEL), tok(ATTN_WIDTH), tok(FNET_WIDTH), tok(HYENA_WIDTH),
                  pl.BlockSpec((pl.Element(tm), pl.Element(3 * D_MODEL)), lambda i: (i * tm, PG_OFF)),
                  full(wa), full(wf), full(wh), full(wo), modspec(2), full(gain2), modspec(3), modspec(4),
                  full(wrt), full(br)],
        out_specs=[tok(D_MODEL), pl.BlockSpec((tm * ROW_TILE, 128), lambda i: (i, 0)), lane, lane, lane,
                   pl.BlockSpec((N_EXPERTS, 128), lambda i: (0, 0))],
        out_shape=[jax.ShapeDtypeStruct((n_tok, D_MODEL), F32),
                   jax.ShapeDtypeStruct((n_tok * ROW_TILE, 128), F32),
                   jax.ShapeDtypeStruct((TOP_K, n_tok), jnp.int32),
                   jax.ShapeDtypeStruct((TOP_K, n_tok), F32),
                   jax.ShapeDtypeStruct((TOP_K, n_tok), jnp.int32),
                   jax.ShapeDtypeStruct((N_EXPERTS, 128), F32)],
        compiler_params=_cparams(("arbitrary",)),
        name="merge_out_norm_route",
    )(x, ya, yf, yh, p, wa, wf, wh, wo, mod4, gain2, mod4, mod4, wrt, br)


def _row_copy(src_ref, row, dst_ref, r, sem):
    return pltpu.make_async_copy(src_ref.at[pl.ds(row, 1), :], dst_ref.at[pl.ds(r, 1), :], sem)


ROW_TILE = D_MODEL // 128


def _rows_from_tiles(ref, n):
    return jnp.concatenate([ref[pl.ds(s, n, stride=ROW_TILE), :] for s in range(ROW_TILE)], axis=1)


def _rows_to_tiles(ref, val, col0, n):
    for j in range(val.shape[1] // 128):
        ref[pl.ds(col0 // 128 + j, n, stride=ROW_TILE), :] = val[:, j * 128:(j + 1) * 128]


def _expert_kernel(be_ref, na_ref, rt_cur_ref, rt_next_ref, h2_ref, wg_ref, wu_ref, wd_ref, o_ref,
                   xbuf, sem, wg_s, wu_s, wd_s):
    i = pl.program_id(0)
    n_act = na_ref[0]
    bm = xbuf.shape[1]
    active = i < n_act

    def gather(rt_ref, slot):
        def body(j, carry):
            r0 = pl.multiple_of(j * 8, 8)
            tile = xbuf.at[slot, pl.ds(r0, 8)]
            for u in range(8):
                _row_copy(h2_ref, rt_ref[0, 0, r0 + u], tile, u, sem.at[slot]).start()
            return carry

        lax.fori_loop(0, bm // 8, body, 0)

    @pl.when((i == 0) & active)
    def _():
        gather(rt_cur_ref, 0)

    @pl.when(i + 1 < n_act)
    def _():
        gather(rt_next_ref, (i + 1) % 2)

    prev = be_ref[jnp.maximum(i - 1, 0)]

    @pl.when(active & ((i == 0) | (be_ref[i] != prev)))
    def _():
        wg_s[...] = wg_ref[0].astype(BF16)
        wu_s[...] = wu_ref[0].astype(BF16)
        wd_s[...] = wd_ref[0].astype(BF16)

    @pl.when(active)
    def _():
        slot = i % 2
        pltpu.make_async_copy(h2_ref.at[pl.ds(0, bm), :], xbuf.at[slot], sem.at[slot]).wait()
        x = xbuf[slot].astype(BF16)
        g = _dot(x, wg_s[...])
        u = _dot(x, wu_s[...])
        h = (g * jax.nn.sigmoid(g)) * u
        o_ref[...] = _dot(h.astype(BF16), wd_s[...])

    @pl.when(jnp.logical_not(active))
    def _():
        o_ref[...] = jnp.zeros_like(o_ref)


def _experts(h2, row_tok, blk_expert, n_active, wg, wu, wd):
    bm = EXPERT_BM
    n_blk = row_tok.shape[0] // bm
    rt = row_tok.reshape(n_blk, 1, bm)
    wspec = lambda k, n: pl.BlockSpec((1, k, n), lambda i, be, na: (be[i], 0, 0))
    return pl.pallas_call(
        _expert_kernel,
        grid_spec=pltpu.PrefetchScalarGridSpec(
            num_scalar_prefetch=2,
            grid=(n_blk,),
            in_specs=[pl.BlockSpec((1, 1, bm), lambda i, be, na: (i, 0, 0), memory_space=pltpu.SMEM),
                      pl.BlockSpec((1, 1, bm), lambda i, be, na: (jnp.minimum(i + 1, n_blk - 1), 0, 0),
                                   memory_space=pltpu.SMEM),
                      pl.BlockSpec(memory_space=pl.ANY),
                      wspec(D_MODEL, EXPERT_FF), wspec(D_MODEL, EXPERT_FF), wspec(EXPERT_FF, D_MODEL)],
            out_specs=pl.BlockSpec((bm, D_MODEL), lambda i, be, na: (i, 0)),
            scratch_shapes=[pltpu.VMEM((2, bm, D_MODEL), F32), pltpu.SemaphoreType.DMA((2,)),
                            pltpu.VMEM((D_MODEL, EXPERT_FF), BF16), pltpu.VMEM((D_MODEL, EXPERT_FF), BF16),
                            pltpu.VMEM((EXPERT_FF, D_MODEL), BF16)]),
        out_shape=jax.ShapeDtypeStruct((row_tok.shape[0], D_MODEL), F32),
        compiler_params=_cparams(("arbitrary",)),
        name="moe_experts",
    )(blk_expert, n_active, rt, rt, h2, wg, wu, wd)


def _dispatch(e_idx, rank, counts, n_tok):
    bm = EXPERT_BM
    counts = counts.astype(jnp.int32)
    padded = (counts + bm - 1) // bm * bm
    pad_end = jnp.cumsum(padded)
    pad_start = pad_end - padded
    experts = jnp.arange(N_EXPERTS, dtype=jnp.int32)
    start = jnp.sum(jnp.where(e_idx[..., None] == experts, pad_start, 0), axis=-1)
    dest = start + rank
    n_rows = -(-(n_tok * TOP_K) // bm) * bm + N_EXPERTS * bm
    n_blk = n_rows // bm
    tok = jnp.tile(jnp.arange(n_tok, dtype=jnp.int32), TOP_K)
    row_tok = jnp.zeros((n_rows,), jnp.int32).at[dest.reshape(-1)].set(tok)
    blk_start = jnp.arange(n_blk, dtype=jnp.int32) * bm
    blk_expert = jnp.minimum(jnp.sum((blk_start[:, None] >= pad_end[None, :]).astype(jnp.int32), axis=-1),
                             N_EXPERTS - 1)
    n_active = (pad_end[-1] // bm).astype(jnp.int32).reshape(1)
    return row_tok, dest, blk_expert, n_active


def _combine_kernel(d_cur_ref, d_next_ref, x_ref, w_ref, g2_ref, gf_ref, ys_ref, o_ref, ybuf, sem,
                    *, final, n_tiles):
    i = pl.program_id(0)
    tm = x_ref.shape[0]

    def gather(d_ref, slot):
        def body(j, carry):
            r0 = pl.multiple_of(j * 8, 8)
            for k in range(TOP_K):
                tile = ybuf.at[slot, k, pl.ds(r0, 8)]
                for u in range(8):
                    _row_copy(ys_ref, d_ref[0, k, r0 + u], tile, u, sem.at[slot]).start()
            return carry

        lax.fori_loop(0, tm // 8, body, 0)

    @pl.when(i == 0)
    def _():
        gather(d_cur_ref, 0)

    @pl.when(i + 1 < n_tiles)
    def _():
        gather(d_next_ref, (i + 1) % 2)

    slot = i % 2
    for k in range(TOP_K):
        pltpu.make_async_copy(ys_ref.at[pl.ds(0, tm), :], ybuf.at[slot, k], sem.at[slot]).wait()
    w = w_ref[...]
    moe = ybuf[slot, 0] * w[:, 0:1] + ybuf[slot, 1] * w[:, 1:2]
    xn = x_ref[...] + g2_ref[0, 0] * moe
    if final:
        y = xn * lax.rsqrt(jnp.mean(xn * xn, axis=-1, keepdims=True) + EPS)
        xn = y * gf_ref[...]
    o_ref[...] = xn


def _combine(x, ys, dest, w_sel, mod4, gain_final, n_tok, final):
    tm = 256
    n_tiles = n_tok // tm
    row = _mod_row(tm)
    tok = lambda w: pl.BlockSpec((tm, w), lambda i: (i, 0))
    d3 = jnp.transpose(dest.reshape(TOP_K, n_tiles, tm), (1, 0, 2))
    return pl.pallas_call(
        functools.partial(_combine_kernel, final=final, n_tiles=n_tiles),
        grid=(n_tiles,),
        in_specs=[pl.BlockSpec((1, TOP_K, tm), lambda i: (i, 0, 0), memory_space=pltpu.SMEM),
                  pl.BlockSpec((1, TOP_K, tm), lambda i: (jnp.minimum(i + 1, n_tiles - 1), 0, 0),
                               memory_space=pltpu.SMEM),
                  tok(D_MODEL), tok(TOP_K),
                  pl.BlockSpec((1, 1, 1, D_MODEL), lambda i: (row(i), 5, 0, 0)),
                  pl.BlockSpec((1, D_MODEL), lambda i: (0, 0)),
                  pl.BlockSpec(memory_space=pl.ANY)],
        out_specs=tok(D_MODEL),
        out_shape=jax.ShapeDtypeStruct((n_tok if final else T_ALL, D_MODEL), F32),
        scratch_shapes=[pltpu.VMEM((2, TOP_K, tm, D_MODEL), F32), pltpu.SemaphoreType.DMA((2,))],
        compiler_params=_cparams(("arbitrary",)),
        name="moe_combine",
    )(d3, d3, x, w_sel, mod4, gain_final, ys)


def _moe_kernel(be_ref, na_ref, ip_ref, ic_ref, sc_ref, sn_ref, h2_ref, wg_ref, wu_ref, wd_ref, y_ref,
                xbuf, obuf, gsem, ssem, wg_s, wu_s, wd_s):
    i = pl.program_id(0)
    n_act = na_ref[0]
    sub = ROW_TILE
    bm = xbuf.shape[1] // sub
    active = i < n_act
    slot = i % 2
    other = (i + 1) % 2
    n_chunk = 4
    cw = EXPERT_FF // n_chunk
    rows_per = bm // n_chunk

    def tile_rows(ref, row):
        return ref.at[pl.ds(pl.multiple_of(row * sub, sub), sub), :]

    def gather(src_ref, dst_slot, lo, hi):
        for r in range(lo, hi):
            pltpu.make_async_copy(tile_rows(h2_ref, src_ref[0, 0, r]), xbuf.at[dst_slot, pl.ds(r * sub, sub), :],
                                  gsem.at[dst_slot]).start(priority=r % 2)

    def scatter(info_ref, src_slot, lo, hi):
        for r in range(lo, hi):
            pltpu.make_async_copy(obuf.at[src_slot, pl.ds(r * sub, sub), :], tile_rows(y_ref, info_ref[0, 0, r]),
                                  ssem.at[src_slot]).start(priority=r % 2)

    def wait_gather(s):
        pltpu.make_async_copy(h2_ref.at[pl.ds(0, bm * sub), :], xbuf.at[s], gsem.at[s]).wait()

    def wait_scatter(s):
        pltpu.make_async_copy(obuf.at[s], y_ref.at[pl.ds(0, bm * sub), :], ssem.at[s]).wait()

    @pl.when((i == 0) & active)
    def _():
        obuf[...] = jnp.zeros_like(obuf)
        gather(sc_ref, 0, 0, bm)

    prev = be_ref[jnp.maximum(i - 1, 0)]

    @pl.when(active & ((i == 0) | (be_ref[i] != prev)))
    def _():
        wg_s[...] = wg_ref[...].astype(BF16)
        wu_s[...] = wu_ref[...].astype(BF16)
        wd_s[...] = wd_ref[...].astype(BF16)

    @pl.when(active)
    def _():
        wait_gather(slot)
        x = _rows_from_tiles(xbuf.at[slot], bm).astype(BF16)
        hs = []
        for c in range(n_chunk):
            g = _dot(x, wg_s[:, c * cw:(c + 1) * cw])
            u = _dot(x, wu_s[:, c * cw:(c + 1) * cw])
            hs.append(((g * jax.nn.sigmoid(g)) * u).astype(BF16))
            gather(sn_ref, other, c * rows_per, (c + 1) * rows_per)
        h = jnp.concatenate(hs, axis=1)
        for c in range(n_chunk):
            _rows_to_tiles(obuf.at[slot], _dot(h, wd_s[:, c * cw:(c + 1) * cw]), c * cw, bm)
            scatter(ip_ref, other, c * rows_per, (c + 1) * rows_per)
        wait_scatter(other)

    @pl.when(i == n_act - 1)
    def _():
        wait_gather(other)
        scatter(ic_ref, slot, 0, bm)
        wait_scatter(slot)


def _moe(h2, info, blk_expert, n_active, wg, wu, wd, layer, n_tok):
    bm = EXPERT_BM
    n_rows = info.shape[0]
    n_blk = n_rows // bm
    info3 = info.reshape(n_blk, 1, bm)
    src3 = jnp.where(info3 >= TOP_K * n_tok, 0, info3 % n_tok)
    wspec = lambda k, n: pl.BlockSpec((None, None, k, n), lambda i, be, na: (layer, be[i], 0, 0))
    ispec = lambda f: pl.BlockSpec((1, 1, bm), lambda i, be, na: (f(i), 0, 0), memory_space=pltpu.SMEM)
    return pl.pallas_call(
        _moe_kernel,
        grid_spec=pltpu.PrefetchScalarGridSpec(
            num_scalar_prefetch=2,
            grid=(n_blk,),
            in_specs=[ispec(lambda i: jnp.maximum(i - 1, 0)), ispec(lambda i: i),
                      ispec(lambda i: i), ispec(lambda i: jnp.minimum(i + 1, n_blk - 1)),
                      pl.BlockSpec(memory_space=pl.ANY),
                      wspec(D_MODEL, EXPERT_FF), wspec(D_MODEL, EXPERT_FF), wspec(EXPERT_FF, D_MODEL)],
            out_specs=pl.BlockSpec(memory_space=pl.ANY),
            scratch_shapes=[pltpu.VMEM((2, bm * ROW_TILE, 128), F32), pltpu.VMEM((2, bm * ROW_TILE, 128), F32),
                            pltpu.SemaphoreType.DMA((2,)), pltpu.SemaphoreType.DMA((2,)),
                            pltpu.VMEM((D_MODEL, EXPERT_FF), BF16), pltpu.VMEM((D_MODEL, EXPERT_FF), BF16),
                            pltpu.VMEM((EXPERT_FF, D_MODEL), BF16)]),
        out_shape=jax.ShapeDtypeStruct(((TOP_K * n_tok + n_rows) * ROW_TILE, 128), F32),
        compiler_params=_cparams(("arbitrary",)),
        name="moe_experts",
    )(blk_expert, n_active, info3, info3, src3, src3, h2, wg, wu, wd)


def _dispatch_info(e_idx, rank, counts, n_tok):
    bm = EXPERT_BM
    counts = counts.astype(jnp.int32)
    padded = (counts + bm - 1) // bm * bm
    pad_end = jnp.cumsum(padded)
    pad_start = pad_end - padded
    experts = jnp.arange(N_EXPERTS, dtype=jnp.int32)
    start = jnp.sum(jnp.where(e_idx[..., None] == experts, pad_start, 0), axis=-1)
    dest = start + rank
    n_rows = -(-(n_tok * TOP_K) // bm) * bm + N_EXPERTS * bm
    n_blk = n_rows // bm
    spill = TOP_K * n_tok + jnp.arange(n_rows, dtype=jnp.int32)
    info = spill.at[dest.reshape(-1)].set(jnp.arange(TOP_K * n_tok, dtype=jnp.int32))
    blk_start = jnp.arange(n_blk, dtype=jnp.int32) * bm
    blk_expert = jnp.minimum(jnp.sum((blk_start[:, None] >= pad_end[None, :]).astype(jnp.int32), axis=-1),
                             N_EXPERTS - 1)
    n_active = (pad_end[-1] // bm).astype(jnp.int32).reshape(1)
    return info, blk_expert, n_active


def _residual_kernel(x_ref, y0_ref, y1_ref, w_ref, g2_ref, gf_ref, o_ref, *, final):
    w = w_ref[...]
    tm = x_ref.shape[0]
    moe = _rows_from_tiles(y0_ref, tm) * w[:, 0:1] + _rows_from_tiles(y1_ref, tm) * w[:, 1:2]
    xn = x_ref[...] + g2_ref[0, 0] * moe
    if final:
        y = xn * lax.rsqrt(jnp.mean(xn * xn, axis=-1, keepdims=True) + EPS)
        xn = y * gf_ref[...]
    o_ref[...] = xn


def _residual(x, y, w_sel, mod4, gain_final, n_tok, final):
    tm = TM
    n_tiles = n_tok // tm
    row = _mod_row(tm)
    tok = lambda w: pl.BlockSpec((tm, w), lambda i: (i, 0))
    return pl.pallas_call(
        functools.partial(_residual_kernel, final=final),
        grid=(n_tiles,),
        in_specs=[tok(D_MODEL), pl.BlockSpec((tm * ROW_TILE, 128), lambda i: (i, 0)),
                  pl.BlockSpec((tm * ROW_TILE, 128), lambda i: (n_tiles + i, 0)), tok(TOP_K),
                  pl.BlockSpec((1, 1, 1, D_MODEL), lambda i: (row(i), 5, 0, 0)),
                  pl.BlockSpec((1, D_MODEL), lambda i: (0, 0))],
        out_specs=tok(D_MODEL),
        out_shape=jax.ShapeDtypeStruct((n_tok if final else T_ALL, D_MODEL), F32),
        compiler_params=_cparams(("parallel",)),
        name="moe_residual",
    )(x, y, y, w_sel, mod4, gain_final)


def kernel(x, c, ctx, c_ctx, w_mod, b_mod, norm_mix, norm_ffn, w_in, attn_sink, conv_w, conv_b, filt_w1, filt_b1, filt_freq, filt_w2, filt_b2, filt_w3, hyena_bias, w_branch_attn, w_branch_fnet, w_branch_hyena, w_out, w_router, b_router, w_exp_gate, w_exp_up, w_exp_down, norm_final):
    mats = _dft_mats()
    cos_t, sin_t = _rope_tables()
    c8 = jnp.concatenate([c, c_ctx[None, :], jnp.zeros((8 - BATCH - 1, D_MODEL), F32)], axis=0)
    mod_all = _modulation(c8, w_mod, b_mod)
    xa = jnp.concatenate([x.reshape(T_LAT, D_MODEL), ctx.reshape(T_CTX, D_MODEL)], axis=0)
    wrt = w_router.T.astype(BF16)
    br = b_router.astype(F32).reshape(N_EXPERTS, 1)
    gain_final = norm_final.reshape(1, D_MODEL)
    out = None
    for l in range(DEPTH):
        last = l == DEPTH - 1
        with_ctx = not last
        n_tok = T_LAT if last else T_ALL
        mod4 = mod_all[l].reshape(8, N_MOD, 1, D_MODEL)
        p = _norm_proj(xa, norm_mix[l].reshape(1, D_MODEL), mod4, w_in, l, T_ALL)
        ya = _attention(p, attn_sink[l], cos_t, sin_t, with_ctx)
        yf = _fourier_mix(p, mats, with_ctx)
        z3 = _short_conv(p, conv_w[l], conv_b[l], with_ctx)
        filt = (filt_w1[l], filt_b1[l], filt_freq[l], filt_w2[l], filt_b2[l], filt_w3[l])
        yh = _hyena_mix(z3, filt, hyena_bias[l], mats, with_ctx)
        xa, h2, e_idx, w_sel, rank, cnt = _merge(
            xa, ya, yf, yh, p, w_branch_attn[l].astype(BF16), w_branch_fnet[l].astype(BF16),
            w_branch_hyena[l].astype(BF16), w_out[l].astype(BF16), mod4,
            norm_ffn[l].reshape(1, D_MODEL), wrt, br, n_tok)
        info, blk_expert, n_active = _dispatch_info(e_idx, rank, cnt[:, 0], n_tok)
        y = _moe(h2, info, blk_expert, n_active, w_exp_gate, w_exp_up, w_exp_down, l, n_tok)
        res = _residual(xa, y, w_sel.T, mod4, gain_final, n_tok, last)
        if last:
            out = res
        else:
            xa = res
    return out.reshape(BATCH, SEQ, D_MODEL)
```

```python
import functools
import math

import jax
import jax.numpy as jnp
from jax import lax
from jax.experimental import pallas as pl
from jax.experimental.pallas import tpu as pltpu

F32 = jnp.float32
BF16 = jnp.bfloat16

D_MODEL = 1024
BATCH = 4
SEQ = 4096
DEPTH = 4
GRID_W = 64
CTX_LEN = 256
EPS = 1e-6
N_MOD = 6

HEAD_DIM = 64
N_Q_HEADS = 8
N_KV_HEADS = 2
Q_PER_KV = N_Q_HEADS // N_KV_HEADS
ATTN_BLOCK = 128
ROPE_BASE = 10000.0

FNET_GROUPS = 4
FNET_GROUP_DIM = 128
FNET_WIDTH = FNET_GROUPS * FNET_GROUP_DIM

HYENA_WIDTH = 512
HYENA_ORDER = 2
FILTER_EMB = 33
FILTER_BANDS = (FILTER_EMB - 1) // 2
FILTER_HIDDEN = 64
DECAY_TARGET = 1e-2
FAST_DECAY_PCT = 0.3
SLOW_DECAY_PCT = 1.5

ATTN_WIDTH = N_Q_HEADS * HEAD_DIM
KV_WIDTH = N_KV_HEADS * HEAD_DIM
Q_OFF = 0
K_OFF = Q_OFF + ATTN_WIDTH
V_OFF = K_OFF + KV_WIDTH
F_OFF = V_OFF + KV_WIDTH
H_OFF = F_OFF + FNET_WIDTH
G_OFF = H_OFF + (HYENA_ORDER + 1) * HYENA_WIDTH
IN_WIDTH = G_OFF + 3 * D_MODEL

N_EXPERTS = 16
N_GROUPS = 4
EXPERTS_PER_GROUP = N_EXPERTS // N_GROUPS
TOP_K = 2
EXPERT_FF = 1024

T_LAT = BATCH * SEQ
T_CTX = BATCH * CTX_LEN
T_ALL = T_LAT + T_CTX

PG_OFF = G_OFF
PH_OFF = H_OFF
PF_OFF = F_OFF
PQ_OFF = Q_OFF
PK_OFF = K_OFF
PV_OFF = V_OFF

HY_N = 2 * SEQ
HY_N2 = 64
HY_N1 = HY_N // HY_N2
FN_N = 64

PITCH = HY_N2 + 8
SEQ_PAD = SEQ // HY_N2 * PITCH
T_PAD = BATCH * SEQ_PAD + T_CTX
UNROLL = 8
TM = 512
EXPERT_BM = 512
VMEM_LIMIT = 52 * 1024 * 1024


def _cparams(sem, vmem=VMEM_LIMIT):
    return pltpu.CompilerParams(dimension_semantics=sem, vmem_limit_bytes=vmem)


def _dot(a, b):
    return jnp.dot(a, b, preferred_element_type=F32)


def _cis(expo, n):
    ang = (2.0 * math.pi / n) * jnp.mod(expo, n).astype(F32)
    return jnp.cos(ang), jnp.sin(ang)


def _real_form(gr, gi):
    return jnp.concatenate([jnp.concatenate([gr, -gi], axis=-1), jnp.concatenate([gi, gr], axis=-1)], axis=-2)


def _dft_mats():
    ar = lambda n: jnp.arange(n, dtype=jnp.int32)
    m = {}
    c, s = _cis(ar(HY_N1)[:, None] * ar(HY_N1 // 2)[None, :], HY_N1)
    m['hy_lead_f'] = _real_form(c, -s).astype(BF16)
    c, s = _cis(ar(HY_N1 // 2)[:, None] * ar(HY_N1)[None, :], HY_N1)
    m['hy_lead_i'] = _real_form(c, s).astype(BF16)
    c, s = _cis(ar(HY_N1)[:, None] * ar(HY_N1 // 2)[None, :], HY_N1)
    m['hy_lead_kh'] = jnp.concatenate([c, -s], axis=0).astype(BF16)
    a = ar(HY_N1)[:, None, None]
    k2 = ar(HY_N2)[None, :, None]
    n2 = ar(HY_N2)[None, None, :]
    c, s = _cis(n2 * (a + HY_N1 * k2), HY_N)
    m['hy_slab_f2'] = jnp.concatenate([c, -s], axis=-1).astype(BF16)
    ct = jnp.swapaxes(c, 1, 2) * (1.0 / HY_N)
    st = jnp.swapaxes(s, 1, 2) * (1.0 / HY_N)
    m['hy_slab_i2'] = jnp.concatenate([ct, st], axis=-1).astype(BF16)
    nc = 2 * CTX_LEN
    c, s = _cis(ar(nc)[:, None] * ar(CTX_LEN)[None, :], nc)
    m['hc_f'] = _real_form(c, -s).astype(BF16)
    c, s = _cis(ar(CTX_LEN)[:, None] * ar(nc)[None, :], nc)
    m['hc_i'] = _real_form(c * (1.0 / nc), s * (1.0 / nc)).astype(BF16)
    c, s = _cis(ar(nc)[:, None] * ar(CTX_LEN)[None, :], nc)
    m['hc_kh'] = jnp.concatenate([c, -s], axis=0).astype(BF16)
    c, s = _cis(ar(FNET_GROUP_DIM)[:, None] * ar(FNET_GROUP_DIM)[None, :], FNET_GROUP_DIM)
    m['fn_chan'] = jnp.concatenate([c, -s], axis=1).astype(BF16)
    c, s = _cis(ar(FN_N)[:, None] * ar(FN_N)[None, :], FN_N)
    m['fn_lead'] = _real_form(c, -s).astype(BF16)
    a = ar(FN_N)[:, None, None]
    k1 = ar(FN_N)[None, :, None]
    n1 = ar(FN_N)[None, None, :]
    scale = 1.0 / math.sqrt(SEQ * FNET_GROUP_DIM)
    c, s = _cis(n1 * (a + FN_N * k1), SEQ)
    m['fn_slab'] = jnp.concatenate([c * scale, s * scale], axis=-1).astype(BF16)
    scale = 1.0 / math.sqrt(CTX_LEN * FNET_GROUP_DIM)
    c, s = _cis(ar(CTX_LEN)[:, None] * ar(CTX_LEN)[None, :], CTX_LEN)
    m['fc'] = jnp.concatenate([c * scale, s * scale], axis=-1).astype(BF16)
    return m


def _mod_kernel(c_ref, w_ref, b_ref, o_ref):
    c = c_ref[...]
    s = c * jax.nn.sigmoid(c)
    o_ref[0] = _dot(s.astype(BF16), w_ref[0].astype(BF16)) + b_ref[0]


def _modulation(c8, w_mod, b_mod):
    tn = 1536
    n = N_MOD * D_MODEL
    return pl.pallas_call(
        _mod_kernel,
        grid=(DEPTH, n // tn),
        in_specs=[pl.BlockSpec((8, D_MODEL), lambda l, j: (0, 0)),
                  pl.BlockSpec((1, D_MODEL, tn), lambda l, j: (l, 0, j)),
                  pl.BlockSpec((1, 1, tn), lambda l, j: (l, 0, j))],
        out_specs=pl.BlockSpec((1, 8, tn), lambda l, j: (l, 0, j)),
        out_shape=jax.ShapeDtypeStruct((DEPTH, 8, n), F32),
        compiler_params=_cparams(("parallel", "parallel")),
        name="adaln_modulation",
    )(c8, w_mod, b_mod.reshape(DEPTH, 1, n))


def _mod_row(tm):
    tiles_per_batch = SEQ // tm
    return lambda i: jnp.minimum(i // tiles_per_batch, BATCH)


def _rms_mod(x, g, sh, sc):
    y = x * lax.rsqrt(jnp.mean(x * x, axis=-1, keepdims=True) + EPS)
    return (y * g) * (1.0 + sc) + sh


def _norm_proj_kernel(x_ref, g_ref, sh_ref, sc_ref, w_ref, o_ref, w_s):
    @pl.when(pl.program_id(1) == 0)
    def _():
        w_s[...] = w_ref[...].astype(BF16)

    h = _rms_mod(x_ref[...], g_ref[...], sh_ref[0, 0], sc_ref[0, 0]).astype(BF16)
    o_ref[...] = _dot(h, w_s[...]).astype(o_ref.dtype)


def _norm_proj(x, gain, mod4, w, layer, n_tok):
    tm = TM
    n_out = w.shape[2]
    tn = n_out // 2
    row = _mod_row(tm)
    return pl.pallas_call(
        _norm_proj_kernel,
        grid=(n_out // tn, n_tok // tm),
        in_specs=[pl.BlockSpec((tm, D_MODEL), lambda j, i: (i, 0)),
                  pl.BlockSpec((1, D_MODEL), lambda j, i: (0, 0)),
                  pl.BlockSpec((1, 1, 1, D_MODEL), lambda j, i: (row(i), 0, 0, 0)),
                  pl.BlockSpec((1, 1, 1, D_MODEL), lambda j, i: (row(i), 1, 0, 0)),
                  pl.BlockSpec((None, D_MODEL, tn), lambda j, i: (layer, 0, j))],
        out_specs=pl.BlockSpec((tm, tn), lambda j, i: (i, j)),
        out_shape=jax.ShapeDtypeStruct((T_ALL, n_out), BF16),
        scratch_shapes=[pltpu.VMEM((D_MODEL, tn), BF16)],
        compiler_params=_cparams(("arbitrary", "arbitrary")),
        name="norm_in_proj",
    )(x, gain, mod4, mod4, w)


def _softmax_pv(qh, k_parts, v_parts, masks, sink):
    nt = (((1,), (1,)), ((), ()))
    scores = []
    for kp, mk in zip(k_parts, masks):
        s = lax.dot_general(qh, kp, nt, preferred_element_type=F32)
        if mk is not None:
            s = jnp.where(mk, s, -1e30)
        scores.append(s)
    m = sink
    for s in scores:
        m = jnp.maximum(m, jnp.max(s, axis=-1, keepdims=True))
    es = [jnp.exp(s - m) for s in scores]
    den = jnp.exp(sink - m)
    for e in es:
        den = den + jnp.sum(e, axis=-1, keepdims=True)
    inv = 1.0 / den
    o = None
    for e, vp in zip(es, v_parts):
        t = _dot((e * inv).astype(BF16), vp)
        o = t if o is None else o + t
    return o


def _attn_kernel(sink_ref, q_ref, km_ref, k0_ref, kp_ref, vm_ref, v0_ref, vp_ref, kc_ref, vc_ref,
                 cos_ref, sin_ref, o_ref, *, nb):
    n = pl.program_id(1)
    blk = ATTN_BLOCK
    lane = lax.broadcasted_iota(jnp.int32, (blk, 128), 1)
    first = (lane % 32) < 16

    def rope(x, blk_idx):
        r0 = pl.multiple_of(blk_idx * blk, blk)
        c = cos_ref[pl.ds(r0, blk), :]
        s = sin_ref[pl.ds(r0, blk), :]
        sw = jnp.where(first, pltpu.roll(x, 112, 1), pltpu.roll(x, 16, 1))
        return x * c + sw * s

    nm = jnp.maximum(n - 1, 0)
    npl = jnp.minimum(n + 1, nb - 1)
    ld = lambda ref: ref[...].astype(F32)
    kall = jnp.concatenate([rope(ld(km_ref), nm), rope(ld(k0_ref), n), rope(ld(kp_ref), npl), ld(kc_ref)], axis=0)
    vall = jnp.concatenate([ld(vm_ref), ld(v0_ref), ld(vp_ref), ld(vc_ref)], axis=0)
    nk = kall.shape[0]
    kswap = pltpu.roll(kall, HEAD_DIM, 1)
    vswap = pltpu.roll(vall, HEAD_DIM, 1)
    lo = lax.broadcasted_iota(jnp.int32, (nk, 128), 1) < HEAD_DIM

    r = lax.broadcasted_iota(jnp.int32, (2 * blk, blk), 0) % blk
    cidx = lax.broadcasted_iota(jnp.int32, (2 * blk, blk), 1)
    ok_prev = jnp.where(cidx >= r, (n > 0).astype(jnp.int32), 0) > 0
    ok_next = jnp.where(cidx <= r, (n < nb - 1).astype(jnp.int32), 0) > 0
    top_rows = lax.broadcasted_iota(jnp.int32, (2 * blk, 1), 0) < blk
    neg = jnp.float32(-1e30)

    scale = HEAD_DIM ** -0.5
    q2 = [(rope(q_ref[:, p * 128:(p + 1) * 128].astype(F32), n) * scale).astype(BF16)
          for p in range(N_Q_HEADS // 2)]
    nt = (((1,), (1,)), ((), ()))
    for h in range(N_KV_HEADS):
        ka, kb = (kall, kswap) if h == 0 else (kswap, kall)
        va, vb = (vall, vswap) if h == 0 else (vswap, vall)
        kbd = jnp.concatenate([jnp.where(lo, ka, 0.0), jnp.where(lo, 0.0, kb)], axis=0).astype(BF16)
        vbd = jnp.concatenate([jnp.where(lo, va, 0.0), jnp.where(lo, 0.0, vb)], axis=0).astype(BF16)
        q4 = jnp.concatenate([q2[2 * h], q2[2 * h + 1]], axis=0)
        s = lax.dot_general(q4, kbd, nt, preferred_element_type=F32)
        probs = []
        for c in range(2):
            base = c * nk
            sink = jnp.where(top_rows, sink_ref[Q_PER_KV * h + c], sink_ref[Q_PER_KV * h + 2 + c])
            parts = [jnp.where(ok_prev, s[:, base:base + blk], neg),
                     s[:, base + blk:base + 2 * blk],
                     jnp.where(ok_next, s[:, base + 2 * blk:base + 3 * blk], neg),
                     s[:, base + 3 * blk:base + nk]]
            m = sink
            for part in parts:
                m = jnp.maximum(m, jnp.max(part, axis=-1, keepdims=True))
            es = [jnp.exp(part - m) for part in parts]
            den = jnp.exp(sink - m)
            for e in es:
                den = den + jnp.sum(e, axis=-1, keepdims=True)
            inv = 1.0 / den
            probs += [(e * inv).astype(BF16) for e in es]
        o = _dot(jnp.concatenate(probs, axis=1), vbd)
        w0 = h * Q_PER_KV * HEAD_DIM
        o_ref[:, w0:w0 + 128] = o[:blk].astype(o_ref.dtype)
        o_ref[:, w0 + 128:w0 + 256] = o[blk:].astype(o_ref.dtype)


def _ctx_attn_kernel(sink_ref, q_ref, kc_ref, vc_ref, o_ref):
    kc = kc_ref[...].astype(BF16)
    vc = vc_ref[...].astype(BF16)
    scale = HEAD_DIM ** -0.5
    outs = []
    for pair in range(N_Q_HEADS // 2):
        q2 = (q_ref[:, pair * 128:(pair + 1) * 128] * scale).astype(BF16)
        for sub in range(2):
            head = 2 * pair + sub
            kvh = head // Q_PER_KV
            sl = slice(kvh * HEAD_DIM, (kvh + 1) * HEAD_DIM)
            qh = q2[:, sub * HEAD_DIM:(sub + 1) * HEAD_DIM]
            outs.append(_softmax_pv(qh, [kc[:, sl]], [vc[:, sl]], [None], sink_ref[head]))
    o_ref[...] = jnp.concatenate(outs, axis=-1).astype(o_ref.dtype)


def _attention(p, sink, cos_t, sin_t, with_ctx):
    blk = ATTN_BLOCK
    nb = SEQ // blk
    qc, kcol, vcol = PQ_OFF // ATTN_WIDTH, PK_OFF // KV_WIDTH, PV_OFF // KV_WIDTH
    ctx_blk = T_LAT // CTX_LEN
    smem = pl.BlockSpec(memory_space=pltpu.SMEM)

    def kv_spec(col, d):
        return pl.BlockSpec((blk, KV_WIDTH),
                            lambda b, n: (b * nb + jnp.clip(n + d, 0, nb - 1), col))

    ya = pl.pallas_call(
        functools.partial(_attn_kernel, nb=nb),
        grid=(BATCH, nb),
        in_specs=[smem,
                  pl.BlockSpec((blk, ATTN_WIDTH), lambda b, n: (b * nb + n, qc)),
                  kv_spec(kcol, -1), kv_spec(kcol, 0), kv_spec(kcol, 1),
                  kv_spec(vcol, -1), kv_spec(vcol, 0), kv_spec(vcol, 1),
                  pl.BlockSpec((CTX_LEN, KV_WIDTH), lambda b, n: (ctx_blk + b, kcol)),
                  pl.BlockSpec((CTX_LEN, KV_WIDTH), lambda b, n: (ctx_blk + b, vcol)),
                  pl.BlockSpec((SEQ, KV_WIDTH), lambda b, n: (0, 0)),
                  pl.BlockSpec((SEQ, KV_WIDTH), lambda b, n: (0, 0))],
        out_specs=pl.BlockSpec((blk, ATTN_WIDTH), lambda b, n: (b * nb + n, 0)),
        out_shape=jax.ShapeDtypeStruct((T_ALL, ATTN_WIDTH), BF16),
        compiler_params=_cparams(("parallel", "parallel")),
        name="banded_attention",
    )(sink, p, p, p, p, p, p, p, p, p, cos_t, sin_t)
    if not with_ctx:
        return ya
    cb = CTX_LEN // blk
    lat_blk = T_LAT // blk

    def alias_kernel(sink_ref, q_ref, kc_ref, vc_ref, ya_in_ref, o_ref):
        del ya_in_ref
        _ctx_attn_kernel(sink_ref, q_ref, kc_ref, vc_ref, o_ref)

    return pl.pallas_call(
        alias_kernel,
        grid=(BATCH, cb),
        in_specs=[smem,
                  pl.BlockSpec((blk, ATTN_WIDTH), lambda b, n: (lat_blk + b * cb + n, qc)),
                  pl.BlockSpec((CTX_LEN, KV_WIDTH), lambda b, n: (ctx_blk + b, kcol)),
                  pl.BlockSpec((CTX_LEN, KV_WIDTH), lambda b, n: (ctx_blk + b, vcol)),
                  pl.BlockSpec(memory_space=pl.ANY)],
        out_specs=pl.BlockSpec((blk, ATTN_WIDTH), lambda b, n: (lat_blk + b * cb + n, 0)),
        out_shape=jax.ShapeDtypeStruct((T_ALL, ATTN_WIDTH), BF16),
        input_output_aliases={4: 0},
        compiler_params=_cparams(("parallel", "parallel")),
        name="context_attention",
    )(sink, p, p, p, ya)


def _rope_tables():
    n_freq = HEAD_DIM // 4
    freqs = ROPE_BASE ** (-jnp.arange(n_freq, dtype=F32) / n_freq)
    t = jnp.arange(SEQ, dtype=jnp.int32)
    rows = (t // GRID_W).astype(F32)[:, None] * freqs
    cols = (t % GRID_W).astype(F32)[:, None] * freqs
    cos_h = jnp.concatenate([jnp.cos(rows), jnp.cos(rows), jnp.cos(cols), jnp.cos(cols)], axis=-1)
    sin_h = jnp.concatenate([-jnp.sin(rows), jnp.sin(rows), -jnp.sin(cols), jnp.sin(cols)], axis=-1)
    return jnp.tile(cos_h, (1, 2)), jnp.tile(sin_h, (1, 2))


def _lead_kernel(*refs, n_in, cmul, epi):
    m_ref = refs[0]
    x_refs = refs[1:1 + n_in]
    pos = 1 + n_in
    xs = []
    for r in x_refs:
        v = r[...]
        xs.append(v.reshape(-1, v.shape[-1]))
    x = xs[0] if n_in == 1 else jnp.concatenate(xs, axis=0)
    if cmul:
        k = refs[pos][...]
        pos += 1
        k = k.reshape(-1, k.shape[-1])
        half = x.shape[0] // 2
        xr, xi, kr, ki = x[:half], x[half:], k[:half], k[half:]
        x = jnp.concatenate([xr * kr - xi * ki, xr * ki + xi * kr], axis=0)
    res = _dot(m_ref[...], x.astype(BF16))
    if epi:
        g_ref, y_ref, b_ref = refs[pos:pos + 3]
        pos += 3
        g = g_ref[...]
        y = y_ref[...]
        res = g.reshape(-1, g.shape[-1]) * (res + y.reshape(-1, y.shape[-1]) * b_ref[...])
    o_ref = refs[pos]
    o_ref[...] = res.reshape(o_ref.shape).astype(o_ref.dtype)


def _lead(mat, xs, x_specs, grid, out_shape, out_spec, *, kspec=None, epi=None, alias_to=None, name):
    ins = [mat] + list(xs)
    specs = [pl.BlockSpec(mat.shape, lambda *a: (0, 0))] + list(x_specs)
    if kspec is not None:
        ins.append(kspec[0])
        specs.append(kspec[1])
    if epi is not None:
        for arr, sp in epi:
            ins.append(arr)
            specs.append(sp)
    kern = functools.partial(_lead_kernel, n_in=len(xs), cmul=kspec is not None, epi=epi is not None)
    aliases = {}
    if alias_to is not None:
        aliases = {len(ins): 0}
        ins.append(alias_to)
        specs.append(pl.BlockSpec(memory_space=pl.ANY))
        inner = kern

        def kern(*refs):
            inner(*refs[:-2], refs[-1])

    return pl.pallas_call(
        kern, grid=grid, in_specs=specs, out_specs=out_spec, out_shape=out_shape,
        input_output_aliases=aliases,
        compiler_params=_cparams(("parallel",) * len(grid)), name=name,
    )(*ins)


def _cstack(xr, xi):
    return jnp.concatenate([jnp.concatenate([xr, xi], axis=1), jnp.concatenate([-xi, xr], axis=1)], axis=0)


def _fnet_kernel(u_ref, mc_ref, ml_ref, ms_ref, o_ref, zr, zi, are, aim):
    n = FN_N
    pitch = PITCH
    gd = FNET_GROUP_DIM
    rows = 4 * n
    mc = mc_ref[...]

    def chan(i, carry):
        r_in = pl.multiple_of(i * rows, rows)
        z = _dot(u_ref[pl.ds(r_in, rows), :].astype(BF16), mc)
        for q in range(rows // n):
            r_out = pl.multiple_of((i * (rows // n) + q) * pitch, 8)
            zr[pl.ds(r_out, n), :] = z[q * n:(q + 1) * n, :gd]
            zi[pl.ds(r_out, n), :] = z[q * n:(q + 1) * n, gd:]
        return carry

    lax.fori_loop(0, SEQ // rows, chan, 0)
    ml = ml_ref[...]

    def lead(i, carry):
        for u in range(UNROLL):
            n1 = i * UNROLL + u
            x = jnp.concatenate([zr[pl.ds(n1, n, stride=pitch), :], zi[pl.ds(n1, n, stride=pitch), :]],
                                axis=0).astype(BF16)
            r = _dot(ml, x)
            are[pl.ds(n1, n, stride=pitch), :] = r[:n]
            aim[pl.ds(n1, n, stride=pitch), :] = r[n:]
        return carry

    lax.fori_loop(0, n // UNROLL, lead, 0)

    def slab(i, carry):
        for u in range(UNROLL):
            k2 = i * UNROLL + u
            r0 = pl.multiple_of(k2 * pitch, 8)
            x = jnp.concatenate([are[pl.ds(r0, n), :], aim[pl.ds(r0, n), :]], axis=0).astype(BF16)
            o_ref[pl.ds(k2, n, stride=n), :] = _dot(ms_ref[k2], x)
        return carry

    lax.fori_loop(0, n // UNROLL, slab, 0)


def _fnet_ctx_kernel(u_ref, mc_ref, mf_ref, yf_in_ref, o_ref):
    del yf_in_ref
    gd = FNET_GROUP_DIM
    mc = mc_ref[...]
    mf = mf_ref[...]
    for g in range(FNET_GROUPS):
        z = _dot(u_ref[:, g * gd:(g + 1) * gd].astype(BF16), mc)
        x = jnp.concatenate([z[:, :gd], z[:, gd:]], axis=0).astype(BF16)
        o_ref[:, g * gd:(g + 1) * gd] = _dot(mf, x)


def _fourier_mix(p, mats, with_ctx):
    gd = FNET_GROUP_DIM
    col0 = PF_OFF // gd
    full = lambda a: pl.BlockSpec(a.shape, lambda *i: (0,) * a.ndim)
    scr = pltpu.VMEM((FN_N * PITCH, gd), F32)
    yf = pl.pallas_call(
        _fnet_kernel,
        grid=(BATCH, FNET_GROUPS),
        in_specs=[pl.BlockSpec((SEQ, gd), lambda b, g: (b, col0 + g)),
                  full(mats['fn_chan']), full(mats['fn_lead']), full(mats['fn_slab'])],
        out_specs=pl.BlockSpec((SEQ, gd), lambda b, g: (b, g)),
        out_shape=jax.ShapeDtypeStruct((T_ALL if with_ctx else T_LAT, FNET_WIDTH), F32),
        scratch_shapes=[scr, scr, scr, scr],
        compiler_params=_cparams(("parallel", "parallel")),
        name="fnet_latent",
    )(p, mats['fn_chan'], mats['fn_lead'], mats['fn_slab'])
    if not with_ctx:
        return yf
    blk0 = T_LAT // CTX_LEN
    return pl.pallas_call(
        _fnet_ctx_kernel,
        grid=(BATCH,),
        in_specs=[pl.BlockSpec((pl.Element(CTX_LEN), pl.Element(FNET_WIDTH)),
                               lambda b: ((blk0 + b) * CTX_LEN, PF_OFF)),
                  full(mats['fn_chan']), full(mats['fc']), pl.BlockSpec(memory_space=pl.ANY)],
        out_specs=pl.BlockSpec((CTX_LEN, FNET_WIDTH), lambda b: (blk0 + b, 0)),
        out_shape=jax.ShapeDtypeStruct((T_ALL, FNET_WIDTH), F32),
        input_output_aliases={3: 0},
        compiler_params=_cparams(("parallel",)),
        name="fnet_ctx",
    )(p, mats['fn_chan'], mats['fc'], yf)


def _short_conv_kernel(u_ref, w_ref, b_ref, o_ref, *, rows, chunk, padded):
    w0 = w_ref[0:1, :]
    w1 = w_ref[1:2, :]
    w2 = w_ref[2:3, :]
    bias = b_ref[...]
    width = u_ref.shape[-1]
    ridx = lax.broadcasted_iota(jnp.int32, (chunk, width), 0)
    n_chunks = rows // chunk
    for ci in range(n_chunks):
        r0 = ci * chunk
        cur = u_ref[r0:r0 + chunk, :].astype(F32)
        if ci > 0:
            prev_row = u_ref[r0 - 16:r0, :].astype(F32)[15:16, :]
        else:
            prev_row = jnp.zeros((1, width), F32)
        if ci < n_chunks - 1:
            next_row = u_ref[r0 + chunk:r0 + chunk + 16, :].astype(F32)[0:1, :]
        else:
            next_row = jnp.zeros((1, width), F32)
        up = jnp.where(ridx == 0, prev_row, pltpu.roll(cur, 1, 0))
        dn = jnp.where(ridx == chunk - 1, next_row, pltpu.roll(cur, chunk - 1, 0))
        res = up * w0 + cur * w1 + dn * w2 + bias
        if not padded:
            o_ref[0, r0:r0 + chunk, :] = res
            continue
        for q in range(chunk // HY_N2):
            p0 = (r0 // HY_N2 + q) * PITCH
            o_ref[0, p0:p0 + HY_N2, :] = res[q * HY_N2:(q + 1) * HY_N2]
            o_ref[0, p0 + HY_N2:p0 + PITCH, :] = jnp.zeros((PITCH - HY_N2, width), F32)


def _short_conv(p, conv_w, conv_b, with_ctx):
    cw = 256
    hw = (HYENA_ORDER + 1) * HYENA_WIDTH
    ncol = hw // cw
    per = HYENA_WIDTH // cw
    col0 = PH_OFF // cw
    out_shape = jax.ShapeDtypeStruct((HYENA_ORDER + 1, T_PAD if with_ctx else BATCH * SEQ_PAD, HYENA_WIDTH), F32)
    b2 = conv_b.reshape(1, hw)

    def call(rows, blk0, alias):
        padded = rows == SEQ
        out_rows = SEQ_PAD if padded else rows
        out_blk0 = 0 if padded else BATCH * SEQ_PAD // rows
        kern = functools.partial(_short_conv_kernel, rows=rows, chunk=min(rows, 256), padded=padded)
        ins = [p, conv_w, b2]
        specs = [pl.BlockSpec((rows, cw), lambda b, j: (blk0 + b, col0 + j)),
                 pl.BlockSpec((3, cw), lambda b, j: (0, j)),
                 pl.BlockSpec((1, cw), lambda b, j: (0, j))]
        aliases = {}
        if alias is not None:
            ins.append(alias)
            specs.append(pl.BlockSpec(memory_space=pl.ANY))
            aliases = {3: 0}
            inner = kern

            def kern(u_ref, w_ref, b_ref, a_ref, o_ref):
                del a_ref
                inner(u_ref, w_ref, b_ref, o_ref)

        return pl.pallas_call(
            kern, grid=(BATCH, ncol), in_specs=specs,
            out_specs=pl.BlockSpec((1, out_rows, cw), lambda b, j: (j // per, out_blk0 + b, j % per)),
            out_shape=out_shape, input_output_aliases=aliases,
            compiler_params=_cparams(("parallel", "parallel")), name="hyena_short_conv",
        )(*ins)

    z3 = call(SEQ, 0, None)
    if with_ctx:
        z3 = call(CTX_LEN, T_LAT // CTX_LEN, z3)
    return z3


def _filter_mlp_kernel(ft_ref, w1_ref, b1_ref, fq_ref, w2_ref, b2_ref, o_ref):
    fq = fq_ref[...]
    h = jnp.sin(fq * (_dot(ft_ref[...].astype(BF16), w1_ref[...]) + b1_ref[...]))
    h = jnp.sin(fq * (_dot(h.astype(BF16), w2_ref[...]) + b2_ref[...]))
    o_ref[...] = h.astype(o_ref.dtype)


def _filter_kernel(h_ref, t_ref, w3f_ref, w3b_ref, dl_ref, m1_ref, *rest, n, dense):
    hb = h_ref[...]
    decay = jnp.exp(-t_ref[...] * dl_ref[...])
    tf = _dot(hb, w3f_ref[...]) * decay
    tb = _dot(hb, w3b_ref[...]) * decay
    tb = jnp.where(lax.broadcasted_iota(jnp.int32, tb.shape, 0) == 0, 0.0, tb)
    scale = 1.0 / (jnp.sum(jnp.abs(tf), axis=0, keepdims=True) + jnp.sum(jnp.abs(tb), axis=0, keepdims=True))
    cw = tf.shape[1]
    if dense:
        o_ref = rest[0]
        r = _dot(m1_ref[...], jnp.concatenate([tf, tb], axis=1).astype(BF16))
        nc = r.shape[0] // 2
        o_ref[0:nc, :] = (r[:nc, :cw] + r[:nc, cw:]) * scale
        o_ref[nc:, :] = (r[nc:, :cw] - r[nc:, cw:]) * scale
        return
    f2_ref, o_ref, tf_s, tb_s, are_f, aim_f, are_b, aim_b = rest
    tf_s[...] = tf
    tb_s[...] = tb
    half = HY_N1 // 2
    m1 = m1_ref[...]

    def stage1(i, carry):
        for u in range(UNROLL):
            n2 = i * UNROLL + u
            x = jnp.concatenate([tf_s[pl.ds(n2, half, stride=HY_N2), :], tb_s[pl.ds(n2, half, stride=HY_N2), :]],
                                axis=1).astype(BF16)
            r = _dot(m1, x)
            rows = pl.ds(n2, HY_N1, stride=PITCH)
            are_f[rows, :] = r[:HY_N1, :cw]
            are_b[rows, :] = r[:HY_N1, cw:]
            aim_f[rows, :] = r[HY_N1:, :cw]
            aim_b[rows, :] = r[HY_N1:, cw:]
        return carry

    lax.fori_loop(0, HY_N2 // UNROLL, stage1, 0)

    def stage2(i, carry):
        for u in range(UNROLL):
            k1 = i * UNROLL + u
            rows = pl.ds(pl.multiple_of(k1 * PITCH, 8), HY_N2)
            ar = jnp.concatenate([are_f[rows, :], are_b[rows, :]], axis=1)
            ai = jnp.concatenate([aim_f[rows, :], aim_b[rows, :]], axis=1)
            t = _dot(f2_ref[k1], _cstack(ar, ai).astype(BF16))
            o_ref[0, k1] = (t[:, 0:cw] + t[:, cw:2 * cw]) * scale
            o_ref[1, k1] = (t[:, 2 * cw:3 * cw] - t[:, 3 * cw:4 * cw]) * scale
        return carry

    lax.fori_loop(0, HY_N1 // UNROLL, stage2, 0)


def _filter_feats(n):
    pos = jnp.arange(n, dtype=F32)
    t = pos / max(n - 1, 1)
    omega = 2.0 * math.pi * pos / n
    bands = jnp.linspace(1e-4, FILTER_BANDS - 1, FILTER_BANDS, dtype=F32)
    feats = jnp.concatenate([t[:, None], jnp.cos(omega[:, None] * bands), -jnp.sin(omega[:, None] * bands)], axis=-1)
    return jnp.pad(feats, ((0, 0), (0, 128 - FILTER_EMB))), t[:, None]


def _filter_spectrum(n, filt, mats):
    w1, b1, freq, w2, b2, w3 = filt
    dense = n == CTX_LEN
    cw = 128
    nch = HYENA_WIDTH // cw
    feats, t = _filter_feats(n)
    w1p = jnp.pad(w1, ((0, 128 - FILTER_EMB), (0, 0))).astype(BF16)
    deltas = jnp.abs(jnp.linspace(math.log(DECAY_TARGET) / SLOW_DECAY_PCT, math.log(DECAY_TARGET) / FAST_DECAY_PCT,
                                  HYENA_WIDTH, dtype=F32)).reshape(1, HYENA_WIDTH)
    full = lambda a: pl.BlockSpec(a.shape, lambda *i: (0,) * a.ndim)
    row = lambda a: a.reshape(1, -1)
    w3b16 = w3.astype(BF16)
    tap_spec = lambda d: pl.BlockSpec((FILTER_HIDDEN, cw), lambda o, ch: (0, (o * 2 + d) * nch + ch))
    m1 = mats['hc_kh'] if dense else mats['hy_lead_kh']
    mlp_ins = [feats, w1p, row(b1), row(freq), w2.astype(BF16), row(b2)]
    hb = pl.pallas_call(
        _filter_mlp_kernel, grid=(1,), in_specs=[full(a) for a in mlp_ins],
        out_specs=pl.BlockSpec((n, FILTER_HIDDEN), lambda i: (0, 0)),
        out_shape=jax.ShapeDtypeStruct((n, FILTER_HIDDEN), BF16),
        compiler_params=_cparams(("arbitrary",)), name="hyena_filter_mlp",
    )(*mlp_ins)
    ins = [hb, t, w3b16, w3b16, deltas, m1]
    specs = [full(hb), full(t), tap_spec(0), tap_spec(1), pl.BlockSpec((1, cw), lambda o, ch: (0, ch)), full(m1)]
    if dense:
        nc = 2 * n
        out_shape = jax.ShapeDtypeStruct((HYENA_ORDER, 2 * nc, HYENA_WIDTH), F32)
        out_spec = pl.BlockSpec((None, 2 * nc, cw), lambda o, ch: (o, 0, ch))
        scratch = []
    else:
        ins.append(mats['hy_slab_f2'])
        specs.append(full(mats['hy_slab_f2']))
        out_shape = jax.ShapeDtypeStruct((HYENA_ORDER, 2, HY_N1, HY_N2, HYENA_WIDTH), F32)
        out_spec = pl.BlockSpec((None, 2, HY_N1, HY_N2, cw), lambda o, ch: (o, 0, 0, 0, ch))
        scratch = [pltpu.VMEM((n, cw), F32)] * 2 + [pltpu.VMEM((HY_N1 * PITCH, cw), F32)] * 4
    return pl.pallas_call(
        functools.partial(_filter_kernel, n=n, dense=dense),
        grid=(HYENA_ORDER, nch), in_specs=specs, out_specs=out_spec, out_shape=out_shape,
        scratch_shapes=scratch,
        compiler_params=_cparams(("parallel", "parallel")),
        name="hyena_filter_ctx" if dense else "hyena_filter",
    )(*ins)


def _hyena_conv_kernel(y_ref, g_ref, k_ref, m1_ref, f2f_ref, f2i_ref, m3_ref, b_ref, o_ref, are, aim, *, out_pitch):
    half = HY_N1 // 2
    out_seq = half * out_pitch
    m1 = m1_ref[...]

    def stage1(i, carry):
        for u in range(UNROLL):
            n2 = i * UNROLL + u
            x = jnp.concatenate([y_ref[pl.ds(n2, half, stride=PITCH), :],
                                 y_ref[pl.ds(SEQ_PAD + n2, half, stride=PITCH), :]], axis=0).astype(BF16)
            r = _dot(m1, x)
            are[pl.ds(n2, HY_N1, stride=PITCH), :] = r[:HY_N1]
            aim[pl.ds(n2, HY_N1, stride=PITCH), :] = r[HY_N1:]
        return carry

    lax.fori_loop(0, HY_N2 // UNROLL, stage1, 0)
    cw = o_ref.shape[-1]

    def stage2(i, carry):
        for u in range(UNROLL):
            k1 = i * UNROLL + u
            r0 = pl.multiple_of(k1 * PITCH, 8)
            y = _dot(f2f_ref[k1], _cstack(are[pl.ds(r0, HY_N2), :], aim[pl.ds(r0, HY_N2), :]).astype(BF16))
            yr, yi = y[:, :cw], y[:, cw:]
            kr, ki = k_ref[0, k1], k_ref[1, k1]
            w = _dot(f2i_ref[k1], _cstack(yr * kr - yi * ki, yr * ki + yi * kr).astype(BF16))
            are[pl.ds(r0, HY_N2), :] = w[:, :cw]
            aim[pl.ds(r0, HY_N2), :] = w[:, cw:]
        return carry

    lax.fori_loop(0, HY_N1 // UNROLL, stage2, 0)
    m3 = m3_ref[...]
    bias = b_ref[...]

    def stage3(i, carry):
        for u in range(UNROLL):
            n2 = i * UNROLL + u
            bn = jnp.concatenate([are[pl.ds(n2, HY_N1, stride=PITCH), :], aim[pl.ds(n2, HY_N1, stride=PITCH), :]],
                                 axis=0).astype(BF16)
            y = _dot(m3, bn)
            for b in range(2):
                rows = pl.ds(b * SEQ_PAD + n2, half, stride=PITCH)
                o_ref[pl.ds(b * out_seq + n2, half, stride=out_pitch), :] = (
                    g_ref[rows, :] * (y[b * half:(b + 1) * half] + y_ref[rows, :] * bias))
        return carry

    lax.fori_loop(0, HY_N2 // UNROLL, stage3, 0)


def _hyena_mix(z3, filt, hyena_bias, mats, with_ctx):
    c = HYENA_WIDTH
    cw = 128
    nch = c // cw
    pairs = BATCH // 2
    full = lambda a: pl.BlockSpec(a.shape, lambda *i: (0,) * a.ndim)
    kspec = _filter_spectrum(SEQ, filt, mats)
    if with_ctx:
        kspec_c = _filter_spectrum(CTX_LEN, filt, mats)
    bias3 = hyena_bias.reshape(HYENA_ORDER, 1, c)
    scr = pltpu.VMEM((HY_N1 * PITCH, cw), F32)
    y = None
    for o in range(HYENA_ORDER):
        final = o == HYENA_ORDER - 1
        out_pitch = HY_N2 if final else PITCH
        out_seq = SEQ if final else SEQ_PAD
        if final:
            out_rows = T_ALL if with_ctx else T_LAT
        else:
            out_rows = T_PAD if with_ctx else BATCH * SEQ_PAD
        if y is None:
            xin, xspec = z3, pl.BlockSpec((None, 2 * SEQ_PAD, cw), lambda b, ch: (0, b, ch))
        else:
            xin, xspec = y, pl.BlockSpec((2 * SEQ_PAD, cw), lambda b, ch: (b, ch))
        ynew = pl.pallas_call(
            functools.partial(_hyena_conv_kernel, out_pitch=out_pitch),
            grid=(pairs, nch),
            in_specs=[xspec,
                      pl.BlockSpec((None, 2 * SEQ_PAD, cw), lambda b, ch, o=o: (o + 1, b, ch)),
                      pl.BlockSpec((None, 2, HY_N1, HY_N2, cw), lambda b, ch, o=o: (o, 0, 0, 0, ch)),
                      full(mats['hy_lead_f']), full(mats['hy_slab_f2']), full(mats['hy_slab_i2']),
                      full(mats['hy_lead_i']),
                      pl.BlockSpec((None, 1, cw), lambda b, ch, o=o: (o, 0, ch))],
            out_specs=pl.BlockSpec((2 * out_seq, cw), lambda b, ch: (b, ch)),
            out_shape=jax.ShapeDtypeStruct((out_rows, c), F32),
            scratch_shapes=[scr, scr],
            compiler_params=_cparams(("parallel", "parallel"), 58 * 1024 * 1024),
            name="hyena_conv",
        )(xin, z3, kspec, mats['hy_lead_f'], mats['hy_slab_f2'], mats['hy_slab_i2'], mats['hy_lead_i'], bias3)
        if with_ctx:
            nc = 2 * CTX_LEN
            blk0 = BATCH * SEQ_PAD // nc
            oblk0 = BATCH * out_seq // nc
            if y is None:
                cin, cspec_in = z3, pl.BlockSpec((1, nc, c), lambda b: (0, blk0 + b, 0))
            else:
                cin, cspec_in = y, pl.BlockSpec((nc, c), lambda b: (blk0 + b, 0))
            xc = _lead(mats['hc_f'], [cin], [cspec_in], (pairs,),
                       jax.ShapeDtypeStruct((pairs, 2 * nc, c), F32),
                       pl.BlockSpec((1, 2 * nc, c), lambda b: (b, 0, 0)), name="hyena_ctx_fwd")
            ynew = _lead(mats['hc_i'], [xc], [pl.BlockSpec((1, 2 * nc, c), lambda b: (b, 0, 0))],
                         (pairs,), jax.ShapeDtypeStruct((out_rows, c), F32),
                         pl.BlockSpec((nc, c), lambda b, oblk0=oblk0: (oblk0 + b, 0)),
                         kspec=(kspec_c, pl.BlockSpec((1, 2 * nc, c), lambda b, o=o: (o, 0, 0))),
                         epi=[(z3, pl.BlockSpec((1, nc, c), lambda b, o=o: (o + 1, blk0 + b, 0))),
                              (cin, cspec_in),
                              (bias3, pl.BlockSpec((1, 1, c), lambda b, o=o: (o, 0, 0)))],
                         alias_to=ynew, name="hyena_ctx_inv")
        y = ynew
    return y


def _route_tile(lt, br, base, tri):
    tm = lt.shape[1]
    aff = jax.nn.sigmoid(lt)
    biased = aff + br
    b = [biased[e:e + 1, :] for e in range(N_EXPERTS)]
    a = [aff[e:e + 1, :] for e in range(N_EXPERTS)]
    epg = EXPERTS_PER_GROUP
    scores = []
    for g in range(N_GROUPS):
        x0, x1, x2, x3 = b[epg * g:epg * g + epg]
        s1, t1 = jnp.maximum(x0, x1), jnp.minimum(x0, x1)
        s2, t2 = jnp.maximum(x2, x3), jnp.minimum(x2, x3)
        scores.append(jnp.maximum(s1, s2) + jnp.maximum(jnp.minimum(s1, s2), jnp.maximum(t1, t2)))
    best = scores[0]
    gsel = jnp.zeros((1, tm), jnp.int32)
    for g in range(1, N_GROUPS):
        gsel = jnp.where(scores[g] > best, g, gsel)
        best = jnp.maximum(best, scores[g])

    def pick(rows, j):
        out = rows[j]
        for g in range(1, N_GROUPS):
            out = jnp.where(gsel == g, rows[epg * g + j], out)
        return out

    v = [pick(b, j) for j in range(epg)]
    av = [pick(a, j) for j in range(epg)]
    i1 = jnp.zeros((1, tm), jnp.int32)
    m1 = v[0]
    for j in range(1, epg):
        i1 = jnp.where(v[j] > m1, j, i1)
        m1 = jnp.maximum(m1, v[j])
    neg = jnp.float32(-3.0e38)
    i2 = jnp.zeros((1, tm), jnp.int32)
    m2 = jnp.full((1, tm), neg, F32)
    for j in range(epg):
        cand = jnp.where(i1 == j, neg, v[j])
        take = cand > m2
        i2 = jnp.where(take, j, i2)
        m2 = jnp.where(take, cand, m2)

    def sel(rows, idx):
        out = rows[0]
        for j in range(1, epg):
            out = jnp.where(idx == j, rows[j], out)
        return out

    a1, a2 = sel(av, i1), sel(av, i2)
    den = a1 + a2
    e1 = gsel * epg + i1
    e2 = gsel * epg + i2
    eio = lax.broadcasted_iota(jnp.int32, (N_EXPERTS, tm), 0)
    oh1 = jnp.where(eio == e1, 1.0, 0.0)
    oh2 = jnp.where(eio == e2, 1.0, 0.0)
    oh = oh1 + oh2
    tot = base + _dot(oh.astype(BF16), tri)
    r1 = jnp.sum(oh1 * tot, axis=0, keepdims=True)
    r2 = jnp.sum(oh2 * tot, axis=0, keepdims=True)
    new_base = base + jnp.sum(oh, axis=1, keepdims=True)
    return (e1, e2), (a1 / den, a2 / den), (r1.astype(jnp.int32), r2.astype(jnp.int32)), new_base


def _merge_kernel(x_ref, ya_ref, yf_ref, yh_ref, gt_ref, wa_ref, wf_ref, wh_ref, wo_ref, g1_ref,
                  gn_ref, sh_ref, sc_ref, wrt_ref, br_ref, xo_ref, h2_ref, e_ref, w_ref, r_ref, cnt_ref):
    d = D_MODEL
    gate = lambda k: jax.nn.sigmoid(gt_ref[:, k * d:(k + 1) * d].astype(F32))
    merged = gate(0) * _dot(ya_ref[...], wa_ref[...])
    merged += gate(1) * _dot(yf_ref[...].astype(BF16), wf_ref[...])
    merged += gate(2) * _dot(yh_ref[...].astype(BF16), wh_ref[...])
    xn = x_ref[...] + g1_ref[0, 0] * _dot(merged.astype(BF16), wo_ref[...])
    xo_ref[...] = xn
    h2f = _rms_mod(xn, gn_ref[...], sh_ref[0, 0], sc_ref[0, 0])
    _rows_to_tiles(h2_ref, h2f, 0, h2f.shape[0])
    h2 = h2f.astype(BF16)

    @pl.when(pl.program_id(0) == 0)
    def _():
        cnt_ref[...] = jnp.zeros_like(cnt_ref)

    tm = h2.shape[0]
    lt = lax.dot_general(wrt_ref[...], h2, (((1,), (1,)), ((), ())), preferred_element_type=F32)
    tri = jnp.where(lax.broadcasted_iota(jnp.int32, (tm, tm), 0) < lax.broadcasted_iota(jnp.int32, (tm, tm), 1),
                    1.0, 0.0).astype(BF16)
    es, ws, rs, new_base = _route_tile(lt, br_ref[...], cnt_ref[:, 0:1], tri)
    e_ref[0:1, :], e_ref[1:2, :] = es
    w_ref[0:1, :], w_ref[1:2, :] = ws
    r_ref[0:1, :], r_ref[1:2, :] = rs
    cnt_ref[...] = jnp.broadcast_to(new_base, cnt_ref.shape)


def _merge(x, ya, yf, yh, p, wa, wf, wh, wo, mod4, gain2, wrt, br, n_tok):
    tm = TM
    row = _mod_row(tm)
    full = lambda a: pl.BlockSpec(a.shape, lambda i: (0,) * a.ndim)
    modspec = lambda k: pl.BlockSpec((1, 1, 1, D_MODEL), lambda i: (row(i), k, 0, 0))
    tok = lambda w: pl.BlockSpec((tm, w), lambda i: (i, 0))
    lane = pl.BlockSpec((TOP_K, tm), lambda i: (0, i))
    return pl.pallas_call(
        _merge_kernel,
        grid=(n_tok // tm,),
        in_specs=[tok(D_MODEL), tok(ATTN_WIDTH), tok(FNET_WIDTH), tok(HYENA_WIDTH),
                  pl.BlockSpec((pl.Element(tm), pl.Element(3 * D_MODEL)), lambda i: (i * tm, PG_OFF)),
                  full(wa), full(wf), full(wh), full(wo), modspec(2), full(gain2), modspec(3), modspec(4),
                  full(wrt), full(br)],
        out_specs=[tok(D_MODEL), pl.BlockSpec((tm * ROW_TILE, 128), lambda i: (i, 0)), lane, lane, lane,
                   pl.BlockSpec((N_EXPERTS, 128), lambda i: (0, 0))],
        out_shape=[jax.ShapeDtypeStruct((n_tok, D_MODEL), F32),
                   jax.ShapeDtypeStruct((n_tok * ROW_TILE, 128), F32),
                   jax.ShapeDtypeStruct((TOP_K, n_tok), jnp.int32),
                   jax.ShapeDtypeStruct((TOP_K, n_tok), F32),
                   jax.ShapeDtypeStruct((TOP_K, n_tok), jnp.int32),
                   jax.ShapeDtypeStruct((N_EXPERTS, 128), F32)],
        compiler_params=_cparams(("arbitrary",)),
        name="merge_out_norm_route",
    )(x, ya, yf, yh, p, wa, wf, wh, wo, mod4, gain2, mod4, mod4, wrt, br)


def _row_copy(src_ref, row, dst_ref, r, sem):
    return pltpu.make_async_copy(src_ref.at[pl.ds(row, 1), :], dst_ref.at[pl.ds(r, 1), :], sem)


ROW_TILE = D_MODEL // 128


def _rows_from_tiles(ref, n):
    return jnp.concatenate([ref[pl.ds(s, n, stride=ROW_TILE), :] for s in range(ROW_TILE)], axis=1)


def _rows_to_tiles(ref, val, col0, n):
    for j in range(val.shape[1] // 128):
        ref[pl.ds(col0 // 128 + j, n, stride=ROW_TILE), :] = val[:, j * 128:(j + 1) * 128]


def _expert_kernel(be_ref, na_ref, rt_cur_ref, rt_next_ref, h2_ref, wg_ref, wu_ref, wd_ref, o_ref,
                   xbuf, sem, wg_s, wu_s, wd_s):
    i = pl.program_id(0)
    n_act = na_ref[0]
    bm = xbuf.shape[1]
    active = i < n_act

    def gather(rt_ref, slot):
        def body(j, carry):
            r0 = pl.multiple_of(j * 8, 8)
            tile = xbuf.at[slot, pl.ds(r0, 8)]
            for u in range(8):
                _row_copy(h2_ref, rt_ref[0, 0, r0 + u], tile, u, sem.at[slot]).start()
            return carry

        lax.fori_loop(0, bm // 8, body, 0)

    @pl.when((i == 0) & active)
    def _():
        gather(rt_cur_ref, 0)

    @pl.when(i + 1 < n_act)
    def _():
        gather(rt_next_ref, (i + 1) % 2)

    prev = be_ref[jnp.maximum(i - 1, 0)]

    @pl.when(active & ((i == 0) | (be_ref[i] != prev)))
    def _():
        wg_s[...] = wg_ref[0].astype(BF16)
        wu_s[...] = wu_ref[0].astype(BF16)
        wd_s[...] = wd_ref[0].astype(BF16)

    @pl.when(active)
    def _():
        slot = i % 2
        pltpu.make_async_copy(h2_ref.at[pl.ds(0, bm), :], xbuf.at[slot], sem.at[slot]).wait()
        x = xbuf[slot].astype(BF16)
        g = _dot(x, wg_s[...])
        u = _dot(x, wu_s[...])
        h = (g * jax.nn.sigmoid(g)) * u
        o_ref[...] = _dot(h.astype(BF16), wd_s[...])

    @pl.when(jnp.logical_not(active))
    def _():
        o_ref[...] = jnp.zeros_like(o_ref)


def _experts(h2, row_tok, blk_expert, n_active, wg, wu, wd):
    bm = EXPERT_BM
    n_blk = row_tok.shape[0] // bm
    rt = row_tok.reshape(n_blk, 1, bm)
    wspec = lambda k, n: pl.BlockSpec((1, k, n), lambda i, be, na: (be[i], 0, 0))
    return pl.pallas_call(
        _expert_kernel,
        grid_spec=pltpu.PrefetchScalarGridSpec(
            num_scalar_prefetch=2,
            grid=(n_blk,),
            in_specs=[pl.BlockSpec((1, 1, bm), lambda i, be, na: (i, 0, 0), memory_space=pltpu.SMEM),
                      pl.BlockSpec((1, 1, bm), lambda i, be, na: (jnp.minimum(i + 1, n_blk - 1), 0, 0),
                                   memory_space=pltpu.SMEM),
                      pl.BlockSpec(memory_space=pl.ANY),
                      wspec(D_MODEL, EXPERT_FF), wspec(D_MODEL, EXPERT_FF), wspec(EXPERT_FF, D_MODEL)],
            out_specs=pl.BlockSpec((bm, D_MODEL), lambda i, be, na: (i, 0)),
            scratch_shapes=[pltpu.VMEM((2, bm, D_MODEL), F32), pltpu.SemaphoreType.DMA((2,)),
                            pltpu.VMEM((D_MODEL, EXPERT_FF), BF16), pltpu.VMEM((D_MODEL, EXPERT_FF), BF16),
                            pltpu.VMEM((EXPERT_FF, D_MODEL), BF16)]),
        out_shape=jax.ShapeDtypeStruct((row_tok.shape[0], D_MODEL), F32),
        compiler_params=_cparams(("arbitrary",)),
        name="moe_experts",
    )(blk_expert, n_active, rt, rt, h2, wg, wu, wd)


def _dispatch(e_idx, rank, counts, n_tok):
    bm = EXPERT_BM
    counts = counts.astype(jnp.int32)
    padded = (counts + bm - 1) // bm * bm
    pad_end = jnp.cumsum(padded)
    pad_start = pad_end - padded
    experts = jnp.arange(N_EXPERTS, dtype=jnp.int32)
    start = jnp.sum(jnp.where(e_idx[..., None] == experts, pad_start, 0), axis=-1)
    dest = start + rank
    n_rows = -(-(n_tok * TOP_K) // bm) * bm + N_EXPERTS * bm
    n_blk = n_rows // bm
    tok = jnp.tile(jnp.arange(n_tok, dtype=jnp.int32), TOP_K)
    row_tok = jnp.zeros((n_rows,), jnp.int32).at[dest.reshape(-1)].set(tok)
    blk_start = jnp.arange(n_blk, dtype=jnp.int32) * bm
    blk_expert = jnp.minimum(jnp.sum((blk_start[:, None] >= pad_end[None, :]).astype(jnp.int32), axis=-1),
                             N_EXPERTS - 1)
    n_active = (pad_end[-1] // bm).astype(jnp.int32).reshape(1)
    return row_tok, dest, blk_expert, n_active


def _combine_kernel(d_cur_ref, d_next_ref, x_ref, w_ref, g2_ref, gf_ref, ys_ref, o_ref, ybuf, sem,
                    *, final, n_tiles):
    i = pl.program_id(0)
    tm = x_ref.shape[0]

    def gather(d_ref, slot):
        def body(j, carry):
            r0 = pl.multiple_of(j * 8, 8)
            for k in range(TOP_K):
                tile = ybuf.at[slot, k, pl.ds(r0, 8)]
                for u in range(8):
                    _row_copy(ys_ref, d_ref[0, k, r0 + u], tile, u, sem.at[slot]).start()
            return carry

        lax.fori_loop(0, tm // 8, body, 0)

    @pl.when(i == 0)
    def _():
        gather(d_cur_ref, 0)

    @pl.when(i + 1 < n_tiles)
    def _():
        gather(d_next_ref, (i + 1) % 2)

    slot = i % 2
    for k in range(TOP_K):
        pltpu.make_async_copy(ys_ref.at[pl.ds(0, tm), :], ybuf.at[slot, k], sem.at[slot]).wait()
    w = w_ref[...]
    moe = ybuf[slot, 0] * w[:, 0:1] + ybuf[slot, 1] * w[:, 1:2]
    xn = x_ref[...] + g2_ref[0, 0] * moe
    if final:
        y = xn * lax.rsqrt(jnp.mean(xn * xn, axis=-1, keepdims=True) + EPS)
        xn = y * gf_ref[...]
    o_ref[...] = xn


def _combine(x, ys, dest, w_sel, mod4, gain_final, n_tok, final):
    tm = 256
    n_tiles = n_tok // tm
    row = _mod_row(tm)
    tok = lambda w: pl.BlockSpec((tm, w), lambda i: (i, 0))
    d3 = jnp.transpose(dest.reshape(TOP_K, n_tiles, tm), (1, 0, 2))
    return pl.pallas_call(
        functools.partial(_combine_kernel, final=final, n_tiles=n_tiles),
        grid=(n_tiles,),
        in_specs=[pl.BlockSpec((1, TOP_K, tm), lambda i: (i, 0, 0), memory_space=pltpu.SMEM),
                  pl.BlockSpec((1, TOP_K, tm), lambda i: (jnp.minimum(i + 1, n_tiles - 1), 0, 0),
                               memory_space=pltpu.SMEM),
                  tok(D_MODEL), tok(TOP_K),
                  pl.BlockSpec((1, 1, 1, D_MODEL), lambda i: (row(i), 5, 0, 0)),
                  pl.BlockSpec((1, D_MODEL), lambda i: (0, 0)),
                  pl.BlockSpec(memory_space=pl.ANY)],
        out_specs=tok(D_MODEL),
        out_shape=jax.ShapeDtypeStruct((n_tok if final else T_ALL, D_MODEL), F32),
        scratch_shapes=[pltpu.VMEM((2, TOP_K, tm, D_MODEL), F32), pltpu.SemaphoreType.DMA((2,))],
        compiler_params=_cparams(("arbitrary",)),
        name="moe_combine",
    )(d3, d3, x, w_sel, mod4, gain_final, ys)


def _moe_kernel(be_ref, na_ref, ip_ref, ic_ref, sc_ref, sn_ref, h2_ref, wg_ref, wu_ref, wd_ref, y_ref,
                xbuf, obuf, gsem, ssem, wg_s, wu_s, wd_s):
    i = pl.program_id(0)
    n_act = na_ref[0]
    sub = ROW_TILE
    bm = xbuf.shape[1] // sub
    active = i < n_act
    slot = i % 2
    other = (i + 1) % 2
    n_chunk = 4
    cw = EXPERT_FF // n_chunk
    rows_per = bm // n_chunk

    def tile_rows(ref, row):
        return ref.at[pl.ds(pl.multiple_of(row * sub, sub), sub), :]

    def gather(src_ref, dst_slot, lo, hi):
        for r in range(lo, hi):
            pltpu.make_async_copy(tile_rows(h2_ref, src_ref[0, 0, r]), xbuf.at[dst_slot, pl.ds(r * sub, sub), :],
                                  gsem.at[dst_slot]).start(priority=r % 2)

    def scatter(info_ref, src_slot, lo, hi):
        for r in range(lo, hi):
            pltpu.make_async_copy(obuf.at[src_slot, pl.ds(r * sub, sub), :], tile_rows(y_ref, info_ref[0, 0, r]),
                                  ssem.at[src_slot]).start(priority=r % 2)

    def wait_gather(s):
        pltpu.make_async_copy(h2_ref.at[pl.ds(0, bm * sub), :], xbuf.at[s], gsem.at[s]).wait()

    def wait_scatter(s):
        pltpu.make_async_copy(obuf.at[s], y_ref.at[pl.ds(0, bm * sub), :], ssem.at[s]).wait()

    @pl.when((i == 0) & active)
    def _():
        obuf[...] = jnp.zeros_like(obuf)
        gather(sc_ref, 0, 0, bm)

    prev = be_ref[jnp.maximum(i - 1, 0)]

    @pl.when(active & ((i == 0) | (be_ref[i] != prev)))
    def _():
        wg_s[...] = wg_ref[...].astype(BF16)
        wu_s[...] = wu_ref[...].astype(BF16)
        wd_s[...] = wd_ref[...].astype(BF16)

    @pl.when(active)
    def _():
        wait_gather(slot)
        x = _rows_from_tiles(xbuf.at[slot], bm).astype(BF16)
        hs = []
        for c in range(n_chunk):
            g = _dot(x, wg_s[:, c * cw:(c + 1) * cw])
            u = _dot(x, wu_s[:, c * cw:(c + 1) * cw])
            hs.append(((g * jax.nn.sigmoid(g)) * u).astype(BF16))
            gather(sn_ref, other, c * rows_per, (c + 1) * rows_per)
        h = jnp.concatenate(hs, axis=1)

        @pl.when(i > 0)
        def _():
            wait_scatter(slot)

        for c in range(n_chunk):
            _rows_to_tiles(obuf.at[slot], _dot(h, wd_s[:, c * cw:(c + 1) * cw]), c * cw, bm)
            scatter(ip_ref, other, c * rows_per, (c + 1) * rows_per)

    @pl.when(i == n_act - 1)
    def _():
        wait_gather(other)
        wait_scatter(other)
        scatter(ic_ref, slot, 0, bm)
        wait_scatter(slot)


def _moe(h2, info, blk_expert, n_active, wg, wu, wd, layer, n_tok):
    bm = EXPERT_BM
    n_rows = info.shape[0]
    n_blk = n_rows // bm
    info3 = info.reshape(n_blk, 1, bm)
    src3 = jnp.where(info3 >= TOP_K * n_tok, 0, info3 % n_tok)
    wspec = lambda k, n: pl.BlockSpec((None, None, k, n), lambda i, be, na: (layer, be[i], 0, 0))
    ispec = lambda f: pl.BlockSpec((1, 1, bm), lambda i, be, na: (f(i), 0, 0), memory_space=pltpu.SMEM)
    return pl.pallas_call(
        _moe_kernel,
        grid_spec=pltpu.PrefetchScalarGridSpec(
            num_scalar_prefetch=2,
            grid=(n_blk,),
            in_specs=[ispec(lambda i: jnp.maximum(i - 1, 0)), ispec(lambda i: i),
                      ispec(lambda i: i), ispec(lambda i: jnp.minimum(i + 1, n_blk - 1)),
                      pl.BlockSpec(memory_space=pl.ANY),
                      wspec(D_MODEL, EXPERT_FF), wspec(D_MODEL, EXPERT_FF), wspec(EXPERT_FF, D_MODEL)],
            out_specs=pl.BlockSpec(memory_space=pl.ANY),
            scratch_shapes=[pltpu.VMEM((2, bm * ROW_TILE, 128), F32), pltpu.VMEM((2, bm * ROW_TILE, 128), F32),
                            pltpu.SemaphoreType.DMA((2,)), pltpu.SemaphoreType.DMA((2,)),
                            pltpu.VMEM((D_MODEL, EXPERT_FF), BF16), pltpu.VMEM((D_MODEL, EXPERT_FF), BF16),
                            pltpu.VMEM((EXPERT_FF, D_MODEL), BF16)]),
        out_shape=jax.ShapeDtypeStruct(((TOP_K * n_tok + n_rows) * ROW_TILE, 128), F32),
        compiler_params=_cparams(("arbitrary",)),
        name="moe_experts",
    )(blk_expert, n_active, info3, info3, src3, src3, h2, wg, wu, wd)


def _dispatch_info(e_idx, rank, counts, n_tok):
    bm = EXPERT_BM
    counts = counts.astype(jnp.int32)
    padded = (counts + bm - 1) // bm * bm
    pad_end = jnp.cumsum(padded)
    pad_start = pad_end - padded
    experts = jnp.arange(N_EXPERTS, dtype=jnp.int32)
    start = jnp.sum(jnp.where(e_idx[..., None] == experts, pad_start, 0), axis=-1)
    dest = start + rank
    n_rows = -(-(n_tok * TOP_K) // bm) * bm + N_EXPERTS * bm
    n_blk = n_rows // bm
    spill = TOP_K * n_tok + jnp.arange(n_rows, dtype=jnp.int32)
    info = spill.at[dest.reshape(-1)].set(jnp.arange(TOP_K * n_tok, dtype=jnp.int32))
    blk_start = jnp.arange(n_blk, dtype=jnp.int32) * bm
    blk_expert = jnp.minimum(jnp.sum((blk_start[:, None] >= pad_end[None, :]).astype(jnp.int32), axis=-1),
                             N_EXPERTS - 1)
    n_active = (pad_end[-1] // bm).astype(jnp.int32).reshape(1)
    return info, blk_expert, n_active


def _residual_kernel(x_ref, y0_ref, y1_ref, w_ref, g2_ref, gf_ref, o_ref, *, final):
    w = w_ref[...]
    tm = x_ref.shape[0]
    moe = _rows_from_tiles(y0_ref, tm) * w[:, 0:1] + _rows_from_tiles(y1_ref, tm) * w[:, 1:2]
    xn = x_ref[...] + g2_ref[0, 0] * moe
    if final:
        y = xn * lax.rsqrt(jnp.mean(xn * xn, axis=-1, keepdims=True) + EPS)
        xn = y * gf_ref[...]
    o_ref[...] = xn


def _residual(x, y, w_sel, mod4, gain_final, n_tok, final):
    tm = TM
    n_tiles = n_tok // tm
    row = _mod_row(tm)
    tok = lambda w: pl.BlockSpec((tm, w), lambda i: (i, 0))
    return pl.pallas_call(
        functools.partial(_residual_kernel, final=final),
        grid=(n_tiles,),
        in_specs=[tok(D_MODEL), pl.BlockSpec((tm * ROW_TILE, 128), lambda i: (i, 0)),
                  pl.BlockSpec((tm * ROW_TILE, 128), lambda i: (n_tiles + i, 0)), tok(TOP_K),
                  pl.BlockSpec((1, 1, 1, D_MODEL), lambda i: (row(i), 5, 0, 0)),
                  pl.BlockSpec((1, D_MODEL), lambda i: (0, 0))],
        out_specs=tok(D_MODEL),
        out_shape=jax.ShapeDtypeStruct((n_tok if final else T_ALL, D_MODEL), F32),
        compiler_params=_cparams(("parallel",)),
        name="moe_residual",
    )(x, y, y, w_sel, mod4, gain_final)


def kernel(x, c, ctx, c_ctx, w_mod, b_mod, norm_mix, norm_ffn, w_in, attn_sink, conv_w, conv_b, filt_w1, filt_b1, filt_freq, filt_w2, filt_b2, filt_w3, hyena_bias, w_branch_attn, w_branch_fnet, w_branch_hyena, w_out, w_router, b_router, w_exp_gate, w_exp_up, w_exp_down, norm_final):
    mats = _dft_mats()
    cos_t, sin_t = _rope_tables()
    c8 = jnp.concatenate([c, c_ctx[None, :], jnp.zeros((8 - BATCH - 1, D_MODEL), F32)], axis=0)
    mod_all = _modulation(c8, w_mod, b_mod)
    xa = jnp.concatenate([x.reshape(T_LAT, D_MODEL), ctx.reshape(T_CTX, D_MODEL)], axis=0)
    wrt = w_router.T.astype(BF16)
    br = b_router.astype(F32).reshape(N_EXPERTS, 1)
    gain_final = norm_final.reshape(1, D_MODEL)
    out = None
    for l in range(DEPTH):
        last = l == DEPTH - 1
        with_ctx = not last
        n_tok = T_LAT if last else T_ALL
        mod4 = mod_all[l].reshape(8, N_MOD, 1, D_MODEL)
        p = _norm_proj(xa, norm_mix[l].reshape(1, D_MODEL), mod4, w_in, l, T_ALL)
        ya = _attention(p, attn_sink[l], cos_t, sin_t, with_ctx)
        yf = _fourier_mix(p, mats, with_ctx)
        z3 = _short_conv(p, conv_w[l], conv_b[l], with_ctx)
        filt = (filt_w1[l], filt_b1[l], filt_freq[l], filt_w2[l], filt_b2[l], filt_w3[l])
        yh = _hyena_mix(z3, filt, hyena_bias[l], mats, with_ctx)
        xa, h2, e_idx, w_sel, rank, cnt = _merge(
            xa, ya, yf, yh, p, w_branch_attn[l].astype(BF16), w_branch_fnet[l].astype(BF16),
            w_branch_hyena[l].astype(BF16), w_out[l].astype(BF16), mod4,
            norm_ffn[l].reshape(1, D_MODEL), wrt, br, n_tok)
        info, blk_expert, n_active = _dispatch_info(e_idx, rank, cnt[:, 0], n_tok)
        y = _moe(h2, info, blk_expert, n_active, w_exp_gate, w_exp_up, w_exp_down, l, n_tok)
        res = _residual(xa, y, w_sel.T, mod4, gain_final, n_tok, last)
        if last:
            out = res
        else:
            xa = res
    return out.reshape(BATCH, SEQ, D_MODEL)
```

```python
import functools
import math

import jax
import jax.numpy as jnp
from jax import lax
from jax.experimental import pallas as pl
from jax.experimental.pallas import tpu as pltpu

F32 = jnp.float32
BF16 = jnp.bfloat16

D_MODEL = 1024
BATCH = 4
SEQ = 4096
DEPTH = 4
GRID_W = 64
CTX_LEN = 256
EPS = 1e-6
N_MOD = 6

HEAD_DIM = 64
N_Q_HEADS = 8
N_KV_HEADS = 2
Q_PER_KV = N_Q_HEADS // N_KV_HEADS
ATTN_BLOCK = 128
ROPE_BASE = 10000.0

FNET_GROUPS = 4
FNET_GROUP_DIM = 128
FNET_WIDTH = FNET_GROUPS * FNET_GROUP_DIM

HYENA_WIDTH = 512
HYENA_ORDER = 2
FILTER_EMB = 33
FILTER_BANDS = (FILTER_EMB - 1) // 2
FILTER_HIDDEN = 64
DECAY_TARGET = 1e-2
FAST_DECAY_PCT = 0.3
SLOW_DECAY_PCT = 1.5

ATTN_WIDTH = N_Q_HEADS * HEAD_DIM
KV_WIDTH = N_KV_HEADS * HEAD_DIM
Q_OFF = 0
K_OFF = Q_OFF + ATTN_WIDTH
V_OFF = K_OFF + KV_WIDTH
F_OFF = V_OFF + KV_WIDTH
H_OFF = F_OFF + FNET_WIDTH
G_OFF = H_OFF + (HYENA_ORDER + 1) * HYENA_WIDTH
IN_WIDTH = G_OFF + 3 * D_MODEL

N_EXPERTS = 16
N_GROUPS = 4
EXPERTS_PER_GROUP = N_EXPERTS // N_GROUPS
TOP_K = 2
EXPERT_FF = 1024

T_LAT = BATCH * SEQ
T_CTX = BATCH * CTX_LEN
T_ALL = T_LAT + T_CTX

PG_OFF = G_OFF
PH_OFF = H_OFF
PF_OFF = F_OFF
PQ_OFF = Q_OFF
PK_OFF = K_OFF
PV_OFF = V_OFF

HY_N = 2 * SEQ
HY_N2 = 64
HY_N1 = HY_N // HY_N2
FN_N = 64

PITCH = HY_N2 + 8
SEQ_PAD = SEQ // HY_N2 * PITCH
T_PAD = BATCH * SEQ_PAD + T_CTX
UNROLL = 8
UNROLL_SLAB = 16
TM = 512
EXPERT_BM = 512
VMEM_LIMIT = 52 * 1024 * 1024


def _cparams(sem, vmem=VMEM_LIMIT):
    return pltpu.CompilerParams(dimension_semantics=sem, vmem_limit_bytes=vmem)


def _dot(a, b):
    return jnp.dot(a, b, preferred_element_type=F32)


def _cis(expo, n):
    ang = (2.0 * math.pi / n) * jnp.mod(expo, n).astype(F32)
    return jnp.cos(ang), jnp.sin(ang)


def _real_form(gr, gi):
    return jnp.concatenate([jnp.concatenate([gr, -gi], axis=-1), jnp.concatenate([gi, gr], axis=-1)], axis=-2)


def _dft_mats():
    ar = lambda n: jnp.arange(n, dtype=jnp.int32)
    m = {}
    c, s = _cis(ar(HY_N1)[:, None] * ar(HY_N1 // 2)[None, :], HY_N1)
    m['hy_lead_f'] = _real_form(c, -s).astype(BF16)
    c, s = _cis(ar(HY_N1 // 2)[:, None] * ar(HY_N1)[None, :], HY_N1)
    m['hy_lead_i'] = _real_form(c, s).astype(BF16)
    c, s = _cis(ar(HY_N1)[:, None] * ar(HY_N1 // 2)[None, :], HY_N1)
    m['hy_lead_kh'] = jnp.concatenate([c, -s], axis=0).astype(BF16)
    a = ar(HY_N1)[:, None, None]
    k2 = ar(HY_N2)[None, :, None]
    n2 = ar(HY_N2)[None, None, :]
    c, s = _cis(n2 * (a + HY_N1 * k2), HY_N)
    m['hy_slab_f2'] = jnp.concatenate([c, -s], axis=-1).astype(BF16)
    ct = jnp.swapaxes(c, 1, 2) * (1.0 / HY_N)
    st = jnp.swapaxes(s, 1, 2) * (1.0 / HY_N)
    m['hy_slab_i2'] = jnp.concatenate([ct, st], axis=-1).astype(BF16)
    nc = 2 * CTX_LEN
    c, s = _cis(ar(nc)[:, None] * ar(CTX_LEN)[None, :], nc)
    m['hc_f'] = _real_form(c, -s).astype(BF16)
    c, s = _cis(ar(CTX_LEN)[:, None] * ar(nc)[None, :], nc)
    m['hc_i'] = _real_form(c * (1.0 / nc), s * (1.0 / nc)).astype(BF16)
    c, s = _cis(ar(nc)[:, None] * ar(CTX_LEN)[None, :], nc)
    m['hc_kh'] = jnp.concatenate([c, -s], axis=0).astype(BF16)
    c, s = _cis(ar(FNET_GROUP_DIM)[:, None] * ar(FNET_GROUP_DIM)[None, :], FNET_GROUP_DIM)
    m['fn_chan'] = jnp.concatenate([c, -s], axis=1).astype(BF16)
    c, s = _cis(ar(FN_N)[:, None] * ar(FN_N)[None, :], FN_N)
    m['fn_lead'] = _real_form(c, -s).astype(BF16)
    a = ar(FN_N)[:, None, None]
    k1 = ar(FN_N)[None, :, None]
    n1 = ar(FN_N)[None, None, :]
    scale = 1.0 / math.sqrt(SEQ * FNET_GROUP_DIM)
    c, s = _cis(n1 * (a + FN_N * k1), SEQ)
    m['fn_slab'] = jnp.concatenate([c * scale, s * scale], axis=-1).astype(BF16)
    scale = 1.0 / math.sqrt(CTX_LEN * FNET_GROUP_DIM)
    c, s = _cis(ar(CTX_LEN)[:, None] * ar(CTX_LEN)[None, :], CTX_LEN)
    m['fc'] = jnp.concatenate([c * scale, s * scale], axis=-1).astype(BF16)
    return m


def _mod_kernel(c_ref, w_ref, b_ref, o_ref):
    c = c_ref[...]
    s = c * jax.nn.sigmoid(c)
    o_ref[0] = _dot(s.astype(BF16), w_ref[0].astype(BF16)) + b_ref[0]


def _modulation(c8, w_mod, b_mod):
    tn = 1536
    n = N_MOD * D_MODEL
    return pl.pallas_call(
        _mod_kernel,
        grid=(DEPTH, n // tn),
        in_specs=[pl.BlockSpec((8, D_MODEL), lambda l, j: (0, 0)),
                  pl.BlockSpec((1, D_MODEL, tn), lambda l, j: (l, 0, j)),
                  pl.BlockSpec((1, 1, tn), lambda l, j: (l, 0, j))],
        out_specs=pl.BlockSpec((1, 8, tn), lambda l, j: (l, 0, j)),
        out_shape=jax.ShapeDtypeStruct((DEPTH, 8, n), F32),
        compiler_params=_cparams(("parallel", "parallel")),
        name="adaln_modulation",
    )(c8, w_mod, b_mod.reshape(DEPTH, 1, n))


def _mod_row(tm):
    tiles_per_batch = SEQ // tm
    return lambda i: jnp.minimum(i // tiles_per_batch, BATCH)


def _rms_mod(x, g, sh, sc):
    y = x * lax.rsqrt(jnp.mean(x * x, axis=-1, keepdims=True) + EPS)
    return (y * g) * (1.0 + sc) + sh


def _norm_proj_kernel(x_ref, g_ref, sh_ref, sc_ref, w_ref, o_ref, w_s):
    @pl.when(pl.program_id(1) == 0)
    def _():
        w_s[...] = w_ref[...].astype(BF16)

    h = _rms_mod(x_ref[...], g_ref[...], sh_ref[0, 0], sc_ref[0, 0]).astype(BF16)
    o_ref[...] = _dot(h, w_s[...]).astype(o_ref.dtype)


def _norm_proj(x, gain, mod4, w, layer, n_tok):
    tm = TM
    n_out = w.shape[2]
    tn = n_out // 2
    row = _mod_row(tm)
    return pl.pallas_call(
        _norm_proj_kernel,
        grid=(n_out // tn, n_tok // tm),
        in_specs=[pl.BlockSpec((tm, D_MODEL), lambda j, i: (i, 0)),
                  pl.BlockSpec((1, D_MODEL), lambda j, i: (0, 0)),
                  pl.BlockSpec((1, 1, 1, D_MODEL), lambda j, i: (row(i), 0, 0, 0)),
                  pl.BlockSpec((1, 1, 1, D_MODEL), lambda j, i: (row(i), 1, 0, 0)),
                  pl.BlockSpec((None, D_MODEL, tn), lambda j, i: (layer, 0, j))],
        out_specs=pl.BlockSpec((tm, tn), lambda j, i: (i, j)),
        out_shape=jax.ShapeDtypeStruct((T_ALL, n_out), BF16),
        scratch_shapes=[pltpu.VMEM((D_MODEL, tn), BF16)],
        compiler_params=_cparams(("arbitrary", "arbitrary")),
        name="norm_in_proj",
    )(x, gain, mod4, mod4, w)


def _softmax_pv(qh, k_parts, v_parts, masks, sink):
    nt = (((1,), (1,)), ((), ()))
    scores = []
    for kp, mk in zip(k_parts, masks):
        s = lax.dot_general(qh, kp, nt, preferred_element_type=F32)
        if mk is not None:
            s = jnp.where(mk, s, -1e30)
        scores.append(s)
    m = sink
    for s in scores:
        m = jnp.maximum(m, jnp.max(s, axis=-1, keepdims=True))
    es = [jnp.exp(s - m) for s in scores]
    den = jnp.exp(sink - m)
    for e in es:
        den = den + jnp.sum(e, axis=-1, keepdims=True)
    inv = 1.0 / den
    o = None
    for e, vp in zip(es, v_parts):
        t = _dot((e * inv).astype(BF16), vp)
        o = t if o is None else o + t
    return o


def _attn_kernel(sink_ref, q_ref, km_ref, k0_ref, kp_ref, vm_ref, v0_ref, vp_ref, kc_ref, vc_ref,
                 cos_ref, sin_ref, psw_ref, p64_ref, o_ref, *, nb):
    n = pl.program_id(1)
    blk = ATTN_BLOCK
    psw = psw_ref[...]
    p64 = p64_ref[...]

    def rope(xb, blk_idx):
        r0 = pl.multiple_of(blk_idx * blk, blk)
        return xb.astype(F32) * cos_ref[pl.ds(r0, blk), :] + _dot(xb, psw) * sin_ref[pl.ds(r0, blk), :]

    nm = jnp.maximum(n - 1, 0)
    npl = jnp.minimum(n + 1, nb - 1)
    kall = jnp.concatenate([rope(km_ref[...], nm), rope(k0_ref[...], n), rope(kp_ref[...], npl),
                            kc_ref[...].astype(F32)], axis=0)
    vall = jnp.concatenate([vm_ref[...], v0_ref[...], vp_ref[...], vc_ref[...]], axis=0)
    nk = kall.shape[0]
    lo = lax.broadcasted_iota(jnp.int32, (nk, 128), 1) < HEAD_DIM

    r = lax.broadcasted_iota(jnp.int32, (blk, blk), 0)
    cidx = lax.broadcasted_iota(jnp.int32, (blk, blk), 1)
    ok_prev = jnp.where(cidx >= r, (n > 0).astype(jnp.int32), 0) > 0
    ok_next = jnp.where(cidx <= r, (n < nb - 1).astype(jnp.int32), 0) > 0
    neg = jnp.float32(-1e30)

    scale = HEAD_DIM ** -0.5
    q2 = [(rope(q_ref[:, p * 128:(p + 1) * 128], n) * scale).astype(BF16) for p in range(N_Q_HEADS // 2)]
    nt = (((1,), (1,)), ((), ()))
    n_pair = Q_PER_KV // 2
    for h in range(N_KV_HEADS):
        kh = (jnp.where(lo, kall, 0.0) if h == 0 else jnp.where(lo, 0.0, kall)).astype(BF16)
        vh = jnp.where(lo, vall, jnp.zeros_like(vall)) if h == 0 else jnp.where(lo, jnp.zeros_like(vall), vall)
        q4 = jnp.concatenate(q2[n_pair * h:n_pair * (h + 1)], axis=0)
        q4s = _dot(q4, p64).astype(BF16)
        first, second = (q4, q4s) if h == 0 else (q4s, q4)
        s = lax.dot_general(jnp.concatenate([first, second], axis=0), kh, nt, preferred_element_type=F32)
        probs = []
        for rb in range(2 * n_pair):
            sink = sink_ref[Q_PER_KV * h + 2 * (rb % n_pair) + rb // n_pair]
            sb = s[rb * blk:(rb + 1) * blk]
            parts = [jnp.where(ok_prev, sb[:, 0:blk], neg), sb[:, blk:2 * blk],
                     jnp.where(ok_next, sb[:, 2 * blk:3 * blk], neg), sb[:, 3 * blk:nk]]
            tiles = parts[:3] + [parts[3][:, t * blk:(t + 1) * blk] for t in range((nk - 3 * blk) // blk)]
            m = jnp.maximum(sink, jnp.max(functools.reduce(jnp.maximum, tiles), axis=-1, keepdims=True))
            es = [jnp.exp(t - m) for t in tiles]
            den = jnp.exp(sink - m) + jnp.sum(functools.reduce(jnp.add, es), axis=-1, keepdims=True)
            inv = 1.0 / den
            probs.append(jnp.concatenate([(e * inv).astype(BF16) for e in es], axis=1))
        o = _dot(jnp.concatenate(probs, axis=0), vh)
        half = n_pair * blk
        if h == 0:
            out = o[:half] + pltpu.roll(o[half:], HEAD_DIM, 1)
        else:
            out = pltpu.roll(o[:half], HEAD_DIM, 1) + o[half:]
        w0 = h * Q_PER_KV * HEAD_DIM
        for j in range(n_pair):
            o_ref[:, w0 + 128 * j:w0 + 128 * (j + 1)] = out[j * blk:(j + 1) * blk].astype(o_ref.dtype)


def _ctx_attn_kernel(sink_ref, q_ref, kc_ref, vc_ref, o_ref):
    kc = kc_ref[...].astype(BF16)
    vc = vc_ref[...].astype(BF16)
    scale = HEAD_DIM ** -0.5
    outs = []
    for pair in range(N_Q_HEADS // 2):
        q2 = (q_ref[:, pair * 128:(pair + 1) * 128] * scale).astype(BF16)
        for sub in range(2):
            head = 2 * pair + sub
            kvh = head // Q_PER_KV
            sl = slice(kvh * HEAD_DIM, (kvh + 1) * HEAD_DIM)
            qh = q2[:, sub * HEAD_DIM:(sub + 1) * HEAD_DIM]
            outs.append(_softmax_pv(qh, [kc[:, sl]], [vc[:, sl]], [None], sink_ref[head]))
    o_ref[...] = jnp.concatenate(outs, axis=-1).astype(o_ref.dtype)


def _attention(p, sink, cos_t, sin_t, with_ctx):
    blk = ATTN_BLOCK
    nb = SEQ // blk
    qc, kcol, vcol = PQ_OFF // ATTN_WIDTH, PK_OFF // KV_WIDTH, PV_OFF // KV_WIDTH
    ctx_blk = T_LAT // CTX_LEN
    smem = pl.BlockSpec(memory_space=pltpu.SMEM)

    def kv_spec(col, d):
        return pl.BlockSpec((blk, KV_WIDTH),
                            lambda b, n: (b * nb + jnp.clip(n + d, 0, nb - 1), col))

    ya = pl.pallas_call(
        functools.partial(_attn_kernel, nb=nb),
        grid=(BATCH, nb),
        in_specs=[smem,
                  pl.BlockSpec((blk, ATTN_WIDTH), lambda b, n: (b * nb + n, qc)),
                  kv_spec(kcol, -1), kv_spec(kcol, 0), kv_spec(kcol, 1),
                  kv_spec(vcol, -1), kv_spec(vcol, 0), kv_spec(vcol, 1),
                  pl.BlockSpec((CTX_LEN, KV_WIDTH), lambda b, n: (ctx_blk + b, kcol)),
                  pl.BlockSpec((CTX_LEN, KV_WIDTH), lambda b, n: (ctx_blk + b, vcol)),
                  pl.BlockSpec((SEQ, KV_WIDTH), lambda b, n: (0, 0)),
                  pl.BlockSpec((SEQ, KV_WIDTH), lambda b, n: (0, 0)),
                  pl.BlockSpec((128, 128), lambda b, n: (0, 0)),
                  pl.BlockSpec((128, 128), lambda b, n: (0, 0))],
        out_specs=pl.BlockSpec((blk, ATTN_WIDTH), lambda b, n: (b * nb + n, 0)),
        out_shape=jax.ShapeDtypeStruct((T_ALL, ATTN_WIDTH), BF16),
        compiler_params=_cparams(("parallel", "parallel")),
        name="banded_attention",
    )(sink, p, p, p, p, p, p, p, p, p, cos_t, sin_t, *_lane_perms())
    if not with_ctx:
        return ya
    cb = CTX_LEN // blk
    lat_blk = T_LAT // blk

    def alias_kernel(sink_ref, q_ref, kc_ref, vc_ref, ya_in_ref, o_ref):
        del ya_in_ref
        _ctx_attn_kernel(sink_ref, q_ref, kc_ref, vc_ref, o_ref)

    return pl.pallas_call(
        alias_kernel,
        grid=(BATCH, cb),
        in_specs=[smem,
                  pl.BlockSpec((blk, ATTN_WIDTH), lambda b, n: (lat_blk + b * cb + n, qc)),
                  pl.BlockSpec((CTX_LEN, KV_WIDTH), lambda b, n: (ctx_blk + b, kcol)),
                  pl.BlockSpec((CTX_LEN, KV_WIDTH), lambda b, n: (ctx_blk + b, vcol)),
                  pl.BlockSpec(memory_space=pl.ANY)],
        out_specs=pl.BlockSpec((blk, ATTN_WIDTH), lambda b, n: (lat_blk + b * cb + n, 0)),
        out_shape=jax.ShapeDtypeStruct((T_ALL, ATTN_WIDTH), BF16),
        input_output_aliases={4: 0},
        compiler_params=_cparams(("parallel", "parallel")),
        name="context_attention",
    )(sink, p, p, p, ya)


def _rope_tables():
    n_freq = HEAD_DIM // 4
    freqs = ROPE_BASE ** (-jnp.arange(n_freq, dtype=F32) / n_freq)
    t = jnp.arange(SEQ, dtype=jnp.int32)
    rows = (t // GRID_W).astype(F32)[:, None] * freqs
    cols = (t % GRID_W).astype(F32)[:, None] * freqs
    cos_h = jnp.concatenate([jnp.cos(rows), jnp.cos(rows), jnp.cos(cols), jnp.cos(cols)], axis=-1)
    sin_h = jnp.concatenate([-jnp.sin(rows), jnp.sin(rows), -jnp.sin(cols), jnp.sin(cols)], axis=-1)
    return jnp.tile(cos_h, (1, 2)), jnp.tile(sin_h, (1, 2))


def _lane_perms():
    j = jnp.arange(128, dtype=jnp.int32)[:, None]
    l = jnp.arange(128, dtype=jnp.int32)[None, :]
    rot = jnp.where(l % 32 < 16, l + 16, l - 16)
    return (j == rot).astype(BF16), (j == (l + HEAD_DIM) % 128).astype(BF16)


def _lead_kernel(*refs, n_in, cmul, epi):
    m_ref = refs[0]
    x_refs = refs[1:1 + n_in]
    pos = 1 + n_in
    xs = []
    for r in x_refs:
        v = r[...]
        xs.append(v.reshape(-1, v.shape[-1]))
    x = xs[0] if n_in == 1 else jnp.concatenate(xs, axis=0)
    if cmul:
        k = refs[pos][...]
        pos += 1
        k = k.reshape(-1, k.shape[-1])
        half = x.shape[0] // 2
        xr, xi, kr, ki = x[:half], x[half:], k[:half], k[half:]
        x = jnp.concatenate([xr * kr - xi * ki, xr * ki + xi * kr], axis=0)
    res = _dot(m_ref[...], x.astype(BF16))
    if epi:
        g_ref, y_ref, b_ref = refs[pos:pos + 3]
        pos += 3
        g = g_ref[...]
        y = y_ref[...]
        res = g.reshape(-1, g.shape[-1]) * (res + y.reshape(-1, y.shape[-1]) * b_ref[...])
    o_ref = refs[pos]
    o_ref[...] = res.reshape(o_ref.shape).astype(o_ref.dtype)


def _lead(mat, xs, x_specs, grid, out_shape, out_spec, *, kspec=None, epi=None, alias_to=None, name):
    ins = [mat] + list(xs)
    specs = [pl.BlockSpec(mat.shape, lambda *a: (0, 0))] + list(x_specs)
    if kspec is not None:
        ins.append(kspec[0])
        specs.append(kspec[1])
    if epi is not None:
        for arr, sp in epi:
            ins.append(arr)
            specs.append(sp)
    kern = functools.partial(_lead_kernel, n_in=len(xs), cmul=kspec is not None, epi=epi is not None)
    aliases = {}
    if alias_to is not None:
        aliases = {len(ins): 0}
        ins.append(alias_to)
        specs.append(pl.BlockSpec(memory_space=pl.ANY))
        inner = kern

        def kern(*refs):
            inner(*refs[:-2], refs[-1])

    return pl.pallas_call(
        kern, grid=grid, in_specs=specs, out_specs=out_spec, out_shape=out_shape,
        input_output_aliases=aliases,
        compiler_params=_cparams(("parallel",) * len(grid)), name=name,
    )(*ins)


def _cstack(xr, xi):
    return jnp.concatenate([jnp.concatenate([xr, xi], axis=1), jnp.concatenate([-xi, xr], axis=1)], axis=0)


def _fnet_kernel(u_ref, mc_ref, ml_ref, ms_ref, o_ref, zr, zi, are, aim):
    n = FN_N
    pitch = PITCH
    gd = FNET_GROUP_DIM
    rows = 4 * n
    mc = mc_ref[...]

    def chan(i, carry):
        r_in = pl.multiple_of(i * rows, rows)
        z = _dot(u_ref[pl.ds(r_in, rows), :].astype(BF16), mc)
        for q in range(rows // n):
            r_out = pl.multiple_of((i * (rows // n) + q) * pitch, 8)
            zr[pl.ds(r_out, n), :] = z[q * n:(q + 1) * n, :gd]
            zi[pl.ds(r_out, n), :] = z[q * n:(q + 1) * n, gd:]
        return carry

    lax.fori_loop(0, SEQ // rows, chan, 0)
    ml = ml_ref[...]

    def lead(i, carry):
        for u in range(UNROLL):
            n1 = i * UNROLL + u
            x = jnp.concatenate([zr[pl.ds(n1, n, stride=pitch), :], zi[pl.ds(n1, n, stride=pitch), :]],
                                axis=0).astype(BF16)
            r = _dot(ml, x)
            are[pl.ds(n1, n, stride=pitch), :] = r[:n]
            aim[pl.ds(n1, n, stride=pitch), :] = r[n:]
        return carry

    lax.fori_loop(0, n // UNROLL, lead, 0)

    def slab(i, carry):
        for u in range(UNROLL):
            k2 = i * UNROLL + u
            r0 = pl.multiple_of(k2 * pitch, 8)
            x = jnp.concatenate([are[pl.ds(r0, n), :], aim[pl.ds(r0, n), :]], axis=0).astype(BF16)
            o_ref[pl.ds(k2, n, stride=n), :] = _dot(ms_ref[k2], x)
        return carry

    lax.fori_loop(0, n // UNROLL, slab, 0)


def _fnet_ctx_kernel(u_ref, mc_ref, mf_ref, yf_in_ref, o_ref):
    del yf_in_ref
    gd = FNET_GROUP_DIM
    mc = mc_ref[...]
    mf = mf_ref[...]
    for g in range(FNET_GROUPS):
        z = _dot(u_ref[:, g * gd:(g + 1) * gd].astype(BF16), mc)
        x = jnp.concatenate([z[:, :gd], z[:, gd:]], axis=0).astype(BF16)
        o_ref[:, g * gd:(g + 1) * gd] = _dot(mf, x)


def _fourier_mix(p, mats, with_ctx):
    gd = FNET_GROUP_DIM
    col0 = PF_OFF // gd
    full = lambda a: pl.BlockSpec(a.shape, lambda *i: (0,) * a.ndim)
    scr = pltpu.VMEM((FN_N * PITCH, gd), F32)
    yf = pl.pallas_call(
        _fnet_kernel,
        grid=(BATCH, FNET_GROUPS),
        in_specs=[pl.BlockSpec((SEQ, gd), lambda b, g: (b, col0 + g)),
                  full(mats['fn_chan']), full(mats['fn_lead']), full(mats['fn_slab'])],
        out_specs=pl.BlockSpec((SEQ, gd), lambda b, g: (b, g)),
        out_shape=jax.ShapeDtypeStruct((T_ALL if with_ctx else T_LAT, FNET_WIDTH), F32),
        scratch_shapes=[scr, scr, scr, scr],
        compiler_params=_cparams(("parallel", "parallel")),
        name="fnet_latent",
    )(p, mats['fn_chan'], mats['fn_lead'], mats['fn_slab'])
    if not with_ctx:
        return yf
    blk0 = T_LAT // CTX_LEN
    return pl.pallas_call(
        _fnet_ctx_kernel,
        grid=(BATCH,),
        in_specs=[pl.BlockSpec((pl.Element(CTX_LEN), pl.Element(FNET_WIDTH)),
                               lambda b: ((blk0 + b) * CTX_LEN, PF_OFF)),
                  full(mats['fn_chan']), full(mats['fc']), pl.BlockSpec(memory_space=pl.ANY)],
        out_specs=pl.BlockSpec((CTX_LEN, FNET_WIDTH), lambda b: (blk0 + b, 0)),
        out_shape=jax.ShapeDtypeStruct((T_ALL, FNET_WIDTH), F32),
        input_output_aliases={3: 0},
        compiler_params=_cparams(("parallel",)),
        name="fnet_ctx",
    )(p, mats['fn_chan'], mats['fc'], yf)


def _short_conv_kernel(u_ref, w_ref, b_ref, o_ref, *, rows, chunk, padded):
    w0 = w_ref[0:1, :]
    w1 = w_ref[1:2, :]
    w2 = w_ref[2:3, :]
    bias = b_ref[...]
    width = u_ref.shape[-1]
    ridx = lax.broadcasted_iota(jnp.int32, (chunk, width), 0)
    n_chunks = rows // chunk
    for ci in range(n_chunks):
        r0 = ci * chunk
        cur = u_ref[r0:r0 + chunk, :].astype(F32)
        if ci > 0:
            prev_row = u_ref[r0 - 16:r0, :].astype(F32)[15:16, :]
        else:
            prev_row = jnp.zeros((1, width), F32)
        if ci < n_chunks - 1:
            next_row = u_ref[r0 + chunk:r0 + chunk + 16, :].astype(F32)[0:1, :]
        else:
            next_row = jnp.zeros((1, width), F32)
        up = jnp.where(ridx == 0, prev_row, pltpu.roll(cur, 1, 0))
        dn = jnp.where(ridx == chunk - 1, next_row, pltpu.roll(cur, chunk - 1, 0))
        res = up * w0 + cur * w1 + dn * w2 + bias
        if not padded:
            o_ref[0, r0:r0 + chunk, :] = res
            continue
        for q in range(chunk // HY_N2):
            p0 = (r0 // HY_N2 + q) * PITCH
            o_ref[0, p0:p0 + HY_N2, :] = res[q * HY_N2:(q + 1) * HY_N2]
            o_ref[0, p0 + HY_N2:p0 + PITCH, :] = jnp.zeros((PITCH - HY_N2, width), F32)


def _short_conv(p, conv_w, conv_b, with_ctx):
    cw = 256
    hw = (HYENA_ORDER + 1) * HYENA_WIDTH
    ncol = hw // cw
    per = HYENA_WIDTH // cw
    col0 = PH_OFF // cw
    out_shape = jax.ShapeDtypeStruct((HYENA_ORDER + 1, T_PAD if with_ctx else BATCH * SEQ_PAD, HYENA_WIDTH), F32)
    b2 = conv_b.reshape(1, hw)

    def call(rows, blk0, alias):
        padded = rows == SEQ
        out_rows = SEQ_PAD if padded else rows
        out_blk0 = 0 if padded else BATCH * SEQ_PAD // rows
        kern = functools.partial(_short_conv_kernel, rows=rows, chunk=min(rows, 256), padded=padded)
        ins = [p, conv_w, b2]
        specs = [pl.BlockSpec((rows, cw), lambda b, j: (blk0 + b, col0 + j)),
                 pl.BlockSpec((3, cw), lambda b, j: (0, j)),
                 pl.BlockSpec((1, cw), lambda b, j: (0, j))]
        aliases = {}
        if alias is not None:
            ins.append(alias)
            specs.append(pl.BlockSpec(memory_space=pl.ANY))
            aliases = {3: 0}
            inner = kern

            def kern(u_ref, w_ref, b_ref, a_ref, o_ref):
                del a_ref
                inner(u_ref, w_ref, b_ref, o_ref)

        return pl.pallas_call(
            kern, grid=(BATCH, ncol), in_specs=specs,
            out_specs=pl.BlockSpec((1, out_rows, cw), lambda b, j: (j // per, out_blk0 + b, j % per)),
            out_shape=out_shape, input_output_aliases=aliases,
            compiler_params=_cparams(("parallel", "parallel")), name="hyena_short_conv",
        )(*ins)

    z3 = call(SEQ, 0, None)
    if with_ctx:
        z3 = call(CTX_LEN, T_LAT // CTX_LEN, z3)
    return z3


def _filter_mlp_kernel(ft_ref, w1_ref, b1_ref, fq_ref, w2_ref, b2_ref, o_ref):
    fq = fq_ref[...]
    h = jnp.sin(fq * (_dot(ft_ref[...].astype(BF16), w1_ref[...]) + b1_ref[...]))
    h = jnp.sin(fq * (_dot(h.astype(BF16), w2_ref[...]) + b2_ref[...]))
    o_ref[...] = h.astype(o_ref.dtype)


def _filter_kernel(h_ref, t_ref, w3f_ref, w3b_ref, dl_ref, m1_ref, *rest, n, dense):
    hb = h_ref[...]
    decay = jnp.exp(-t_ref[...] * dl_ref[...])
    tf = _dot(hb, w3f_ref[...]) * decay
    tb = _dot(hb, w3b_ref[...]) * decay
    tb = jnp.where(lax.broadcasted_iota(jnp.int32, tb.shape, 0) == 0, 0.0, tb)
    scale = 1.0 / (jnp.sum(jnp.abs(tf), axis=0, keepdims=True) + jnp.sum(jnp.abs(tb), axis=0, keepdims=True))
    cw = tf.shape[1]
    if dense:
        o_ref = rest[0]
        r = _dot(m1_ref[...], jnp.concatenate([tf, tb], axis=1).astype(BF16))
        nc = r.shape[0] // 2
        o_ref[0:nc, :] = (r[:nc, :cw] + r[:nc, cw:]) * scale
        o_ref[nc:, :] = (r[nc:, :cw] - r[nc:, cw:]) * scale
        return
    f2_ref, o_ref, tf_s, tb_s, are_f, aim_f, are_b, aim_b = rest
    tf_s[...] = tf
    tb_s[...] = tb
    half = HY_N1 // 2
    m1 = m1_ref[...]

    def stage1(i, carry):
        for u in range(UNROLL):
            n2 = i * UNROLL + u
            x = jnp.concatenate([tf_s[pl.ds(n2, half, stride=HY_N2), :], tb_s[pl.ds(n2, half, stride=HY_N2), :]],
                                axis=1).astype(BF16)
            r = _dot(m1, x)
            rows = pl.ds(n2, HY_N1, stride=PITCH)
            are_f[rows, :] = r[:HY_N1, :cw]
            are_b[rows, :] = r[:HY_N1, cw:]
            aim_f[rows, :] = r[HY_N1:, :cw]
            aim_b[rows, :] = r[HY_N1:, cw:]
        return carry

    lax.fori_loop(0, HY_N2 // UNROLL, stage1, 0)

    def stage2(i, carry):
        for u in range(UNROLL):
            k1 = i * UNROLL + u
            rows = pl.ds(pl.multiple_of(k1 * PITCH, 8), HY_N2)
            ar = jnp.concatenate([are_f[rows, :], are_b[rows, :]], axis=1)
            ai = jnp.concatenate([aim_f[rows, :], aim_b[rows, :]], axis=1)
            t = _dot(f2_ref[k1], _cstack(ar, ai).astype(BF16))
            o_ref[0, k1] = (t[:, 0:cw] + t[:, cw:2 * cw]) * scale
            o_ref[1, k1] = (t[:, 2 * cw:3 * cw] - t[:, 3 * cw:4 * cw]) * scale
        return carry

    lax.fori_loop(0, HY_N1 // UNROLL, stage2, 0)


def _filter_feats(n):
    pos = jnp.arange(n, dtype=F32)
    t = pos / max(n - 1, 1)
    omega = 2.0 * math.pi * pos / n
    bands = jnp.linspace(1e-4, FILTER_BANDS - 1, FILTER_BANDS, dtype=F32)
    feats = jnp.concatenate([t[:, None], jnp.cos(omega[:, None] * bands), -jnp.sin(omega[:, None] * bands)], axis=-1)
    return jnp.pad(feats, ((0, 0), (0, 128 - FILTER_EMB))), t[:, None]


def _filter_spectrum(n, filt, mats):
    w1, b1, freq, w2, b2, w3 = filt
    dense = n == CTX_LEN
    cw = 128
    nch = HYENA_WIDTH // cw
    feats, t = _filter_feats(n)
    w1p = jnp.pad(w1, ((0, 128 - FILTER_EMB), (0, 0))).astype(BF16)
    deltas = jnp.abs(jnp.linspace(math.log(DECAY_TARGET) / SLOW_DECAY_PCT, math.log(DECAY_TARGET) / FAST_DECAY_PCT,
                                  HYENA_WIDTH, dtype=F32)).reshape(1, HYENA_WIDTH)
    full = lambda a: pl.BlockSpec(a.shape, lambda *i: (0,) * a.ndim)
    row = lambda a: a.reshape(1, -1)
    w3b16 = w3.astype(BF16)
    tap_spec = lambda d: pl.BlockSpec((FILTER_HIDDEN, cw), lambda o, ch: (0, (o * 2 + d) * nch + ch))
    m1 = mats['hc_kh'] if dense else mats['hy_lead_kh']
    mlp_ins = [feats, w1p, row(b1), row(freq), w2.astype(BF16), row(b2)]
    hb = pl.pallas_call(
        _filter_mlp_kernel, grid=(1,), in_specs=[full(a) for a in mlp_ins],
        out_specs=pl.BlockSpec((n, FILTER_HIDDEN), lambda i: (0, 0)),
        out_shape=jax.ShapeDtypeStruct((n, FILTER_HIDDEN), BF16),
        compiler_params=_cparams(("arbitrary",)), name="hyena_filter_mlp",
    )(*mlp_ins)
    ins = [hb, t, w3b16, w3b16, deltas, m1]
    specs = [full(hb), full(t), tap_spec(0), tap_spec(1), pl.BlockSpec((1, cw), lambda o, ch: (0, ch)), full(m1)]
    if dense:
        nc = 2 * n
        out_shape = jax.ShapeDtypeStruct((HYENA_ORDER, 2 * nc, HYENA_WIDTH), F32)
        out_spec = pl.BlockSpec((None, 2 * nc, cw), lambda o, ch: (o, 0, ch))
        scratch = []
    else:
        ins.append(mats['hy_slab_f2'])
        specs.append(full(mats['hy_slab_f2']))
        out_shape = jax.ShapeDtypeStruct((HYENA_ORDER, 2, HY_N1, HY_N2, HYENA_WIDTH), F32)
        out_spec = pl.BlockSpec((None, 2, HY_N1, HY_N2, cw), lambda o, ch: (o, 0, 0, 0, ch))
        scratch = [pltpu.VMEM((n, cw), F32)] * 2 + [pltpu.VMEM((HY_N1 * PITCH, cw), F32)] * 4
    return pl.pallas_call(
        functools.partial(_filter_kernel, n=n, dense=dense),
        grid=(HYENA_ORDER, nch), in_specs=specs, out_specs=out_spec, out_shape=out_shape,
        scratch_shapes=scratch,
        compiler_params=_cparams(("parallel", "parallel")),
        name="hyena_filter_ctx" if dense else "hyena_filter",
    )(*ins)


def _hyena_conv_kernel(y_ref, g_ref, k_ref, m1_ref, f2f_ref, f2i_ref, m3_ref, b_ref, o_ref, are, aim, *, out_pitch):
    half = HY_N1 // 2
    out_seq = half * out_pitch
    m1 = m1_ref[...]

    def stage1(i, carry):
        for u in range(UNROLL):
            n2 = i * UNROLL + u
            x = jnp.concatenate([y_ref[pl.ds(n2, half, stride=PITCH), :],
                                 y_ref[pl.ds(SEQ_PAD + n2, half, stride=PITCH), :]], axis=0).astype(BF16)
            r = _dot(m1, x)
            are[pl.ds(n2, HY_N1, stride=PITCH), :] = r[:HY_N1]
            aim[pl.ds(n2, HY_N1, stride=PITCH), :] = r[HY_N1:]
        return carry

    lax.fori_loop(0, HY_N2 // UNROLL, stage1, 0)
    cw = o_ref.shape[-1]

    def stage2(i, carry):
        for u in range(UNROLL_SLAB):
            k1 = i * UNROLL_SLAB + u
            r0 = pl.multiple_of(k1 * PITCH, 8)
            y = _dot(f2f_ref[k1], _cstack(are[pl.ds(r0, HY_N2), :], aim[pl.ds(r0, HY_N2), :]).astype(BF16))
            yr, yi = y[:, :cw], y[:, cw:]
            kr, ki = k_ref[0, k1], k_ref[1, k1]
            w = _dot(f2i_ref[k1], _cstack(yr * kr - yi * ki, yr * ki + yi * kr).astype(BF16))
            are[pl.ds(r0, HY_N2), :] = w[:, :cw]
            aim[pl.ds(r0, HY_N2), :] = w[:, cw:]
        return carry

    lax.fori_loop(0, HY_N1 // UNROLL_SLAB, stage2, 0)
    m3 = m3_ref[...]
    bias = b_ref[...]

    def stage3(i, carry):
        for u in range(UNROLL):
            n2 = i * UNROLL + u
            bn = jnp.concatenate([are[pl.ds(n2, HY_N1, stride=PITCH), :], aim[pl.ds(n2, HY_N1, stride=PITCH), :]],
                                 axis=0).astype(BF16)
            y = _dot(m3, bn)
            for b in range(2):
                rows = pl.ds(b * SEQ_PAD + n2, half, stride=PITCH)
                o_ref[pl.ds(b * out_seq + n2, half, stride=out_pitch), :] = (
                    g_ref[rows, :] * (y[b * half:(b + 1) * half] + y_ref[rows, :] * bias))
        return carry

    lax.fori_loop(0, HY_N2 // UNROLL, stage3, 0)


def _hyena_mix(z3, filt, hyena_bias, mats, with_ctx):
    c = HYENA_WIDTH
    cw = 128
    nch = c // cw
    pairs = BATCH // 2
    full = lambda a: pl.BlockSpec(a.shape, lambda *i: (0,) * a.ndim)
    kspec = _filter_spectrum(SEQ, filt, mats)
    if with_ctx:
        kspec_c = _filter_spectrum(CTX_LEN, filt, mats)
    bias3 = hyena_bias.reshape(HYENA_ORDER, 1, c)
    scr = pltpu.VMEM((HY_N1 * PITCH, cw), F32)
    y = None
    for o in range(HYENA_ORDER):
        final = o == HYENA_ORDER - 1
        out_pitch = HY_N2 if final else PITCH
        out_seq = SEQ if final else SEQ_PAD
        if final:
            out_rows = T_ALL if with_ctx else T_LAT
        else:
            out_rows = T_PAD if with_ctx else BATCH * SEQ_PAD
        if y is None:
            xin, xspec = z3, pl.BlockSpec((None, 2 * SEQ_PAD, cw), lambda b, ch: (0, b, ch))
        else:
            xin, xspec = y, pl.BlockSpec((2 * SEQ_PAD, cw), lambda b, ch: (b, ch))
        ynew = pl.pallas_call(
            functools.partial(_hyena_conv_kernel, out_pitch=out_pitch),
            grid=(pairs, nch),
            in_specs=[xspec,
                      pl.BlockSpec((None, 2 * SEQ_PAD, cw), lambda b, ch, o=o: (o + 1, b, ch)),
                      pl.BlockSpec((None, 2, HY_N1, HY_N2, cw), lambda b, ch, o=o: (o, 0, 0, 0, ch)),
                      full(mats['hy_lead_f']), full(mats['hy_slab_f2']), full(mats['hy_slab_i2']),
                      full(mats['hy_lead_i']),
                      pl.BlockSpec((None, 1, cw), lambda b, ch, o=o: (o, 0, ch))],
            out_specs=pl.BlockSpec((2 * out_seq, cw), lambda b, ch: (b, ch)),
            out_shape=jax.ShapeDtypeStruct((out_rows, c), F32),
            scratch_shapes=[scr, scr],
            compiler_params=_cparams(("parallel", "parallel"), 58 * 1024 * 1024),
            name="hyena_conv",
        )(xin, z3, kspec, mats['hy_lead_f'], mats['hy_slab_f2'], mats['hy_slab_i2'], mats['hy_lead_i'], bias3)
        if with_ctx:
            nc = 2 * CTX_LEN
            blk0 = BATCH * SEQ_PAD // nc
            oblk0 = BATCH * out_seq // nc
            if y is None:
                cin, cspec_in = z3, pl.BlockSpec((1, nc, c), lambda b: (0, blk0 + b, 0))
            else:
                cin, cspec_in = y, pl.BlockSpec((nc, c), lambda b: (blk0 + b, 0))
            xc = _lead(mats['hc_f'], [cin], [cspec_in], (pairs,),
                       jax.ShapeDtypeStruct((pairs, 2 * nc, c), F32),
                       pl.BlockSpec((1, 2 * nc, c), lambda b: (b, 0, 0)), name="hyena_ctx_fwd")
            ynew = _lead(mats['hc_i'], [xc], [pl.BlockSpec((1, 2 * nc, c), lambda b: (b, 0, 0))],
                         (pairs,), jax.ShapeDtypeStruct((out_rows, c), F32),
                         pl.BlockSpec((nc, c), lambda b, oblk0=oblk0: (oblk0 + b, 0)),
                         kspec=(kspec_c, pl.BlockSpec((1, 2 * nc, c), lambda b, o=o: (o, 0, 0))),
                         epi=[(z3, pl.BlockSpec((1, nc, c), lambda b, o=o: (o + 1, blk0 + b, 0))),
                              (cin, cspec_in),
                              (bias3, pl.BlockSpec((1, 1, c), lambda b, o=o: (o, 0, 0)))],
                         alias_to=ynew, name="hyena_ctx_inv")
        y = ynew
    return y


def _route_tile(lt, br, base, tri):
    tm = lt.shape[1]
    aff = jax.nn.sigmoid(lt)
    biased = aff + br
    b = [biased[e:e + 1, :] for e in range(N_EXPERTS)]
    a = [aff[e:e + 1, :] for e in range(N_EXPERTS)]
    epg = EXPERTS_PER_GROUP
    scores = []
    for g in range(N_GROUPS):
        x0, x1, x2, x3 = b[epg * g:epg * g + epg]
        s1, t1 = jnp.maximum(x0, x1), jnp.minimum(x0, x1)
        s2, t2 = jnp.maximum(x2, x3), jnp.minimum(x2, x3)
        scores.append(jnp.maximum(s1, s2) + jnp.maximum(jnp.minimum(s1, s2), jnp.maximum(t1, t2)))
    best = scores[0]
    gsel = jnp.zeros((1, tm), jnp.int32)
    for g in range(1, N_GROUPS):
        gsel = jnp.where(scores[g] > best, g, gsel)
        best = jnp.maximum(best, scores[g])

    def pick(rows, j):
        out = rows[j]
        for g in range(1, N_GROUPS):
            out = jnp.where(gsel == g, rows[epg * g + j], out)
        return out

    v = [pick(b, j) for j in range(epg)]
    av = [pick(a, j) for j in range(epg)]
    i1 = jnp.zeros((1, tm), jnp.int32)
    m1 = v[0]
    for j in range(1, epg):
        i1 = jnp.where(v[j] > m1, j, i1)
        m1 = jnp.maximum(m1, v[j])
    neg = jnp.float32(-3.0e38)
    i2 = jnp.zeros((1, tm), jnp.int32)
    m2 = jnp.full((1, tm), neg, F32)
    for j in range(epg):
        cand = jnp.where(i1 == j, neg, v[j])
        take = cand > m2
        i2 = jnp.where(take, j, i2)
        m2 = jnp.where(take, cand, m2)

    def sel(rows, idx):
        out = rows[0]
        for j in range(1, epg):
            out = jnp.where(idx == j, rows[j], out)
        return out

    a1, a2 = sel(av, i1), sel(av, i2)
    den = a1 + a2
    e1 = gsel * epg + i1
    e2 = gsel * epg + i2
    eio = lax.broadcasted_iota(jnp.int32, (N_EXPERTS, tm), 0)
    oh1 = jnp.where(eio == e1, 1.0, 0.0)
    oh2 = jnp.where(eio == e2, 1.0, 0.0)
    oh = oh1 + oh2
    tot = base + _dot(oh.astype(BF16), tri)
    r1 = jnp.sum(oh1 * tot, axis=0, keepdims=True)
    r2 = jnp.sum(oh2 * tot, axis=0, keepdims=True)
    new_base = base + jnp.sum(oh, axis=1, keepdims=True)
    return (e1, e2), (a1 / den, a2 / den), (r1.astype(jnp.int32), r2.astype(jnp.int32)), new_base


def _merge_kernel(x_ref, ya_ref, yf_ref, yh_ref, gt_ref, wa_ref, wf_ref, wh_ref, wo_ref, g1_ref,
                  gn_ref, sh_ref, sc_ref, wrt_ref, br_ref, xo_ref, h2_ref, e_ref, w_ref, r_ref, cnt_ref):
    d = D_MODEL
    gate = lambda k: jax.nn.sigmoid(gt_ref[:, k * d:(k + 1) * d].astype(F32))
    merged = gate(0) * _dot(ya_ref[...], wa_ref[...])
    merged += gate(1) * _dot(yf_ref[...].astype(BF16), wf_ref[...])
    merged += gate(2) * _dot(yh_ref[...].astype(BF16), wh_ref[...])
    xn = x_ref[...] + g1_ref[0, 0] * _dot(merged.astype(BF16), wo_ref[...])
    xo_ref[...] = xn
    h2f = _rms_mod(xn, gn_ref[...], sh_ref[0, 0], sc_ref[0, 0])
    _rows_to_tiles(h2_ref, h2f, 0, h2f.shape[0])
    h2 = h2f.astype(BF16)

    @pl.when(pl.program_id(0) == 0)
    def _():
        cnt_ref[...] = jnp.zeros_like(cnt_ref)

    tm = h2.shape[0]
    lt = lax.dot_general(wrt_ref[...], h2, (((1,), (1,)), ((), ())), preferred_element_type=F32)
    tri = jnp.where(lax.broadcasted_iota(jnp.int32, (tm, tm), 0) < lax.broadcasted_iota(jnp.int32, (tm, tm), 1),
                    1.0, 0.0).astype(BF16)
    es, ws, rs, new_base = _route_tile(lt, br_ref[...], cnt_ref[:, 0:1], tri)
    e_ref[0:1, :], e_ref[1:2, :] = es
    w_ref[0:1, :], w_ref[1:2, :] = ws
    r_ref[0:1, :], r_ref[1:2, :] = rs
    cnt_ref[...] = jnp.broadcast_to(new_base, cnt_ref.shape)


def _merge(x, ya, yf, yh, p, wa, wf, wh, wo, mod4, gain2, wrt, br, n_tok):
    tm = TM
    row = _mod_row(tm)
    full = lambda a: pl.BlockSpec(a.shape, lambda i: (0,) * a.ndim)
    modspec = lambda k: pl.BlockSpec((1, 1, 1, D_MODEL), lambda i: (row(i), k, 0, 0))
    tok = lambda w: pl.BlockSpec((tm, w), lambda i: (i, 0))
    lane = pl.BlockSpec((TOP_K, tm), lambda i: (0, i))
    return pl.pallas_call(
        _merge_kernel,
        grid=(n_tok // tm,),
        in_specs=[tok(D_MODEL), tok(ATTN_WIDTH), tok(FNET_WIDTH), tok(HYENA_WIDTH),
                  pl.BlockSpec((pl.Element(tm), pl.Element(3 * D_MODEL)), lambda i: (i * tm, PG_OFF)),
                  full(wa), full(wf), full(wh), full(wo), modspec(2), full(gain2), modspec(3), modspec(4),
                  full(wrt), full(br)],
        out_specs=[tok(D_MODEL), pl.BlockSpec((tm * ROW_TILE, 128), lambda i: (i, 0)), lane, lane, lane,
                   pl.BlockSpec((N_EXPERTS, 128), lambda i: (0, 0))],
        out_shape=[jax.ShapeDtypeStruct((n_tok, D_MODEL), F32),
                   jax.ShapeDtypeStruct((n_tok * ROW_TILE, 128), F32),
                   jax.ShapeDtypeStruct((TOP_K, n_tok), jnp.int32),
                   jax.ShapeDtypeStruct((TOP_K, n_tok), F32),
                   jax.ShapeDtypeStruct((TOP_K, n_tok), jnp.int32),
                   jax.ShapeDtypeStruct((N_EXPERTS, 128), F32)],
        compiler_params=_cparams(("arbitrary",)),
        name="merge_out_norm_route",
    )(x, ya, yf, yh, p, wa, wf, wh, wo, mod4, gain2, mod4, mod4, wrt, br)


def _row_copy(src_ref, row, dst_ref, r, sem):
    return pltpu.make_async_copy(src_ref.at[pl.ds(row, 1), :], dst_ref.at[pl.ds(r, 1), :], sem)


ROW_TILE = D_MODEL // 128


def _rows_from_tiles(ref, n):
    return jnp.concatenate([ref[pl.ds(s, n, stride=ROW_TILE), :] for s in range(ROW_TILE)], axis=1)


def _rows_to_tiles(ref, val, col0, n):
    for j in range(val.shape[1] // 128):
        ref[pl.ds(col0 // 128 + j, n, stride=ROW_TILE), :] = val[:, j * 128:(j + 1) * 128]


def _expert_kernel(be_ref, na_ref, rt_cur_ref, rt_next_ref, h2_ref, wg_ref, wu_ref, wd_ref, o_ref,
                   xbuf, sem, wg_s, wu_s, wd_s):
    i = pl.program_id(0)
    n_act = na_ref[0]
    bm = xbuf.shape[1]
    active = i < n_act

    def gather(rt_ref, slot):
        def body(j, carry):
            r0 = pl.multiple_of(j * 8, 8)
            tile = xbuf.at[slot, pl.ds(r0, 8)]
            for u in range(8):
                _row_copy(h2_ref, rt_ref[0, 0, r0 + u], tile, u, sem.at[slot]).start()
            return carry

        lax.fori_loop(0, bm // 8, body, 0)

    @pl.when((i == 0) & active)
    def _():
        gather(rt_cur_ref, 0)

    @pl.when(i + 1 < n_act)
    def _():
        gather(rt_next_ref, (i + 1) % 2)

    prev = be_ref[jnp.maximum(i - 1, 0)]

    @pl.when(active & ((i == 0) | (be_ref[i] != prev)))
    def _():
        wg_s[...] = wg_ref[0].astype(BF16)
        wu_s[...] = wu_ref[0].astype(BF16)
        wd_s[...] = wd_ref[0].astype(BF16)

    @pl.when(active)
    def _():
        slot = i % 2
        pltpu.make_async_copy(h2_ref.at[pl.ds(0, bm), :], xbuf.at[slot], sem.at[slot]).wait()
        x = xbuf[slot].astype(BF16)
        g = _dot(x, wg_s[...])
        u = _dot(x, wu_s[...])
        h = (g * jax.nn.sigmoid(g)) * u
        o_ref[...] = _dot(h.astype(BF16), wd_s[...])

    @pl.when(jnp.logical_not(active))
    def _():
        o_ref[...] = jnp.zeros_like(o_ref)


def _experts(h2, row_tok, blk_expert, n_active, wg, wu, wd):
    bm = EXPERT_BM
    n_blk = row_tok.shape[0] // bm
    rt = row_tok.reshape(n_blk, 1, bm)
    wspec = lambda k, n: pl.BlockSpec((1, k, n), lambda i, be, na: (be[i], 0, 0))
    return pl.pallas_call(
        _expert_kernel,
        grid_spec=pltpu.PrefetchScalarGridSpec(
            num_scalar_prefetch=2,
            grid=(n_blk,),
            in_specs=[pl.BlockSpec((1, 1, bm), lambda i, be, na: (i, 0, 0), memory_space=pltpu.SMEM),
                      pl.BlockSpec((1, 1, bm), lambda i, be, na: (jnp.minimum(i + 1, n_blk - 1), 0, 0),
                                   memory_space=pltpu.SMEM),
                      pl.BlockSpec(memory_space=pl.ANY),
                      wspec(D_MODEL, EXPERT_FF), wspec(D_MODEL, EXPERT_FF), wspec(EXPERT_FF, D_MODEL)],
            out_specs=pl.BlockSpec((bm, D_MODEL), lambda i, be, na: (i, 0)),
            scratch_shapes=[pltpu.VMEM((2, bm, D_MODEL), F32), pltpu.SemaphoreType.DMA((2,)),
                            pltpu.VMEM((D_MODEL, EXPERT_FF), BF16), pltpu.VMEM((D_MODEL, EXPERT_FF), BF16),
                            pltpu.VMEM((EXPERT_FF, D_MODEL), BF16)]),
        out_shape=jax.ShapeDtypeStruct((row_tok.shape[0], D_MODEL), F32),
        compiler_params=_cparams(("arbitrary",)),
        name="moe_experts",
    )(blk_expert, n_active, rt, rt, h2, wg, wu, wd)


def _dispatch(e_idx, rank, counts, n_tok):
    bm = EXPERT_BM
    counts = counts.astype(jnp.int32)
    padded = (counts + bm - 1) // bm * bm
    pad_end = jnp.cumsum(padded)
    pad_start = pad_end - padded
    experts = jnp.arange(N_EXPERTS, dtype=jnp.int32)
    start = jnp.sum(jnp.where(e_idx[..., None] == experts, pad_start, 0), axis=-1)
    dest = start + rank
    n_rows = -(-(n_tok * TOP_K) // bm) * bm + N_EXPERTS * bm
    n_blk = n_rows // bm
    tok = jnp.tile(jnp.arange(n_tok, dtype=jnp.int32), TOP_K)
    row_tok = jnp.zeros((n_rows,), jnp.int32).at[dest.reshape(-1)].set(tok)
    blk_start = jnp.arange(n_blk, dtype=jnp.int32) * bm
    blk_expert = jnp.minimum(jnp.sum((blk_start[:, None] >= pad_end[None, :]).astype(jnp.int32), axis=-1),
                             N_EXPERTS - 1)
    n_active = (pad_end[-1] // bm).astype(jnp.int32).reshape(1)
    return row_tok, dest, blk_expert, n_active


def _combine_kernel(d_cur_ref, d_next_ref, x_ref, w_ref, g2_ref, gf_ref, ys_ref, o_ref, ybuf, sem,
                    *, final, n_tiles):
    i = pl.program_id(0)
    tm = x_ref.shape[0]

    def gather(d_ref, slot):
        def body(j, carry):
            r0 = pl.multiple_of(j * 8, 8)
            for k in range(TOP_K):
                tile = ybuf.at[slot, k, pl.ds(r0, 8)]
                for u in range(8):
                    _row_copy(ys_ref, d_ref[0, k, r0 + u], tile, u, sem.at[slot]).start()
            return carry

        lax.fori_loop(0, tm // 8, body, 0)

    @pl.when(i == 0)
    def _():
        gather(d_cur_ref, 0)

    @pl.when(i + 1 < n_tiles)
    def _():
        gather(d_next_ref, (i + 1) % 2)

    slot = i % 2
    for k in range(TOP_K):
        pltpu.make_async_copy(ys_ref.at[pl.ds(0, tm), :], ybuf.at[slot, k], sem.at[slot]).wait()
    w = w_ref[...]
    moe = ybuf[slot, 0] * w[:, 0:1] + ybuf[slot, 1] * w[:, 1:2]
    xn = x_ref[...] + g2_ref[0, 0] * moe
    if final:
        y = xn * lax.rsqrt(jnp.mean(xn * xn, axis=-1, keepdims=True) + EPS)
        xn = y * gf_ref[...]
    o_ref[...] = xn


def _combine(x, ys, dest, w_sel, mod4, gain_final, n_tok, final):
    tm = 256
    n_tiles = n_tok // tm
    row = _mod_row(tm)
    tok = lambda w: pl.BlockSpec((tm, w), lambda i: (i, 0))
    d3 = jnp.transpose(dest.reshape(TOP_K, n_tiles, tm), (1, 0, 2))
    return pl.pallas_call(
        functools.partial(_combine_kernel, final=final, n_tiles=n_tiles),
        grid=(n_tiles,),
        in_specs=[pl.BlockSpec((1, TOP_K, tm), lambda i: (i, 0, 0), memory_space=pltpu.SMEM),
                  pl.BlockSpec((1, TOP_K, tm), lambda i: (jnp.minimum(i + 1, n_tiles - 1), 0, 0),
                               memory_space=pltpu.SMEM),
                  tok(D_MODEL), tok(TOP_K),
                  pl.BlockSpec((1, 1, 1, D_MODEL), lambda i: (row(i), 5, 0, 0)),
                  pl.BlockSpec((1, D_MODEL), lambda i: (0, 0)),
                  pl.BlockSpec(memory_space=pl.ANY)],
        out_specs=tok(D_MODEL),
        out_shape=jax.ShapeDtypeStruct((n_tok if final else T_ALL, D_MODEL), F32),
        scratch_shapes=[pltpu.VMEM((2, TOP_K, tm, D_MODEL), F32), pltpu.SemaphoreType.DMA((2,))],
        compiler_params=_cparams(("arbitrary",)),
        name="moe_combine",
    )(d3, d3, x, w_sel, mod4, gain_final, ys)


def _moe_kernel(be_ref, na_ref, ip_ref, ic_ref, sc_ref, sn_ref, h2_ref, wg_ref, wu_ref, wd_ref, y_ref,
                xbuf, obuf, gsem, ssem, wg_s, wu_s, wd_s):
    i = pl.program_id(0)
    n_act = na_ref[0]
    sub = ROW_TILE
    bm = xbuf.shape[1] // sub
    active = i < n_act
    slot = i % 2
    other = (i + 1) % 2
    n_chunk = 4
    cw = EXPERT_FF // n_chunk
    rows_per = bm // n_chunk

    def tile_rows(ref, row):
        return ref.at[pl.ds(pl.multiple_of(row * sub, sub), sub), :]

    def gather(src_ref, dst_slot, lo, hi):
        for r in range(lo, hi):
            pltpu.make_async_copy(tile_rows(h2_ref, src_ref[0, 0, r]), xbuf.at[dst_slot, pl.ds(r * sub, sub), :],
                                  gsem.at[dst_slot]).start(priority=r % 2)

    def scatter(info_ref, src_slot, lo, hi):
        for r in range(lo, hi):
            pltpu.make_async_copy(obuf.at[src_slot, pl.ds(r * sub, sub), :], tile_rows(y_ref, info_ref[0, 0, r]),
                                  ssem.at[src_slot]).start(priority=r % 2)

    def wait_gather(s):
        pltpu.make_async_copy(h2_ref.at[pl.ds(0, bm * sub), :], xbuf.at[s], gsem.at[s]).wait()

    def wait_scatter(s):
        pltpu.make_async_copy(obuf.at[s], y_ref.at[pl.ds(0, bm * sub), :], ssem.at[s]).wait()

    @pl.when((i == 0) & active)
    def _():
        obuf[...] = jnp.zeros_like(obuf)
        gather(sc_ref, 0, 0, bm)

    prev = be_ref[jnp.maximum(i - 1, 0)]

    @pl.when(active & ((i == 0) | (be_ref[i] != prev)))
    def _():
        wg_s[...] = wg_ref[...].astype(BF16)
        wu_s[...] = wu_ref[...].astype(BF16)
        wd_s[...] = wd_ref[...].astype(BF16)

    @pl.when(active)
    def _():
        wait_gather(slot)
        x = _rows_from_tiles(xbuf.at[slot], bm).astype(BF16)
        hs = []
        for c in range(n_chunk):
            g = _dot(x, wg_s[:, c * cw:(c + 1) * cw])
            u = _dot(x, wu_s[:, c * cw:(c + 1) * cw])
            hs.append(((g * jax.nn.sigmoid(g)) * u).astype(BF16))
            gather(sn_ref, other, c * rows_per, (c + 1) * rows_per)
        h = jnp.concatenate(hs, axis=1)

        @pl.when(i > 0)
        def _():
            wait_scatter(slot)

        for c in range(n_chunk):
            _rows_to_tiles(obuf.at[slot], _dot(h, wd_s[:, c * cw:(c + 1) * cw]), c * cw, bm)
            scatter(ip_ref, other, c * rows_per, (c + 1) * rows_per)

    @pl.when(i == n_act - 1)
    def _():
        wait_gather(other)
        wait_scatter(other)
        scatter(ic_ref, slot, 0, bm)
        wait_scatter(slot)


def _moe(h2, info, blk_expert, n_active, wg, wu, wd, layer, n_tok):
    bm = EXPERT_BM
    n_rows = info.shape[0]
    n_blk = n_rows // bm
    info3 = info.reshape(n_blk, 1, bm)
    src3 = jnp.where(info3 >= TOP_K * n_tok, 0, info3 % n_tok)
    wspec = lambda k, n: pl.BlockSpec((None, None, k, n), lambda i, be, na: (layer, be[i], 0, 0))
    ispec = lambda f: pl.BlockSpec((1, 1, bm), lambda i, be, na: (f(i), 0, 0), memory_space=pltpu.SMEM)
    return pl.pallas_call(
        _moe_kernel,
        grid_spec=pltpu.PrefetchScalarGridSpec(
            num_scalar_prefetch=2,
            grid=(n_blk,),
            in_specs=[ispec(lambda i: jnp.maximum(i - 1, 0)), ispec(lambda i: i),
                      ispec(lambda i: i), ispec(lambda i: jnp.minimum(i + 1, n_blk - 1)),
                      pl.BlockSpec(memory_space=pl.ANY),
                      wspec(D_MODEL, EXPERT_FF), wspec(D_MODEL, EXPERT_FF), wspec(EXPERT_FF, D_MODEL)],
            out_specs=pl.BlockSpec(memory_space=pl.ANY),
            scratch_shapes=[pltpu.VMEM((2, bm * ROW_TILE, 128), F32), pltpu.VMEM((2, bm * ROW_TILE, 128), F32),
                            pltpu.SemaphoreType.DMA((2,)), pltpu.SemaphoreType.DMA((2,)),
                            pltpu.VMEM((D_MODEL, EXPERT_FF), BF16), pltpu.VMEM((D_MODEL, EXPERT_FF), BF16),
                            pltpu.VMEM((EXPERT_FF, D_MODEL), BF16)]),
        out_shape=jax.ShapeDtypeStruct(((TOP_K * n_tok + n_rows) * ROW_TILE, 128), F32),
        compiler_params=_cparams(("arbitrary",)),
        name="moe_experts",
    )(blk_expert, n_active, info3, info3, src3, src3, h2, wg, wu, wd)


def _dispatch_info(e_idx, rank, counts, n_tok):
    bm = EXPERT_BM
    counts = counts.astype(jnp.int32)
    padded = (counts + bm - 1) // bm * bm
    pad_end = jnp.cumsum(padded)
    pad_start = pad_end - padded
    experts = jnp.arange(N_EXPERTS, dtype=jnp.int32)
    start = jnp.sum(jnp.where(e_idx[..., None] == experts, pad_start, 0), axis=-1)
    dest = start + rank
    n_rows = -(-(n_tok * TOP_K) // bm) * bm + N_EXPERTS * bm
    n_blk = n_rows // bm
    spill = TOP_K * n_tok + jnp.arange(n_rows, dtype=jnp.int32)
    info = spill.at[dest.reshape(-1)].set(jnp.arange(TOP_K * n_tok, dtype=jnp.int32))
    blk_start = jnp.arange(n_blk, dtype=jnp.int32) * bm
    blk_expert = jnp.minimum(jnp.sum((blk_start[:, None] >= pad_end[None, :]).astype(jnp.int32), axis=-1),
                             N_EXPERTS - 1)
    n_active = (pad_end[-1] // bm).astype(jnp.int32).reshape(1)
    return info, blk_expert, n_active


def _residual_kernel(x_ref, y0_ref, y1_ref, w_ref, g2_ref, gf_ref, o_ref, *, final):
    w = w_ref[...]
    tm = x_ref.shape[0]
    moe = _rows_from_tiles(y0_ref, tm) * w[:, 0:1] + _rows_from_tiles(y1_ref, tm) * w[:, 1:2]
    xn = x_ref[...] + g2_ref[0, 0] * moe
    if final:
        y = xn * lax.rsqrt(jnp.mean(xn * xn, axis=-1, keepdims=True) + EPS)
        xn = y * gf_ref[...]
    o_ref[...] = xn


def _residual(x, y, w_sel, mod4, gain_final, n_tok, final):
    tm = TM
    n_tiles = n_tok // tm
    row = _mod_row(tm)
    tok = lambda w: pl.BlockSpec((tm, w), lambda i: (i, 0))
    return pl.pallas_call(
        functools.partial(_residual_kernel, final=final),
        grid=(n_tiles,),
        in_specs=[tok(D_MODEL), pl.BlockSpec((tm * ROW_TILE, 128), lambda i: (i, 0)),
                  pl.BlockSpec((tm * ROW_TILE, 128), lambda i: (n_tiles + i, 0)), tok(TOP_K),
                  pl.BlockSpec((1, 1, 1, D_MODEL), lambda i: (row(i), 5, 0, 0)),
                  pl.BlockSpec((1, D_MODEL), lambda i: (0, 0))],
        out_specs=tok(D_MODEL),
        out_shape=jax.ShapeDtypeStruct((n_tok if final else T_ALL, D_MODEL), F32),
        compiler_params=_cparams(("parallel",)),
        name="moe_residual",
    )(x, y, y, w_sel, mod4, gain_final)


def kernel(x, c, ctx, c_ctx, w_mod, b_mod, norm_mix, norm_ffn, w_in, attn_sink, conv_w, conv_b, filt_w1, filt_b1, filt_freq, filt_w2, filt_b2, filt_w3, hyena_bias, w_branch_attn, w_branch_fnet, w_branch_hyena, w_out, w_router, b_router, w_exp_gate, w_exp_up, w_exp_down, norm_final):
    mats = _dft_mats()
    cos_t, sin_t = _rope_tables()
    c8 = jnp.concatenate([c, c_ctx[None, :], jnp.zeros((8 - BATCH - 1, D_MODEL), F32)], axis=0)
    mod_all = _modulation(c8, w_mod, b_mod)
    xa = jnp.concatenate([x.reshape(T_LAT, D_MODEL), ctx.reshape(T_CTX, D_MODEL)], axis=0)
    wrt = w_router.T.astype(BF16)
    br = b_router.astype(F32).reshape(N_EXPERTS, 1)
    gain_final = norm_final.reshape(1, D_MODEL)
    out = None
    for l in range(DEPTH):
        last = l == DEPTH - 1
        with_ctx = not last
        n_tok = T_LAT if last else T_ALL
        mod4 = mod_all[l].reshape(8, N_MOD, 1, D_MODEL)
        p = _norm_proj(xa, norm_mix[l].reshape(1, D_MODEL), mod4, w_in, l, T_ALL)
        ya = _attention(p, attn_sink[l], cos_t, sin_t, with_ctx)
        yf = _fourier_mix(p, mats, with_ctx)
        z3 = _short_conv(p, conv_w[l], conv_b[l], with_ctx)
        filt = (filt_w1[l], filt_b1[l], filt_freq[l], filt_w2[l], filt_b2[l], filt_w3[l])
        yh = _hyena_mix(z3, filt, hyena_bias[l], mats, with_ctx)
        xa, h2, e_idx, w_sel, rank, cnt = _merge(
            xa, ya, yf, yh, p, w_branch_attn[l].astype(BF16), w_branch_fnet[l].astype(BF16),
            w_branch_hyena[l].astype(BF16), w_out[l].astype(BF16), mod4,
            norm_ffn[l].reshape(1, D_MODEL), wrt, br, n_tok)
        info, blk_expert, n_active = _dispatch_info(e_idx, rank, cnt[:, 0], n_tok)
        y = _moe(h2, info, blk_expert, n_active, w_exp_gate, w_exp_up, w_exp_down, l, n_tok)
        res = _residual(xa, y, w_sel.T, mod4, gain_final, n_tok, last)
        if last:
            out = res
        else:
            xa = res
    return out.reshape(BATCH, SEQ, D_MODEL)
```

```python
import functools
import math

import jax
import jax.numpy as jnp
from jax import lax
from jax.experimental import pallas as pl
from jax.experimental.pallas import tpu as pltpu

F32 = jnp.float32
BF16 = jnp.bfloat16

D_MODEL = 1024
BATCH = 4
SEQ = 4096
DEPTH = 4
GRID_W = 64
CTX_LEN = 256
EPS = 1e-6
N_MOD = 6

HEAD_DIM = 64
N_Q_HEADS = 8
N_KV_HEADS = 2
Q_PER_KV = N_Q_HEADS // N_KV_HEADS
ATTN_BLOCK = 128
ROPE_BASE = 10000.0

FNET_GROUPS = 4
FNET_GROUP_DIM = 128
FNET_WIDTH = FNET_GROUPS * FNET_GROUP_DIM

HYENA_WIDTH = 512
HYENA_ORDER = 2
FILTER_EMB = 33
FILTER_BANDS = (FILTER_EMB - 1) // 2
FILTER_HIDDEN = 64
DECAY_TARGET = 1e-2
FAST_DECAY_PCT = 0.3
SLOW_DECAY_PCT = 1.5

ATTN_WIDTH = N_Q_HEADS * HEAD_DIM
KV_WIDTH = N_KV_HEADS * HEAD_DIM
Q_OFF = 0
K_OFF = Q_OFF + ATTN_WIDTH
V_OFF = K_OFF + KV_WIDTH
F_OFF = V_OFF + KV_WIDTH
H_OFF = F_OFF + FNET_WIDTH
G_OFF = H_OFF + (HYENA_ORDER + 1) * HYENA_WIDTH
IN_WIDTH = G_OFF + 3 * D_MODEL

N_EXPERTS = 16
N_GROUPS = 4
EXPERTS_PER_GROUP = N_EXPERTS // N_GROUPS
TOP_K = 2
EXPERT_FF = 1024

T_LAT = BATCH * SEQ
T_CTX = BATCH * CTX_LEN
T_ALL = T_LAT + T_CTX

PG_OFF = G_OFF
PH_OFF = H_OFF
PF_OFF = F_OFF
PQ_OFF = Q_OFF
PK_OFF = K_OFF
PV_OFF = V_OFF

HY_N = 2 * SEQ
HY_N2 = 64
HY_N1 = HY_N // HY_N2
FN_N = 64

PITCH = HY_N2 + 8
SEQ_PAD = SEQ // HY_N2 * PITCH
T_PAD = BATCH * SEQ_PAD + T_CTX
UNROLL = 16
TM = 512
EXPERT_BM = 512
VMEM_LIMIT = 52 * 1024 * 1024


def _cparams(sem, vmem=VMEM_LIMIT):
    return pltpu.CompilerParams(dimension_semantics=sem, vmem_limit_bytes=vmem)


def _dot(a, b):
    return jnp.dot(a, b, preferred_element_type=F32)


def _cis(expo, n):
    ang = (2.0 * math.pi / n) * jnp.mod(expo, n).astype(F32)
    return jnp.cos(ang), jnp.sin(ang)


def _real_form(gr, gi):
    return jnp.concatenate([jnp.concatenate([gr, -gi], axis=-1), jnp.concatenate([gi, gr], axis=-1)], axis=-2)


def _dft_mats():
    ar = lambda n: jnp.arange(n, dtype=jnp.int32)
    m = {}
    c, s = _cis(ar(HY_N1)[:, None] * ar(HY_N1 // 2)[None, :], HY_N1)
    m['hy_lead_f'] = _real_form(c, -s).astype(BF16)
    c, s = _cis(ar(HY_N1 // 2)[:, None] * ar(HY_N1)[None, :], HY_N1)
    m['hy_lead_i'] = _real_form(c, s).astype(BF16)
    c, s = _cis(ar(HY_N1)[:, None] * ar(HY_N1 // 2)[None, :], HY_N1)
    m['hy_lead_kh'] = jnp.concatenate([c, -s], axis=0).astype(BF16)
    a = ar(HY_N1)[:, None, None]
    k2 = ar(HY_N2)[None, :, None]
    n2 = ar(HY_N2)[None, None, :]
    c, s = _cis(n2 * (a + HY_N1 * k2), HY_N)
    m['hy_slab_f2'] = jnp.concatenate([c, -s], axis=-1).astype(BF16)
    ct = jnp.swapaxes(c, 1, 2) * (1.0 / HY_N)
    st = jnp.swapaxes(s, 1, 2) * (1.0 / HY_N)
    m['hy_slab_i2'] = jnp.concatenate([ct, st], axis=-1).astype(BF16)
    nc = 2 * CTX_LEN
    c, s = _cis(ar(nc)[:, None] * ar(CTX_LEN)[None, :], nc)
    m['hc_f'] = _real_form(c, -s).astype(BF16)
    c, s = _cis(ar(CTX_LEN)[:, None] * ar(nc)[None, :], nc)
    m['hc_i'] = _real_form(c * (1.0 / nc), s * (1.0 / nc)).astype(BF16)
    c, s = _cis(ar(nc)[:, None] * ar(CTX_LEN)[None, :], nc)
    m['hc_kh'] = jnp.concatenate([c, -s], axis=0).astype(BF16)
    c, s = _cis(ar(FNET_GROUP_DIM)[:, None] * ar(FNET_GROUP_DIM)[None, :], FNET_GROUP_DIM)
    m['fn_chan'] = jnp.concatenate([c, -s], axis=1).astype(BF16)
    c, s = _cis(ar(FN_N)[:, None] * ar(FN_N)[None, :], FN_N)
    m['fn_lead'] = _real_form(c, -s).astype(BF16)
    a = ar(FN_N)[:, None, None]
    k1 = ar(FN_N)[None, :, None]
    n1 = ar(FN_N)[None, None, :]
    scale = 1.0 / math.sqrt(SEQ * FNET_GROUP_DIM)
    c, s = _cis(n1 * (a + FN_N * k1), SEQ)
    m['fn_slab'] = jnp.concatenate([c * scale, s * scale], axis=-1).astype(BF16)
    scale = 1.0 / math.sqrt(CTX_LEN * FNET_GROUP_DIM)
    c, s = _cis(ar(CTX_LEN)[:, None] * ar(CTX_LEN)[None, :], CTX_LEN)
    m['fc'] = jnp.concatenate([c * scale, s * scale], axis=-1).astype(BF16)
    return m


def _mod_kernel(c_ref, w_ref, b_ref, o_ref):
    c = c_ref[...]
    s = c * jax.nn.sigmoid(c)
    o_ref[0] = _dot(s.astype(BF16), w_ref[0].astype(BF16)) + b_ref[0]


def _modulation(c8, w_mod, b_mod):
    tn = 1536
    n = N_MOD * D_MODEL
    return pl.pallas_call(
        _mod_kernel,
        grid=(DEPTH, n // tn),
        in_specs=[pl.BlockSpec((8, D_MODEL), lambda l, j: (0, 0)),
                  pl.BlockSpec((1, D_MODEL, tn), lambda l, j: (l, 0, j)),
                  pl.BlockSpec((1, 1, tn), lambda l, j: (l, 0, j))],
        out_specs=pl.BlockSpec((1, 8, tn), lambda l, j: (l, 0, j)),
        out_shape=jax.ShapeDtypeStruct((DEPTH, 8, n), F32),
        compiler_params=_cparams(("parallel", "parallel")),
        name="adaln_modulation",
    )(c8, w_mod, b_mod.reshape(DEPTH, 1, n))


def _mod_row(tm):
    tiles_per_batch = SEQ // tm
    return lambda i: jnp.minimum(i // tiles_per_batch, BATCH)


def _rms_mod(x, g, sh, sc):
    y = x * lax.rsqrt(jnp.mean(x * x, axis=-1, keepdims=True) + EPS)
    return (y * g) * (1.0 + sc) + sh


def _norm_proj_kernel(x_ref, g_ref, sh_ref, sc_ref, w_ref, o_ref, w_s):
    @pl.when(pl.program_id(1) == 0)
    def _():
        w_s[...] = w_ref[...].astype(BF16)

    h = _rms_mod(x_ref[...], g_ref[...], sh_ref[0, 0], sc_ref[0, 0]).astype(BF16)
    o_ref[...] = _dot(h, w_s[...]).astype(o_ref.dtype)


def _norm_proj(x, gain, mod4, w, layer, n_tok):
    tm = TM
    n_out = w.shape[2]
    tn = n_out // 2
    row = _mod_row(tm)
    return pl.pallas_call(
        _norm_proj_kernel,
        grid=(n_out // tn, n_tok // tm),
        in_specs=[pl.BlockSpec((tm, D_MODEL), lambda j, i: (i, 0)),
                  pl.BlockSpec((1, D_MODEL), lambda j, i: (0, 0)),
                  pl.BlockSpec((1, 1, 1, D_MODEL), lambda j, i: (row(i), 0, 0, 0)),
                  pl.BlockSpec((1, 1, 1, D_MODEL), lambda j, i: (row(i), 1, 0, 0)),
                  pl.BlockSpec((None, D_MODEL, tn), lambda j, i: (layer, 0, j))],
        out_specs=pl.BlockSpec((tm, tn), lambda j, i: (i, j)),
        out_shape=jax.ShapeDtypeStruct((T_ALL, n_out), BF16),
        scratch_shapes=[pltpu.VMEM((D_MODEL, tn), BF16)],
        compiler_params=_cparams(("arbitrary", "arbitrary")),
        name="norm_in_proj",
    )(x, gain, mod4, mod4, w)


def _softmax_pv(qh, k_parts, v_parts, masks, sink):
    nt = (((1,), (1,)), ((), ()))
    scores = []
    for kp, mk in zip(k_parts, masks):
        s = lax.dot_general(qh, kp, nt, preferred_element_type=F32)
        if mk is not None:
            s = jnp.where(mk, s, -1e30)
        scores.append(s)
    m = sink
    for s in scores:
        m = jnp.maximum(m, jnp.max(s, axis=-1, keepdims=True))
    es = [jnp.exp(s - m) for s in scores]
    den = jnp.exp(sink - m)
    for e in es:
        den = den + jnp.sum(e, axis=-1, keepdims=True)
    inv = 1.0 / den
    o = None
    for e, vp in zip(es, v_parts):
        t = _dot((e * inv).astype(BF16), vp)
        o = t if o is None else o + t
    return o


def _attn_kernel(sink_ref, q_ref, km_ref, k0_ref, kp_ref, vm_ref, v0_ref, vp_ref, kc_ref, vc_ref,
                 cos_ref, sin_ref, psw_ref, p64_ref, o_ref, *, nb):
    n = pl.program_id(1)
    blk = ATTN_BLOCK
    psw = psw_ref[...]
    p64 = p64_ref[...]

    def rope(xb, blk_idx):
        r0 = pl.multiple_of(blk_idx * blk, blk)
        return xb.astype(F32) * cos_ref[pl.ds(r0, blk), :] + _dot(xb, psw) * sin_ref[pl.ds(r0, blk), :]

    nm = jnp.maximum(n - 1, 0)
    npl = jnp.minimum(n + 1, nb - 1)
    kall = jnp.concatenate([rope(km_ref[...], nm), rope(k0_ref[...], n), rope(kp_ref[...], npl),
                            kc_ref[...].astype(F32)], axis=0)
    vall = jnp.concatenate([vm_ref[...], v0_ref[...], vp_ref[...], vc_ref[...]], axis=0)
    nk = kall.shape[0]
    lo = lax.broadcasted_iota(jnp.int32, (nk, 128), 1) < HEAD_DIM

    r = lax.broadcasted_iota(jnp.int32, (blk, blk), 0)
    cidx = lax.broadcasted_iota(jnp.int32, (blk, blk), 1)
    ok_prev = jnp.where(cidx >= r, (n > 0).astype(jnp.int32), 0) > 0
    ok_next = jnp.where(cidx <= r, (n < nb - 1).astype(jnp.int32), 0) > 0
    neg = jnp.float32(-1e30)

    scale = HEAD_DIM ** -0.5
    q2 = [(rope(q_ref[:, p * 128:(p + 1) * 128], n) * scale).astype(BF16) for p in range(N_Q_HEADS // 2)]
    nt = (((1,), (1,)), ((), ()))
    n_pair = Q_PER_KV // 2
    for h in range(N_KV_HEADS):
        kh = (jnp.where(lo, kall, 0.0) if h == 0 else jnp.where(lo, 0.0, kall)).astype(BF16)
        vh = jnp.where(lo, vall, jnp.zeros_like(vall)) if h == 0 else jnp.where(lo, jnp.zeros_like(vall), vall)
        q4 = jnp.concatenate(q2[n_pair * h:n_pair * (h + 1)], axis=0)
        q4s = _dot(q4, p64).astype(BF16)
        first, second = (q4, q4s) if h == 0 else (q4s, q4)
        s = lax.dot_general(jnp.concatenate([first, second], axis=0), kh, nt, preferred_element_type=F32)
        probs = []
        for rb in range(2 * n_pair):
            sink = sink_ref[Q_PER_KV * h + 2 * (rb % n_pair) + rb // n_pair]
            sb = s[rb * blk:(rb + 1) * blk]
            parts = [jnp.where(ok_prev, sb[:, 0:blk], neg), sb[:, blk:2 * blk],
                     jnp.where(ok_next, sb[:, 2 * blk:3 * blk], neg), sb[:, 3 * blk:nk]]
            tiles = parts[:3] + [parts[3][:, t * blk:(t + 1) * blk] for t in range((nk - 3 * blk) // blk)]
            m = jnp.maximum(sink, jnp.max(functools.reduce(jnp.maximum, tiles), axis=-1, keepdims=True))
            es = [jnp.exp(t - m) for t in tiles]
            den = jnp.exp(sink - m) + jnp.sum(functools.reduce(jnp.add, es), axis=-1, keepdims=True)
            inv = 1.0 / den
            probs.append(jnp.concatenate([(e * inv).astype(BF16) for e in es], axis=1))
        o = _dot(jnp.concatenate(probs, axis=0), vh)
        half = n_pair * blk
        if h == 0:
            out = o[:half] + pltpu.roll(o[half:], HEAD_DIM, 1)
        else:
            out = pltpu.roll(o[:half], HEAD_DIM, 1) + o[half:]
        w0 = h * Q_PER_KV * HEAD_DIM
        for j in range(n_pair):
            o_ref[:, w0 + 128 * j:w0 + 128 * (j + 1)] = out[j * blk:(j + 1) * blk].astype(o_ref.dtype)


def _ctx_attn_kernel(sink_ref, q_ref, kc_ref, vc_ref, o_ref):
    kc = kc_ref[...].astype(BF16)
    vc = vc_ref[...].astype(BF16)
    scale = HEAD_DIM ** -0.5
    outs = []
    for pair in range(N_Q_HEADS // 2):
        q2 = (q_ref[:, pair * 128:(pair + 1) * 128] * scale).astype(BF16)
        for sub in range(2):
            head = 2 * pair + sub
            kvh = head // Q_PER_KV
            sl = slice(kvh * HEAD_DIM, (kvh + 1) * HEAD_DIM)
            qh = q2[:, sub * HEAD_DIM:(sub + 1) * HEAD_DIM]
            outs.append(_softmax_pv(qh, [kc[:, sl]], [vc[:, sl]], [None], sink_ref[head]))
    o_ref[...] = jnp.concatenate(outs, axis=-1).astype(o_ref.dtype)


def _attention(p, sink, cos_t, sin_t, with_ctx):
    blk = ATTN_BLOCK
    nb = SEQ // blk
    qc, kcol, vcol = PQ_OFF // ATTN_WIDTH, PK_OFF // KV_WIDTH, PV_OFF // KV_WIDTH
    ctx_blk = T_LAT // CTX_LEN
    smem = pl.BlockSpec(memory_space=pltpu.SMEM)

    def kv_spec(col, d):
        return pl.BlockSpec((blk, KV_WIDTH),
                            lambda b, n: (b * nb + jnp.clip(n + d, 0, nb - 1), col))

    ya = pl.pallas_call(
        functools.partial(_attn_kernel, nb=nb),
        grid=(BATCH, nb),
        in_specs=[smem,
                  pl.BlockSpec((blk, ATTN_WIDTH), lambda b, n: (b * nb + n, qc)),
                  kv_spec(kcol, -1), kv_spec(kcol, 0), kv_spec(kcol, 1),
                  kv_spec(vcol, -1), kv_spec(vcol, 0), kv_spec(vcol, 1),
                  pl.BlockSpec((CTX_LEN, KV_WIDTH), lambda b, n: (ctx_blk + b, kcol)),
                  pl.BlockSpec((CTX_LEN, KV_WIDTH), lambda b, n: (ctx_blk + b, vcol)),
                  pl.BlockSpec((SEQ, KV_WIDTH), lambda b, n: (0, 0)),
                  pl.BlockSpec((SEQ, KV_WIDTH), lambda b, n: (0, 0)),
                  pl.BlockSpec((128, 128), lambda b, n: (0, 0)),
                  pl.BlockSpec((128, 128), lambda b, n: (0, 0))],
        out_specs=pl.BlockSpec((blk, ATTN_WIDTH), lambda b, n: (b * nb + n, 0)),
        out_shape=jax.ShapeDtypeStruct((T_ALL, ATTN_WIDTH), BF16),
        compiler_params=_cparams(("parallel", "parallel")),
        name="banded_attention",
    )(sink, p, p, p, p, p, p, p, p, p, cos_t, sin_t, *_lane_perms())
    if not with_ctx:
        return ya
    cb = CTX_LEN // blk
    lat_blk = T_LAT // blk

    def alias_kernel(sink_ref, q_ref, kc_ref, vc_ref, ya_in_ref, o_ref):
        del ya_in_ref
        _ctx_attn_kernel(sink_ref, q_ref, kc_ref, vc_ref, o_ref)

    return pl.pallas_call(
        alias_kernel,
        grid=(BATCH, cb),
        in_specs=[smem,
                  pl.BlockSpec((blk, ATTN_WIDTH), lambda b, n: (lat_blk + b * cb + n, qc)),
                  pl.BlockSpec((CTX_LEN, KV_WIDTH), lambda b, n: (ctx_blk + b, kcol)),
                  pl.BlockSpec((CTX_LEN, KV_WIDTH), lambda b, n: (ctx_blk + b, vcol)),
                  pl.BlockSpec(memory_space=pl.ANY)],
        out_specs=pl.BlockSpec((blk, ATTN_WIDTH), lambda b, n: (lat_blk + b * cb + n, 0)),
        out_shape=jax.ShapeDtypeStruct((T_ALL, ATTN_WIDTH), BF16),
        input_output_aliases={4: 0},
        compiler_params=_cparams(("parallel", "parallel")),
        name="context_attention",
    )(sink, p, p, p, ya)


def _rope_tables():
    n_freq = HEAD_DIM // 4
    freqs = ROPE_BASE ** (-jnp.arange(n_freq, dtype=F32) / n_freq)
    t = jnp.arange(SEQ, dtype=jnp.int32)
    rows = (t // GRID_W).astype(F32)[:, None] * freqs
    cols = (t % GRID_W).astype(F32)[:, None] * freqs
    cos_h = jnp.concatenate([jnp.cos(rows), jnp.cos(rows), jnp.cos(cols), jnp.cos(cols)], axis=-1)
    sin_h = jnp.concatenate([-jnp.sin(rows), jnp.sin(rows), -jnp.sin(cols), jnp.sin(cols)], axis=-1)
    return jnp.tile(cos_h, (1, 2)), jnp.tile(sin_h, (1, 2))


def _lane_perms():
    j = jnp.arange(128, dtype=jnp.int32)[:, None]
    l = jnp.arange(128, dtype=jnp.int32)[None, :]
    rot = jnp.where(l % 32 < 16, l + 16, l - 16)
    return (j == rot).astype(BF16), (j == (l + HEAD_DIM) % 128).astype(BF16)


def _lead_kernel(*refs, n_in, cmul, epi):
    m_ref = refs[0]
    x_refs = refs[1:1 + n_in]
    pos = 1 + n_in
    xs = []
    for r in x_refs:
        v = r[...]
        xs.append(v.reshape(-1, v.shape[-1]))
    x = xs[0] if n_in == 1 else jnp.concatenate(xs, axis=0)
    if cmul:
        k = refs[pos][...]
        pos += 1
        k = k.reshape(-1, k.shape[-1])
        half = x.shape[0] // 2
        xr, xi, kr, ki = x[:half], x[half:], k[:half], k[half:]
        x = jnp.concatenate([xr * kr - xi * ki, xr * ki + xi * kr], axis=0)
    res = _dot(m_ref[...], x.astype(BF16))
    if epi:
        g_ref, y_ref, b_ref = refs[pos:pos + 3]
        pos += 3
        g = g_ref[...]
        y = y_ref[...]
        res = g.reshape(-1, g.shape[-1]) * (res + y.reshape(-1, y.shape[-1]) * b_ref[...])
    o_ref = refs[pos]
    o_ref[...] = res.reshape(o_ref.shape).astype(o_ref.dtype)


def _lead(mat, xs, x_specs, grid, out_shape, out_spec, *, kspec=None, epi=None, alias_to=None, name):
    ins = [mat] + list(xs)
    specs = [pl.BlockSpec(mat.shape, lambda *a: (0, 0))] + list(x_specs)
    if kspec is not None:
        ins.append(kspec[0])
        specs.append(kspec[1])
    if epi is not None:
        for arr, sp in epi:
            ins.append(arr)
            specs.append(sp)
    kern = functools.partial(_lead_kernel, n_in=len(xs), cmul=kspec is not None, epi=epi is not None)
    aliases = {}
    if alias_to is not None:
        aliases = {len(ins): 0}
        ins.append(alias_to)
        specs.append(pl.BlockSpec(memory_space=pl.ANY))
        inner = kern

        def kern(*refs):
            inner(*refs[:-2], refs[-1])

    return pl.pallas_call(
        kern, grid=grid, in_specs=specs, out_specs=out_spec, out_shape=out_shape,
        input_output_aliases=aliases,
        compiler_params=_cparams(("parallel",) * len(grid)), name=name,
    )(*ins)


def _cstack(xr, xi):
    return jnp.concatenate([jnp.concatenate([xr, xi], axis=1), jnp.concatenate([-xi, xr], axis=1)], axis=0)


def _fnet_kernel(u_ref, mc_ref, ml_ref, ms_ref, o_ref, zr, zi, are, aim):
    n = FN_N
    pitch = PITCH
    gd = FNET_GROUP_DIM
    rows = 4 * n
    mc = mc_ref[...]

    def chan(i, carry):
        r_in = pl.multiple_of(i * rows, rows)
        z = _dot(u_ref[pl.ds(r_in, rows), :].astype(BF16), mc)
        for q in range(rows // n):
            r_out = pl.multiple_of((i * (rows // n) + q) * pitch, 8)
            zr[pl.ds(r_out, n), :] = z[q * n:(q + 1) * n, :gd]
            zi[pl.ds(r_out, n), :] = z[q * n:(q + 1) * n, gd:]
        return carry

    lax.fori_loop(0, SEQ // rows, chan, 0)
    ml = ml_ref[...]

    def lead(i, carry):
        for u in range(UNROLL):
            n1 = i * UNROLL + u
            x = jnp.concatenate([zr[pl.ds(n1, n, stride=pitch), :], zi[pl.ds(n1, n, stride=pitch), :]],
                                axis=0).astype(BF16)
            r = _dot(ml, x)
            are[pl.ds(n1, n, stride=pitch), :] = r[:n]
            aim[pl.ds(n1, n, stride=pitch), :] = r[n:]
        return carry

    lax.fori_loop(0, n // UNROLL, lead, 0)

    def slab(i, carry):
        for u in range(UNROLL):
            k2 = i * UNROLL + u
            r0 = pl.multiple_of(k2 * pitch, 8)
            x = jnp.concatenate([are[pl.ds(r0, n), :], aim[pl.ds(r0, n), :]], axis=0).astype(BF16)
            o_ref[pl.ds(k2, n, stride=n), :] = _dot(ms_ref[k2], x)
        return carry

    lax.fori_loop(0, n // UNROLL, slab, 0)


def _fnet_ctx_kernel(u_ref, mc_ref, mf_ref, yf_in_ref, o_ref):
    del yf_in_ref
    gd = FNET_GROUP_DIM
    mc = mc_ref[...]
    mf = mf_ref[...]
    for g in range(FNET_GROUPS):
        z = _dot(u_ref[:, g * gd:(g + 1) * gd].astype(BF16), mc)
        x = jnp.concatenate([z[:, :gd], z[:, gd:]], axis=0).astype(BF16)
        o_ref[:, g * gd:(g + 1) * gd] = _dot(mf, x)


def _fourier_mix(p, mats, with_ctx):
    gd = FNET_GROUP_DIM
    col0 = PF_OFF // gd
    full = lambda a: pl.BlockSpec(a.shape, lambda *i: (0,) * a.ndim)
    scr = pltpu.VMEM((FN_N * PITCH, gd), F32)
    yf = pl.pallas_call(
        _fnet_kernel,
        grid=(BATCH, FNET_GROUPS),
        in_specs=[pl.BlockSpec((SEQ, gd), lambda b, g: (b, col0 + g)),
                  full(mats['fn_chan']), full(mats['fn_lead']), full(mats['fn_slab'])],
        out_specs=pl.BlockSpec((SEQ, gd), lambda b, g: (b, g)),
        out_shape=jax.ShapeDtypeStruct((T_ALL if with_ctx else T_LAT, FNET_WIDTH), F32),
        scratch_shapes=[scr, scr, scr, scr],
        compiler_params=_cparams(("parallel", "parallel")),
        name="fnet_latent",
    )(p, mats['fn_chan'], mats['fn_lead'], mats['fn_slab'])
    if not with_ctx:
        return yf
    blk0 = T_LAT // CTX_LEN
    return pl.pallas_call(
        _fnet_ctx_kernel,
        grid=(BATCH,),
        in_specs=[pl.BlockSpec((pl.Element(CTX_LEN), pl.Element(FNET_WIDTH)),
                               lambda b: ((blk0 + b) * CTX_LEN, PF_OFF)),
                  full(mats['fn_chan']), full(mats['fc']), pl.BlockSpec(memory_space=pl.ANY)],
        out_specs=pl.BlockSpec((CTX_LEN, FNET_WIDTH), lambda b: (blk0 + b, 0)),
        out_shape=jax.ShapeDtypeStruct((T_ALL, FNET_WIDTH), F32),
        input_output_aliases={3: 0},
        compiler_params=_cparams(("parallel",)),
        name="fnet_ctx",
    )(p, mats['fn_chan'], mats['fc'], yf)


def _short_conv_kernel(u_ref, w_ref, b_ref, o_ref, *, rows, chunk, padded):
    w0 = w_ref[0:1, :]
    w1 = w_ref[1:2, :]
    w2 = w_ref[2:3, :]
    bias = b_ref[...]
    width = u_ref.shape[-1]
    ridx = lax.broadcasted_iota(jnp.int32, (chunk, width), 0)
    n_chunks = rows // chunk
    for ci in range(n_chunks):
        r0 = ci * chunk
        cur = u_ref[r0:r0 + chunk, :].astype(F32)
        if ci > 0:
            prev_row = u_ref[r0 - 16:r0, :].astype(F32)[15:16, :]
        else:
            prev_row = jnp.zeros((1, width), F32)
        if ci < n_chunks - 1:
            next_row = u_ref[r0 + chunk:r0 + chunk + 16, :].astype(F32)[0:1, :]
        else:
            next_row = jnp.zeros((1, width), F32)
        up = jnp.where(ridx == 0, prev_row, pltpu.roll(cur, 1, 0))
        dn = jnp.where(ridx == chunk - 1, next_row, pltpu.roll(cur, chunk - 1, 0))
        res = up * w0 + cur * w1 + dn * w2 + bias
        if not padded:
            o_ref[0, r0:r0 + chunk, :] = res
            continue
        for q in range(chunk // HY_N2):
            p0 = (r0 // HY_N2 + q) * PITCH
            o_ref[0, p0:p0 + HY_N2, :] = res[q * HY_N2:(q + 1) * HY_N2]
            o_ref[0, p0 + HY_N2:p0 + PITCH, :] = jnp.zeros((PITCH - HY_N2, width), F32)


def _short_conv(p, conv_w, conv_b, with_ctx):
    cw = 256
    hw = (HYENA_ORDER + 1) * HYENA_WIDTH
    ncol = hw // cw
    per = HYENA_WIDTH // cw
    col0 = PH_OFF // cw
    out_shape = jax.ShapeDtypeStruct((HYENA_ORDER + 1, T_PAD if with_ctx else BATCH * SEQ_PAD, HYENA_WIDTH), F32)
    b2 = conv_b.reshape(1, hw)

    def call(rows, blk0, alias):
        padded = rows == SEQ
        out_rows = SEQ_PAD if padded else rows
        out_blk0 = 0 if padded else BATCH * SEQ_PAD // rows
        kern = functools.partial(_short_conv_kernel, rows=rows, chunk=min(rows, 256), padded=padded)
        ins = [p, conv_w, b2]
        specs = [pl.BlockSpec((rows, cw), lambda b, j: (blk0 + b, col0 + j)),
                 pl.BlockSpec((3, cw), lambda b, j: (0, j)),
                 pl.BlockSpec((1, cw), lambda b, j: (0, j))]
        aliases = {}
        if alias is not None:
            ins.append(alias)
            specs.append(pl.BlockSpec(memory_space=pl.ANY))
            aliases = {3: 0}
            inner = kern

            def kern(u_ref, w_ref, b_ref, a_ref, o_ref):
                del a_ref
                inner(u_ref, w_ref, b_ref, o_ref)

        return pl.pallas_call(
            kern, grid=(BATCH, ncol), in_specs=specs,
            out_specs=pl.BlockSpec((1, out_rows, cw), lambda b, j: (j // per, out_blk0 + b, j % per)),
            out_shape=out_shape, input_output_aliases=aliases,
            compiler_params=_cparams(("parallel", "parallel")), name="hyena_short_conv",
        )(*ins)

    z3 = call(SEQ, 0, None)
    if with_ctx:
        z3 = call(CTX_LEN, T_LAT // CTX_LEN, z3)
    return z3


def _filter_mlp_kernel(ft_ref, w1_ref, b1_ref, fq_ref, w2_ref, b2_ref, o_ref):
    fq = fq_ref[...]
    h = jnp.sin(fq * (_dot(ft_ref[...].astype(BF16), w1_ref[...]) + b1_ref[...]))
    h = jnp.sin(fq * (_dot(h.astype(BF16), w2_ref[...]) + b2_ref[...]))
    o_ref[...] = h.astype(o_ref.dtype)


def _filter_kernel(h_ref, t_ref, w3f_ref, w3b_ref, dl_ref, m1_ref, *rest, n, dense):
    hb = h_ref[...]
    decay = jnp.exp(-t_ref[...] * dl_ref[...])
    tf = _dot(hb, w3f_ref[...]) * decay
    tb = _dot(hb, w3b_ref[...]) * decay
    tb = jnp.where(lax.broadcasted_iota(jnp.int32, tb.shape, 0) == 0, 0.0, tb)
    scale = 1.0 / (jnp.sum(jnp.abs(tf), axis=0, keepdims=True) + jnp.sum(jnp.abs(tb), axis=0, keepdims=True))
    cw = tf.shape[1]
    if dense:
        o_ref = rest[0]
        r = _dot(m1_ref[...], jnp.concatenate([tf, tb], axis=1).astype(BF16))
        nc = r.shape[0] // 2
        o_ref[0:nc, :] = (r[:nc, :cw] + r[:nc, cw:]) * scale
        o_ref[nc:, :] = (r[nc:, :cw] - r[nc:, cw:]) * scale
        return
    f2_ref, o_ref, tf_s, tb_s, are_f, aim_f, are_b, aim_b = rest
    tf_s[...] = tf
    tb_s[...] = tb
    half = HY_N1 // 2
    m1 = m1_ref[...]

    def stage1(i, carry):
        for u in range(UNROLL):
            n2 = i * UNROLL + u
            x = jnp.concatenate([tf_s[pl.ds(n2, half, stride=HY_N2), :], tb_s[pl.ds(n2, half, stride=HY_N2), :]],
                                axis=1).astype(BF16)
            r = _dot(m1, x)
            rows = pl.ds(n2, HY_N1, stride=PITCH)
            are_f[rows, :] = r[:HY_N1, :cw]
            are_b[rows, :] = r[:HY_N1, cw:]
            aim_f[rows, :] = r[HY_N1:, :cw]
            aim_b[rows, :] = r[HY_N1:, cw:]
        return carry

    lax.fori_loop(0, HY_N2 // UNROLL, stage1, 0)

    def stage2(i, carry):
        for u in range(UNROLL):
            k1 = i * UNROLL + u
            rows = pl.ds(pl.multiple_of(k1 * PITCH, 8), HY_N2)
            ar = jnp.concatenate([are_f[rows, :], are_b[rows, :]], axis=1)
            ai = jnp.concatenate([aim_f[rows, :], aim_b[rows, :]], axis=1)
            t = _dot(f2_ref[k1], _cstack(ar, ai).astype(BF16))
            o_ref[0, k1] = (t[:, 0:cw] + t[:, cw:2 * cw]) * scale
            o_ref[1, k1] = (t[:, 2 * cw:3 * cw] - t[:, 3 * cw:4 * cw]) * scale
        return carry

    lax.fori_loop(0, HY_N1 // UNROLL, stage2, 0)


def _filter_feats(n):
    pos = jnp.arange(n, dtype=F32)
    t = pos / max(n - 1, 1)
    omega = 2.0 * math.pi * pos / n
    bands = jnp.linspace(1e-4, FILTER_BANDS - 1, FILTER_BANDS, dtype=F32)
    feats = jnp.concatenate([t[:, None], jnp.cos(omega[:, None] * bands), -jnp.sin(omega[:, None] * bands)], axis=-1)
    return jnp.pad(feats, ((0, 0), (0, 128 - FILTER_EMB))), t[:, None]


def _filter_spectrum(n, filt, mats):
    w1, b1, freq, w2, b2, w3 = filt
    dense = n == CTX_LEN
    cw = 128
    nch = HYENA_WIDTH // cw
    feats, t = _filter_feats(n)
    w1p = jnp.pad(w1, ((0, 128 - FILTER_EMB), (0, 0))).astype(BF16)
    deltas = jnp.abs(jnp.linspace(math.log(DECAY_TARGET) / SLOW_DECAY_PCT, math.log(DECAY_TARGET) / FAST_DECAY_PCT,
                                  HYENA_WIDTH, dtype=F32)).reshape(1, HYENA_WIDTH)
    full = lambda a: pl.BlockSpec(a.shape, lambda *i: (0,) * a.ndim)
    row = lambda a: a.reshape(1, -1)
    w3b16 = w3.astype(BF16)
    tap_spec = lambda d: pl.BlockSpec((FILTER_HIDDEN, cw), lambda o, ch: (0, (o * 2 + d) * nch + ch))
    m1 = mats['hc_kh'] if dense else mats['hy_lead_kh']
    mlp_ins = [feats, w1p, row(b1), row(freq), w2.astype(BF16), row(b2)]
    hb = pl.pallas_call(
        _filter_mlp_kernel, grid=(1,), in_specs=[full(a) for a in mlp_ins],
        out_specs=pl.BlockSpec((n, FILTER_HIDDEN), lambda i: (0, 0)),
        out_shape=jax.ShapeDtypeStruct((n, FILTER_HIDDEN), BF16),
        compiler_params=_cparams(("arbitrary",)), name="hyena_filter_mlp",
    )(*mlp_ins)
    ins = [hb, t, w3b16, w3b16, deltas, m1]
    specs = [full(hb), full(t), tap_spec(0), tap_spec(1), pl.BlockSpec((1, cw), lambda o, ch: (0, ch)), full(m1)]
    if dense:
        nc = 2 * n
        out_shape = jax.ShapeDtypeStruct((HYENA_ORDER, 2 * nc, HYENA_WIDTH), F32)
        out_spec = pl.BlockSpec((None, 2 * nc, cw), lambda o, ch: (o, 0, ch))
        scratch = []
    else:
        ins.append(mats['hy_slab_f2'])
        specs.append(full(mats['hy_slab_f2']))
        out_shape = jax.ShapeDtypeStruct((HYENA_ORDER, 2, HY_N1, HY_N2, HYENA_WIDTH), F32)
        out_spec = pl.BlockSpec((None, 2, HY_N1, HY_N2, cw), lambda o, ch: (o, 0, 0, 0, ch))
        scratch = [pltpu.VMEM((n, cw), F32)] * 2 + [pltpu.VMEM((HY_N1 * PITCH, cw), F32)] * 4
    return pl.pallas_call(
        functools.partial(_filter_kernel, n=n, dense=dense),
        grid=(HYENA_ORDER, nch), in_specs=specs, out_specs=out_spec, out_shape=out_shape,
        scratch_shapes=scratch,
        compiler_params=_cparams(("parallel", "parallel")),
        name="hyena_filter_ctx" if dense else "hyena_filter",
    )(*ins)


def _hyena_conv_kernel(y_ref, g_ref, k_ref, m1_ref, f2f_ref, f2i_ref, m3_ref, b_ref, o_ref, are, aim, *, out_pitch):
    half = HY_N1 // 2
    out_seq = half * out_pitch
    m1 = m1_ref[...]

    def stage1(i, carry):
        for u in range(UNROLL):
            n2 = i * UNROLL + u
            x = jnp.concatenate([y_ref[pl.ds(n2, half, stride=PITCH), :],
                                 y_ref[pl.ds(SEQ_PAD + n2, half, stride=PITCH), :]], axis=0).astype(BF16)
            r = _dot(m1, x)
            are[pl.ds(n2, HY_N1, stride=PITCH), :] = r[:HY_N1]
            aim[pl.ds(n2, HY_N1, stride=PITCH), :] = r[HY_N1:]
        return carry

    lax.fori_loop(0, HY_N2 // UNROLL, stage1, 0)
    cw = o_ref.shape[-1]

    def stage2(i, carry):
        for u in range(UNROLL):
            k1 = i * UNROLL + u
            r0 = pl.multiple_of(k1 * PITCH, 8)
            y = _dot(f2f_ref[k1], _cstack(are[pl.ds(r0, HY_N2), :], aim[pl.ds(r0, HY_N2), :]).astype(BF16))
            yr, yi = y[:, :cw], y[:, cw:]
            kr, ki = k_ref[0, k1], k_ref[1, k1]
            w = _dot(f2i_ref[k1], _cstack(yr * kr - yi * ki, yr * ki + yi * kr).astype(BF16))
            are[pl.ds(r0, HY_N2), :] = w[:, :cw]
            aim[pl.ds(r0, HY_N2), :] = w[:, cw:]
        return carry

    lax.fori_loop(0, HY_N1 // UNROLL, stage2, 0)
    m3 = m3_ref[...]
    bias = b_ref[...]

    def stage3(i, carry):
        for u in range(UNROLL):
            n2 = i * UNROLL + u
            bn = jnp.concatenate([are[pl.ds(n2, HY_N1, stride=PITCH), :], aim[pl.ds(n2, HY_N1, stride=PITCH), :]],
                                 axis=0).astype(BF16)
            y = _dot(m3, bn)
            for b in range(2):
                rows = pl.ds(b * SEQ_PAD + n2, half, stride=PITCH)
                o_ref[pl.ds(b * out_seq + n2, half, stride=out_pitch), :] = (
                    g_ref[rows, :] * (y[b * half:(b + 1) * half] + y_ref[rows, :] * bias))
        return carry

    lax.fori_loop(0, HY_N2 // UNROLL, stage3, 0)


def _hyena_mix(z3, filt, hyena_bias, mats, with_ctx):
    c = HYENA_WIDTH
    cw = 128
    nch = c // cw
    pairs = BATCH // 2
    full = lambda a: pl.BlockSpec(a.shape, lambda *i: (0,) * a.ndim)
    kspec = _filter_spectrum(SEQ, filt, mats)
    if with_ctx:
        kspec_c = _filter_spectrum(CTX_LEN, filt, mats)
    bias3 = hyena_bias.reshape(HYENA_ORDER, 1, c)
    scr = pltpu.VMEM((HY_N1 * PITCH, cw), F32)
    y = None
    for o in range(HYENA_ORDER):
        final = o == HYENA_ORDER - 1
        out_pitch = HY_N2 if final else PITCH
        out_seq = SEQ if final else SEQ_PAD
        if final:
            out_rows = T_ALL if with_ctx else T_LAT
        else:
            out_rows = T_PAD if with_ctx else BATCH * SEQ_PAD
        if y is None:
            xin, xspec = z3, pl.BlockSpec((None, 2 * SEQ_PAD, cw), lambda b, ch: (0, b, ch))
        else:
            xin, xspec = y, pl.BlockSpec((2 * SEQ_PAD, cw), lambda b, ch: (b, ch))
        ynew = pl.pallas_call(
            functools.partial(_hyena_conv_kernel, out_pitch=out_pitch),
            grid=(pairs, nch),
            in_specs=[xspec,
                      pl.BlockSpec((None, 2 * SEQ_PAD, cw), lambda b, ch, o=o: (o + 1, b, ch)),
                      pl.BlockSpec((None, 2, HY_N1, HY_N2, cw), lambda b, ch, o=o: (o, 0, 0, 0, ch)),
                      full(mats['hy_lead_f']), full(mats['hy_slab_f2']), full(mats['hy_slab_i2']),
                      full(mats['hy_lead_i']),
                      pl.BlockSpec((None, 1, cw), lambda b, ch, o=o: (o, 0, ch))],
            out_specs=pl.BlockSpec((2 * out_seq, cw), lambda b, ch: (b, ch)),
            out_shape=jax.ShapeDtypeStruct((out_rows, c), F32),
            scratch_shapes=[scr, scr],
            compiler_params=_cparams(("parallel", "parallel"), 58 * 1024 * 1024),
            name="hyena_conv",
        )(xin, z3, kspec, mats['hy_lead_f'], mats['hy_slab_f2'], mats['hy_slab_i2'], mats['hy_lead_i'], bias3)
        if with_ctx:
            nc = 2 * CTX_LEN
            blk0 = BATCH * SEQ_PAD // nc
            oblk0 = BATCH * out_seq // nc
            if y is None:
                cin, cspec_in = z3, pl.BlockSpec((1, nc, c), lambda b: (0, blk0 + b, 0))
            else:
                cin, cspec_in = y, pl.BlockSpec((nc, c), lambda b: (blk0 + b, 0))
            xc = _lead(mats['hc_f'], [cin], [cspec_in], (pairs,),
                       jax.ShapeDtypeStruct((pairs, 2 * nc, c), F32),
                       pl.BlockSpec((1, 2 * nc, c), lambda b: (b, 0, 0)), name="hyena_ctx_fwd")
            ynew = _lead(mats['hc_i'], [xc], [pl.BlockSpec((1, 2 * nc, c), lambda b: (b, 0, 0))],
                         (pairs,), jax.ShapeDtypeStruct((out_rows, c), F32),
                         pl.BlockSpec((nc, c), lambda b, oblk0=oblk0: (oblk0 + b, 0)),
                         kspec=(kspec_c, pl.BlockSpec((1, 2 * nc, c), lambda b, o=o: (o, 0, 0))),
                         epi=[(z3, pl.BlockSpec((1, nc, c), lambda b, o=o: (o + 1, blk0 + b, 0))),
                              (cin, cspec_in),
                              (bias3, pl.BlockSpec((1, 1, c), lambda b, o=o: (o, 0, 0)))],
                         alias_to=ynew, name="hyena_ctx_inv")
        y = ynew
    return y


def _route_tile(lt, br, base, tri):
    tm = lt.shape[1]
    aff = jax.nn.sigmoid(lt)
    biased = aff + br
    b = [biased[e:e + 1, :] for e in range(N_EXPERTS)]
    a = [aff[e:e + 1, :] for e in range(N_EXPERTS)]
    epg = EXPERTS_PER_GROUP
    scores = []
    for g in range(N_GROUPS):
        x0, x1, x2, x3 = b[epg * g:epg * g + epg]
        s1, t1 = jnp.maximum(x0, x1), jnp.minimum(x0, x1)
        s2, t2 = jnp.maximum(x2, x3), jnp.minimum(x2, x3)
        scores.append(jnp.maximum(s1, s2) + jnp.maximum(jnp.minimum(s1, s2), jnp.maximum(t1, t2)))
    best = scores[0]
    gsel = jnp.zeros((1, tm), jnp.int32)
    for g in range(1, N_GROUPS):
        gsel = jnp.where(scores[g] > best, g, gsel)
        best = jnp.maximum(best, scores[g])

    def pick(rows, j):
        out = rows[j]
        for g in range(1, N_GROUPS):
            out = jnp.where(gsel == g, rows[epg * g + j], out)
        return out

    v = [pick(b, j) for j in range(epg)]
    av = [pick(a, j) for j in range(epg)]
    i1 = jnp.zeros((1, tm), jnp.int32)
    m1 = v[0]
    for j in range(1, epg):
        i1 = jnp.where(v[j] > m1, j, i1)
        m1 = jnp.maximum(m1, v[j])
    neg = jnp.float32(-3.0e38)
    i2 = jnp.zeros((1, tm), jnp.int32)
    m2 = jnp.full((1, tm), neg, F32)
    for j in range(epg):
        cand = jnp.where(i1 == j, neg, v[j])
        take = cand > m2
        i2 = jnp.where(take, j, i2)
        m2 = jnp.where(take, cand, m2)

    def sel(rows, idx):
        out = rows[0]
        for j in range(1, epg):
            out = jnp.where(idx == j, rows[j], out)
        return out

    a1, a2 = sel(av, i1), sel(av, i2)
    den = a1 + a2
    e1 = gsel * epg + i1
    e2 = gsel * epg + i2
    eio = lax.broadcasted_iota(jnp.int32, (N_EXPERTS, tm), 0)
    oh1 = jnp.where(eio == e1, 1.0, 0.0)
    oh2 = jnp.where(eio == e2, 1.0, 0.0)
    oh = oh1 + oh2
    tot = base + _dot(oh.astype(BF16), tri)
    r1 = jnp.sum(oh1 * tot, axis=0, keepdims=True)
    r2 = jnp.sum(oh2 * tot, axis=0, keepdims=True)
    new_base = base + jnp.sum(oh, axis=1, keepdims=True)
    return (e1, e2), (a1 / den, a2 / den), (r1.astype(jnp.int32), r2.astype(jnp.int32)), new_base


def _merge_kernel(x_ref, ya_ref, yf_ref, yh_ref, gt_ref, wa_ref, wf_ref, wh_ref, wo_ref, g1_ref,
                  gn_ref, sh_ref, sc_ref, wrt_ref, br_ref, xo_ref, h2_ref, e_ref, w_ref, r_ref, cnt_ref):
    d = D_MODEL
    gate = lambda k: jax.nn.sigmoid(gt_ref[:, k * d:(k + 1) * d].astype(F32))
    merged = gate(0) * _dot(ya_ref[...], wa_ref[...])
    merged += gate(1) * _dot(yf_ref[...].astype(BF16), wf_ref[...])
    merged += gate(2) * _dot(yh_ref[...].astype(BF16), wh_ref[...])
    xn = x_ref[...] + g1_ref[0, 0] * _dot(merged.astype(BF16), wo_ref[...])
    xo_ref[...] = xn
    h2f = _rms_mod(xn, gn_ref[...], sh_ref[0, 0], sc_ref[0, 0])
    _rows_to_tiles(h2_ref, h2f, 0, h2f.shape[0])
    h2 = h2f.astype(BF16)

    @pl.when(pl.program_id(0) == 0)
    def _():
        cnt_ref[...] = jnp.zeros_like(cnt_ref)

    tm = h2.shape[0]
    lt = lax.dot_general(wrt_ref[...], h2, (((1,), (1,)), ((), ())), preferred_element_type=F32)
    tri = jnp.where(lax.broadcasted_iota(jnp.int32, (tm, tm), 0) < lax.broadcasted_iota(jnp.int32, (tm, tm), 1),
                    1.0, 0.0).astype(BF16)
    es, ws, rs, new_base = _route_tile(lt, br_ref[...], cnt_ref[:, 0:1], tri)
    e_ref[0:1, :], e_ref[1:2, :] = es
    w_ref[0:1, :], w_ref[1:2, :] = ws
    r_ref[0:1, :], r_ref[1:2, :] = rs
    cnt_ref[...] = jnp.broadcast_to(new_base, cnt_ref.shape)


def _merge(x, ya, yf, yh, p, wa, wf, wh, wo, mod4, gain2, wrt, br, n_tok):
    tm = TM
    row = _mod_row(tm)
    full = lambda a: pl.BlockSpec(a.shape, lambda i: (0,) * a.ndim)
    modspec = lambda k: pl.BlockSpec((1, 1, 1, D_MODEL), lambda i: (row(i), k, 0, 0))
    tok = lambda w: pl.BlockSpec((tm, w), lambda i: (i, 0))
    lane = pl.BlockSpec((TOP_K, tm), lambda i: (0, i))
    return pl.pallas_call(
        _merge_kernel,
        grid=(n_tok // tm,),
        in_specs=[tok(D_MODEL), tok(ATTN_WIDTH), tok(FNET_WIDTH), tok(HYENA_WIDTH),
                  pl.BlockSpec((pl.Element(tm), pl.Element(3 * D_MODEL)), lambda i: (i * tm, PG_OFF)),
                  full(wa), full(wf), full(wh), full(wo), modspec(2), full(gain2), modspec(3), modspec(4),
                  full(wrt), full(br)],
        out_specs=[tok(D_MODEL), pl.BlockSpec((tm * ROW_TILE, 128), lambda i: (i, 0)), lane, lane, lane,
                   pl.BlockSpec((N_EXPERTS, 128), lambda i: (0, 0))],
        out_shape=[jax.ShapeDtypeStruct((n_tok, D_MODEL), F32),
                   jax.ShapeDtypeStruct((n_tok * ROW_TILE, 128), F32),
                   jax.ShapeDtypeStruct((TOP_K, n_tok), jnp.int32),
                   jax.ShapeDtypeStruct((TOP_K, n_tok), F32),
                   jax.ShapeDtypeStruct((TOP_K, n_tok), jnp.int32),
                   jax.ShapeDtypeStruct((N_EXPERTS, 128), F32)],
        compiler_params=_cparams(("arbitrary",)),
        name="merge_out_norm_route",
    )(x, ya, yf, yh, p, wa, wf, wh, wo, mod4, gain2, mod4, mod4, wrt, br)


ROW_TILE = D_MODEL // 128


def _rows_from_tiles(ref, n):
    return jnp.concatenate([ref[pl.ds(s, n, stride=ROW_TILE), :] for s in range(ROW_TILE)], axis=1)


def _rows_to_tiles(ref, val, col0, n):
    for j in range(val.shape[1] // 128):
        ref[pl.ds(col0 // 128 + j, n, stride=ROW_TILE), :] = val[:, j * 128:(j + 1) * 128]


def _moe_kernel(be_ref, na_ref, ip_ref, ic_ref, sc_ref, sn_ref, h2_ref, wg_ref, wu_ref, wd_ref, y_ref,
                xbuf, obuf, gsem, ssem, wg_s, wu_s, wd_s):
    i = pl.program_id(0)
    n_act = na_ref[0]
    sub = ROW_TILE
    bm = xbuf.shape[1] // sub
    active = i < n_act
    slot = i % 2
    other = (i + 1) % 2
    n_chunk = 4
    cw = EXPERT_FF // n_chunk
    rows_per = bm // n_chunk

    def tile_rows(ref, row):
        return ref.at[pl.ds(pl.multiple_of(row * sub, sub), sub), :]

    def gather(src_ref, dst_slot, lo, hi):
        for r in range(lo, hi):
            pltpu.make_async_copy(tile_rows(h2_ref, src_ref[0, 0, r]), xbuf.at[dst_slot, pl.ds(r * sub, sub), :],
                                  gsem.at[dst_slot]).start(priority=r % 2)

    def scatter(info_ref, src_slot, lo, hi):
        for r in range(lo, hi):
            pltpu.make_async_copy(obuf.at[src_slot, pl.ds(r * sub, sub), :], tile_rows(y_ref, info_ref[0, 0, r]),
                                  ssem.at[src_slot]).start(priority=r % 2)

    def wait_gather(s):
        pltpu.make_async_copy(h2_ref.at[pl.ds(0, bm * sub), :], xbuf.at[s], gsem.at[s]).wait()

    def wait_scatter(s):
        pltpu.make_async_copy(obuf.at[s], y_ref.at[pl.ds(0, bm * sub), :], ssem.at[s]).wait()

    @pl.when((i == 0) & active)
    def _():
        obuf[...] = jnp.zeros_like(obuf)
        gather(sc_ref, 0, 0, bm)

    prev = be_ref[jnp.maximum(i - 1, 0)]

    @pl.when(active & ((i == 0) | (be_ref[i] != prev)))
    def _():
        wg_s[...] = wg_ref[...].astype(BF16)
        wu_s[...] = wu_ref[...].astype(BF16)
        wd_s[...] = wd_ref[...].astype(BF16)

    @pl.when(active)
    def _():
        wait_gather(slot)
        x = _rows_from_tiles(xbuf.at[slot], bm).astype(BF16)
        hs = []
        for c in range(n_chunk):
            g = _dot(x, wg_s[:, c * cw:(c + 1) * cw])
            u = _dot(x, wu_s[:, c * cw:(c + 1) * cw])
            hs.append(((g * jax.nn.sigmoid(g)) * u).astype(BF16))
            gather(sn_ref, other, c * rows_per, (c + 1) * rows_per)
        h = jnp.concatenate(hs, axis=1)

        @pl.when(i > 0)
        def _():
            wait_scatter(slot)

        for c in range(n_chunk):
            _rows_to_tiles(obuf.at[slot], _dot(h, wd_s[:, c * cw:(c + 1) * cw]), c * cw, bm)
            scatter(ip_ref, other, c * rows_per, (c + 1) * rows_per)

    @pl.when(i == n_act - 1)
    def _():
        wait_gather(other)
        wait_scatter(other)
        scatter(ic_ref, slot, 0, bm)
        wait_scatter(slot)


def _moe(h2, info, blk_expert, n_active, wg, wu, wd, layer, n_tok):
    bm = EXPERT_BM
    n_rows = info.shape[0]
    n_blk = n_rows // bm
    info3 = info.reshape(n_blk, 1, bm)
    src3 = jnp.where(info3 >= TOP_K * n_tok, 0, info3 % n_tok)
    wspec = lambda k, n: pl.BlockSpec((None, None, k, n), lambda i, be, na: (layer, be[i], 0, 0))
    ispec = lambda f: pl.BlockSpec((1, 1, bm), lambda i, be, na: (f(i), 0, 0), memory_space=pltpu.SMEM)
    return pl.pallas_call(
        _moe_kernel,
        grid_spec=pltpu.PrefetchScalarGridSpec(
            num_scalar_prefetch=2,
            grid=(n_blk,),
            in_specs=[ispec(lambda i: jnp.maximum(i - 1, 0)), ispec(lambda i: i),
                      ispec(lambda i: i), ispec(lambda i: jnp.minimum(i + 1, n_blk - 1)),
                      pl.BlockSpec(memory_space=pl.ANY),
                      wspec(D_MODEL, EXPERT_FF), wspec(D_MODEL, EXPERT_FF), wspec(EXPERT_FF, D_MODEL)],
            out_specs=pl.BlockSpec(memory_space=pl.ANY),
            scratch_shapes=[pltpu.VMEM((2, bm * ROW_TILE, 128), F32), pltpu.VMEM((2, bm * ROW_TILE, 128), F32),
                            pltpu.SemaphoreType.DMA((2,)), pltpu.SemaphoreType.DMA((2,)),
                            pltpu.VMEM((D_MODEL, EXPERT_FF), BF16), pltpu.VMEM((D_MODEL, EXPERT_FF), BF16),
                            pltpu.VMEM((EXPERT_FF, D_MODEL), BF16)]),
        out_shape=jax.ShapeDtypeStruct(((TOP_K * n_tok + n_rows) * ROW_TILE, 128), F32),
        compiler_params=_cparams(("arbitrary",)),
        name="moe_experts",
    )(blk_expert, n_active, info3, info3, src3, src3, h2, wg, wu, wd)


def _dispatch_info(e_idx, rank, counts, n_tok):
    bm = EXPERT_BM
    counts = counts.astype(jnp.int32)
    padded = (counts + bm - 1) // bm * bm
    pad_end = jnp.cumsum(padded)
    pad_start = pad_end - padded
    experts = jnp.arange(N_EXPERTS, dtype=jnp.int32)
    start = jnp.sum(jnp.where(e_idx[..., None] == experts, pad_start, 0), axis=-1)
    dest = start + rank
    n_rows = -(-(n_tok * TOP_K) // bm) * bm + N_EXPERTS * bm
    n_blk = n_rows // bm
    spill = TOP_K * n_tok + jnp.arange(n_rows, dtype=jnp.int32)
    info = spill.at[dest.reshape(-1)].set(jnp.arange(TOP_K * n_tok, dtype=jnp.int32))
    blk_start = jnp.arange(n_blk, dtype=jnp.int32) * bm
    blk_expert = jnp.minimum(jnp.sum((blk_start[:, None] >= pad_end[None, :]).astype(jnp.int32), axis=-1),
                             N_EXPERTS - 1)
    n_active = (pad_end[-1] // bm).astype(jnp.int32).reshape(1)
    return info, blk_expert, n_active


def _residual_kernel(x_ref, y0_ref, y1_ref, w_ref, g2_ref, gf_ref, o_ref, *, final):
    w = w_ref[...]
    tm = x_ref.shape[0]
    moe = _rows_from_tiles(y0_ref, tm) * w[:, 0:1] + _rows_from_tiles(y1_ref, tm) * w[:, 1:2]
    xn = x_ref[...] + g2_ref[0, 0] * moe
    if final:
        y = xn * lax.rsqrt(jnp.mean(xn * xn, axis=-1, keepdims=True) + EPS)
        xn = y * gf_ref[...]
    o_ref[...] = xn


def _residual(x, y, w_sel, mod4, gain_final, n_tok, final):
    tm = TM
    n_tiles = n_tok // tm
    row = _mod_row(tm)
    tok = lambda w: pl.BlockSpec((tm, w), lambda i: (i, 0))
    return pl.pallas_call(
        functools.partial(_residual_kernel, final=final),
        grid=(n_tiles,),
        in_specs=[tok(D_MODEL), pl.BlockSpec((tm * ROW_TILE, 128), lambda i: (i, 0)),
                  pl.BlockSpec((tm * ROW_TILE, 128), lambda i: (n_tiles + i, 0)), tok(TOP_K),
                  pl.BlockSpec((1, 1, 1, D_MODEL), lambda i: (row(i), 5, 0, 0)),
                  pl.BlockSpec((1, D_MODEL), lambda i: (0, 0))],
        out_specs=tok(D_MODEL),
        out_shape=jax.ShapeDtypeStruct((n_tok if final else T_ALL, D_MODEL), F32),
        compiler_params=_cparams(("parallel",)),
        name="moe_residual",
    )(x, y, y, w_sel, mod4, gain_final)


def kernel(x, c, ctx, c_ctx, w_mod, b_mod, norm_mix, norm_ffn, w_in, attn_sink, conv_w, conv_b, filt_w1, filt_b1, filt_freq, filt_w2, filt_b2, filt_w3, hyena_bias, w_branch_attn, w_branch_fnet, w_branch_hyena, w_out, w_router, b_router, w_exp_gate, w_exp_up, w_exp_down, norm_final):
    mats = _dft_mats()
    cos_t, sin_t = _rope_tables()
    c8 = jnp.concatenate([c, c_ctx[None, :], jnp.zeros((8 - BATCH - 1, D_MODEL), F32)], axis=0)
    mod_all = _modulation(c8, w_mod, b_mod)
    xa = jnp.concatenate([x.reshape(T_LAT, D_MODEL), ctx.reshape(T_CTX, D_MODEL)], axis=0)
    wrt = w_router.T.astype(BF16)
    br = b_router.astype(F32).reshape(N_EXPERTS, 1)
    gain_final = norm_final.reshape(1, D_MODEL)
    out = None
    for l in range(DEPTH):
        last = l == DEPTH - 1
        with_ctx = not last
        n_tok = T_LAT if last else T_ALL
        mod4 = mod_all[l].reshape(8, N_MOD, 1, D_MODEL)
        p = _norm_proj(xa, norm_mix[l].reshape(1, D_MODEL), mod4, w_in, l, T_ALL)
        ya = _attention(p, attn_sink[l], cos_t, sin_t, with_ctx)
        yf = _fourier_mix(p, mats, with_ctx)
        z3 = _short_conv(p, conv_w[l], conv_b[l], with_ctx)
        filt = (filt_w1[l], filt_b1[l], filt_freq[l], filt_w2[l], filt_b2[l], filt_w3[l])
        yh = _hyena_mix(z3, filt, hyena_bias[l], mats, with_ctx)
        xa, h2, e_idx, w_sel, rank, cnt = _merge(
            xa, ya, yf, yh, p, w_branch_attn[l].astype(BF16), w_branch_fnet[l].astype(BF16),
            w_branch_hyena[l].astype(BF16), w_out[l].astype(BF16), mod4,
            norm_ffn[l].reshape(1, D_MODEL), wrt, br, n_tok)
        info, blk_expert, n_active = _dispatch_info(e_idx, rank, cnt[:, 0], n_tok)
        y = _moe(h2, info, blk_expert, n_active, w_exp_gate, w_exp_up, w_exp_down, l, n_tok)
        res = _residual(xa, y, w_sel.T, mod4, gain_final, n_tok, last)
        if last:
            out = res
        else:
            xa = res
    return out.reshape(BATCH, SEQ, D_MODEL)
```

```python
import functools
import math

import jax
import jax.numpy as jnp
from jax import lax
from jax.experimental import pallas as pl
from jax.experimental.pallas import tpu as pltpu

F32 = jnp.float32
BF16 = jnp.bfloat16

D_MODEL = 1024
BATCH = 4
SEQ = 4096
DEPTH = 4
GRID_W = 64
CTX_LEN = 256
EPS = 1e-6
N_MOD = 6

HEAD_DIM = 64
N_Q_HEADS = 8
N_KV_HEADS = 2
Q_PER_KV = N_Q_HEADS // N_KV_HEADS
ATTN_BLOCK = 128
ROPE_BASE = 10000.0

FNET_GROUPS = 4
FNET_GROUP_DIM = 128
FNET_WIDTH = FNET_GROUPS * FNET_GROUP_DIM

HYENA_WIDTH = 512
HYENA_ORDER = 2
FILTER_EMB = 33
FILTER_BANDS = (FILTER_EMB - 1) // 2
FILTER_HIDDEN = 64
DECAY_TARGET = 1e-2
FAST_DECAY_PCT = 0.3
SLOW_DECAY_PCT = 1.5

ATTN_WIDTH = N_Q_HEADS * HEAD_DIM
KV_WIDTH = N_KV_HEADS * HEAD_DIM
Q_OFF = 0
K_OFF = Q_OFF + ATTN_WIDTH
V_OFF = K_OFF + KV_WIDTH
F_OFF = V_OFF + KV_WIDTH
H_OFF = F_OFF + FNET_WIDTH
G_OFF = H_OFF + (HYENA_ORDER + 1) * HYENA_WIDTH
IN_WIDTH = G_OFF + 3 * D_MODEL

N_EXPERTS = 16
N_GROUPS = 4
EXPERTS_PER_GROUP = N_EXPERTS // N_GROUPS
TOP_K = 2
EXPERT_FF = 1024

T_LAT = BATCH * SEQ
T_CTX = BATCH * CTX_LEN
T_ALL = T_LAT + T_CTX

PG_OFF = G_OFF
PH_OFF = H_OFF
PF_OFF = F_OFF
PQ_OFF = Q_OFF
PK_OFF = K_OFF
PV_OFF = V_OFF

HY_N = 2 * SEQ
HY_N2 = 64
HY_N1 = HY_N // HY_N2
FN_N = 64

PITCH = HY_N2 + 8
SEQ_PAD = SEQ // HY_N2 * PITCH
T_PAD = BATCH * SEQ_PAD + T_CTX
UNROLL = 16
TM = 512
EXPERT_BM = 256
VMEM_LIMIT = 52 * 1024 * 1024
VMEM_LIMIT_CONV = 58 * 1024 * 1024


def _cparams(sem, vmem=VMEM_LIMIT):
    return pltpu.CompilerParams(dimension_semantics=sem, vmem_limit_bytes=vmem)


def _dot(a, b):
    return jnp.dot(a, b, preferred_element_type=F32)


def _cis(expo, n):
    ang = (2.0 * math.pi / n) * jnp.mod(expo, n).astype(F32)
    return jnp.cos(ang), jnp.sin(ang)


def _real_form(gr, gi):
    return jnp.concatenate([jnp.concatenate([gr, -gi], axis=-1), jnp.concatenate([gi, gr], axis=-1)], axis=-2)


def _dft_mats():
    ar = lambda n: jnp.arange(n, dtype=jnp.int32)
    m = {}
    c, s = _cis(ar(HY_N1)[:, None] * ar(HY_N1 // 2)[None, :], HY_N1)
    m['hy_lead_f'] = _real_form(c, -s).astype(BF16)
    c, s = _cis(ar(HY_N1 // 2)[:, None] * ar(HY_N1)[None, :], HY_N1)
    m['hy_lead_i'] = _real_form(c, s).astype(BF16)
    c, s = _cis(ar(HY_N1)[:, None] * ar(HY_N1 // 2)[None, :], HY_N1)
    m['hy_lead_kh'] = jnp.concatenate([c, -s], axis=0).astype(BF16)
    a = ar(HY_N1)[:, None, None]
    k2 = ar(HY_N2)[None, :, None]
    n2 = ar(HY_N2)[None, None, :]
    c, s = _cis(n2 * (a + HY_N1 * k2), HY_N)
    m['hy_slab_f2'] = jnp.concatenate([c, -s], axis=-1).astype(BF16)
    ct = jnp.swapaxes(c, 1, 2) * (1.0 / HY_N)
    st = jnp.swapaxes(s, 1, 2) * (1.0 / HY_N)
    m['hy_slab_i2'] = jnp.concatenate([ct, st], axis=-1).astype(BF16)
    nc = 2 * CTX_LEN
    c, s = _cis(ar(nc)[:, None] * ar(CTX_LEN)[None, :], nc)
    m['hc_f'] = _real_form(c, -s).astype(BF16)
    c, s = _cis(ar(CTX_LEN)[:, None] * ar(nc)[None, :], nc)
    m['hc_i'] = _real_form(c * (1.0 / nc), s * (1.0 / nc)).astype(BF16)
    c, s = _cis(ar(nc)[:, None] * ar(CTX_LEN)[None, :], nc)
    m['hc_kh'] = jnp.concatenate([c, -s], axis=0).astype(BF16)
    c, s = _cis(ar(FNET_GROUP_DIM)[:, None] * ar(FNET_GROUP_DIM)[None, :], FNET_GROUP_DIM)
    m['fn_chan'] = jnp.concatenate([c, -s], axis=1).astype(BF16)
    c, s = _cis(ar(FN_N)[:, None] * ar(FN_N)[None, :], FN_N)
    m['fn_lead'] = _real_form(c, -s).astype(BF16)
    a = ar(FN_N)[:, None, None]
    k1 = ar(FN_N)[None, :, None]
    n1 = ar(FN_N)[None, None, :]
    scale = 1.0 / math.sqrt(SEQ * FNET_GROUP_DIM)
    c, s = _cis(n1 * (a + FN_N * k1), SEQ)
    m['fn_slab'] = jnp.concatenate([c * scale, s * scale], axis=-1).astype(BF16)
    scale = 1.0 / math.sqrt(CTX_LEN * FNET_GROUP_DIM)
    c, s = _cis(ar(CTX_LEN)[:, None] * ar(CTX_LEN)[None, :], CTX_LEN)
    m['fc'] = jnp.concatenate([c * scale, s * scale], axis=-1).astype(BF16)
    return m


def _mod_kernel(c_ref, w_ref, b_ref, o_ref):
    c = c_ref[...]
    s = c * jax.nn.sigmoid(c)
    o_ref[0] = _dot(s.astype(BF16), w_ref[0].astype(BF16)) + b_ref[0]


def _modulation(c8, w_mod, b_mod):
    tn = 1536
    n = N_MOD * D_MODEL
    return pl.pallas_call(
        _mod_kernel,
        grid=(DEPTH, n // tn),
        in_specs=[pl.BlockSpec((8, D_MODEL), lambda l, j: (0, 0)),
                  pl.BlockSpec((1, D_MODEL, tn), lambda l, j: (l, 0, j)),
                  pl.BlockSpec((1, 1, tn), lambda l, j: (l, 0, j))],
        out_specs=pl.BlockSpec((1, 8, tn), lambda l, j: (l, 0, j)),
        out_shape=jax.ShapeDtypeStruct((DEPTH, 8, n), F32),
        compiler_params=_cparams(("parallel", "parallel")),
        name="adaln_modulation",
    )(c8, w_mod, b_mod.reshape(DEPTH, 1, n))


def _mod_row(tm):
    tiles_per_batch = SEQ // tm
    return lambda i: jnp.minimum(i // tiles_per_batch, BATCH)


def _rms_mod(x, g, sh, sc):
    y = x * lax.rsqrt(jnp.mean(x * x, axis=-1, keepdims=True) + EPS)
    return (y * g) * (1.0 + sc) + sh


def _norm_proj_kernel(x_ref, g_ref, sh_ref, sc_ref, w_ref, o_ref, w_s):
    @pl.when(pl.program_id(1) == 0)
    def _():
        w_s[...] = w_ref[...].astype(BF16)

    h = _rms_mod(x_ref[...], g_ref[...], sh_ref[0, 0], sc_ref[0, 0]).astype(BF16)
    o_ref[...] = _dot(h, w_s[...]).astype(o_ref.dtype)


def _norm_proj(x, gain, mod4, w, layer, n_tok):
    tm = TM
    n_out = w.shape[2]
    tn = n_out // 2
    row = _mod_row(tm)
    return pl.pallas_call(
        _norm_proj_kernel,
        grid=(n_out // tn, n_tok // tm),
        in_specs=[pl.BlockSpec((tm, D_MODEL), lambda j, i: (i, 0)),
                  pl.BlockSpec((1, D_MODEL), lambda j, i: (0, 0)),
                  pl.BlockSpec((1, 1, 1, D_MODEL), lambda j, i: (row(i), 0, 0, 0)),
                  pl.BlockSpec((1, 1, 1, D_MODEL), lambda j, i: (row(i), 1, 0, 0)),
                  pl.BlockSpec((None, D_MODEL, tn), lambda j, i: (layer, 0, j))],
        out_specs=pl.BlockSpec((tm, tn), lambda j, i: (i, j)),
        out_shape=jax.ShapeDtypeStruct((T_ALL, n_out), BF16),
        scratch_shapes=[pltpu.VMEM((D_MODEL, tn), BF16)],
        compiler_params=_cparams(("arbitrary", "arbitrary")),
        name="norm_in_proj",
    )(x, gain, mod4, mod4, w)


def _softmax_pv(qh, k_parts, v_parts, masks, sink):
    nt = (((1,), (1,)), ((), ()))
    scores = []
    for kp, mk in zip(k_parts, masks):
        s = lax.dot_general(qh, kp, nt, preferred_element_type=F32)
        if mk is not None:
            s = jnp.where(mk, s, -1e30)
        scores.append(s)
    m = sink
    for s in scores:
        m = jnp.maximum(m, jnp.max(s, axis=-1, keepdims=True))
    es = [jnp.exp(s - m) for s in scores]
    den = jnp.exp(sink - m)
    for e in es:
        den = den + jnp.sum(e, axis=-1, keepdims=True)
    inv = 1.0 / den
    o = None
    for e, vp in zip(es, v_parts):
        t = _dot((e * inv).astype(BF16), vp)
        o = t if o is None else o + t
    return o


def _attn_kernel(sink_ref, q_ref, km_ref, k0_ref, kp_ref, vm_ref, v0_ref, vp_ref, kc_ref, vc_ref,
                 cos_ref, sin_ref, psw_ref, p64_ref, o_ref, *, nb):
    n = pl.program_id(1)
    blk = ATTN_BLOCK
    psw = psw_ref[...]
    p64 = p64_ref[...]

    def rope(xb, blk_idx):
        r0 = pl.multiple_of(blk_idx * blk, blk)
        return xb.astype(F32) * cos_ref[pl.ds(r0, blk), :] + _dot(xb, psw) * sin_ref[pl.ds(r0, blk), :]

    nm = jnp.maximum(n - 1, 0)
    npl = jnp.minimum(n + 1, nb - 1)
    kall = jnp.concatenate([rope(km_ref[...], nm), rope(k0_ref[...], n), rope(kp_ref[...], npl),
                            kc_ref[...].astype(F32)], axis=0)
    vall = jnp.concatenate([vm_ref[...], v0_ref[...], vp_ref[...], vc_ref[...]], axis=0)
    nk = kall.shape[0]
    lo = lax.broadcasted_iota(jnp.int32, (nk, 128), 1) < HEAD_DIM

    r = lax.broadcasted_iota(jnp.int32, (blk, blk), 0)
    cidx = lax.broadcasted_iota(jnp.int32, (blk, blk), 1)
    ok_prev = jnp.where(cidx >= r, (n > 0).astype(jnp.int32), 0) > 0
    ok_next = jnp.where(cidx <= r, (n < nb - 1).astype(jnp.int32), 0) > 0
    neg = jnp.float32(-1e30)

    scale = HEAD_DIM ** -0.5
    q2 = [(rope(q_ref[:, p * 128:(p + 1) * 128], n) * scale).astype(BF16) for p in range(N_Q_HEADS // 2)]
    nt = (((1,), (1,)), ((), ()))
    n_pair = Q_PER_KV // 2
    for h in range(N_KV_HEADS):
        kh = (jnp.where(lo, kall, 0.0) if h == 0 else jnp.where(lo, 0.0, kall)).astype(BF16)
        vh = jnp.where(lo, vall, jnp.zeros_like(vall)) if h == 0 else jnp.where(lo, jnp.zeros_like(vall), vall)
        q4 = jnp.concatenate(q2[n_pair * h:n_pair * (h + 1)], axis=0)
        q4s = _dot(q4, p64).astype(BF16)
        first, second = (q4, q4s) if h == 0 else (q4s, q4)
        s = lax.dot_general(jnp.concatenate([first, second], axis=0), kh, nt, preferred_element_type=F32)
        probs = []
        for rb in range(2 * n_pair):
            sink = sink_ref[Q_PER_KV * h + 2 * (rb % n_pair) + rb // n_pair]
            sb = s[rb * blk:(rb + 1) * blk]
            parts = [jnp.where(ok_prev, sb[:, 0:blk], neg), sb[:, blk:2 * blk],
                     jnp.where(ok_next, sb[:, 2 * blk:3 * blk], neg), sb[:, 3 * blk:nk]]
            tiles = parts[:3] + [parts[3][:, t * blk:(t + 1) * blk] for t in range((nk - 3 * blk) // blk)]
            m = jnp.maximum(sink, jnp.max(functools.reduce(jnp.maximum, tiles), axis=-1, keepdims=True))
            es = [jnp.exp(t - m) for t in tiles]
            den = jnp.exp(sink - m) + jnp.sum(functools.reduce(jnp.add, es), axis=-1, keepdims=True)
            inv = 1.0 / den
            probs.append(jnp.concatenate([(e * inv).astype(BF16) for e in es], axis=1))
        o = _dot(jnp.concatenate(probs, axis=0), vh)
        half = n_pair * blk
        if h == 0:
            out = o[:half] + pltpu.roll(o[half:], HEAD_DIM, 1)
        else:
            out = pltpu.roll(o[:half], HEAD_DIM, 1) + o[half:]
        w0 = h * Q_PER_KV * HEAD_DIM
        for j in range(n_pair):
            o_ref[:, w0 + 128 * j:w0 + 128 * (j + 1)] = out[j * blk:(j + 1) * blk].astype(o_ref.dtype)


def _ctx_attn_kernel(sink_ref, q_ref, kc_ref, vc_ref, o_ref):
    kc = kc_ref[...].astype(BF16)
    vc = vc_ref[...].astype(BF16)
    scale = HEAD_DIM ** -0.5
    outs = []
    for pair in range(N_Q_HEADS // 2):
        q2 = (q_ref[:, pair * 128:(pair + 1) * 128] * scale).astype(BF16)
        for sub in range(2):
            head = 2 * pair + sub
            kvh = head // Q_PER_KV
            sl = slice(kvh * HEAD_DIM, (kvh + 1) * HEAD_DIM)
            qh = q2[:, sub * HEAD_DIM:(sub + 1) * HEAD_DIM]
            outs.append(_softmax_pv(qh, [kc[:, sl]], [vc[:, sl]], [None], sink_ref[head]))
    o_ref[...] = jnp.concatenate(outs, axis=-1).astype(o_ref.dtype)


def _attention(p, sink, cos_t, sin_t, with_ctx):
    blk = ATTN_BLOCK
    nb = SEQ // blk
    qc, kcol, vcol = PQ_OFF // ATTN_WIDTH, PK_OFF // KV_WIDTH, PV_OFF // KV_WIDTH
    ctx_blk = T_LAT // CTX_LEN
    smem = pl.BlockSpec(memory_space=pltpu.SMEM)

    def kv_spec(col, d):
        return pl.BlockSpec((blk, KV_WIDTH),
                            lambda b, n: (b * nb + jnp.clip(n + d, 0, nb - 1), col))

    ya = pl.pallas_call(
        functools.partial(_attn_kernel, nb=nb),
        grid=(BATCH, nb),
        in_specs=[smem,
                  pl.BlockSpec((blk, ATTN_WIDTH), lambda b, n: (b * nb + n, qc)),
                  kv_spec(kcol, -1), kv_spec(kcol, 0), kv_spec(kcol, 1),
                  kv_spec(vcol, -1), kv_spec(vcol, 0), kv_spec(vcol, 1),
                  pl.BlockSpec((CTX_LEN, KV_WIDTH), lambda b, n: (ctx_blk + b, kcol)),
                  pl.BlockSpec((CTX_LEN, KV_WIDTH), lambda b, n: (ctx_blk + b, vcol)),
                  pl.BlockSpec((SEQ, KV_WIDTH), lambda b, n: (0, 0)),
                  pl.BlockSpec((SEQ, KV_WIDTH), lambda b, n: (0, 0)),
                  pl.BlockSpec((128, 128), lambda b, n: (0, 0)),
                  pl.BlockSpec((128, 128), lambda b, n: (0, 0))],
        out_specs=pl.BlockSpec((blk, ATTN_WIDTH), lambda b, n: (b * nb + n, 0)),
        out_shape=jax.ShapeDtypeStruct((T_ALL, ATTN_WIDTH), BF16),
        compiler_params=_cparams(("parallel", "parallel")),
        name="banded_attention",
    )(sink, p, p, p, p, p, p, p, p, p, cos_t, sin_t, *_lane_perms())
    if not with_ctx:
        return ya
    cb = CTX_LEN // blk
    lat_blk = T_LAT // blk

    def alias_kernel(sink_ref, q_ref, kc_ref, vc_ref, ya_in_ref, o_ref):
        del ya_in_ref
        _ctx_attn_kernel(sink_ref, q_ref, kc_ref, vc_ref, o_ref)

    return pl.pallas_call(
        alias_kernel,
        grid=(BATCH, cb),
        in_specs=[smem,
                  pl.BlockSpec((blk, ATTN_WIDTH), lambda b, n: (lat_blk + b * cb + n, qc)),
                  pl.BlockSpec((CTX_LEN, KV_WIDTH), lambda b, n: (ctx_blk + b, kcol)),
                  pl.BlockSpec((CTX_LEN, KV_WIDTH), lambda b, n: (ctx_blk + b, vcol)),
                  pl.BlockSpec(memory_space=pl.ANY)],
        out_specs=pl.BlockSpec((blk, ATTN_WIDTH), lambda b, n: (lat_blk + b * cb + n, 0)),
        out_shape=jax.ShapeDtypeStruct((T_ALL, ATTN_WIDTH), BF16),
        input_output_aliases={4: 0},
        compiler_params=_cparams(("parallel", "parallel")),
        name="context_attention",
    )(sink, p, p, p, ya)


def _rope_tables():
    n_freq = HEAD_DIM // 4
    freqs = ROPE_BASE ** (-jnp.arange(n_freq, dtype=F32) / n_freq)
    t = jnp.arange(SEQ, dtype=jnp.int32)
    rows = (t // GRID_W).astype(F32)[:, None] * freqs
    cols = (t % GRID_W).astype(F32)[:, None] * freqs
    cos_h = jnp.concatenate([jnp.cos(rows), jnp.cos(rows), jnp.cos(cols), jnp.cos(cols)], axis=-1)
    sin_h = jnp.concatenate([-jnp.sin(rows), jnp.sin(rows), -jnp.sin(cols), jnp.sin(cols)], axis=-1)
    return jnp.tile(cos_h, (1, 2)), jnp.tile(sin_h, (1, 2))


def _lane_perms():
    j = jnp.arange(128, dtype=jnp.int32)[:, None]
    l = jnp.arange(128, dtype=jnp.int32)[None, :]
    rot = jnp.where(l % 32 < 16, l + 16, l - 16)
    return (j == rot).astype(BF16), (j == (l + HEAD_DIM) % 128).astype(BF16)


def _lead_kernel(*refs, n_in, cmul, epi):
    m_ref = refs[0]
    x_refs = refs[1:1 + n_in]
    pos = 1 + n_in
    xs = []
    for r in x_refs:
        v = r[...]
        xs.append(v.reshape(-1, v.shape[-1]))
    x = xs[0] if n_in == 1 else jnp.concatenate(xs, axis=0)
    if cmul:
        k = refs[pos][...]
        pos += 1
        k = k.reshape(-1, k.shape[-1])
        half = x.shape[0] // 2
        xr, xi, kr, ki = x[:half], x[half:], k[:half], k[half:]
        x = jnp.concatenate([xr * kr - xi * ki, xr * ki + xi * kr], axis=0)
    res = _dot(m_ref[...], x.astype(BF16))
    if epi:
        g_ref, y_ref, b_ref = refs[pos:pos + 3]
        pos += 3
        g = g_ref[...]
        y = y_ref[...]
        res = g.reshape(-1, g.shape[-1]) * (res + y.reshape(-1, y.shape[-1]) * b_ref[...])
    o_ref = refs[pos]
    o_ref[...] = res.reshape(o_ref.shape).astype(o_ref.dtype)


def _lead(mat, xs, x_specs, grid, out_shape, out_spec, *, kspec=None, epi=None, alias_to=None, name):
    ins = [mat] + list(xs)
    specs = [pl.BlockSpec(mat.shape, lambda *a: (0, 0))] + list(x_specs)
    if kspec is not None:
        ins.append(kspec[0])
        specs.append(kspec[1])
    if epi is not None:
        for arr, sp in epi:
            ins.append(arr)
            specs.append(sp)
    kern = functools.partial(_lead_kernel, n_in=len(xs), cmul=kspec is not None, epi=epi is not None)
    aliases = {}
    if alias_to is not None:
        aliases = {len(ins): 0}
        ins.append(alias_to)
        specs.append(pl.BlockSpec(memory_space=pl.ANY))
        inner = kern

        def kern(*refs):
            inner(*refs[:-2], refs[-1])

    return pl.pallas_call(
        kern, grid=grid, in_specs=specs, out_specs=out_spec, out_shape=out_shape,
        input_output_aliases=aliases,
        compiler_params=_cparams(("parallel",) * len(grid)), name=name,
    )(*ins)


def _cstack(xr, xi):
    return jnp.concatenate([jnp.concatenate([xr, xi], axis=1), jnp.concatenate([-xi, xr], axis=1)], axis=0)


def _fnet_kernel(u_ref, mc_ref, ml_ref, ms_ref, o_ref, zr, zi, are, aim):
    n = FN_N
    pitch = PITCH
    gd = FNET_GROUP_DIM
    rows = 4 * n
    mc = mc_ref[...]

    def chan(i, carry):
        r_in = pl.multiple_of(i * rows, rows)
        z = _dot(u_ref[pl.ds(r_in, rows), :].astype(BF16), mc)
        for q in range(rows // n):
            r_out = pl.multiple_of((i * (rows // n) + q) * pitch, 8)
            zr[pl.ds(r_out, n), :] = z[q * n:(q + 1) * n, :gd]
            zi[pl.ds(r_out, n), :] = z[q * n:(q + 1) * n, gd:]
        return carry

    lax.fori_loop(0, SEQ // rows, chan, 0)
    ml = ml_ref[...]

    def lead(i, carry):
        for u in range(UNROLL):
            n1 = i * UNROLL + u
            x = jnp.concatenate([zr[pl.ds(n1, n, stride=pitch), :], zi[pl.ds(n1, n, stride=pitch), :]],
                                axis=0).astype(BF16)
            r = _dot(ml, x)
            are[pl.ds(n1, n, stride=pitch), :] = r[:n]
            aim[pl.ds(n1, n, stride=pitch), :] = r[n:]
        return carry

    lax.fori_loop(0, n // UNROLL, lead, 0)

    def slab(i, carry):
        for u in range(UNROLL):
            k2 = i * UNROLL + u
            r0 = pl.multiple_of(k2 * pitch, 8)
            x = jnp.concatenate([are[pl.ds(r0, n), :], aim[pl.ds(r0, n), :]], axis=0).astype(BF16)
            o_ref[pl.ds(k2, n, stride=n), :] = _dot(ms_ref[k2], x)
        return carry

    lax.fori_loop(0, n // UNROLL, slab, 0)


def _fnet_ctx_kernel(u_ref, mc_ref, mf_ref, yf_in_ref, o_ref):
    del yf_in_ref
    gd = FNET_GROUP_DIM
    mc = mc_ref[...]
    mf = mf_ref[...]
    for g in range(FNET_GROUPS):
        z = _dot(u_ref[:, g * gd:(g + 1) * gd].astype(BF16), mc)
        x = jnp.concatenate([z[:, :gd], z[:, gd:]], axis=0).astype(BF16)
        o_ref[:, g * gd:(g + 1) * gd] = _dot(mf, x)


def _fourier_mix(p, mats, with_ctx):
    gd = FNET_GROUP_DIM
    col0 = PF_OFF // gd
    full = lambda a: pl.BlockSpec(a.shape, lambda *i: (0,) * a.ndim)
    scr = pltpu.VMEM((FN_N * PITCH, gd), F32)
    yf = pl.pallas_call(
        _fnet_kernel,
        grid=(BATCH, FNET_GROUPS),
        in_specs=[pl.BlockSpec((SEQ, gd), lambda b, g: (b, col0 + g)),
                  full(mats['fn_chan']), full(mats['fn_lead']), full(mats['fn_slab'])],
        out_specs=pl.BlockSpec((SEQ, gd), lambda b, g: (b, g)),
        out_shape=jax.ShapeDtypeStruct((T_ALL if with_ctx else T_LAT, FNET_WIDTH), F32),
        scratch_shapes=[scr, scr, scr, scr],
        compiler_params=_cparams(("parallel", "parallel")),
        name="fnet_latent",
    )(p, mats['fn_chan'], mats['fn_lead'], mats['fn_slab'])
    if not with_ctx:
        return yf
    blk0 = T_LAT // CTX_LEN
    return pl.pallas_call(
        _fnet_ctx_kernel,
        grid=(BATCH,),
        in_specs=[pl.BlockSpec((pl.Element(CTX_LEN), pl.Element(FNET_WIDTH)),
                               lambda b: ((blk0 + b) * CTX_LEN, PF_OFF)),
                  full(mats['fn_chan']), full(mats['fc']), pl.BlockSpec(memory_space=pl.ANY)],
        out_specs=pl.BlockSpec((CTX_LEN, FNET_WIDTH), lambda b: (blk0 + b, 0)),
        out_shape=jax.ShapeDtypeStruct((T_ALL, FNET_WIDTH), F32),
        input_output_aliases={3: 0},
        compiler_params=_cparams(("parallel",)),
        name="fnet_ctx",
    )(p, mats['fn_chan'], mats['fc'], yf)


def _short_conv_kernel(u_ref, w_ref, b_ref, o_ref, *, rows, chunk, padded):
    w0 = w_ref[0:1, :]
    w1 = w_ref[1:2, :]
    w2 = w_ref[2:3, :]
    bias = b_ref[...]
    width = u_ref.shape[-1]
    ridx = lax.broadcasted_iota(jnp.int32, (chunk, width), 0)
    n_chunks = rows // chunk
    for ci in range(n_chunks):
        r0 = ci * chunk
        cur = u_ref[r0:r0 + chunk, :].astype(F32)
        if ci > 0:
            prev_row = u_ref[r0 - 16:r0, :].astype(F32)[15:16, :]
        else:
            prev_row = jnp.zeros((1, width), F32)
        if ci < n_chunks - 1:
            next_row = u_ref[r0 + chunk:r0 + chunk + 16, :].astype(F32)[0:1, :]
        else:
            next_row = jnp.zeros((1, width), F32)
        up = jnp.where(ridx == 0, prev_row, pltpu.roll(cur, 1, 0))
        dn = jnp.where(ridx == chunk - 1, next_row, pltpu.roll(cur, chunk - 1, 0))
        res = up * w0 + cur * w1 + dn * w2 + bias
        if not padded:
            o_ref[0, r0:r0 + chunk, :] = res
            continue
        for q in range(chunk // HY_N2):
            p0 = (r0 // HY_N2 + q) * PITCH
            o_ref[0, p0:p0 + HY_N2, :] = res[q * HY_N2:(q + 1) * HY_N2]
            o_ref[0, p0 + HY_N2:p0 + PITCH, :] = jnp.zeros((PITCH - HY_N2, width), F32)


def _short_conv(p, conv_w, conv_b, with_ctx):
    cw = 256
    hw = (HYENA_ORDER + 1) * HYENA_WIDTH
    ncol = hw // cw
    per = HYENA_WIDTH // cw
    col0 = PH_OFF // cw
    out_shape = jax.ShapeDtypeStruct((HYENA_ORDER + 1, T_PAD if with_ctx else BATCH * SEQ_PAD, HYENA_WIDTH), F32)
    b2 = conv_b.reshape(1, hw)

    def call(rows, blk0, alias):
        padded = rows == SEQ
        out_rows = SEQ_PAD if padded else rows
        out_blk0 = 0 if padded else BATCH * SEQ_PAD // rows
        kern = functools.partial(_short_conv_kernel, rows=rows, chunk=min(rows, 256), padded=padded)
        ins = [p, conv_w, b2]
        specs = [pl.BlockSpec((rows, cw), lambda b, j: (blk0 + b, col0 + j)),
                 pl.BlockSpec((3, cw), lambda b, j: (0, j)),
                 pl.BlockSpec((1, cw), lambda b, j: (0, j))]
        aliases = {}
        if alias is not None:
            ins.append(alias)
            specs.append(pl.BlockSpec(memory_space=pl.ANY))
            aliases = {3: 0}
            inner = kern

            def kern(u_ref, w_ref, b_ref, a_ref, o_ref):
                del a_ref
                inner(u_ref, w_ref, b_ref, o_ref)

        return pl.pallas_call(
            kern, grid=(BATCH, ncol), in_specs=specs,
            out_specs=pl.BlockSpec((1, out_rows, cw), lambda b, j: (j // per, out_blk0 + b, j % per)),
            out_shape=out_shape, input_output_aliases=aliases,
            compiler_params=_cparams(("parallel", "parallel")), name="hyena_short_conv",
        )(*ins)

    z3 = call(SEQ, 0, None)
    if with_ctx:
        z3 = call(CTX_LEN, T_LAT // CTX_LEN, z3)
    return z3


def _filter_mlp_kernel(ft_ref, w1_ref, b1_ref, fq_ref, w2_ref, b2_ref, o_ref):
    fq = fq_ref[...]
    h = jnp.sin(fq * (_dot(ft_ref[...].astype(BF16), w1_ref[...]) + b1_ref[...]))
    h = jnp.sin(fq * (_dot(h.astype(BF16), w2_ref[...]) + b2_ref[...]))
    o_ref[...] = h.astype(o_ref.dtype)


def _filter_kernel(h_ref, t_ref, w3f_ref, w3b_ref, dl_ref, m1_ref, *rest, n, dense):
    hb = h_ref[...]
    decay = jnp.exp(-t_ref[...] * dl_ref[...])
    tf = _dot(hb, w3f_ref[...]) * decay
    tb = _dot(hb, w3b_ref[...]) * decay
    tb = jnp.where(lax.broadcasted_iota(jnp.int32, tb.shape, 0) == 0, 0.0, tb)
    scale = 1.0 / (jnp.sum(jnp.abs(tf), axis=0, keepdims=True) + jnp.sum(jnp.abs(tb), axis=0, keepdims=True))
    cw = tf.shape[1]
    if dense:
        o_ref = rest[0]
        r = _dot(m1_ref[...], jnp.concatenate([tf, tb], axis=1).astype(BF16))
        nc = r.shape[0] // 2
        o_ref[0:nc, :] = (r[:nc, :cw] + r[:nc, cw:]) * scale
        o_ref[nc:, :] = (r[nc:, :cw] - r[nc:, cw:]) * scale
        return
    f2_ref, o_ref, tf_s, tb_s, are_f, aim_f, are_b, aim_b = rest
    tf_s[...] = tf
    tb_s[...] = tb
    half = HY_N1 // 2
    m1 = m1_ref[...]

    def stage1(i, carry):
        for u in range(UNROLL):
            n2 = i * UNROLL + u
            x = jnp.concatenate([tf_s[pl.ds(n2, half, stride=HY_N2), :], tb_s[pl.ds(n2, half, stride=HY_N2), :]],
                                axis=1).astype(BF16)
            r = _dot(m1, x)
            rows = pl.ds(n2, HY_N1, stride=PITCH)
            are_f[rows, :] = r[:HY_N1, :cw]
            are_b[rows, :] = r[:HY_N1, cw:]
            aim_f[rows, :] = r[HY_N1:, :cw]
            aim_b[rows, :] = r[HY_N1:, cw:]
        return carry

    lax.fori_loop(0, HY_N2 // UNROLL, stage1, 0)

    def stage2(i, carry):
        for u in range(UNROLL):
            k1 = i * UNROLL + u
            rows = pl.ds(pl.multiple_of(k1 * PITCH, 8), HY_N2)
            ar = jnp.concatenate([are_f[rows, :], are_b[rows, :]], axis=1)
            ai = jnp.concatenate([aim_f[rows, :], aim_b[rows, :]], axis=1)
            t = _dot(f2_ref[k1], _cstack(ar, ai).astype(BF16))
            o_ref[0, k1] = (t[:, 0:cw] + t[:, cw:2 * cw]) * scale
            o_ref[1, k1] = (t[:, 2 * cw:3 * cw] - t[:, 3 * cw:4 * cw]) * scale
        return carry

    lax.fori_loop(0, HY_N1 // UNROLL, stage2, 0)


def _filter_feats(n):
    pos = jnp.arange(n, dtype=F32)
    t = pos / max(n - 1, 1)
    omega = 2.0 * math.pi * pos / n
    bands = jnp.linspace(1e-4, FILTER_BANDS - 1, FILTER_BANDS, dtype=F32)
    feats = jnp.concatenate([t[:, None], jnp.cos(omega[:, None] * bands), -jnp.sin(omega[:, None] * bands)], axis=-1)
    return jnp.pad(feats, ((0, 0), (0, 128 - FILTER_EMB))), t[:, None]


def _filter_spectrum(n, filt, mats):
    w1, b1, freq, w2, b2, w3 = filt
    dense = n == CTX_LEN
    cw = 128
    nch = HYENA_WIDTH // cw
    feats, t = _filter_feats(n)
    w1p = jnp.pad(w1, ((0, 128 - FILTER_EMB), (0, 0))).astype(BF16)
    deltas = jnp.abs(jnp.linspace(math.log(DECAY_TARGET) / SLOW_DECAY_PCT, math.log(DECAY_TARGET) / FAST_DECAY_PCT,
                                  HYENA_WIDTH, dtype=F32)).reshape(1, HYENA_WIDTH)
    full = lambda a: pl.BlockSpec(a.shape, lambda *i: (0,) * a.ndim)
    row = lambda a: a.reshape(1, -1)
    w3b16 = w3.astype(BF16)
    tap_spec = lambda d: pl.BlockSpec((FILTER_HIDDEN, cw), lambda o, ch: (0, (o * 2 + d) * nch + ch))
    m1 = mats['hc_kh'] if dense else mats['hy_lead_kh']
    mlp_ins = [feats, w1p, row(b1), row(freq), w2.astype(BF16), row(b2)]
    hb = pl.pallas_call(
        _filter_mlp_kernel, grid=(1,), in_specs=[full(a) for a in mlp_ins],
        out_specs=pl.BlockSpec((n, FILTER_HIDDEN), lambda i: (0, 0)),
        out_shape=jax.ShapeDtypeStruct((n, FILTER_HIDDEN), BF16),
        compiler_params=_cparams(("arbitrary",)), name="hyena_filter_mlp",
    )(*mlp_ins)
    ins = [hb, t, w3b16, w3b16, deltas, m1]
    specs = [full(hb), full(t), tap_spec(0), tap_spec(1), pl.BlockSpec((1, cw), lambda o, ch: (0, ch)), full(m1)]
    if dense:
        nc = 2 * n
        out_shape = jax.ShapeDtypeStruct((HYENA_ORDER, 2 * nc, HYENA_WIDTH), F32)
        out_spec = pl.BlockSpec((None, 2 * nc, cw), lambda o, ch: (o, 0, ch))
        scratch = []
    else:
        ins.append(mats['hy_slab_f2'])
        specs.append(full(mats['hy_slab_f2']))
        out_shape = jax.ShapeDtypeStruct((HYENA_ORDER, 2, HY_N1, HY_N2, HYENA_WIDTH), F32)
        out_spec = pl.BlockSpec((None, 2, HY_N1, HY_N2, cw), lambda o, ch: (o, 0, 0, 0, ch))
        scratch = [pltpu.VMEM((n, cw), F32)] * 2 + [pltpu.VMEM((HY_N1 * PITCH, cw), F32)] * 4
    return pl.pallas_call(
        functools.partial(_filter_kernel, n=n, dense=dense),
        grid=(HYENA_ORDER, nch), in_specs=specs, out_specs=out_spec, out_shape=out_shape,
        scratch_shapes=scratch,
        compiler_params=_cparams(("parallel", "parallel")),
        name="hyena_filter_ctx" if dense else "hyena_filter",
    )(*ins)


def _hyena_conv_kernel(y_ref, g_ref, k_ref, m1_ref, f2f_ref, f2i_ref, m3_ref, b_ref, o_ref, are, aim, *, out_pitch):
    half = HY_N1 // 2
    out_seq = half * out_pitch
    m1 = m1_ref[...]

    def stage1(i, carry):
        for u in range(UNROLL):
            n2 = i * UNROLL + u
            x = jnp.concatenate([y_ref[pl.ds(n2, half, stride=PITCH), :],
                                 y_ref[pl.ds(SEQ_PAD + n2, half, stride=PITCH), :]], axis=0).astype(BF16)
            r = _dot(m1, x)
            are[pl.ds(n2, HY_N1, stride=PITCH), :] = r[:HY_N1]
            aim[pl.ds(n2, HY_N1, stride=PITCH), :] = r[HY_N1:]
        return carry

    lax.fori_loop(0, HY_N2 // UNROLL, stage1, 0)
    cw = o_ref.shape[-1]

    def stage2(i, carry):
        for u in range(UNROLL):
            k1 = i * UNROLL + u
            r0 = pl.multiple_of(k1 * PITCH, 8)
            y = _dot(f2f_ref[k1], _cstack(are[pl.ds(r0, HY_N2), :], aim[pl.ds(r0, HY_N2), :]).astype(BF16))
            yr, yi = y[:, :cw], y[:, cw:]
            kr, ki = k_ref[0, k1], k_ref[1, k1]
            w = _dot(f2i_ref[k1], _cstack(yr * kr - yi * ki, yr * ki + yi * kr).astype(BF16))
            are[pl.ds(r0, HY_N2), :] = w[:, :cw]
            aim[pl.ds(r0, HY_N2), :] = w[:, cw:]
        return carry

    lax.fori_loop(0, HY_N1 // UNROLL, stage2, 0)
    m3 = m3_ref[...]
    bias = b_ref[...]

    def stage3(i, carry):
        for u in range(UNROLL):
            n2 = i * UNROLL + u
            bn = jnp.concatenate([are[pl.ds(n2, HY_N1, stride=PITCH), :], aim[pl.ds(n2, HY_N1, stride=PITCH), :]],
                                 axis=0).astype(BF16)
            y = _dot(m3, bn)
            for b in range(2):
                rows = pl.ds(b * SEQ_PAD + n2, half, stride=PITCH)
                o_ref[pl.ds(b * out_seq + n2, half, stride=out_pitch), :] = (
                    g_ref[rows, :] * (y[b * half:(b + 1) * half] + y_ref[rows, :] * bias))
        return carry

    lax.fori_loop(0, HY_N2 // UNROLL, stage3, 0)


def _hyena_mix(z3, filt, hyena_bias, mats, with_ctx):
    c = HYENA_WIDTH
    cw = 128
    nch = c // cw
    pairs = BATCH // 2
    full = lambda a: pl.BlockSpec(a.shape, lambda *i: (0,) * a.ndim)
    kspec = _filter_spectrum(SEQ, filt, mats)
    if with_ctx:
        kspec_c = _filter_spectrum(CTX_LEN, filt, mats)
    bias3 = hyena_bias.reshape(HYENA_ORDER, 1, c)
    scr = pltpu.VMEM((HY_N1 * PITCH, cw), F32)
    y = None
    for o in range(HYENA_ORDER):
        final = o == HYENA_ORDER - 1
        out_pitch = HY_N2 if final else PITCH
        out_seq = SEQ if final else SEQ_PAD
        if final:
            out_rows = T_ALL if with_ctx else T_LAT
        else:
            out_rows = T_PAD if with_ctx else BATCH * SEQ_PAD
        if y is None:
            xin, xspec = z3, pl.BlockSpec((None, 2 * SEQ_PAD, cw), lambda b, ch: (0, b, ch))
        else:
            xin, xspec = y, pl.BlockSpec((2 * SEQ_PAD, cw), lambda b, ch: (b, ch))
        ynew = pl.pallas_call(
            functools.partial(_hyena_conv_kernel, out_pitch=out_pitch),
            grid=(pairs, nch),
            in_specs=[xspec,
                      pl.BlockSpec((None, 2 * SEQ_PAD, cw), lambda b, ch, o=o: (o + 1, b, ch)),
                      pl.BlockSpec((None, 2, HY_N1, HY_N2, cw), lambda b, ch, o=o: (o, 0, 0, 0, ch)),
                      full(mats['hy_lead_f']), full(mats['hy_slab_f2']), full(mats['hy_slab_i2']),
                      full(mats['hy_lead_i']),
                      pl.BlockSpec((None, 1, cw), lambda b, ch, o=o: (o, 0, ch))],
            out_specs=pl.BlockSpec((2 * out_seq, cw), lambda b, ch: (b, ch)),
            out_shape=jax.ShapeDtypeStruct((out_rows, c), F32),
            scratch_shapes=[scr, scr],
            compiler_params=_cparams(("parallel", "parallel"), VMEM_LIMIT_CONV),
            name="hyena_conv",
        )(xin, z3, kspec, mats['hy_lead_f'], mats['hy_slab_f2'], mats['hy_slab_i2'], mats['hy_lead_i'], bias3)
        if with_ctx:
            nc = 2 * CTX_LEN
            blk0 = BATCH * SEQ_PAD // nc
            oblk0 = BATCH * out_seq // nc
            if y is None:
                cin, cspec_in = z3, pl.BlockSpec((1, nc, c), lambda b: (0, blk0 + b, 0))
            else:
                cin, cspec_in = y, pl.BlockSpec((nc, c), lambda b: (blk0 + b, 0))
            xc = _lead(mats['hc_f'], [cin], [cspec_in], (pairs,),
                       jax.ShapeDtypeStruct((pairs, 2 * nc, c), F32),
                       pl.BlockSpec((1, 2 * nc, c), lambda b: (b, 0, 0)), name="hyena_ctx_fwd")
            ynew = _lead(mats['hc_i'], [xc], [pl.BlockSpec((1, 2 * nc, c), lambda b: (b, 0, 0))],
                         (pairs,), jax.ShapeDtypeStruct((out_rows, c), F32),
                         pl.BlockSpec((nc, c), lambda b, oblk0=oblk0: (oblk0 + b, 0)),
                         kspec=(kspec_c, pl.BlockSpec((1, 2 * nc, c), lambda b, o=o: (o, 0, 0))),
                         epi=[(z3, pl.BlockSpec((1, nc, c), lambda b, o=o: (o + 1, blk0 + b, 0))),
                              (cin, cspec_in),
                              (bias3, pl.BlockSpec((1, 1, c), lambda b, o=o: (o, 0, 0)))],
                         alias_to=ynew, name="hyena_ctx_inv")
        y = ynew
    return y


def _route_tile(lt, br, base, tri):
    tm = lt.shape[1]
    aff = jax.nn.sigmoid(lt)
    biased = aff + br
    b = [biased[e:e + 1, :] for e in range(N_EXPERTS)]
    a = [aff[e:e + 1, :] for e in range(N_EXPERTS)]
    epg = EXPERTS_PER_GROUP
    scores = []
    for g in range(N_GROUPS):
        x0, x1, x2, x3 = b[epg * g:epg * g + epg]
        s1, t1 = jnp.maximum(x0, x1), jnp.minimum(x0, x1)
        s2, t2 = jnp.maximum(x2, x3), jnp.minimum(x2, x3)
        scores.append(jnp.maximum(s1, s2) + jnp.maximum(jnp.minimum(s1, s2), jnp.maximum(t1, t2)))
    best = scores[0]
    gsel = jnp.zeros((1, tm), jnp.int32)
    for g in range(1, N_GROUPS):
        gsel = jnp.where(scores[g] > best, g, gsel)
        best = jnp.maximum(best, scores[g])

    def pick(rows, j):
        out = rows[j]
        for g in range(1, N_GROUPS):
            out = jnp.where(gsel == g, rows[epg * g + j], out)
        return out

    v = [pick(b, j) for j in range(epg)]
    av = [pick(a, j) for j in range(epg)]
    i1 = jnp.zeros((1, tm), jnp.int32)
    m1 = v[0]
    for j in range(1, epg):
        i1 = jnp.where(v[j] > m1, j, i1)
        m1 = jnp.maximum(m1, v[j])
    neg = jnp.float32(-3.0e38)
    i2 = jnp.zeros((1, tm), jnp.int32)
    m2 = jnp.full((1, tm), neg, F32)
    for j in range(epg):
        cand = jnp.where(i1 == j, neg, v[j])
        take = cand > m2
        i2 = jnp.where(take, j, i2)
        m2 = jnp.where(take, cand, m2)

    def sel(rows, idx):
        out = rows[0]
        for j in range(1, epg):
            out = jnp.where(idx == j, rows[j], out)
        return out

    a1, a2 = sel(av, i1), sel(av, i2)
    den = a1 + a2
    e1 = gsel * epg + i1
    e2 = gsel * epg + i2
    eio = lax.broadcasted_iota(jnp.int32, (N_EXPERTS, tm), 0)
    oh1 = jnp.where(eio == e1, 1.0, 0.0)
    oh2 = jnp.where(eio == e2, 1.0, 0.0)
    oh = oh1 + oh2
    tot = base + _dot(oh.astype(BF16), tri)
    r1 = jnp.sum(oh1 * tot, axis=0, keepdims=True)
    r2 = jnp.sum(oh2 * tot, axis=0, keepdims=True)
    new_base = base + jnp.sum(oh, axis=1, keepdims=True)
    return (e1, e2), (a1 / den, a2 / den), (r1.astype(jnp.int32), r2.astype(jnp.int32)), new_base


def _merge_kernel(x_ref, ya_ref, yf_ref, yh_ref, gt_ref, wa_ref, wf_ref, wh_ref, wo_ref, g1_ref,
                  gn_ref, sh_ref, sc_ref, wrt_ref, br_ref, xo_ref, h2_ref, e_ref, w_ref, r_ref, cnt_ref):
    d = D_MODEL
    gate = lambda k: jax.nn.sigmoid(gt_ref[:, k * d:(k + 1) * d].astype(F32))
    merged = gate(0) * _dot(ya_ref[...], wa_ref[...])
    merged += gate(1) * _dot(yf_ref[...].astype(BF16), wf_ref[...])
    merged += gate(2) * _dot(yh_ref[...].astype(BF16), wh_ref[...])
    xn = x_ref[...] + g1_ref[0, 0] * _dot(merged.astype(BF16), wo_ref[...])
    xo_ref[...] = xn
    h2f = _rms_mod(xn, gn_ref[...], sh_ref[0, 0], sc_ref[0, 0])
    _rows_to_tiles(h2_ref, h2f, 0, h2f.shape[0])
    h2 = h2f.astype(BF16)

    @pl.when(pl.program_id(0) == 0)
    def _():
        cnt_ref[...] = jnp.zeros_like(cnt_ref)

    tm = h2.shape[0]
    lt = lax.dot_general(wrt_ref[...], h2, (((1,), (1,)), ((), ())), preferred_element_type=F32)
    tri = jnp.where(lax.broadcasted_iota(jnp.int32, (tm, tm), 0) < lax.broadcasted_iota(jnp.int32, (tm, tm), 1),
                    1.0, 0.0).astype(BF16)
    es, ws, rs, new_base = _route_tile(lt, br_ref[...], cnt_ref[:, 0:1], tri)
    e_ref[0:1, :], e_ref[1:2, :] = es
    w_ref[0:1, :], w_ref[1:2, :] = ws
    r_ref[0:1, :], r_ref[1:2, :] = rs
    cnt_ref[...] = jnp.broadcast_to(new_base, cnt_ref.shape)


def _merge(x, ya, yf, yh, p, wa, wf, wh, wo, mod4, gain2, wrt, br, n_tok):
    tm = TM
    row = _mod_row(tm)
    full = lambda a: pl.BlockSpec(a.shape, lambda i: (0,) * a.ndim)
    modspec = lambda k: pl.BlockSpec((1, 1, 1, D_MODEL), lambda i: (row(i), k, 0, 0))
    tok = lambda w: pl.BlockSpec((tm, w), lambda i: (i, 0))
    lane = pl.BlockSpec((TOP_K, tm), lambda i: (0, i))
    return pl.pallas_call(
        _merge_kernel,
        grid=(n_tok // tm,),
        in_specs=[tok(D_MODEL), tok(ATTN_WIDTH), tok(FNET_WIDTH), tok(HYENA_WIDTH),
                  pl.BlockSpec((pl.Element(tm), pl.Element(3 * D_MODEL)), lambda i: (i * tm, PG_OFF)),
                  full(wa), full(wf), full(wh), full(wo), modspec(2), full(gain2), modspec(3), modspec(4),
                  full(wrt), full(br)],
        out_specs=[tok(D_MODEL), pl.BlockSpec((tm * ROW_TILE, 128), lambda i: (i, 0)), lane, lane, lane,
                   pl.BlockSpec((N_EXPERTS, 128), lambda i: (0, 0))],
        out_shape=[jax.ShapeDtypeStruct((n_tok, D_MODEL), F32),
                   jax.ShapeDtypeStruct((n_tok * ROW_TILE, 128), F32),
                   jax.ShapeDtypeStruct((TOP_K, n_tok), jnp.int32),
                   jax.ShapeDtypeStruct((TOP_K, n_tok), F32),
                   jax.ShapeDtypeStruct((TOP_K, n_tok), jnp.int32),
                   jax.ShapeDtypeStruct((N_EXPERTS, 128), F32)],
        compiler_params=_cparams(("arbitrary",)),
        name="merge_out_norm_route",
    )(x, ya, yf, yh, p, wa, wf, wh, wo, mod4, gain2, mod4, mod4, wrt, br)


ROW_TILE = D_MODEL // 128


def _rows_from_tiles(ref, n):
    return jnp.concatenate([ref[pl.ds(s, n, stride=ROW_TILE), :] for s in range(ROW_TILE)], axis=1)


def _rows_to_tiles(ref, val, col0, n):
    for j in range(val.shape[1] // 128):
        ref[pl.ds(col0 // 128 + j, n, stride=ROW_TILE), :] = val[:, j * 128:(j + 1) * 128]


def _moe_kernel(be_ref, na_ref, ip_ref, ic_ref, sc_ref, sn_ref, h2_ref, wg_ref, wu_ref, wd_ref, y_ref,
                xbuf, obuf, gsem, ssem, wg_s, wu_s, wd_s):
    i = pl.program_id(0)
    n_act = na_ref[0]
    sub = ROW_TILE
    bm = xbuf.shape[1] // sub
    active = i < n_act
    slot = i % 2
    other = (i + 1) % 2
    n_chunk = 4
    cw = EXPERT_FF // n_chunk
    rows_per = bm // n_chunk

    def tile_rows(ref, row):
        return ref.at[pl.ds(pl.multiple_of(row * sub, sub), sub), :]

    def gather(src_ref, dst_slot, lo, hi):
        for r in range(lo, hi):
            pltpu.make_async_copy(tile_rows(h2_ref, src_ref[0, 0, r]), xbuf.at[dst_slot, pl.ds(r * sub, sub), :],
                                  gsem.at[dst_slot]).start(priority=r % 2)

    def scatter(info_ref, src_slot, lo, hi):
        for r in range(lo, hi):
            pltpu.make_async_copy(obuf.at[src_slot, pl.ds(r * sub, sub), :], tile_rows(y_ref, info_ref[0, 0, r]),
                                  ssem.at[src_slot]).start(priority=r % 2)

    def wait_gather(s):
        pltpu.make_async_copy(h2_ref.at[pl.ds(0, bm * sub), :], xbuf.at[s], gsem.at[s]).wait()

    def wait_scatter(s):
        pltpu.make_async_copy(obuf.at[s], y_ref.at[pl.ds(0, bm * sub), :], ssem.at[s]).wait()

    @pl.when((i == 0) & active)
    def _():
        obuf[...] = jnp.zeros_like(obuf)
        gather(sc_ref, 0, 0, bm)

    prev = be_ref[jnp.maximum(i - 1, 0)]

    @pl.when(active & ((i == 0) | (be_ref[i] != prev)))
    def _():
        wg_s[...] = wg_ref[...].astype(BF16)
        wu_s[...] = wu_ref[...].astype(BF16)
        wd_s[...] = wd_ref[...].astype(BF16)

    @pl.when(active)
    def _():
        wait_gather(slot)
        x = _rows_from_tiles(xbuf.at[slot], bm).astype(BF16)
        hs = []
        for c in range(n_chunk):
            g = _dot(x, wg_s[:, c * cw:(c + 1) * cw])
            u = _dot(x, wu_s[:, c * cw:(c + 1) * cw])
            hs.append(((g * jax.nn.sigmoid(g)) * u).astype(BF16))
            gather(sn_ref, other, c * rows_per, (c + 1) * rows_per)
        h = jnp.concatenate(hs, axis=1)

        @pl.when(i > 0)
        def _():
            wait_scatter(slot)

        for c in range(n_chunk):
            _rows_to_tiles(obuf.at[slot], _dot(h, wd_s[:, c * cw:(c + 1) * cw]), c * cw, bm)
            scatter(ip_ref, other, c * rows_per, (c + 1) * rows_per)

    @pl.when(i == n_act - 1)
    def _():
        wait_gather(other)
        wait_scatter(other)
        scatter(ic_ref, slot, 0, bm)
        wait_scatter(slot)


def _moe(h2, info, blk_expert, n_active, wg, wu, wd, layer, n_tok):
    bm = EXPERT_BM
    n_rows = info.shape[0]
    n_blk = n_rows // bm
    info3 = info.reshape(n_blk, 1, bm)
    src3 = jnp.where(info3 >= TOP_K * n_tok, 0, info3 % n_tok)
    wspec = lambda k, n: pl.BlockSpec((None, None, k, n), lambda i, be, na: (layer, be[i], 0, 0))
    ispec = lambda f: pl.BlockSpec((1, 1, bm), lambda i, be, na: (f(i), 0, 0), memory_space=pltpu.SMEM)
    return pl.pallas_call(
        _moe_kernel,
        grid_spec=pltpu.PrefetchScalarGridSpec(
            num_scalar_prefetch=2,
            grid=(n_blk,),
            in_specs=[ispec(lambda i: jnp.maximum(i - 1, 0)), ispec(lambda i: i),
                      ispec(lambda i: i), ispec(lambda i: jnp.minimum(i + 1, n_blk - 1)),
                      pl.BlockSpec(memory_space=pl.ANY),
                      wspec(D_MODEL, EXPERT_FF), wspec(D_MODEL, EXPERT_FF), wspec(EXPERT_FF, D_MODEL)],
            out_specs=pl.BlockSpec(memory_space=pl.ANY),
            scratch_shapes=[pltpu.VMEM((2, bm * ROW_TILE, 128), F32), pltpu.VMEM((2, bm * ROW_TILE, 128), F32),
                            pltpu.SemaphoreType.DMA((2,)), pltpu.SemaphoreType.DMA((2,)),
                            pltpu.VMEM((D_MODEL, EXPERT_FF), BF16), pltpu.VMEM((D_MODEL, EXPERT_FF), BF16),
                            pltpu.VMEM((EXPERT_FF, D_MODEL), BF16)]),
        out_shape=jax.ShapeDtypeStruct(((TOP_K * n_tok + n_rows) * ROW_TILE, 128), F32),
        compiler_params=_cparams(("arbitrary",)),
        name="moe_experts",
    )(blk_expert, n_active, info3, info3, src3, src3, h2, wg, wu, wd)


def _dispatch_info(e_idx, rank, counts, n_tok):
    bm = EXPERT_BM
    counts = counts.astype(jnp.int32)
    padded = (counts + bm - 1) // bm * bm
    pad_end = jnp.cumsum(padded)
    pad_start = pad_end - padded
    experts = jnp.arange(N_EXPERTS, dtype=jnp.int32)
    start = jnp.sum(jnp.where(e_idx[..., None] == experts, pad_start, 0), axis=-1)
    dest = start + rank
    n_rows = -(-(n_tok * TOP_K) // bm) * bm + N_EXPERTS * bm
    n_blk = n_rows // bm
    spill = TOP_K * n_tok + jnp.arange(n_rows, dtype=jnp.int32)
    info = spill.at[dest.reshape(-1)].set(jnp.arange(TOP_K * n_tok, dtype=jnp.int32))
    blk_start = jnp.arange(n_blk, dtype=jnp.int32) * bm
    blk_expert = jnp.minimum(jnp.sum((blk_start[:, None] >= pad_end[None, :]).astype(jnp.int32), axis=-1),
                             N_EXPERTS - 1)
    n_active = (pad_end[-1] // bm).astype(jnp.int32).reshape(1)
    return info, blk_expert, n_active


def _residual_kernel(x_ref, y0_ref, y1_ref, w_ref, g2_ref, gf_ref, o_ref, *, final):
    w = w_ref[...]
    tm = x_ref.shape[0]
    moe = _rows_from_tiles(y0_ref, tm) * w[:, 0:1] + _rows_from_tiles(y1_ref, tm) * w[:, 1:2]
    xn = x_ref[...] + g2_ref[0, 0] * moe
    if final:
        y = xn * lax.rsqrt(jnp.mean(xn * xn, axis=-1, keepdims=True) + EPS)
        xn = y * gf_ref[...]
    o_ref[...] = xn


def _residual(x, y, w_sel, mod4, gain_final, n_tok, final):
    tm = TM
    n_tiles = n_tok // tm
    row = _mod_row(tm)
    tok = lambda w: pl.BlockSpec((tm, w), lambda i: (i, 0))
    return pl.pallas_call(
        functools.partial(_residual_kernel, final=final),
        grid=(n_tiles,),
        in_specs=[tok(D_MODEL), pl.BlockSpec((tm * ROW_TILE, 128), lambda i: (i, 0)),
                  pl.BlockSpec((tm * ROW_TILE, 128), lambda i: (n_tiles + i, 0)), tok(TOP_K),
                  pl.BlockSpec((1, 1, 1, D_MODEL), lambda i: (row(i), 5, 0, 0)),
                  pl.BlockSpec((1, D_MODEL), lambda i: (0, 0))],
        out_specs=tok(D_MODEL),
        out_shape=jax.ShapeDtypeStruct((n_tok if final else T_ALL, D_MODEL), F32),
        compiler_params=_cparams(("parallel",)),
        name="moe_residual",
    )(x, y, y, w_sel, mod4, gain_final)


def kernel(x, c, ctx, c_ctx, w_mod, b_mod, norm_mix, norm_ffn, w_in, attn_sink, conv_w, conv_b, filt_w1, filt_b1, filt_freq, filt_w2, filt_b2, filt_w3, hyena_bias, w_branch_attn, w_branch_fnet, w_branch_hyena, w_out, w_router, b_router, w_exp_gate, w_exp_up, w_exp_down, norm_final):
    mats = _dft_mats()
    cos_t, sin_t = _rope_tables()
    c8 = jnp.concatenate([c, c_ctx[None, :], jnp.zeros((8 - BATCH - 1, D_MODEL), F32)], axis=0)
    mod_all = _modulation(c8, w_mod, b_mod)
    xa = jnp.concatenate([x.reshape(T_LAT, D_MODEL), ctx.reshape(T_CTX, D_MODEL)], axis=0)
    wrt = w_router.T.astype(BF16)
    br = b_router.astype(F32).reshape(N_EXPERTS, 1)
    gain_final = norm_final.reshape(1, D_MODEL)
    out = None
    for l in range(DEPTH):
        last = l == DEPTH - 1
        with_ctx = not last
        n_tok = T_LAT if last else T_ALL
        mod4 = mod_all[l].reshape(8, N_MOD, 1, D_MODEL)
        p = _norm_proj(xa, norm_mix[l].reshape(1, D_MODEL), mod4, w_in, l, T_ALL)
        ya = _attention(p, attn_sink[l], cos_t, sin_t, with_ctx)
        yf = _fourier_mix(p, mats, with_ctx)
        z3 = _short_conv(p, conv_w[l], conv_b[l], with_ctx)
        filt = (filt_w1[l], filt_b1[l], filt_freq[l], filt_w2[l], filt_b2[l], filt_w3[l])
        yh = _hyena_mix(z3, filt, hyena_bias[l], mats, with_ctx)
        xa, h2, e_idx, w_sel, rank, cnt = _merge(
            xa, ya, yf, yh, p, w_branch_attn[l].astype(BF16), w_branch_fnet[l].astype(BF16),
            w_branch_hyena[l].astype(BF16), w_out[l].astype(BF16), mod4,
            norm_ffn[l].reshape(1, D_MODEL), wrt, br, n_tok)
        info, blk_expert, n_active = _dispatch_info(e_idx, rank, cnt[:, 0], n_tok)
        y = _moe(h2, info, blk_expert, n_active, w_exp_gate, w_exp_up, w_exp_down, l, n_tok)
        res = _residual(xa, y, w_sel.T, mod4, gain_final, n_tok, last)
        if last:
            out = res
        else:
            xa = res
    return out.reshape(BATCH, SEQ, D_MODEL)
```

```python
import functools
import math

import jax
import jax.numpy as jnp
from jax import lax
from jax.experimental import pallas as pl
from jax.experimental.pallas import tpu as pltpu

F32 = jnp.float32
BF16 = jnp.bfloat16

D_MODEL = 1024
BATCH = 4
SEQ = 4096
DEPTH = 4
GRID_W = 64
CTX_LEN = 256
EPS = 1e-6
N_MOD = 6

HEAD_DIM = 64
N_Q_HEADS = 8
N_KV_HEADS = 2
Q_PER_KV = N_Q_HEADS // N_KV_HEADS
ATTN_BLOCK = 128
ROPE_BASE = 10000.0

FNET_GROUPS = 4
FNET_GROUP_DIM = 128
FNET_WIDTH = FNET_GROUPS * FNET_GROUP_DIM

HYENA_WIDTH = 512
HYENA_ORDER = 2
FILTER_EMB = 33
FILTER_BANDS = (FILTER_EMB - 1) // 2
FILTER_HIDDEN = 64
DECAY_TARGET = 1e-2
FAST_DECAY_PCT = 0.3
SLOW_DECAY_PCT = 1.5

ATTN_WIDTH = N_Q_HEADS * HEAD_DIM
KV_WIDTH = N_KV_HEADS * HEAD_DIM
Q_OFF = 0
K_OFF = Q_OFF + ATTN_WIDTH
V_OFF = K_OFF + KV_WIDTH
F_OFF = V_OFF + KV_WIDTH
H_OFF = F_OFF + FNET_WIDTH
G_OFF = H_OFF + (HYENA_ORDER + 1) * HYENA_WIDTH
IN_WIDTH = G_OFF + 3 * D_MODEL

N_EXPERTS = 16
N_GROUPS = 4
EXPERTS_PER_GROUP = N_EXPERTS // N_GROUPS
TOP_K = 2
EXPERT_FF = 1024

T_LAT = BATCH * SEQ
T_CTX = BATCH * CTX_LEN
T_ALL = T_LAT + T_CTX

PG_OFF = G_OFF
PH_OFF = H_OFF
PF_OFF = F_OFF
PQ_OFF = Q_OFF
PK_OFF = K_OFF
PV_OFF = V_OFF

HY_N = 2 * SEQ
HY_N2 = 64
HY_N1 = HY_N // HY_N2
FN_N = 64

PITCH = HY_N2 + 8
SEQ_PAD = SEQ // HY_N2 * PITCH
T_PAD = BATCH * SEQ_PAD + T_CTX
UNROLL = 16
TM = 512
EXPERT_BM = 256
VMEM_LIMIT = 52 * 1024 * 1024
VMEM_LIMIT_CONV = 58 * 1024 * 1024


def _cparams(sem, vmem=VMEM_LIMIT):
    return pltpu.CompilerParams(dimension_semantics=sem, vmem_limit_bytes=vmem)


def _dot(a, b):
    return jnp.dot(a, b, preferred_element_type=F32)


def _cis(expo, n):
    ang = (2.0 * math.pi / n) * jnp.mod(expo, n).astype(F32)
    return jnp.cos(ang), jnp.sin(ang)


def _real_form(gr, gi):
    return jnp.concatenate([jnp.concatenate([gr, -gi], axis=-1), jnp.concatenate([gi, gr], axis=-1)], axis=-2)


def _dft_mats():
    ar = lambda n: jnp.arange(n, dtype=jnp.int32)
    m = {}
    c, s = _cis(ar(HY_N1)[:, None] * ar(HY_N1 // 2)[None, :], HY_N1)
    m['hy_lead_f'] = _real_form(c, -s).astype(BF16)
    c, s = _cis(ar(HY_N1 // 2)[:, None] * ar(HY_N1)[None, :], HY_N1)
    m['hy_lead_i'] = _real_form(c, s).astype(BF16)
    c, s = _cis(ar(HY_N1)[:, None] * ar(HY_N1 // 2)[None, :], HY_N1)
    m['hy_lead_kh'] = jnp.concatenate([c, -s], axis=0).astype(BF16)
    a = ar(HY_N1)[:, None, None]
    k2 = ar(HY_N2)[None, :, None]
    n2 = ar(HY_N2)[None, None, :]
    c, s = _cis(n2 * (a + HY_N1 * k2), HY_N)
    m['hy_slab_f2'] = jnp.concatenate([c, -s], axis=-1).astype(BF16)
    ct = jnp.swapaxes(c, 1, 2) * (1.0 / HY_N)
    st = jnp.swapaxes(s, 1, 2) * (1.0 / HY_N)
    m['hy_slab_i2'] = jnp.concatenate([ct, st], axis=-1).astype(BF16)
    nc = 2 * CTX_LEN
    c, s = _cis(ar(nc)[:, None] * ar(CTX_LEN)[None, :], nc)
    m['hc_f'] = _real_form(c, -s).astype(BF16)
    c, s = _cis(ar(CTX_LEN)[:, None] * ar(nc)[None, :], nc)
    m['hc_i'] = _real_form(c * (1.0 / nc), s * (1.0 / nc)).astype(BF16)
    c, s = _cis(ar(nc)[:, None] * ar(CTX_LEN)[None, :], nc)
    m['hc_kh'] = jnp.concatenate([c, -s], axis=0).astype(BF16)
    c, s = _cis(ar(FNET_GROUP_DIM)[:, None] * ar(FNET_GROUP_DIM)[None, :], FNET_GROUP_DIM)
    m['fn_chan'] = jnp.concatenate([c, -s], axis=1).astype(BF16)
    c, s = _cis(ar(FN_N)[:, None] * ar(FN_N)[None, :], FN_N)
    m['fn_lead'] = _real_form(c, -s).astype(BF16)
    a = ar(FN_N)[:, None, None]
    k1 = ar(FN_N)[None, :, None]
    n1 = ar(FN_N)[None, None, :]
    scale = 1.0 / math.sqrt(SEQ * FNET_GROUP_DIM)
    c, s = _cis(n1 * (a + FN_N * k1), SEQ)
    m['fn_slab'] = jnp.concatenate([c * scale, s * scale], axis=-1).astype(BF16)
    scale = 1.0 / math.sqrt(CTX_LEN * FNET_GROUP_DIM)
    c, s = _cis(ar(CTX_LEN)[:, None] * ar(CTX_LEN)[None, :], CTX_LEN)
    m['fc'] = jnp.concatenate([c * scale, s * scale], axis=-1).astype(BF16)
    return m


def _mod_kernel(c_ref, w_ref, b_ref, o_ref):
    c = c_ref[...]
    s = c * jax.nn.sigmoid(c)
    o_ref[0] = _dot(s.astype(BF16), w_ref[0].astype(BF16)) + b_ref[0]


def _modulation(c8, w_mod, b_mod):
    tn = 1536
    n = N_MOD * D_MODEL
    return pl.pallas_call(
        _mod_kernel,
        grid=(DEPTH, n // tn),
        in_specs=[pl.BlockSpec((8, D_MODEL), lambda l, j: (0, 0)),
                  pl.BlockSpec((1, D_MODEL, tn), lambda l, j: (l, 0, j)),
                  pl.BlockSpec((1, 1, tn), lambda l, j: (l, 0, j))],
        out_specs=pl.BlockSpec((1, 8, tn), lambda l, j: (l, 0, j)),
        out_shape=jax.ShapeDtypeStruct((DEPTH, 8, n), F32),
        compiler_params=_cparams(("parallel", "parallel")),
        name="adaln_modulation",
    )(c8, w_mod, b_mod.reshape(DEPTH, 1, n))


def _mod_row(tm):
    tiles_per_batch = SEQ // tm
    return lambda i: jnp.minimum(i // tiles_per_batch, BATCH)


def _rms_mod(x, g, sh, sc):
    y = x * lax.rsqrt(jnp.mean(x * x, axis=-1, keepdims=True) + EPS)
    return (y * g) * (1.0 + sc) + sh


def _norm_proj_kernel(x_ref, g_ref, sh_ref, sc_ref, w_ref, o_ref, w_s):
    @pl.when(pl.program_id(1) == 0)
    def _():
        w_s[...] = w_ref[...].astype(BF16)

    h = _rms_mod(x_ref[...], g_ref[...], sh_ref[0, 0], sc_ref[0, 0]).astype(BF16)
    o_ref[...] = _dot(h, w_s[...]).astype(o_ref.dtype)


def _norm_proj(x, gain, mod4, w, layer, n_tok):
    tm = TM
    n_out = w.shape[2]
    tn = n_out // 2
    row = _mod_row(tm)
    return pl.pallas_call(
        _norm_proj_kernel,
        grid=(n_out // tn, n_tok // tm),
        in_specs=[pl.BlockSpec((tm, D_MODEL), lambda j, i: (i, 0)),
                  pl.BlockSpec((1, D_MODEL), lambda j, i: (0, 0)),
                  pl.BlockSpec((1, 1, 1, D_MODEL), lambda j, i: (row(i), 0, 0, 0)),
                  pl.BlockSpec((1, 1, 1, D_MODEL), lambda j, i: (row(i), 1, 0, 0)),
                  pl.BlockSpec((None, D_MODEL, tn), lambda j, i: (layer, 0, j))],
        out_specs=pl.BlockSpec((tm, tn), lambda j, i: (i, j)),
        out_shape=jax.ShapeDtypeStruct((T_ALL, n_out), BF16),
        scratch_shapes=[pltpu.VMEM((D_MODEL, tn), BF16)],
        compiler_params=_cparams(("arbitrary", "arbitrary")),
        name="norm_in_proj",
    )(x, gain, mod4, mod4, w)


def _softmax_pv(qh, k_parts, v_parts, masks, sink):
    nt = (((1,), (1,)), ((), ()))
    scores = []
    for kp, mk in zip(k_parts, masks):
        s = lax.dot_general(qh, kp, nt, preferred_element_type=F32)
        if mk is not None:
            s = jnp.where(mk, s, -1e30)
        scores.append(s)
    m = sink
    for s in scores:
        m = jnp.maximum(m, jnp.max(s, axis=-1, keepdims=True))
    es = [jnp.exp(s - m) for s in scores]
    den = jnp.exp(sink - m)
    for e in es:
        den = den + jnp.sum(e, axis=-1, keepdims=True)
    inv = 1.0 / den
    o = None
    for e, vp in zip(es, v_parts):
        t = _dot((e * inv).astype(BF16), vp)
        o = t if o is None else o + t
    return o


def _attn_kernel(sink_ref, q_ref, km_ref, k0_ref, kp_ref, vm_ref, v0_ref, vp_ref, kc_ref, vc_ref,
                 cos_ref, sin_ref, psw_ref, p64_ref, o_ref, *, nb):
    n = pl.program_id(1)
    blk = ATTN_BLOCK
    psw = psw_ref[...]
    p64 = p64_ref[...]

    def rope(xb, blk_idx):
        r0 = pl.multiple_of(blk_idx * blk, blk)
        return xb.astype(F32) * cos_ref[pl.ds(r0, blk), :] + _dot(xb, psw) * sin_ref[pl.ds(r0, blk), :]

    nm = jnp.maximum(n - 1, 0)
    npl = jnp.minimum(n + 1, nb - 1)
    kall = jnp.concatenate([rope(km_ref[...], nm), rope(k0_ref[...], n), rope(kp_ref[...], npl),
                            kc_ref[...].astype(F32)], axis=0)
    vall = jnp.concatenate([vm_ref[...], v0_ref[...], vp_ref[...], vc_ref[...]], axis=0)
    nk = kall.shape[0]
    lo = lax.broadcasted_iota(jnp.int32, (nk, 128), 1) < HEAD_DIM

    r = lax.broadcasted_iota(jnp.int32, (blk, blk), 0)
    cidx = lax.broadcasted_iota(jnp.int32, (blk, blk), 1)
    ok_prev = jnp.where(cidx >= r, (n > 0).astype(jnp.int32), 0) > 0
    ok_next = jnp.where(cidx <= r, (n < nb - 1).astype(jnp.int32), 0) > 0
    neg = jnp.float32(-1e30)

    scale = HEAD_DIM ** -0.5
    q2 = [(rope(q_ref[:, p * 128:(p + 1) * 128], n) * scale).astype(BF16) for p in range(N_Q_HEADS // 2)]
    nt = (((1,), (1,)), ((), ()))
    n_pair = Q_PER_KV // 2
    for h in range(N_KV_HEADS):
        kh = (jnp.where(lo, kall, 0.0) if h == 0 else jnp.where(lo, 0.0, kall)).astype(BF16)
        vh = jnp.where(lo, vall, jnp.zeros_like(vall)) if h == 0 else jnp.where(lo, jnp.zeros_like(vall), vall)
        q4 = jnp.concatenate(q2[n_pair * h:n_pair * (h + 1)], axis=0)
        q4s = _dot(q4, p64).astype(BF16)
        first, second = (q4, q4s) if h == 0 else (q4s, q4)
        s = lax.dot_general(jnp.concatenate([first, second], axis=0), kh, nt, preferred_element_type=F32)
        probs = []
        for rb in range(2 * n_pair):
            sink = sink_ref[Q_PER_KV * h + 2 * (rb % n_pair) + rb // n_pair]
            sb = s[rb * blk:(rb + 1) * blk]
            parts = [jnp.where(ok_prev, sb[:, 0:blk], neg), sb[:, blk:2 * blk],
                     jnp.where(ok_next, sb[:, 2 * blk:3 * blk], neg), sb[:, 3 * blk:nk]]
            tiles = parts[:3] + [parts[3][:, t * blk:(t + 1) * blk] for t in range((nk - 3 * blk) // blk)]
            m = jnp.maximum(sink, jnp.max(functools.reduce(jnp.maximum, tiles), axis=-1, keepdims=True))
            es = [jnp.exp(t - m) for t in tiles]
            den = jnp.exp(sink - m) + jnp.sum(functools.reduce(jnp.add, es), axis=-1, keepdims=True)
            inv = 1.0 / den
            probs.append(jnp.concatenate([(e * inv).astype(BF16) for e in es], axis=1))
        o = _dot(jnp.concatenate(probs, axis=0), vh)
        half = n_pair * blk
        if h == 0:
            out = o[:half] + pltpu.roll(o[half:], HEAD_DIM, 1)
        else:
            out = pltpu.roll(o[:half], HEAD_DIM, 1) + o[half:]
        w0 = h * Q_PER_KV * HEAD_DIM
        for j in range(n_pair):
            o_ref[:, w0 + 128 * j:w0 + 128 * (j + 1)] = out[j * blk:(j + 1) * blk].astype(o_ref.dtype)


def _ctx_attn_kernel(sink_ref, q_ref, kc_ref, vc_ref, o_ref):
    kc = kc_ref[...].astype(BF16)
    vc = vc_ref[...].astype(BF16)
    scale = HEAD_DIM ** -0.5
    outs = []
    for pair in range(N_Q_HEADS // 2):
        q2 = (q_ref[:, pair * 128:(pair + 1) * 128] * scale).astype(BF16)
        for sub in range(2):
            head = 2 * pair + sub
            kvh = head // Q_PER_KV
            sl = slice(kvh * HEAD_DIM, (kvh + 1) * HEAD_DIM)
            qh = q2[:, sub * HEAD_DIM:(sub + 1) * HEAD_DIM]
            outs.append(_softmax_pv(qh, [kc[:, sl]], [vc[:, sl]], [None], sink_ref[head]))
    o_ref[...] = jnp.concatenate(outs, axis=-1).astype(o_ref.dtype)


def _attention(p, sink, cos_t, sin_t, with_ctx):
    blk = ATTN_BLOCK
    nb = SEQ // blk
    qc, kcol, vcol = PQ_OFF // ATTN_WIDTH, PK_OFF // KV_WIDTH, PV_OFF // KV_WIDTH
    ctx_blk = T_LAT // CTX_LEN
    smem = pl.BlockSpec(memory_space=pltpu.SMEM)

    def kv_spec(col, d):
        return pl.BlockSpec((blk, KV_WIDTH),
                            lambda b, n: (b * nb + jnp.clip(n + d, 0, nb - 1), col))

    ya = pl.pallas_call(
        functools.partial(_attn_kernel, nb=nb),
        grid=(BATCH, nb),
        in_specs=[smem,
                  pl.BlockSpec((blk, ATTN_WIDTH), lambda b, n: (b * nb + n, qc)),
                  kv_spec(kcol, -1), kv_spec(kcol, 0), kv_spec(kcol, 1),
                  kv_spec(vcol, -1), kv_spec(vcol, 0), kv_spec(vcol, 1),
                  pl.BlockSpec((CTX_LEN, KV_WIDTH), lambda b, n: (ctx_blk + b, kcol)),
                  pl.BlockSpec((CTX_LEN, KV_WIDTH), lambda b, n: (ctx_blk + b, vcol)),
                  pl.BlockSpec((SEQ, KV_WIDTH), lambda b, n: (0, 0)),
                  pl.BlockSpec((SEQ, KV_WIDTH), lambda b, n: (0, 0)),
                  pl.BlockSpec((128, 128), lambda b, n: (0, 0)),
                  pl.BlockSpec((128, 128), lambda b, n: (0, 0))],
        out_specs=pl.BlockSpec((blk, ATTN_WIDTH), lambda b, n: (b * nb + n, 0)),
        out_shape=jax.ShapeDtypeStruct((T_ALL, ATTN_WIDTH), BF16),
        compiler_params=_cparams(("parallel", "parallel")),
        name="banded_attention",
    )(sink, p, p, p, p, p, p, p, p, p, cos_t, sin_t, *_lane_perms())
    if not with_ctx:
        return ya
    cb = CTX_LEN // blk
    lat_blk = T_LAT // blk

    def alias_kernel(sink_ref, q_ref, kc_ref, vc_ref, ya_in_ref, o_ref):
        del ya_in_ref
        _ctx_attn_kernel(sink_ref, q_ref, kc_ref, vc_ref, o_ref)

    return pl.pallas_call(
        alias_kernel,
        grid=(BATCH, cb),
        in_specs=[smem,
                  pl.BlockSpec((blk, ATTN_WIDTH), lambda b, n: (lat_blk + b * cb + n, qc)),
                  pl.BlockSpec((CTX_LEN, KV_WIDTH), lambda b, n: (ctx_blk + b, kcol)),
                  pl.BlockSpec((CTX_LEN, KV_WIDTH), lambda b, n: (ctx_blk + b, vcol)),
                  pl.BlockSpec(memory_space=pl.ANY)],
        out_specs=pl.BlockSpec((blk, ATTN_WIDTH), lambda b, n: (lat_blk + b * cb + n, 0)),
        out_shape=jax.ShapeDtypeStruct((T_ALL, ATTN_WIDTH), BF16),
        input_output_aliases={4: 0},
        compiler_params=_cparams(("parallel", "parallel")),
        name="context_attention",
    )(sink, p, p, p, ya)


def _rope_tables():
    n_freq = HEAD_DIM // 4
    freqs = ROPE_BASE ** (-jnp.arange(n_freq, dtype=F32) / n_freq)
    t = jnp.arange(SEQ, dtype=jnp.int32)
    rows = (t // GRID_W).astype(F32)[:, None] * freqs
    cols = (t % GRID_W).astype(F32)[:, None] * freqs
    cos_h = jnp.concatenate([jnp.cos(rows), jnp.cos(rows), jnp.cos(cols), jnp.cos(cols)], axis=-1)
    sin_h = jnp.concatenate([-jnp.sin(rows), jnp.sin(rows), -jnp.sin(cols), jnp.sin(cols)], axis=-1)
    return jnp.tile(cos_h, (1, 2)), jnp.tile(sin_h, (1, 2))


def _lane_perms():
    j = jnp.arange(128, dtype=jnp.int32)[:, None]
    l = jnp.arange(128, dtype=jnp.int32)[None, :]
    rot = jnp.where(l % 32 < 16, l + 16, l - 16)
    return (j == rot).astype(BF16), (j == (l + HEAD_DIM) % 128).astype(BF16)


def _lead_kernel(*refs, n_in, cmul, epi):
    m_ref = refs[0]
    x_refs = refs[1:1 + n_in]
    pos = 1 + n_in
    xs = []
    for r in x_refs:
        v = r[...]
        xs.append(v.reshape(-1, v.shape[-1]))
    x = xs[0] if n_in == 1 else jnp.concatenate(xs, axis=0)
    if cmul:
        k = refs[pos][...]
        pos += 1
        k = k.reshape(-1, k.shape[-1])
        half = x.shape[0] // 2
        xr, xi, kr, ki = x[:half], x[half:], k[:half], k[half:]
        x = jnp.concatenate([xr * kr - xi * ki, xr * ki + xi * kr], axis=0)
    res = _dot(m_ref[...], x.astype(BF16))
    if epi:
        g_ref, y_ref, b_ref = refs[pos:pos + 3]
        pos += 3
        g = g_ref[...]
        y = y_ref[...]
        res = g.reshape(-1, g.shape[-1]) * (res + y.reshape(-1, y.shape[-1]) * b_ref[...])
    o_ref = refs[pos]
    o_ref[...] = res.reshape(o_ref.shape).astype(o_ref.dtype)


def _lead(mat, xs, x_specs, grid, out_shape, out_spec, *, kspec=None, epi=None, alias_to=None, name):
    ins = [mat] + list(xs)
    specs = [pl.BlockSpec(mat.shape, lambda *a: (0, 0))] + list(x_specs)
    if kspec is not None:
        ins.append(kspec[0])
        specs.append(kspec[1])
    if epi is not None:
        for arr, sp in epi:
            ins.append(arr)
            specs.append(sp)
    kern = functools.partial(_lead_kernel, n_in=len(xs), cmul=kspec is not None, epi=epi is not None)
    aliases = {}
    if alias_to is not None:
        aliases = {len(ins): 0}
        ins.append(alias_to)
        specs.append(pl.BlockSpec(memory_space=pl.ANY))
        inner = kern

        def kern(*refs):
            inner(*refs[:-2], refs[-1])

    return pl.pallas_call(
        kern, grid=grid, in_specs=specs, out_specs=out_spec, out_shape=out_shape,
        input_output_aliases=aliases,
        compiler_params=_cparams(("parallel",) * len(grid)), name=name,
    )(*ins)


def _cstack(xr, xi):
    return jnp.concatenate([jnp.concatenate([xr, xi], axis=1), jnp.concatenate([-xi, xr], axis=1)], axis=0)


def _fnet_kernel(u_ref, mc_ref, ml_ref, ms_ref, o_ref, zr, zi, are, aim):
    n = FN_N
    pitch = PITCH
    gd = FNET_GROUP_DIM
    rows = 4 * n
    mc = mc_ref[...]

    def chan(i, carry):
        r_in = pl.multiple_of(i * rows, rows)
        z = _dot(u_ref[pl.ds(r_in, rows), :].astype(BF16), mc)
        for q in range(rows // n):
            r_out = pl.multiple_of((i * (rows // n) + q) * pitch, 8)
            zr[pl.ds(r_out, n), :] = z[q * n:(q + 1) * n, :gd]
            zi[pl.ds(r_out, n), :] = z[q * n:(q + 1) * n, gd:]
        return carry

    lax.fori_loop(0, SEQ // rows, chan, 0)
    ml = ml_ref[...]

    def lead(i, carry):
        for u in range(UNROLL):
            n1 = i * UNROLL + u
            x = jnp.concatenate([zr[pl.ds(n1, n, stride=pitch), :], zi[pl.ds(n1, n, stride=pitch), :]],
                                axis=0).astype(BF16)
            r = _dot(ml, x)
            are[pl.ds(n1, n, stride=pitch), :] = r[:n]
            aim[pl.ds(n1, n, stride=pitch), :] = r[n:]
        return carry

    lax.fori_loop(0, n // UNROLL, lead, 0)

    def slab(i, carry):
        for u in range(UNROLL):
            k2 = i * UNROLL + u
            r0 = pl.multiple_of(k2 * pitch, 8)
            x = jnp.concatenate([are[pl.ds(r0, n), :], aim[pl.ds(r0, n), :]], axis=0).astype(BF16)
            o_ref[pl.ds(k2, n, stride=n), :] = _dot(ms_ref[k2], x)
        return carry

    lax.fori_loop(0, n // UNROLL, slab, 0)


def _fnet_ctx_kernel(u_ref, mc_ref, mf_ref, yf_in_ref, o_ref):
    del yf_in_ref
    gd = FNET_GROUP_DIM
    mc = mc_ref[...]
    mf = mf_ref[...]
    for g in range(FNET_GROUPS):
        z = _dot(u_ref[:, g * gd:(g + 1) * gd].astype(BF16), mc)
        x = jnp.concatenate([z[:, :gd], z[:, gd:]], axis=0).astype(BF16)
        o_ref[:, g * gd:(g + 1) * gd] = _dot(mf, x)


def _fourier_mix(p, mats, with_ctx):
    gd = FNET_GROUP_DIM
    col0 = PF_OFF // gd
    full = lambda a: pl.BlockSpec(a.shape, lambda *i: (0,) * a.ndim)
    scr = pltpu.VMEM((FN_N * PITCH, gd), F32)
    yf = pl.pallas_call(
        _fnet_kernel,
        grid=(BATCH, FNET_GROUPS),
        in_specs=[pl.BlockSpec((SEQ, gd), lambda b, g: (b, col0 + g)),
                  full(mats['fn_chan']), full(mats['fn_lead']), full(mats['fn_slab'])],
        out_specs=pl.BlockSpec((SEQ, gd), lambda b, g: (b, g)),
        out_shape=jax.ShapeDtypeStruct((T_ALL if with_ctx else T_LAT, FNET_WIDTH), F32),
        scratch_shapes=[scr, scr, scr, scr],
        compiler_params=_cparams(("parallel", "parallel")),
        name="fnet_latent",
    )(p, mats['fn_chan'], mats['fn_lead'], mats['fn_slab'])
    if not with_ctx:
        return yf
    blk0 = T_LAT // CTX_LEN
    return pl.pallas_call(
        _fnet_ctx_kernel,
        grid=(BATCH,),
        in_specs=[pl.BlockSpec((pl.Element(CTX_LEN), pl.Element(FNET_WIDTH)),
                               lambda b: ((blk0 + b) * CTX_LEN, PF_OFF)),
                  full(mats['fn_chan']), full(mats['fc']), pl.BlockSpec(memory_space=pl.ANY)],
        out_specs=pl.BlockSpec((CTX_LEN, FNET_WIDTH), lambda b: (blk0 + b, 0)),
        out_shape=jax.ShapeDtypeStruct((T_ALL, FNET_WIDTH), F32),
        input_output_aliases={3: 0},
        compiler_params=_cparams(("parallel",)),
        name="fnet_ctx",
    )(p, mats['fn_chan'], mats['fc'], yf)


def _short_conv_kernel(u_ref, w_ref, b_ref, o_ref, *, rows, chunk, padded):
    w0 = w_ref[0:1, :]
    w1 = w_ref[1:2, :]
    w2 = w_ref[2:3, :]
    bias = b_ref[...]
    width = u_ref.shape[-1]
    ridx = lax.broadcasted_iota(jnp.int32, (chunk, width), 0)
    n_chunks = rows // chunk
    for ci in range(n_chunks):
        r0 = ci * chunk
        cur = u_ref[r0:r0 + chunk, :].astype(F32)
        if ci > 0:
            prev_row = u_ref[r0 - 16:r0, :].astype(F32)[15:16, :]
        else:
            prev_row = jnp.zeros((1, width), F32)
        if ci < n_chunks - 1:
            next_row = u_ref[r0 + chunk:r0 + chunk + 16, :].astype(F32)[0:1, :]
        else:
            next_row = jnp.zeros((1, width), F32)
        up = jnp.where(ridx == 0, prev_row, pltpu.roll(cur, 1, 0))
        dn = jnp.where(ridx == chunk - 1, next_row, pltpu.roll(cur, chunk - 1, 0))
        res = up * w0 + cur * w1 + dn * w2 + bias
        if not padded:
            o_ref[0, r0:r0 + chunk, :] = res
            continue
        for q in range(chunk // HY_N2):
            p0 = (r0 // HY_N2 + q) * PITCH
            o_ref[0, p0:p0 + HY_N2, :] = res[q * HY_N2:(q + 1) * HY_N2]
            o_ref[0, p0 + HY_N2:p0 + PITCH, :] = jnp.zeros((PITCH - HY_N2, width), F32)


def _short_conv(p, conv_w, conv_b, with_ctx):
    cw = 256
    hw = (HYENA_ORDER + 1) * HYENA_WIDTH
    ncol = hw // cw
    per = HYENA_WIDTH // cw
    col0 = PH_OFF // cw
    out_shape = jax.ShapeDtypeStruct((HYENA_ORDER + 1, T_PAD if with_ctx else BATCH * SEQ_PAD, HYENA_WIDTH), F32)
    b2 = conv_b.reshape(1, hw)

    def call(rows, blk0, alias):
        padded = rows == SEQ
        out_rows = SEQ_PAD if padded else rows
        out_blk0 = 0 if padded else BATCH * SEQ_PAD // rows
        kern = functools.partial(_short_conv_kernel, rows=rows, chunk=min(rows, 256), padded=padded)
        ins = [p, conv_w, b2]
        specs = [pl.BlockSpec((rows, cw), lambda b, j: (blk0 + b, col0 + j)),
                 pl.BlockSpec((3, cw), lambda b, j: (0, j)),
                 pl.BlockSpec((1, cw), lambda b, j: (0, j))]
        aliases = {}
        if alias is not None:
            ins.append(alias)
            specs.append(pl.BlockSpec(memory_space=pl.ANY))
            aliases = {3: 0}
            inner = kern

            def kern(u_ref, w_ref, b_ref, a_ref, o_ref):
                del a_ref
                inner(u_ref, w_ref, b_ref, o_ref)

        return pl.pallas_call(
            kern, grid=(BATCH, ncol), in_specs=specs,
            out_specs=pl.BlockSpec((1, out_rows, cw), lambda b, j: (j // per, out_blk0 + b, j % per)),
            out_shape=out_shape, input_output_aliases=aliases,
            compiler_params=_cparams(("parallel", "parallel")), name="hyena_short_conv",
        )(*ins)

    z3 = call(SEQ, 0, None)
    if with_ctx:
        z3 = call(CTX_LEN, T_LAT // CTX_LEN, z3)
    return z3


def _filter_mlp_kernel(ft_ref, w1_ref, b1_ref, fq_ref, w2_ref, b2_ref, o_ref):
    fq = fq_ref[...]
    h = jnp.sin(fq * (_dot(ft_ref[...].astype(BF16), w1_ref[...]) + b1_ref[...]))
    h = jnp.sin(fq * (_dot(h.astype(BF16), w2_ref[...]) + b2_ref[...]))
    o_ref[...] = h.astype(o_ref.dtype)


def _filter_kernel(h_ref, t_ref, w3f_ref, w3b_ref, dl_ref, m1_ref, *rest, n, dense):
    hb = h_ref[...]
    decay = jnp.exp(-t_ref[...] * dl_ref[...])
    tf = _dot(hb, w3f_ref[...]) * decay
    tb = _dot(hb, w3b_ref[...]) * decay
    tb = jnp.where(lax.broadcasted_iota(jnp.int32, tb.shape, 0) == 0, 0.0, tb)
    scale = 1.0 / (jnp.sum(jnp.abs(tf), axis=0, keepdims=True) + jnp.sum(jnp.abs(tb), axis=0, keepdims=True))
    cw = tf.shape[1]
    if dense:
        o_ref = rest[0]
        r = _dot(m1_ref[...], jnp.concatenate([tf, tb], axis=1).astype(BF16))
        nc = r.shape[0] // 2
        o_ref[0:nc, :] = (r[:nc, :cw] + r[:nc, cw:]) * scale
        o_ref[nc:, :] = (r[nc:, :cw] - r[nc:, cw:]) * scale
        return
    f2_ref, o_ref, tf_s, tb_s, are_f, aim_f, are_b, aim_b = rest
    tf_s[...] = tf
    tb_s[...] = tb
    half = HY_N1 // 2
    m1 = m1_ref[...]

    def stage1(i, carry):
        for u in range(UNROLL):
            n2 = i * UNROLL + u
            x = jnp.concatenate([tf_s[pl.ds(n2, half, stride=HY_N2), :], tb_s[pl.ds(n2, half, stride=HY_N2), :]],
                                axis=1).astype(BF16)
            r = _dot(m1, x)
            rows = pl.ds(n2, HY_N1, stride=PITCH)
            are_f[rows, :] = r[:HY_N1, :cw]
            are_b[rows, :] = r[:HY_N1, cw:]
            aim_f[rows, :] = r[HY_N1:, :cw]
            aim_b[rows, :] = r[HY_N1:, cw:]
        return carry

    lax.fori_loop(0, HY_N2 // UNROLL, stage1, 0)

    def stage2(i, carry):
        for u in range(UNROLL):
            k1 = i * UNROLL + u
            rows = pl.ds(pl.multiple_of(k1 * PITCH, 8), HY_N2)
            ar = jnp.concatenate([are_f[rows, :], are_b[rows, :]], axis=1)
            ai = jnp.concatenate([aim_f[rows, :], aim_b[rows, :]], axis=1)
            t = _dot(f2_ref[k1], _cstack(ar, ai).astype(BF16))
            o_ref[0, k1] = (t[:, 0:cw] + t[:, cw:2 * cw]) * scale
            o_ref[1, k1] = (t[:, 2 * cw:3 * cw] - t[:, 3 * cw:4 * cw]) * scale
        return carry

    lax.fori_loop(0, HY_N1 // UNROLL, stage2, 0)


def _filter_feats(n):
    pos = jnp.arange(n, dtype=F32)
    t = pos / max(n - 1, 1)
    omega = 2.0 * math.pi * pos / n
    bands = jnp.linspace(1e-4, FILTER_BANDS - 1, FILTER_BANDS, dtype=F32)
    feats = jnp.concatenate([t[:, None], jnp.cos(omega[:, None] * bands), -jnp.sin(omega[:, None] * bands)], axis=-1)
    return jnp.pad(feats, ((0, 0), (0, 128 - FILTER_EMB))), t[:, None]


def _filter_spectrum(n, filt, mats):
    w1, b1, freq, w2, b2, w3 = filt
    dense = n == CTX_LEN
    cw = 128
    nch = HYENA_WIDTH // cw
    feats, t = _filter_feats(n)
    w1p = jnp.pad(w1, ((0, 128 - FILTER_EMB), (0, 0))).astype(BF16)
    deltas = jnp.abs(jnp.linspace(math.log(DECAY_TARGET) / SLOW_DECAY_PCT, math.log(DECAY_TARGET) / FAST_DECAY_PCT,
                                  HYENA_WIDTH, dtype=F32)).reshape(1, HYENA_WIDTH)
    full = lambda a: pl.BlockSpec(a.shape, lambda *i: (0,) * a.ndim)
    row = lambda a: a.reshape(1, -1)
    w3b16 = w3.astype(BF16)
    tap_spec = lambda d: pl.BlockSpec((FILTER_HIDDEN, cw), lambda o, ch: (0, (o * 2 + d) * nch + ch))
    m1 = mats['hc_kh'] if dense else mats['hy_lead_kh']
    mlp_ins = [feats, w1p, row(b1), row(freq), w2.astype(BF16), row(b2)]
    hb = pl.pallas_call(
        _filter_mlp_kernel, grid=(1,), in_specs=[full(a) for a in mlp_ins],
        out_specs=pl.BlockSpec((n, FILTER_HIDDEN), lambda i: (0, 0)),
        out_shape=jax.ShapeDtypeStruct((n, FILTER_HIDDEN), BF16),
        compiler_params=_cparams(("arbitrary",)), name="hyena_filter_mlp",
    )(*mlp_ins)
    ins = [hb, t, w3b16, w3b16, deltas, m1]
    specs = [full(hb), full(t), tap_spec(0), tap_spec(1), pl.BlockSpec((1, cw), lambda o, ch: (0, ch)), full(m1)]
    if dense:
        nc = 2 * n
        out_shape = jax.ShapeDtypeStruct((HYENA_ORDER, 2 * nc, HYENA_WIDTH), F32)
        out_spec = pl.BlockSpec((None, 2 * nc, cw), lambda o, ch: (o, 0, ch))
        scratch = []
    else:
        ins.append(mats['hy_slab_f2'])
        specs.append(full(mats['hy_slab_f2']))
        out_shape = jax.ShapeDtypeStruct((HYENA_ORDER, 2, HY_N1, HY_N2, HYENA_WIDTH), F32)
        out_spec = pl.BlockSpec((None, 2, HY_N1, HY_N2, cw), lambda o, ch: (o, 0, 0, 0, ch))
        scratch = [pltpu.VMEM((n, cw), F32)] * 2 + [pltpu.VMEM((HY_N1 * PITCH, cw), F32)] * 4
    return pl.pallas_call(
        functools.partial(_filter_kernel, n=n, dense=dense),
        grid=(HYENA_ORDER, nch), in_specs=specs, out_specs=out_spec, out_shape=out_shape,
        scratch_shapes=scratch,
        compiler_params=_cparams(("parallel", "parallel")),
        name="hyena_filter_ctx" if dense else "hyena_filter",
    )(*ins)


def _hyena_conv_kernel(y_ref, g_ref, k_ref, m1_ref, f2f_ref, f2i_ref, m3_ref, b_ref, o_ref, are, aim, *, out_pitch):
    half = HY_N1 // 2
    out_seq = half * out_pitch
    m1 = m1_ref[...]

    def stage1(i, carry):
        for u in range(UNROLL):
            n2 = i * UNROLL + u
            x = jnp.concatenate([y_ref[pl.ds(n2, half, stride=PITCH), :],
                                 y_ref[pl.ds(SEQ_PAD + n2, half, stride=PITCH), :]], axis=0).astype(BF16)
            r = _dot(m1, x)
            are[pl.ds(n2, HY_N1, stride=PITCH), :] = r[:HY_N1]
            aim[pl.ds(n2, HY_N1, stride=PITCH), :] = r[HY_N1:]
        return carry

    lax.fori_loop(0, HY_N2 // UNROLL, stage1, 0)
    cw = o_ref.shape[-1]

    def stage2(i, carry):
        for u in range(UNROLL):
            k1 = i * UNROLL + u
            r0 = pl.multiple_of(k1 * PITCH, 8)
            y = _dot(f2f_ref[k1], _cstack(are[pl.ds(r0, HY_N2), :], aim[pl.ds(r0, HY_N2), :]).astype(BF16))
            yr, yi = y[:, :cw], y[:, cw:]
            kr, ki = k_ref[0, k1], k_ref[1, k1]
            w = _dot(f2i_ref[k1], _cstack(yr * kr - yi * ki, yr * ki + yi * kr).astype(BF16))
            are[pl.ds(r0, HY_N2), :] = w[:, :cw]
            aim[pl.ds(r0, HY_N2), :] = w[:, cw:]
        return carry

    lax.fori_loop(0, HY_N1 // UNROLL, stage2, 0)
    m3 = m3_ref[...]
    bias = b_ref[...]

    def stage3(i, carry):
        for u in range(UNROLL):
            n2 = i * UNROLL + u
            bn = jnp.concatenate([are[pl.ds(n2, HY_N1, stride=PITCH), :], aim[pl.ds(n2, HY_N1, stride=PITCH), :]],
                                 axis=0).astype(BF16)
            y = _dot(m3, bn)
            for b in range(2):
                rows = pl.ds(b * SEQ_PAD + n2, half, stride=PITCH)
                o_ref[pl.ds(b * out_seq + n2, half, stride=out_pitch), :] = (
                    g_ref[rows, :] * (y[b * half:(b + 1) * half] + y_ref[rows, :] * bias))
        return carry

    lax.fori_loop(0, HY_N2 // UNROLL, stage3, 0)


def _hyena_mix(z3, filt, hyena_bias, mats, with_ctx):
    c = HYENA_WIDTH
    cw = 128
    nch = c // cw
    pairs = BATCH // 2
    full = lambda a: pl.BlockSpec(a.shape, lambda *i: (0,) * a.ndim)
    kspec = _filter_spectrum(SEQ, filt, mats)
    if with_ctx:
        kspec_c = _filter_spectrum(CTX_LEN, filt, mats)
    bias3 = hyena_bias.reshape(HYENA_ORDER, 1, c)
    scr = pltpu.VMEM((HY_N1 * PITCH, cw), F32)
    y = None
    for o in range(HYENA_ORDER):
        final = o == HYENA_ORDER - 1
        out_pitch = HY_N2 if final else PITCH
        out_seq = SEQ if final else SEQ_PAD
        if final:
            out_rows = T_ALL if with_ctx else T_LAT
        else:
            out_rows = T_PAD if with_ctx else BATCH * SEQ_PAD
        if y is None:
            xin, xspec = z3, pl.BlockSpec((None, 2 * SEQ_PAD, cw), lambda b, ch: (0, b, ch))
        else:
            xin, xspec = y, pl.BlockSpec((2 * SEQ_PAD, cw), lambda b, ch: (b, ch))
        ynew = pl.pallas_call(
            functools.partial(_hyena_conv_kernel, out_pitch=out_pitch),
            grid=(pairs, nch),
            in_specs=[xspec,
                      pl.BlockSpec((None, 2 * SEQ_PAD, cw), lambda b, ch, o=o: (o + 1, b, ch)),
                      pl.BlockSpec((None, 2, HY_N1, HY_N2, cw), lambda b, ch, o=o: (o, 0, 0, 0, ch)),
                      full(mats['hy_lead_f']), full(mats['hy_slab_f2']), full(mats['hy_slab_i2']),
                      full(mats['hy_lead_i']),
                      pl.BlockSpec((None, 1, cw), lambda b, ch, o=o: (o, 0, ch))],
            out_specs=pl.BlockSpec((2 * out_seq, cw), lambda b, ch: (b, ch)),
            out_shape=jax.ShapeDtypeStruct((out_rows, c), F32),
            scratch_shapes=[scr, scr],
            compiler_params=_cparams(("parallel", "parallel"), VMEM_LIMIT_CONV),
            name="hyena_conv",
        )(xin, z3, kspec, mats['hy_lead_f'], mats['hy_slab_f2'], mats['hy_slab_i2'], mats['hy_lead_i'], bias3)
        if with_ctx:
            nc = 2 * CTX_LEN
            blk0 = BATCH * SEQ_PAD // nc
            oblk0 = BATCH * out_seq // nc
            if y is None:
                cin, cspec_in = z3, pl.BlockSpec((1, nc, c), lambda b: (0, blk0 + b, 0))
            else:
                cin, cspec_in = y, pl.BlockSpec((nc, c), lambda b: (blk0 + b, 0))
            xc = _lead(mats['hc_f'], [cin], [cspec_in], (pairs,),
                       jax.ShapeDtypeStruct((pairs, 2 * nc, c), F32),
                       pl.BlockSpec((1, 2 * nc, c), lambda b: (b, 0, 0)), name="hyena_ctx_fwd")
            ynew = _lead(mats['hc_i'], [xc], [pl.BlockSpec((1, 2 * nc, c), lambda b: (b, 0, 0))],
                         (pairs,), jax.ShapeDtypeStruct((out_rows, c), F32),
                         pl.BlockSpec((nc, c), lambda b, oblk0=oblk0: (oblk0 + b, 0)),
                         kspec=(kspec_c, pl.BlockSpec((1, 2 * nc, c), lambda b, o=o: (o, 0, 0))),
                         epi=[(z3, pl.BlockSpec((1, nc, c), lambda b, o=o: (o + 1, blk0 + b, 0))),
                              (cin, cspec_in),
                              (bias3, pl.BlockSpec((1, 1, c), lambda b, o=o: (o, 0, 0)))],
                         alias_to=ynew, name="hyena_ctx_inv")
        y = ynew
    return y


def _route_tile(lt, br, base, tri):
    tm = lt.shape[1]
    aff = jax.nn.sigmoid(lt)
    biased = aff + br
    b = [biased[e:e + 1, :] for e in range(N_EXPERTS)]
    a = [aff[e:e + 1, :] for e in range(N_EXPERTS)]
    epg = EXPERTS_PER_GROUP
    scores = []
    for g in range(N_GROUPS):
        x0, x1, x2, x3 = b[epg * g:epg * g + epg]
        s1, t1 = jnp.maximum(x0, x1), jnp.minimum(x0, x1)
        s2, t2 = jnp.maximum(x2, x3), jnp.minimum(x2, x3)
        scores.append(jnp.maximum(s1, s2) + jnp.maximum(jnp.minimum(s1, s2), jnp.maximum(t1, t2)))
    best = scores[0]
    gsel = jnp.zeros((1, tm), jnp.int32)
    for g in range(1, N_GROUPS):
        gsel = jnp.where(scores[g] > best, g, gsel)
        best = jnp.maximum(best, scores[g])

    def pick(rows, j):
        out = rows[j]
        for g in range(1, N_GROUPS):
            out = jnp.where(gsel == g, rows[epg * g + j], out)
        return out

    v = [pick(b, j) for j in range(epg)]
    av = [pick(a, j) for j in range(epg)]
    i1 = jnp.zeros((1, tm), jnp.int32)
    m1 = v[0]
    for j in range(1, epg):
        i1 = jnp.where(v[j] > m1, j, i1)
        m1 = jnp.maximum(m1, v[j])
    neg = jnp.float32(-3.0e38)
    i2 = jnp.zeros((1, tm), jnp.int32)
    m2 = jnp.full((1, tm), neg, F32)
    for j in range(epg):
        cand = jnp.where(i1 == j, neg, v[j])
        take = cand > m2
        i2 = jnp.where(take, j, i2)
        m2 = jnp.where(take, cand, m2)

    def sel(rows, idx):
        out = rows[0]
        for j in range(1, epg):
            out = jnp.where(idx == j, rows[j], out)
        return out

    a1, a2 = sel(av, i1), sel(av, i2)
    den = a1 + a2
    e1 = gsel * epg + i1
    e2 = gsel * epg + i2
    eio = lax.broadcasted_iota(jnp.int32, (N_EXPERTS, tm), 0)
    oh1 = jnp.where(eio == e1, 1.0, 0.0)
    oh2 = jnp.where(eio == e2, 1.0, 0.0)
    oh = oh1 + oh2
    tot = base + _dot(oh.astype(BF16), tri)
    r1 = jnp.sum(oh1 * tot, axis=0, keepdims=True)
    r2 = jnp.sum(oh2 * tot, axis=0, keepdims=True)
    new_base = base + jnp.sum(oh, axis=1, keepdims=True)
    return (e1, e2), (a1 / den, a2 / den), (r1.astype(jnp.int32), r2.astype(jnp.int32)), new_base


def _merge_kernel(x_ref, ya_ref, yf_ref, yh_ref, gt_ref, wa_ref, wf_ref, wh_ref, wo_ref, g1_ref,
                  gn_ref, sh_ref, sc_ref, wrt_ref, br_ref, xo_ref, h2_ref, e_ref, w_ref, r_ref, cnt_ref):
    d = D_MODEL
    gate = lambda k: jax.nn.sigmoid(gt_ref[:, k * d:(k + 1) * d].astype(F32))
    merged = gate(0) * _dot(ya_ref[...], wa_ref[...])
    merged += gate(1) * _dot(yf_ref[...].astype(BF16), wf_ref[...])
    merged += gate(2) * _dot(yh_ref[...].astype(BF16), wh_ref[...])
    xn = x_ref[...] + g1_ref[0, 0] * _dot(merged.astype(BF16), wo_ref[...])
    xo_ref[...] = xn
    h2f = _rms_mod(xn, gn_ref[...], sh_ref[0, 0], sc_ref[0, 0])
    _rows_to_tiles(h2_ref, h2f, 0, h2f.shape[0])
    h2 = h2f.astype(BF16)

    @pl.when(pl.program_id(0) == 0)
    def _():
        cnt_ref[...] = jnp.zeros_like(cnt_ref)

    tm = h2.shape[0]
    lt = lax.dot_general(wrt_ref[...], h2, (((1,), (1,)), ((), ())), preferred_element_type=F32)
    tri = jnp.where(lax.broadcasted_iota(jnp.int32, (tm, tm), 0) < lax.broadcasted_iota(jnp.int32, (tm, tm), 1),
                    1.0, 0.0).astype(BF16)
    es, ws, rs, new_base = _route_tile(lt, br_ref[...], cnt_ref[:, 0:1], tri)
    e_ref[0:1, :], e_ref[1:2, :] = es
    w_ref[0:1, :], w_ref[1:2, :] = ws
    r_ref[0:1, :], r_ref[1:2, :] = rs
    cnt_ref[...] = jnp.broadcast_to(new_base, cnt_ref.shape)


def _merge(x, ya, yf, yh, p, wa, wf, wh, wo, mod4, gain2, wrt, br, n_tok):
    tm = TM
    row = _mod_row(tm)
    full = lambda a: pl.BlockSpec(a.shape, lambda i: (0,) * a.ndim)
    modspec = lambda k: pl.BlockSpec((1, 1, 1, D_MODEL), lambda i: (row(i), k, 0, 0))
    tok = lambda w: pl.BlockSpec((tm, w), lambda i: (i, 0))
    lane = pl.BlockSpec((TOP_K, tm), lambda i: (0, i))
    return pl.pallas_call(
        _merge_kernel,
        grid=(n_tok // tm,),
        in_specs=[tok(D_MODEL), tok(ATTN_WIDTH), tok(FNET_WIDTH), tok(HYENA_WIDTH),
                  pl.BlockSpec((pl.Element(tm), pl.Element(3 * D_MODEL)), lambda i: (i * tm, PG_OFF)),
                  full(wa), full(wf), full(wh), full(wo), modspec(2), full(gain2), modspec(3), modspec(4),
                  full(wrt), full(br)],
        out_specs=[tok(D_MODEL), pl.BlockSpec((tm * ROW_TILE, 128), lambda i: (i, 0)), lane, lane, lane,
                   pl.BlockSpec((N_EXPERTS, 128), lambda i: (0, 0))],
        out_shape=[jax.ShapeDtypeStruct((n_tok, D_MODEL), F32),
                   jax.ShapeDtypeStruct((n_tok * ROW_TILE, 128), F32),
                   jax.ShapeDtypeStruct((TOP_K, n_tok), jnp.int32),
                   jax.ShapeDtypeStruct((TOP_K, n_tok), F32),
                   jax.ShapeDtypeStruct((TOP_K, n_tok), jnp.int32),
                   jax.ShapeDtypeStruct((N_EXPERTS, 128), F32)],
        compiler_params=_cparams(("arbitrary",)),
        name="merge_out_norm_route",
    )(x, ya, yf, yh, p, wa, wf, wh, wo, mod4, gain2, mod4, mod4, wrt, br)


ROW_TILE = D_MODEL // 128


def _rows_from_tiles(ref, n):
    return jnp.concatenate([ref[pl.ds(s, n, stride=ROW_TILE), :] for s in range(ROW_TILE)], axis=1)


def _rows_to_tiles(ref, val, col0, n):
    for j in range(val.shape[1] // 128):
        ref[pl.ds(col0 // 128 + j, n, stride=ROW_TILE), :] = val[:, j * 128:(j + 1) * 128]


def _moe_kernel(be_ref, na_ref, ip_ref, ic_ref, sc_ref, sn_ref, h2_ref, wg_ref, wu_ref, wd_ref, y_ref,
                xbuf, obuf, gsem, ssem, wg_s, wu_s, wd_s):
    i = pl.program_id(0)
    n_act = na_ref[0]
    sub = ROW_TILE
    bm = xbuf.shape[1] // sub
    active = i < n_act
    slot = i % 2
    other = (i + 1) % 2
    n_chunk = 4
    cw = EXPERT_FF // n_chunk
    rows_per = bm // n_chunk

    def tile_rows(ref, row):
        return ref.at[pl.ds(pl.multiple_of(row * sub, sub), sub), :]

    def gather(src_ref, dst_slot, lo, hi):
        for r in range(lo, hi):
            pltpu.make_async_copy(tile_rows(h2_ref, src_ref[0, 0, r]), xbuf.at[dst_slot, pl.ds(r * sub, sub), :],
                                  gsem.at[dst_slot]).start(priority=r % 2)

    def scatter(info_ref, src_slot, lo, hi):
        for r in range(lo, hi):
            pltpu.make_async_copy(obuf.at[src_slot, pl.ds(r * sub, sub), :], tile_rows(y_ref, info_ref[0, 0, r]),
                                  ssem.at[src_slot]).start(priority=r % 2)

    def wait_gather(s):
        pltpu.make_async_copy(h2_ref.at[pl.ds(0, bm * sub), :], xbuf.at[s], gsem.at[s]).wait()

    def wait_scatter(s):
        pltpu.make_async_copy(obuf.at[s], y_ref.at[pl.ds(0, bm * sub), :], ssem.at[s]).wait()

    @pl.when((i == 0) & active)
    def _():
        obuf[...] = jnp.zeros_like(obuf)
        gather(sc_ref, 0, 0, bm)

    prev = be_ref[jnp.maximum(i - 1, 0)]

    @pl.when(active & ((i == 0) | (be_ref[i] != prev)))
    def _():
        wg_s[...] = wg_ref[...].astype(BF16)
        wu_s[...] = wu_ref[...].astype(BF16)
        wd_s[...] = wd_ref[...].astype(BF16)

    @pl.when(active)
    def _():
        wait_gather(slot)
        x = _rows_from_tiles(xbuf.at[slot], bm).astype(BF16)
        hs = []
        for c in range(n_chunk):
            g = _dot(x, wg_s[:, c * cw:(c + 1) * cw])
            u = _dot(x, wu_s[:, c * cw:(c + 1) * cw])
            hs.append(((g * jax.nn.sigmoid(g)) * u).astype(BF16))
            gather(sn_ref, other, c * rows_per, (c + 1) * rows_per)
        h = jnp.concatenate(hs, axis=1)

        @pl.when(i > 0)
        def _():
            wait_scatter(slot)

        for c in range(n_chunk):
            _rows_to_tiles(obuf.at[slot], _dot(h, wd_s[:, c * cw:(c + 1) * cw]), c * cw, bm)
            scatter(ip_ref, other, c * rows_per, (c + 1) * rows_per)

    @pl.when(i == n_act - 1)
    def _():
        wait_gather(other)
        wait_scatter(other)
        scatter(ic_ref, slot, 0, bm)
        wait_scatter(slot)


def _moe(h2, info, blk_expert, n_active, wg, wu, wd, layer, n_tok):
    bm = EXPERT_BM
    n_rows = info.shape[0]
    n_blk = n_rows // bm
    info3 = info.reshape(n_blk, 1, bm)
    src3 = jnp.where(info3 >= TOP_K * n_tok, 0, info3 % n_tok)
    wspec = lambda k, n: pl.BlockSpec((None, None, k, n), lambda i, be, na: (layer, be[i], 0, 0))
    ispec = lambda f: pl.BlockSpec((1, 1, bm), lambda i, be, na: (f(i), 0, 0), memory_space=pltpu.SMEM)
    return pl.pallas_call(
        _moe_kernel,
        grid_spec=pltpu.PrefetchScalarGridSpec(
            num_scalar_prefetch=2,
            grid=(n_blk,),
            in_specs=[ispec(lambda i: jnp.maximum(i - 1, 0)), ispec(lambda i: i),
                      ispec(lambda i: i), ispec(lambda i: jnp.minimum(i + 1, n_blk - 1)),
                      pl.BlockSpec(memory_space=pl.ANY),
                      wspec(D_MODEL, EXPERT_FF), wspec(D_MODEL, EXPERT_FF), wspec(EXPERT_FF, D_MODEL)],
            out_specs=pl.BlockSpec(memory_space=pl.ANY),
            scratch_shapes=[pltpu.VMEM((2, bm * ROW_TILE, 128), F32), pltpu.VMEM((2, bm * ROW_TILE, 128), F32),
                            pltpu.SemaphoreType.DMA((2,)), pltpu.SemaphoreType.DMA((2,)),
                            pltpu.VMEM((D_MODEL, EXPERT_FF), BF16), pltpu.VMEM((D_MODEL, EXPERT_FF), BF16),
                            pltpu.VMEM((EXPERT_FF, D_MODEL), BF16)]),
        out_shape=jax.ShapeDtypeStruct(((TOP_K * n_tok + n_rows) * ROW_TILE, 128), F32),
        compiler_params=_cparams(("arbitrary",)),
        name="moe_experts",
    )(blk_expert, n_active, info3, info3, src3, src3, h2, wg, wu, wd)


def _dispatch_info(e_idx, rank, counts, n_tok):
    bm = EXPERT_BM
    counts = counts.astype(jnp.int32)
    padded = (counts + bm - 1) // bm * bm
    pad_end = jnp.cumsum(padded)
    pad_start = pad_end - padded
    experts = jnp.arange(N_EXPERTS, dtype=jnp.int32)
    start = jnp.sum(jnp.where(e_idx[..., None] == experts, pad_start, 0), axis=-1)
    dest = start + rank
    n_rows = -(-(n_tok * TOP_K) // bm) * bm + N_EXPERTS * bm
    n_blk = n_rows // bm
    spill = TOP_K * n_tok + jnp.arange(n_rows, dtype=jnp.int32)
    info = spill.at[dest.reshape(-1)].set(jnp.arange(TOP_K * n_tok, dtype=jnp.int32),
                                          unique_indices=True, mode='promise_in_bounds')
    blk_start = jnp.arange(n_blk, dtype=jnp.int32) * bm
    blk_expert = jnp.minimum(jnp.sum((blk_start[:, None] >= pad_end[None, :]).astype(jnp.int32), axis=-1),
                             N_EXPERTS - 1)
    n_active = (pad_end[-1] // bm).astype(jnp.int32).reshape(1)
    return info, blk_expert, n_active


def _residual_kernel(x_ref, y0_ref, y1_ref, w_ref, g2_ref, gf_ref, o_ref, *, final):
    w = w_ref[...]
    tm = x_ref.shape[0]
    moe = _rows_from_tiles(y0_ref, tm) * w[:, 0:1] + _rows_from_tiles(y1_ref, tm) * w[:, 1:2]
    xn = x_ref[...] + g2_ref[0, 0] * moe
    if final:
        y = xn * lax.rsqrt(jnp.mean(xn * xn, axis=-1, keepdims=True) + EPS)
        xn = y * gf_ref[...]
    o_ref[...] = xn


def _residual(x, y, w_sel, mod4, gain_final, n_tok, final):
    tm = TM
    n_tiles = n_tok // tm
    row = _mod_row(tm)
    tok = lambda w: pl.BlockSpec((tm, w), lambda i: (i, 0))
    return pl.pallas_call(
        functools.partial(_residual_kernel, final=final),
        grid=(n_tiles,),
        in_specs=[tok(D_MODEL), pl.BlockSpec((tm * ROW_TILE, 128), lambda i: (i, 0)),
                  pl.BlockSpec((tm * ROW_TILE, 128), lambda i: (n_tiles + i, 0)), tok(TOP_K),
                  pl.BlockSpec((1, 1, 1, D_MODEL), lambda i: (row(i), 5, 0, 0)),
                  pl.BlockSpec((1, D_MODEL), lambda i: (0, 0))],
        out_specs=tok(D_MODEL),
        out_shape=jax.ShapeDtypeStruct((n_tok if final else T_ALL, D_MODEL), F32),
        compiler_params=_cparams(("parallel",)),
        name="moe_residual",
    )(x, y, y, w_sel, mod4, gain_final)


def kernel(x, c, ctx, c_ctx, w_mod, b_mod, norm_mix, norm_ffn, w_in, attn_sink, conv_w, conv_b, filt_w1, filt_b1, filt_freq, filt_w2, filt_b2, filt_w3, hyena_bias, w_branch_attn, w_branch_fnet, w_branch_hyena, w_out, w_router, b_router, w_exp_gate, w_exp_up, w_exp_down, norm_final):
    mats = _dft_mats()
    cos_t, sin_t = _rope_tables()
    c8 = jnp.concatenate([c, c_ctx[None, :], jnp.zeros((8 - BATCH - 1, D_MODEL), F32)], axis=0)
    mod_all = _modulation(c8, w_mod, b_mod)
    xa = jnp.concatenate([x.reshape(T_LAT, D_MODEL), ctx.reshape(T_CTX, D_MODEL)], axis=0)
    wrt = w_router.T.astype(BF16)
    br = b_router.astype(F32).reshape(N_EXPERTS, 1)
    gain_final = norm_final.reshape(1, D_MODEL)
    out = None
    for l in range(DEPTH):
        last = l == DEPTH - 1
        with_ctx = not last
        n_tok = T_LAT if last else T_ALL
        mod4 = mod_all[l].reshape(8, N_MOD, 1, D_MODEL)
        p = _norm_proj(xa, norm_mix[l].reshape(1, D_MODEL), mod4, w_in, l, T_ALL)
        ya = _attention(p, attn_sink[l], cos_t, sin_t, with_ctx)
        yf = _fourier_mix(p, mats, with_ctx)
        z3 = _short_conv(p, conv_w[l], conv_b[l], with_ctx)
        filt = (filt_w1[l], filt_b1[l], filt_freq[l], filt_w2[l], filt_b2[l], filt_w3[l])
        yh = _hyena_mix(z3, filt, hyena_bias[l], mats, with_ctx)
        xa, h2, e_idx, w_sel, rank, cnt = _merge(
            xa, ya, yf, yh, p, w_branch_attn[l].astype(BF16), w_branch_fnet[l].astype(BF16),
            w_branch_hyena[l].astype(BF16), w_out[l].astype(BF16), mod4,
            norm_ffn[l].reshape(1, D_MODEL), wrt, br, n_tok)
        info, blk_expert, n_active = _dispatch_info(e_idx, rank, cnt[:, 0], n_tok)
        y = _moe(h2, info, blk_expert, n_active, w_exp_gate, w_exp_up, w_exp_down, l, n_tok)
        res = _residual(xa, y, w_sel.T, mod4, gain_final, n_tok, last)
        if last:
            out = res
        else:
            xa = res
    return out.reshape(BATCH, SEQ, D_MODEL)
```

```python
import functools
import math

import jax
import jax.numpy as jnp
from jax import lax
from jax.experimental import pallas as pl
from jax.experimental.pallas import tpu as pltpu

F32 = jnp.float32
BF16 = jnp.bfloat16

D_MODEL = 1024
BATCH = 4
SEQ = 4096
DEPTH = 4
GRID_W = 64
CTX_LEN = 256
EPS = 1e-6
N_MOD = 6

HEAD_DIM = 64
N_Q_HEADS = 8
N_KV_HEADS = 2
Q_PER_KV = N_Q_HEADS // N_KV_HEADS
ATTN_BLOCK = 128
ROPE_BASE = 10000.0

FNET_GROUPS = 4
FNET_GROUP_DIM = 128
FNET_WIDTH = FNET_GROUPS * FNET_GROUP_DIM

HYENA_WIDTH = 512
HYENA_ORDER = 2
FILTER_EMB = 33
FILTER_BANDS = (FILTER_EMB - 1) // 2
FILTER_HIDDEN = 64
DECAY_TARGET = 1e-2
FAST_DECAY_PCT = 0.3
SLOW_DECAY_PCT = 1.5

ATTN_WIDTH = N_Q_HEADS * HEAD_DIM
KV_WIDTH = N_KV_HEADS * HEAD_DIM
Q_OFF = 0
K_OFF = Q_OFF + ATTN_WIDTH
V_OFF = K_OFF + KV_WIDTH
F_OFF = V_OFF + KV_WIDTH
H_OFF = F_OFF + FNET_WIDTH
G_OFF = H_OFF + (HYENA_ORDER + 1) * HYENA_WIDTH
IN_WIDTH = G_OFF + 3 * D_MODEL

N_EXPERTS = 16
N_GROUPS = 4
EXPERTS_PER_GROUP = N_EXPERTS // N_GROUPS
TOP_K = 2
EXPERT_FF = 1024

T_LAT = BATCH * SEQ
T_CTX = BATCH * CTX_LEN
T_ALL = T_LAT + T_CTX

PG_OFF = G_OFF
PH_OFF = H_OFF
PF_OFF = F_OFF
PQ_OFF = Q_OFF
PK_OFF = K_OFF
PV_OFF = V_OFF

HY_N = 2 * SEQ
HY_N2 = 64
HY_N1 = HY_N // HY_N2
FN_N = 64

PITCH = HY_N2 + 8
SEQ_PAD = SEQ // HY_N2 * PITCH
T_PAD = BATCH * SEQ_PAD + T_CTX
UNROLL = 32
TM = 512
EXPERT_BM = 256
VMEM_LIMIT = 52 * 1024 * 1024
VMEM_LIMIT_CONV = 58 * 1024 * 1024


def _cparams(sem, vmem=VMEM_LIMIT):
    return pltpu.CompilerParams(dimension_semantics=sem, vmem_limit_bytes=vmem)


def _dot(a, b):
    return jnp.dot(a, b, preferred_element_type=F32)


def _cis(expo, n):
    ang = (2.0 * math.pi / n) * jnp.mod(expo, n).astype(F32)
    return jnp.cos(ang), jnp.sin(ang)


def _real_form(gr, gi):
    return jnp.concatenate([jnp.concatenate([gr, -gi], axis=-1), jnp.concatenate([gi, gr], axis=-1)], axis=-2)


def _dft_mats():
    ar = lambda n: jnp.arange(n, dtype=jnp.int32)
    m = {}
    c, s = _cis(ar(HY_N1)[:, None] * ar(HY_N1 // 2)[None, :], HY_N1)
    m['hy_lead_f'] = _real_form(c, -s).astype(BF16)
    c, s = _cis(ar(HY_N1 // 2)[:, None] * ar(HY_N1)[None, :], HY_N1)
    m['hy_lead_i'] = _real_form(c, s).astype(BF16)
    c, s = _cis(ar(HY_N1)[:, None] * ar(HY_N1 // 2)[None, :], HY_N1)
    m['hy_lead_kh'] = jnp.concatenate([c, -s], axis=0).astype(BF16)
    a = ar(HY_N1)[:, None, None]
    k2 = ar(HY_N2)[None, :, None]
    n2 = ar(HY_N2)[None, None, :]
    c, s = _cis(n2 * (a + HY_N1 * k2), HY_N)
    m['hy_slab_f2'] = jnp.concatenate([c, -s], axis=-1).astype(BF16)
    ct = jnp.swapaxes(c, 1, 2) * (1.0 / HY_N)
    st = jnp.swapaxes(s, 1, 2) * (1.0 / HY_N)
    m['hy_slab_i2'] = jnp.concatenate([ct, st], axis=-1).astype(BF16)
    nc = 2 * CTX_LEN
    c, s = _cis(ar(nc)[:, None] * ar(CTX_LEN)[None, :], nc)
    m['hc_f'] = _real_form(c, -s).astype(BF16)
    c, s = _cis(ar(CTX_LEN)[:, None] * ar(nc)[None, :], nc)
    m['hc_i'] = _real_form(c * (1.0 / nc), s * (1.0 / nc)).astype(BF16)
    c, s = _cis(ar(nc)[:, None] * ar(CTX_LEN)[None, :], nc)
    m['hc_kh'] = jnp.concatenate([c, -s], axis=0).astype(BF16)
    c, s = _cis(ar(FNET_GROUP_DIM)[:, None] * ar(FNET_GROUP_DIM)[None, :], FNET_GROUP_DIM)
    m['fn_chan'] = jnp.concatenate([c, -s], axis=1).astype(BF16)
    c, s = _cis(ar(FN_N)[:, None] * ar(FN_N)[None, :], FN_N)
    m['fn_lead'] = _real_form(c, -s).astype(BF16)
    a = ar(FN_N)[:, None, None]
    k1 = ar(FN_N)[None, :, None]
    n1 = ar(FN_N)[None, None, :]
    scale = 1.0 / math.sqrt(SEQ * FNET_GROUP_DIM)
    c, s = _cis(n1 * (a + FN_N * k1), SEQ)
    m['fn_slab'] = jnp.concatenate([c * scale, s * scale], axis=-1).astype(BF16)
    scale = 1.0 / math.sqrt(CTX_LEN * FNET_GROUP_DIM)
    c, s = _cis(ar(CTX_LEN)[:, None] * ar(CTX_LEN)[None, :], CTX_LEN)
    m['fc'] = jnp.concatenate([c * scale, s * scale], axis=-1).astype(BF16)
    return m


def _mod_kernel(c_ref, w_ref, b_ref, o_ref):
    c = c_ref[...]
    s = c * jax.nn.sigmoid(c)
    o_ref[0] = _dot(s.astype(BF16), w_ref[0].astype(BF16)) + b_ref[0]


def _modulation(c8, w_mod, b_mod):
    tn = 1536
    n = N_MOD * D_MODEL
    return pl.pallas_call(
        _mod_kernel,
        grid=(DEPTH, n // tn),
        in_specs=[pl.BlockSpec((8, D_MODEL), lambda l, j: (0, 0)),
                  pl.BlockSpec((1, D_MODEL, tn), lambda l, j: (l, 0, j)),
                  pl.BlockSpec((1, 1, tn), lambda l, j: (l, 0, j))],
        out_specs=pl.BlockSpec((1, 8, tn), lambda l, j: (l, 0, j)),
        out_shape=jax.ShapeDtypeStruct((DEPTH, 8, n), F32),
        compiler_params=_cparams(("parallel", "parallel")),
        name="adaln_modulation",
    )(c8, w_mod, b_mod.reshape(DEPTH, 1, n))


def _mod_row(tm):
    tiles_per_batch = SEQ // tm
    return lambda i: jnp.minimum(i // tiles_per_batch, BATCH)


def _rms_mod(x, g, sh, sc):
    y = x * lax.rsqrt(jnp.mean(x * x, axis=-1, keepdims=True) + EPS)
    return (y * g) * (1.0 + sc) + sh


def _norm_proj_kernel(x_ref, g_ref, sh_ref, sc_ref, w_ref, o_ref, w_s):
    @pl.when(pl.program_id(1) == 0)
    def _():
        w_s[...] = w_ref[...].astype(BF16)

    h = _rms_mod(x_ref[...], g_ref[...], sh_ref[0, 0], sc_ref[0, 0]).astype(BF16)
    o_ref[...] = _dot(h, w_s[...]).astype(o_ref.dtype)


def _norm_proj(x, gain, mod4, w, layer, n_tok):
    tm = TM
    n_out = w.shape[2]
    tn = n_out // 2
    row = _mod_row(tm)
    return pl.pallas_call(
        _norm_proj_kernel,
        grid=(n_out // tn, n_tok // tm),
        in_specs=[pl.BlockSpec((tm, D_MODEL), lambda j, i: (i, 0)),
                  pl.BlockSpec((1, D_MODEL), lambda j, i: (0, 0)),
                  pl.BlockSpec((1, 1, 1, D_MODEL), lambda j, i: (row(i), 0, 0, 0)),
                  pl.BlockSpec((1, 1, 1, D_MODEL), lambda j, i: (row(i), 1, 0, 0)),
                  pl.BlockSpec((None, D_MODEL, tn), lambda j, i: (layer, 0, j))],
        out_specs=pl.BlockSpec((tm, tn), lambda j, i: (i, j)),
        out_shape=jax.ShapeDtypeStruct((T_ALL, n_out), BF16),
        scratch_shapes=[pltpu.VMEM((D_MODEL, tn), BF16)],
        compiler_params=_cparams(("arbitrary", "arbitrary")),
        name="norm_in_proj",
    )(x, gain, mod4, mod4, w)


def _softmax_pv(qh, k_parts, v_parts, masks, sink):
    nt = (((1,), (1,)), ((), ()))
    scores = []
    for kp, mk in zip(k_parts, masks):
        s = lax.dot_general(qh, kp, nt, preferred_element_type=F32)
        if mk is not None:
            s = jnp.where(mk, s, -1e30)
        scores.append(s)
    m = sink
    for s in scores:
        m = jnp.maximum(m, jnp.max(s, axis=-1, keepdims=True))
    es = [jnp.exp(s - m) for s in scores]
    den = jnp.exp(sink - m)
    for e in es:
        den = den + jnp.sum(e, axis=-1, keepdims=True)
    inv = 1.0 / den
    o = None
    for e, vp in zip(es, v_parts):
        t = _dot((e * inv).astype(BF16), vp)
        o = t if o is None else o + t
    return o


def _attn_kernel(sink_ref, q_ref, km_ref, k0_ref, kp_ref, vm_ref, v0_ref, vp_ref, kc_ref, vc_ref,
                 cos_ref, sin_ref, psw_ref, p64_ref, o_ref, *, nb):
    n = pl.program_id(1)
    blk = ATTN_BLOCK
    psw = psw_ref[...]
    p64 = p64_ref[...]

    def rope(xb, blk_idx):
        r0 = pl.multiple_of(blk_idx * blk, blk)
        return xb.astype(F32) * cos_ref[pl.ds(r0, blk), :] + _dot(xb, psw) * sin_ref[pl.ds(r0, blk), :]

    nm = jnp.maximum(n - 1, 0)
    npl = jnp.minimum(n + 1, nb - 1)
    kall = jnp.concatenate([rope(km_ref[...], nm), rope(k0_ref[...], n), rope(kp_ref[...], npl),
                            kc_ref[...].astype(F32)], axis=0)
    vall = jnp.concatenate([vm_ref[...], v0_ref[...], vp_ref[...], vc_ref[...]], axis=0)
    nk = kall.shape[0]
    lo = lax.broadcasted_iota(jnp.int32, (nk, 128), 1) < HEAD_DIM

    r = lax.broadcasted_iota(jnp.int32, (blk, blk), 0)
    cidx = lax.broadcasted_iota(jnp.int32, (blk, blk), 1)
    ok_prev = jnp.where(cidx >= r, (n > 0).astype(jnp.int32), 0) > 0
    ok_next = jnp.where(cidx <= r, (n < nb - 1).astype(jnp.int32), 0) > 0
    neg = jnp.float32(-1e30)

    scale = HEAD_DIM ** -0.5
    q2 = [(rope(q_ref[:, p * 128:(p + 1) * 128], n) * scale).astype(BF16) for p in range(N_Q_HEADS // 2)]
    nt = (((1,), (1,)), ((), ()))
    n_pair = Q_PER_KV // 2
    for h in range(N_KV_HEADS):
        kh = (jnp.where(lo, kall, 0.0) if h == 0 else jnp.where(lo, 0.0, kall)).astype(BF16)
        vh = jnp.where(lo, vall, jnp.zeros_like(vall)) if h == 0 else jnp.where(lo, jnp.zeros_like(vall), vall)
        q4 = jnp.concatenate(q2[n_pair * h:n_pair * (h + 1)], axis=0)
        q4s = _dot(q4, p64).astype(BF16)
        first, second = (q4, q4s) if h == 0 else (q4s, q4)
        s = lax.dot_general(jnp.concatenate([first, second], axis=0), kh, nt, preferred_element_type=F32)
        probs = []
        for rb in range(2 * n_pair):
            sink = sink_ref[Q_PER_KV * h + 2 * (rb % n_pair) + rb // n_pair]
            sb = s[rb * blk:(rb + 1) * blk]
            parts = [jnp.where(ok_prev, sb[:, 0:blk], neg), sb[:, blk:2 * blk],
                     jnp.where(ok_next, sb[:, 2 * blk:3 * blk], neg), sb[:, 3 * blk:nk]]
            tiles = parts[:3] + [parts[3][:, t * blk:(t + 1) * blk] for t in range((nk - 3 * blk) // blk)]
            m = jnp.maximum(sink, jnp.max(functools.reduce(jnp.maximum, tiles), axis=-1, keepdims=True))
            es = [jnp.exp(t - m) for t in tiles]
            den = jnp.exp(sink - m) + jnp.sum(functools.reduce(jnp.add, es), axis=-1, keepdims=True)
            inv = 1.0 / den
            probs.append(jnp.concatenate([(e * inv).astype(BF16) for e in es], axis=1))
        o = _dot(jnp.concatenate(probs, axis=0), vh)
        half = n_pair * blk
        if h == 0:
            out = o[:half] + pltpu.roll(o[half:], HEAD_DIM, 1)
        else:
            out = pltpu.roll(o[:half], HEAD_DIM, 1) + o[half:]
        w0 = h * Q_PER_KV * HEAD_DIM
        for j in range(n_pair):
            o_ref[:, w0 + 128 * j:w0 + 128 * (j + 1)] = out[j * blk:(j + 1) * blk].astype(o_ref.dtype)


def _ctx_attn_kernel(sink_ref, q_ref, kc_ref, vc_ref, o_ref):
    kc = kc_ref[...].astype(BF16)
    vc = vc_ref[...].astype(BF16)
    scale = HEAD_DIM ** -0.5
    outs = []
    for pair in range(N_Q_HEADS // 2):
        q2 = (q_ref[:, pair * 128:(pair + 1) * 128] * scale).astype(BF16)
        for sub in range(2):
            head = 2 * pair + sub
            kvh = head // Q_PER_KV
            sl = slice(kvh * HEAD_DIM, (kvh + 1) * HEAD_DIM)
            qh = q2[:, sub * HEAD_DIM:(sub + 1) * HEAD_DIM]
            outs.append(_softmax_pv(qh, [kc[:, sl]], [vc[:, sl]], [None], sink_ref[head]))
    o_ref[...] = jnp.concatenate(outs, axis=-1).astype(o_ref.dtype)


def _attention(p, sink, cos_t, sin_t, with_ctx):
    blk = ATTN_BLOCK
    nb = SEQ // blk
    qc, kcol, vcol = PQ_OFF // ATTN_WIDTH, PK_OFF // KV_WIDTH, PV_OFF // KV_WIDTH
    ctx_blk = T_LAT // CTX_LEN
    smem = pl.BlockSpec(memory_space=pltpu.SMEM)

    def kv_spec(col, d):
        return pl.BlockSpec((blk, KV_WIDTH),
                            lambda b, n: (b * nb + jnp.clip(n + d, 0, nb - 1), col))

    ya = pl.pallas_call(
        functools.partial(_attn_kernel, nb=nb),
        grid=(BATCH, nb),
        in_specs=[smem,
                  pl.BlockSpec((blk, ATTN_WIDTH), lambda b, n: (b * nb + n, qc)),
                  kv_spec(kcol, -1), kv_spec(kcol, 0), kv_spec(kcol, 1),
                  kv_spec(vcol, -1), kv_spec(vcol, 0), kv_spec(vcol, 1),
                  pl.BlockSpec((CTX_LEN, KV_WIDTH), lambda b, n: (ctx_blk + b, kcol)),
                  pl.BlockSpec((CTX_LEN, KV_WIDTH), lambda b, n: (ctx_blk + b, vcol)),
                  pl.BlockSpec((SEQ, KV_WIDTH), lambda b, n: (0, 0)),
                  pl.BlockSpec((SEQ, KV_WIDTH), lambda b, n: (0, 0)),
                  pl.BlockSpec((128, 128), lambda b, n: (0, 0)),
                  pl.BlockSpec((128, 128), lambda b, n: (0, 0))],
        out_specs=pl.BlockSpec((blk, ATTN_WIDTH), lambda b, n: (b * nb + n, 0)),
        out_shape=jax.ShapeDtypeStruct((T_ALL, ATTN_WIDTH), BF16),
        compiler_params=_cparams(("parallel", "parallel")),
        name="banded_attention",
    )(sink, p, p, p, p, p, p, p, p, p, cos_t, sin_t, *_lane_perms())
    if not with_ctx:
        return ya
    cb = CTX_LEN // blk
    lat_blk = T_LAT // blk

    def alias_kernel(sink_ref, q_ref, kc_ref, vc_ref, ya_in_ref, o_ref):
        del ya_in_ref
        _ctx_attn_kernel(sink_ref, q_ref, kc_ref, vc_ref, o_ref)

    return pl.pallas_call(
        alias_kernel,
        grid=(BATCH, cb),
        in_specs=[smem,
                  pl.BlockSpec((blk, ATTN_WIDTH), lambda b, n: (lat_blk + b * cb + n, qc)),
                  pl.BlockSpec((CTX_LEN, KV_WIDTH), lambda b, n: (ctx_blk + b, kcol)),
                  pl.BlockSpec((CTX_LEN, KV_WIDTH), lambda b, n: (ctx_blk + b, vcol)),
                  pl.BlockSpec(memory_space=pl.ANY)],
        out_specs=pl.BlockSpec((blk, ATTN_WIDTH), lambda b, n: (lat_blk + b * cb + n, 0)),
        out_shape=jax.ShapeDtypeStruct((T_ALL, ATTN_WIDTH), BF16),
        input_output_aliases={4: 0},
        compiler_params=_cparams(("parallel", "parallel")),
        name="context_attention",
    )(sink, p, p, p, ya)


def _rope_tables():
    n_freq = HEAD_DIM // 4
    freqs = ROPE_BASE ** (-jnp.arange(n_freq, dtype=F32) / n_freq)
    t = jnp.arange(SEQ, dtype=jnp.int32)
    rows = (t // GRID_W).astype(F32)[:, None] * freqs
    cols = (t % GRID_W).astype(F32)[:, None] * freqs
    cos_h = jnp.concatenate([jnp.cos(rows), jnp.cos(rows), jnp.cos(cols), jnp.cos(cols)], axis=-1)
    sin_h = jnp.concatenate([-jnp.sin(rows), jnp.sin(rows), -jnp.sin(cols), jnp.sin(cols)], axis=-1)
    return jnp.tile(cos_h, (1, 2)), jnp.tile(sin_h, (1, 2))


def _lane_perms():
    j = jnp.arange(128, dtype=jnp.int32)[:, None]
    l = jnp.arange(128, dtype=jnp.int32)[None, :]
    rot = jnp.where(l % 32 < 16, l + 16, l - 16)
    return (j == rot).astype(BF16), (j == (l + HEAD_DIM) % 128).astype(BF16)


def _lead_kernel(*refs, n_in, cmul, epi):
    m_ref = refs[0]
    x_refs = refs[1:1 + n_in]
    pos = 1 + n_in
    xs = []
    for r in x_refs:
        v = r[...]
        xs.append(v.reshape(-1, v.shape[-1]))
    x = xs[0] if n_in == 1 else jnp.concatenate(xs, axis=0)
    if cmul:
        k = refs[pos][...]
        pos += 1
        k = k.reshape(-1, k.shape[-1])
        half = x.shape[0] // 2
        xr, xi, kr, ki = x[:half], x[half:], k[:half], k[half:]
        x = jnp.concatenate([xr * kr - xi * ki, xr * ki + xi * kr], axis=0)
    res = _dot(m_ref[...], x.astype(BF16))
    if epi:
        g_ref, y_ref, b_ref = refs[pos:pos + 3]
        pos += 3
        g = g_ref[...]
        y = y_ref[...]
        res = g.reshape(-1, g.shape[-1]) * (res + y.reshape(-1, y.shape[-1]) * b_ref[...])
    o_ref = refs[pos]
    o_ref[...] = res.reshape(o_ref.shape).astype(o_ref.dtype)


def _lead(mat, xs, x_specs, grid, out_shape, out_spec, *, kspec=None, epi=None, alias_to=None, name):
    ins = [mat] + list(xs)
    specs = [pl.BlockSpec(mat.shape, lambda *a: (0, 0))] + list(x_specs)
    if kspec is not None:
        ins.append(kspec[0])
        specs.append(kspec[1])
    if epi is not None:
        for arr, sp in epi:
            ins.append(arr)
            specs.append(sp)
    kern = functools.partial(_lead_kernel, n_in=len(xs), cmul=kspec is not None, epi=epi is not None)
    aliases = {}
    if alias_to is not None:
        aliases = {len(ins): 0}
        ins.append(alias_to)
        specs.append(pl.BlockSpec(memory_space=pl.ANY))
        inner = kern

        def kern(*refs):
            inner(*refs[:-2], refs[-1])

    return pl.pallas_call(
        kern, grid=grid, in_specs=specs, out_specs=out_spec, out_shape=out_shape,
        input_output_aliases=aliases,
        compiler_params=_cparams(("parallel",) * len(grid)), name=name,
    )(*ins)


def _cstack(xr, xi):
    return jnp.concatenate([jnp.concatenate([xr, xi], axis=1), jnp.concatenate([-xi, xr], axis=1)], axis=0)


def _fnet_kernel(u_ref, mc_ref, ml_ref, ms_ref, o_ref, zr, zi, are, aim):
    n = FN_N
    pitch = PITCH
    gd = FNET_GROUP_DIM
    rows = 4 * n
    mc = mc_ref[...]

    def chan(i, carry):
        r_in = pl.multiple_of(i * rows, rows)
        z = _dot(u_ref[pl.ds(r_in, rows), :].astype(BF16), mc)
        for q in range(rows // n):
            r_out = pl.multiple_of((i * (rows // n) + q) * pitch, 8)
            zr[pl.ds(r_out, n), :] = z[q * n:(q + 1) * n, :gd]
            zi[pl.ds(r_out, n), :] = z[q * n:(q + 1) * n, gd:]
        return carry

    lax.fori_loop(0, SEQ // rows, chan, 0)
    ml = ml_ref[...]

    def lead(i, carry):
        for u in range(UNROLL):
            n1 = i * UNROLL + u
            x = jnp.concatenate([zr[pl.ds(n1, n, stride=pitch), :], zi[pl.ds(n1, n, stride=pitch), :]],
                                axis=0).astype(BF16)
            r = _dot(ml, x)
            are[pl.ds(n1, n, stride=pitch), :] = r[:n]
            aim[pl.ds(n1, n, stride=pitch), :] = r[n:]
        return carry

    lax.fori_loop(0, n // UNROLL, lead, 0)

    def slab(i, carry):
        for u in range(UNROLL):
            k2 = i * UNROLL + u
            r0 = pl.multiple_of(k2 * pitch, 8)
            x = jnp.concatenate([are[pl.ds(r0, n), :], aim[pl.ds(r0, n), :]], axis=0).astype(BF16)
            o_ref[pl.ds(k2, n, stride=n), :] = _dot(ms_ref[k2], x)
        return carry

    lax.fori_loop(0, n // UNROLL, slab, 0)


def _fnet_ctx_kernel(u_ref, mc_ref, mf_ref, yf_in_ref, o_ref):
    del yf_in_ref
    gd = FNET_GROUP_DIM
    mc = mc_ref[...]
    mf = mf_ref[...]
    for g in range(FNET_GROUPS):
        z = _dot(u_ref[:, g * gd:(g + 1) * gd].astype(BF16), mc)
        x = jnp.concatenate([z[:, :gd], z[:, gd:]], axis=0).astype(BF16)
        o_ref[:, g * gd:(g + 1) * gd] = _dot(mf, x)


def _fourier_mix(p, mats, with_ctx):
    gd = FNET_GROUP_DIM
    col0 = PF_OFF // gd
    full = lambda a: pl.BlockSpec(a.shape, lambda *i: (0,) * a.ndim)
    scr = pltpu.VMEM((FN_N * PITCH, gd), F32)
    yf = pl.pallas_call(
        _fnet_kernel,
        grid=(BATCH, FNET_GROUPS),
        in_specs=[pl.BlockSpec((SEQ, gd), lambda b, g: (b, col0 + g)),
                  full(mats['fn_chan']), full(mats['fn_lead']), full(mats['fn_slab'])],
        out_specs=pl.BlockSpec((SEQ, gd), lambda b, g: (b, g)),
        out_shape=jax.ShapeDtypeStruct((T_ALL if with_ctx else T_LAT, FNET_WIDTH), F32),
        scratch_shapes=[scr, scr, scr, scr],
        compiler_params=_cparams(("parallel", "parallel")),
        name="fnet_latent",
    )(p, mats['fn_chan'], mats['fn_lead'], mats['fn_slab'])
    if not with_ctx:
        return yf
    blk0 = T_LAT // CTX_LEN
    return pl.pallas_call(
        _fnet_ctx_kernel,
        grid=(BATCH,),
        in_specs=[pl.BlockSpec((pl.Element(CTX_LEN), pl.Element(FNET_WIDTH)),
                               lambda b: ((blk0 + b) * CTX_LEN, PF_OFF)),
                  full(mats['fn_chan']), full(mats['fc']), pl.BlockSpec(memory_space=pl.ANY)],
        out_specs=pl.BlockSpec((CTX_LEN, FNET_WIDTH), lambda b: (blk0 + b, 0)),
        out_shape=jax.ShapeDtypeStruct((T_ALL, FNET_WIDTH), F32),
        input_output_aliases={3: 0},
        compiler_params=_cparams(("parallel",)),
        name="fnet_ctx",
    )(p, mats['fn_chan'], mats['fc'], yf)


def _short_conv_kernel(u_ref, w_ref, b_ref, o_ref, *, rows, chunk, padded):
    w0 = w_ref[0:1, :]
    w1 = w_ref[1:2, :]
    w2 = w_ref[2:3, :]
    bias = b_ref[...]
    width = u_ref.shape[-1]
    ridx = lax.broadcasted_iota(jnp.int32, (chunk, width), 0)
    n_chunks = rows // chunk
    for ci in range(n_chunks):
        r0 = ci * chunk
        cur = u_ref[r0:r0 + chunk, :].astype(F32)
        if ci > 0:
            prev_row = u_ref[r0 - 16:r0, :].astype(F32)[15:16, :]
        else:
            prev_row = jnp.zeros((1, width), F32)
        if ci < n_chunks - 1:
            next_row = u_ref[r0 + chunk:r0 + chunk + 16, :].astype(F32)[0:1, :]
        else:
            next_row = jnp.zeros((1, width), F32)
        up = jnp.where(ridx == 0, prev_row, pltpu.roll(cur, 1, 0))
        dn = jnp.where(ridx == chunk - 1, next_row, pltpu.roll(cur, chunk - 1, 0))
        res = up * w0 + cur * w1 + dn * w2 + bias
        if not padded:
            o_ref[0, r0:r0 + chunk, :] = res
            continue
        for q in range(chunk // HY_N2):
            p0 = (r0 // HY_N2 + q) * PITCH
            o_ref[0, p0:p0 + HY_N2, :] = res[q * HY_N2:(q + 1) * HY_N2]
            o_ref[0, p0 + HY_N2:p0 + PITCH, :] = jnp.zeros((PITCH - HY_N2, width), F32)


def _short_conv(p, conv_w, conv_b, with_ctx):
    cw = 256
    hw = (HYENA_ORDER + 1) * HYENA_WIDTH
    ncol = hw // cw
    per = HYENA_WIDTH // cw
    col0 = PH_OFF // cw
    out_shape = jax.ShapeDtypeStruct((HYENA_ORDER + 1, T_PAD if with_ctx else BATCH * SEQ_PAD, HYENA_WIDTH), F32)
    b2 = conv_b.reshape(1, hw)

    def call(rows, blk0, alias):
        padded = rows == SEQ
        out_rows = SEQ_PAD if padded else rows
        out_blk0 = 0 if padded else BATCH * SEQ_PAD // rows
        kern = functools.partial(_short_conv_kernel, rows=rows, chunk=min(rows, 256), padded=padded)
        ins = [p, conv_w, b2]
        specs = [pl.BlockSpec((rows, cw), lambda b, j: (blk0 + b, col0 + j)),
                 pl.BlockSpec((3, cw), lambda b, j: (0, j)),
                 pl.BlockSpec((1, cw), lambda b, j: (0, j))]
        aliases = {}
        if alias is not None:
            ins.append(alias)
            specs.append(pl.BlockSpec(memory_space=pl.ANY))
            aliases = {3: 0}
            inner = kern

            def kern(u_ref, w_ref, b_ref, a_ref, o_ref):
                del a_ref
                inner(u_ref, w_ref, b_ref, o_ref)

        return pl.pallas_call(
            kern, grid=(BATCH, ncol), in_specs=specs,
            out_specs=pl.BlockSpec((1, out_rows, cw), lambda b, j: (j // per, out_blk0 + b, j % per)),
            out_shape=out_shape, input_output_aliases=aliases,
            compiler_params=_cparams(("parallel", "parallel")), name="hyena_short_conv",
        )(*ins)

    z3 = call(SEQ, 0, None)
    if with_ctx:
        z3 = call(CTX_LEN, T_LAT // CTX_LEN, z3)
    return z3


def _filter_mlp_kernel(ft_ref, w1_ref, b1_ref, fq_ref, w2_ref, b2_ref, o_ref):
    fq = fq_ref[...]
    h = jnp.sin(fq * (_dot(ft_ref[...].astype(BF16), w1_ref[...]) + b1_ref[...]))
    h = jnp.sin(fq * (_dot(h.astype(BF16), w2_ref[...]) + b2_ref[...]))
    o_ref[...] = h.astype(o_ref.dtype)


def _filter_kernel(h_ref, t_ref, w3f_ref, w3b_ref, dl_ref, m1_ref, *rest, n, dense):
    hb = h_ref[...]
    decay = jnp.exp(-t_ref[...] * dl_ref[...])
    tf = _dot(hb, w3f_ref[...]) * decay
    tb = _dot(hb, w3b_ref[...]) * decay
    tb = jnp.where(lax.broadcasted_iota(jnp.int32, tb.shape, 0) == 0, 0.0, tb)
    scale = 1.0 / (jnp.sum(jnp.abs(tf), axis=0, keepdims=True) + jnp.sum(jnp.abs(tb), axis=0, keepdims=True))
    cw = tf.shape[1]
    if dense:
        o_ref = rest[0]
        r = _dot(m1_ref[...], jnp.concatenate([tf, tb], axis=1).astype(BF16))
        nc = r.shape[0] // 2
        o_ref[0:nc, :] = (r[:nc, :cw] + r[:nc, cw:]) * scale
        o_ref[nc:, :] = (r[nc:, :cw] - r[nc:, cw:]) * scale
        return
    f2_ref, o_ref, tf_s, tb_s, are_f, aim_f, are_b, aim_b = rest
    tf_s[...] = tf
    tb_s[...] = tb
    half = HY_N1 // 2
    m1 = m1_ref[...]

    def stage1(i, carry):
        for u in range(UNROLL):
            n2 = i * UNROLL + u
            x = jnp.concatenate([tf_s[pl.ds(n2, half, stride=HY_N2), :], tb_s[pl.ds(n2, half, stride=HY_N2), :]],
                                axis=1).astype(BF16)
            r = _dot(m1, x)
            rows = pl.ds(n2, HY_N1, stride=PITCH)
            are_f[rows, :] = r[:HY_N1, :cw]
            are_b[rows, :] = r[:HY_N1, cw:]
            aim_f[rows, :] = r[HY_N1:, :cw]
            aim_b[rows, :] = r[HY_N1:, cw:]
        return carry

    lax.fori_loop(0, HY_N2 // UNROLL, stage1, 0)

    def stage2(i, carry):
        for u in range(UNROLL):
            k1 = i * UNROLL + u
            rows = pl.ds(pl.multiple_of(k1 * PITCH, 8), HY_N2)
            ar = jnp.concatenate([are_f[rows, :], are_b[rows, :]], axis=1)
            ai = jnp.concatenate([aim_f[rows, :], aim_b[rows, :]], axis=1)
            t = _dot(f2_ref[k1], _cstack(ar, ai).astype(BF16))
            o_ref[0, k1] = (t[:, 0:cw] + t[:, cw:2 * cw]) * scale
            o_ref[1, k1] = (t[:, 2 * cw:3 * cw] - t[:, 3 * cw:4 * cw]) * scale
        return carry

    lax.fori_loop(0, HY_N1 // UNROLL, stage2, 0)


def _filter_feats(n):
    pos = jnp.arange(n, dtype=F32)
    t = pos / max(n - 1, 1)
    omega = 2.0 * math.pi * pos / n
    bands = jnp.linspace(1e-4, FILTER_BANDS - 1, FILTER_BANDS, dtype=F32)
    feats = jnp.concatenate([t[:, None], jnp.cos(omega[:, None] * bands), -jnp.sin(omega[:, None] * bands)], axis=-1)
    return jnp.pad(feats, ((0, 0), (0, 128 - FILTER_EMB))), t[:, None]


def _filter_spectrum(n, filt, mats):
    w1, b1, freq, w2, b2, w3 = filt
    dense = n == CTX_LEN
    cw = 128
    nch = HYENA_WIDTH // cw
    feats, t = _filter_feats(n)
    w1p = jnp.pad(w1, ((0, 128 - FILTER_EMB), (0, 0))).astype(BF16)
    deltas = jnp.abs(jnp.linspace(math.log(DECAY_TARGET) / SLOW_DECAY_PCT, math.log(DECAY_TARGET) / FAST_DECAY_PCT,
                                  HYENA_WIDTH, dtype=F32)).reshape(1, HYENA_WIDTH)
    full = lambda a: pl.BlockSpec(a.shape, lambda *i: (0,) * a.ndim)
    row = lambda a: a.reshape(1, -1)
    w3b16 = w3.astype(BF16)
    tap_spec = lambda d: pl.BlockSpec((FILTER_HIDDEN, cw), lambda o, ch: (0, (o * 2 + d) * nch + ch))
    m1 = mats['hc_kh'] if dense else mats['hy_lead_kh']
    mlp_ins = [feats, w1p, row(b1), row(freq), w2.astype(BF16), row(b2)]
    hb = pl.pallas_call(
        _filter_mlp_kernel, grid=(1,), in_specs=[full(a) for a in mlp_ins],
        out_specs=pl.BlockSpec((n, FILTER_HIDDEN), lambda i: (0, 0)),
        out_shape=jax.ShapeDtypeStruct((n, FILTER_HIDDEN), BF16),
        compiler_params=_cparams(("arbitrary",)), name="hyena_filter_mlp",
    )(*mlp_ins)
    ins = [hb, t, w3b16, w3b16, deltas, m1]
    specs = [full(hb), full(t), tap_spec(0), tap_spec(1), pl.BlockSpec((1, cw), lambda o, ch: (0, ch)), full(m1)]
    if dense:
        nc = 2 * n
        out_shape = jax.ShapeDtypeStruct((HYENA_ORDER, 2 * nc, HYENA_WIDTH), F32)
        out_spec = pl.BlockSpec((None, 2 * nc, cw), lambda o, ch: (o, 0, ch))
        scratch = []
    else:
        ins.append(mats['hy_slab_f2'])
        specs.append(full(mats['hy_slab_f2']))
        out_shape = jax.ShapeDtypeStruct((HYENA_ORDER, 2, HY_N1, HY_N2, HYENA_WIDTH), F32)
        out_spec = pl.BlockSpec((None, 2, HY_N1, HY_N2, cw), lambda o, ch: (o, 0, 0, 0, ch))
        scratch = [pltpu.VMEM((n, cw), F32)] * 2 + [pltpu.VMEM((HY_N1 * PITCH, cw), F32)] * 4
    return pl.pallas_call(
        functools.partial(_filter_kernel, n=n, dense=dense),
        grid=(HYENA_ORDER, nch), in_specs=specs, out_specs=out_spec, out_shape=out_shape,
        scratch_shapes=scratch,
        compiler_params=_cparams(("parallel", "parallel")),
        name="hyena_filter_ctx" if dense else "hyena_filter",
    )(*ins)


def _hyena_conv_kernel(y_ref, g_ref, k_ref, m1_ref, f2f_ref, f2i_ref, m3_ref, b_ref, o_ref, are, aim, *, out_pitch):
    half = HY_N1 // 2
    out_seq = half * out_pitch
    m1 = m1_ref[...]

    def stage1(i, carry):
        for u in range(UNROLL):
            n2 = i * UNROLL + u
            x = jnp.concatenate([y_ref[pl.ds(n2, half, stride=PITCH), :],
                                 y_ref[pl.ds(SEQ_PAD + n2, half, stride=PITCH), :]], axis=0).astype(BF16)
            r = _dot(m1, x)
            are[pl.ds(n2, HY_N1, stride=PITCH), :] = r[:HY_N1]
            aim[pl.ds(n2, HY_N1, stride=PITCH), :] = r[HY_N1:]
        return carry

    lax.fori_loop(0, HY_N2 // UNROLL, stage1, 0)
    cw = o_ref.shape[-1]

    def stage2(i, carry):
        for u in range(UNROLL):
            k1 = i * UNROLL + u
            r0 = pl.multiple_of(k1 * PITCH, 8)
            y = _dot(f2f_ref[k1], _cstack(are[pl.ds(r0, HY_N2), :], aim[pl.ds(r0, HY_N2), :]).astype(BF16))
            yr, yi = y[:, :cw], y[:, cw:]
            kr, ki = k_ref[0, k1], k_ref[1, k1]
            w = _dot(f2i_ref[k1], _cstack(yr * kr - yi * ki, yr * ki + yi * kr).astype(BF16))
            are[pl.ds(r0, HY_N2), :] = w[:, :cw]
            aim[pl.ds(r0, HY_N2), :] = w[:, cw:]
        return carry

    lax.fori_loop(0, HY_N1 // UNROLL, stage2, 0)
    m3 = m3_ref[...]
    bias = b_ref[...]

    def stage3(i, carry):
        for u in range(UNROLL):
            n2 = i * UNROLL + u
            bn = jnp.concatenate([are[pl.ds(n2, HY_N1, stride=PITCH), :], aim[pl.ds(n2, HY_N1, stride=PITCH), :]],
                                 axis=0).astype(BF16)
            y = _dot(m3, bn)
            for b in range(2):
                rows = pl.ds(b * SEQ_PAD + n2, half, stride=PITCH)
                o_ref[pl.ds(b * out_seq + n2, half, stride=out_pitch), :] = (
                    g_ref[rows, :] * (y[b * half:(b + 1) * half] + y_ref[rows, :] * bias))
        return carry

    lax.fori_loop(0, HY_N2 // UNROLL, stage3, 0)


def _hyena_mix(z3, filt, hyena_bias, mats, with_ctx):
    c = HYENA_WIDTH
    cw = 128
    nch = c // cw
    pairs = BATCH // 2
    full = lambda a: pl.BlockSpec(a.shape, lambda *i: (0,) * a.ndim)
    kspec = _filter_spectrum(SEQ, filt, mats)
    if with_ctx:
        kspec_c = _filter_spectrum(CTX_LEN, filt, mats)
    bias3 = hyena_bias.reshape(HYENA_ORDER, 1, c)
    scr = pltpu.VMEM((HY_N1 * PITCH, cw), F32)
    y = None
    for o in range(HYENA_ORDER):
        final = o == HYENA_ORDER - 1
        out_pitch = HY_N2 if final else PITCH
        out_seq = SEQ if final else SEQ_PAD
        if final:
            out_rows = T_ALL if with_ctx else T_LAT
        else:
            out_rows = T_PAD if with_ctx else BATCH * SEQ_PAD
        if y is None:
            xin, xspec = z3, pl.BlockSpec((None, 2 * SEQ_PAD, cw), lambda b, ch: (0, b, ch))
        else:
            xin, xspec = y, pl.BlockSpec((2 * SEQ_PAD, cw), lambda b, ch: (b, ch))
        ynew = pl.pallas_call(
            functools.partial(_hyena_conv_kernel, out_pitch=out_pitch),
            grid=(pairs, nch),
            in_specs=[xspec,
                      pl.BlockSpec((None, 2 * SEQ_PAD, cw), lambda b, ch, o=o: (o + 1, b, ch)),
                      pl.BlockSpec((None, 2, HY_N1, HY_N2, cw), lambda b, ch, o=o: (o, 0, 0, 0, ch)),
                      full(mats['hy_lead_f']), full(mats['hy_slab_f2']), full(mats['hy_slab_i2']),
                      full(mats['hy_lead_i']),
                      pl.BlockSpec((None, 1, cw), lambda b, ch, o=o: (o, 0, ch))],
            out_specs=pl.BlockSpec((2 * out_seq, cw), lambda b, ch: (b, ch)),
            out_shape=jax.ShapeDtypeStruct((out_rows, c), F32),
            scratch_shapes=[scr, scr],
            compiler_params=_cparams(("parallel", "parallel"), VMEM_LIMIT_CONV),
            name="hyena_conv",
        )(xin, z3, kspec, mats['hy_lead_f'], mats['hy_slab_f2'], mats['hy_slab_i2'], mats['hy_lead_i'], bias3)
        if with_ctx:
            nc = 2 * CTX_LEN
            blk0 = BATCH * SEQ_PAD // nc
            oblk0 = BATCH * out_seq // nc
            if y is None:
                cin, cspec_in = z3, pl.BlockSpec((1, nc, c), lambda b: (0, blk0 + b, 0))
            else:
                cin, cspec_in = y, pl.BlockSpec((nc, c), lambda b: (blk0 + b, 0))
            xc = _lead(mats['hc_f'], [cin], [cspec_in], (pairs,),
                       jax.ShapeDtypeStruct((pairs, 2 * nc, c), F32),
                       pl.BlockSpec((1, 2 * nc, c), lambda b: (b, 0, 0)), name="hyena_ctx_fwd")
            ynew = _lead(mats['hc_i'], [xc], [pl.BlockSpec((1, 2 * nc, c), lambda b: (b, 0, 0))],
                         (pairs,), jax.ShapeDtypeStruct((out_rows, c), F32),
                         pl.BlockSpec((nc, c), lambda b, oblk0=oblk0: (oblk0 + b, 0)),
                         kspec=(kspec_c, pl.BlockSpec((1, 2 * nc, c), lambda b, o=o: (o, 0, 0))),
                         epi=[(z3, pl.BlockSpec((1, nc, c), lambda b, o=o: (o + 1, blk0 + b, 0))),
                              (cin, cspec_in),
                              (bias3, pl.BlockSpec((1, 1, c), lambda b, o=o: (o, 0, 0)))],
                         alias_to=ynew, name="hyena_ctx_inv")
        y = ynew
    return y


def _route_tile(lt, br, base, tri):
    tm = lt.shape[1]
    aff = jax.nn.sigmoid(lt)
    biased = aff + br
    b = [biased[e:e + 1, :] for e in range(N_EXPERTS)]
    a = [aff[e:e + 1, :] for e in range(N_EXPERTS)]
    epg = EXPERTS_PER_GROUP
    scores = []
    for g in range(N_GROUPS):
        x0, x1, x2, x3 = b[epg * g:epg * g + epg]
        s1, t1 = jnp.maximum(x0, x1), jnp.minimum(x0, x1)
        s2, t2 = jnp.maximum(x2, x3), jnp.minimum(x2, x3)
        scores.append(jnp.maximum(s1, s2) + jnp.maximum(jnp.minimum(s1, s2), jnp.maximum(t1, t2)))
    best = scores[0]
    gsel = jnp.zeros((1, tm), jnp.int32)
    for g in range(1, N_GROUPS):
        gsel = jnp.where(scores[g] > best, g, gsel)
        best = jnp.maximum(best, scores[g])

    def pick(rows, j):
        out = rows[j]
        for g in range(1, N_GROUPS):
            out = jnp.where(gsel == g, rows[epg * g + j], out)
        return out

    v = [pick(b, j) for j in range(epg)]
    av = [pick(a, j) for j in range(epg)]
    i1 = jnp.zeros((1, tm), jnp.int32)
    m1 = v[0]
    for j in range(1, epg):
        i1 = jnp.where(v[j] > m1, j, i1)
        m1 = jnp.maximum(m1, v[j])
    neg = jnp.float32(-3.0e38)
    i2 = jnp.zeros((1, tm), jnp.int32)
    m2 = jnp.full((1, tm), neg, F32)
    for j in range(epg):
        cand = jnp.where(i1 == j, neg, v[j])
        take = cand > m2
        i2 = jnp.where(take, j, i2)
        m2 = jnp.where(take, cand, m2)

    def sel(rows, idx):
        out = rows[0]
        for j in range(1, epg):
            out = jnp.where(idx == j, rows[j], out)
        return out

    a1, a2 = sel(av, i1), sel(av, i2)
    den = a1 + a2
    e1 = gsel * epg + i1
    e2 = gsel * epg + i2
    eio = lax.broadcasted_iota(jnp.int32, (N_EXPERTS, tm), 0)
    oh1 = jnp.where(eio == e1, 1.0, 0.0)
    oh2 = jnp.where(eio == e2, 1.0, 0.0)
    oh = oh1 + oh2
    tot = base + _dot(oh.astype(BF16), tri)
    r1 = jnp.sum(oh1 * tot, axis=0, keepdims=True)
    r2 = jnp.sum(oh2 * tot, axis=0, keepdims=True)
    new_base = base + jnp.sum(oh, axis=1, keepdims=True)
    return (e1, e2), (a1 / den, a2 / den), (r1.astype(jnp.int32), r2.astype(jnp.int32)), new_base


def _merge_kernel(x_ref, ya_ref, yf_ref, yh_ref, gt_ref, wa_ref, wf_ref, wh_ref, wo_ref, g1_ref,
                  gn_ref, sh_ref, sc_ref, wrt_ref, br_ref, xo_ref, h2_ref, e_ref, w_ref, r_ref, cnt_ref):
    d = D_MODEL
    gate = lambda k: jax.nn.sigmoid(gt_ref[:, k * d:(k + 1) * d].astype(F32))
    merged = gate(0) * _dot(ya_ref[...], wa_ref[...])
    merged += gate(1) * _dot(yf_ref[...].astype(BF16), wf_ref[...])
    merged += gate(2) * _dot(yh_ref[...].astype(BF16), wh_ref[...])
    xn = x_ref[...] + g1_ref[0, 0] * _dot(merged.astype(BF16), wo_ref[...])
    xo_ref[...] = xn
    h2f = _rms_mod(xn, gn_ref[...], sh_ref[0, 0], sc_ref[0, 0])
    _rows_to_tiles(h2_ref, h2f, 0, h2f.shape[0])
    h2 = h2f.astype(BF16)

    @pl.when(pl.program_id(0) == 0)
    def _():
        cnt_ref[...] = jnp.zeros_like(cnt_ref)

    tm = h2.shape[0]
    lt = lax.dot_general(wrt_ref[...], h2, (((1,), (1,)), ((), ())), preferred_element_type=F32)
    tri = jnp.where(lax.broadcasted_iota(jnp.int32, (tm, tm), 0) < lax.broadcasted_iota(jnp.int32, (tm, tm), 1),
                    1.0, 0.0).astype(BF16)
    es, ws, rs, new_base = _route_tile(lt, br_ref[...], cnt_ref[:, 0:1], tri)
    e_ref[0:1, :], e_ref[1:2, :] = es
    w_ref[0:1, :], w_ref[1:2, :] = ws
    r_ref[0:1, :], r_ref[1:2, :] = rs
    cnt_ref[...] = jnp.broadcast_to(new_base, cnt_ref.shape)


def _merge(x, ya, yf, yh, p, wa, wf, wh, wo, mod4, gain2, wrt, br, n_tok):
    tm = TM
    row = _mod_row(tm)
    full = lambda a: pl.BlockSpec(a.shape, lambda i: (0,) * a.ndim)
    modspec = lambda k: pl.BlockSpec((1, 1, 1, D_MODEL), lambda i: (row(i), k, 0, 0))
    tok = lambda w: pl.BlockSpec((tm, w), lambda i: (i, 0))
    lane = pl.BlockSpec((TOP_K, tm), lambda i: (0, i))
    return pl.pallas_call(
        _merge_kernel,
        grid=(n_tok // tm,),
        in_specs=[tok(D_MODEL), tok(ATTN_WIDTH), tok(FNET_WIDTH), tok(HYENA_WIDTH),
                  pl.BlockSpec((pl.Element(tm), pl.Element(3 * D_MODEL)), lambda i: (i * tm, PG_OFF)),
                  full(wa), full(wf), full(wh), full(wo), modspec(2), full(gain2), modspec(3), modspec(4),
                  full(wrt), full(br)],
        out_specs=[tok(D_MODEL), pl.BlockSpec((tm * ROW_TILE, 128), lambda i: (i, 0)), lane, lane, lane,
                   pl.BlockSpec((N_EXPERTS, 128), lambda i: (0, 0))],
        out_shape=[jax.ShapeDtypeStruct((n_tok, D_MODEL), F32),
                   jax.ShapeDtypeStruct((n_tok * ROW_TILE, 128), F32),
                   jax.ShapeDtypeStruct((TOP_K, n_tok), jnp.int32),
                   jax.ShapeDtypeStruct((TOP_K, n_tok), F32),
                   jax.ShapeDtypeStruct((TOP_K, n_tok), jnp.int32),
                   jax.ShapeDtypeStruct((N_EXPERTS, 128), F32)],
        compiler_params=_cparams(("arbitrary",)),
        name="merge_out_norm_route",
    )(x, ya, yf, yh, p, wa, wf, wh, wo, mod4, gain2, mod4, mod4, wrt, br)


ROW_TILE = D_MODEL // 128


def _rows_from_tiles(ref, n):
    return jnp.concatenate([ref[pl.ds(s, n, stride=ROW_TILE), :] for s in range(ROW_TILE)], axis=1)


def _rows_to_tiles(ref, val, col0, n):
    for j in range(val.shape[1] // 128):
        ref[pl.ds(col0 // 128 + j, n, stride=ROW_TILE), :] = val[:, j * 128:(j + 1) * 128]


def _moe_kernel(be_ref, na_ref, ip_ref, ic_ref, sc_ref, sn_ref, h2_ref, wg_ref, wu_ref, wd_ref, y_ref,
                xbuf, obuf, gsem, ssem, wg_s, wu_s, wd_s):
    i = pl.program_id(0)
    n_act = na_ref[0]
    sub = ROW_TILE
    bm = xbuf.shape[1] // sub
    active = i < n_act
    slot = i % 2
    other = (i + 1) % 2
    n_chunk = 4
    cw = EXPERT_FF // n_chunk
    rows_per = bm // n_chunk

    def tile_rows(ref, row):
        return ref.at[pl.ds(pl.multiple_of(row * sub, sub), sub), :]

    def gather(src_ref, dst_slot, lo, hi):
        for r in range(lo, hi):
            pltpu.make_async_copy(tile_rows(h2_ref, src_ref[0, 0, r]), xbuf.at[dst_slot, pl.ds(r * sub, sub), :],
                                  gsem.at[dst_slot]).start(priority=r % 2)

    def scatter(info_ref, src_slot, lo, hi):
        for r in range(lo, hi):
            pltpu.make_async_copy(obuf.at[src_slot, pl.ds(r * sub, sub), :], tile_rows(y_ref, info_ref[0, 0, r]),
                                  ssem.at[src_slot]).start(priority=r % 2)

    def wait_gather(s):
        pltpu.make_async_copy(h2_ref.at[pl.ds(0, bm * sub), :], xbuf.at[s], gsem.at[s]).wait()

    def wait_scatter(s):
        pltpu.make_async_copy(obuf.at[s], y_ref.at[pl.ds(0, bm * sub), :], ssem.at[s]).wait()

    @pl.when((i == 0) & active)
    def _():
        obuf[...] = jnp.zeros_like(obuf)
        gather(sc_ref, 0, 0, bm)

    prev = be_ref[jnp.maximum(i - 1, 0)]

    @pl.when(active & ((i == 0) | (be_ref[i] != prev)))
    def _():
        wg_s[...] = wg_ref[...].astype(BF16)
        wu_s[...] = wu_ref[...].astype(BF16)
        wd_s[...] = wd_ref[...].astype(BF16)

    @pl.when(active)
    def _():
        wait_gather(slot)
        x = _rows_from_tiles(xbuf.at[slot], bm).astype(BF16)
        hs = []
        for c in range(n_chunk):
            g = _dot(x, wg_s[:, c * cw:(c + 1) * cw])
            u = _dot(x, wu_s[:, c * cw:(c + 1) * cw])
            hs.append(((g * jax.nn.sigmoid(g)) * u).astype(BF16))
            gather(sn_ref, other, c * rows_per, (c + 1) * rows_per)
        h = jnp.concatenate(hs, axis=1)

        @pl.when(i > 0)
        def _():
            wait_scatter(slot)

        for c in range(n_chunk):
            _rows_to_tiles(obuf.at[slot], _dot(h, wd_s[:, c * cw:(c + 1) * cw]), c * cw, bm)
            scatter(ip_ref, other, c * rows_per, (c + 1) * rows_per)

    @pl.when(i == n_act - 1)
    def _():
        wait_gather(other)
        wait_scatter(other)
        scatter(ic_ref, slot, 0, bm)
        wait_scatter(slot)


def _moe(h2, info, blk_expert, n_active, wg, wu, wd, layer, n_tok):
    bm = EXPERT_BM
    n_rows = info.shape[0]
    n_blk = n_rows // bm
    info3 = info.reshape(n_blk, 1, bm)
    src3 = jnp.where(info3 >= TOP_K * n_tok, 0, info3 % n_tok)
    wspec = lambda k, n: pl.BlockSpec((None, None, k, n), lambda i, be, na: (layer, be[i], 0, 0))
    ispec = lambda f: pl.BlockSpec((1, 1, bm), lambda i, be, na: (f(i), 0, 0), memory_space=pltpu.SMEM)
    return pl.pallas_call(
        _moe_kernel,
        grid_spec=pltpu.PrefetchScalarGridSpec(
            num_scalar_prefetch=2,
            grid=(n_blk,),
            in_specs=[ispec(lambda i: jnp.maximum(i - 1, 0)), ispec(lambda i: i),
                      ispec(lambda i: i), ispec(lambda i: jnp.minimum(i + 1, n_blk - 1)),
                      pl.BlockSpec(memory_space=pl.ANY),
                      wspec(D_MODEL, EXPERT_FF), wspec(D_MODEL, EXPERT_FF), wspec(EXPERT_FF, D_MODEL)],
            out_specs=pl.BlockSpec(memory_space=pl.ANY),
            scratch_shapes=[pltpu.VMEM((2, bm * ROW_TILE, 128), F32), pltpu.VMEM((2, bm * ROW_TILE, 128), F32),
                            pltpu.SemaphoreType.DMA((2,)), pltpu.SemaphoreType.DMA((2,)),
                            pltpu.VMEM((D_MODEL, EXPERT_FF), BF16), pltpu.VMEM((D_MODEL, EXPERT_FF), BF16),
                            pltpu.VMEM((EXPERT_FF, D_MODEL), BF16)]),
        out_shape=jax.ShapeDtypeStruct(((TOP_K * n_tok + n_rows) * ROW_TILE, 128), F32),
        compiler_params=_cparams(("arbitrary",)),
        name="moe_experts",
    )(blk_expert, n_active, info3, info3, src3, src3, h2, wg, wu, wd)


def _dispatch_info(e_idx, rank, counts, n_tok):
    bm = EXPERT_BM
    counts = counts.astype(jnp.int32)
    padded = (counts + bm - 1) // bm * bm
    pad_end = jnp.cumsum(padded)
    pad_start = pad_end - padded
    experts = jnp.arange(N_EXPERTS, dtype=jnp.int32)
    start = jnp.sum(jnp.where(e_idx[..., None] == experts, pad_start, 0), axis=-1)
    dest = start + rank
    n_rows = -(-(n_tok * TOP_K) // bm) * bm + N_EXPERTS * bm
    n_blk = n_rows // bm
    spill = TOP_K * n_tok + jnp.arange(n_rows, dtype=jnp.int32)
    info = spill.at[dest.reshape(-1)].set(jnp.arange(TOP_K * n_tok, dtype=jnp.int32))
    blk_start = jnp.arange(n_blk, dtype=jnp.int32) * bm
    blk_expert = jnp.minimum(jnp.sum((blk_start[:, None] >= pad_end[None, :]).astype(jnp.int32), axis=-1),
                             N_EXPERTS - 1)
    n_active = (pad_end[-1] // bm).astype(jnp.int32).reshape(1)
    return info, blk_expert, n_active


def _residual_kernel(x_ref, y0_ref, y1_ref, w_ref, g2_ref, gf_ref, o_ref, *, final):
    w = w_ref[...]
    tm = x_ref.shape[0]
    moe = _rows_from_tiles(y0_ref, tm) * w[:, 0:1] + _rows_from_tiles(y1_ref, tm) * w[:, 1:2]
    xn = x_ref[...] + g2_ref[0, 0] * moe
    if final:
        y = xn * lax.rsqrt(jnp.mean(xn * xn, axis=-1, keepdims=True) + EPS)
        xn = y * gf_ref[...]
    o_ref[...] = xn


def _residual(x, y, w_sel, mod4, gain_final, n_tok, final):
    tm = TM
    n_tiles = n_tok // tm
    row = _mod_row(tm)
    tok = lambda w: pl.BlockSpec((tm, w), lambda i: (i, 0))
    return pl.pallas_call(
        functools.partial(_residual_kernel, final=final),
        grid=(n_tiles,),
        in_specs=[tok(D_MODEL), pl.BlockSpec((tm * ROW_TILE, 128), lambda i: (i, 0)),
                  pl.BlockSpec((tm * ROW_TILE, 128), lambda i: (n_tiles + i, 0)), tok(TOP_K),
                  pl.BlockSpec((1, 1, 1, D_MODEL), lambda i: (row(i), 5, 0, 0)),
                  pl.BlockSpec((1, D_MODEL), lambda i: (0, 0))],
        out_specs=tok(D_MODEL),
        out_shape=jax.ShapeDtypeStruct((n_tok if final else T_ALL, D_MODEL), F32),
        compiler_params=_cparams(("parallel",)),
        name="moe_residual",
    )(x, y, y, w_sel, mod4, gain_final)


def kernel(x, c, ctx, c_ctx, w_mod, b_mod, norm_mix, norm_ffn, w_in, attn_sink, conv_w, conv_b, filt_w1, filt_b1, filt_freq, filt_w2, filt_b2, filt_w3, hyena_bias, w_branch_attn, w_branch_fnet, w_branch_hyena, w_out, w_router, b_router, w_exp_gate, w_exp_up, w_exp_down, norm_final):
    mats = _dft_mats()
    cos_t, sin_t = _rope_tables()
    c8 = jnp.concatenate([c, c_ctx[None, :], jnp.zeros((8 - BATCH - 1, D_MODEL), F32)], axis=0)
    mod_all = _modulation(c8, w_mod, b_mod)
    xa = jnp.concatenate([x.reshape(T_LAT, D_MODEL), ctx.reshape(T_CTX, D_MODEL)], axis=0)
    wrt = w_router.T.astype(BF16)
    br = b_router.astype(F32).reshape(N_EXPERTS, 1)
    gain_final = norm_final.reshape(1, D_MODEL)
    out = None
    for l in range(DEPTH):
        last = l == DEPTH - 1
        with_ctx = not last
        n_tok = T_LAT if last else T_ALL
        mod4 = mod_all[l].reshape(8, N_MOD, 1, D_MODEL)
        p = _norm_proj(xa, norm_mix[l].reshape(1, D_MODEL), mod4, w_in, l, T_ALL)
        ya = _attention(p, attn_sink[l], cos_t, sin_t, with_ctx)
        yf = _fourier_mix(p, mats, with_ctx)
        z3 = _short_conv(p, conv_w[l], conv_b[l], with_ctx)
        filt = (filt_w1[l], filt_b1[l], filt_freq[l], filt_w2[l], filt_b2[l], filt_w3[l])
        yh = _hyena_mix(z3, filt, hyena_bias[l], mats, with_ctx)
        xa, h2, e_idx, w_sel, rank, cnt = _merge(
            xa, ya, yf, yh, p, w_branch_attn[l].astype(BF16), w_branch_fnet[l].astype(BF16),
            w_branch_hyena[l].astype(BF16), w_out[l].astype(BF16), mod4,
            norm_ffn[l].reshape(1, D_MODEL), wrt, br, n_tok)
        info, blk_expert, n_active = _dispatch_info(e_idx, rank, cnt[:, 0], n_tok)
        y = _moe(h2, info, blk_expert, n_active, w_exp_gate, w_exp_up, w_exp_down, l, n_tok)
        res = _residual(xa, y, w_sel.T, mod4, gain_final, n_tok, last)
        if last:
            out = res
        else:
            xa = res
    return out.reshape(BATCH, SEQ, D_MODEL)
```
